```python
import jax, jax.numpy as jnp
from jax import lax
import numpy as np

D_MODEL = 2048
BATCH = 8
SEQ = 8192
DEPTH = 1

N_META = 16
MIX = D_MODEL
RET_WIDTH = MIX // 2
RET_HEADS = 4
RET_HEAD_DIM = RET_WIDTH // RET_HEADS
CONV_WIDTH = MIX - RET_WIDTH
CONV_GROUPS = 4
CONV_K = 31
CHUNK = 128
ROPE_BASE = 10000.0
EPS = 1e-6
IN_SPLITS = (RET_WIDTH, RET_WIDTH, RET_WIDTH, RET_WIDTH, CONV_WIDTH, CONV_WIDTH, CONV_WIDTH)
IN_WIDTH = sum(IN_SPLITS)

kernel_name = "hybrid_retention_conformer_block"


def rms_norm(x, g):
    xf = x.astype(jnp.float32)
    y = xf * lax.rsqrt(jnp.mean(xf * xf, axis=-1, keepdims=True) + EPS)
    return (y * g.astype(jnp.float32)).astype(x.dtype)


def rotary(x, pos):
    half = x.shape[-1] // 2
    inv_freq = ROPE_BASE ** (-jnp.arange(half, dtype=jnp.float32) / half)
    ang = pos[:, None] * inv_freq[None, :]
    cos = jnp.cos(ang)[None, :, None, :]
    sin = jnp.sin(ang)[None, :, None, :]
    x1, x2 = x[..., :half], x[..., half:]
    return jnp.concatenate([x1 * cos - x2 * sin, x1 * sin + x2 * cos], axis=-1)


def chunkwise_retention(q, k, v):
    b, l, h, d = q.shape
    n = l // CHUNK
    to_chunks = lambda t: t.reshape(b, n, CHUNK, h, t.shape[-1]).transpose(1, 0, 3, 2, 4)
    qc, kc, vc = to_chunks(q), to_chunks(k), to_chunks(v)
    gamma = 1.0 - jnp.exp2(-5.0 - jnp.arange(h, dtype=jnp.float32))
    log_g = jnp.log(gamma)
    idx = jnp.arange(CHUNK, dtype=jnp.float32)
    rel = idx[:, None] - idx[None, :]
    decay_mask = jnp.where(rel[None] >= 0,
                           jnp.exp(jnp.maximum(rel, 0.0)[None] * log_g[:, None, None]),
                           0.0)
    q_decay = jnp.exp((idx[None, :] + 1.0) * log_g[:, None])
    k_decay = jnp.exp((CHUNK - 1.0 - idx[None, :]) * log_g[:, None])
    chunk_decay = jnp.exp(CHUNK * log_g)

    def step(state, inp):
        qi, ki, vi = inp
        scores = jnp.einsum('bhqd,bhkd->bhqk', qi, ki) * decay_mask[None]
        inner = jnp.einsum('bhqk,bhkv->bhqv', scores, vi)
        cross = jnp.einsum('bhqd,bhdv->bhqv', qi * q_decay[None, :, :, None], state)
        new_state = state * chunk_decay[None, :, None, None] + jnp.einsum(
            'bhkd,bhkv->bhdv', ki * k_decay[None, :, :, None], vi)
        return new_state, inner + cross

    state0 = jnp.zeros((b, h, d, v.shape[-1]), jnp.float32)
    _, out = lax.scan(step, state0, (qc, kc, vc))
    return out.transpose(1, 0, 3, 2, 4).reshape(b, l, h, v.shape[-1])


def retention_group(q_in, k_in, v_in, g_in, gn_g):
    b, l, _ = q_in.shape
    pos = jnp.arange(l, dtype=jnp.float32)
    shp = (b, l, RET_HEADS, RET_HEAD_DIM)
    q = rotary(q_in.astype(jnp.float32).reshape(shp), pos)
    k = rotary(k_in.astype(jnp.float32).reshape(shp), pos) * (RET_HEAD_DIM ** -0.5)
    v = v_in.astype(jnp.float32).reshape(shp)
    lead = (-N_META) % CHUNK
    padw = ((0, 0), (lead, 0), (0, 0), (0, 0))
    y = chunkwise_retention(jnp.pad(q, padw), jnp.pad(k, padw), jnp.pad(v, padw))[:, lead:]
    mu = jnp.mean(y, axis=-1, keepdims=True)
    var = jnp.mean(jnp.square(y - mu), axis=-1, keepdims=True)
    y = ((y - mu) * lax.rsqrt(var + EPS)).reshape(b, l, RET_WIDTH) * gn_g.astype(jnp.float32)
    return (y * jax.nn.silu(g_in.astype(jnp.float32))).astype(q_in.dtype)


def conformer_conv_group(a_in, b_in, g_in, dw_w, dw_b, ln_g, ln_b, pw_w, pw_b):
    u = a_in * jax.nn.sigmoid(b_in)
    u = lax.conv_general_dilated(u, dw_w[:, None, :].astype(u.dtype), window_strides=(1,),
                                 padding=[(CONV_K - 1, 0)],
                                 dimension_numbers=('NWC', 'WIO', 'NWC'),
                                 feature_group_count=CONV_WIDTH) + dw_b
    uf = u.astype(jnp.float32)
    mu = jnp.mean(uf, axis=-1, keepdims=True)
    var = jnp.mean(jnp.square(uf - mu), axis=-1, keepdims=True)
    uf = (uf - mu) * lax.rsqrt(var + EPS) * ln_g.astype(jnp.float32) + ln_b.astype(jnp.float32)
    u = jax.nn.silu(uf).astype(a_in.dtype)
    u = jnp.einsum('blc,cd->bld', u, pw_w) + pw_b
    return u * jax.nn.silu(g_in)


def _fwd_setup_inputs(seed: int = 0) -> dict:
    key = jax.random.key(seed)
    ks = jax.random.split(key, 14)
    f32 = jnp.float32
    nrm = lambda k, s, sc: jax.random.normal(k, s, f32) * sc
    return {
        "x": nrm(ks[0], (BATCH, SEQ, D_MODEL), 1.0),
        "meta_tokens": nrm(ks[1], (N_META, D_MODEL), 1.0),
        "ln_g": 1.0 + nrm(ks[2], (DEPTH, D_MODEL), 0.02),
        "w_in": nrm(ks[3], (DEPTH, D_MODEL, IN_WIDTH), D_MODEL ** -0.5),
        "ret_gn_g": 1.0 + nrm(ks[4], (DEPTH, RET_WIDTH), 0.02),
        "conv_dw_w": nrm(ks[5], (DEPTH, CONV_K, CONV_WIDTH), CONV_K ** -0.5),
        "conv_dw_b": nrm(ks[6], (DEPTH, CONV_WIDTH), 0.01),
        "conv_ln_g": 1.0 + nrm(ks[7], (DEPTH, CONV_WIDTH), 0.02),
        "conv_ln_b": nrm(ks[8], (DEPTH, CONV_WIDTH), 0.01),
        "conv_pw_w": nrm(ks[9], (DEPTH, CONV_WIDTH, CONV_WIDTH), CONV_WIDTH ** -0.5),
        "conv_pw_b": nrm(ks[10], (DEPTH, CONV_WIDTH), 0.01),
        "w_out": nrm(ks[11], (DEPTH, MIX, D_MODEL), MIX ** -0.5),
        "final_g": 1.0 + nrm(ks[12], (D_MODEL,), 0.02),
    }


def _fwd_reference(x, meta_tokens, ln_g, w_in, ret_gn_g, conv_dw_w, conv_dw_b, conv_ln_g,
              conv_ln_b, conv_pw_w, conv_pw_b, w_out, final_g):
    b = x.shape[0]
    meta = jnp.broadcast_to(meta_tokens[None].astype(x.dtype), (b, N_META, D_MODEL))
    h = jnp.concatenate([meta, x], axis=1)
    offs = np.cumsum(IN_SPLITS)[:-1].tolist()
    for l in range(DEPTH):
        hn = rms_norm(h, ln_g[l])
        proj = jnp.einsum('bld,de->ble', hn, w_in[l])
        q_in, k_in, v_in, gr_in, a_in, b_in, gc_in = jnp.split(proj, offs, axis=-1)
        y_ret = retention_group(q_in, k_in, v_in, gr_in, ret_gn_g[l])
        y_conv = conformer_conv_group(a_in, b_in, gc_in, conv_dw_w[l], conv_dw_b[l],
                                      conv_ln_g[l], conv_ln_b[l], conv_pw_w[l], conv_pw_b[l])
        y = jnp.concatenate([y_ret, y_conv], axis=-1)
        h = h + jnp.einsum('ble,ed->bld', y, w_out[l])
    h = rms_norm(h, final_g)
    return h[:, N_META:]


import jax as _jax
import jax.numpy as _jnp

TWIN_FORMAT = 'train_step'
FWD_PARAMS = ['x', 'meta_tokens', 'ln_g', 'w_in', 'ret_gn_g', 'conv_dw_w', 'conv_dw_b', 'conv_ln_g', 'conv_ln_b', 'conv_pw_w', 'conv_pw_b', 'w_out', 'final_g']
TWIN_WEIGHTS = ['meta_tokens', 'ln_g', 'w_in', 'ret_gn_g', 'conv_dw_w', 'conv_dw_b', 'conv_ln_g', 'conv_ln_b', 'conv_pw_w', 'conv_pw_b', 'w_out', 'final_g']
TWIN_DIFF_INPUT = 'x'
TWIN_INPUTS = ['x', 'meta_tokens', 'ln_g', 'w_in', 'ret_gn_g', 'conv_dw_w', 'conv_dw_b', 'conv_ln_g', 'conv_ln_b', 'conv_pw_w', 'conv_pw_b', 'w_out', 'final_g', 'loss_target', 'm_meta_tokens', 'm_ln_g', 'm_w_in', 'm_ret_gn_g', 'm_conv_dw_w', 'm_conv_dw_b', 'm_conv_ln_g', 'm_conv_ln_b', 'm_conv_pw_w', 'm_conv_pw_b', 'm_w_out', 'm_final_g', 'v_meta_tokens', 'v_ln_g', 'v_w_in', 'v_ret_gn_g', 'v_conv_dw_w', 'v_conv_dw_b', 'v_conv_ln_g', 'v_conv_ln_b', 'v_conv_pw_w', 'v_conv_pw_b', 'v_w_out', 'v_final_g']
TWIN_OUTPUTS = ['loss', 'grad_x', 'grad_meta_tokens', 'grad_ln_g', 'grad_w_in', 'grad_ret_gn_g', 'grad_conv_dw_w', 'grad_conv_dw_b', 'grad_conv_ln_g', 'grad_conv_ln_b', 'grad_conv_pw_w', 'grad_conv_pw_b', 'grad_w_out', 'grad_final_g', 'delta_meta_tokens', 'delta_ln_g', 'delta_w_in', 'delta_ret_gn_g', 'delta_conv_dw_w', 'delta_conv_dw_b', 'delta_conv_ln_g', 'delta_conv_ln_b', 'delta_conv_pw_w', 'delta_conv_pw_b', 'delta_w_out', 'delta_final_g', 'new_m_meta_tokens', 'new_m_ln_g', 'new_m_w_in', 'new_m_ret_gn_g', 'new_m_conv_dw_w', 'new_m_conv_dw_b', 'new_m_conv_ln_g', 'new_m_conv_ln_b', 'new_m_conv_pw_w', 'new_m_conv_pw_b', 'new_m_w_out', 'new_m_final_g', 'new_v_meta_tokens', 'new_v_ln_g', 'new_v_w_in', 'new_v_ret_gn_g', 'new_v_conv_dw_w', 'new_v_conv_dw_b', 'new_v_conv_ln_g', 'new_v_conv_ln_b', 'new_v_conv_pw_w', 'new_v_conv_pw_b', 'new_v_w_out', 'new_v_final_g']
TWIN_LEAF_KINDS = {'loss': 'loss', 'grad_x': 'grad_x', 'grad_meta_tokens': 'grad_w', 'grad_ln_g': 'grad_w', 'grad_w_in': 'grad_w', 'grad_ret_gn_g': 'grad_w', 'grad_conv_dw_w': 'grad_w', 'grad_conv_dw_b': 'grad_w', 'grad_conv_ln_g': 'grad_w', 'grad_conv_ln_b': 'grad_w', 'grad_conv_pw_w': 'grad_w', 'grad_conv_pw_b': 'grad_w', 'grad_w_out': 'grad_w', 'grad_final_g': 'grad_w', 'delta_meta_tokens': 'delta_w', 'delta_ln_g': 'delta_w', 'delta_w_in': 'delta_w', 'delta_ret_gn_g': 'delta_w', 'delta_conv_dw_w': 'delta_w', 'delta_conv_dw_b': 'delta_w', 'delta_conv_ln_g': 'delta_w', 'delta_conv_ln_b': 'delta_w', 'delta_conv_pw_w': 'delta_w', 'delta_conv_pw_b': 'delta_w', 'delta_w_out': 'delta_w', 'delta_final_g': 'delta_w', 'new_m_meta_tokens': 'new_m', 'new_m_ln_g': 'new_m', 'new_m_w_in': 'new_m', 'new_m_ret_gn_g': 'new_m', 'new_m_conv_dw_w': 'new_m', 'new_m_conv_dw_b': 'new_m', 'new_m_conv_ln_g': 'new_m', 'new_m_conv_ln_b': 'new_m', 'new_m_conv_pw_w': 'new_m', 'new_m_conv_pw_b': 'new_m', 'new_m_w_out': 'new_m', 'new_m_final_g': 'new_m', 'new_v_meta_tokens': 'new_v', 'new_v_ln_g': 'new_v', 'new_v_w_in': 'new_v', 'new_v_ret_gn_g': 'new_v', 'new_v_conv_dw_w': 'new_v', 'new_v_conv_dw_b': 'new_v', 'new_v_conv_ln_g': 'new_v', 'new_v_conv_ln_b': 'new_v', 'new_v_conv_pw_w': 'new_v', 'new_v_conv_pw_b': 'new_v', 'new_v_w_out': 'new_v', 'new_v_final_g': 'new_v'}


def _forward(args):
    return _fwd_reference(*[args[k] for k in FWD_PARAMS])


def _output_shape():
    def fwd():
        inp = _fwd_setup_inputs(0)
        return _fwd_reference(*[inp[k] for k in FWD_PARAMS])
    out = _jax.eval_shape(fwd)
    return out.shape, out.dtype

N_MICROBATCH = 1
ADAM_LR = 0.001
ADAM_B1 = 0.9
ADAM_B2 = 0.999
ADAM_EPS = 1e-08
ADAM_WD = 0.01
ADAM_STEP = 10
PER_EXAMPLE_BATCH_AXIS = {'x': 0, 'loss_target': 0}
SHARED_INPUTS = []
_WEIGHT_DTYPES = {'meta_tokens': _jnp.float32, 'ln_g': _jnp.float32, 'w_in': _jnp.float32, 'ret_gn_g': _jnp.float32, 'conv_dw_w': _jnp.float32, 'conv_dw_b': _jnp.float32, 'conv_ln_g': _jnp.float32, 'conv_ln_b': _jnp.float32, 'conv_pw_w': _jnp.float32, 'conv_pw_b': _jnp.float32, 'w_out': _jnp.float32, 'final_g': _jnp.float32}
MOMENT_SCALE = {'meta_tokens': 5.393394e-03, 'ln_g': 1.077247e-01, 'w_in': 5.636724e-02, 'ret_gn_g': 6.588978e-02, 'conv_dw_w': 4.170986e-02, 'conv_dw_b': 7.823477e-02, 'conv_ln_g': 4.853903e-02, 'conv_ln_b': 4.186257e-02, 'conv_pw_w': 4.024315e-02, 'conv_pw_b': 6.806505e-02, 'w_out': 5.547650e-02, 'final_g': 3.196588e+01}


def _to_microbatches(a, axis):
    t = _jnp.moveaxis(a, axis, 0)
    t = t.reshape((N_MICROBATCH, t.shape[0] // N_MICROBATCH) + t.shape[1:])
    return _jnp.moveaxis(t, 1, axis + 1)


def setup_inputs(seed: int = 0) -> dict:
    inp = _fwd_setup_inputs(seed)
    key = _jax.random.fold_in(_jax.random.key(seed), 7919)
    shape, _ = _output_shape()
    out = dict(inp)
    out["loss_target"] = _jax.random.normal(_jax.random.fold_in(key, 0), shape, _jnp.float32)
    for i, name in enumerate(TWIN_WEIGHTS):
        w = inp[name].astype(_jnp.float32)
        if MOMENT_SCALE is None:
            s = _jnp.sqrt(_jnp.mean(_jnp.square(w)) + 1e-30)
        else:
            s = MOMENT_SCALE[name]
        km, kv = _jax.random.split(_jax.random.fold_in(key, i + 1))
        out[name] = w
        out["m_" + name] = s * _jax.random.normal(km, w.shape, _jnp.float32)
        out["v_" + name] = (s * s) * _jax.random.uniform(kv, w.shape, _jnp.float32, 0.5, 1.5)
    if N_MICROBATCH > 1:
        for name, axis in PER_EXAMPLE_BATCH_AXIS.items():
            out[name] = _to_microbatches(out[name], axis)
    return {'x': out['x'], 'meta_tokens': out['meta_tokens'], 'ln_g': out['ln_g'], 'w_in': out['w_in'], 'ret_gn_g': out['ret_gn_g'], 'conv_dw_w': out['conv_dw_w'], 'conv_dw_b': out['conv_dw_b'], 'conv_ln_g': out['conv_ln_g'], 'conv_ln_b': out['conv_ln_b'], 'conv_pw_w': out['conv_pw_w'], 'conv_pw_b': out['conv_pw_b'], 'w_out': out['w_out'], 'final_g': out['final_g'], 'loss_target': out['loss_target'], 'm_meta_tokens': out['m_meta_tokens'], 'm_ln_g': out['m_ln_g'], 'm_w_in': out['m_w_in'], 'm_ret_gn_g': out['m_ret_gn_g'], 'm_conv_dw_w': out['m_conv_dw_w'], 'm_conv_dw_b': out['m_conv_dw_b'], 'm_conv_ln_g': out['m_conv_ln_g'], 'm_conv_ln_b': out['m_conv_ln_b'], 'm_conv_pw_w': out['m_conv_pw_w'], 'm_conv_pw_b': out['m_conv_pw_b'], 'm_w_out': out['m_w_out'], 'm_final_g': out['m_final_g'], 'v_meta_tokens': out['v_meta_tokens'], 'v_ln_g': out['v_ln_g'], 'v_w_in': out['v_w_in'], 'v_ret_gn_g': out['v_ret_gn_g'], 'v_conv_dw_w': out['v_conv_dw_w'], 'v_conv_dw_b': out['v_conv_dw_b'], 'v_conv_ln_g': out['v_conv_ln_g'], 'v_conv_ln_b': out['v_conv_ln_b'], 'v_conv_pw_w': out['v_conv_pw_w'], 'v_conv_pw_b': out['v_conv_pw_b'], 'v_w_out': out['v_w_out'], 'v_final_g': out['v_final_g']}


def _loss(weights, diff, rest, loss_target):
    with _jax.named_scope("forward"):
        args = {**rest, TWIN_DIFF_INPUT: diff, **{k: w.astype(_WEIGHT_DTYPES[k]) for k, w in weights.items()}}
        y = _forward(args)
    with _jax.named_scope("loss_head"):
        err = _jnp.square(y.astype(_jnp.float32) - loss_target)
        return 0.5 * _jnp.sum(_jnp.mean(err, axis=-1)) if err.ndim else 0.5 * err


def _adamw(w, g, m, v):
    m = ADAM_B1 * m + (1.0 - ADAM_B1) * g
    v = ADAM_B2 * v + (1.0 - ADAM_B2) * _jnp.square(g)
    m_hat = m / (1.0 - ADAM_B1 ** ADAM_STEP)
    v_hat = v / (1.0 - ADAM_B2 ** ADAM_STEP)
    delta = -ADAM_LR * (m_hat / (_jnp.sqrt(v_hat) + ADAM_EPS) + ADAM_WD * w)
    return delta, m, v


def reference(x, meta_tokens, ln_g, w_in, ret_gn_g, conv_dw_w, conv_dw_b, conv_ln_g, conv_ln_b, conv_pw_w, conv_pw_b, w_out, final_g, loss_target, m_meta_tokens, m_ln_g, m_w_in, m_ret_gn_g, m_conv_dw_w, m_conv_dw_b, m_conv_ln_g, m_conv_ln_b, m_conv_pw_w, m_conv_pw_b, m_w_out, m_final_g, v_meta_tokens, v_ln_g, v_w_in, v_ret_gn_g, v_conv_dw_w, v_conv_dw_b, v_conv_ln_g, v_conv_ln_b, v_conv_pw_w, v_conv_pw_b, v_w_out, v_final_g):
    given = dict(x=x, meta_tokens=meta_tokens, ln_g=ln_g, w_in=w_in, ret_gn_g=ret_gn_g, conv_dw_w=conv_dw_w, conv_dw_b=conv_dw_b, conv_ln_g=conv_ln_g, conv_ln_b=conv_ln_b, conv_pw_w=conv_pw_w, conv_pw_b=conv_pw_b, w_out=w_out, final_g=final_g, loss_target=loss_target, m_meta_tokens=m_meta_tokens, m_ln_g=m_ln_g, m_w_in=m_w_in, m_ret_gn_g=m_ret_gn_g, m_conv_dw_w=m_conv_dw_w, m_conv_dw_b=m_conv_dw_b, m_conv_ln_g=m_conv_ln_g, m_conv_ln_b=m_conv_ln_b, m_conv_pw_w=m_conv_pw_w, m_conv_pw_b=m_conv_pw_b, m_w_out=m_w_out, m_final_g=m_final_g, v_meta_tokens=v_meta_tokens, v_ln_g=v_ln_g, v_w_in=v_w_in, v_ret_gn_g=v_ret_gn_g, v_conv_dw_w=v_conv_dw_w, v_conv_dw_b=v_conv_dw_b, v_conv_ln_g=v_conv_ln_g, v_conv_ln_b=v_conv_ln_b, v_conv_pw_w=v_conv_pw_w, v_conv_pw_b=v_conv_pw_b, v_w_out=v_w_out, v_final_g=v_final_g)
    weights = {n: given[n] for n in TWIN_WEIGHTS}
    shared = {n: given[n] for n in SHARED_INPUTS}
    per_example = {n: given[n] for n in ['x']}
    grad_fn = _jax.value_and_grad(_loss, argnums=(0, 1))

    def one_microbatch(ex, loss_target):
        ex = dict(ex)
        diff = ex.pop(TWIN_DIFF_INPUT)
        return grad_fn(weights, diff, {**shared, **ex}, loss_target)

    if N_MICROBATCH == 1:
        loss, (grad_w, grad_x) = one_microbatch(per_example, given["loss_target"])
    else:
        def body(carry, xs):
            loss_sum, grad_sum = carry
            l_k, (gw_k, gx_k) = one_microbatch(xs[0], xs[1])
            with _jax.named_scope("update"):
                return (loss_sum + l_k, _jax.tree.map(_jnp.add, grad_sum, gw_k)), gx_k

        init = (_jnp.zeros((), _jnp.float32), _jax.tree.map(_jnp.zeros_like, weights))
        (loss, grad_w), grad_x = _jax.lax.scan(body, init, (per_example, given["loss_target"]))
    with _jax.named_scope("update"):
        delta_w, new_m, new_v = {}, {}, {}
        for n in TWIN_WEIGHTS:
            delta_w[n], new_m[n], new_v[n] = _adamw(weights[n], grad_w[n], given["m_" + n], given["v_" + n])
    return (loss, grad_x, *[grad_w[n] for n in TWIN_WEIGHTS], *[delta_w[n] for n in TWIN_WEIGHTS],
            *[new_m[n] for n in TWIN_WEIGHTS], *[new_v[n] for n in TWIN_WEIGHTS])
```

```python
import functools

import numpy as np
import jax
import jax.numpy as jnp
from jax import lax
from jax.experimental import pallas as pl
from jax.experimental.pallas import tpu as pltpu

N_META = 16
RET_HEADS = 4
CONV_K = 31
CHUNK = 128
ROPE_BASE = 10000.0
EPS = 1e-6
ADAM_LR = 0.001
ADAM_B1 = 0.9
ADAM_B2 = 0.999
ADAM_EPS = 1e-08
ADAM_WD = 0.01
ADAM_STEP = 10

N_DEV = 8
META_TILE = 256
HALO = 32
SMALL_ROWS = 32
VMEM_BYTES_V7X = 64 * 1024 * 1024

F32 = jnp.float32
BF16 = jnp.bfloat16
MESH = pl.DeviceIdType.MESH

NN = (((1,), (0,)), ((), ()))
NT = (((1,), (1,)), ((), ()))
TN = (((0,), (0,)), ((), ()))


def _dot(a, b, dims=NN):
    return lax.dot_general(a, b, dims, preferred_element_type=F32)


def _pick_tile(n, target, mult=16):
    best = None
    for t in range(mult, min(n, target) + 1, mult):
        if n % t == 0:
            best = t
    assert best is not None, (n, target)
    return best


def _params(sem=None, vmem_mb=None):
    kw = {}
    if sem is not None:
        kw["dimension_semantics"] = sem
    if vmem_mb is not None:
        kw["vmem_limit_bytes"] = min(vmem_mb * 1024 * 1024, VMEM_BYTES_V7X - 4 * 1024 * 1024)
    return pltpu.CompilerParams(**kw)


def _sigmoid(x):
    return jax.nn.sigmoid(x)


def _dsilu(x, sg):
    return sg * (1.0 + x * (1.0 - sg))


def _decay_tables(heads):
    h = np.arange(heads, dtype=np.float32)
    gamma = (1.0 - np.exp2(-5.0 - h)).astype(np.float32)
    log_g = np.log(gamma).astype(np.float32)
    idx = np.arange(CHUNK, dtype=np.float32)
    rel = idx[:, None] - idx[None, :]
    mask = np.where(rel[None] >= 0, np.exp(np.maximum(rel, 0.0)[None] * log_g[:, None, None]), 0.0)
    qd = np.exp((idx[None, :] + 1.0) * log_g[:, None])
    kd = np.exp((CHUNK - 1.0 - idx[None, :]) * log_g[:, None])
    cd = np.exp(CHUNK * log_g)
    return (mask.astype(np.float32), qd.astype(np.float32)[:, :, None], kd.astype(np.float32)[:, :, None],
            [float(c) for c in cd.astype(np.float32)])


def _gather_weights(shards, block_axes):
    n_arr = len(shards)
    out_shapes = []
    for s, ax in zip(shards, block_axes):
        shp = list(s.shape)
        shp[ax] *= N_DEV
        out_shapes.append(jax.ShapeDtypeStruct(tuple(shp), s.dtype))

    def body(*refs):
        ins, outs = refs[:n_arr], refs[n_arr:2 * n_arr]
        send_sems, recv_sems, local_sems = refs[2 * n_arr:]
        x, y, c = lax.axis_index("x"), lax.axis_index("y"), lax.axis_index("c")
        me, sibling = (x, y, c), (x, y, 1 - c)
        chips = [(1 - x, y), (x, 1 - y), (1 - x, 1 - y)]

        def block(a, dev):
            n = ins[a].shape[block_axes[a]]
            start = pl.multiple_of((4 * dev[0] + 2 * dev[1] + dev[2]) * n, n)
            idx = [slice(None)] * len(ins[a].shape)
            idx[block_axes[a]] = pl.ds(start, n)
            return outs[a].at[tuple(idx)]

        def copy(a, k, dev, to, src=None):
            return pltpu.make_async_remote_copy(
                src_ref=block(a, dev) if src is None else src, dst_ref=block(a, dev),
                send_sem=send_sems.at[a, k], recv_sem=recv_sems.at[a, k],
                device_id=to, device_id_type=MESH)

        mine = [pltpu.make_async_copy(ins[a], block(a, me), local_sems.at[a]) for a in range(n_arr)]
        for cp in mine:
            cp.start()
        first = []
        for a in range(n_arr):
            first.append(copy(a, 0, me, sibling, src=ins[a]))
            first += [copy(a, 1 + j, me, (*chip, c), src=ins[a]) for j, chip in enumerate(chips)]
        for cp in first:
            cp.start()
        passed = []
        for j, chip in enumerate(chips):
            for a in range(n_arr):
                copy(a, 1 + j, (*chip, c), me).wait_recv()
                fwd = copy(a, 4 + j, (*chip, c), sibling)
                fwd.start()
                passed.append(fwd)
        for a in range(n_arr):
            copy(a, 0, sibling, me).wait_recv()
            for j, chip in enumerate(chips):
                copy(a, 4 + j, (*chip, 1 - c), me).wait_recv()
        for cp in first + passed:
            cp.wait_send()
        for cp in mine:
            cp.wait()

    hbm = pl.BlockSpec(memory_space=pl.ANY)
    return pl.pallas_call(
        body, name="gather_weights",
        out_shape=out_shapes,
        in_specs=[hbm] * n_arr, out_specs=[hbm] * n_arr,
        scratch_shapes=[pltpu.SemaphoreType.DMA((n_arr, 7)), pltpu.SemaphoreType.DMA((n_arr, 7)),
                        pltpu.SemaphoreType.DMA((n_arr,))],
    )(*shards)


def _exchange_grads(parts, block_axes):
    n_arr = len(parts)
    out_shapes = []
    for p, ax in zip(parts, block_axes):
        shp = list(p.shape)
        if ax is not None:
            shp[ax] //= N_DEV
        out_shapes.append(jax.ShapeDtypeStruct((N_DEV, *shp), p.dtype))

    def body(*refs):
        ins, outs = refs[:n_arr], refs[n_arr:2 * n_arr]
        send_sems, recv_sems, local_sems = refs[2 * n_arr:]
        x, y, c = lax.axis_index("x"), lax.axis_index("y"), lax.axis_index("c")
        me_idx = 4 * x + 2 * y + c

        def src_block(a, dev_idx):
            ax = block_axes[a]
            if ax is None:
                return ins[a]
            n = ins[a].shape[ax] // N_DEV
            idx = [slice(None)] * len(ins[a].shape)
            idx[ax] = pl.ds(pl.multiple_of(dev_idx * n, n), n)
            return ins[a].at[tuple(idx)]

        copies = []
        for a in range(n_arr):
            cp = pltpu.make_async_copy(src_block(a, me_idx), outs[a].at[me_idx], local_sems.at[a])
            cp.start()
            copies.append(cp)
        remote = []
        for m in range(1, N_DEV):
            fx, fy, fc = (m >> 2) & 1, (m >> 1) & 1, m & 1
            px, py, pc = x ^ fx, y ^ fy, c ^ fc
            peer_idx = 4 * px + 2 * py + pc
            for a in range(n_arr):
                cp = pltpu.make_async_remote_copy(
                    src_ref=src_block(a, peer_idx), dst_ref=outs[a].at[me_idx],
                    send_sem=send_sems.at[a, m - 1], recv_sem=recv_sems.at[a, m - 1],
                    device_id=(px, py, pc), device_id_type=MESH)
                cp.start()
                remote.append(cp)
        for cp in remote:
            cp.wait_recv()
        for cp in remote:
            cp.wait_send()
        for cp in copies:
            cp.wait()

    hbm = pl.BlockSpec(memory_space=pl.ANY)
    return pl.pallas_call(
        body, name="exchange_grads",
        out_shape=out_shapes,
        in_specs=[hbm] * n_arr, out_specs=[hbm] * n_arr,
        scratch_shapes=[pltpu.SemaphoreType.DMA((n_arr, 7)), pltpu.SemaphoreType.DMA((n_arr, 7)),
                        pltpu.SemaphoreType.DMA((n_arr,))],
    )(*parts)


def _in_proj(hp, ln_g, w_in):
    R, D = hp.shape
    E = w_in.shape[1]
    tm = _pick_tile(R, 1056)
    tn = _pick_tile(E, 1024, 128)

    def body(h_ref, g_ref, w_ref, proj_ref, hn_ref):
        @pl.when(pl.program_id(1) == 0)
        def _():
            h = h_ref[...]
            r = lax.rsqrt(jnp.mean(h * h, axis=-1, keepdims=True) + EPS)
            hn_ref[...] = (h * r * g_ref[...]).astype(BF16)

        proj_ref[...] = _dot(hn_ref[...], w_ref[...]).astype(BF16)

    return pl.pallas_call(
        body, name="in_proj",
        grid=(R // tm, E // tn),
        in_specs=[pl.BlockSpec((tm, D), lambda i, j: (i, 0)),
                  pl.BlockSpec((1, D), lambda i, j: (0, 0)),
                  pl.BlockSpec((D, tn), lambda i, j: (0, j))],
        out_specs=[pl.BlockSpec((tm, tn), lambda i, j: (i, j)),
                   pl.BlockSpec((tm, D), lambda i, j: (i, 0))],
        out_shape=[jax.ShapeDtypeStruct((R, E), BF16), jax.ShapeDtypeStruct((R, D), BF16)],
        compiler_params=_params(("arbitrary", "arbitrary"), 56),
    )(hp, ln_g, w_in)


def _rot(t, cos, sin, half):
    t1, t2 = t[:, :half], t[:, half:]
    return jnp.concatenate([t1 * cos - t2 * sin, t1 * sin + t2 * cos], axis=-1)


def _rot_inv(t, cos, sin, half):
    t1, t2 = t[:, :half], t[:, half:]
    return jnp.concatenate([t1 * cos + t2 * sin, t2 * cos - t1 * sin], axis=-1)


def _chunk_order(n_chunks):
    lead = META_TILE // CHUNK
    return lambda l: (l + n_chunks - lead) % n_chunks


def _retention_fwd(proj, cos, sin, gn_g, mix):
    R, E = proj.shape
    RW = gn_g.shape[1]
    H = RET_HEADS
    hd = RW // H
    half = hd // 2
    NC = R // CHUNK
    mask, qd, kd, cd = _decay_tables(H)
    scale = float(hd) ** -0.5
    phys = _chunk_order(NC)

    def body(p_ref, cos_ref, sin_ref, mask_ref, qd_ref, kd_ref, gn_ref, y_ref, st_ref, state):
        @pl.when(pl.program_id(0) == 0)
        def _():
            state[...] = jnp.zeros_like(state)

        cs, sn = cos_ref[...], sin_ref[...]
        for h in range(H):
            cols = slice(h * hd, (h + 1) * hd)
            q = p_ref[:, h * hd:(h + 1) * hd].astype(F32)
            k = p_ref[:, RW + h * hd:RW + (h + 1) * hd].astype(F32)
            v = p_ref[:, 2 * RW + h * hd:2 * RW + (h + 1) * hd]
            g = p_ref[:, 3 * RW + h * hd:3 * RW + (h + 1) * hd].astype(F32)
            qr = _rot(q, cs, sn, half)
            kr = _rot(k, cs, sn, half) * scale
            s = _dot(qr.astype(BF16), kr.astype(BF16), NT) * mask_ref[h]
            s_prev = state[h]
            s_prev_b = s_prev.astype(BF16)
            st_ref[0, h] = s_prev_b
            y_raw = _dot(s.astype(BF16), v) + _dot((qr * qd_ref[h]).astype(BF16), s_prev_b)
            state[h] = s_prev * cd[h] + _dot((kr * kd_ref[h]).astype(BF16), v, TN)
            mu = jnp.mean(y_raw, axis=-1, keepdims=True)
            yc = y_raw - mu
            var = jnp.mean(yc * yc, axis=-1, keepdims=True)
            out = yc * lax.rsqrt(var + EPS) * gn_ref[:, cols] * (g * _sigmoid(g))
            y_ref[:, cols] = out.astype(BF16)

    const3 = lambda l: (0, 0, 0)
    return pl.pallas_call(
        body, name="retention_fwd",
        grid=(NC,),
        in_specs=[pl.BlockSpec((CHUNK, 4 * RW), lambda l: (phys(l), 0)),
                  pl.BlockSpec((CHUNK, half), lambda l: (phys(l), 0)),
                  pl.BlockSpec((CHUNK, half), lambda l: (phys(l), 0)),
                  pl.BlockSpec((H, CHUNK, CHUNK), const3),
                  pl.BlockSpec((H, CHUNK, 1), const3),
                  pl.BlockSpec((H, CHUNK, 1), const3),
                  pl.BlockSpec((1, RW), lambda l: (0, 0))],
        out_specs=[pl.BlockSpec((CHUNK, RW), lambda l: (phys(l), 0)),
                   pl.BlockSpec((1, H, hd, hd), lambda l: (phys(l), 0, 0, 0))],
        out_shape=[jax.ShapeDtypeStruct((R, mix), BF16), jax.ShapeDtypeStruct((NC, H, hd, hd), BF16)],
        scratch_shapes=[pltpu.VMEM((H, hd, hd), F32)],
        compiler_params=_params(("arbitrary",), 32),
    )(proj, cos, sin, jnp.asarray(mask), jnp.asarray(qd), jnp.asarray(kd), gn_g)


def _conv_taps(u_ref, w_ref, row0, rows, lanes):
    acc = None
    for k in range(CONV_K):
        term = u_ref[row0 + k:row0 + k + rows, lanes] * w_ref[k:k + 1, lanes]
        acc = term if acc is None else acc + term
    return acc


CONV_ROWS = 32
CONV_LANES = 512


def _conv_order(n_tiles):
    return lambda l: (l + n_tiles - 1) % n_tiles


def _halo_block(n_tiles, tm):
    per = tm // HALO
    return lambda l: ((l + n_tiles - 2) % n_tiles) * per + per - 1


def _conv_forward_tile(first, a_ref, b_ref, ah_ref, bh_ref, w_ref, wb_ref, lg_ref, lb_ref, u_ext, c_scr, tm, CW):
    a = a_ref[...].astype(F32)
    b = b_ref[...].astype(F32)
    uh = ah_ref[...].astype(F32) * _sigmoid(bh_ref[...].astype(F32))
    u_ext[0:HALO, :] = jnp.where(first, 0.0, uh)
    u_ext[HALO:HALO + tm, :] = a * _sigmoid(b)
    base = HALO - (CONV_K - 1)
    for r0 in range(0, tm, CONV_ROWS):
        for l0 in range(0, CW, CONV_LANES):
            lanes = slice(l0, l0 + CONV_LANES)
            c_scr[r0:r0 + CONV_ROWS, lanes] = _conv_taps(u_ext, w_ref, base + r0, CONV_ROWS, lanes) + wb_ref[:, lanes]
    cv = c_scr[...]
    mu = jnp.mean(cv, axis=-1, keepdims=True)
    cc = cv - mu
    var = jnp.mean(cc * cc, axis=-1, keepdims=True)
    rstd = lax.rsqrt(var + EPS)
    xh = cc * rstd
    return xh, rstd, xh * lg_ref[...] + lb_ref[...]


def _conv_fwd(proj, y_in, dw_w, dw_b, ln_g, ln_b, pw_w, pw_b):
    R, E = proj.shape
    CW = pw_w.shape[0]
    tm = META_TILE
    NTL = R // tm
    phys = _conv_order(NTL)
    halo = _halo_block(NTL, tm)
    cb = (E - 3 * CW) // CW

    def body(a_ref, b_ref, g_ref, ah_ref, bh_ref, w_ref, wb_ref, lg_ref, lb_ref, pw_ref, pb_ref, yin_ref,
             y_ref, u_ext, c_scr):
        first = pl.program_id(0) == 0
        _, _, ln = _conv_forward_tile(first, a_ref, b_ref, ah_ref, bh_ref, w_ref, wb_ref, lg_ref, lb_ref,
                                      u_ext, c_scr, tm, CW)
        s = (ln * _sigmoid(ln)).astype(BF16)
        upw = _dot(s, pw_ref[...]) + pb_ref[...]
        g = g_ref[...].astype(F32)
        y_ref[...] = (upw * (g * _sigmoid(g))).astype(BF16)

    row = lambda l: (0, 0)
    return pl.pallas_call(
        body, name="conv_fwd",
        grid=(NTL,),
        in_specs=[pl.BlockSpec((tm, CW), lambda l: (phys(l), cb)),
                  pl.BlockSpec((tm, CW), lambda l: (phys(l), cb + 1)),
                  pl.BlockSpec((tm, CW), lambda l: (phys(l), cb + 2)),
                  pl.BlockSpec((HALO, CW), lambda l: (halo(l), cb)),
                  pl.BlockSpec((HALO, CW), lambda l: (halo(l), cb + 1)),
                  pl.BlockSpec((HALO, CW), row),
                  pl.BlockSpec((1, CW), row), pl.BlockSpec((1, CW), row), pl.BlockSpec((1, CW), row),
                  pl.BlockSpec((CW, CW), row),
                  pl.BlockSpec((1, CW), row),
                  pl.BlockSpec(memory_space=pl.ANY)],
        out_specs=pl.BlockSpec((tm, CW), lambda l: (phys(l), 1)),
        out_shape=jax.ShapeDtypeStruct(y_in.shape, BF16),
        input_output_aliases={11: 0},
        scratch_shapes=[pltpu.VMEM((HALO + tm, CW), F32), pltpu.VMEM((tm, CW), F32)],
        compiler_params=_params(("arbitrary",), 40),
    )(proj, proj, proj, proj, proj, dw_w, dw_b, ln_g, ln_b, pw_w, pw_b, y_in)


def _out_proj_loss(hp, y, w_out, final_g, target):
    R, D = hp.shape
    MIX = y.shape[1]
    SEQ = target.shape[0]
    tm = _pick_tile(META_TILE, 256)
    n_seq = SEQ // tm
    n_tiles = R // tm
    rows_out = _pick_tile(MIX, 256)

    def body(h_ref, y_ref, w_hbm, fg_ref, t_ref, dh2_ref, dy_ref, dwo_hbm, dfg_ref, loss_ref, w_scr, acc, stage, sem):
        i = pl.program_id(0)

        @pl.when(i == 0)
        def _():
            cp = pltpu.make_async_copy(w_hbm, w_scr, sem)
            cp.start()
            acc[...] = jnp.zeros_like(acc)
            dfg_ref[...] = jnp.zeros_like(dfg_ref)
            loss_ref[...] = jnp.zeros_like(loss_ref)
            cp.wait()

        yb = y_ref[...]
        h2 = h_ref[...] + _dot(yb, w_scr[...])
        r2 = lax.rsqrt(jnp.mean(h2 * h2, axis=-1, keepdims=True) + EPS)
        n = h2 * r2
        fg = fg_ref[...]
        err = jnp.where(i < n_seq, n * fg - t_ref[...], 0.0)
        loss_ref[...] += 0.5 * jnp.sum(jnp.mean(err * err, axis=-1, keepdims=True), axis=0, keepdims=True)
        dout = err * (1.0 / D)
        dfg_ref[...] += jnp.sum(dout * n, axis=0, keepdims=True)
        dn = dout * fg
        dh2 = r2 * (dn - n * jnp.mean(dn * n, axis=-1, keepdims=True))
        dh2_ref[...] = dh2
        dh2b = dh2.astype(BF16)
        dy_ref[...] = _dot(dh2b, w_scr[...], NT).astype(BF16)
        acc[...] += _dot(yb, dh2b, TN)

        @pl.when(i == n_tiles - 1)
        def _():
            for r in range(0, MIX, rows_out):
                stage[...] = acc[r:r + rows_out, :].astype(BF16)
                cp = pltpu.make_async_copy(stage, dwo_hbm.at[r:r + rows_out, :], sem)
                cp.start()
                cp.wait()

    row = lambda i: (0, 0)
    return pl.pallas_call(
        body, name="out_proj_loss",
        grid=(n_tiles,),
        in_specs=[pl.BlockSpec((tm, D), lambda i: (i, 0)),
                  pl.BlockSpec((tm, MIX), lambda i: (i, 0)),
                  pl.BlockSpec(memory_space=pl.ANY),
                  pl.BlockSpec((1, D), row),
                  pl.BlockSpec((tm, D), lambda i: (jnp.minimum(i, n_seq - 1), 0))],
        out_specs=[pl.BlockSpec((tm, D), lambda i: (i, 0)),
                   pl.BlockSpec((tm, MIX), lambda i: (i, 0)),
                   pl.BlockSpec(memory_space=pl.ANY),
                   pl.BlockSpec((1, D), row),
                   pl.BlockSpec((1, 1), row)],
        out_shape=[jax.ShapeDtypeStruct((R, D), F32), jax.ShapeDtypeStruct((R, MIX), BF16),
                   jax.ShapeDtypeStruct((MIX, D), BF16), jax.ShapeDtypeStruct((1, D), F32),
                   jax.ShapeDtypeStruct((1, 1), F32)],
        scratch_shapes=[pltpu.VMEM((MIX, D), BF16), pltpu.VMEM((MIX, D), F32), pltpu.VMEM((rows_out, D), BF16),
                        pltpu.SemaphoreType.DMA],
        compiler_params=_params(("arbitrary",), 60),
    )(hp, y, w_out, final_g, target)


def _conv_bwd(proj, dy, dw_w, dw_b, ln_g, ln_b, pw_w, pw_b):
    R, E = proj.shape
    CW = pw_w.shape[0]
    tm = META_TILE
    NTL = R // tm
    order = _conv_order(NTL)
    phys = lambda i: order(NTL - 1 - i)
    halo_l = _halo_block(NTL, tm)
    halo = lambda i: halo_l(NTL - 1 - i)
    cb = (E - 3 * CW) // CW
    base = HALO - (CONV_K - 1)

    def body(a_ref, b_ref, g_ref, ah_ref, bh_ref, dy_ref, w_ref, wb_ref, lg_ref, lb_ref, pw_ref, pb_ref,
             dp_ref, dpw_ref, dww_ref, vec_ref, u_ext, c_scr, dc_ext, du_scr):
        i = pl.program_id(0)

        @pl.when(i == 0)
        def _():
            dpw_ref[...] = jnp.zeros_like(dpw_ref)
            dww_ref[...] = jnp.zeros_like(dww_ref)
            vec_ref[...] = jnp.zeros_like(vec_ref)
            dc_ext[tm:tm + HALO, :] = jnp.zeros((HALO, CW), F32)

        first = i == NTL - 1
        xh, rstd, ln = _conv_forward_tile(first, a_ref, b_ref, ah_ref, bh_ref, w_ref, wb_ref, lg_ref, lb_ref,
                                          u_ext, c_scr, tm, CW)
        sg = _sigmoid(ln)
        sb = (ln * sg).astype(BF16)
        upw = _dot(sb, pw_ref[...]) + pb_ref[...]
        g = g_ref[...].astype(F32)
        sgg = _sigmoid(g)
        dyc = dy_ref[...].astype(F32)
        dp_ref[:, 2 * CW:3 * CW] = (dyc * upw * _dsilu(g, sgg)).astype(BF16)
        dupw = dyc * (g * sgg)
        dupw_b = dupw.astype(BF16)
        vec_ref[0:1, :] += jnp.sum(dupw, axis=0, keepdims=True)
        dpw_ref[...] += _dot(sb, dupw_b, TN)
        dln = _dot(dupw_b, pw_ref[...], NT) * _dsilu(ln, sg)
        vec_ref[1:2, :] += jnp.sum(dln * xh, axis=0, keepdims=True)
        vec_ref[2:3, :] += jnp.sum(dln, axis=0, keepdims=True)
        dxh = dln * lg_ref[...]
        dc = rstd * (dxh - jnp.mean(dxh, axis=-1, keepdims=True) - xh * jnp.mean(dxh * xh, axis=-1, keepdims=True))
        vec_ref[3:4, :] += jnp.sum(dc, axis=0, keepdims=True)
        dc_ext[0:tm, :] = dc

        for l0 in range(0, CW, CONV_LANES):
            lanes = slice(l0, l0 + CONV_LANES)
            for r0 in range(0, tm, CONV_ROWS):
                acc = None
                for k in range(CONV_K):
                    off = r0 + CONV_K - 1 - k
                    term = dc_ext[off:off + CONV_ROWS, lanes] * w_ref[k:k + 1, lanes]
                    acc = term if acc is None else acc + term
                du_scr[r0:r0 + CONV_ROWS, lanes] = acc
            for k in range(CONV_K):
                prod = dc_ext[0:tm, lanes] * u_ext[base + k:base + k + tm, lanes]
                dww_ref[k:k + 1, lanes] += jnp.sum(prod, axis=0, keepdims=True)

        du = du_scr[...]
        a = a_ref[...].astype(F32)
        sgb = _sigmoid(b_ref[...].astype(F32))
        dp_ref[:, 0:CW] = (du * sgb).astype(BF16)
        dp_ref[:, CW:2 * CW] = (du * a * sgb * (1.0 - sgb)).astype(BF16)
        dc_ext[tm:tm + HALO, :] = dc_ext[0:HALO, :]

    row = lambda i: (0, 0)
    return pl.pallas_call(
        body, name="conv_bwd",
        grid=(NTL,),
        in_specs=[pl.BlockSpec((tm, CW), lambda i: (phys(i), cb)),
                  pl.BlockSpec((tm, CW), lambda i: (phys(i), cb + 1)),
                  pl.BlockSpec((tm, CW), lambda i: (phys(i), cb + 2)),
                  pl.BlockSpec((HALO, CW), lambda i: (halo(i), cb)),
                  pl.BlockSpec((HALO, CW), lambda i: (halo(i), cb + 1)),
                  pl.BlockSpec((tm, CW), lambda i: (phys(i), 1)),
                  pl.BlockSpec((HALO, CW), row),
                  pl.BlockSpec((1, CW), row), pl.BlockSpec((1, CW), row), pl.BlockSpec((1, CW), row),
                  pl.BlockSpec((CW, CW), row),
                  pl.BlockSpec((1, CW), row)],
        out_specs=[pl.BlockSpec((tm, 3 * CW), lambda i: (phys(i), 0)),
                   pl.BlockSpec((CW, CW), row),
                   pl.BlockSpec((HALO, CW), row),
                   pl.BlockSpec((8, CW), row)],
        out_shape=[jax.ShapeDtypeStruct((R, 3 * CW), BF16), jax.ShapeDtypeStruct((CW, CW), F32),
                   jax.ShapeDtypeStruct((HALO, CW), F32), jax.ShapeDtypeStruct((8, CW), F32)],
        scratch_shapes=[pltpu.VMEM((HALO + tm, CW), F32), pltpu.VMEM((tm, CW), F32),
                        pltpu.VMEM((tm + HALO, CW), F32), pltpu.VMEM((tm, CW), F32)],
        compiler_params=_params(("arbitrary",), 56),
    )(proj, proj, proj, proj, proj, dy, dw_w, dw_b, ln_g, ln_b, pw_w, pw_b)


def _retention_bwd(proj, cos, sin, gn_g, states, dy):
    R, E = proj.shape
    RW = gn_g.shape[1]
    H = RET_HEADS
    hd = RW // H
    half = hd // 2
    NC = R // CHUNK
    mask, qd, kd, cd = _decay_tables(H)
    scale = float(hd) ** -0.5
    order = _chunk_order(NC)
    phys = lambda i: order(NC - 1 - i)

    def body(p_ref, cos_ref, sin_ref, mask_ref, qd_ref, kd_ref, gn_ref, st_ref, dy_ref, dp_ref, dgn_ref, dstate):
        @pl.when(pl.program_id(0) == 0)
        def _():
            dstate[...] = jnp.zeros_like(dstate)
            dgn_ref[...] = jnp.zeros_like(dgn_ref)

        cs, sn = cos_ref[...], sin_ref[...]
        for h in range(H):
            cols = slice(h * hd, (h + 1) * hd)
            q = p_ref[:, h * hd:(h + 1) * hd].astype(F32)
            k = p_ref[:, RW + h * hd:RW + (h + 1) * hd].astype(F32)
            v = p_ref[:, 2 * RW + h * hd:2 * RW + (h + 1) * hd]
            g = p_ref[:, 3 * RW + h * hd:3 * RW + (h + 1) * hd].astype(F32)
            msk, qdh, kdh = mask_ref[h], qd_ref[h], kd_ref[h]
            qr = _rot(q, cs, sn, half)
            kr = _rot(k, cs, sn, half) * scale
            qb, kb = qr.astype(BF16), kr.astype(BF16)
            qdb, kdb = (qr * qdh).astype(BF16), (kr * kdh).astype(BF16)
            s_prev = st_ref[0, h]
            sb = (_dot(qb, kb, NT) * msk).astype(BF16)
            y_raw = _dot(sb, v) + _dot(qdb, s_prev)
            mu = jnp.mean(y_raw, axis=-1, keepdims=True)
            yc = y_raw - mu
            rstd = lax.rsqrt(jnp.mean(yc * yc, axis=-1, keepdims=True) + EPS)
            xh = yc * rstd
            gn = gn_ref[:, cols]
            sg = _sigmoid(g)
            dyh = dy_ref[:, cols].astype(F32)
            dg = dyh * (xh * gn) * _dsilu(g, sg)
            dyn = dyh * (g * sg)
            dgn_ref[:, cols] += jnp.sum(dyn * xh, axis=0, keepdims=True)
            dxh = dyn * gn
            dyr = rstd * (dxh - jnp.mean(dxh, axis=-1, keepdims=True)
                          - xh * jnp.mean(dxh * xh, axis=-1, keepdims=True))
            dyrb = dyr.astype(BF16)
            dst = dstate[h]
            dstb = dst.astype(BF16)
            dsb = (_dot(dyrb, v, NT) * msk).astype(BF16)
            dqr = _dot(dsb, kb) + _dot(dyrb, s_prev, NT) * qdh
            dkr = _dot(dsb, qb, TN) + _dot(v, dstb, NT) * kdh
            dv = _dot(sb, dyrb, TN) + _dot(kdb, dstb)
            dstate[h] = dst * cd[h] + _dot(qdb, dyrb, TN)
            dp_ref[:, h * hd:(h + 1) * hd] = _rot_inv(dqr, cs, sn, half).astype(BF16)
            dp_ref[:, RW + h * hd:RW + (h + 1) * hd] = (_rot_inv(dkr, cs, sn, half) * scale).astype(BF16)
            dp_ref[:, 2 * RW + h * hd:2 * RW + (h + 1) * hd] = dv.astype(BF16)
            dp_ref[:, 3 * RW + h * hd:3 * RW + (h + 1) * hd] = dg.astype(BF16)

    const3 = lambda i: (0, 0, 0)
    return pl.pallas_call(
        body, name="retention_bwd",
        grid=(NC,),
        in_specs=[pl.BlockSpec((CHUNK, 4 * RW), lambda i: (phys(i), 0)),
                  pl.BlockSpec((CHUNK, half), lambda i: (phys(i), 0)),
                  pl.BlockSpec((CHUNK, half), lambda i: (phys(i), 0)),
                  pl.BlockSpec((H, CHUNK, CHUNK), const3),
                  pl.BlockSpec((H, CHUNK, 1), const3),
                  pl.BlockSpec((H, CHUNK, 1), const3),
                  pl.BlockSpec((1, RW), lambda i: (0, 0)),
                  pl.BlockSpec((1, H, hd, hd), lambda i: (phys(i), 0, 0, 0)),
                  pl.BlockSpec((CHUNK, RW), lambda i: (phys(i), 0))],
        out_specs=[pl.BlockSpec((CHUNK, 4 * RW), lambda i: (phys(i), 0)),
                   pl.BlockSpec((1, RW), lambda i: (0, 0))],
        out_shape=[jax.ShapeDtypeStruct((R, 4 * RW), BF16), jax.ShapeDtypeStruct((1, RW), F32)],
        scratch_shapes=[pltpu.VMEM((H, hd, hd), F32)],
        compiler_params=_params(("arbitrary",), 32),
    )(proj, cos, sin, jnp.asarray(mask), jnp.asarray(qd), jnp.asarray(kd), gn_g, states, dy)


def _dproj_specs(tk, tn, n_ret, tile_axis, col_axis):
    def ret_map(*ids):
        t, j = ids[tile_axis], ids[col_axis]
        return (jnp.where(j < n_ret, t, 0), jnp.minimum(j, n_ret - 1))

    def conv_map(*ids):
        t, j = ids[tile_axis], ids[col_axis]
        return (jnp.where(j >= n_ret, t, 0), jnp.maximum(j - n_ret, 0))

    return pl.BlockSpec((tk, tn), ret_map), pl.BlockSpec((tk, tn), conv_map)


def _w_in_grad(hn, dp_ret, dp_conv):
    R, D = hn.shape
    tn = _pick_tile(dp_conv.shape[1] // 3, 1024, 128)
    n_ret, n_conv = dp_ret.shape[1] // tn, dp_conv.shape[1] // tn
    E = dp_ret.shape[1] + dp_conv.shape[1]
    tk = _pick_tile(R, 1056)
    n_t = R // tk
    ret_spec, conv_spec = _dproj_specs(tk, tn, n_ret, 1, 0)

    def body(hn_ref, r_ref, c_ref, out_ref, acc):
        j, t = pl.program_id(0), pl.program_id(1)

        @pl.when(t == 0)
        def _():
            acc[...] = jnp.zeros_like(acc)

        @pl.when(j < n_ret)
        def _():
            acc[...] += _dot(hn_ref[...], r_ref[...], TN)

        @pl.when(j >= n_ret)
        def _():
            acc[...] += _dot(hn_ref[...], c_ref[...], TN)

        @pl.when(t == n_t - 1)
        def _():
            out_ref[...] = acc[...].astype(BF16)

    return pl.pallas_call(
        body, name="w_in_grad",
        grid=(n_ret + n_conv, n_t),
        in_specs=[pl.BlockSpec((tk, D), lambda j, t: (t, 0)), ret_spec, conv_spec],
        out_specs=pl.BlockSpec((D, tn), lambda j, t: (0, j)),
        out_shape=jax.ShapeDtypeStruct((D, E), BF16),
        scratch_shapes=[pltpu.VMEM((D, tn), F32)],
        compiler_params=_params(("arbitrary", "arbitrary"), 48),
    )(hn, dp_ret, dp_conv)


def _h_grad(dp_ret, dp_conv, w_in, hp, dh2, ln_g):
    R, D = hp.shape
    tn = _pick_tile(dp_conv.shape[1] // 3, 1024, 128)
    n_ret, n_conv = dp_ret.shape[1] // tn, dp_conv.shape[1] // tn
    n_k = n_ret + n_conv
    tm = _pick_tile(R, 528)
    ret_spec, conv_spec = _dproj_specs(tm, tn, n_ret, 0, 1)

    def body(r_ref, c_ref, w_ref, h_ref, dh2_ref, g_ref, dh_ref, dlg_ref, acc):
        t, k = pl.program_id(0), pl.program_id(1)

        @pl.when(k == 0)
        def _():
            acc[...] = jnp.zeros_like(acc)

        @pl.when((k == 0) & (t == 0))
        def _():
            dlg_ref[...] = jnp.zeros_like(dlg_ref)

        @pl.when(k < n_ret)
        def _():
            acc[...] += _dot(r_ref[...], w_ref[...], NT)

        @pl.when(k >= n_ret)
        def _():
            acc[...] += _dot(c_ref[...], w_ref[...], NT)

        @pl.when(k == n_k - 1)
        def _():
            h = h_ref[...]
            r = lax.rsqrt(jnp.mean(h * h, axis=-1, keepdims=True) + EPS)
            n = h * r
            dhn = acc[...]
            dlg_ref[...] += jnp.sum(dhn * n, axis=0, keepdims=True)
            dn = dhn * g_ref[...]
            dh_ref[...] = dh2_ref[...] + r * (dn - n * jnp.mean(dn * n, axis=-1, keepdims=True))

    row = lambda t, k: (0, 0)
    return pl.pallas_call(
        body, name="h_grad",
        grid=(R // tm, n_k),
        in_specs=[ret_spec, conv_spec,
                  pl.BlockSpec((D, tn), lambda t, k: (0, k)),
                  pl.BlockSpec((tm, D), lambda t, k: (t, 0)),
                  pl.BlockSpec((tm, D), lambda t, k: (t, 0)),
                  pl.BlockSpec((1, D), row)],
        out_specs=[pl.BlockSpec((tm, D), lambda t, k: (t, 0)),
                   pl.BlockSpec((1, D), row)],
        out_shape=[jax.ShapeDtypeStruct((R, D), F32), jax.ShapeDtypeStruct((1, D), F32)],
        scratch_shapes=[pltpu.VMEM((tm, D), F32)],
        compiler_params=_params(("arbitrary", "arbitrary"), 56),
    )(dp_ret, dp_conv, w_in, hp, dh2, ln_g)


def _adamw(w, g, m, v):
    m = ADAM_B1 * m + (1.0 - ADAM_B1) * g
    v = ADAM_B2 * v + (1.0 - ADAM_B2) * (g * g)
    m_hat = m / (1.0 - ADAM_B1 ** ADAM_STEP)
    v_hat = v / (1.0 - ADAM_B2 ** ADAM_STEP)
    delta = -ADAM_LR * (m_hat / (jnp.sqrt(v_hat) + ADAM_EPS) + ADAM_WD * w)
    return delta, m, v


def _sum_slots(ref):
    g = ref[0].astype(F32)
    for s in range(1, N_DEV):
        g = g + ref[s].astype(F32)
    return g


def _sum_adamw(name, parts, w, m, v, rows_target):
    rows, cols = w.shape
    tr = _pick_tile(rows, rows_target, 8)

    def body(p_ref, w_ref, m_ref, v_ref, g_ref, d_ref, nm_ref, nv_ref):
        g = _sum_slots(p_ref)
        d, nm, nv = _adamw(w_ref[...], g, m_ref[...], v_ref[...])
        g_ref[...] = g
        d_ref[...] = d
        nm_ref[...] = nm
        nv_ref[...] = nv

    tile = pl.BlockSpec((tr, cols), lambda i: (i, 0))
    return pl.pallas_call(
        body, name=name,
        grid=(rows // tr,),
        in_specs=[pl.BlockSpec((N_DEV, tr, cols), lambda i: (0, i, 0)), tile, tile, tile],
        out_specs=[tile] * 4,
        out_shape=[jax.ShapeDtypeStruct((rows, cols), F32)] * 4,
        compiler_params=_params(("arbitrary",), 40),
    )(parts, w, m, v)


def _sum_adamw_small(parts_list, w_list, m_list, v_list, loss_parts):
    n = len(w_list)

    def body(*refs):
        p_refs, w_refs, m_refs, v_refs = refs[:n], refs[n:2 * n], refs[2 * n:3 * n], refs[3 * n:4 * n]
        lp_ref = refs[4 * n]
        outs = refs[4 * n + 1:]
        for a in range(n):
            g = _sum_slots(p_refs[a])
            d, nm, nv = _adamw(w_refs[a][...], g, m_refs[a][...], v_refs[a][...])
            outs[4 * a][...] = g
            outs[4 * a + 1][...] = d
            outs[4 * a + 2][...] = nm
            outs[4 * a + 3][...] = nv
        outs[4 * n][...] = _sum_slots(lp_ref)

    out_shape = []
    for w in w_list:
        out_shape += [jax.ShapeDtypeStruct(w.shape, F32)] * 4
    out_shape.append(jax.ShapeDtypeStruct(loss_parts.shape[1:], F32))
    return pl.pallas_call(body, name="sum_adamw_small", out_shape=out_shape)(
        *parts_list, *w_list, *m_list, *v_list, loss_parts)


def kernel(x, meta_tokens, ln_g, w_in, ret_gn_g, conv_dw_w, conv_dw_b, conv_ln_g, conv_ln_b, conv_pw_w, conv_pw_b, w_out, final_g, loss_target, m_meta_tokens, m_ln_g, m_w_in, m_ret_gn_g, m_conv_dw_w, m_conv_dw_b, m_conv_ln_g, m_conv_ln_b, m_conv_pw_w, m_conv_pw_b, m_w_out, m_final_g, v_meta_tokens, v_ln_g, v_w_in, v_ret_gn_g, v_conv_dw_w, v_conv_dw_b, v_conv_ln_g, v_conv_ln_b, v_conv_pw_w, v_conv_pw_b, v_w_out, v_final_g):
    _, SEQ, D = x.shape
    MIX = w_out.shape[2]
    RW = ret_gn_g.shape[1]
    CW = conv_pw_b.shape[1]
    assert RW == CW and MIX == RW + CW and SEQ % META_TILE == 0 and CONV_K - 1 <= HALO
    R = SEQ + META_TILE
    hd = RW // RET_HEADS
    half = hd // 2
    me = 4 * lax.axis_index("x") + 2 * lax.axis_index("y") + lax.axis_index("c")

    dw_pad = jnp.pad(conv_dw_w[0], ((0, HALO - CONV_K), (0, 0)))
    w_in_g, w_out_g, pw_g, dw_g, meta_g = _gather_weights(
        [w_in[0].astype(BF16), w_out[0].astype(BF16), conv_pw_w[0].astype(BF16), dw_pad, meta_tokens],
        [1, 0, 0, 1, 1])

    pos = jnp.concatenate([jnp.arange(SEQ, dtype=F32) + N_META, jnp.zeros((META_TILE - N_META,), F32),
                           jnp.arange(N_META, dtype=F32)])
    inv_freq = ROPE_BASE ** (-jnp.arange(half, dtype=F32) / half)
    ang = pos[:, None] * inv_freq[None, :]
    cos, sin = jnp.cos(ang), jnp.sin(ang)

    hp = jnp.concatenate([x[0], jnp.zeros((META_TILE - N_META, D), F32), meta_g], axis=0)
    target = loss_target[0]
    final_g2 = final_g[None, :]

    proj, hn = _in_proj(hp, ln_g, w_in_g)
    y, states = _retention_fwd(proj, cos, sin, ret_gn_g, MIX)
    y = _conv_fwd(proj, y, dw_g, conv_dw_b, conv_ln_g, conv_ln_b, pw_g, conv_pw_b)
    dh2, dy, dwo_p, dfg_p, loss_p = _out_proj_loss(hp, y, w_out_g, final_g2, target)

    dp_conv, dpw_p, dww_p, cvec_p = _conv_bwd(proj, dy, dw_g, conv_dw_b, conv_ln_g, conv_ln_b, pw_g, conv_pw_b)
    dp_ret, dgn_p = _retention_bwd(proj, cos, sin, ret_gn_g, states, dy)
    dwi_p = _w_in_grad(hn, dp_ret, dp_conv)
    dh, dlg_p = _h_grad(dp_ret, dp_conv, w_in_g, hp, dh2, ln_g)
    grad_x = dh[:SEQ][None]

    zrow = jnp.zeros((1, D - CW - RW), F32) if D > CW + RW else None
    def two(a, b):
        parts = [a, b] + ([zrow] if zrow is not None else [])
        return jnp.concatenate(parts, axis=1)
    small = jnp.concatenate([
        dh[R - N_META:],
        dlg_p, dfg_p,
        two(dgn_p, cvec_p[3:4]),
        two(cvec_p[1:2], cvec_p[2:3]),
        two(cvec_p[0:1], jnp.broadcast_to(loss_p, (1, CW))),
        jnp.zeros((SMALL_ROWS - N_META - 5, D), F32)], axis=0)

    r_wi, r_wo, r_pw, r_dww, r_small = _exchange_grads(
        [dwi_p, dwo_p, dpw_p.astype(BF16), dww_p, small], [1, 0, 0, 1, None])

    g_wi, d_wi, nm_wi, nv_wi = _sum_adamw("sum_adamw_w_in", r_wi, w_in[0], m_w_in[0], v_w_in[0], 256)
    g_wo, d_wo, nm_wo, nv_wo = _sum_adamw("sum_adamw_w_out", r_wo, w_out[0], m_w_out[0], v_w_out[0], 128)
    g_pw, d_pw, nm_pw, nv_pw = _sum_adamw("sum_adamw_pw", r_pw, conv_pw_w[0], m_conv_pw_w[0], v_conv_pw_w[0], 128)

    dcol = D // N_DEV
    sm = lambda r0, nr, c0, nc: lax.slice(r_small, (0, r0, c0), (N_DEV, r0 + nr, c0 + nc))
    meta_parts = lax.dynamic_slice(r_small, (0, 0, me * dcol), (N_DEV, N_META, dcol))
    small_parts = [meta_parts, sm(16, 1, 0, D), sm(18, 1, 0, RW), r_dww, sm(18, 1, RW, CW),
                   sm(19, 1, 0, CW), sm(19, 1, CW, CW), sm(20, 1, 0, CW), sm(17, 1, 0, D)]
    pad31 = lambda a: jnp.pad(a, ((0, HALO - CONV_K), (0, 0)))
    ws = [meta_tokens, ln_g, ret_gn_g, pad31(conv_dw_w[0]), conv_dw_b, conv_ln_g, conv_ln_b, conv_pw_b, final_g2]
    ms = [m_meta_tokens, m_ln_g, m_ret_gn_g, pad31(m_conv_dw_w[0]), m_conv_dw_b, m_conv_ln_g, m_conv_ln_b,
          m_conv_pw_b, m_final_g[None, :]]
    vs = [v_meta_tokens, v_ln_g, v_ret_gn_g, pad31(v_conv_dw_w[0]), v_conv_dw_b, v_conv_ln_g, v_conv_ln_b,
          v_conv_pw_b, v_final_g[None, :]]
    loss_parts = sm(20, 1, CW, 1)
    outs = _sum_adamw_small(small_parts, ws, ms, vs, loss_parts)
    loss = outs[-1][0, 0]
    quad = [outs[4 * a:4 * a + 4] for a in range(len(ws))]
    (q_meta, q_lng, q_gn, q_dww, q_dwb, q_clg, q_clb, q_pwb, q_fg) = quad
    q_dww = [t[:CONV_K][None] for t in q_dww]
    q_fg = [t[0] for t in q_fg]
    q_wi = [t[None] for t in (g_wi, d_wi, nm_wi, nv_wi)]
    q_wo = [t[None] for t in (g_wo, d_wo, nm_wo, nv_wo)]
    q_pw = [t[None] for t in (g_pw, d_pw, nm_pw, nv_pw)]

    per_w = [q_meta, q_lng, q_wi, q_gn, q_dww, q_dwb, q_clg, q_clb, q_pw, q_pwb, q_wo, q_fg]
    result = [loss, grad_x]
    for which in range(4):
        result += [q[which] for q in per_w]
    return tuple(result)
```

```python
import functools

import numpy as np
import jax
import jax.numpy as jnp
from jax import lax
from jax.experimental import pallas as pl
from jax.experimental.pallas import tpu as pltpu

N_META = 16
RET_HEADS = 4
CONV_K = 31
CHUNK = 128
ROPE_BASE = 10000.0
EPS = 1e-6
ADAM_LR = 0.001
ADAM_B1 = 0.9
ADAM_B2 = 0.999
ADAM_EPS = 1e-08
ADAM_WD = 0.01
ADAM_STEP = 10

N_DEV = 8
META_TILE = 256
HALO = 32
SMALL_ROWS = 32
VMEM_BYTES_V7X = 64 * 1024 * 1024

F32 = jnp.float32
BF16 = jnp.bfloat16
MESH = pl.DeviceIdType.MESH

NN = (((1,), (0,)), ((), ()))
NT = (((1,), (1,)), ((), ()))
TN = (((0,), (0,)), ((), ()))


def _dot(a, b, dims=NN):
    return lax.dot_general(a, b, dims, preferred_element_type=F32)


def _pick_tile(n, target, mult=16):
    best = None
    for t in range(mult, min(n, target) + 1, mult):
        if n % t == 0:
            best = t
    assert best is not None, (n, target)
    return best


def _params(sem=None, vmem_mb=None):
    kw = {}
    if sem is not None:
        kw["dimension_semantics"] = sem
    if vmem_mb is not None:
        kw["vmem_limit_bytes"] = min(vmem_mb * 1024 * 1024, VMEM_BYTES_V7X - 4 * 1024 * 1024)
    return pltpu.CompilerParams(**kw)


def _sigmoid(x):
    return jax.nn.sigmoid(x)


def _dsilu(x, sg):
    return sg * (1.0 + x * (1.0 - sg))


def _decay_tables(heads):
    h = np.arange(heads, dtype=np.float32)
    gamma = (1.0 - np.exp2(-5.0 - h)).astype(np.float32)
    log_g = np.log(gamma).astype(np.float32)
    idx = np.arange(CHUNK, dtype=np.float32)
    rel = idx[:, None] - idx[None, :]
    mask = np.where(rel[None] >= 0, np.exp(np.maximum(rel, 0.0)[None] * log_g[:, None, None]), 0.0)
    qd = np.exp((idx[None, :] + 1.0) * log_g[:, None])
    kd = np.exp((CHUNK - 1.0 - idx[None, :]) * log_g[:, None])
    cd = np.exp(CHUNK * log_g)
    return (mask.astype(np.float32), qd.astype(np.float32)[:, :, None], kd.astype(np.float32)[:, :, None],
            [float(c) for c in cd.astype(np.float32)])


def _gather_weights(shards, block_axes):
    n_arr = len(shards)
    out_shapes = []
    for s, ax in zip(shards, block_axes):
        shp = list(s.shape)
        shp[ax] *= N_DEV
        out_shapes.append(jax.ShapeDtypeStruct(tuple(shp), s.dtype))

    def body(*refs):
        ins, outs = refs[:n_arr], refs[n_arr:2 * n_arr]
        send_sems, recv_sems, local_sems = refs[2 * n_arr:]
        x, y, c = lax.axis_index("x"), lax.axis_index("y"), lax.axis_index("c")
        me, sibling = (x, y, c), (x, y, 1 - c)
        chips = [(1 - x, y), (x, 1 - y), (1 - x, 1 - y)]

        def block(a, dev):
            n = ins[a].shape[block_axes[a]]
            start = pl.multiple_of((4 * dev[0] + 2 * dev[1] + dev[2]) * n, n)
            idx = [slice(None)] * len(ins[a].shape)
            idx[block_axes[a]] = pl.ds(start, n)
            return outs[a].at[tuple(idx)]

        def copy(a, k, dev, to, src=None):
            return pltpu.make_async_remote_copy(
                src_ref=block(a, dev) if src is None else src, dst_ref=block(a, dev),
                send_sem=send_sems.at[a, k], recv_sem=recv_sems.at[a, k],
                device_id=to, device_id_type=MESH)

        mine = [pltpu.make_async_copy(ins[a], block(a, me), local_sems.at[a]) for a in range(n_arr)]
        for cp in mine:
            cp.start()
        first = []
        for a in range(n_arr):
            first.append(copy(a, 0, me, sibling, src=ins[a]))
            first += [copy(a, 1 + j, me, (*chip, c), src=ins[a]) for j, chip in enumerate(chips)]
        for cp in first:
            cp.start()
        passed = []
        for j, chip in enumerate(chips):
            for a in range(n_arr):
                copy(a, 1 + j, (*chip, c), me).wait_recv()
                fwd = copy(a, 4 + j, (*chip, c), sibling)
                fwd.start()
                passed.append(fwd)
        for a in range(n_arr):
            copy(a, 0, sibling, me).wait_recv()
            for j, chip in enumerate(chips):
                copy(a, 4 + j, (*chip, 1 - c), me).wait_recv()
        for cp in first + passed:
            cp.wait_send()
        for cp in mine:
            cp.wait()

    hbm = pl.BlockSpec(memory_space=pl.ANY)
    return pl.pallas_call(
        body, name="gather_weights",
        out_shape=out_shapes,
        in_specs=[hbm] * n_arr, out_specs=[hbm] * n_arr,
        scratch_shapes=[pltpu.SemaphoreType.DMA((n_arr, 7)), pltpu.SemaphoreType.DMA((n_arr, 7)),
                        pltpu.SemaphoreType.DMA((n_arr,))],
    )(*shards)


class _Exchange:
    def __init__(self, parts, block_axes):
        self.block_axes = list(block_axes)
        self.n = len(parts)
        self.out_shape = []
        for p, ax in zip(parts, block_axes):
            shp = list(p.shape)
            if ax is not None:
                assert shp[ax] % N_DEV == 0
                shp[ax] //= N_DEV
            self.out_shape.append(jax.ShapeDtypeStruct((N_DEV, *shp), p.dtype))
        self.scratch = [pltpu.SemaphoreType.DMA((self.n, N_DEV - 1)), pltpu.SemaphoreType.DMA((self.n, N_DEV - 1)),
                        pltpu.SemaphoreType.DMA((self.n,))]
        self.specs = [pl.BlockSpec(memory_space=pl.ANY)] * self.n

    def _copies(self, ins, outs, sems):
        send_sems, recv_sems, local_sems = sems
        x, y, c = lax.axis_index("x"), lax.axis_index("y"), lax.axis_index("c")
        me_idx = 4 * x + 2 * y + c

        def src_block(a, dev_idx):
            ax = self.block_axes[a]
            if ax is None:
                return ins[a]
            n = ins[a].shape[ax] // N_DEV
            idx = [slice(None)] * len(ins[a].shape)
            idx[ax] = pl.ds(pl.multiple_of(dev_idx * n, n), n)
            return ins[a].at[tuple(idx)]

        local = [pltpu.make_async_copy(src_block(a, me_idx), outs[a].at[me_idx], local_sems.at[a])
                 for a in range(self.n)]
        remote = []
        for m in range(1, N_DEV):
            px, py, pc = x ^ ((m >> 2) & 1), y ^ ((m >> 1) & 1), c ^ (m & 1)
            for a in range(self.n):
                remote.append(pltpu.make_async_remote_copy(
                    src_ref=src_block(a, 4 * px + 2 * py + pc), dst_ref=outs[a].at[me_idx],
                    send_sem=send_sems.at[a, m - 1], recv_sem=recv_sems.at[a, m - 1],
                    device_id=(px, py, pc), device_id_type=MESH))
        return local, remote

    def start(self, ins, outs, sems):
        local, remote = self._copies(ins, outs, sems)
        for cp in local + remote:
            cp.start()

    def wait(self, ins, outs, sems):
        local, remote = self._copies(ins, outs, sems)
        for cp in remote:
            cp.wait_recv()
        for cp in remote:
            cp.wait_send()
        for cp in local:
            cp.wait()


def _in_proj(h, ln_g, w_in, n_rows_out, row0, prev=None, ex=None, ex_parts=()):
    n, D = h.shape
    E = w_in.shape[1]
    tm = _pick_tile(n, 1024)
    assert row0 % tm == 0
    b0 = row0 // tm
    tn = _pick_tile(E, 1024, 128)
    n_i, n_j = n // tm, E // tn
    n_prev = 0 if prev is None else 2
    n_ex = 0 if ex is None else ex.n

    def body(*refs):
        h_ref, g_ref, w_ref = refs[:3]
        ex_ins = refs[3 + n_prev:3 + n_prev + n_ex]
        o = 3 + n_prev + n_ex
        proj_ref, hn_ref = refs[o], refs[o + 1]
        ex_outs, sems = refs[o + 2:o + 2 + n_ex], refs[o + 2 + n_ex:]
        i, j = pl.program_id(0), pl.program_id(1)
        if ex is not None:
            @pl.when((i == 0) & (j == 0))
            def _():
                ex.start(ex_ins, ex_outs, sems)

        @pl.when(j == 0)
        def _():
            hv = h_ref[...]
            r = lax.rsqrt(jnp.mean(hv * hv, axis=-1, keepdims=True) + EPS)
            hn_ref[...] = (hv * r * g_ref[...]).astype(BF16)

        proj_ref[...] = _dot(hn_ref[...], w_ref[...]).astype(BF16)
        if ex is not None:
            @pl.when((i == n_i - 1) & (j == n_j - 1))
            def _():
                ex.wait(ex_ins, ex_outs, sems)

    hbm = pl.BlockSpec(memory_space=pl.ANY)
    extra = {} if prev is None else dict(input_output_aliases={3: 0, 4: 1})
    outs = pl.pallas_call(
        body, name="in_proj" if prev is None else "in_proj_meta",
        grid=(n_i, n_j),
        in_specs=[pl.BlockSpec((tm, D), lambda i, j: (i, 0)),
                  pl.BlockSpec((1, D), lambda i, j: (0, 0)),
                  pl.BlockSpec((D, tn), lambda i, j: (0, j))] + [hbm] * n_prev + ([] if ex is None else ex.specs),
        out_specs=[pl.BlockSpec((tm, tn), lambda i, j: (b0 + i, j)),
                   pl.BlockSpec((tm, D), lambda i, j: (b0 + i, 0))] + ([] if ex is None else ex.specs),
        out_shape=[jax.ShapeDtypeStruct((n_rows_out, E), BF16), jax.ShapeDtypeStruct((n_rows_out, D), BF16)]
                  + ([] if ex is None else ex.out_shape),
        scratch_shapes=[] if ex is None else ex.scratch,
        compiler_params=_params(("arbitrary", "arbitrary"), 56),
        **extra,
    )(h, ln_g, w_in, *([] if prev is None else list(prev)), *ex_parts)
    return outs[0], outs[1], outs[2:]


def _rot(t, cos, sin, half):
    t1, t2 = t[:, :half], t[:, half:]
    return jnp.concatenate([t1 * cos - t2 * sin, t1 * sin + t2 * cos], axis=-1)


def _rot_inv(t, cos, sin, half):
    t1, t2 = t[:, :half], t[:, half:]
    return jnp.concatenate([t1 * cos + t2 * sin, t2 * cos - t1 * sin], axis=-1)


def _chunk_order(n_chunks):
    lead = META_TILE // CHUNK
    return lambda l: (l + n_chunks - lead) % n_chunks


def _retention_fwd(proj, cos, sin, gn_g, mix):
    R, E = proj.shape
    RW = gn_g.shape[1]
    H = RET_HEADS
    hd = RW // H
    half = hd // 2
    NC = R // CHUNK
    mask, qd, kd, cd = _decay_tables(H)
    scale = float(hd) ** -0.5
    phys = _chunk_order(NC)

    def body(p_ref, cos_ref, sin_ref, mask_ref, qd_ref, kd_ref, gn_ref, y_ref, st_ref, state):
        @pl.when(pl.program_id(0) == 0)
        def _():
            state[...] = jnp.zeros_like(state)

        cs, sn = cos_ref[...], sin_ref[...]
        for h in range(H):
            cols = slice(h * hd, (h + 1) * hd)
            q = p_ref[:, h * hd:(h + 1) * hd].astype(F32)
            k = p_ref[:, RW + h * hd:RW + (h + 1) * hd].astype(F32)
            v = p_ref[:, 2 * RW + h * hd:2 * RW + (h + 1) * hd]
            g = p_ref[:, 3 * RW + h * hd:3 * RW + (h + 1) * hd].astype(F32)
            qr = _rot(q, cs, sn, half)
            kr = _rot(k, cs, sn, half) * scale
            s = _dot(qr.astype(BF16), kr.astype(BF16), NT) * mask_ref[h]
            s_prev = state[h]
            s_prev_b = s_prev.astype(BF16)
            st_ref[0, h] = s_prev_b
            y_raw = _dot(s.astype(BF16), v) + _dot((qr * qd_ref[h]).astype(BF16), s_prev_b)
            state[h] = s_prev * cd[h] + _dot((kr * kd_ref[h]).astype(BF16), v, TN)
            mu = jnp.mean(y_raw, axis=-1, keepdims=True)
            yc = y_raw - mu
            var = jnp.mean(yc * yc, axis=-1, keepdims=True)
            out = yc * lax.rsqrt(var + EPS) * gn_ref[:, cols] * (g * _sigmoid(g))
            y_ref[:, cols] = out.astype(BF16)

    const3 = lambda l: (0, 0, 0)
    return pl.pallas_call(
        body, name="retention_fwd",
        grid=(NC,),
        in_specs=[pl.BlockSpec((CHUNK, 4 * RW), lambda l: (phys(l), 0)),
                  pl.BlockSpec((CHUNK, half), lambda l: (phys(l), 0)),
                  pl.BlockSpec((CHUNK, half), lambda l: (phys(l), 0)),
                  pl.BlockSpec((H, CHUNK, CHUNK), const3),
                  pl.BlockSpec((H, CHUNK, 1), const3),
                  pl.BlockSpec((H, CHUNK, 1), const3),
                  pl.BlockSpec((1, RW), lambda l: (0, 0))],
        out_specs=[pl.BlockSpec((CHUNK, RW), lambda l: (phys(l), 0)),
                   pl.BlockSpec((1, H, hd, hd), lambda l: (phys(l), 0, 0, 0))],
        out_shape=[jax.ShapeDtypeStruct((R, mix), BF16), jax.ShapeDtypeStruct((NC, H, hd, hd), BF16)],
        scratch_shapes=[pltpu.VMEM((H, hd, hd), F32)],
        compiler_params=_params(("arbitrary",), 32),
    )(proj, cos, sin, jnp.asarray(mask), jnp.asarray(qd), jnp.asarray(kd), gn_g)


CONV_ROWS = 32
CONV_LANES = 512


def _conv_order(n_tiles):
    return lambda l: (l + n_tiles - 1) % n_tiles


def _halo_block(n_tiles, tm):
    per = tm // HALO
    return lambda l: ((l + n_tiles - 2) % n_tiles) * per + per - 1


def _fill_shifted(src, dst):
    rows, width = dst.shape[1], dst.shape[2]
    step = _pick_tile(rows, 64, 8)
    for r in range(1, 8):
        for r0 in range(0, rows, step):
            for l0 in range(0, width, CONV_LANES):
                dst[r - 1, r0:r0 + step, l0:l0 + CONV_LANES] = src[r + r0:r + r0 + step, l0:l0 + CONV_LANES]


def _at_offset(src, shifted, off, r0, rows, lanes):
    r = off % 8
    a = off - r + r0
    if r == 0:
        return src[a:a + rows, lanes]
    return shifted[r - 1, a:a + rows, lanes]


def _fill_glu(first, a_ref, b_ref, ah_ref, bh_ref, u_ext, tm):
    uh = ah_ref[...].astype(F32) * _sigmoid(bh_ref[...].astype(F32))
    u_ext[0:HALO, :] = jnp.where(first, 0.0, uh)
    u_ext[HALO:HALO + tm, :] = a_ref[...].astype(F32) * _sigmoid(b_ref[...].astype(F32))


def _layer_norm(cv, lg_ref, lb_ref):
    mu = jnp.mean(cv, axis=-1, keepdims=True)
    cc = cv - mu
    rstd = lax.rsqrt(jnp.mean(cc * cc, axis=-1, keepdims=True) + EPS)
    xh = cc * rstd
    return xh, rstd, xh * lg_ref[...] + lb_ref[...]


def _conv_fwd(proj, y_in, dw_w, dw_b, ln_g, ln_b, pw_w, pw_b):
    R, E = proj.shape
    CW = pw_w.shape[0]
    tm = META_TILE
    NTL = R // tm
    phys = _conv_order(NTL)
    halo = _halo_block(NTL, tm)
    cb = (E - 3 * CW) // CW
    base = HALO - (CONV_K - 1)

    def body(a_ref, b_ref, g_ref, ah_ref, bh_ref, w_ref, wb_ref, lg_ref, lb_ref, pw_ref, pb_ref, yin_ref,
             y_ref, c_ref, u_ext, u_sh):
        _fill_glu(pl.program_id(0) == 0, a_ref, b_ref, ah_ref, bh_ref, u_ext, tm)
        _fill_shifted(u_ext, u_sh)
        for r0 in range(0, tm, CONV_ROWS):
            for l0 in range(0, CW, CONV_LANES):
                lanes = slice(l0, l0 + CONV_LANES)
                acc = None
                for k in range(CONV_K):
                    term = _at_offset(u_ext, u_sh, base + k, r0, CONV_ROWS, lanes) * w_ref[k:k + 1, lanes]
                    acc = term if acc is None else acc + term
                c_ref[r0:r0 + CONV_ROWS, lanes] = acc + wb_ref[:, lanes]
        _, _, ln = _layer_norm(c_ref[...], lg_ref, lb_ref)
        s = (ln * _sigmoid(ln)).astype(BF16)
        upw = _dot(s, pw_ref[...]) + pb_ref[...]
        g = g_ref[...].astype(F32)
        y_ref[...] = (upw * (g * _sigmoid(g))).astype(BF16)

    row = lambda l: (0, 0)
    return pl.pallas_call(
        body, name="conv_fwd",
        grid=(NTL,),
        in_specs=[pl.BlockSpec((tm, CW), lambda l: (phys(l), cb)),
                  pl.BlockSpec((tm, CW), lambda l: (phys(l), cb + 1)),
                  pl.BlockSpec((tm, CW), lambda l: (phys(l), cb + 2)),
                  pl.BlockSpec((HALO, CW), lambda l: (halo(l), cb)),
                  pl.BlockSpec((HALO, CW), lambda l: (halo(l), cb + 1)),
                  pl.BlockSpec((HALO, CW), row),
                  pl.BlockSpec((1, CW), row), pl.BlockSpec((1, CW), row), pl.BlockSpec((1, CW), row),
                  pl.BlockSpec((CW, CW), row),
                  pl.BlockSpec((1, CW), row),
                  pl.BlockSpec(memory_space=pl.ANY)],
        out_specs=[pl.BlockSpec((tm, CW), lambda l: (phys(l), 1)),
                   pl.BlockSpec((tm, CW), lambda l: (phys(l), 0))],
        out_shape=[jax.ShapeDtypeStruct(y_in.shape, BF16), jax.ShapeDtypeStruct((R, CW), F32)],
        input_output_aliases={11: 0},
        scratch_shapes=[pltpu.VMEM((HALO + tm, CW), F32), pltpu.VMEM((7, tm + HALO - 8, CW), F32)],
        compiler_params=_params(("arbitrary",), 48),
    )(proj, proj, proj, proj, proj, dw_w, dw_b, ln_g, ln_b, pw_w, pw_b, y_in)


def _out_proj_loss(xs, meta_tile, y, w_out, final_g, target):
    SEQ, D = xs.shape
    R, MIX = y.shape
    tm = META_TILE
    n_seq = SEQ // tm
    n_tiles = R // tm
    rows_out = _pick_tile(MIX, 256)

    def body(x_ref, mt_ref, y_ref, w_hbm, fg_ref, t_ref, dh2_ref, dy_ref, dwo_hbm, dfg_ref, loss_ref, w_scr, acc, stage, sem):
        i = pl.program_id(0)

        @pl.when(i == 0)
        def _():
            cp = pltpu.make_async_copy(w_hbm, w_scr, sem)
            cp.start()
            acc[...] = jnp.zeros_like(acc)
            dfg_ref[...] = jnp.zeros_like(dfg_ref)
            loss_ref[...] = jnp.zeros_like(loss_ref)
            cp.wait()

        yb = y_ref[...]
        h2 = jnp.where(i < n_seq, x_ref[...], mt_ref[...]) + _dot(yb, w_scr[...])
        r2 = lax.rsqrt(jnp.mean(h2 * h2, axis=-1, keepdims=True) + EPS)
        n = h2 * r2
        fg = fg_ref[...]
        err = jnp.where(i < n_seq, n * fg - t_ref[...], 0.0)
        loss_ref[...] += 0.5 * jnp.sum(jnp.mean(err * err, axis=-1, keepdims=True), axis=0, keepdims=True)
        dout = err * (1.0 / D)
        dfg_ref[...] += jnp.sum(dout * n, axis=0, keepdims=True)
        dn = dout * fg
        dh2 = r2 * (dn - n * jnp.mean(dn * n, axis=-1, keepdims=True))
        dh2_ref[...] = dh2
        dh2b = dh2.astype(BF16)
        dy_ref[...] = _dot(dh2b, w_scr[...], NT).astype(BF16)
        acc[...] += _dot(yb, dh2b, TN)

        @pl.when(i == n_tiles - 1)
        def _():
            for r in range(0, MIX, rows_out):
                stage[...] = acc[r:r + rows_out, :].astype(BF16)
                cp = pltpu.make_async_copy(stage, dwo_hbm.at[r:r + rows_out, :], sem)
                cp.start()
                cp.wait()

    row = lambda i: (0, 0)
    return pl.pallas_call(
        body, name="out_proj_loss",
        grid=(n_tiles,),
        in_specs=[pl.BlockSpec((tm, D), lambda i: (jnp.minimum(i, n_seq - 1), 0)),
                  pl.BlockSpec((tm, D), row),
                  pl.BlockSpec((tm, MIX), lambda i: (i, 0)),
                  pl.BlockSpec(memory_space=pl.ANY),
                  pl.BlockSpec((1, D), row),
                  pl.BlockSpec((tm, D), lambda i: (jnp.minimum(i, n_seq - 1), 0))],
        out_specs=[pl.BlockSpec((tm, D), lambda i: (i, 0)),
                   pl.BlockSpec((tm, MIX), lambda i: (i, 0)),
                   pl.BlockSpec(memory_space=pl.ANY),
                   pl.BlockSpec((1, D), row),
                   pl.BlockSpec((1, 1), row)],
        out_shape=[jax.ShapeDtypeStruct((R, D), F32), jax.ShapeDtypeStruct((R, MIX), BF16),
                   jax.ShapeDtypeStruct((MIX, D), BF16), jax.ShapeDtypeStruct((1, D), F32),
                   jax.ShapeDtypeStruct((1, 1), F32)],
        scratch_shapes=[pltpu.VMEM((MIX, D), BF16), pltpu.VMEM((MIX, D), F32), pltpu.VMEM((rows_out, D), BF16),
                        pltpu.SemaphoreType.DMA],
        compiler_params=_params(("arbitrary",), 60),
    )(xs, meta_tile, y, w_out, final_g, target)


def _conv_bwd(proj, conv_out, dy, dw_w, ln_g, ln_b, pw_w, pw_b):
    R, E = proj.shape
    CW = pw_w.shape[0]
    tm = META_TILE
    NTL = R // tm
    order = _conv_order(NTL)
    phys = lambda i: order(NTL - 1 - i)
    halo_l = _halo_block(NTL, tm)
    halo = lambda i: halo_l(NTL - 1 - i)
    cb = (E - 3 * CW) // CW
    base = HALO - (CONV_K - 1)

    def body(a_ref, b_ref, g_ref, ah_ref, bh_ref, c_ref, dy_ref, w_ref, lg_ref, lb_ref, pw_ref, pb_ref,
             dp_ref, dpw_ref, dww_ref, vec_ref, u_ext, u_sh, dc_ext, dc_sh, du_scr, dww_acc, dpw_acc):
        i = pl.program_id(0)

        @pl.when(i == 0)
        def _():
            dpw_acc[...] = jnp.zeros_like(dpw_acc)
            dww_ref[...] = jnp.zeros_like(dww_ref)
            vec_ref[...] = jnp.zeros_like(vec_ref)
            dww_acc[...] = jnp.zeros_like(dww_acc)
            dc_ext[tm:tm + HALO, :] = jnp.zeros((HALO, CW), F32)

        _fill_glu(i == NTL - 1, a_ref, b_ref, ah_ref, bh_ref, u_ext, tm)
        _fill_shifted(u_ext, u_sh)
        xh, rstd, ln = _layer_norm(c_ref[...], lg_ref, lb_ref)
        sg = _sigmoid(ln)
        sb = (ln * sg).astype(BF16)
        upw = _dot(sb, pw_ref[...]) + pb_ref[...]
        g = g_ref[...].astype(F32)
        sgg = _sigmoid(g)
        dyc = dy_ref[...].astype(F32)
        dp_ref[:, 2 * CW:3 * CW] = (dyc * upw * _dsilu(g, sgg)).astype(BF16)
        dupw = dyc * (g * sgg)
        dupw_b = dupw.astype(BF16)
        vec_ref[0:1, :] += jnp.sum(dupw, axis=0, keepdims=True)
        dpw_acc[...] += _dot(sb, dupw_b, TN)
        dln = _dot(dupw_b, pw_ref[...], NT) * _dsilu(ln, sg)
        vec_ref[1:2, :] += jnp.sum(dln * xh, axis=0, keepdims=True)
        vec_ref[2:3, :] += jnp.sum(dln, axis=0, keepdims=True)
        dxh = dln * lg_ref[...]
        dc = rstd * (dxh - jnp.mean(dxh, axis=-1, keepdims=True) - xh * jnp.mean(dxh * xh, axis=-1, keepdims=True))
        vec_ref[3:4, :] += jnp.sum(dc, axis=0, keepdims=True)
        dc_ext[0:tm, :] = dc
        _fill_shifted(dc_ext, dc_sh)

        for l0 in range(0, CW, CONV_LANES):
            lanes = slice(l0, l0 + CONV_LANES)
            for r0 in range(0, tm, CONV_ROWS):
                dcb = dc_ext[r0:r0 + CONV_ROWS, lanes]
                acc = None
                for k in range(CONV_K):
                    term = _at_offset(dc_ext, dc_sh, CONV_K - 1 - k, r0, CONV_ROWS, lanes) * w_ref[k:k + 1, lanes]
                    acc = term if acc is None else acc + term
                    prod = dcb * _at_offset(u_ext, u_sh, base + k, r0, CONV_ROWS, lanes)
                    part = prod[0:8]
                    for q in range(8, CONV_ROWS, 8):
                        part = part + prod[q:q + 8]
                    dww_acc[k, :, lanes] += part
                du_scr[r0:r0 + CONV_ROWS, lanes] = acc

        du = du_scr[...]
        a = a_ref[...].astype(F32)
        sgb = _sigmoid(b_ref[...].astype(F32))
        dp_ref[:, 0:CW] = (du * sgb).astype(BF16)
        dp_ref[:, CW:2 * CW] = (du * a * sgb * (1.0 - sgb)).astype(BF16)
        dc_ext[tm:tm + HALO, :] = dc_ext[0:HALO, :]

        @pl.when(i == NTL - 1)
        def _():
            for k in range(CONV_K):
                dww_ref[k:k + 1, :] = jnp.sum(dww_acc[k], axis=0, keepdims=True)
            dpw_ref[...] = dpw_acc[...].astype(BF16)

    row = lambda i: (0, 0)
    return pl.pallas_call(
        body, name="conv_bwd",
        grid=(NTL,),
        in_specs=[pl.BlockSpec((tm, CW), lambda i: (phys(i), cb)),
                  pl.BlockSpec((tm, CW), lambda i: (phys(i), cb + 1)),
                  pl.BlockSpec((tm, CW), lambda i: (phys(i), cb + 2)),
                  pl.BlockSpec((HALO, CW), lambda i: (halo(i), cb)),
                  pl.BlockSpec((HALO, CW), lambda i: (halo(i), cb + 1)),
                  pl.BlockSpec((tm, CW), lambda i: (phys(i), 0)),
                  pl.BlockSpec((tm, CW), lambda i: (phys(i), 1)),
                  pl.BlockSpec((HALO, CW), row),
                  pl.BlockSpec((1, CW), row), pl.BlockSpec((1, CW), row),
                  pl.BlockSpec((CW, CW), row),
                  pl.BlockSpec((1, CW), row)],
        out_specs=[pl.BlockSpec((tm, 3 * CW), lambda i: (phys(i), 0)),
                   pl.BlockSpec((CW, CW), row),
                   pl.BlockSpec((HALO, CW), row),
                   pl.BlockSpec((8, CW), row)],
        out_shape=[jax.ShapeDtypeStruct((R, 3 * CW), BF16), jax.ShapeDtypeStruct((CW, CW), BF16),
                   jax.ShapeDtypeStruct((HALO, CW), F32), jax.ShapeDtypeStruct((8, CW), F32)],
        scratch_shapes=[pltpu.VMEM((HALO + tm, CW), F32), pltpu.VMEM((7, tm + HALO - 8, CW), F32),
                        pltpu.VMEM((tm + HALO, CW), F32), pltpu.VMEM((7, tm + HALO - 8, CW), F32),
                        pltpu.VMEM((tm, CW), F32), pltpu.VMEM((CONV_K, 8, CW), F32), pltpu.VMEM((CW, CW), F32)],
        compiler_params=_params(("arbitrary",), 60),
    )(proj, proj, proj, proj, proj, conv_out, dy, dw_w, ln_g, ln_b, pw_w, pw_b)


def _retention_bwd(proj, cos, sin, gn_g, states, dy, ex, ex_parts):
    R, E = proj.shape
    RW = gn_g.shape[1]
    H = RET_HEADS
    hd = RW // H
    half = hd // 2
    NC = R // CHUNK
    mask, qd, kd, cd = _decay_tables(H)
    scale = float(hd) ** -0.5
    order = _chunk_order(NC)
    phys = lambda i: order(NC - 1 - i)

    def body(*refs):
        p_ref, cos_ref, sin_ref, mask_ref, qd_ref, kd_ref, gn_ref, st_ref, dy_ref = refs[:9]
        ex_ins = refs[9:9 + ex.n]
        dp_ref, dgn_ref = refs[9 + ex.n:11 + ex.n]
        ex_outs = refs[11 + ex.n:11 + 2 * ex.n]
        dstate = refs[11 + 2 * ex.n]
        sems = refs[12 + 2 * ex.n:]

        @pl.when(pl.program_id(0) == 0)
        def _():
            ex.start(ex_ins, ex_outs, sems)
            dstate[...] = jnp.zeros_like(dstate)
            dgn_ref[...] = jnp.zeros_like(dgn_ref)

        cs, sn = cos_ref[...], sin_ref[...]
        for h in range(H):
            cols = slice(h * hd, (h + 1) * hd)
            q = p_ref[:, h * hd:(h + 1) * hd].astype(F32)
            k = p_ref[:, RW + h * hd:RW + (h + 1) * hd].astype(F32)
            v = p_ref[:, 2 * RW + h * hd:2 * RW + (h + 1) * hd]
            g = p_ref[:, 3 * RW + h * hd:3 * RW + (h + 1) * hd].astype(F32)
            msk, qdh, kdh = mask_ref[h], qd_ref[h], kd_ref[h]
            qr = _rot(q, cs, sn, half)
            kr = _rot(k, cs, sn, half) * scale
            qb, kb = qr.astype(BF16), kr.astype(BF16)
            qdb, kdb = (qr * qdh).astype(BF16), (kr * kdh).astype(BF16)
            s_prev = st_ref[0, h]
            sb = (_dot(qb, kb, NT) * msk).astype(BF16)
            y_raw = _dot(sb, v) + _dot(qdb, s_prev)
            mu = jnp.mean(y_raw, axis=-1, keepdims=True)
            yc = y_raw - mu
            rstd = lax.rsqrt(jnp.mean(yc * yc, axis=-1, keepdims=True) + EPS)
            xh = yc * rstd
            gn = gn_ref[:, cols]
            sg = _sigmoid(g)
            dyh = dy_ref[:, cols].astype(F32)
            dg = dyh * (xh * gn) * _dsilu(g, sg)
            dyn = dyh * (g * sg)
            dgn_ref[:, cols] += jnp.sum(dyn * xh, axis=0, keepdims=True)
            dxh = dyn * gn
            dyr = rstd * (dxh - jnp.mean(dxh, axis=-1, keepdims=True)
                          - xh * jnp.mean(dxh * xh, axis=-1, keepdims=True))
            dyrb = dyr.astype(BF16)
            dst = dstate[h]
            dstb = dst.astype(BF16)
            dsb = (_dot(dyrb, v, NT) * msk).astype(BF16)
            dqr = _dot(dsb, kb) + _dot(dyrb, s_prev, NT) * qdh
            dkr = _dot(dsb, qb, TN) + _dot(v, dstb, NT) * kdh
            dv = _dot(sb, dyrb, TN) + _dot(kdb, dstb)
            dstate[h] = dst * cd[h] + _dot(qdb, dyrb, TN)
            dp_ref[:, h * hd:(h + 1) * hd] = _rot_inv(dqr, cs, sn, half).astype(BF16)
            dp_ref[:, RW + h * hd:RW + (h + 1) * hd] = (_rot_inv(dkr, cs, sn, half) * scale).astype(BF16)
            dp_ref[:, 2 * RW + h * hd:2 * RW + (h + 1) * hd] = dv.astype(BF16)
            dp_ref[:, 3 * RW + h * hd:3 * RW + (h + 1) * hd] = dg.astype(BF16)

        @pl.when(pl.program_id(0) == NC - 1)
        def _():
            ex.wait(ex_ins, ex_outs, sems)

    const3 = lambda i: (0, 0, 0)
    outs = pl.pallas_call(
        body, name="retention_bwd",
        grid=(NC,),
        in_specs=[pl.BlockSpec((CHUNK, 4 * RW), lambda i: (phys(i), 0)),
                  pl.BlockSpec((CHUNK, half), lambda i: (phys(i), 0)),
                  pl.BlockSpec((CHUNK, half), lambda i: (phys(i), 0)),
                  pl.BlockSpec((H, CHUNK, CHUNK), const3),
                  pl.BlockSpec((H, CHUNK, 1), const3),
                  pl.BlockSpec((H, CHUNK, 1), const3),
                  pl.BlockSpec((1, RW), lambda i: (0, 0)),
                  pl.BlockSpec((1, H, hd, hd), lambda i: (phys(i), 0, 0, 0)),
                  pl.BlockSpec((CHUNK, RW), lambda i: (phys(i), 0))] + ex.specs,
        out_specs=[pl.BlockSpec((CHUNK, 4 * RW), lambda i: (phys(i), 0)),
                   pl.BlockSpec((1, RW), lambda i: (0, 0))] + ex.specs,
        out_shape=[jax.ShapeDtypeStruct((R, 4 * RW), BF16), jax.ShapeDtypeStruct((1, RW), F32)] + ex.out_shape,
        scratch_shapes=[pltpu.VMEM((H, hd, hd), F32)] + ex.scratch,
        compiler_params=_params(("arbitrary",), 32),
    )(proj, cos, sin, jnp.asarray(mask), jnp.asarray(qd), jnp.asarray(kd), gn_g, states, dy, *ex_parts)
    return outs[0], outs[1], outs[2:]


def _dproj_specs(tk, tn, n_ret, tile_axis, col_axis):
    def ret_map(*ids):
        t, j = ids[tile_axis], ids[col_axis]
        return (jnp.where(j < n_ret, t, 0), jnp.minimum(j, n_ret - 1))

    def conv_map(*ids):
        t, j = ids[tile_axis], ids[col_axis]
        return (jnp.where(j >= n_ret, t, 0), jnp.maximum(j - n_ret, 0))

    return pl.BlockSpec((tk, tn), ret_map), pl.BlockSpec((tk, tn), conv_map)


def _w_in_grad(hn, dp_ret, dp_conv):
    R, D = hn.shape
    tn = _pick_tile(dp_conv.shape[1] // 3, 1024, 128)
    n_ret, n_conv = dp_ret.shape[1] // tn, dp_conv.shape[1] // tn
    E = dp_ret.shape[1] + dp_conv.shape[1]
    tk = _pick_tile(R, 1056)
    n_t = R // tk
    ret_spec, conv_spec = _dproj_specs(tk, tn, n_ret, 1, 0)

    def body(hn_ref, r_ref, c_ref, out_ref, acc):
        j, t = pl.program_id(0), pl.program_id(1)

        @pl.when(t == 0)
        def _():
            acc[...] = jnp.zeros_like(acc)

        @pl.when(j < n_ret)
        def _():
            acc[...] += _dot(hn_ref[...], r_ref[...], TN)

        @pl.when(j >= n_ret)
        def _():
            acc[...] += _dot(hn_ref[...], c_ref[...], TN)

        @pl.when(t == n_t - 1)
        def _():
            out_ref[...] = acc[...].astype(BF16)

    return pl.pallas_call(
        body, name="w_in_grad",
        grid=(n_ret + n_conv, n_t),
        in_specs=[pl.BlockSpec((tk, D), lambda j, t: (t, 0)), ret_spec, conv_spec],
        out_specs=pl.BlockSpec((D, tn), lambda j, t: (0, j)),
        out_shape=jax.ShapeDtypeStruct((D, E), BF16),
        scratch_shapes=[pltpu.VMEM((D, tn), F32)],
        compiler_params=_params(("arbitrary", "arbitrary"), 48),
    )(hn, dp_ret, dp_conv)


def _h_grad(name, dp_ret, dp_conv, w_in, h, dh2, ln_g, row0, dlg_init=None, ex=None, ex_parts=()):
    n, D = h.shape
    tn = _pick_tile(dp_conv.shape[1] // 3, 1024, 128)
    n_ret, n_conv = dp_ret.shape[1] // tn, dp_conv.shape[1] // tn
    n_k = n_ret + n_conv
    tm = _pick_tile(n, 512)
    assert row0 % tm == 0
    b0 = row0 // tm

    def ret_map(t, k):
        return (jnp.where(k < n_ret, b0 + t, b0), jnp.minimum(k, n_ret - 1))

    def conv_map(t, k):
        return (jnp.where(k >= n_ret, b0 + t, b0), jnp.maximum(k - n_ret, 0))

    n_ex = 0 if ex is None else ex.n
    n_t = n // tm

    def body(*refs):
        r_ref, c_ref, w_ref, h_ref, dh2_ref, g_ref, init_ref = refs[:7]
        ex_ins = refs[7:7 + n_ex]
        dh_ref, dlg_ref = refs[7 + n_ex:9 + n_ex]
        ex_outs = refs[9 + n_ex:9 + 2 * n_ex]
        acc = refs[9 + 2 * n_ex]
        sems = refs[10 + 2 * n_ex:]
        t, k = pl.program_id(0), pl.program_id(1)
        if ex is not None:
            @pl.when((k == 0) & (t == 0))
            def _():
                ex.start(ex_ins, ex_outs, sems)

        @pl.when(k == 0)
        def _():
            acc[...] = jnp.zeros_like(acc)

        @pl.when((k == 0) & (t == 0))
        def _():
            dlg_ref[...] = init_ref[...]

        @pl.when(k < n_ret)
        def _():
            acc[...] += _dot(r_ref[...], w_ref[...], NT)

        @pl.when(k >= n_ret)
        def _():
            acc[...] += _dot(c_ref[...], w_ref[...], NT)

        @pl.when(k == n_k - 1)
        def _():
            hv = h_ref[...]
            r = lax.rsqrt(jnp.mean(hv * hv, axis=-1, keepdims=True) + EPS)
            nrm = hv * r
            dhn = acc[...]
            dlg_ref[...] += jnp.sum(dhn * nrm, axis=0, keepdims=True)
            dn = dhn * g_ref[...]
            dh_ref[...] = dh2_ref[...] + r * (dn - nrm * jnp.mean(dn * nrm, axis=-1, keepdims=True))

        if ex is not None:
            @pl.when((k == n_k - 1) & (t == n_t - 1))
            def _():
                ex.wait(ex_ins, ex_outs, sems)

    row = lambda t, k: (0, 0)
    if dlg_init is None:
        dlg_init = jnp.zeros((1, D), F32)
    ex_specs, ex_shape, ex_scratch = ([], [], []) if ex is None else (ex.specs, ex.out_shape, ex.scratch)
    outs = pl.pallas_call(
        body, name=name,
        grid=(n_t, n_k),
        in_specs=[pl.BlockSpec((tm, tn), ret_map), pl.BlockSpec((tm, tn), conv_map),
                  pl.BlockSpec((D, tn), lambda t, k: (0, k)),
                  pl.BlockSpec((tm, D), lambda t, k: (t, 0)),
                  pl.BlockSpec((tm, D), lambda t, k: (b0 + t, 0)),
                  pl.BlockSpec((1, D), row), pl.BlockSpec((1, D), row)] + ex_specs,
        out_specs=[pl.BlockSpec((tm, D), lambda t, k: (t, 0)),
                   pl.BlockSpec((1, D), row)] + ex_specs,
        out_shape=[jax.ShapeDtypeStruct((n, D), F32), jax.ShapeDtypeStruct((1, D), F32)] + ex_shape,
        scratch_shapes=[pltpu.VMEM((tm, D), F32)] + ex_scratch,
        compiler_params=_params(("arbitrary", "arbitrary"), 56),
    )(dp_ret, dp_conv, w_in, h, dh2, ln_g, dlg_init, *ex_parts)
    return outs[0], outs[1], outs[2:]


def _adamw(w, g, m, v):
    m = ADAM_B1 * m + (1.0 - ADAM_B1) * g
    v = ADAM_B2 * v + (1.0 - ADAM_B2) * (g * g)
    m_hat = m / (1.0 - ADAM_B1 ** ADAM_STEP)
    v_hat = v / (1.0 - ADAM_B2 ** ADAM_STEP)
    delta = -ADAM_LR * (m_hat / (jnp.sqrt(v_hat) + ADAM_EPS) + ADAM_WD * w)
    return delta, m, v


def _sum_slots(ref):
    g = ref[0].astype(F32)
    for s in range(1, N_DEV):
        g = g + ref[s].astype(F32)
    return g


def _sum_adamw(name, parts, w, m, v, rows_target, ex=None, ex_parts=()):
    rows, cols = w.shape
    tr = _pick_tile(rows, rows_target, 8)
    n_ex = 0 if ex is None else ex.n
    n_steps = rows // tr

    def body(*refs):
        p_ref, w_ref, m_ref, v_ref = refs[:4]
        ex_ins = refs[4:4 + n_ex]
        g_ref, d_ref, nm_ref, nv_ref = refs[4 + n_ex:8 + n_ex]
        ex_outs, sems = refs[8 + n_ex:8 + 2 * n_ex], refs[8 + 2 * n_ex:]
        if ex is not None:
            @pl.when(pl.program_id(0) == 0)
            def _():
                ex.start(ex_ins, ex_outs, sems)

        g = _sum_slots(p_ref)
        d, nm, nv = _adamw(w_ref[...], g, m_ref[...], v_ref[...])
        g_ref[...] = g
        d_ref[...] = d
        nm_ref[...] = nm
        nv_ref[...] = nv
        if ex is not None:
            @pl.when(pl.program_id(0) == n_steps - 1)
            def _():
                ex.wait(ex_ins, ex_outs, sems)

    tile = pl.BlockSpec((tr, cols), lambda i: (i, 0))
    ex_specs, ex_shape, ex_scratch = ([], [], []) if ex is None else (ex.specs, ex.out_shape, ex.scratch)
    outs = pl.pallas_call(
        body, name=name,
        grid=(n_steps,),
        in_specs=[pl.BlockSpec((N_DEV, tr, cols), lambda i: (0, i, 0)), tile, tile, tile] + ex_specs,
        out_specs=[tile] * 4 + ex_specs,
        out_shape=[jax.ShapeDtypeStruct((rows, cols), F32)] * 4 + ex_shape,
        scratch_shapes=ex_scratch,
        compiler_params=_params(("arbitrary",), 40),
    )(parts, w, m, v, *ex_parts)
    return outs[:4], outs[4:]


def _sum_adamw_small(parts_list, w_list, m_list, v_list, loss_parts):
    n = len(w_list)

    def body(*refs):
        p_refs, w_refs, m_refs, v_refs = refs[:n], refs[n:2 * n], refs[2 * n:3 * n], refs[3 * n:4 * n]
        lp_ref = refs[4 * n]
        outs = refs[4 * n + 1:]
        for a in range(n):
            g = _sum_slots(p_refs[a])
            d, nm, nv = _adamw(w_refs[a][...], g, m_refs[a][...], v_refs[a][...])
            outs[4 * a][...] = g
            outs[4 * a + 1][...] = d
            outs[4 * a + 2][...] = nm
            outs[4 * a + 3][...] = nv
        outs[4 * n][...] = _sum_slots(lp_ref)

    out_shape = []
    for w in w_list:
        out_shape += [jax.ShapeDtypeStruct(w.shape, F32)] * 4
    out_shape.append(jax.ShapeDtypeStruct(loss_parts.shape[1:], F32))
    return pl.pallas_call(body, name="sum_adamw_small", out_shape=out_shape)(
        *parts_list, *w_list, *m_list, *v_list, loss_parts)


def kernel(x, meta_tokens, ln_g, w_in, ret_gn_g, conv_dw_w, conv_dw_b, conv_ln_g, conv_ln_b, conv_pw_w, conv_pw_b, w_out, final_g, loss_target, m_meta_tokens, m_ln_g, m_w_in, m_ret_gn_g, m_conv_dw_w, m_conv_dw_b, m_conv_ln_g, m_conv_ln_b, m_conv_pw_w, m_conv_pw_b, m_w_out, m_final_g, v_meta_tokens, v_ln_g, v_w_in, v_ret_gn_g, v_conv_dw_w, v_conv_dw_b, v_conv_ln_g, v_conv_ln_b, v_conv_pw_w, v_conv_pw_b, v_w_out, v_final_g):
    _, SEQ, D = x.shape
    MIX = w_out.shape[2]
    RW = ret_gn_g.shape[1]
    CW = conv_pw_b.shape[1]
    assert RW == CW and MIX == RW + CW and SEQ % META_TILE == 0 and CONV_K - 1 <= HALO
    R = SEQ + META_TILE
    hd = RW // RET_HEADS
    half = hd // 2
    me = 4 * lax.axis_index("x") + 2 * lax.axis_index("y") + lax.axis_index("c")

    dw_pad = jnp.pad(conv_dw_w[0], ((0, HALO - CONV_K), (0, 0)))
    w_in_g, dw_g, meta_g = _gather_weights([w_in[0].astype(BF16), dw_pad, meta_tokens], [1, 1, 1])

    pos = jnp.concatenate([jnp.arange(SEQ, dtype=F32) + N_META, jnp.zeros((META_TILE - N_META,), F32),
                           jnp.arange(N_META, dtype=F32)])
    inv_freq = ROPE_BASE ** (-jnp.arange(half, dtype=F32) / half)
    ang = pos[:, None] * inv_freq[None, :]
    cos, sin = jnp.cos(ang), jnp.sin(ang)

    xs = x[0]
    meta_tile = jnp.concatenate([jnp.zeros((META_TILE - N_META, D), F32), meta_g], axis=0)
    target = loss_target[0]
    final_g2 = final_g[None, :]

    row_shards = [w_out[0].astype(BF16), conv_pw_w[0].astype(BF16)]
    proj, hn, (w_out_g, pw_g) = _in_proj(xs, ln_g, w_in_g, R, 0, ex=_Exchange(row_shards, [None, None]),
                                         ex_parts=row_shards)
    w_out_g, pw_g = w_out_g.reshape(MIX, D), pw_g.reshape(CW, CW)
    proj, hn, _ = _in_proj(meta_tile, ln_g, w_in_g, R, SEQ, prev=(proj, hn))
    y, states = _retention_fwd(proj, cos, sin, ret_gn_g, MIX)
    y, conv_out = _conv_fwd(proj, y, dw_g, conv_dw_b, conv_ln_g, conv_ln_b, pw_g, conv_pw_b)
    dh2, dy, dwo_p, dfg_p, loss_p = _out_proj_loss(xs, meta_tile, y, w_out_g, final_g2, target)

    dp_conv, dpw_p, dww_p, cvec_p = _conv_bwd(proj, conv_out, dy, dw_g, conv_ln_g, conv_ln_b, pw_g, conv_pw_b)
    dp_ret, dgn_p, (r_wo, r_pw) = _retention_bwd(proj, cos, sin, ret_gn_g, states, dy,
                                                 _Exchange([dwo_p, dpw_p], [0, 0]), [dwo_p, dpw_p])
    dwi_p = _w_in_grad(hn, dp_ret, dp_conv)
    grad_xs, dlg_x, (r_wi,) = _h_grad("h_grad", dp_ret, dp_conv, w_in_g, xs, dh2, ln_g, 0,
                                      ex=_Exchange([dwi_p], [1]), ex_parts=[dwi_p])
    dh_meta, dlg_p, _ = _h_grad("h_grad_meta", dp_ret, dp_conv, w_in_g, meta_tile, dh2, ln_g, SEQ, dlg_init=dlg_x)
    grad_x = grad_xs[None]

    def at_row(r, a, b=None):
        v = a if b is None else jnp.concatenate([a, b], axis=1)
        return jnp.pad(v, ((r, 7 - r), (0, D - v.shape[1])))
    vec8 = (at_row(0, dlg_p) + at_row(1, dfg_p)
            + at_row(2, dgn_p, cvec_p[3:4])
            + at_row(3, cvec_p[1:2], cvec_p[2:3])
            + at_row(4, cvec_p[0:1], jnp.broadcast_to(loss_p, (1, CW))))
    small = jnp.concatenate([dh_meta[META_TILE - N_META:], vec8,
                             jnp.zeros((SMALL_ROWS - N_META - 8, D), F32)], axis=0)

    (g_wi, d_wi, nm_wi, nv_wi), (r_dww, r_small) = _sum_adamw(
        "sum_adamw_w_in", r_wi, w_in[0], m_w_in[0], v_w_in[0], 256,
        ex=_Exchange([dww_p, small], [1, None]), ex_parts=[dww_p, small])
    (g_wo, d_wo, nm_wo, nv_wo), _ = _sum_adamw("sum_adamw_w_out", r_wo, w_out[0], m_w_out[0], v_w_out[0], 128)
    (g_pw, d_pw, nm_pw, nv_pw), _ = _sum_adamw("sum_adamw_pw", r_pw, conv_pw_w[0], m_conv_pw_w[0], v_conv_pw_w[0], 128)

    dcol = D // N_DEV
    sm = lambda r0, nr, c0, nc: lax.slice(r_small, (0, r0, c0), (N_DEV, r0 + nr, c0 + nc))
    meta_parts = lax.dynamic_slice(r_small, (0, 0, me * dcol), (N_DEV, N_META, dcol))
    small_parts = [meta_parts, sm(16, 1, 0, D), sm(18, 1, 0, RW), r_dww, sm(18, 1, RW, CW),
                   sm(19, 1, 0, CW), sm(19, 1, CW, CW), sm(20, 1, 0, CW), sm(17, 1, 0, D)]
    pad31 = lambda a: jnp.pad(a, ((0, HALO - CONV_K), (0, 0)))
    ws = [meta_tokens, ln_g, ret_gn_g, pad31(conv_dw_w[0]), conv_dw_b, conv_ln_g, conv_ln_b, conv_pw_b, final_g2]
    ms = [m_meta_tokens, m_ln_g, m_ret_gn_g, pad31(m_conv_dw_w[0]), m_conv_dw_b, m_conv_ln_g, m_conv_ln_b,
          m_conv_pw_b, m_final_g[None, :]]
    vs = [v_meta_tokens, v_ln_g, v_ret_gn_g, pad31(v_conv_dw_w[0]), v_conv_dw_b, v_conv_ln_g, v_conv_ln_b,
          v_conv_pw_b, v_final_g[None, :]]
    loss_parts = sm(20, 1, CW, 1)
    outs = _sum_adamw_small(small_parts, ws, ms, vs, loss_parts)
    loss = outs[-1][0, 0]
    quad = [outs[4 * a:4 * a + 4] for a in range(len(ws))]
    (q_meta, q_lng, q_gn, q_dww, q_dwb, q_clg, q_clb, q_pwb, q_fg) = quad
    q_dww = [t[:CONV_K][None] for t in q_dww]
    q_fg = [t[0] for t in q_fg]
    q_wi = [t[None] for t in (g_wi, d_wi, nm_wi, nv_wi)]
    q_wo = [t[None] for t in (g_wo, d_wo, nm_wo, nv_wo)]
    q_pw = [t[None] for t in (g_pw, d_pw, nm_pw, nv_pw)]

    per_w = [q_meta, q_lng, q_wi, q_gn, q_dww, q_dwb, q_clg, q_clb, q_pw, q_pwb, q_wo, q_fg]
    result = [loss, grad_x]
    for which in range(4):
        result += [q[which] for q in per_w]
    return tuple(result)
```

```python
import functools

import numpy as np
import jax
import jax.numpy as jnp
from jax import lax
from jax.experimental import pallas as pl
from jax.experimental.pallas import tpu as pltpu

N_META = 16
RET_HEADS = 4
CONV_K = 31
CHUNK = 128
ROPE_BASE = 10000.0
EPS = 1e-6
ADAM_LR = 0.001
ADAM_B1 = 0.9
ADAM_B2 = 0.999
ADAM_EPS = 1e-08
ADAM_WD = 0.01
ADAM_STEP = 10

N_DEV = 8
META_TILE = 256
HALO = 32
SMALL_ROWS = 32
VMEM_BYTES_V7X = 64 * 1024 * 1024
MXU_DIM = 256

F32 = jnp.float32
BF16 = jnp.bfloat16
MESH = pl.DeviceIdType.MESH

NN = (((1,), (0,)), ((), ()))
NT = (((1,), (1,)), ((), ()))
TN = (((0,), (0,)), ((), ()))


def _dot(a, b, dims=NN):
    return lax.dot_general(a, b, dims, preferred_element_type=F32)


def _pick_tile(n, target, mult=16):
    best = None
    for t in range(mult, min(n, target) + 1, mult):
        if n % t == 0:
            best = t
    assert best is not None, (n, target)
    return best


def _params(sem=None, vmem_mb=None):
    kw = {}
    if sem is not None:
        kw["dimension_semantics"] = sem
    if vmem_mb is not None:
        kw["vmem_limit_bytes"] = min(vmem_mb * 1024 * 1024, VMEM_BYTES_V7X - 4 * 1024 * 1024)
    return pltpu.CompilerParams(**kw)


def _sigmoid(x):
    return jax.nn.sigmoid(x)


def _dsilu(x, sg):
    return sg * (1.0 + x * (1.0 - sg))


def _decay_tables(heads):
    h = np.arange(heads, dtype=np.float32)
    gamma = (1.0 - np.exp2(-5.0 - h)).astype(np.float32)
    log_g = np.log(gamma).astype(np.float32)
    idx = np.arange(CHUNK, dtype=np.float32)
    rel = idx[:, None] - idx[None, :]
    mask = np.where(rel[None] >= 0, np.exp(np.maximum(rel, 0.0)[None] * log_g[:, None, None]), 0.0)
    qd = np.exp((idx[None, :] + 1.0) * log_g[:, None])
    kd = np.exp((CHUNK - 1.0 - idx[None, :]) * log_g[:, None])
    cd = np.exp(CHUNK * log_g)
    return (mask.astype(np.float32), qd.astype(np.float32)[:, :, None], kd.astype(np.float32)[:, :, None],
            [float(c) for c in cd.astype(np.float32)])


def _gather_weights(shards, block_axes):
    n_arr = len(shards)
    out_shapes = []
    for s, ax in zip(shards, block_axes):
        shp = list(s.shape)
        shp[ax] *= N_DEV
        out_shapes.append(jax.ShapeDtypeStruct(tuple(shp), s.dtype))

    def body(*refs):
        ins, outs = refs[:n_arr], refs[n_arr:2 * n_arr]
        send_sems, recv_sems, local_sems = refs[2 * n_arr:]
        x, y, c = lax.axis_index("x"), lax.axis_index("y"), lax.axis_index("c")
        me, sibling = (x, y, c), (x, y, 1 - c)
        chips = [(1 - x, y), (x, 1 - y), (1 - x, 1 - y)]

        def block(a, dev):
            n = ins[a].shape[block_axes[a]]
            start = pl.multiple_of((4 * dev[0] + 2 * dev[1] + dev[2]) * n, n)
            idx = [slice(None)] * len(ins[a].shape)
            idx[block_axes[a]] = pl.ds(start, n)
            return outs[a].at[tuple(idx)]

        def copy(a, k, dev, to, src=None):
            return pltpu.make_async_remote_copy(
                src_ref=block(a, dev) if src is None else src, dst_ref=block(a, dev),
                send_sem=send_sems.at[a, k], recv_sem=recv_sems.at[a, k],
                device_id=to, device_id_type=MESH)

        mine = [pltpu.make_async_copy(ins[a], block(a, me), local_sems.at[a]) for a in range(n_arr)]
        for cp in mine:
            cp.start()
        first = []
        for a in range(n_arr):
            first.append(copy(a, 0, me, sibling, src=ins[a]))
            first += [copy(a, 1 + j, me, (*chip, c), src=ins[a]) for j, chip in enumerate(chips)]
        for cp in first:
            cp.start()
        passed = []
        for j, chip in enumerate(chips):
            for a in range(n_arr):
                copy(a, 1 + j, (*chip, c), me).wait_recv()
                fwd = copy(a, 4 + j, (*chip, c), sibling)
                fwd.start()
                passed.append(fwd)
        for a in range(n_arr):
            copy(a, 0, sibling, me).wait_recv()
            for j, chip in enumerate(chips):
                copy(a, 4 + j, (*chip, 1 - c), me).wait_recv()
        for cp in first + passed:
            cp.wait_send()
        for cp in mine:
            cp.wait()

    hbm = pl.BlockSpec(memory_space=pl.ANY)
    return pl.pallas_call(
        body, name="gather_weights",
        out_shape=out_shapes,
        in_specs=[hbm] * n_arr, out_specs=[hbm] * n_arr,
        scratch_shapes=[pltpu.SemaphoreType.DMA((n_arr, 7)), pltpu.SemaphoreType.DMA((n_arr, 7)),
                        pltpu.SemaphoreType.DMA((n_arr,))],
    )(*shards)


class _Exchange:
    def __init__(self, parts, block_axes):
        self.block_axes = list(block_axes)
        self.n = len(parts)
        self.out_shape = []
        for p, ax in zip(parts, block_axes):
            shp = list(p.shape)
            if ax is not None:
                assert shp[ax] % N_DEV == 0
                shp[ax] //= N_DEV
            self.out_shape.append(jax.ShapeDtypeStruct((N_DEV, *shp), p.dtype))
        self.scratch = [pltpu.SemaphoreType.DMA((self.n, N_DEV - 1)), pltpu.SemaphoreType.DMA((self.n, N_DEV - 1)),
                        pltpu.SemaphoreType.DMA((self.n,))]
        self.specs = [pl.BlockSpec(memory_space=pl.ANY)] * self.n

    def _copies(self, ins, outs, sems):
        send_sems, recv_sems, local_sems = sems
        x, y, c = lax.axis_index("x"), lax.axis_index("y"), lax.axis_index("c")
        me_idx = 4 * x + 2 * y + c

        def src_block(a, dev_idx):
            ax = self.block_axes[a]
            if ax is None:
                return ins[a]
            n = ins[a].shape[ax] // N_DEV
            idx = [slice(None)] * len(ins[a].shape)
            idx[ax] = pl.ds(pl.multiple_of(dev_idx * n, n), n)
            return ins[a].at[tuple(idx)]

        local = [pltpu.make_async_copy(src_block(a, me_idx), outs[a].at[me_idx], local_sems.at[a])
                 for a in range(self.n)]
        remote = []
        for m in range(1, N_DEV):
            px, py, pc = x ^ ((m >> 2) & 1), y ^ ((m >> 1) & 1), c ^ (m & 1)
            for a in range(self.n):
                remote.append(pltpu.make_async_remote_copy(
                    src_ref=src_block(a, 4 * px + 2 * py + pc), dst_ref=outs[a].at[me_idx],
                    send_sem=send_sems.at[a, m - 1], recv_sem=recv_sems.at[a, m - 1],
                    device_id=(px, py, pc), device_id_type=MESH))
        return local, remote

    def start(self, ins, outs, sems):
        local, remote = self._copies(ins, outs, sems)
        for cp in local + remote:
            cp.start()

    def wait(self, ins, outs, sems):
        local, remote = self._copies(ins, outs, sems)
        for cp in remote:
            cp.wait_recv()
        for cp in remote:
            cp.wait_send()
        for cp in local:
            cp.wait()


def _in_proj(h, ln_g, w_in, n_rows_out, row0, prev=None, ex=None, ex_parts=()):
    n, D = h.shape
    E = w_in.shape[1]
    tm = _pick_tile(n, 1024)
    assert row0 % tm == 0
    b0 = row0 // tm
    tn = _pick_tile(E, 1024, 128)
    n_i, n_j = n // tm, E // tn
    n_prev = 0 if prev is None else 2
    n_ex = 0 if ex is None else ex.n

    def body(*refs):
        h_ref, g_ref, w_ref = refs[:3]
        ex_ins = refs[3 + n_prev:3 + n_prev + n_ex]
        o = 3 + n_prev + n_ex
        proj_ref, hn_ref = refs[o], refs[o + 1]
        ex_outs, sems = refs[o + 2:o + 2 + n_ex], refs[o + 2 + n_ex:]
        i, j = pl.program_id(0), pl.program_id(1)
        if ex is not None:
            @pl.when((i == 0) & (j == 0))
            def _():
                ex.start(ex_ins, ex_outs, sems)

        @pl.when(j == 0)
        def _():
            hv = h_ref[...]
            r = lax.rsqrt(jnp.mean(hv * hv, axis=-1, keepdims=True) + EPS)
            hn_ref[...] = (hv * r * g_ref[...]).astype(BF16)

        proj_ref[...] = _dot(hn_ref[...], w_ref[...]).astype(BF16)
        if ex is not None:
            @pl.when((i == n_i - 1) & (j == n_j - 1))
            def _():
                ex.wait(ex_ins, ex_outs, sems)

    hbm = pl.BlockSpec(memory_space=pl.ANY)
    extra = {} if prev is None else dict(input_output_aliases={3: 0, 4: 1})
    outs = pl.pallas_call(
        body, name="in_proj" if prev is None else "in_proj_meta",
        grid=(n_i, n_j),
        in_specs=[pl.BlockSpec((tm, D), lambda i, j: (i, 0)),
                  pl.BlockSpec((1, D), lambda i, j: (0, 0)),
                  pl.BlockSpec((D, tn), lambda i, j: (0, j))] + [hbm] * n_prev + ([] if ex is None else ex.specs),
        out_specs=[pl.BlockSpec((tm, tn), lambda i, j: (b0 + i, j)),
                   pl.BlockSpec((tm, D), lambda i, j: (b0 + i, 0))] + ([] if ex is None else ex.specs),
        out_shape=[jax.ShapeDtypeStruct((n_rows_out, E), BF16), jax.ShapeDtypeStruct((n_rows_out, D), BF16)]
                  + ([] if ex is None else ex.out_shape),
        scratch_shapes=[] if ex is None else ex.scratch,
        compiler_params=_params(("arbitrary", "arbitrary"), 56),
        **extra,
    )(h, ln_g, w_in, *([] if prev is None else list(prev)), *ex_parts)
    return outs[0], outs[1], outs[2:]


def _rot(t, cos, sin, half):
    t1, t2 = t[:, :half], t[:, half:]
    return jnp.concatenate([t1 * cos - t2 * sin, t1 * sin + t2 * cos], axis=-1)


def _rot_inv(t, cos, sin, half):
    t1, t2 = t[:, :half], t[:, half:]
    return jnp.concatenate([t1 * cos + t2 * sin, t2 * cos - t1 * sin], axis=-1)


def _chunk_order(n_chunks):
    lead = META_TILE // CHUNK
    return lambda l: (l + n_chunks - lead) % n_chunks


def _retention_fwd(proj, cos, sin, gn_g, mix):
    R, E = proj.shape
    RW = gn_g.shape[1]
    H = RET_HEADS
    hd = RW // H
    half = hd // 2
    NC = R // CHUNK
    mask, qd, kd, cd = _decay_tables(H)
    scale = float(hd) ** -0.5
    phys = _chunk_order(NC)

    def body(p_ref, cos_ref, sin_ref, mask_ref, qd_ref, kd_ref, gn_ref, y_ref, st_ref, state):
        @pl.when(pl.program_id(0) == 0)
        def _():
            state[...] = jnp.zeros_like(state)

        cs, sn = cos_ref[...], sin_ref[...]
        hs = range(H)
        col = lambda j, h: slice(j * RW + h * hd, j * RW + (h + 1) * hd)
        qr = [_rot(p_ref[:, col(0, h)].astype(F32), cs, sn, half) for h in hs]
        kr = [_rot(p_ref[:, col(1, h)].astype(F32), cs, sn, half) * scale for h in hs]
        v = [p_ref[:, col(2, h)] for h in hs]
        s_prev = [state[h] for h in hs]
        s_prev_b = [s_prev[h].astype(BF16) for h in hs]
        s = [(_dot(qr[h].astype(BF16), kr[h].astype(BF16), NT) * mask_ref[h]).astype(BF16) for h in hs]
        y_raw = [_dot(s[h], v[h]) + _dot((qr[h] * qd_ref[h]).astype(BF16), s_prev_b[h]) for h in hs]
        s_new = [s_prev[h] * cd[h] + _dot((kr[h] * kd_ref[h]).astype(BF16), v[h], TN) for h in hs]
        for h in hs:
            st_ref[0, h] = s_prev_b[h]
            state[h] = s_new[h]
        for h in hs:
            g = p_ref[:, col(3, h)].astype(F32)
            mu = jnp.mean(y_raw[h], axis=-1, keepdims=True)
            yc = y_raw[h] - mu
            var = jnp.mean(yc * yc, axis=-1, keepdims=True)
            out = yc * lax.rsqrt(var + EPS) * gn_ref[:, col(0, h)] * (g * _sigmoid(g))
            y_ref[:, col(0, h)] = out.astype(BF16)

    const3 = lambda l: (0, 0, 0)
    return pl.pallas_call(
        body, name="retention_fwd",
        grid=(NC,),
        in_specs=[pl.BlockSpec((CHUNK, 4 * RW), lambda l: (phys(l), 0)),
                  pl.BlockSpec((CHUNK, half), lambda l: (phys(l), 0)),
                  pl.BlockSpec((CHUNK, half), lambda l: (phys(l), 0)),
                  pl.BlockSpec((H, CHUNK, CHUNK), const3),
                  pl.BlockSpec((H, CHUNK, 1), const3),
                  pl.BlockSpec((H, CHUNK, 1), const3),
                  pl.BlockSpec((1, RW), lambda l: (0, 0))],
        out_specs=[pl.BlockSpec((CHUNK, RW), lambda l: (phys(l), 0)),
                   pl.BlockSpec((1, H, hd, hd), lambda l: (phys(l), 0, 0, 0))],
        out_shape=[jax.ShapeDtypeStruct((R, mix), BF16), jax.ShapeDtypeStruct((NC, H, hd, hd), BF16)],
        scratch_shapes=[pltpu.VMEM((H, hd, hd), F32)],
        compiler_params=_params(("arbitrary",), 32),
    )(proj, cos, sin, jnp.asarray(mask), jnp.asarray(qd), jnp.asarray(kd), gn_g)


CONV_ROWS = 32
CONV_LANES = 512
DW_LANES = 256


def _conv_order(n_tiles):
    return lambda l: (l + n_tiles - 1) % n_tiles


def _halo_block(n_tiles, tm):
    per = tm // HALO
    return lambda l: ((l + n_tiles - 2) % n_tiles) * per + per - 1


def _fill_shifted(src, dst):
    rows, width = dst.shape[1], dst.shape[2]
    step = _pick_tile(rows, 64, 8)
    for r in range(1, 8):
        for r0 in range(0, rows, step):
            for l0 in range(0, width, CONV_LANES):
                dst[r - 1, r0:r0 + step, l0:l0 + CONV_LANES] = src[r + r0:r + r0 + step, l0:l0 + CONV_LANES]


def _at_offset(src, shifted, off, r0, rows, lanes):
    r = off % 8
    a = off - r + r0
    if r == 0:
        return src[a:a + rows, lanes]
    return shifted[r - 1, a:a + rows, lanes]


def _fill_glu(first, a_ref, b_ref, ah_ref, bh_ref, u_ext, tm):
    uh = ah_ref[...].astype(F32) * _sigmoid(bh_ref[...].astype(F32))
    u_ext[0:HALO, :] = jnp.where(first, 0.0, uh)
    u_ext[HALO:HALO + tm, :] = a_ref[...].astype(F32) * _sigmoid(b_ref[...].astype(F32))


def _layer_norm(cv, lg_ref, lb_ref):
    mu = jnp.mean(cv, axis=-1, keepdims=True)
    cc = cv - mu
    rstd = lax.rsqrt(jnp.mean(cc * cc, axis=-1, keepdims=True) + EPS)
    xh = cc * rstd
    return xh, rstd, xh * lg_ref[...] + lb_ref[...]


def _conv_fwd(proj, y_in, dw_w, dw_b, ln_g, ln_b, pw_w, pw_b):
    R, E = proj.shape
    CW = pw_w.shape[0]
    tm = META_TILE
    NTL = R // tm
    phys = _conv_order(NTL)
    halo = _halo_block(NTL, tm)
    cb = (E - 3 * CW) // CW
    base = HALO - (CONV_K - 1)

    def body(a_ref, b_ref, g_ref, ah_ref, bh_ref, w_ref, wb_ref, lg_ref, lb_ref, pw_ref, pb_ref, yin_ref,
             y_ref, c_ref, u_ext, u_sh):
        _fill_glu(pl.program_id(0) == 0, a_ref, b_ref, ah_ref, bh_ref, u_ext, tm)
        _fill_shifted(u_ext, u_sh)
        for r0 in range(0, tm, CONV_ROWS):
            for l0 in range(0, CW, CONV_LANES):
                lanes = slice(l0, l0 + CONV_LANES)
                acc = None
                for k in range(CONV_K):
                    term = _at_offset(u_ext, u_sh, base + k, r0, CONV_ROWS, lanes) * w_ref[k:k + 1, lanes]
                    acc = term if acc is None else acc + term
                c_ref[r0:r0 + CONV_ROWS, lanes] = acc + wb_ref[:, lanes]
        _, _, ln = _layer_norm(c_ref[...], lg_ref, lb_ref)
        s = (ln * _sigmoid(ln)).astype(BF16)
        upw = _dot(s, pw_ref[...]) + pb_ref[...]
        g = g_ref[...].astype(F32)
        y_ref[...] = (upw * (g * _sigmoid(g))).astype(BF16)

    row = lambda l: (0, 0)
    return pl.pallas_call(
        body, name="conv_fwd",
        grid=(NTL,),
        in_specs=[pl.BlockSpec((tm, CW), lambda l: (phys(l), cb)),
                  pl.BlockSpec((tm, CW), lambda l: (phys(l), cb + 1)),
                  pl.BlockSpec((tm, CW), lambda l: (phys(l), cb + 2)),
                  pl.BlockSpec((HALO, CW), lambda l: (halo(l), cb)),
                  pl.BlockSpec((HALO, CW), lambda l: (halo(l), cb + 1)),
                  pl.BlockSpec((HALO, CW), row),
                  pl.BlockSpec((1, CW), row), pl.BlockSpec((1, CW), row), pl.BlockSpec((1, CW), row),
                  pl.BlockSpec((CW, CW), row),
                  pl.BlockSpec((1, CW), row),
                  pl.BlockSpec(memory_space=pl.ANY)],
        out_specs=[pl.BlockSpec((tm, CW), lambda l: (phys(l), 1)),
                   pl.BlockSpec((tm, CW), lambda l: (phys(l), 0))],
        out_shape=[jax.ShapeDtypeStruct(y_in.shape, BF16), jax.ShapeDtypeStruct((R, CW), F32)],
        input_output_aliases={11: 0},
        scratch_shapes=[pltpu.VMEM((HALO + tm, CW), F32), pltpu.VMEM((7, tm + HALO - 8, CW), F32)],
        compiler_params=_params(("arbitrary",), 48),
    )(proj, proj, proj, proj, proj, dw_w, dw_b, ln_g, ln_b, pw_w, pw_b, y_in)


def _out_proj_loss(xs, meta_tile, y, w_out, final_g, target):
    SEQ, D = xs.shape
    R, MIX = y.shape
    tm = META_TILE
    n_seq = SEQ // tm
    n_tiles = R // tm
    rows_out = _pick_tile(MIX, 256)

    def body(x_ref, mt_ref, y_ref, w_hbm, fg_ref, t_ref, dh2_ref, dy_ref, dwo_hbm, dfg_ref, loss_ref, w_scr, acc, stage, sem):
        i = pl.program_id(0)

        @pl.when(i == 0)
        def _():
            cp = pltpu.make_async_copy(w_hbm, w_scr, sem)
            cp.start()
            acc[...] = jnp.zeros_like(acc)
            dfg_ref[...] = jnp.zeros_like(dfg_ref)
            loss_ref[...] = jnp.zeros_like(loss_ref)
            cp.wait()

        yb = y_ref[...]
        h2 = jnp.where(i < n_seq, x_ref[...], mt_ref[...]) + _dot(yb, w_scr[...])
        r2 = lax.rsqrt(jnp.mean(h2 * h2, axis=-1, keepdims=True) + EPS)
        n = h2 * r2
        fg = fg_ref[...]
        err = jnp.where(i < n_seq, n * fg - t_ref[...], 0.0)
        loss_ref[...] += 0.5 * jnp.sum(jnp.mean(err * err, axis=-1, keepdims=True), axis=0, keepdims=True)
        dout = err * (1.0 / D)
        dfg_ref[...] += jnp.sum(dout * n, axis=0, keepdims=True)
        dn = dout * fg
        dh2 = r2 * (dn - n * jnp.mean(dn * n, axis=-1, keepdims=True))
        dh2_ref[...] = dh2
        dh2b = dh2.astype(BF16)
        dy_ref[...] = _dot(dh2b, w_scr[...], NT).astype(BF16)
        acc[...] += _dot(yb, dh2b, TN)

        @pl.when(i == n_tiles - 1)
        def _():
            for r in range(0, MIX, rows_out):
                stage[...] = acc[r:r + rows_out, :].astype(BF16)
                cp = pltpu.make_async_copy(stage, dwo_hbm.at[r:r + rows_out, :], sem)
                cp.start()
                cp.wait()

    row = lambda i: (0, 0)
    return pl.pallas_call(
        body, name="out_proj_loss",
        grid=(n_tiles,),
        in_specs=[pl.BlockSpec((tm, D), lambda i: (jnp.minimum(i, n_seq - 1), 0)),
                  pl.BlockSpec((tm, D), row),
                  pl.BlockSpec((tm, MIX), lambda i: (i, 0)),
                  pl.BlockSpec(memory_space=pl.ANY),
                  pl.BlockSpec((1, D), row),
                  pl.BlockSpec((tm, D), lambda i: (jnp.minimum(i, n_seq - 1), 0))],
        out_specs=[pl.BlockSpec((tm, D), lambda i: (i, 0)),
                   pl.BlockSpec((tm, MIX), lambda i: (i, 0)),
                   pl.BlockSpec(memory_space=pl.ANY),
                   pl.BlockSpec((1, D), row),
                   pl.BlockSpec((1, 1), row)],
        out_shape=[jax.ShapeDtypeStruct((R, D), F32), jax.ShapeDtypeStruct((R, MIX), BF16),
                   jax.ShapeDtypeStruct((MIX, D), BF16), jax.ShapeDtypeStruct((1, D), F32),
                   jax.ShapeDtypeStruct((1, 1), F32)],
        scratch_shapes=[pltpu.VMEM((MIX, D), BF16), pltpu.VMEM((MIX, D), F32), pltpu.VMEM((rows_out, D), BF16),
                        pltpu.SemaphoreType.DMA],
        compiler_params=_params(("arbitrary",), 60),
    )(xs, meta_tile, y, w_out, final_g, target)


def _conv_bwd(proj, conv_out, dy, dw_w, ln_g, ln_b, pw_w, pw_b):
    R, E = proj.shape
    CW = pw_w.shape[0]
    tm = META_TILE
    NTL = R // tm
    order = _conv_order(NTL)
    phys = lambda i: order(NTL - 1 - i)
    halo_l = _halo_block(NTL, tm)
    halo = lambda i: halo_l(NTL - 1 - i)
    cb = (E - 3 * CW) // CW
    base = HALO - (CONV_K - 1)

    def body(a_ref, b_ref, g_ref, ah_ref, bh_ref, c_ref, dy_ref, w_ref, lg_ref, lb_ref, pw_ref, pb_ref,
             dp_ref, dpw_ref, dww_ref, vec_ref, u_ext, u_sh, dc_ext, dc_sh, du_scr, dww_acc, dpw_acc):
        i = pl.program_id(0)

        @pl.when(i == 0)
        def _():
            dpw_acc[...] = jnp.zeros_like(dpw_acc)
            dww_ref[...] = jnp.zeros_like(dww_ref)
            vec_ref[...] = jnp.zeros_like(vec_ref)
            dww_acc[...] = jnp.zeros_like(dww_acc)
            dc_ext[tm:tm + HALO, :] = jnp.zeros((HALO, CW), F32)

        _fill_glu(i == NTL - 1, a_ref, b_ref, ah_ref, bh_ref, u_ext, tm)
        _fill_shifted(u_ext, u_sh)
        xh, rstd, ln = _layer_norm(c_ref[...], lg_ref, lb_ref)
        sg = _sigmoid(ln)
        sb = (ln * sg).astype(BF16)
        upw = _dot(sb, pw_ref[...]) + pb_ref[...]
        g = g_ref[...].astype(F32)
        sgg = _sigmoid(g)
        dyc = dy_ref[...].astype(F32)
        dp_ref[:, 2 * CW:3 * CW] = (dyc * upw * _dsilu(g, sgg)).astype(BF16)
        dupw = dyc * (g * sgg)
        dupw_b = dupw.astype(BF16)
        vec_ref[0:1, :] += jnp.sum(dupw, axis=0, keepdims=True)
        dpw_acc[...] += _dot(sb, dupw_b, TN)
        dln = _dot(dupw_b, pw_ref[...], NT) * _dsilu(ln, sg)
        vec_ref[1:2, :] += jnp.sum(dln * xh, axis=0, keepdims=True)
        vec_ref[2:3, :] += jnp.sum(dln, axis=0, keepdims=True)
        dxh = dln * lg_ref[...]
        dc = rstd * (dxh - jnp.mean(dxh, axis=-1, keepdims=True) - xh * jnp.mean(dxh * xh, axis=-1, keepdims=True))
        vec_ref[3:4, :] += jnp.sum(dc, axis=0, keepdims=True)
        dc_ext[0:tm, :] = dc
        _fill_shifted(dc_ext, dc_sh)

        for l0 in range(0, CW, CONV_LANES):
            lanes = slice(l0, l0 + CONV_LANES)
            for r0 in range(0, tm, CONV_ROWS):
                acc = None
                for k in range(CONV_K):
                    term = _at_offset(dc_ext, dc_sh, CONV_K - 1 - k, r0, CONV_ROWS, lanes) * w_ref[k:k + 1, lanes]
                    acc = term if acc is None else acc + term
                du_scr[r0:r0 + CONV_ROWS, lanes] = acc
        for l0 in range(0, CW, DW_LANES):
            lanes = slice(l0, l0 + DW_LANES)
            for r0 in range(0, tm, CONV_ROWS):
                dcb = dc_ext[r0:r0 + CONV_ROWS, lanes]
                for k in range(CONV_K):
                    prod = dcb * _at_offset(u_ext, u_sh, base + k, r0, CONV_ROWS, lanes)
                    part = prod[0:8]
                    for q in range(8, CONV_ROWS, 8):
                        part = part + prod[q:q + 8]
                    dww_acc[k, :, lanes] += part

        du = du_scr[...]
        a = a_ref[...].astype(F32)
        sgb = _sigmoid(b_ref[...].astype(F32))
        dp_ref[:, 0:CW] = (du * sgb).astype(BF16)
        dp_ref[:, CW:2 * CW] = (du * a * sgb * (1.0 - sgb)).astype(BF16)
        dc_ext[tm:tm + HALO, :] = dc_ext[0:HALO, :]

        @pl.when(i == NTL - 1)
        def _():
            for k in range(CONV_K):
                dww_ref[k:k + 1, :] = jnp.sum(dww_acc[k], axis=0, keepdims=True)
            dpw_ref[...] = dpw_acc[...].astype(BF16)

    row = lambda i: (0, 0)
    return pl.pallas_call(
        body, name="conv_bwd",
        grid=(NTL,),
        in_specs=[pl.BlockSpec((tm, CW), lambda i: (phys(i), cb)),
                  pl.BlockSpec((tm, CW), lambda i: (phys(i), cb + 1)),
                  pl.BlockSpec((tm, CW), lambda i: (phys(i), cb + 2)),
                  pl.BlockSpec((HALO, CW), lambda i: (halo(i), cb)),
                  pl.BlockSpec((HALO, CW), lambda i: (halo(i), cb + 1)),
                  pl.BlockSpec((tm, CW), lambda i: (phys(i), 0)),
                  pl.BlockSpec((tm, CW), lambda i: (phys(i), 1)),
                  pl.BlockSpec((HALO, CW), row),
                  pl.BlockSpec((1, CW), row), pl.BlockSpec((1, CW), row),
                  pl.BlockSpec((CW, CW), row),
                  pl.BlockSpec((1, CW), row)],
        out_specs=[pl.BlockSpec((tm, 3 * CW), lambda i: (phys(i), 0)),
                   pl.BlockSpec((CW, CW), row),
                   pl.BlockSpec((HALO, CW), row),
                   pl.BlockSpec((8, CW), row)],
        out_shape=[jax.ShapeDtypeStruct((R, 3 * CW), BF16), jax.ShapeDtypeStruct((CW, CW), BF16),
                   jax.ShapeDtypeStruct((HALO, CW), F32), jax.ShapeDtypeStruct((8, CW), F32)],
        scratch_shapes=[pltpu.VMEM((HALO + tm, CW), F32), pltpu.VMEM((7, tm + HALO - 8, CW), F32),
                        pltpu.VMEM((tm + HALO, CW), F32), pltpu.VMEM((7, tm + HALO - 8, CW), F32),
                        pltpu.VMEM((tm, CW), F32), pltpu.VMEM((CONV_K, 8, CW), F32), pltpu.VMEM((CW, CW), F32)],
        compiler_params=_params(("arbitrary",), 60),
    )(proj, proj, proj, proj, proj, conv_out, dy, dw_w, ln_g, ln_b, pw_w, pw_b)


def _retention_bwd(proj, cos, sin, gn_g, states, dy, ex, ex_parts):
    R, E = proj.shape
    RW = gn_g.shape[1]
    H = RET_HEADS
    hd = RW // H
    half = hd // 2
    NC = R // CHUNK
    mask, qd, kd, cd = _decay_tables(H)
    scale = float(hd) ** -0.5
    order = _chunk_order(NC)
    phys = lambda i: order(NC - 1 - i)

    def body(*refs):
        p_ref, cos_ref, sin_ref, mask_ref, qd_ref, kd_ref, gn_ref, st_ref, dy_ref = refs[:9]
        ex_ins = refs[9:9 + ex.n]
        dp_ref, dgn_ref = refs[9 + ex.n:11 + ex.n]
        ex_outs = refs[11 + ex.n:11 + 2 * ex.n]
        dstate = refs[11 + 2 * ex.n]
        sems = refs[12 + 2 * ex.n:]

        @pl.when(pl.program_id(0) == 0)
        def _():
            ex.start(ex_ins, ex_outs, sems)
            dstate[...] = jnp.zeros_like(dstate)
            dgn_ref[...] = jnp.zeros_like(dgn_ref)

        cs, sn = cos_ref[...], sin_ref[...]
        hs = range(H)
        col = lambda j, h: slice(j * RW + h * hd, j * RW + (h + 1) * hd)
        qr = [_rot(p_ref[:, col(0, h)].astype(F32), cs, sn, half) for h in hs]
        kr = [_rot(p_ref[:, col(1, h)].astype(F32), cs, sn, half) * scale for h in hs]
        v = [p_ref[:, col(2, h)] for h in hs]
        qb = [qr[h].astype(BF16) for h in hs]
        kb = [kr[h].astype(BF16) for h in hs]
        qdb = [(qr[h] * qd_ref[h]).astype(BF16) for h in hs]
        kdb = [(kr[h] * kd_ref[h]).astype(BF16) for h in hs]
        s_prev = [st_ref[0, h] for h in hs]
        dst = [dstate[h] for h in hs]
        dstb = [dst[h].astype(BF16) for h in hs]
        sb = [(_dot(qb[h], kb[h], NT) * mask_ref[h]).astype(BF16) for h in hs]
        y_raw = [_dot(sb[h], v[h]) + _dot(qdb[h], s_prev[h]) for h in hs]
        dyrb, dg = [], []
        for h in hs:
            g = p_ref[:, col(3, h)].astype(F32)
            mu = jnp.mean(y_raw[h], axis=-1, keepdims=True)
            yc = y_raw[h] - mu
            rstd = lax.rsqrt(jnp.mean(yc * yc, axis=-1, keepdims=True) + EPS)
            xh = yc * rstd
            gn = gn_ref[:, col(0, h)]
            sg = _sigmoid(g)
            dyh = dy_ref[:, col(0, h)].astype(F32)
            dg.append((dyh * (xh * gn) * _dsilu(g, sg)).astype(BF16))
            dyn = dyh * (g * sg)
            dgn_ref[:, col(0, h)] += jnp.sum(dyn * xh, axis=0, keepdims=True)
            dxh = dyn * gn
            dyr = rstd * (dxh - jnp.mean(dxh, axis=-1, keepdims=True)
                          - xh * jnp.mean(dxh * xh, axis=-1, keepdims=True))
            dyrb.append(dyr.astype(BF16))
        dsb = [(_dot(dyrb[h], v[h], NT) * mask_ref[h]).astype(BF16) for h in hs]
        dqr = [_dot(dsb[h], kb[h]) + _dot(dyrb[h], s_prev[h], NT) * qd_ref[h] for h in hs]
        dkr = [_dot(dsb[h], qb[h], TN) + _dot(v[h], dstb[h], NT) * kd_ref[h] for h in hs]
        dv = [_dot(sb[h], dyrb[h], TN) + _dot(kdb[h], dstb[h]) for h in hs]
        dst_new = [dst[h] * cd[h] + _dot(qdb[h], dyrb[h], TN) for h in hs]
        for h in hs:
            dstate[h] = dst_new[h]
            dp_ref[:, col(0, h)] = _rot_inv(dqr[h], cs, sn, half).astype(BF16)
            dp_ref[:, col(1, h)] = (_rot_inv(dkr[h], cs, sn, half) * scale).astype(BF16)
            dp_ref[:, col(2, h)] = dv[h].astype(BF16)
            dp_ref[:, col(3, h)] = dg[h]

        @pl.when(pl.program_id(0) == NC - 1)
        def _():
            ex.wait(ex_ins, ex_outs, sems)

    const3 = lambda i: (0, 0, 0)
    outs = pl.pallas_call(
        body, name="retention_bwd",
        grid=(NC,),
        in_specs=[pl.BlockSpec((CHUNK, 4 * RW), lambda i: (phys(i), 0)),
                  pl.BlockSpec((CHUNK, half), lambda i: (phys(i), 0)),
                  pl.BlockSpec((CHUNK, half), lambda i: (phys(i), 0)),
                  pl.BlockSpec((H, CHUNK, CHUNK), const3),
                  pl.BlockSpec((H, CHUNK, 1), const3),
                  pl.BlockSpec((H, CHUNK, 1), const3),
                  pl.BlockSpec((1, RW), lambda i: (0, 0)),
                  pl.BlockSpec((1, H, hd, hd), lambda i: (phys(i), 0, 0, 0)),
                  pl.BlockSpec((CHUNK, RW), lambda i: (phys(i), 0))] + ex.specs,
        out_specs=[pl.BlockSpec((CHUNK, 4 * RW), lambda i: (phys(i), 0)),
                   pl.BlockSpec((1, RW), lambda i: (0, 0))] + ex.specs,
        out_shape=[jax.ShapeDtypeStruct((R, 4 * RW), BF16), jax.ShapeDtypeStruct((1, RW), F32)] + ex.out_shape,
        scratch_shapes=[pltpu.VMEM((H, hd, hd), F32)] + ex.scratch,
        compiler_params=_params(("arbitrary",), 32),
    )(proj, cos, sin, jnp.asarray(mask), jnp.asarray(qd), jnp.asarray(kd), gn_g, states, dy, *ex_parts)
    return outs[0], outs[1], outs[2:]


def _dproj_specs(tk, tn, n_ret, tile_axis, col_axis):
    def ret_map(*ids):
        t, j = ids[tile_axis], ids[col_axis]
        return (jnp.where(j < n_ret, t, 0), jnp.minimum(j, n_ret - 1))

    def conv_map(*ids):
        t, j = ids[tile_axis], ids[col_axis]
        return (jnp.where(j >= n_ret, t, 0), jnp.maximum(j - n_ret, 0))

    return pl.BlockSpec((tk, tn), ret_map), pl.BlockSpec((tk, tn), conv_map)


def _w_in_grad(hn, dp_ret, dp_conv):
    R, D = hn.shape
    tn = _pick_tile(dp_conv.shape[1] // 3, 1024, 128)
    n_ret, n_conv = dp_ret.shape[1] // tn, dp_conv.shape[1] // tn
    E = dp_ret.shape[1] + dp_conv.shape[1]
    tk = _pick_tile(R, 1024, MXU_DIM)
    n_t = R // tk
    ret_spec, conv_spec = _dproj_specs(tk, tn, n_ret, 1, 0)

    def body(hn_ref, r_ref, c_ref, out_ref, acc):
        j, t = pl.program_id(0), pl.program_id(1)

        @pl.when(t == 0)
        def _():
            acc[...] = jnp.zeros_like(acc)

        @pl.when(j < n_ret)
        def _():
            acc[...] += _dot(hn_ref[...], r_ref[...], TN)

        @pl.when(j >= n_ret)
        def _():
            acc[...] += _dot(hn_ref[...], c_ref[...], TN)

        @pl.when(t == n_t - 1)
        def _():
            out_ref[...] = acc[...].astype(BF16)

    return pl.pallas_call(
        body, name="w_in_grad",
        grid=(n_ret + n_conv, n_t),
        in_specs=[pl.BlockSpec((tk, D), lambda j, t: (t, 0)), ret_spec, conv_spec],
        out_specs=pl.BlockSpec((D, tn), lambda j, t: (0, j)),
        out_shape=jax.ShapeDtypeStruct((D, E), BF16),
        scratch_shapes=[pltpu.VMEM((D, tn), F32)],
        compiler_params=_params(("arbitrary", "arbitrary"), 48),
    )(hn, dp_ret, dp_conv)


def _h_grad(name, dp_ret, dp_conv, w_in, h, dh2, ln_g, h_row0, row0, n, dlg_init=None, prev=None, ex=None,
            ex_parts=()):
    D = h.shape[1]
    tn = _pick_tile(dp_conv.shape[1] // 3, 1024, 128)
    n_ret, n_conv = dp_ret.shape[1] // tn, dp_conv.shape[1] // tn
    n_k = n_ret + n_conv
    tm = _pick_tile(n, 1024)
    te = _pick_tile(tm, 256)
    n_e = tm // te
    assert row0 % tm == 0 and h_row0 % tm == 0
    b0, e0, he0 = row0 // tm, row0 // te, h_row0 // te

    def ret_map(t, k):
        return (jnp.where(k < n_ret, b0 + t, b0), jnp.minimum(k, n_ret - 1))

    def conv_map(t, k):
        return (jnp.where(k >= n_ret, b0 + t, b0), jnp.clip(k - n_ret, 0, n_conv - 1))

    def rows_e(first):
        return lambda t, k: (first + t * n_e + jnp.maximum(k - n_k, 0), 0)

    n_ex = 0 if ex is None else ex.n
    n_prev = 0 if prev is None else 1
    n_t = n // tm

    def body(*refs):
        r_ref, c_ref, w_ref, h_ref, dh2_ref, g_ref, init_ref = refs[:7]
        ex_ins = refs[7 + n_prev:7 + n_prev + n_ex]
        o = 7 + n_prev + n_ex
        dh_ref, dlg_ref = refs[o:o + 2]
        ex_outs = refs[o + 2:o + 2 + n_ex]
        acc = refs[o + 2 + n_ex]
        sems = refs[o + 3 + n_ex:]
        t, k = pl.program_id(0), pl.program_id(1)
        if ex is not None:
            @pl.when((k == 0) & (t == 0))
            def _():
                ex.start(ex_ins, ex_outs, sems)

        @pl.when(k == 0)
        def _():
            acc[...] = jnp.zeros_like(acc)

        @pl.when((k == 0) & (t == 0))
        def _():
            dlg_ref[...] = init_ref[...]

        @pl.when(k < n_ret)
        def _():
            acc[...] += _dot(r_ref[...], w_ref[...], NT)

        @pl.when((k >= n_ret) & (k < n_k))
        def _():
            acc[...] += _dot(c_ref[...], w_ref[...], NT)

        @pl.when(k >= n_k)
        def _():
            hv = h_ref[...]
            r = lax.rsqrt(jnp.mean(hv * hv, axis=-1, keepdims=True) + EPS)
            nrm = hv * r
            dhn = acc[pl.ds(pl.multiple_of((k - n_k) * te, te), te), :]
            dlg_ref[...] += jnp.sum(dhn * nrm, axis=0, keepdims=True)
            dn = dhn * g_ref[...]
            dh_ref[...] = dh2_ref[...] + r * (dn - nrm * jnp.mean(dn * nrm, axis=-1, keepdims=True))

        if ex is not None:
            @pl.when((k == n_k + n_e - 1) & (t == n_t - 1))
            def _():
                ex.wait(ex_ins, ex_outs, sems)

    row = lambda t, k: (0, 0)
    if dlg_init is None:
        dlg_init = jnp.zeros((1, D), F32)
    ex_specs, ex_shape, ex_scratch = ([], [], []) if ex is None else (ex.specs, ex.out_shape, ex.scratch)
    extra = {} if prev is None else dict(input_output_aliases={7: 0})
    outs = pl.pallas_call(
        body, name=name,
        grid=(n_t, n_k + n_e),
        in_specs=[pl.BlockSpec((tm, tn), ret_map), pl.BlockSpec((tm, tn), conv_map),
                  pl.BlockSpec((D, tn), lambda t, k: (0, jnp.minimum(k, n_k - 1))),
                  pl.BlockSpec((te, D), rows_e(he0)),
                  pl.BlockSpec((te, D), rows_e(e0)),
                  pl.BlockSpec((1, D), row), pl.BlockSpec((1, D), row)]
                 + [pl.BlockSpec(memory_space=pl.ANY)] * n_prev + ex_specs,
        out_specs=[pl.BlockSpec((te, D), rows_e(he0)),
                   pl.BlockSpec((1, D), row)] + ex_specs,
        out_shape=[jax.ShapeDtypeStruct(h.shape, F32), jax.ShapeDtypeStruct((1, D), F32)] + ex_shape,
        scratch_shapes=[pltpu.VMEM((tm, D), F32)] + ex_scratch,
        compiler_params=_params(("arbitrary", "arbitrary"), 60),
        **extra,
    )(dp_ret, dp_conv, w_in, h, dh2, ln_g, dlg_init, *([] if prev is None else [prev]), *ex_parts)
    return outs[0], outs[1], outs[2:]


def _adamw(w, g, m, v):
    m = ADAM_B1 * m + (1.0 - ADAM_B1) * g
    v = ADAM_B2 * v + (1.0 - ADAM_B2) * (g * g)
    m_hat = m / (1.0 - ADAM_B1 ** ADAM_STEP)
    v_hat = v / (1.0 - ADAM_B2 ** ADAM_STEP)
    delta = -ADAM_LR * (m_hat / (jnp.sqrt(v_hat) + ADAM_EPS) + ADAM_WD * w)
    return delta, m, v


def _sum_slots(ref):
    g = ref[0].astype(F32)
    for s in range(1, N_DEV):
        g = g + ref[s].astype(F32)
    return g


def _sum_adamw(name, parts, w, m, v, rows_target, first=0, count=None, prev=None, ex=None, ex_parts=()):
    rows, cols = w.shape
    tr = _pick_tile(rows, rows_target, 8)
    n_ex = 0 if ex is None else ex.n
    n_prev = 0 if prev is None else 4
    n_steps = rows // tr - first if count is None else count

    def body(*refs):
        p_ref, w_ref, m_ref, v_ref = refs[:4]
        ex_ins = refs[4 + n_prev:4 + n_prev + n_ex]
        o = 4 + n_prev + n_ex
        g_ref, d_ref, nm_ref, nv_ref = refs[o:o + 4]
        ex_outs, sems = refs[o + 4:o + 4 + n_ex], refs[o + 4 + n_ex:]
        if ex is not None:
            @pl.when(pl.program_id(0) == 0)
            def _():
                ex.start(ex_ins, ex_outs, sems)

        g = _sum_slots(p_ref)
        d, nm, nv = _adamw(w_ref[...], g, m_ref[...], v_ref[...])
        g_ref[...] = g
        d_ref[...] = d
        nm_ref[...] = nm
        nv_ref[...] = nv
        if ex is not None:
            @pl.when(pl.program_id(0) == n_steps - 1)
            def _():
                ex.wait(ex_ins, ex_outs, sems)

    tile = pl.BlockSpec((tr, cols), lambda i: (first + i, 0))
    ex_specs, ex_shape, ex_scratch = ([], [], []) if ex is None else (ex.specs, ex.out_shape, ex.scratch)
    extra = {} if prev is None else dict(input_output_aliases={4 + a: a for a in range(4)})
    outs = pl.pallas_call(
        body, name=name,
        grid=(n_steps,),
        in_specs=[pl.BlockSpec((N_DEV, tr, cols), lambda i: (0, first + i, 0)), tile, tile, tile]
                 + [pl.BlockSpec(memory_space=pl.ANY)] * n_prev + ex_specs,
        out_specs=[tile] * 4 + ex_specs,
        out_shape=[jax.ShapeDtypeStruct((rows, cols), F32)] * 4 + ex_shape,
        scratch_shapes=ex_scratch,
        compiler_params=_params(("arbitrary",), 40),
        **extra,
    )(parts, w, m, v, *([] if prev is None else list(prev)), *ex_parts)
    return outs[:4], outs[4:]


def _sum_adamw_small(parts_list, w_list, m_list, v_list, loss_parts):
    n = len(w_list)

    def body(*refs):
        p_refs, w_refs, m_refs, v_refs = refs[:n], refs[n:2 * n], refs[2 * n:3 * n], refs[3 * n:4 * n]
        lp_ref = refs[4 * n]
        outs = refs[4 * n + 1:]
        for a in range(n):
            g = _sum_slots(p_refs[a])
            d, nm, nv = _adamw(w_refs[a][...], g, m_refs[a][...], v_refs[a][...])
            outs[4 * a][...] = g
            outs[4 * a + 1][...] = d
            outs[4 * a + 2][...] = nm
            outs[4 * a + 3][...] = nv
        outs[4 * n][...] = _sum_slots(lp_ref)

    out_shape = []
    for w in w_list:
        out_shape += [jax.ShapeDtypeStruct(w.shape, F32)] * 4
    out_shape.append(jax.ShapeDtypeStruct(loss_parts.shape[1:], F32))
    return pl.pallas_call(body, name="sum_adamw_small", out_shape=out_shape)(
        *parts_list, *w_list, *m_list, *v_list, loss_parts)


def kernel(x, meta_tokens, ln_g, w_in, ret_gn_g, conv_dw_w, conv_dw_b, conv_ln_g, conv_ln_b, conv_pw_w, conv_pw_b, w_out, final_g, loss_target, m_meta_tokens, m_ln_g, m_w_in, m_ret_gn_g, m_conv_dw_w, m_conv_dw_b, m_conv_ln_g, m_conv_ln_b, m_conv_pw_w, m_conv_pw_b, m_w_out, m_final_g, v_meta_tokens, v_ln_g, v_w_in, v_ret_gn_g, v_conv_dw_w, v_conv_dw_b, v_conv_ln_g, v_conv_ln_b, v_conv_pw_w, v_conv_pw_b, v_w_out, v_final_g):
    _, SEQ, D = x.shape
    MIX = w_out.shape[2]
    RW = ret_gn_g.shape[1]
    CW = conv_pw_b.shape[1]
    assert RW == CW and MIX == RW + CW and SEQ % META_TILE == 0 and CONV_K - 1 <= HALO
    R = SEQ + META_TILE
    hd = RW // RET_HEADS
    half = hd // 2
    me = 4 * lax.axis_index("x") + 2 * lax.axis_index("y") + lax.axis_index("c")

    dw_pad = jnp.pad(conv_dw_w[0], ((0, HALO - CONV_K), (0, 0)))
    w_in_g, dw_g, meta_g = _gather_weights([w_in[0].astype(BF16), dw_pad, meta_tokens], [1, 1, 1])

    pos = jnp.concatenate([jnp.arange(SEQ, dtype=F32) + N_META, jnp.zeros((META_TILE - N_META,), F32),
                           jnp.arange(N_META, dtype=F32)])
    inv_freq = ROPE_BASE ** (-jnp.arange(half, dtype=F32) / half)
    ang = pos[:, None] * inv_freq[None, :]
    cos, sin = jnp.cos(ang), jnp.sin(ang)

    xs = x[0]
    meta_tile = jnp.concatenate([jnp.zeros((META_TILE - N_META, D), F32), meta_g], axis=0)
    target = loss_target[0]
    final_g2 = final_g[None, :]

    row_shards = [w_out[0].astype(BF16), conv_pw_w[0].astype(BF16)]
    proj, hn, (w_out_g, pw_g) = _in_proj(xs, ln_g, w_in_g, R, 0, ex=_Exchange(row_shards, [None, None]),
                                         ex_parts=row_shards)
    w_out_g, pw_g = w_out_g.reshape(MIX, D), pw_g.reshape(CW, CW)
    proj, hn, _ = _in_proj(meta_tile, ln_g, w_in_g, R, SEQ, prev=(proj, hn))
    y, states = _retention_fwd(proj, cos, sin, ret_gn_g, MIX)
    y, conv_out = _conv_fwd(proj, y, dw_g, conv_dw_b, conv_ln_g, conv_ln_b, pw_g, conv_pw_b)
    dh2, dy, dwo_p, dfg_p, loss_p = _out_proj_loss(xs, meta_tile, y, w_out_g, final_g2, target)

    dp_conv, dpw_p, dww_p, cvec_p = _conv_bwd(proj, conv_out, dy, dw_g, conv_ln_g, conv_ln_b, pw_g, conv_pw_b)
    dp_ret, dgn_p, (r_wo, r_pw) = _retention_bwd(proj, cos, sin, ret_gn_g, states, dy,
                                                 _Exchange([dwo_p, dpw_p], [0, 0]), [dwo_p, dpw_p])
    dwi_p = _w_in_grad(hn, dp_ret, dp_conv)
    tail = _pick_tile(SEQ // 2, 1024)
    grad_xs, dlg_x, (r_wi,) = _h_grad("h_grad", dp_ret, dp_conv, w_in_g, xs, dh2, ln_g, 0, 0, SEQ - tail,
                                      ex=_Exchange([dwi_p], [1]), ex_parts=[dwi_p])
    grad_xs, dlg_x, _ = _h_grad("h_grad_tail", dp_ret, dp_conv, w_in_g, xs, dh2, ln_g, SEQ - tail, SEQ - tail, tail,
                                dlg_init=dlg_x, prev=grad_xs)
    dh_meta, dlg_p, _ = _h_grad("h_grad_meta", dp_ret, dp_conv, w_in_g, meta_tile, dh2, ln_g, 0, SEQ, META_TILE,
                                dlg_init=dlg_x)
    grad_x = grad_xs[None]

    def at_row(r, a, b=None):
        v = a if b is None else jnp.concatenate([a, b], axis=1)
        return jnp.pad(v, ((r, 7 - r), (0, D - v.shape[1])))
    vec8 = (at_row(0, dlg_p) + at_row(1, dfg_p)
            + at_row(2, dgn_p, cvec_p[3:4])
            + at_row(3, cvec_p[1:2], cvec_p[2:3])
            + at_row(4, cvec_p[0:1], jnp.broadcast_to(loss_p, (1, CW))))
    small = jnp.concatenate([dh_meta[META_TILE - N_META:], vec8,
                             jnp.zeros((SMALL_ROWS - N_META - 8, D), F32)], axis=0)

    n_wi = D // 256
    quad_wi, (r_dww, r_small) = _sum_adamw(
        "sum_adamw_w_in", r_wi, w_in[0], m_w_in[0], v_w_in[0], 256, count=n_wi - 1,
        ex=_Exchange([dww_p, small], [1, None]), ex_parts=[dww_p, small])
    (g_wi, d_wi, nm_wi, nv_wi), _ = _sum_adamw(
        "sum_adamw_w_in_tail", r_wi, w_in[0], m_w_in[0], v_w_in[0], 256, first=n_wi - 1, prev=quad_wi)
    (g_wo, d_wo, nm_wo, nv_wo), _ = _sum_adamw("sum_adamw_w_out", r_wo, w_out[0], m_w_out[0], v_w_out[0], 128)
    (g_pw, d_pw, nm_pw, nv_pw), _ = _sum_adamw("sum_adamw_pw", r_pw, conv_pw_w[0], m_conv_pw_w[0], v_conv_pw_w[0], 128)

    dcol = D // N_DEV
    sm = lambda r0, nr, c0, nc: lax.slice(r_small, (0, r0, c0), (N_DEV, r0 + nr, c0 + nc))
    meta_parts = lax.dynamic_slice(r_small, (0, 0, me * dcol), (N_DEV, N_META, dcol))
    small_parts = [meta_parts, sm(16, 1, 0, D), sm(18, 1, 0, RW), r_dww, sm(18, 1, RW, CW),
                   sm(19, 1, 0, CW), sm(19, 1, CW, CW), sm(20, 1, 0, CW), sm(17, 1, 0, D)]
    pad31 = lambda a: jnp.pad(a, ((0, HALO - CONV_K), (0, 0)))
    ws = [meta_tokens, ln_g, ret_gn_g, pad31(conv_dw_w[0]), conv_dw_b, conv_ln_g, conv_ln_b, conv_pw_b, final_g2]
    ms = [m_meta_tokens, m_ln_g, m_ret_gn_g, pad31(m_conv_dw_w[0]), m_conv_dw_b, m_conv_ln_g, m_conv_ln_b,
          m_conv_pw_b, m_final_g[None, :]]
    vs = [v_meta_tokens, v_ln_g, v_ret_gn_g, pad31(v_conv_dw_w[0]), v_conv_dw_b, v_conv_ln_g, v_conv_ln_b,
          v_conv_pw_b, v_final_g[None, :]]
    loss_parts = sm(20, 1, CW, 1)
    outs = _sum_adamw_small(small_parts, ws, ms, vs, loss_parts)
    loss = outs[-1][0, 0]
    quad = [outs[4 * a:4 * a + 4] for a in range(len(ws))]
    (q_meta, q_lng, q_gn, q_dww, q_dwb, q_clg, q_clb, q_pwb, q_fg) = quad
    q_dww = [t[:CONV_K][None] for t in q_dww]
    q_fg = [t[0] for t in q_fg]
    q_wi = [t[None] for t in (g_wi, d_wi, nm_wi, nv_wi)]
    q_wo = [t[None] for t in (g_wo, d_wo, nm_wo, nv_wo)]
    q_pw = [t[None] for t in (g_pw, d_pw, nm_pw, nv_pw)]

    per_w = [q_meta, q_lng, q_wi, q_gn, q_dww, q_dwb, q_clg, q_clb, q_pw, q_pwb, q_wo, q_fg]
    result = [loss, grad_x]
    for which in range(4):
        result += [q[which] for q in per_w]
    return tuple(result)
```

```python
import functools

import numpy as np
import jax
import jax.numpy as jnp
from jax import lax
from jax.experimental import pallas as pl
from jax.experimental.pallas import tpu as pltpu

N_META = 16
RET_HEADS = 4
CONV_K = 31
CHUNK = 128
ROPE_BASE = 10000.0
EPS = 1e-6
ADAM_LR = 0.001
ADAM_B1 = 0.9
ADAM_B2 = 0.999
ADAM_EPS = 1e-08
ADAM_WD = 0.01
ADAM_STEP = 10

N_DEV = 8
META_TILE = 256
HALO = 32
SMALL_ROWS = 32
VMEM_BYTES_V7X = 64 * 1024 * 1024
MXU_DIM = 256

F32 = jnp.float32
BF16 = jnp.bfloat16
MESH = pl.DeviceIdType.MESH

NN = (((1,), (0,)), ((), ()))
NT = (((1,), (1,)), ((), ()))
TN = (((0,), (0,)), ((), ()))


def _dot(a, b, dims=NN):
    return lax.dot_general(a, b, dims, preferred_element_type=F32)


def _pick_tile(n, target, mult=16):
    best = None
    for t in range(mult, min(n, target) + 1, mult):
        if n % t == 0:
            best = t
    assert best is not None, (n, target)
    return best


def _params(sem=None, vmem_mb=None):
    kw = {}
    if sem is not None:
        kw["dimension_semantics"] = sem
    if vmem_mb is not None:
        kw["vmem_limit_bytes"] = min(vmem_mb * 1024 * 1024, VMEM_BYTES_V7X - 4 * 1024 * 1024)
    return pltpu.CompilerParams(**kw)


def _sigmoid(x):
    return jax.nn.sigmoid(x)


def _dsilu(x, sg):
    return sg * (1.0 + x * (1.0 - sg))


def _decay_tables(heads):
    h = np.arange(heads, dtype=np.float32)
    gamma = (1.0 - np.exp2(-5.0 - h)).astype(np.float32)
    log_g = np.log(gamma).astype(np.float32)
    idx = np.arange(CHUNK, dtype=np.float32)
    rel = idx[:, None] - idx[None, :]
    mask = np.where(rel[None] >= 0, np.exp(np.maximum(rel, 0.0)[None] * log_g[:, None, None]), 0.0)
    qd = np.exp((idx[None, :] + 1.0) * log_g[:, None])
    kd = np.exp((CHUNK - 1.0 - idx[None, :]) * log_g[:, None])
    cd = np.exp(CHUNK * log_g)
    return (mask.astype(np.float32), qd.astype(np.float32)[:, :, None], kd.astype(np.float32)[:, :, None],
            [float(c) for c in cd.astype(np.float32)])


def _gather_weights(shards, block_axes):
    n_arr = len(shards)
    out_shapes = []
    for s, ax in zip(shards, block_axes):
        shp = list(s.shape)
        shp[ax] *= N_DEV
        out_shapes.append(jax.ShapeDtypeStruct(tuple(shp), s.dtype))

    def body(*refs):
        ins, outs = refs[:n_arr], refs[n_arr:2 * n_arr]
        send_sems, recv_sems, local_sems = refs[2 * n_arr:]
        x, y, c = lax.axis_index("x"), lax.axis_index("y"), lax.axis_index("c")
        me, sibling = (x, y, c), (x, y, 1 - c)
        chips = [(1 - x, y), (x, 1 - y), (1 - x, 1 - y)]

        def block(a, dev):
            n = ins[a].shape[block_axes[a]]
            start = pl.multiple_of((4 * dev[0] + 2 * dev[1] + dev[2]) * n, n)
            idx = [slice(None)] * len(ins[a].shape)
            idx[block_axes[a]] = pl.ds(start, n)
            return outs[a].at[tuple(idx)]

        def copy(a, k, dev, to, src=None):
            return pltpu.make_async_remote_copy(
                src_ref=block(a, dev) if src is None else src, dst_ref=block(a, dev),
                send_sem=send_sems.at[a, k], recv_sem=recv_sems.at[a, k],
                device_id=to, device_id_type=MESH)

        mine = [pltpu.make_async_copy(ins[a], block(a, me), local_sems.at[a]) for a in range(n_arr)]
        for cp in mine:
            cp.start()
        first = []
        for a in range(n_arr):
            first.append(copy(a, 0, me, sibling, src=ins[a]))
            first += [copy(a, 1 + j, me, (*chip, c), src=ins[a]) for j, chip in enumerate(chips)]
        for cp in first:
            cp.start()
        passed = []
        for j, chip in enumerate(chips):
            for a in range(n_arr):
                copy(a, 1 + j, (*chip, c), me).wait_recv()
                fwd = copy(a, 4 + j, (*chip, c), sibling)
                fwd.start()
                passed.append(fwd)
        for a in range(n_arr):
            copy(a, 0, sibling, me).wait_recv()
            for j, chip in enumerate(chips):
                copy(a, 4 + j, (*chip, 1 - c), me).wait_recv()
        for cp in first + passed:
            cp.wait_send()
        for cp in mine:
            cp.wait()

    hbm = pl.BlockSpec(memory_space=pl.ANY)
    return pl.pallas_call(
        body, name="gather_weights",
        out_shape=out_shapes,
        in_specs=[hbm] * n_arr, out_specs=[hbm] * n_arr,
        scratch_shapes=[pltpu.SemaphoreType.DMA((n_arr, 7)), pltpu.SemaphoreType.DMA((n_arr, 7)),
                        pltpu.SemaphoreType.DMA((n_arr,))],
    )(*shards)


class _Exchange:
    def __init__(self, parts, block_axes):
        self.block_axes = list(block_axes)
        self.n = len(parts)
        self.out_shape = []
        for p, ax in zip(parts, block_axes):
            shp = list(p.shape)
            if ax is not None:
                assert shp[ax] % N_DEV == 0
                shp[ax] //= N_DEV
            self.out_shape.append(jax.ShapeDtypeStruct((N_DEV, *shp), p.dtype))
        self.scratch = [pltpu.SemaphoreType.DMA((self.n, N_DEV - 1)), pltpu.SemaphoreType.DMA((self.n, N_DEV - 1)),
                        pltpu.SemaphoreType.DMA((self.n,))]
        self.specs = [pl.BlockSpec(memory_space=pl.ANY)] * self.n

    def _copies(self, ins, outs, sems):
        send_sems, recv_sems, local_sems = sems
        x, y, c = lax.axis_index("x"), lax.axis_index("y"), lax.axis_index("c")
        me_idx = 4 * x + 2 * y + c

        def src_block(a, dev_idx):
            ax = self.block_axes[a]
            if ax is None:
                return ins[a]
            n = ins[a].shape[ax] // N_DEV
            idx = [slice(None)] * len(ins[a].shape)
            idx[ax] = pl.ds(pl.multiple_of(dev_idx * n, n), n)
            return ins[a].at[tuple(idx)]

        local = [pltpu.make_async_copy(src_block(a, me_idx), outs[a].at[me_idx], local_sems.at[a])
                 for a in range(self.n)]
        remote = []
        for m in range(1, N_DEV):
            px, py, pc = x ^ ((m >> 2) & 1), y ^ ((m >> 1) & 1), c ^ (m & 1)
            for a in range(self.n):
                remote.append(pltpu.make_async_remote_copy(
                    src_ref=src_block(a, 4 * px + 2 * py + pc), dst_ref=outs[a].at[me_idx],
                    send_sem=send_sems.at[a, m - 1], recv_sem=recv_sems.at[a, m - 1],
                    device_id=(px, py, pc), device_id_type=MESH))
        return local, remote

    def start(self, ins, outs, sems):
        local, remote = self._copies(ins, outs, sems)
        for cp in local + remote:
            cp.start()

    def wait(self, ins, outs, sems):
        local, remote = self._copies(ins, outs, sems)
        for cp in remote:
            cp.wait_recv()
        for cp in remote:
            cp.wait_send()
        for cp in local:
            cp.wait()


def _in_proj(h, ln_g, w_in, n_rows_out, row0, prev=None, ex=None, ex_parts=()):
    n, D = h.shape
    E = w_in.shape[1]
    tm = _pick_tile(n, 1024)
    assert row0 % tm == 0
    b0 = row0 // tm
    tn = _pick_tile(E, 1024, 128)
    n_i, n_j = n // tm, E // tn
    n_prev = 0 if prev is None else 2
    n_ex = 0 if ex is None else ex.n

    def body(*refs):
        h_ref, g_ref, w_ref = refs[:3]
        ex_ins = refs[3 + n_prev:3 + n_prev + n_ex]
        o = 3 + n_prev + n_ex
        proj_ref, hn_ref = refs[o], refs[o + 1]
        ex_outs, sems = refs[o + 2:o + 2 + n_ex], refs[o + 2 + n_ex:]
        i, j = pl.program_id(0), pl.program_id(1)
        if ex is not None:
            @pl.when((i == 0) & (j == 0))
            def _():
                ex.start(ex_ins, ex_outs, sems)

        @pl.when(j == 0)
        def _():
            hv = h_ref[...]
            r = lax.rsqrt(jnp.mean(hv * hv, axis=-1, keepdims=True) + EPS)
            hn_ref[...] = (hv * r * g_ref[...]).astype(BF16)

        proj_ref[...] = _dot(hn_ref[...], w_ref[...]).astype(BF16)
        if ex is not None:
            @pl.when((i == n_i - 1) & (j == n_j - 1))
            def _():
                ex.wait(ex_ins, ex_outs, sems)

    hbm = pl.BlockSpec(memory_space=pl.ANY)
    extra = {} if prev is None else dict(input_output_aliases={3: 0, 4: 1})
    outs = pl.pallas_call(
        body, name="in_proj" if prev is None else "in_proj_meta",
        grid=(n_i, n_j),
        in_specs=[pl.BlockSpec((tm, D), lambda i, j: (i, 0)),
                  pl.BlockSpec((1, D), lambda i, j: (0, 0)),
                  pl.BlockSpec((D, tn), lambda i, j: (0, j))] + [hbm] * n_prev + ([] if ex is None else ex.specs),
        out_specs=[pl.BlockSpec((tm, tn), lambda i, j: (b0 + i, j)),
                   pl.BlockSpec((tm, D), lambda i, j: (b0 + i, 0))] + ([] if ex is None else ex.specs),
        out_shape=[jax.ShapeDtypeStruct((n_rows_out, E), BF16), jax.ShapeDtypeStruct((n_rows_out, D), BF16)]
                  + ([] if ex is None else ex.out_shape),
        scratch_shapes=[] if ex is None else ex.scratch,
        compiler_params=_params(("arbitrary", "arbitrary"), 56),
        **extra,
    )(h, ln_g, w_in, *([] if prev is None else list(prev)), *ex_parts)
    return outs[0], outs[1], outs[2:]


def _rot(t, cos, sin, half):
    t1, t2 = t[:, :half], t[:, half:]
    return jnp.concatenate([t1 * cos - t2 * sin, t1 * sin + t2 * cos], axis=-1)


def _rot_inv(t, cos, sin, half):
    t1, t2 = t[:, :half], t[:, half:]
    return jnp.concatenate([t1 * cos + t2 * sin, t2 * cos - t1 * sin], axis=-1)


def _chunk_order(n_chunks):
    lead = META_TILE // CHUNK
    return lambda l: (l + n_chunks - lead) % n_chunks


def _retention_fwd(proj, cos, sin, gn_g, mix):
    R, E = proj.shape
    RW = gn_g.shape[1]
    H = RET_HEADS
    hd = RW // H
    half = hd // 2
    NC = R // CHUNK
    mask, qd, kd, cd = _decay_tables(H)
    scale = float(hd) ** -0.5
    phys = _chunk_order(NC)

    def body(p_ref, cos_ref, sin_ref, mask_ref, qd_ref, kd_ref, gn_ref, y_ref, st_ref, state):
        @pl.when(pl.program_id(0) == 0)
        def _():
            state[...] = jnp.zeros_like(state)

        cs, sn = cos_ref[...], sin_ref[...]
        hs = range(H)
        col = lambda j, h: slice(j * RW + h * hd, j * RW + (h + 1) * hd)
        qr = [_rot(p_ref[:, col(0, h)].astype(F32), cs, sn, half) for h in hs]
        kr = [_rot(p_ref[:, col(1, h)].astype(F32), cs, sn, half) * scale for h in hs]
        v = [p_ref[:, col(2, h)] for h in hs]
        s_prev = [state[h] for h in hs]
        s_prev_b = [s_prev[h].astype(BF16) for h in hs]
        s = [(_dot(qr[h].astype(BF16), kr[h].astype(BF16), NT) * mask_ref[h]).astype(BF16) for h in hs]
        y_raw = [_dot(s[h], v[h]) + _dot((qr[h] * qd_ref[h]).astype(BF16), s_prev_b[h]) for h in hs]
        s_new = [s_prev[h] * cd[h] + _dot((kr[h] * kd_ref[h]).astype(BF16), v[h], TN) for h in hs]
        for h in hs:
            st_ref[0, h] = s_prev_b[h]
            state[h] = s_new[h]
        for h in hs:
            g = p_ref[:, col(3, h)].astype(F32)
            mu = jnp.mean(y_raw[h], axis=-1, keepdims=True)
            yc = y_raw[h] - mu
            var = jnp.mean(yc * yc, axis=-1, keepdims=True)
            out = yc * lax.rsqrt(var + EPS) * gn_ref[:, col(0, h)] * (g * _sigmoid(g))
            y_ref[:, col(0, h)] = out.astype(BF16)

    const3 = lambda l: (0, 0, 0)
    return pl.pallas_call(
        body, name="retention_fwd",
        grid=(NC,),
        in_specs=[pl.BlockSpec((CHUNK, 4 * RW), lambda l: (phys(l), 0)),
                  pl.BlockSpec((CHUNK, half), lambda l: (phys(l), 0)),
                  pl.BlockSpec((CHUNK, half), lambda l: (phys(l), 0)),
                  pl.BlockSpec((H, CHUNK, CHUNK), const3),
                  pl.BlockSpec((H, CHUNK, 1), const3),
                  pl.BlockSpec((H, CHUNK, 1), const3),
                  pl.BlockSpec((1, RW), lambda l: (0, 0))],
        out_specs=[pl.BlockSpec((CHUNK, RW), lambda l: (phys(l), 0)),
                   pl.BlockSpec((1, H, hd, hd), lambda l: (phys(l), 0, 0, 0))],
        out_shape=[jax.ShapeDtypeStruct((R, mix), BF16), jax.ShapeDtypeStruct((NC, H, hd, hd), BF16)],
        scratch_shapes=[pltpu.VMEM((H, hd, hd), F32)],
        compiler_params=_params(("arbitrary",), 32),
    )(proj, cos, sin, jnp.asarray(mask), jnp.asarray(qd), jnp.asarray(kd), gn_g)


CONV_ROWS = 32
CONV_LANES = 512
DW_LANES = 256


def _conv_order(n_tiles):
    return lambda l: (l + n_tiles - 1) % n_tiles


def _halo_block(n_tiles, tm):
    per = tm // HALO
    return lambda l: ((l + n_tiles - 2) % n_tiles) * per + per - 1


def _fill_shifted(src, dst):
    rows, width = dst.shape[1], dst.shape[2]
    step = _pick_tile(rows, 64, 8)
    for r in range(1, 8):
        for r0 in range(0, rows, step):
            for l0 in range(0, width, CONV_LANES):
                dst[r - 1, r0:r0 + step, l0:l0 + CONV_LANES] = src[r + r0:r + r0 + step, l0:l0 + CONV_LANES]


def _at_offset(src, shifted, off, r0, rows, lanes):
    r = off % 8
    a = off - r + r0
    if r == 0:
        return src[a:a + rows, lanes]
    return shifted[r - 1, a:a + rows, lanes]


def _fill_glu(first, a_ref, b_ref, ah_ref, bh_ref, u_ext, tm):
    uh = ah_ref[...].astype(F32) * _sigmoid(bh_ref[...].astype(F32))
    u_ext[0:HALO, :] = jnp.where(first, 0.0, uh)
    u_ext[HALO:HALO + tm, :] = a_ref[...].astype(F32) * _sigmoid(b_ref[...].astype(F32))


def _layer_norm(cv, lg_ref, lb_ref):
    mu = jnp.mean(cv, axis=-1, keepdims=True)
    cc = cv - mu
    rstd = lax.rsqrt(jnp.mean(cc * cc, axis=-1, keepdims=True) + EPS)
    xh = cc * rstd
    return xh, rstd, xh * lg_ref[...] + lb_ref[...]


def _conv_fwd(proj, y_in, dw_w, dw_b, ln_g, ln_b, pw_w, pw_b):
    R, E = proj.shape
    CW = pw_w.shape[0]
    tm = META_TILE
    NTL = R // tm
    phys = _conv_order(NTL)
    halo = _halo_block(NTL, tm)
    cb = (E - 3 * CW) // CW
    base = HALO - (CONV_K - 1)

    def body(a_ref, b_ref, g_ref, ah_ref, bh_ref, w_ref, wb_ref, lg_ref, lb_ref, pw_ref, pb_ref, yin_ref,
             y_ref, c_ref, u_ext, u_sh):
        _fill_glu(pl.program_id(0) == 0, a_ref, b_ref, ah_ref, bh_ref, u_ext, tm)
        _fill_shifted(u_ext, u_sh)
        for r0 in range(0, tm, CONV_ROWS):
            for l0 in range(0, CW, CONV_LANES):
                lanes = slice(l0, l0 + CONV_LANES)
                acc = None
                for k in range(CONV_K):
                    term = _at_offset(u_ext, u_sh, base + k, r0, CONV_ROWS, lanes) * w_ref[k:k + 1, lanes]
                    acc = term if acc is None else acc + term
                c_ref[r0:r0 + CONV_ROWS, lanes] = acc + wb_ref[:, lanes]
        _, _, ln = _layer_norm(c_ref[...], lg_ref, lb_ref)
        s = (ln * _sigmoid(ln)).astype(BF16)
        upw = _dot(s, pw_ref[...]) + pb_ref[...]
        g = g_ref[...].astype(F32)
        y_ref[...] = (upw * (g * _sigmoid(g))).astype(BF16)

    row = lambda l: (0, 0)
    return pl.pallas_call(
        body, name="conv_fwd",
        grid=(NTL,),
        in_specs=[pl.BlockSpec((tm, CW), lambda l: (phys(l), cb)),
                  pl.BlockSpec((tm, CW), lambda l: (phys(l), cb + 1)),
                  pl.BlockSpec((tm, CW), lambda l: (phys(l), cb + 2)),
                  pl.BlockSpec((HALO, CW), lambda l: (halo(l), cb)),
                  pl.BlockSpec((HALO, CW), lambda l: (halo(l), cb + 1)),
                  pl.BlockSpec((HALO, CW), row),
                  pl.BlockSpec((1, CW), row), pl.BlockSpec((1, CW), row), pl.BlockSpec((1, CW), row),
                  pl.BlockSpec((CW, CW), row),
                  pl.BlockSpec((1, CW), row),
                  pl.BlockSpec(memory_space=pl.ANY)],
        out_specs=[pl.BlockSpec((tm, CW), lambda l: (phys(l), 1)),
                   pl.BlockSpec((tm, CW), lambda l: (phys(l), 0))],
        out_shape=[jax.ShapeDtypeStruct(y_in.shape, BF16), jax.ShapeDtypeStruct((R, CW), F32)],
        input_output_aliases={11: 0},
        scratch_shapes=[pltpu.VMEM((HALO + tm, CW), F32), pltpu.VMEM((7, tm + HALO - 8, CW), F32)],
        compiler_params=_params(("arbitrary",), 48),
    )(proj, proj, proj, proj, proj, dw_w, dw_b, ln_g, ln_b, pw_w, pw_b, y_in)


def _out_proj_loss(xs, meta_tile, y, w_out, w_out_t, final_g, target):
    SEQ, D = xs.shape
    R, MIX = y.shape
    tm = META_TILE
    n_seq = SEQ // tm
    n_tiles = R // tm
    rows_out = _pick_tile(MIX, 256)

    def body(x_ref, mt_ref, y_ref, w_hbm, wt_hbm, fg_ref, t_ref, dh2_ref, dy_ref, dwo_hbm, dfg_ref, loss_ref,
             w_scr, wt_scr, acc, stage, sem, sem_t):
        i = pl.program_id(0)

        @pl.when(i == 0)
        def _():
            cp = pltpu.make_async_copy(w_hbm, w_scr, sem)
            cp_t = pltpu.make_async_copy(wt_hbm, wt_scr, sem_t)
            cp.start()
            cp_t.start()
            acc[...] = jnp.zeros_like(acc)
            dfg_ref[...] = jnp.zeros_like(dfg_ref)
            loss_ref[...] = jnp.zeros_like(loss_ref)
            cp.wait()
            cp_t.wait()

        yb = y_ref[...]
        h2 = jnp.where(i < n_seq, x_ref[...], mt_ref[...]) + _dot(yb, w_scr[...])
        r2 = lax.rsqrt(jnp.mean(h2 * h2, axis=-1, keepdims=True) + EPS)
        n = h2 * r2
        fg = fg_ref[...]
        err = jnp.where(i < n_seq, n * fg - t_ref[...], 0.0)
        loss_ref[...] += 0.5 * jnp.sum(jnp.mean(err * err, axis=-1, keepdims=True), axis=0, keepdims=True)
        dout = err * (1.0 / D)
        dfg_ref[...] += jnp.sum(dout * n, axis=0, keepdims=True)
        dn = dout * fg
        dh2 = r2 * (dn - n * jnp.mean(dn * n, axis=-1, keepdims=True))
        dh2_ref[...] = dh2
        dh2b = dh2.astype(BF16)
        dy_ref[...] = _dot(dh2b, wt_scr[...]).astype(BF16)
        acc[...] += _dot(yb, dh2b, TN)

        @pl.when(i == n_tiles - 1)
        def _():
            for r in range(0, MIX, rows_out):
                stage[...] = acc[r:r + rows_out, :].astype(BF16)
                cp = pltpu.make_async_copy(stage, dwo_hbm.at[r:r + rows_out, :], sem)
                cp.start()
                cp.wait()

    row = lambda i: (0, 0)
    return pl.pallas_call(
        body, name="out_proj_loss",
        grid=(n_tiles,),
        in_specs=[pl.BlockSpec((tm, D), lambda i: (jnp.minimum(i, n_seq - 1), 0)),
                  pl.BlockSpec((tm, D), row),
                  pl.BlockSpec((tm, MIX), lambda i: (i, 0)),
                  pl.BlockSpec(memory_space=pl.ANY),
                  pl.BlockSpec(memory_space=pl.ANY),
                  pl.BlockSpec((1, D), row),
                  pl.BlockSpec((tm, D), lambda i: (jnp.minimum(i, n_seq - 1), 0))],
        out_specs=[pl.BlockSpec((tm, D), lambda i: (i, 0)),
                   pl.BlockSpec((tm, MIX), lambda i: (i, 0)),
                   pl.BlockSpec(memory_space=pl.ANY),
                   pl.BlockSpec((1, D), row),
                   pl.BlockSpec((1, 1), row)],
        out_shape=[jax.ShapeDtypeStruct((R, D), F32), jax.ShapeDtypeStruct((R, MIX), BF16),
                   jax.ShapeDtypeStruct((MIX, D), BF16), jax.ShapeDtypeStruct((1, D), F32),
                   jax.ShapeDtypeStruct((1, 1), F32)],
        scratch_shapes=[pltpu.VMEM((MIX, D), BF16), pltpu.VMEM((D, MIX), BF16), pltpu.VMEM((MIX, D), F32),
                        pltpu.VMEM((rows_out, D), BF16), pltpu.SemaphoreType.DMA, pltpu.SemaphoreType.DMA],
        compiler_params=_params(("arbitrary",), 60),
    )(xs, meta_tile, y, w_out, w_out_t, final_g, target)


def _conv_bwd(proj, conv_out, dy, dw_w, ln_g, ln_b, pw_w, pw_t, pw_b):
    R, E = proj.shape
    CW = pw_w.shape[0]
    tm = META_TILE
    NTL = R // tm
    order = _conv_order(NTL)
    phys = lambda i: order(NTL - 1 - i)
    halo_l = _halo_block(NTL, tm)
    halo = lambda i: halo_l(NTL - 1 - i)
    cb = (E - 3 * CW) // CW
    base = HALO - (CONV_K - 1)

    def body(a_ref, b_ref, g_ref, ah_ref, bh_ref, c_ref, dy_ref, w_ref, lg_ref, lb_ref, pw_ref, pwt_ref, pb_ref,
             dp_ref, dpw_ref, dww_ref, vec_ref, u_ext, u_sh, dc_ext, dc_sh, du_scr, dww_acc, dpw_acc):
        i = pl.program_id(0)

        @pl.when(i == 0)
        def _():
            dpw_acc[...] = jnp.zeros_like(dpw_acc)
            dww_ref[...] = jnp.zeros_like(dww_ref)
            vec_ref[...] = jnp.zeros_like(vec_ref)
            dww_acc[...] = jnp.zeros_like(dww_acc)
            dc_ext[tm:tm + HALO, :] = jnp.zeros((HALO, CW), F32)

        _fill_glu(i == NTL - 1, a_ref, b_ref, ah_ref, bh_ref, u_ext, tm)
        _fill_shifted(u_ext, u_sh)
        xh, rstd, ln = _layer_norm(c_ref[...], lg_ref, lb_ref)
        sg = _sigmoid(ln)
        sb = (ln * sg).astype(BF16)
        upw = _dot(sb, pw_ref[...]) + pb_ref[...]
        g = g_ref[...].astype(F32)
        sgg = _sigmoid(g)
        dyc = dy_ref[...].astype(F32)
        dp_ref[:, 2 * CW:3 * CW] = (dyc * upw * _dsilu(g, sgg)).astype(BF16)
        dupw = dyc * (g * sgg)
        dupw_b = dupw.astype(BF16)
        vec_ref[0:1, :] += jnp.sum(dupw, axis=0, keepdims=True)
        dpw_acc[...] += _dot(sb, dupw_b, TN)
        dln = _dot(dupw_b, pwt_ref[...]) * _dsilu(ln, sg)
        vec_ref[1:2, :] += jnp.sum(dln * xh, axis=0, keepdims=True)
        vec_ref[2:3, :] += jnp.sum(dln, axis=0, keepdims=True)
        dxh = dln * lg_ref[...]
        dc = rstd * (dxh - jnp.mean(dxh, axis=-1, keepdims=True) - xh * jnp.mean(dxh * xh, axis=-1, keepdims=True))
        vec_ref[3:4, :] += jnp.sum(dc, axis=0, keepdims=True)
        dc_ext[0:tm, :] = dc
        _fill_shifted(dc_ext, dc_sh)

        for l0 in range(0, CW, CONV_LANES):
            lanes = slice(l0, l0 + CONV_LANES)
            for r0 in range(0, tm, CONV_ROWS):
                acc = None
                for k in range(CONV_K):
                    term = _at_offset(dc_ext, dc_sh, CONV_K - 1 - k, r0, CONV_ROWS, lanes) * w_ref[k:k + 1, lanes]
                    acc = term if acc is None else acc + term
                du_scr[r0:r0 + CONV_ROWS, lanes] = acc
        for l0 in range(0, CW, DW_LANES):
            lanes = slice(l0, l0 + DW_LANES)
            for r0 in range(0, tm, CONV_ROWS):
                dcb = dc_ext[r0:r0 + CONV_ROWS, lanes]
                for k in range(CONV_K):
                    prod = dcb * _at_offset(u_ext, u_sh, base + k, r0, CONV_ROWS, lanes)
                    part = prod[0:8]
                    for q in range(8, CONV_ROWS, 8):
                        part = part + prod[q:q + 8]
                    dww_acc[k, :, lanes] += part

        du = du_scr[...]
        a = a_ref[...].astype(F32)
        sgb = _sigmoid(b_ref[...].astype(F32))
        dp_ref[:, 0:CW] = (du * sgb).astype(BF16)
        dp_ref[:, CW:2 * CW] = (du * a * sgb * (1.0 - sgb)).astype(BF16)
        dc_ext[tm:tm + HALO, :] = dc_ext[0:HALO, :]

        @pl.when(i == NTL - 1)
        def _():
            for k in range(CONV_K):
                dww_ref[k:k + 1, :] = jnp.sum(dww_acc[k], axis=0, keepdims=True)
            dpw_ref[...] = dpw_acc[...].astype(BF16)

    row = lambda i: (0, 0)
    return pl.pallas_call(
        body, name="conv_bwd",
        grid=(NTL,),
        in_specs=[pl.BlockSpec((tm, CW), lambda i: (phys(i), cb)),
                  pl.BlockSpec((tm, CW), lambda i: (phys(i), cb + 1)),
                  pl.BlockSpec((tm, CW), lambda i: (phys(i), cb + 2)),
                  pl.BlockSpec((HALO, CW), lambda i: (halo(i), cb)),
                  pl.BlockSpec((HALO, CW), lambda i: (halo(i), cb + 1)),
                  pl.BlockSpec((tm, CW), lambda i: (phys(i), 0)),
                  pl.BlockSpec((tm, CW), lambda i: (phys(i), 1)),
                  pl.BlockSpec((HALO, CW), row),
                  pl.BlockSpec((1, CW), row), pl.BlockSpec((1, CW), row),
                  pl.BlockSpec((CW, CW), row), pl.BlockSpec((CW, CW), row),
                  pl.BlockSpec((1, CW), row)],
        out_specs=[pl.BlockSpec((tm, 3 * CW), lambda i: (phys(i), 0)),
                   pl.BlockSpec((CW, CW), row),
                   pl.BlockSpec((HALO, CW), row),
                   pl.BlockSpec((8, CW), row)],
        out_shape=[jax.ShapeDtypeStruct((R, 3 * CW), BF16), jax.ShapeDtypeStruct((CW, CW), BF16),
                   jax.ShapeDtypeStruct((HALO, CW), F32), jax.ShapeDtypeStruct((8, CW), F32)],
        scratch_shapes=[pltpu.VMEM((HALO + tm, CW), F32), pltpu.VMEM((7, tm + HALO - 8, CW), F32),
                        pltpu.VMEM((tm + HALO, CW), F32), pltpu.VMEM((7, tm + HALO - 8, CW), F32),
                        pltpu.VMEM((tm, CW), F32), pltpu.VMEM((CONV_K, 8, CW), F32), pltpu.VMEM((CW, CW), F32)],
        compiler_params=_params(("arbitrary",), 60),
    )(proj, proj, proj, proj, proj, conv_out, dy, dw_w, ln_g, ln_b, pw_w, pw_t, pw_b)


def _retention_bwd(proj, cos, sin, gn_g, states, dy, ex, ex_parts):
    R, E = proj.shape
    RW = gn_g.shape[1]
    H = RET_HEADS
    hd = RW // H
    half = hd // 2
    NC = R // CHUNK
    mask, qd, kd, cd = _decay_tables(H)
    scale = float(hd) ** -0.5
    order = _chunk_order(NC)
    phys = lambda i: order(NC - 1 - i)

    def body(*refs):
        p_ref, cos_ref, sin_ref, mask_ref, qd_ref, kd_ref, gn_ref, st_ref, dy_ref = refs[:9]
        ex_ins = refs[9:9 + ex.n]
        dp_ref, dgn_ref = refs[9 + ex.n:11 + ex.n]
        ex_outs = refs[11 + ex.n:11 + 2 * ex.n]
        dstate = refs[11 + 2 * ex.n]
        sems = refs[12 + 2 * ex.n:]

        @pl.when(pl.program_id(0) == 0)
        def _():
            ex.start(ex_ins, ex_outs, sems)
            dstate[...] = jnp.zeros_like(dstate)
            dgn_ref[...] = jnp.zeros_like(dgn_ref)

        cs, sn = cos_ref[...], sin_ref[...]
        hs = range(H)
        col = lambda j, h: slice(j * RW + h * hd, j * RW + (h + 1) * hd)
        qr = [_rot(p_ref[:, col(0, h)].astype(F32), cs, sn, half) for h in hs]
        kr = [_rot(p_ref[:, col(1, h)].astype(F32), cs, sn, half) * scale for h in hs]
        v = [p_ref[:, col(2, h)] for h in hs]
        qb = [qr[h].astype(BF16) for h in hs]
        kb = [kr[h].astype(BF16) for h in hs]
        qdb = [(qr[h] * qd_ref[h]).astype(BF16) for h in hs]
        kdb = [(kr[h] * kd_ref[h]).astype(BF16) for h in hs]
        s_prev = [st_ref[0, h] for h in hs]
        dst = [dstate[h] for h in hs]
        dstb = [dst[h].astype(BF16) for h in hs]
        sb = [(_dot(qb[h], kb[h], NT) * mask_ref[h]).astype(BF16) for h in hs]
        y_raw = [_dot(sb[h], v[h]) + _dot(qdb[h], s_prev[h]) for h in hs]
        dyrb, dg = [], []
        for h in hs:
            g = p_ref[:, col(3, h)].astype(F32)
            mu = jnp.mean(y_raw[h], axis=-1, keepdims=True)
            yc = y_raw[h] - mu
            rstd = lax.rsqrt(jnp.mean(yc * yc, axis=-1, keepdims=True) + EPS)
            xh = yc * rstd
            gn = gn_ref[:, col(0, h)]
            sg = _sigmoid(g)
            dyh = dy_ref[:, col(0, h)].astype(F32)
            dg.append((dyh * (xh * gn) * _dsilu(g, sg)).astype(BF16))
            dyn = dyh * (g * sg)
            dgn_ref[:, col(0, h)] += jnp.sum(dyn * xh, axis=0, keepdims=True)
            dxh = dyn * gn
            dyr = rstd * (dxh - jnp.mean(dxh, axis=-1, keepdims=True)
                          - xh * jnp.mean(dxh * xh, axis=-1, keepdims=True))
            dyrb.append(dyr.astype(BF16))
        dsb = [(_dot(dyrb[h], v[h], NT) * mask_ref[h]).astype(BF16) for h in hs]
        dqr = [_dot(dsb[h], kb[h]) + _dot(dyrb[h], s_prev[h], NT) * qd_ref[h] for h in hs]
        dkr = [_dot(dsb[h], qb[h], TN) + _dot(v[h], dstb[h], NT) * kd_ref[h] for h in hs]
        dv = [_dot(sb[h], dyrb[h], TN) + _dot(kdb[h], dstb[h]) for h in hs]
        dst_new = [dst[h] * cd[h] + _dot(qdb[h], dyrb[h], TN) for h in hs]
        for h in hs:
            dstate[h] = dst_new[h]
            dp_ref[:, col(0, h)] = _rot_inv(dqr[h], cs, sn, half).astype(BF16)
            dp_ref[:, col(1, h)] = (_rot_inv(dkr[h], cs, sn, half) * scale).astype(BF16)
            dp_ref[:, col(2, h)] = dv[h].astype(BF16)
            dp_ref[:, col(3, h)] = dg[h]

        @pl.when(pl.program_id(0) == NC - 1)
        def _():
            ex.wait(ex_ins, ex_outs, sems)

    const3 = lambda i: (0, 0, 0)
    outs = pl.pallas_call(
        body, name="retention_bwd",
        grid=(NC,),
        in_specs=[pl.BlockSpec((CHUNK, 4 * RW), lambda i: (phys(i), 0)),
                  pl.BlockSpec((CHUNK, half), lambda i: (phys(i), 0)),
                  pl.BlockSpec((CHUNK, half), lambda i: (phys(i), 0)),
                  pl.BlockSpec((H, CHUNK, CHUNK), const3),
                  pl.BlockSpec((H, CHUNK, 1), const3),
                  pl.BlockSpec((H, CHUNK, 1), const3),
                  pl.BlockSpec((1, RW), lambda i: (0, 0)),
                  pl.BlockSpec((1, H, hd, hd), lambda i: (phys(i), 0, 0, 0)),
                  pl.BlockSpec((CHUNK, RW), lambda i: (phys(i), 0))] + ex.specs,
        out_specs=[pl.BlockSpec((CHUNK, 4 * RW), lambda i: (phys(i), 0)),
                   pl.BlockSpec((1, RW), lambda i: (0, 0))] + ex.specs,
        out_shape=[jax.ShapeDtypeStruct((R, 4 * RW), BF16), jax.ShapeDtypeStruct((1, RW), F32)] + ex.out_shape,
        scratch_shapes=[pltpu.VMEM((H, hd, hd), F32)] + ex.scratch,
        compiler_params=_params(("arbitrary",), 32),
    )(proj, cos, sin, jnp.asarray(mask), jnp.asarray(qd), jnp.asarray(kd), gn_g, states, dy, *ex_parts)
    return outs[0], outs[1], outs[2:]


def _dproj_specs(tk, tn, n_ret, tile_axis, col_axis):
    def ret_map(*ids):
        t, j = ids[tile_axis], ids[col_axis]
        return (jnp.where(j < n_ret, t, 0), jnp.minimum(j, n_ret - 1))

    def conv_map(*ids):
        t, j = ids[tile_axis], ids[col_axis]
        return (jnp.where(j >= n_ret, t, 0), jnp.maximum(j - n_ret, 0))

    return pl.BlockSpec((tk, tn), ret_map), pl.BlockSpec((tk, tn), conv_map)


def _w_in_grad(hn, dp_ret, dp_conv):
    R, D = hn.shape
    tn = _pick_tile(dp_conv.shape[1] // 3, 1024, 128)
    n_ret, n_conv = dp_ret.shape[1] // tn, dp_conv.shape[1] // tn
    E = dp_ret.shape[1] + dp_conv.shape[1]
    tk = _pick_tile(R, 1024, MXU_DIM)
    n_t = R // tk
    ret_spec, conv_spec = _dproj_specs(tk, tn, n_ret, 1, 0)

    def body(hn_ref, r_ref, c_ref, out_ref, acc):
        j, t = pl.program_id(0), pl.program_id(1)

        @pl.when(t == 0)
        def _():
            acc[...] = jnp.zeros_like(acc)

        @pl.when(j < n_ret)
        def _():
            acc[...] += _dot(hn_ref[...], r_ref[...], TN)

        @pl.when(j >= n_ret)
        def _():
            acc[...] += _dot(hn_ref[...], c_ref[...], TN)

        @pl.when(t == n_t - 1)
        def _():
            out_ref[...] = acc[...].astype(BF16)

    return pl.pallas_call(
        body, name="w_in_grad",
        grid=(n_ret + n_conv, n_t),
        in_specs=[pl.BlockSpec((tk, D), lambda j, t: (t, 0)), ret_spec, conv_spec],
        out_specs=pl.BlockSpec((D, tn), lambda j, t: (0, j)),
        out_shape=jax.ShapeDtypeStruct((D, E), BF16),
        scratch_shapes=[pltpu.VMEM((D, tn), F32)],
        compiler_params=_params(("arbitrary", "arbitrary"), 48),
    )(hn, dp_ret, dp_conv)


def _h_grad(name, dp_ret, dp_conv, w_in_t, h, dh2, ln_g, h_row0, row0, n, dlg_init=None, prev=None, ex=None,
            ex_parts=()):
    D = h.shape[1]
    tn = _pick_tile(dp_conv.shape[1] // 3, 1024, 128)
    n_ret, n_conv = dp_ret.shape[1] // tn, dp_conv.shape[1] // tn
    n_k = n_ret + n_conv
    tm = _pick_tile(n, 1024)
    te = _pick_tile(tm, 256)
    n_e = tm // te
    assert row0 % tm == 0 and h_row0 % tm == 0
    b0, e0, he0 = row0 // tm, row0 // te, h_row0 // te

    def ret_map(t, k):
        return (jnp.where(k < n_ret, b0 + t, b0), jnp.minimum(k, n_ret - 1))

    def conv_map(t, k):
        return (jnp.where(k >= n_ret, b0 + t, b0), jnp.clip(k - n_ret, 0, n_conv - 1))

    def rows_e(first):
        return lambda t, k: (first + t * n_e + jnp.maximum(k - n_k, 0), 0)

    n_ex = 0 if ex is None else ex.n
    n_prev = 0 if prev is None else 1
    n_t = n // tm

    def body(*refs):
        r_ref, c_ref, w_ref, h_ref, dh2_ref, g_ref, init_ref = refs[:7]
        ex_ins = refs[7 + n_prev:7 + n_prev + n_ex]
        o = 7 + n_prev + n_ex
        dh_ref, dlg_ref = refs[o:o + 2]
        ex_outs = refs[o + 2:o + 2 + n_ex]
        acc = refs[o + 2 + n_ex]
        sems = refs[o + 3 + n_ex:]
        t, k = pl.program_id(0), pl.program_id(1)
        if ex is not None:
            @pl.when((k == 0) & (t == 0))
            def _():
                ex.start(ex_ins, ex_outs, sems)

        @pl.when(k == 0)
        def _():
            acc[...] = jnp.zeros_like(acc)

        @pl.when((k == 0) & (t == 0))
        def _():
            dlg_ref[...] = init_ref[...]

        @pl.when(k < n_ret)
        def _():
            acc[...] += _dot(r_ref[...], w_ref[...])

        @pl.when((k >= n_ret) & (k < n_k))
        def _():
            acc[...] += _dot(c_ref[...], w_ref[...])

        @pl.when(k >= n_k)
        def _():
            hv = h_ref[...]
            r = lax.rsqrt(jnp.mean(hv * hv, axis=-1, keepdims=True) + EPS)
            nrm = hv * r
            dhn = acc[pl.ds(pl.multiple_of((k - n_k) * te, te), te), :]
            dlg_ref[...] += jnp.sum(dhn * nrm, axis=0, keepdims=True)
            dn = dhn * g_ref[...]
            dh_ref[...] = dh2_ref[...] + r * (dn - nrm * jnp.mean(dn * nrm, axis=-1, keepdims=True))

        if ex is not None:
            @pl.when((k == n_k + n_e - 1) & (t == n_t - 1))
            def _():
                ex.wait(ex_ins, ex_outs, sems)

    row = lambda t, k: (0, 0)
    if dlg_init is None:
        dlg_init = jnp.zeros((1, D), F32)
    ex_specs, ex_shape, ex_scratch = ([], [], []) if ex is None else (ex.specs, ex.out_shape, ex.scratch)
    extra = {} if prev is None else dict(input_output_aliases={7: 0})
    outs = pl.pallas_call(
        body, name=name,
        grid=(n_t, n_k + n_e),
        in_specs=[pl.BlockSpec((tm, tn), ret_map), pl.BlockSpec((tm, tn), conv_map),
                  pl.BlockSpec((tn, D), lambda t, k: (jnp.minimum(k, n_k - 1), 0)),
                  pl.BlockSpec((te, D), rows_e(he0)),
                  pl.BlockSpec((te, D), rows_e(e0)),
                  pl.BlockSpec((1, D), row), pl.BlockSpec((1, D), row)]
                 + [pl.BlockSpec(memory_space=pl.ANY)] * n_prev + ex_specs,
        out_specs=[pl.BlockSpec((te, D), rows_e(he0)),
                   pl.BlockSpec((1, D), row)] + ex_specs,
        out_shape=[jax.ShapeDtypeStruct(h.shape, F32), jax.ShapeDtypeStruct((1, D), F32)] + ex_shape,
        scratch_shapes=[pltpu.VMEM((tm, D), F32)] + ex_scratch,
        compiler_params=_params(("arbitrary", "arbitrary"), 60),
        **extra,
    )(dp_ret, dp_conv, w_in_t, h, dh2, ln_g, dlg_init, *([] if prev is None else [prev]), *ex_parts)
    return outs[0], outs[1], outs[2:]


def _adamw(w, g, m, v):
    m = ADAM_B1 * m + (1.0 - ADAM_B1) * g
    v = ADAM_B2 * v + (1.0 - ADAM_B2) * (g * g)
    m_hat = m / (1.0 - ADAM_B1 ** ADAM_STEP)
    v_hat = v / (1.0 - ADAM_B2 ** ADAM_STEP)
    delta = -ADAM_LR * (m_hat / (jnp.sqrt(v_hat) + ADAM_EPS) + ADAM_WD * w)
    return delta, m, v


def _sum_slots(ref):
    g = ref[0].astype(F32)
    for s in range(1, N_DEV):
        g = g + ref[s].astype(F32)
    return g


def _sum_adamw(name, parts, w, m, v, rows_target, first=0, count=None, prev=None, ex=None, ex_parts=()):
    rows, cols = w.shape
    tr = _pick_tile(rows, rows_target, 8)
    n_ex = 0 if ex is None else ex.n
    n_prev = 0 if prev is None else 4
    n_steps = rows // tr - first if count is None else count

    def body(*refs):
        p_ref, w_ref, m_ref, v_ref = refs[:4]
        ex_ins = refs[4 + n_prev:4 + n_prev + n_ex]
        o = 4 + n_prev + n_ex
        g_ref, d_ref, nm_ref, nv_ref = refs[o:o + 4]
        ex_outs, sems = refs[o + 4:o + 4 + n_ex], refs[o + 4 + n_ex:]
        if ex is not None:
            @pl.when(pl.program_id(0) == 0)
            def _():
                ex.start(ex_ins, ex_outs, sems)

        g = _sum_slots(p_ref)
        d, nm, nv = _adamw(w_ref[...], g, m_ref[...], v_ref[...])
        g_ref[...] = g
        d_ref[...] = d
        nm_ref[...] = nm
        nv_ref[...] = nv
        if ex is not None:
            @pl.when(pl.program_id(0) == n_steps - 1)
            def _():
                ex.wait(ex_ins, ex_outs, sems)

    tile = pl.BlockSpec((tr, cols), lambda i: (first + i, 0))
    ex_specs, ex_shape, ex_scratch = ([], [], []) if ex is None else (ex.specs, ex.out_shape, ex.scratch)
    extra = {} if prev is None else dict(input_output_aliases={4 + a: a for a in range(4)})
    outs = pl.pallas_call(
        body, name=name,
        grid=(n_steps,),
        in_specs=[pl.BlockSpec((N_DEV, tr, cols), lambda i: (0, first + i, 0)), tile, tile, tile]
                 + [pl.BlockSpec(memory_space=pl.ANY)] * n_prev + ex_specs,
        out_specs=[tile] * 4 + ex_specs,
        out_shape=[jax.ShapeDtypeStruct((rows, cols), F32)] * 4 + ex_shape,
        scratch_shapes=ex_scratch,
        compiler_params=_params(("arbitrary",), 40),
        **extra,
    )(parts, w, m, v, *([] if prev is None else list(prev)), *ex_parts)
    return outs[:4], outs[4:]


def _sum_adamw_small(parts_list, w_list, m_list, v_list, loss_parts):
    n = len(w_list)

    def body(*refs):
        p_refs, w_refs, m_refs, v_refs = refs[:n], refs[n:2 * n], refs[2 * n:3 * n], refs[3 * n:4 * n]
        lp_ref = refs[4 * n]
        outs = refs[4 * n + 1:]
        for a in range(n):
            g = _sum_slots(p_refs[a])
            d, nm, nv = _adamw(w_refs[a][...], g, m_refs[a][...], v_refs[a][...])
            outs[4 * a][...] = g
            outs[4 * a + 1][...] = d
            outs[4 * a + 2][...] = nm
            outs[4 * a + 3][...] = nv
        outs[4 * n][...] = _sum_slots(lp_ref)

    out_shape = []
    for w in w_list:
        out_shape += [jax.ShapeDtypeStruct(w.shape, F32)] * 4
    out_shape.append(jax.ShapeDtypeStruct(loss_parts.shape[1:], F32))
    return pl.pallas_call(body, name="sum_adamw_small", out_shape=out_shape)(
        *parts_list, *w_list, *m_list, *v_list, loss_parts)


def kernel(x, meta_tokens, ln_g, w_in, ret_gn_g, conv_dw_w, conv_dw_b, conv_ln_g, conv_ln_b, conv_pw_w, conv_pw_b, w_out, final_g, loss_target, m_meta_tokens, m_ln_g, m_w_in, m_ret_gn_g, m_conv_dw_w, m_conv_dw_b, m_conv_ln_g, m_conv_ln_b, m_conv_pw_w, m_conv_pw_b, m_w_out, m_final_g, v_meta_tokens, v_ln_g, v_w_in, v_ret_gn_g, v_conv_dw_w, v_conv_dw_b, v_conv_ln_g, v_conv_ln_b, v_conv_pw_w, v_conv_pw_b, v_w_out, v_final_g):
    _, SEQ, D = x.shape
    MIX = w_out.shape[2]
    RW = ret_gn_g.shape[1]
    CW = conv_pw_b.shape[1]
    assert RW == CW and MIX == RW + CW and SEQ % META_TILE == 0 and CONV_K - 1 <= HALO
    R = SEQ + META_TILE
    hd = RW // RET_HEADS
    half = hd // 2
    me = 4 * lax.axis_index("x") + 2 * lax.axis_index("y") + lax.axis_index("c")

    dw_pad = jnp.pad(conv_dw_w[0], ((0, HALO - CONV_K), (0, 0)))
    w_in_g, dw_g, meta_g = _gather_weights([w_in[0].astype(BF16), dw_pad, meta_tokens], [1, 1, 1])

    pos = jnp.concatenate([jnp.arange(SEQ, dtype=F32) + N_META, jnp.zeros((META_TILE - N_META,), F32),
                           jnp.arange(N_META, dtype=F32)])
    inv_freq = ROPE_BASE ** (-jnp.arange(half, dtype=F32) / half)
    ang = pos[:, None] * inv_freq[None, :]
    cos, sin = jnp.cos(ang), jnp.sin(ang)

    xs = x[0]
    meta_tile = jnp.concatenate([jnp.zeros((META_TILE - N_META, D), F32), meta_g], axis=0)
    target = loss_target[0]
    final_g2 = final_g[None, :]

    row_shards = [w_out[0].astype(BF16), conv_pw_w[0].astype(BF16)]
    proj, hn, (w_out_g, pw_g) = _in_proj(xs, ln_g, w_in_g, R, 0, ex=_Exchange(row_shards, [None, None]),
                                         ex_parts=row_shards)
    w_out_g, pw_g = w_out_g.reshape(MIX, D), pw_g.reshape(CW, CW)
    w_in_t, w_out_t, pw_t = w_in_g.T, w_out_g.T, pw_g.T
    proj, hn, _ = _in_proj(meta_tile, ln_g, w_in_g, R, SEQ, prev=(proj, hn))
    y, states = _retention_fwd(proj, cos, sin, ret_gn_g, MIX)
    y, conv_out = _conv_fwd(proj, y, dw_g, conv_dw_b, conv_ln_g, conv_ln_b, pw_g, conv_pw_b)
    dh2, dy, dwo_p, dfg_p, loss_p = _out_proj_loss(xs, meta_tile, y, w_out_g, w_out_t, final_g2, target)

    dp_conv, dpw_p, dww_p, cvec_p = _conv_bwd(proj, conv_out, dy, dw_g, conv_ln_g, conv_ln_b, pw_g, pw_t, conv_pw_b)
    dp_ret, dgn_p, (r_wo, r_pw) = _retention_bwd(proj, cos, sin, ret_gn_g, states, dy,
                                                 _Exchange([dwo_p, dpw_p], [0, 0]), [dwo_p, dpw_p])
    dwi_p = _w_in_grad(hn, dp_ret, dp_conv)
    grad_xs, dlg_x, (r_wi,) = _h_grad("h_grad", dp_ret, dp_conv, w_in_t, xs, dh2, ln_g, 0, 0, SEQ,
                                      ex=_Exchange([dwi_p], [1]), ex_parts=[dwi_p])
    dh_meta, dlg_p, _ = _h_grad("h_grad_meta", dp_ret, dp_conv, w_in_t, meta_tile, dh2, ln_g, 0, SEQ, META_TILE,
                                dlg_init=dlg_x)
    grad_x = grad_xs[None]

    def at_row(r, a, b=None):
        v = a if b is None else jnp.concatenate([a, b], axis=1)
        return jnp.pad(v, ((r, 7 - r), (0, D - v.shape[1])))
    vec8 = (at_row(0, dlg_p) + at_row(1, dfg_p)
            + at_row(2, dgn_p, cvec_p[3:4])
            + at_row(3, cvec_p[1:2], cvec_p[2:3])
            + at_row(4, cvec_p[0:1], jnp.broadcast_to(loss_p, (1, CW))))
    small = jnp.concatenate([dh_meta[META_TILE - N_META:], vec8,
                             jnp.zeros((SMALL_ROWS - N_META - 8, D), F32)], axis=0)

    (g_wi, d_wi, nm_wi, nv_wi), (r_dww, r_small) = _sum_adamw(
        "sum_adamw_w_in", r_wi, w_in[0], m_w_in[0], v_w_in[0], 256,
        ex=_Exchange([dww_p, small], [1, None]), ex_parts=[dww_p, small])
    (g_wo, d_wo, nm_wo, nv_wo), _ = _sum_adamw("sum_adamw_w_out", r_wo, w_out[0], m_w_out[0], v_w_out[0], 128)
    (g_pw, d_pw, nm_pw, nv_pw), _ = _sum_adamw("sum_adamw_pw", r_pw, conv_pw_w[0], m_conv_pw_w[0], v_conv_pw_w[0], 128)

    dcol = D // N_DEV
    sm = lambda r0, nr, c0, nc: lax.slice(r_small, (0, r0, c0), (N_DEV, r0 + nr, c0 + nc))
    meta_parts = lax.dynamic_slice(r_small, (0, 0, me * dcol), (N_DEV, N_META, dcol))
    small_parts = [meta_parts, sm(16, 1, 0, D), sm(18, 1, 0, RW), r_dww, sm(18, 1, RW, CW),
                   sm(19, 1, 0, CW), sm(19, 1, CW, CW), sm(20, 1, 0, CW), sm(17, 1, 0, D)]
    pad31 = lambda a: jnp.pad(a, ((0, HALO - CONV_K), (0, 0)))
    ws = [meta_tokens, ln_g, ret_gn_g, pad31(conv_dw_w[0]), conv_dw_b, conv_ln_g, conv_ln_b, conv_pw_b, final_g2]
    ms = [m_meta_tokens, m_ln_g, m_ret_gn_g, pad31(m_conv_dw_w[0]), m_conv_dw_b, m_conv_ln_g, m_conv_ln_b,
          m_conv_pw_b, m_final_g[None, :]]
    vs = [v_meta_tokens, v_ln_g, v_ret_gn_g, pad31(v_conv_dw_w[0]), v_conv_dw_b, v_conv_ln_g, v_conv_ln_b,
          v_conv_pw_b, v_final_g[None, :]]
    loss_parts = sm(20, 1, CW, 1)
    outs = _sum_adamw_small(small_parts, ws, ms, vs, loss_parts)
    loss = outs[-1][0, 0]
    quad = [outs[4 * a:4 * a + 4] for a in range(len(ws))]
    (q_meta, q_lng, q_gn, q_dww, q_dwb, q_clg, q_clb, q_pwb, q_fg) = quad
    q_dww = [t[:CONV_K][None] for t in q_dww]
    q_fg = [t[0] for t in q_fg]
    q_wi = [t[None] for t in (g_wi, d_wi, nm_wi, nv_wi)]
    q_wo = [t[None] for t in (g_wo, d_wo, nm_wo, nv_wo)]
    q_pw = [t[None] for t in (g_pw, d_pw, nm_pw, nv_pw)]

    per_w = [q_meta, q_lng, q_wi, q_gn, q_dww, q_dwb, q_clg, q_clb, q_pw, q_pwb, q_wo, q_fg]
    result = [loss, grad_x]
    for which in range(4):
        result += [q[which] for q in per_w]
    return tuple(result)
```

```python
import functools

import numpy as np
import jax
import jax.numpy as jnp
from jax import lax
from jax.experimental import pallas as pl
from jax.experimental.pallas import tpu as pltpu

N_META = 16
RET_HEADS = 4
CONV_K = 31
CHUNK = 128
ROPE_BASE = 10000.0
EPS = 1e-6
ADAM_LR = 0.001
ADAM_B1 = 0.9
ADAM_B2 = 0.999
ADAM_EPS = 1e-08
ADAM_WD = 0.01
ADAM_STEP = 10

N_DEV = 8
META_TILE = 256
HALO = 32
SMALL_ROWS = 32
VMEM_BYTES_V7X = 64 * 1024 * 1024
MXU_DIM = 256

F32 = jnp.float32
BF16 = jnp.bfloat16
MESH = pl.DeviceIdType.MESH

NN = (((1,), (0,)), ((), ()))
NT = (((1,), (1,)), ((), ()))
TN = (((0,), (0,)), ((), ()))


def _dot(a, b, dims=NN):
    return lax.dot_general(a, b, dims, preferred_element_type=F32)


def _pick_tile(n, target, mult=16):
    best = None
    for t in range(mult, min(n, target) + 1, mult):
        if n % t == 0:
            best = t
    assert best is not None, (n, target)
    return best


def _params(sem=None, vmem_mb=None):
    kw = {}
    if sem is not None:
        kw["dimension_semantics"] = sem
    if vmem_mb is not None:
        kw["vmem_limit_bytes"] = min(vmem_mb * 1024 * 1024, VMEM_BYTES_V7X - 4 * 1024 * 1024)
    return pltpu.CompilerParams(**kw)


def _sigmoid(x):
    return jax.nn.sigmoid(x)


def _dsilu(x, sg):
    return sg * (1.0 + x * (1.0 - sg))


def _decay_tables(heads):
    h = np.arange(heads, dtype=np.float32)
    gamma = (1.0 - np.exp2(-5.0 - h)).astype(np.float32)
    log_g = np.log(gamma).astype(np.float32)
    idx = np.arange(CHUNK, dtype=np.float32)
    rel = idx[:, None] - idx[None, :]
    mask = np.where(rel[None] >= 0, np.exp(np.maximum(rel, 0.0)[None] * log_g[:, None, None]), 0.0)
    qd = np.exp((idx[None, :] + 1.0) * log_g[:, None])
    kd = np.exp((CHUNK - 1.0 - idx[None, :]) * log_g[:, None])
    cd = np.exp(CHUNK * log_g)
    return (mask.astype(np.float32), qd.astype(np.float32)[:, :, None], kd.astype(np.float32)[:, :, None],
            [float(c) for c in cd.astype(np.float32)])


def _gather_weights(shards, block_axes):
    n_arr = len(shards)
    out_shapes = []
    for s, ax in zip(shards, block_axes):
        shp = list(s.shape)
        shp[ax] *= N_DEV
        out_shapes.append(jax.ShapeDtypeStruct(tuple(shp), s.dtype))

    def body(*refs):
        ins, outs = refs[:n_arr], refs[n_arr:2 * n_arr]
        send_sems, recv_sems, local_sems = refs[2 * n_arr:]
        x, y, c = lax.axis_index("x"), lax.axis_index("y"), lax.axis_index("c")
        me, sibling = (x, y, c), (x, y, 1 - c)
        chips = [(1 - x, y), (x, 1 - y), (1 - x, 1 - y)]

        def block(a, dev):
            n = ins[a].shape[block_axes[a]]
            start = pl.multiple_of((4 * dev[0] + 2 * dev[1] + dev[2]) * n, n)
            idx = [slice(None)] * len(ins[a].shape)
            idx[block_axes[a]] = pl.ds(start, n)
            return outs[a].at[tuple(idx)]

        def copy(a, k, dev, to, src=None):
            return pltpu.make_async_remote_copy(
                src_ref=block(a, dev) if src is None else src, dst_ref=block(a, dev),
                send_sem=send_sems.at[a, k], recv_sem=recv_sems.at[a, k],
                device_id=to, device_id_type=MESH)

        mine = [pltpu.make_async_copy(ins[a], block(a, me), local_sems.at[a]) for a in range(n_arr)]
        for cp in mine:
            cp.start()
        first = []
        for a in range(n_arr):
            first.append(copy(a, 0, me, sibling, src=ins[a]))
            first += [copy(a, 1 + j, me, (*chip, c), src=ins[a]) for j, chip in enumerate(chips)]
        for cp in first:
            cp.start()
        passed = []
        for j, chip in enumerate(chips):
            for a in range(n_arr):
                copy(a, 1 + j, (*chip, c), me).wait_recv()
                fwd = copy(a, 4 + j, (*chip, c), sibling)
                fwd.start()
                passed.append(fwd)
        for a in range(n_arr):
            copy(a, 0, sibling, me).wait_recv()
            for j, chip in enumerate(chips):
                copy(a, 4 + j, (*chip, 1 - c), me).wait_recv()
        for cp in first + passed:
            cp.wait_send()
        for cp in mine:
            cp.wait()

    hbm = pl.BlockSpec(memory_space=pl.ANY)
    return pl.pallas_call(
        body, name="gather_weights",
        out_shape=out_shapes,
        in_specs=[hbm] * n_arr, out_specs=[hbm] * n_arr,
        scratch_shapes=[pltpu.SemaphoreType.DMA((n_arr, 7)), pltpu.SemaphoreType.DMA((n_arr, 7)),
                        pltpu.SemaphoreType.DMA((n_arr,))],
    )(*shards)


class _Exchange:
    def __init__(self, parts, block_axes):
        self.block_axes = list(block_axes)
        self.n = len(parts)
        self.out_shape = []
        for p, ax in zip(parts, block_axes):
            shp = list(p.shape)
            if ax is not None:
                assert shp[ax] % N_DEV == 0
                shp[ax] //= N_DEV
            self.out_shape.append(jax.ShapeDtypeStruct((N_DEV, *shp), p.dtype))
        self.scratch = [pltpu.SemaphoreType.DMA((self.n, N_DEV - 1)), pltpu.SemaphoreType.DMA((self.n, N_DEV - 1)),
                        pltpu.SemaphoreType.DMA((self.n,))]
        self.specs = [pl.BlockSpec(memory_space=pl.ANY)] * self.n

    def _copies(self, ins, outs, sems):
        send_sems, recv_sems, local_sems = sems
        x, y, c = lax.axis_index("x"), lax.axis_index("y"), lax.axis_index("c")
        me_idx = 4 * x + 2 * y + c

        def src_block(a, dev_idx):
            ax = self.block_axes[a]
            if ax is None:
                return ins[a]
            n = ins[a].shape[ax] // N_DEV
            idx = [slice(None)] * len(ins[a].shape)
            idx[ax] = pl.ds(pl.multiple_of(dev_idx * n, n), n)
            return ins[a].at[tuple(idx)]

        local = [pltpu.make_async_copy(src_block(a, me_idx), outs[a].at[me_idx], local_sems.at[a])
                 for a in range(self.n)]
        remote = []
        for m in range(1, N_DEV):
            px, py, pc = x ^ ((m >> 2) & 1), y ^ ((m >> 1) & 1), c ^ (m & 1)
            for a in range(self.n):
                remote.append(pltpu.make_async_remote_copy(
                    src_ref=src_block(a, 4 * px + 2 * py + pc), dst_ref=outs[a].at[me_idx],
                    send_sem=send_sems.at[a, m - 1], recv_sem=recv_sems.at[a, m - 1],
                    device_id=(px, py, pc), device_id_type=MESH))
        return local, remote

    def start(self, ins, outs, sems):
        local, remote = self._copies(ins, outs, sems)
        for cp in local + remote:
            cp.start()

    def wait(self, ins, outs, sems):
        local, remote = self._copies(ins, outs, sems)
        for cp in remote:
            cp.wait_recv()
        for cp in remote:
            cp.wait_send()
        for cp in local:
            cp.wait()


def _rms_norm(xs, meta_tile, ln_g):
    SEQ, D = xs.shape
    tm = meta_tile.shape[0]
    n_seq = SEQ // tm

    def body(x_ref, mt_ref, g_ref, hn_ref):
        hv = jnp.where(pl.program_id(0) < n_seq, x_ref[...], mt_ref[...])
        r = lax.rsqrt(jnp.mean(hv * hv, axis=-1, keepdims=True) + EPS)
        hn_ref[...] = (hv * r * g_ref[...]).astype(BF16)

    return pl.pallas_call(
        body, name="rms_norm",
        grid=(n_seq + 1,),
        in_specs=[pl.BlockSpec((tm, D), lambda i: (jnp.minimum(i, n_seq - 1), 0)),
                  pl.BlockSpec((tm, D), lambda i: (0, 0)),
                  pl.BlockSpec((1, D), lambda i: (0, 0))],
        out_specs=pl.BlockSpec((tm, D), lambda i: (i, 0)),
        out_shape=jax.ShapeDtypeStruct((SEQ + tm, D), BF16),
        compiler_params=_params(("arbitrary",), 32),
    )(xs, meta_tile, ln_g)


def _chip_visited(q):
    mine = 2 * lax.axis_index("x") + lax.axis_index("y")
    return mine ^ (((q & 1) << 1) | (q >> 1))


def _in_proj_gather(hn, w_shard, ex, ex_parts):
    R, D = hn.shape
    wb = w_shard.shape[1]
    E, tn = wb * N_DEV, 2 * wb
    n_q = N_DEV // 2
    tm = _pick_tile(R, min(768, R // 2), MXU_DIM)
    n_i = R // tm

    def body(*refs):
        hn_ref, wsh_hbm = refs[:2]
        ex_ins = refs[2:2 + ex.n]
        proj_ref, wg_hbm = refs[2 + ex.n:4 + ex.n]
        ex_outs = refs[4 + ex.n:4 + 2 * ex.n]
        w_vmem, send_sems, recv_sems, local_sem, vmem_sems = refs[4 + 2 * ex.n:9 + 2 * ex.n]
        ex_sems = refs[9 + 2 * ex.n:]
        q, i = pl.program_id(0), pl.program_id(1)
        x, y, c = lax.axis_index("x"), lax.axis_index("y"), lax.axis_index("c")
        me, sibling = (x, y, c), (x, y, 1 - c)
        chips = [(1 - x, y), (x, 1 - y), (1 - x, 1 - y)]

        def block(dev):
            return wg_hbm.at[:, pl.ds(pl.multiple_of((4 * dev[0] + 2 * dev[1] + dev[2]) * wb, wb), wb)]

        def copy(k, dev, to, src=None):
            return pltpu.make_async_remote_copy(
                src_ref=block(dev) if src is None else src, dst_ref=block(dev),
                send_sem=send_sems.at[k], recv_sem=recv_sems.at[k], device_id=to, device_id_type=MESH)

        def to_vmem(p):
            cols = pl.ds(pl.multiple_of(_chip_visited(p) * tn, tn), tn)
            return pltpu.make_async_copy(wg_hbm.at[:, cols], w_vmem.at[p % 2], vmem_sems.at[p % 2])

        mine = pltpu.make_async_copy(wsh_hbm, block(me), local_sem)
        first = [copy(0, me, sibling, src=wsh_hbm)] + [copy(1 + j, me, (*chip, c), src=wsh_hbm)
                                                       for j, chip in enumerate(chips)]
        passed = [copy(4 + j, (*chip, c), sibling) for j, chip in enumerate(chips)]

        @pl.when((q == 0) & (i == 0))
        def _():
            ex.start(ex_ins, ex_outs, ex_sems)
            mine.start()
            for cp in first:
                cp.start()
            mine.wait()
            copy(0, sibling, me).wait_recv()
            to_vmem(0).start()
            to_vmem(0).wait()

        for p in range(1, n_q):
            chip = chips[p - 1]

            @pl.when((q == p - 1) & (i == n_i - 2))
            def _():
                copy(p, (*chip, c), me).wait_recv()
                passed[p - 1].start()

            @pl.when((q == p - 1) & (i == n_i - 1))
            def _():
                copy(3 + p, (*chip, 1 - c), me).wait_recv()
                to_vmem(p).start()

            @pl.when((q == p) & (i == 0))
            def _():
                to_vmem(p).wait()

        proj_ref[...] = _dot(hn_ref[...], w_vmem[q % 2]).astype(BF16)

        @pl.when((q == n_q - 1) & (i == n_i - 1))
        def _():
            for cp in first + passed:
                cp.wait_send()
            ex.wait(ex_ins, ex_outs, ex_sems)

    hbm = pl.BlockSpec(memory_space=pl.ANY)
    outs = pl.pallas_call(
        body, name="in_proj",
        grid=(n_q, n_i),
        in_specs=[pl.BlockSpec((tm, D), lambda q, i: (i, 0)), hbm] + ex.specs,
        out_specs=[pl.BlockSpec((tm, tn), lambda q, i: (i, _chip_visited(q))), hbm] + ex.specs,
        out_shape=[jax.ShapeDtypeStruct((R, E), BF16), jax.ShapeDtypeStruct((D, E), BF16)] + ex.out_shape,
        scratch_shapes=[pltpu.VMEM((2, D, tn), BF16), pltpu.SemaphoreType.DMA((7,)), pltpu.SemaphoreType.DMA((7,)),
                        pltpu.SemaphoreType.DMA, pltpu.SemaphoreType.DMA((2,))] + ex.scratch,
        compiler_params=_params(("arbitrary", "arbitrary"), 48),
    )(hn, w_shard, *ex_parts)
    return outs[0], outs[1], outs[2:]


def _rot(t, cos, sin, half):
    t1, t2 = t[:, :half], t[:, half:]
    return jnp.concatenate([t1 * cos - t2 * sin, t1 * sin + t2 * cos], axis=-1)


def _rot_inv(t, cos, sin, half):
    t1, t2 = t[:, :half], t[:, half:]
    return jnp.concatenate([t1 * cos + t2 * sin, t2 * cos - t1 * sin], axis=-1)


def _chunk_order(n_chunks):
    lead = META_TILE // CHUNK
    return lambda l: (l + n_chunks - lead) % n_chunks


def _retention_fwd(proj, cos, sin, gn_g, mix):
    R, E = proj.shape
    RW = gn_g.shape[1]
    H = RET_HEADS
    hd = RW // H
    half = hd // 2
    NC = R // CHUNK
    mask, qd, kd, cd = _decay_tables(H)
    scale = float(hd) ** -0.5
    phys = _chunk_order(NC)

    def body(p_ref, cos_ref, sin_ref, mask_ref, qd_ref, kd_ref, gn_ref, y_ref, st_ref, state):
        @pl.when(pl.program_id(0) == 0)
        def _():
            state[...] = jnp.zeros_like(state)

        cs, sn = cos_ref[...], sin_ref[...]
        hs = range(H)
        col = lambda j, h: slice(j * RW + h * hd, j * RW + (h + 1) * hd)
        qr = [_rot(p_ref[:, col(0, h)].astype(F32), cs, sn, half) for h in hs]
        kr = [_rot(p_ref[:, col(1, h)].astype(F32), cs, sn, half) * scale for h in hs]
        v = [p_ref[:, col(2, h)] for h in hs]
        s_prev = [state[h] for h in hs]
        s_prev_b = [s_prev[h].astype(BF16) for h in hs]
        s = [(_dot(qr[h].astype(BF16), kr[h].astype(BF16), NT) * mask_ref[h]).astype(BF16) for h in hs]
        y_raw = [_dot(s[h], v[h]) + _dot((qr[h] * qd_ref[h]).astype(BF16), s_prev_b[h]) for h in hs]
        s_new = [s_prev[h] * cd[h] + _dot((kr[h] * kd_ref[h]).astype(BF16), v[h], TN) for h in hs]
        for h in hs:
            st_ref[0, h] = s_prev_b[h]
            state[h] = s_new[h]
        for h in hs:
            g = p_ref[:, col(3, h)].astype(F32)
            mu = jnp.mean(y_raw[h], axis=-1, keepdims=True)
            yc = y_raw[h] - mu
            var = jnp.mean(yc * yc, axis=-1, keepdims=True)
            out = yc * lax.rsqrt(var + EPS) * gn_ref[:, col(0, h)] * (g * _sigmoid(g))
            y_ref[:, col(0, h)] = out.astype(BF16)

    const3 = lambda l: (0, 0, 0)
    return pl.pallas_call(
        body, name="retention_fwd",
        grid=(NC,),
        in_specs=[pl.BlockSpec((CHUNK, 4 * RW), lambda l: (phys(l), 0)),
                  pl.BlockSpec((CHUNK, half), lambda l: (phys(l), 0)),
                  pl.BlockSpec((CHUNK, half), lambda l: (phys(l), 0)),
                  pl.BlockSpec((H, CHUNK, CHUNK), const3),
                  pl.BlockSpec((H, CHUNK, 1), const3),
                  pl.BlockSpec((H, CHUNK, 1), const3),
                  pl.BlockSpec((1, RW), lambda l: (0, 0))],
        out_specs=[pl.BlockSpec((CHUNK, RW), lambda l: (phys(l), 0)),
                   pl.BlockSpec((1, H, hd, hd), lambda l: (phys(l), 0, 0, 0))],
        out_shape=[jax.ShapeDtypeStruct((R, mix), BF16), jax.ShapeDtypeStruct((NC, H, hd, hd), BF16)],
        scratch_shapes=[pltpu.VMEM((H, hd, hd), F32)],
        compiler_params=_params(("arbitrary",), 32),
    )(proj, cos, sin, jnp.asarray(mask), jnp.asarray(qd), jnp.asarray(kd), gn_g)


CONV_ROWS = 32
CONV_LANES = 512
DW_LANES = 256


def _conv_order(n_tiles):
    return lambda l: (l + n_tiles - 1) % n_tiles


def _halo_block(n_tiles, tm):
    per = tm // HALO
    return lambda l: ((l + n_tiles - 2) % n_tiles) * per + per - 1


def _fill_shifted(src, dst):
    rows, width = dst.shape[1], dst.shape[2]
    step = _pick_tile(rows, 64, 8)
    for r in range(1, 8):
        for r0 in range(0, rows, step):
            for l0 in range(0, width, CONV_LANES):
                dst[r - 1, r0:r0 + step, l0:l0 + CONV_LANES] = src[r + r0:r + r0 + step, l0:l0 + CONV_LANES]


def _at_offset(src, shifted, off, r0, rows, lanes):
    r = off % 8
    a = off - r + r0
    if r == 0:
        return src[a:a + rows, lanes]
    return shifted[r - 1, a:a + rows, lanes]


def _fill_glu(first, a_ref, b_ref, ah_ref, bh_ref, u_ext, tm):
    uh = ah_ref[...].astype(F32) * _sigmoid(bh_ref[...].astype(F32))
    u_ext[0:HALO, :] = jnp.where(first, 0.0, uh)
    u_ext[HALO:HALO + tm, :] = a_ref[...].astype(F32) * _sigmoid(b_ref[...].astype(F32))


def _layer_norm(cv, lg_ref, lb_ref):
    mu = jnp.mean(cv, axis=-1, keepdims=True)
    cc = cv - mu
    rstd = lax.rsqrt(jnp.mean(cc * cc, axis=-1, keepdims=True) + EPS)
    xh = cc * rstd
    return xh, rstd, xh * lg_ref[...] + lb_ref[...]


def _conv_fwd(proj, y_in, dw_w, dw_b, ln_g, ln_b, pw_w, pw_b):
    R, E = proj.shape
    CW = pw_w.shape[0]
    tm = META_TILE
    NTL = R // tm
    phys = _conv_order(NTL)
    halo = _halo_block(NTL, tm)
    cb = (E - 3 * CW) // CW
    base = HALO - (CONV_K - 1)

    def body(a_ref, b_ref, g_ref, ah_ref, bh_ref, w_ref, wb_ref, lg_ref, lb_ref, pw_ref, pb_ref, yin_ref,
             y_ref, c_ref, u_ext, u_sh):
        _fill_glu(pl.program_id(0) == 0, a_ref, b_ref, ah_ref, bh_ref, u_ext, tm)
        _fill_shifted(u_ext, u_sh)
        for r0 in range(0, tm, CONV_ROWS):
            for l0 in range(0, CW, CONV_LANES):
                lanes = slice(l0, l0 + CONV_LANES)
                acc = None
                for k in range(CONV_K):
                    term = _at_offset(u_ext, u_sh, base + k, r0, CONV_ROWS, lanes) * w_ref[k:k + 1, lanes]
                    acc = term if acc is None else acc + term
                c_ref[r0:r0 + CONV_ROWS, lanes] = acc + wb_ref[:, lanes]
        _, _, ln = _layer_norm(c_ref[...], lg_ref, lb_ref)
        s = (ln * _sigmoid(ln)).astype(BF16)
        upw = _dot(s, pw_ref[...]) + pb_ref[...]
        g = g_ref[...].astype(F32)
        y_ref[...] = (upw * (g * _sigmoid(g))).astype(BF16)

    row = lambda l: (0, 0)
    return pl.pallas_call(
        body, name="conv_fwd",
        grid=(NTL,),
        in_specs=[pl.BlockSpec((tm, CW), lambda l: (phys(l), cb)),
                  pl.BlockSpec((tm, CW), lambda l: (phys(l), cb + 1)),
                  pl.BlockSpec((tm, CW), lambda l: (phys(l), cb + 2)),
                  pl.BlockSpec((HALO, CW), lambda l: (halo(l), cb)),
                  pl.BlockSpec((HALO, CW), lambda l: (halo(l), cb + 1)),
                  pl.BlockSpec((HALO, CW), row),
                  pl.BlockSpec((1, CW), row), pl.BlockSpec((1, CW), row), pl.BlockSpec((1, CW), row),
                  pl.BlockSpec((CW, CW), row),
                  pl.BlockSpec((1, CW), row),
                  pl.BlockSpec(memory_space=pl.ANY)],
        out_specs=[pl.BlockSpec((tm, CW), lambda l: (phys(l), 1)),
                   pl.BlockSpec((tm, CW), lambda l: (phys(l), 0))],
        out_shape=[jax.ShapeDtypeStruct(y_in.shape, BF16), jax.ShapeDtypeStruct((R, CW), F32)],
        input_output_aliases={11: 0},
        scratch_shapes=[pltpu.VMEM((HALO + tm, CW), F32), pltpu.VMEM((7, tm + HALO - 8, CW), F32)],
        compiler_params=_params(("arbitrary",), 48),
    )(proj, proj, proj, proj, proj, dw_w, dw_b, ln_g, ln_b, pw_w, pw_b, y_in)


def _out_proj_loss(xs, meta_tile, y, w_out, final_g, target):
    SEQ, D = xs.shape
    R, MIX = y.shape
    tm = META_TILE
    n_seq = SEQ // tm
    n_tiles = R // tm
    rows_out = _pick_tile(MIX, 256)

    def body(x_ref, mt_ref, y_ref, w_hbm, fg_ref, t_ref, dh2_ref, dy_ref, dwo_hbm, dfg_ref, loss_ref,
             w_scr, acc, stage, sem):
        i = pl.program_id(0)

        @pl.when(i == 0)
        def _():
            cp = pltpu.make_async_copy(w_hbm, w_scr, sem)
            cp.start()
            acc[...] = jnp.zeros_like(acc)
            dfg_ref[...] = jnp.zeros_like(dfg_ref)
            loss_ref[...] = jnp.zeros_like(loss_ref)
            cp.wait()

        yb = y_ref[...]
        h2 = jnp.where(i < n_seq, x_ref[...], mt_ref[...]) + _dot(yb, w_scr[...])
        r2 = lax.rsqrt(jnp.mean(h2 * h2, axis=-1, keepdims=True) + EPS)
        n = h2 * r2
        fg = fg_ref[...]
        err = jnp.where(i < n_seq, n * fg - t_ref[...], 0.0)
        loss_ref[...] += 0.5 * jnp.sum(jnp.mean(err * err, axis=-1, keepdims=True), axis=0, keepdims=True)
        dout = err * (1.0 / D)
        dfg_ref[...] += jnp.sum(dout * n, axis=0, keepdims=True)
        dn = dout * fg
        dh2 = r2 * (dn - n * jnp.mean(dn * n, axis=-1, keepdims=True))
        dh2_ref[...] = dh2
        dh2b = dh2.astype(BF16)
        dy_ref[...] = _dot(dh2b, w_scr[...], NT).astype(BF16)
        acc[...] += _dot(yb, dh2b, TN)

        @pl.when(i == n_tiles - 1)
        def _():
            for r in range(0, MIX, rows_out):
                stage[...] = acc[r:r + rows_out, :].astype(BF16)
                cp = pltpu.make_async_copy(stage, dwo_hbm.at[r:r + rows_out, :], sem)
                cp.start()
                cp.wait()

    row = lambda i: (0, 0)
    return pl.pallas_call(
        body, name="out_proj_loss",
        grid=(n_tiles,),
        in_specs=[pl.BlockSpec((tm, D), lambda i: (jnp.minimum(i, n_seq - 1), 0)),
                  pl.BlockSpec((tm, D), row),
                  pl.BlockSpec((tm, MIX), lambda i: (i, 0)),
                  pl.BlockSpec(memory_space=pl.ANY),
                  pl.BlockSpec((1, D), row),
                  pl.BlockSpec((tm, D), lambda i: (jnp.minimum(i, n_seq - 1), 0))],
        out_specs=[pl.BlockSpec((tm, D), lambda i: (i, 0)),
                   pl.BlockSpec((tm, MIX), lambda i: (i, 0)),
                   pl.BlockSpec(memory_space=pl.ANY),
                   pl.BlockSpec((1, D), row),
                   pl.BlockSpec((1, 1), row)],
        out_shape=[jax.ShapeDtypeStruct((R, D), F32), jax.ShapeDtypeStruct((R, MIX), BF16),
                   jax.ShapeDtypeStruct((MIX, D), BF16), jax.ShapeDtypeStruct((1, D), F32),
                   jax.ShapeDtypeStruct((1, 1), F32)],
        scratch_shapes=[pltpu.VMEM((MIX, D), BF16), pltpu.VMEM((MIX, D), F32), pltpu.VMEM((rows_out, D), BF16),
                        pltpu.SemaphoreType.DMA],
        compiler_params=_params(("arbitrary",), 60),
    )(xs, meta_tile, y, w_out, final_g, target)


def _conv_bwd(proj, conv_out, dy, dw_w, ln_g, ln_b, pw_w, pw_b):
    R, E = proj.shape
    CW = pw_w.shape[0]
    tm = META_TILE
    NTL = R // tm
    order = _conv_order(NTL)
    phys = lambda i: order(NTL - 1 - i)
    halo_l = _halo_block(NTL, tm)
    halo = lambda i: halo_l(NTL - 1 - i)
    cb = (E - 3 * CW) // CW
    base = HALO - (CONV_K - 1)

    def body(a_ref, b_ref, g_ref, ah_ref, bh_ref, c_ref, dy_ref, w_ref, lg_ref, lb_ref, pw_ref, pb_ref,
             dp_ref, dpw_ref, dww_ref, vec_ref, u_ext, u_sh, dc_ext, dc_sh, du_scr, dww_acc, dpw_acc):
        i = pl.program_id(0)

        @pl.when(i == 0)
        def _():
            dpw_acc[...] = jnp.zeros_like(dpw_acc)
            dww_ref[...] = jnp.zeros_like(dww_ref)
            vec_ref[...] = jnp.zeros_like(vec_ref)
            dww_acc[...] = jnp.zeros_like(dww_acc)
            dc_ext[tm:tm + HALO, :] = jnp.zeros((HALO, CW), F32)

        _fill_glu(i == NTL - 1, a_ref, b_ref, ah_ref, bh_ref, u_ext, tm)
        _fill_shifted(u_ext, u_sh)
        xh, rstd, ln = _layer_norm(c_ref[...], lg_ref, lb_ref)
        sg = _sigmoid(ln)
        sb = (ln * sg).astype(BF16)
        upw = _dot(sb, pw_ref[...]) + pb_ref[...]
        g = g_ref[...].astype(F32)
        sgg = _sigmoid(g)
        dyc = dy_ref[...].astype(F32)
        dp_ref[:, 2 * CW:3 * CW] = (dyc * upw * _dsilu(g, sgg)).astype(BF16)
        dupw = dyc * (g * sgg)
        dupw_b = dupw.astype(BF16)
        vec_ref[0:1, :] += jnp.sum(dupw, axis=0, keepdims=True)
        dpw_acc[...] += _dot(sb, dupw_b, TN)
        dln = _dot(dupw_b, pw_ref[...], NT) * _dsilu(ln, sg)
        vec_ref[1:2, :] += jnp.sum(dln * xh, axis=0, keepdims=True)
        vec_ref[2:3, :] += jnp.sum(dln, axis=0, keepdims=True)
        dxh = dln * lg_ref[...]
        dc = rstd * (dxh - jnp.mean(dxh, axis=-1, keepdims=True) - xh * jnp.mean(dxh * xh, axis=-1, keepdims=True))
        vec_ref[3:4, :] += jnp.sum(dc, axis=0, keepdims=True)
        dc_ext[0:tm, :] = dc
        _fill_shifted(dc_ext, dc_sh)

        for l0 in range(0, CW, CONV_LANES):
            lanes = slice(l0, l0 + CONV_LANES)
            for r0 in range(0, tm, CONV_ROWS):
                acc = None
                for k in range(CONV_K):
                    term = _at_offset(dc_ext, dc_sh, CONV_K - 1 - k, r0, CONV_ROWS, lanes) * w_ref[k:k + 1, lanes]
                    acc = term if acc is None else acc + term
                du_scr[r0:r0 + CONV_ROWS, lanes] = acc
        for l0 in range(0, CW, DW_LANES):
            lanes = slice(l0, l0 + DW_LANES)
            for r0 in range(0, tm, CONV_ROWS):
                dcb = dc_ext[r0:r0 + CONV_ROWS, lanes]
                for k in range(CONV_K):
                    prod = dcb * _at_offset(u_ext, u_sh, base + k, r0, CONV_ROWS, lanes)
                    part = prod[0:8]
                    for q in range(8, CONV_ROWS, 8):
                        part = part + prod[q:q + 8]
                    dww_acc[k, :, lanes] += part

        du = du_scr[...]
        a = a_ref[...].astype(F32)
        sgb = _sigmoid(b_ref[...].astype(F32))
        dp_ref[:, 0:CW] = (du * sgb).astype(BF16)
        dp_ref[:, CW:2 * CW] = (du * a * sgb * (1.0 - sgb)).astype(BF16)
        dc_ext[tm:tm + HALO, :] = dc_ext[0:HALO, :]

        @pl.when(i == NTL - 1)
        def _():
            for k in range(CONV_K):
                dww_ref[k:k + 1, :] = jnp.sum(dww_acc[k], axis=0, keepdims=True)
            dpw_ref[...] = dpw_acc[...].astype(BF16)

    row = lambda i: (0, 0)
    return pl.pallas_call(
        body, name="conv_bwd",
        grid=(NTL,),
        in_specs=[pl.BlockSpec((tm, CW), lambda i: (phys(i), cb)),
                  pl.BlockSpec((tm, CW), lambda i: (phys(i), cb + 1)),
                  pl.BlockSpec((tm, CW), lambda i: (phys(i), cb + 2)),
                  pl.BlockSpec((HALO, CW), lambda i: (halo(i), cb)),
                  pl.BlockSpec((HALO, CW), lambda i: (halo(i), cb + 1)),
                  pl.BlockSpec((tm, CW), lambda i: (phys(i), 0)),
                  pl.BlockSpec((tm, CW), lambda i: (phys(i), 1)),
                  pl.BlockSpec((HALO, CW), row),
                  pl.BlockSpec((1, CW), row), pl.BlockSpec((1, CW), row),
                  pl.BlockSpec((CW, CW), row),
                  pl.BlockSpec((1, CW), row)],
        out_specs=[pl.BlockSpec((tm, 3 * CW), lambda i: (phys(i), 0)),
                   pl.BlockSpec((CW, CW), row),
                   pl.BlockSpec((HALO, CW), row),
                   pl.BlockSpec((8, CW), row)],
        out_shape=[jax.ShapeDtypeStruct((R, 3 * CW), BF16), jax.ShapeDtypeStruct((CW, CW), BF16),
                   jax.ShapeDtypeStruct((HALO, CW), F32), jax.ShapeDtypeStruct((8, CW), F32)],
        scratch_shapes=[pltpu.VMEM((HALO + tm, CW), F32), pltpu.VMEM((7, tm + HALO - 8, CW), F32),
                        pltpu.VMEM((tm + HALO, CW), F32), pltpu.VMEM((7, tm + HALO - 8, CW), F32),
                        pltpu.VMEM((tm, CW), F32), pltpu.VMEM((CONV_K, 8, CW), F32), pltpu.VMEM((CW, CW), F32)],
        compiler_params=_params(("arbitrary",), 60),
    )(proj, proj, proj, proj, proj, conv_out, dy, dw_w, ln_g, ln_b, pw_w, pw_b)


def _retention_bwd(proj, cos, sin, gn_g, states, dy, ex, ex_parts):
    R, E = proj.shape
    RW = gn_g.shape[1]
    H = RET_HEADS
    hd = RW // H
    half = hd // 2
    NC = R // CHUNK
    mask, qd, kd, cd = _decay_tables(H)
    scale = float(hd) ** -0.5
    order = _chunk_order(NC)
    phys = lambda i: order(NC - 1 - i)

    def body(*refs):
        p_ref, cos_ref, sin_ref, mask_ref, qd_ref, kd_ref, gn_ref, st_ref, dy_ref = refs[:9]
        ex_ins = refs[9:9 + ex.n]
        dp_ref, dgn_ref = refs[9 + ex.n:11 + ex.n]
        ex_outs = refs[11 + ex.n:11 + 2 * ex.n]
        dstate = refs[11 + 2 * ex.n]
        sems = refs[12 + 2 * ex.n:]

        @pl.when(pl.program_id(0) == 0)
        def _():
            ex.start(ex_ins, ex_outs, sems)
            dstate[...] = jnp.zeros_like(dstate)
            dgn_ref[...] = jnp.zeros_like(dgn_ref)

        cs, sn = cos_ref[...], sin_ref[...]
        hs = range(H)
        col = lambda j, h: slice(j * RW + h * hd, j * RW + (h + 1) * hd)
        qr = [_rot(p_ref[:, col(0, h)].astype(F32), cs, sn, half) for h in hs]
        kr = [_rot(p_ref[:, col(1, h)].astype(F32), cs, sn, half) * scale for h in hs]
        v = [p_ref[:, col(2, h)] for h in hs]
        qb = [qr[h].astype(BF16) for h in hs]
        kb = [kr[h].astype(BF16) for h in hs]
        qdb = [(qr[h] * qd_ref[h]).astype(BF16) for h in hs]
        kdb = [(kr[h] * kd_ref[h]).astype(BF16) for h in hs]
        s_prev = [st_ref[0, h] for h in hs]
        dst = [dstate[h] for h in hs]
        dstb = [dst[h].astype(BF16) for h in hs]
        sb = [(_dot(qb[h], kb[h], NT) * mask_ref[h]).astype(BF16) for h in hs]
        y_raw = [_dot(sb[h], v[h]) + _dot(qdb[h], s_prev[h]) for h in hs]
        dyrb, dg = [], []
        for h in hs:
            g = p_ref[:, col(3, h)].astype(F32)
            mu = jnp.mean(y_raw[h], axis=-1, keepdims=True)
            yc = y_raw[h] - mu
            rstd = lax.rsqrt(jnp.mean(yc * yc, axis=-1, keepdims=True) + EPS)
            xh = yc * rstd
            gn = gn_ref[:, col(0, h)]
            sg = _sigmoid(g)
            dyh = dy_ref[:, col(0, h)].astype(F32)
            dg.append((dyh * (xh * gn) * _dsilu(g, sg)).astype(BF16))
            dyn = dyh * (g * sg)
            dgn_ref[:, col(0, h)] += jnp.sum(dyn * xh, axis=0, keepdims=True)
            dxh = dyn * gn
            dyr = rstd * (dxh - jnp.mean(dxh, axis=-1, keepdims=True)
                          - xh * jnp.mean(dxh * xh, axis=-1, keepdims=True))
            dyrb.append(dyr.astype(BF16))
        dsb = [(_dot(dyrb[h], v[h], NT) * mask_ref[h]).astype(BF16) for h in hs]
        dqr = [_dot(dsb[h], kb[h]) + _dot(dyrb[h], s_prev[h], NT) * qd_ref[h] for h in hs]
        dkr = [_dot(dsb[h], qb[h], TN) + _dot(v[h], dstb[h], NT) * kd_ref[h] for h in hs]
        dv = [_dot(sb[h], dyrb[h], TN) + _dot(kdb[h], dstb[h]) for h in hs]
        dst_new = [dst[h] * cd[h] + _dot(qdb[h], dyrb[h], TN) for h in hs]
        for h in hs:
            dstate[h] = dst_new[h]
            dp_ref[:, col(0, h)] = _rot_inv(dqr[h], cs, sn, half).astype(BF16)
            dp_ref[:, col(1, h)] = (_rot_inv(dkr[h], cs, sn, half) * scale).astype(BF16)
            dp_ref[:, col(2, h)] = dv[h].astype(BF16)
            dp_ref[:, col(3, h)] = dg[h]

        @pl.when(pl.program_id(0) == NC - 1)
        def _():
            ex.wait(ex_ins, ex_outs, sems)

    const3 = lambda i: (0, 0, 0)
    outs = pl.pallas_call(
        body, name="retention_bwd",
        grid=(NC,),
        in_specs=[pl.BlockSpec((CHUNK, 4 * RW), lambda i: (phys(i), 0)),
                  pl.BlockSpec((CHUNK, half), lambda i: (phys(i), 0)),
                  pl.BlockSpec((CHUNK, half), lambda i: (phys(i), 0)),
                  pl.BlockSpec((H, CHUNK, CHUNK), const3),
                  pl.BlockSpec((H, CHUNK, 1), const3),
                  pl.BlockSpec((H, CHUNK, 1), const3),
                  pl.BlockSpec((1, RW), lambda i: (0, 0)),
                  pl.BlockSpec((1, H, hd, hd), lambda i: (phys(i), 0, 0, 0)),
                  pl.BlockSpec((CHUNK, RW), lambda i: (phys(i), 0))] + ex.specs,
        out_specs=[pl.BlockSpec((CHUNK, 4 * RW), lambda i: (phys(i), 0)),
                   pl.BlockSpec((1, RW), lambda i: (0, 0))] + ex.specs,
        out_shape=[jax.ShapeDtypeStruct((R, 4 * RW), BF16), jax.ShapeDtypeStruct((1, RW), F32)] + ex.out_shape,
        scratch_shapes=[pltpu.VMEM((H, hd, hd), F32)] + ex.scratch,
        compiler_params=_params(("arbitrary",), 32),
    )(proj, cos, sin, jnp.asarray(mask), jnp.asarray(qd), jnp.asarray(kd), gn_g, states, dy, *ex_parts)
    return outs[0], outs[1], outs[2:]


def _dproj_specs(tk, tn, n_ret, tile_axis, col_axis):
    def ret_map(*ids):
        t, j = ids[tile_axis], ids[col_axis]
        return (jnp.where(j < n_ret, t, 0), jnp.minimum(j, n_ret - 1))

    def conv_map(*ids):
        t, j = ids[tile_axis], ids[col_axis]
        return (jnp.where(j >= n_ret, t, 0), jnp.maximum(j - n_ret, 0))

    return pl.BlockSpec((tk, tn), ret_map), pl.BlockSpec((tk, tn), conv_map)


def _w_in_grad(hn, dp_ret, dp_conv):
    R, D = hn.shape
    tn = _pick_tile(dp_conv.shape[1] // 3, 1024, 128)
    n_ret, n_conv = dp_ret.shape[1] // tn, dp_conv.shape[1] // tn
    E = dp_ret.shape[1] + dp_conv.shape[1]
    tk = _pick_tile(R, 1024, MXU_DIM)
    n_t = R // tk
    ret_spec, conv_spec = _dproj_specs(tk, tn, n_ret, 1, 0)

    def body(hn_ref, r_ref, c_ref, out_ref, acc):
        j, t = pl.program_id(0), pl.program_id(1)

        @pl.when(t == 0)
        def _():
            acc[...] = jnp.zeros_like(acc)

        @pl.when(j < n_ret)
        def _():
            acc[...] += _dot(hn_ref[...], r_ref[...], TN)

        @pl.when(j >= n_ret)
        def _():
            acc[...] += _dot(hn_ref[...], c_ref[...], TN)

        @pl.when(t == n_t - 1)
        def _():
            out_ref[...] = acc[...].astype(BF16)

    return pl.pallas_call(
        body, name="w_in_grad",
        grid=(n_ret + n_conv, n_t),
        in_specs=[pl.BlockSpec((tk, D), lambda j, t: (t, 0)), ret_spec, conv_spec],
        out_specs=pl.BlockSpec((D, tn), lambda j, t: (0, j)),
        out_shape=jax.ShapeDtypeStruct((D, E), BF16),
        scratch_shapes=[pltpu.VMEM((D, tn), F32)],
        compiler_params=_params(("arbitrary", "arbitrary"), 48),
    )(hn, dp_ret, dp_conv)


def _h_grad(name, dp_ret, dp_conv, w_in, h, dh2, ln_g, row0, dlg_init=None, ex=None, ex_parts=()):
    n, D = h.shape
    tn = _pick_tile(dp_conv.shape[1] // 3, 1024, 128)
    n_ret, n_conv = dp_ret.shape[1] // tn, dp_conv.shape[1] // tn
    n_k = n_ret + n_conv
    tm = _pick_tile(n, 1024)
    te = _pick_tile(tm, 256)
    n_e = tm // te
    assert row0 % tm == 0
    b0, e0 = row0 // tm, row0 // te

    def ret_map(t, k):
        return (jnp.where(k < n_ret, b0 + t, b0), jnp.minimum(k, n_ret - 1))

    def conv_map(t, k):
        return (jnp.where(k >= n_ret, b0 + t, b0), jnp.clip(k - n_ret, 0, n_conv - 1))

    def rows_e(first):
        return lambda t, k: (first + t * n_e + jnp.maximum(k - n_k, 0), 0)

    n_ex = 0 if ex is None else ex.n
    n_t = n // tm

    def body(*refs):
        r_ref, c_ref, w_ref, h_ref, dh2_ref, g_ref, init_ref = refs[:7]
        ex_ins = refs[7:7 + n_ex]
        o = 7 + n_ex
        dh_ref, dlg_ref = refs[o:o + 2]
        ex_outs = refs[o + 2:o + 2 + n_ex]
        acc = refs[o + 2 + n_ex]
        sems = refs[o + 3 + n_ex:]
        t, k = pl.program_id(0), pl.program_id(1)
        if ex is not None:
            @pl.when((k == 0) & (t == 0))
            def _():
                ex.start(ex_ins, ex_outs, sems)

        @pl.when(k == 0)
        def _():
            acc[...] = jnp.zeros_like(acc)

        @pl.when((k == 0) & (t == 0))
        def _():
            dlg_ref[...] = init_ref[...]

        @pl.when(k < n_ret)
        def _():
            acc[...] += _dot(r_ref[...], w_ref[...], NT)

        @pl.when((k >= n_ret) & (k < n_k))
        def _():
            acc[...] += _dot(c_ref[...], w_ref[...], NT)

        @pl.when(k >= n_k)
        def _():
            hv = h_ref[...]
            r = lax.rsqrt(jnp.mean(hv * hv, axis=-1, keepdims=True) + EPS)
            nrm = hv * r
            dhn = acc[pl.ds(pl.multiple_of((k - n_k) * te, te), te), :]
            dlg_ref[...] += jnp.sum(dhn * nrm, axis=0, keepdims=True)
            dn = dhn * g_ref[...]
            dh_ref[...] = dh2_ref[...] + r * (dn - nrm * jnp.mean(dn * nrm, axis=-1, keepdims=True))

        if ex is not None:
            @pl.when((k == n_k + n_e - 1) & (t == n_t - 1))
            def _():
                ex.wait(ex_ins, ex_outs, sems)

    row = lambda t, k: (0, 0)
    if dlg_init is None:
        dlg_init = jnp.zeros((1, D), F32)
    ex_specs, ex_shape, ex_scratch = ([], [], []) if ex is None else (ex.specs, ex.out_shape, ex.scratch)
    outs = pl.pallas_call(
        body, name=name,
        grid=(n_t, n_k + n_e),
        in_specs=[pl.BlockSpec((tm, tn), ret_map), pl.BlockSpec((tm, tn), conv_map),
                  pl.BlockSpec((D, tn), lambda t, k: (0, jnp.minimum(k, n_k - 1))),
                  pl.BlockSpec((te, D), rows_e(0)),
                  pl.BlockSpec((te, D), rows_e(e0)),
                  pl.BlockSpec((1, D), row), pl.BlockSpec((1, D), row)] + ex_specs,
        out_specs=[pl.BlockSpec((te, D), rows_e(0)),
                   pl.BlockSpec((1, D), row)] + ex_specs,
        out_shape=[jax.ShapeDtypeStruct((n, D), F32), jax.ShapeDtypeStruct((1, D), F32)] + ex_shape,
        scratch_shapes=[pltpu.VMEM((tm, D), F32)] + ex_scratch,
        compiler_params=_params(("arbitrary", "arbitrary"), 60),
    )(dp_ret, dp_conv, w_in, h, dh2, ln_g, dlg_init, *ex_parts)
    return outs[0], outs[1], outs[2:]


def _adamw(w, g, m, v):
    m = ADAM_B1 * m + (1.0 - ADAM_B1) * g
    v = ADAM_B2 * v + (1.0 - ADAM_B2) * (g * g)
    m_hat = m / (1.0 - ADAM_B1 ** ADAM_STEP)
    v_hat = v / (1.0 - ADAM_B2 ** ADAM_STEP)
    delta = -ADAM_LR * (m_hat / (jnp.sqrt(v_hat) + ADAM_EPS) + ADAM_WD * w)
    return delta, m, v


def _sum_slots(ref):
    g = ref[0].astype(F32)
    for s in range(1, N_DEV):
        g = g + ref[s].astype(F32)
    return g


def _sum_adamw(name, parts, w, m, v, rows_target, ex=None, ex_parts=()):
    rows, cols = w.shape
    tr = _pick_tile(rows, rows_target, 8)
    n_ex = 0 if ex is None else ex.n
    n_steps = rows // tr

    def body(*refs):
        p_ref, w_ref, m_ref, v_ref = refs[:4]
        ex_ins = refs[4:4 + n_ex]
        o = 4 + n_ex
        g_ref, d_ref, nm_ref, nv_ref = refs[o:o + 4]
        ex_outs, sems = refs[o + 4:o + 4 + n_ex], refs[o + 4 + n_ex:]
        if ex is not None:
            @pl.when(pl.program_id(0) == 0)
            def _():
                ex.start(ex_ins, ex_outs, sems)

        g = _sum_slots(p_ref)
        d, nm, nv = _adamw(w_ref[...], g, m_ref[...], v_ref[...])
        g_ref[...] = g
        d_ref[...] = d
        nm_ref[...] = nm
        nv_ref[...] = nv
        if ex is not None:
            @pl.when(pl.program_id(0) == n_steps - 1)
            def _():
                ex.wait(ex_ins, ex_outs, sems)

    tile = pl.BlockSpec((tr, cols), lambda i: (i, 0))
    ex_specs, ex_shape, ex_scratch = ([], [], []) if ex is None else (ex.specs, ex.out_shape, ex.scratch)
    outs = pl.pallas_call(
        body, name=name,
        grid=(n_steps,),
        in_specs=[pl.BlockSpec((N_DEV, tr, cols), lambda i: (0, i, 0)), tile, tile, tile] + ex_specs,
        out_specs=[tile] * 4 + ex_specs,
        out_shape=[jax.ShapeDtypeStruct((rows, cols), F32)] * 4 + ex_shape,
        scratch_shapes=ex_scratch,
        compiler_params=_params(("arbitrary",), 40),
    )(parts, w, m, v, *ex_parts)
    return outs[:4], outs[4:]


def _sum_adamw_small(parts_list, w_list, m_list, v_list, loss_parts):
    n = len(w_list)

    def body(*refs):
        p_refs, w_refs, m_refs, v_refs = refs[:n], refs[n:2 * n], refs[2 * n:3 * n], refs[3 * n:4 * n]
        lp_ref = refs[4 * n]
        outs = refs[4 * n + 1:]
        for a in range(n):
            g = _sum_slots(p_refs[a])
            d, nm, nv = _adamw(w_refs[a][...], g, m_refs[a][...], v_refs[a][...])
            outs[4 * a][...] = g
            outs[4 * a + 1][...] = d
            outs[4 * a + 2][...] = nm
            outs[4 * a + 3][...] = nv
        outs[4 * n][...] = _sum_slots(lp_ref)

    out_shape = []
    for w in w_list:
        out_shape += [jax.ShapeDtypeStruct(w.shape, F32)] * 4
    out_shape.append(jax.ShapeDtypeStruct(loss_parts.shape[1:], F32))
    return pl.pallas_call(body, name="sum_adamw_small", out_shape=out_shape)(
        *parts_list, *w_list, *m_list, *v_list, loss_parts)


def kernel(x, meta_tokens, ln_g, w_in, ret_gn_g, conv_dw_w, conv_dw_b, conv_ln_g, conv_ln_b, conv_pw_w, conv_pw_b, w_out, final_g, loss_target, m_meta_tokens, m_ln_g, m_w_in, m_ret_gn_g, m_conv_dw_w, m_conv_dw_b, m_conv_ln_g, m_conv_ln_b, m_conv_pw_w, m_conv_pw_b, m_w_out, m_final_g, v_meta_tokens, v_ln_g, v_w_in, v_ret_gn_g, v_conv_dw_w, v_conv_dw_b, v_conv_ln_g, v_conv_ln_b, v_conv_pw_w, v_conv_pw_b, v_w_out, v_final_g):
    _, SEQ, D = x.shape
    MIX = w_out.shape[2]
    RW = ret_gn_g.shape[1]
    CW = conv_pw_b.shape[1]
    assert RW == CW and MIX == RW + CW and SEQ % META_TILE == 0 and CONV_K - 1 <= HALO
    R = SEQ + META_TILE
    hd = RW // RET_HEADS
    half = hd // 2
    me = 4 * lax.axis_index("x") + 2 * lax.axis_index("y") + lax.axis_index("c")

    dw_pad = jnp.pad(conv_dw_w[0], ((0, HALO - CONV_K), (0, 0)))
    dw_g, meta_g = _gather_weights([dw_pad, meta_tokens], [1, 1])

    pos = jnp.concatenate([jnp.arange(SEQ, dtype=F32) + N_META, jnp.zeros((META_TILE - N_META,), F32),
                           jnp.arange(N_META, dtype=F32)])
    inv_freq = ROPE_BASE ** (-jnp.arange(half, dtype=F32) / half)
    ang = pos[:, None] * inv_freq[None, :]
    cos, sin = jnp.cos(ang), jnp.sin(ang)

    xs = x[0]
    meta_tile = jnp.concatenate([jnp.zeros((META_TILE - N_META, D), F32), meta_g], axis=0)
    target = loss_target[0]
    final_g2 = final_g[None, :]

    hn = _rms_norm(xs, meta_tile, ln_g)
    row_shards = [w_out[0].astype(BF16), conv_pw_w[0].astype(BF16)]
    proj, w_in_g, (w_out_g, pw_g) = _in_proj_gather(hn, w_in[0].astype(BF16), _Exchange(row_shards, [None, None]),
                                                    row_shards)
    w_out_g, pw_g = w_out_g.reshape(MIX, D), pw_g.reshape(CW, CW)
    y, states = _retention_fwd(proj, cos, sin, ret_gn_g, MIX)
    y, conv_out = _conv_fwd(proj, y, dw_g, conv_dw_b, conv_ln_g, conv_ln_b, pw_g, conv_pw_b)
    dh2, dy, dwo_p, dfg_p, loss_p = _out_proj_loss(xs, meta_tile, y, w_out_g, final_g2, target)

    dp_conv, dpw_p, dww_p, cvec_p = _conv_bwd(proj, conv_out, dy, dw_g, conv_ln_g, conv_ln_b, pw_g, conv_pw_b)
    dp_ret, dgn_p, (r_wo, r_pw) = _retention_bwd(proj, cos, sin, ret_gn_g, states, dy,
                                                 _Exchange([dwo_p, dpw_p], [0, 0]), [dwo_p, dpw_p])
    dwi_p = _w_in_grad(hn, dp_ret, dp_conv)
    grad_xs, dlg_x, (r_wi,) = _h_grad("h_grad", dp_ret, dp_conv, w_in_g, xs, dh2, ln_g, 0,
                                      ex=_Exchange([dwi_p], [1]), ex_parts=[dwi_p])
    dh_meta, dlg_p, _ = _h_grad("h_grad_meta", dp_ret, dp_conv, w_in_g, meta_tile, dh2, ln_g, SEQ, dlg_init=dlg_x)
    grad_x = grad_xs[None]

    def at_row(r, a, b=None):
        v = a if b is None else jnp.concatenate([a, b], axis=1)
        return jnp.pad(v, ((r, 7 - r), (0, D - v.shape[1])))
    vec8 = (at_row(0, dlg_p) + at_row(1, dfg_p)
            + at_row(2, dgn_p, cvec_p[3:4])
            + at_row(3, cvec_p[1:2], cvec_p[2:3])
            + at_row(4, cvec_p[0:1], jnp.broadcast_to(loss_p, (1, CW))))
    small = jnp.concatenate([dh_meta[META_TILE - N_META:], vec8,
                             jnp.zeros((SMALL_ROWS - N_META - 8, D), F32)], axis=0)

    (g_wi, d_wi, nm_wi, nv_wi), (r_dww, r_small) = _sum_adamw(
        "sum_adamw_w_in", r_wi, w_in[0], m_w_in[0], v_w_in[0], 256,
        ex=_Exchange([dww_p, small], [1, None]), ex_parts=[dww_p, small])
    (g_wo, d_wo, nm_wo, nv_wo), _ = _sum_adamw("sum_adamw_w_out", r_wo, w_out[0], m_w_out[0], v_w_out[0], 128)
    (g_pw, d_pw, nm_pw, nv_pw), _ = _sum_adamw("sum_adamw_pw", r_pw, conv_pw_w[0], m_conv_pw_w[0], v_conv_pw_w[0], 128)

    dcol = D // N_DEV
    sm = lambda r0, nr, c0, nc: lax.slice(r_small, (0, r0, c0), (N_DEV, r0 + nr, c0 + nc))
    meta_parts = lax.dynamic_slice(r_small, (0, 0, me * dcol), (N_DEV, N_META, dcol))
    small_parts = [meta_parts, sm(16, 1, 0, D), sm(18, 1, 0, RW), r_dww, sm(18, 1, RW, CW),
                   sm(19, 1, 0, CW), sm(19, 1, CW, CW), sm(20, 1, 0, CW), sm(17, 1, 0, D)]
    pad31 = lambda a: jnp.pad(a, ((0, HALO - CONV_K), (0, 0)))
    ws = [meta_tokens, ln_g, ret_gn_g, pad31(conv_dw_w[0]), conv_dw_b, conv_ln_g, conv_ln_b, conv_pw_b, final_g2]
    ms = [m_meta_tokens, m_ln_g, m_ret_gn_g, pad31(m_conv_dw_w[0]), m_conv_dw_b, m_conv_ln_g, m_conv_ln_b,
          m_conv_pw_b, m_final_g[None, :]]
    vs = [v_meta_tokens, v_ln_g, v_ret_gn_g, pad31(v_conv_dw_w[0]), v_conv_dw_b, v_conv_ln_g, v_conv_ln_b,
          v_conv_pw_b, v_final_g[None, :]]
    loss_parts = sm(20, 1, CW, 1)
    outs = _sum_adamw_small(small_parts, ws, ms, vs, loss_parts)
    loss = outs[-1][0, 0]
    quad = [outs[4 * a:4 * a + 4] for a in range(len(ws))]
    (q_meta, q_lng, q_gn, q_dww, q_dwb, q_clg, q_clb, q_pwb, q_fg) = quad
    q_dww = [t[:CONV_K][None] for t in q_dww]
    q_fg = [t[0] for t in q_fg]
    q_wi = [t[None] for t in (g_wi, d_wi, nm_wi, nv_wi)]
    q_wo = [t[None] for t in (g_wo, d_wo, nm_wo, nv_wo)]
    q_pw = [t[None] for t in (g_pw, d_pw, nm_pw, nv_pw)]

    per_w = [q_meta, q_lng, q_wi, q_gn, q_dww, q_dwb, q_clg, q_clb, q_pw, q_pwb, q_wo, q_fg]
    result = [loss, grad_x]
    for which in range(4):
        result += [q[which] for q in per_w]
    return tuple(result)
```

```python
import functools

import numpy as np
import jax
import jax.numpy as jnp
from jax import lax
from jax.experimental import pallas as pl
from jax.experimental.pallas import tpu as pltpu

N_META = 16
RET_HEADS = 4
CONV_K = 31
CHUNK = 128
ROPE_BASE = 10000.0
EPS = 1e-6
ADAM_LR = 0.001
ADAM_B1 = 0.9
ADAM_B2 = 0.999
ADAM_EPS = 1e-08
ADAM_WD = 0.01
ADAM_STEP = 10

N_DEV = 8
META_TILE = 256
HALO = 32
SMALL_ROWS = 32
VMEM_BYTES_V7X = 64 * 1024 * 1024
MXU_DIM = 256

F32 = jnp.float32
BF16 = jnp.bfloat16
MESH = pl.DeviceIdType.MESH

NN = (((1,), (0,)), ((), ()))
NT = (((1,), (1,)), ((), ()))
TN = (((0,), (0,)), ((), ()))


def _dot(a, b, dims=NN):
    return lax.dot_general(a, b, dims, preferred_element_type=F32)


def _pick_tile(n, target, mult=16):
    best = None
    for t in range(mult, min(n, target) + 1, mult):
        if n % t == 0:
            best = t
    assert best is not None, (n, target)
    return best


def _params(sem=None, vmem_mb=None):
    kw = {}
    if sem is not None:
        kw["dimension_semantics"] = sem
    if vmem_mb is not None:
        kw["vmem_limit_bytes"] = min(vmem_mb * 1024 * 1024, VMEM_BYTES_V7X - 4 * 1024 * 1024)
    return pltpu.CompilerParams(**kw)


def _sigmoid(x):
    return jax.nn.sigmoid(x)


def _dsilu(x, sg):
    return sg * (1.0 + x * (1.0 - sg))


def _decay_tables(heads):
    h = np.arange(heads, dtype=np.float32)
    gamma = (1.0 - np.exp2(-5.0 - h)).astype(np.float32)
    log_g = np.log(gamma).astype(np.float32)
    idx = np.arange(CHUNK, dtype=np.float32)
    rel = idx[:, None] - idx[None, :]
    mask = np.where(rel[None] >= 0, np.exp(np.maximum(rel, 0.0)[None] * log_g[:, None, None]), 0.0)
    qd = np.exp((idx[None, :] + 1.0) * log_g[:, None])
    kd = np.exp((CHUNK - 1.0 - idx[None, :]) * log_g[:, None])
    cd = np.exp(CHUNK * log_g)
    return (mask.astype(np.float32), qd.astype(np.float32)[:, :, None], kd.astype(np.float32)[:, :, None],
            [float(c) for c in cd.astype(np.float32)])


def _gather_weights(shards, block_axes):
    n_arr = len(shards)
    out_shapes = []
    for s, ax in zip(shards, block_axes):
        shp = list(s.shape)
        shp[ax] *= N_DEV
        out_shapes.append(jax.ShapeDtypeStruct(tuple(shp), s.dtype))

    def body(*refs):
        ins, outs = refs[:n_arr], refs[n_arr:2 * n_arr]
        send_sems, recv_sems, local_sems = refs[2 * n_arr:]
        x, y, c = lax.axis_index("x"), lax.axis_index("y"), lax.axis_index("c")
        me, sibling = (x, y, c), (x, y, 1 - c)
        chips = [(1 - x, y), (x, 1 - y), (1 - x, 1 - y)]

        def block(a, dev):
            n = ins[a].shape[block_axes[a]]
            start = pl.multiple_of((4 * dev[0] + 2 * dev[1] + dev[2]) * n, n)
            idx = [slice(None)] * len(ins[a].shape)
            idx[block_axes[a]] = pl.ds(start, n)
            return outs[a].at[tuple(idx)]

        def copy(a, k, dev, to, src=None):
            return pltpu.make_async_remote_copy(
                src_ref=block(a, dev) if src is None else src, dst_ref=block(a, dev),
                send_sem=send_sems.at[a, k], recv_sem=recv_sems.at[a, k],
                device_id=to, device_id_type=MESH)

        mine = [pltpu.make_async_copy(ins[a], block(a, me), local_sems.at[a]) for a in range(n_arr)]
        for cp in mine:
            cp.start()
        first = []
        for a in range(n_arr):
            first.append(copy(a, 0, me, sibling, src=ins[a]))
            first += [copy(a, 1 + j, me, (*chip, c), src=ins[a]) for j, chip in enumerate(chips)]
        for cp in first:
            cp.start()
        passed = []
        for j, chip in enumerate(chips):
            for a in range(n_arr):
                copy(a, 1 + j, (*chip, c), me).wait_recv()
                fwd = copy(a, 4 + j, (*chip, c), sibling)
                fwd.start()
                passed.append(fwd)
        for a in range(n_arr):
            copy(a, 0, sibling, me).wait_recv()
            for j, chip in enumerate(chips):
                copy(a, 4 + j, (*chip, 1 - c), me).wait_recv()
        for cp in first + passed:
            cp.wait_send()
        for cp in mine:
            cp.wait()

    hbm = pl.BlockSpec(memory_space=pl.ANY)
    return pl.pallas_call(
        body, name="gather_weights",
        out_shape=out_shapes,
        in_specs=[hbm] * n_arr, out_specs=[hbm] * n_arr,
        scratch_shapes=[pltpu.SemaphoreType.DMA((n_arr, 7)), pltpu.SemaphoreType.DMA((n_arr, 7)),
                        pltpu.SemaphoreType.DMA((n_arr,))],
    )(*shards)


class _Exchange:
    def __init__(self, parts, block_axes):
        self.block_axes = list(block_axes)
        self.n = len(parts)
        self.out_shape = []
        for p, ax in zip(parts, block_axes):
            shp = list(p.shape)
            if ax is not None:
                assert shp[ax] % N_DEV == 0
                shp[ax] //= N_DEV
            self.out_shape.append(jax.ShapeDtypeStruct((N_DEV, *shp), p.dtype))
        self.scratch = [pltpu.SemaphoreType.DMA((self.n, N_DEV - 1)), pltpu.SemaphoreType.DMA((self.n, N_DEV - 1)),
                        pltpu.SemaphoreType.DMA((self.n,))]
        self.specs = [pl.BlockSpec(memory_space=pl.ANY)] * self.n

    def _copies(self, ins, outs, sems):
        send_sems, recv_sems, local_sems = sems
        x, y, c = lax.axis_index("x"), lax.axis_index("y"), lax.axis_index("c")
        me_idx = 4 * x + 2 * y + c

        def src_block(a, dev_idx):
            ax = self.block_axes[a]
            if ax is None:
                return ins[a]
            n = ins[a].shape[ax] // N_DEV
            idx = [slice(None)] * len(ins[a].shape)
            idx[ax] = pl.ds(pl.multiple_of(dev_idx * n, n), n)
            return ins[a].at[tuple(idx)]

        local = [pltpu.make_async_copy(src_block(a, me_idx), outs[a].at[me_idx], local_sems.at[a])
                 for a in range(self.n)]
        remote = []
        for m in range(1, N_DEV):
            px, py, pc = x ^ ((m >> 2) & 1), y ^ ((m >> 1) & 1), c ^ (m & 1)
            for a in range(self.n):
                remote.append(pltpu.make_async_remote_copy(
                    src_ref=src_block(a, 4 * px + 2 * py + pc), dst_ref=outs[a].at[me_idx],
                    send_sem=send_sems.at[a, m - 1], recv_sem=recv_sems.at[a, m - 1],
                    device_id=(px, py, pc), device_id_type=MESH))
        return local, remote

    def start(self, ins, outs, sems):
        local, remote = self._copies(ins, outs, sems)
        for cp in local + remote:
            cp.start()

    def wait(self, ins, outs, sems):
        local, remote = self._copies(ins, outs, sems)
        for cp in remote:
            cp.wait_recv()
        for cp in remote:
            cp.wait_send()
        for cp in local:
            cp.wait()


def _rms_norm(xs, meta_tile, ln_g):
    SEQ, D = xs.shape
    tm = meta_tile.shape[0]
    n_seq = SEQ // tm

    def body(x_ref, mt_ref, g_ref, hn_ref):
        hv = jnp.where(pl.program_id(0) < n_seq, x_ref[...], mt_ref[...])
        r = lax.rsqrt(jnp.mean(hv * hv, axis=-1, keepdims=True) + EPS)
        hn_ref[...] = (hv * r * g_ref[...]).astype(BF16)

    return pl.pallas_call(
        body, name="rms_norm",
        grid=(n_seq + 1,),
        in_specs=[pl.BlockSpec((tm, D), lambda i: (jnp.minimum(i, n_seq - 1), 0)),
                  pl.BlockSpec((tm, D), lambda i: (0, 0)),
                  pl.BlockSpec((1, D), lambda i: (0, 0))],
        out_specs=pl.BlockSpec((tm, D), lambda i: (i, 0)),
        out_shape=jax.ShapeDtypeStruct((SEQ + tm, D), BF16),
        compiler_params=_params(("arbitrary",), 32),
    )(xs, meta_tile, ln_g)


def _chip_visited(q):
    mine = 2 * lax.axis_index("x") + lax.axis_index("y")
    return mine ^ (((q & 1) << 1) | (q >> 1))


def _in_proj_gather(hn, w_shard, ex, ex_parts):
    R, D = hn.shape
    wb = w_shard.shape[1]
    E, tn = wb * N_DEV, 2 * wb
    n_q = N_DEV // 2
    tm = _pick_tile(R, min(768, R // 2), MXU_DIM)
    n_i = R // tm

    def body(*refs):
        hn_ref, wsh_hbm = refs[:2]
        ex_ins = refs[2:2 + ex.n]
        proj_ref, wg_hbm = refs[2 + ex.n:4 + ex.n]
        ex_outs = refs[4 + ex.n:4 + 2 * ex.n]
        w_vmem, send_sems, recv_sems, local_sem, vmem_sems = refs[4 + 2 * ex.n:9 + 2 * ex.n]
        ex_sems = refs[9 + 2 * ex.n:]
        q, i = pl.program_id(0), pl.program_id(1)
        x, y, c = lax.axis_index("x"), lax.axis_index("y"), lax.axis_index("c")
        me, sibling = (x, y, c), (x, y, 1 - c)
        chips = [(1 - x, y), (x, 1 - y), (1 - x, 1 - y)]

        def block(dev):
            return wg_hbm.at[:, pl.ds(pl.multiple_of((4 * dev[0] + 2 * dev[1] + dev[2]) * wb, wb), wb)]

        def copy(k, dev, to, src=None):
            return pltpu.make_async_remote_copy(
                src_ref=block(dev) if src is None else src, dst_ref=block(dev),
                send_sem=send_sems.at[k], recv_sem=recv_sems.at[k], device_id=to, device_id_type=MESH)

        def to_vmem(p):
            cols = pl.ds(pl.multiple_of(_chip_visited(p) * tn, tn), tn)
            return pltpu.make_async_copy(wg_hbm.at[:, cols], w_vmem.at[p % 2], vmem_sems.at[p % 2])

        mine = pltpu.make_async_copy(wsh_hbm, block(me), local_sem)
        first = [copy(0, me, sibling, src=wsh_hbm)] + [copy(1 + j, me, (*chip, c), src=wsh_hbm)
                                                       for j, chip in enumerate(chips)]
        passed = [copy(4 + j, (*chip, c), sibling) for j, chip in enumerate(chips)]

        @pl.when((q == 0) & (i == 0))
        def _():
            mine.start()
            for cp in first:
                cp.start()
            ex.start(ex_ins, ex_outs, ex_sems)
            mine.wait()
            copy(0, sibling, me).wait_recv()
            to_vmem(0).start()
            to_vmem(0).wait()

        for p in range(1, n_q):
            chip = chips[p - 1]

            @pl.when((q == p - 1) & (i == n_i - 2))
            def _():
                copy(p, (*chip, c), me).wait_recv()
                passed[p - 1].start()

            @pl.when((q == p - 1) & (i == n_i - 1))
            def _():
                copy(3 + p, (*chip, 1 - c), me).wait_recv()
                to_vmem(p).start()

            @pl.when((q == p) & (i == 0))
            def _():
                to_vmem(p).wait()

        proj_ref[...] = _dot(hn_ref[...], w_vmem[q % 2]).astype(BF16)

        @pl.when((q == n_q - 1) & (i == n_i - 1))
        def _():
            for cp in first + passed:
                cp.wait_send()
            ex.wait(ex_ins, ex_outs, ex_sems)

    hbm = pl.BlockSpec(memory_space=pl.ANY)
    outs = pl.pallas_call(
        body, name="in_proj",
        grid=(n_q, n_i),
        in_specs=[pl.BlockSpec((tm, D), lambda q, i: (i, 0)), hbm] + ex.specs,
        out_specs=[pl.BlockSpec((tm, tn), lambda q, i: (i, _chip_visited(q))), hbm] + ex.specs,
        out_shape=[jax.ShapeDtypeStruct((R, E), BF16), jax.ShapeDtypeStruct((D, E), BF16)] + ex.out_shape,
        scratch_shapes=[pltpu.VMEM((2, D, tn), BF16), pltpu.SemaphoreType.DMA((7,)), pltpu.SemaphoreType.DMA((7,)),
                        pltpu.SemaphoreType.DMA, pltpu.SemaphoreType.DMA((2,))] + ex.scratch,
        compiler_params=_params(("arbitrary", "arbitrary"), 48),
    )(hn, w_shard, *ex_parts)
    return outs[0], outs[1], outs[2:]


def _rot(t, cos, sin, half):
    t1, t2 = t[:, :half], t[:, half:]
    return jnp.concatenate([t1 * cos - t2 * sin, t1 * sin + t2 * cos], axis=-1)


def _rot_inv(t, cos, sin, half):
    t1, t2 = t[:, :half], t[:, half:]
    return jnp.concatenate([t1 * cos + t2 * sin, t2 * cos - t1 * sin], axis=-1)


def _chunk_order(n_chunks):
    lead = META_TILE // CHUNK
    return lambda l: (l + n_chunks - lead) % n_chunks


def _retention_fwd(proj, cos, sin, gn_g, mix):
    R, E = proj.shape
    RW = gn_g.shape[1]
    H = RET_HEADS
    hd = RW // H
    half = hd // 2
    NC = R // CHUNK
    mask, qd, kd, cd = _decay_tables(H)
    scale = float(hd) ** -0.5
    phys = _chunk_order(NC)

    def body(p_ref, cos_ref, sin_ref, mask_ref, qd_ref, kd_ref, gn_ref, y_ref, st_ref, state):
        @pl.when(pl.program_id(0) == 0)
        def _():
            state[...] = jnp.zeros_like(state)

        cs, sn = cos_ref[...], sin_ref[...]
        hs = range(H)
        col = lambda j, h: slice(j * RW + h * hd, j * RW + (h + 1) * hd)
        qr = [_rot(p_ref[:, col(0, h)].astype(F32), cs, sn, half) for h in hs]
        kr = [_rot(p_ref[:, col(1, h)].astype(F32), cs, sn, half) * scale for h in hs]
        v = [p_ref[:, col(2, h)] for h in hs]
        s_prev = [state[h] for h in hs]
        s_prev_b = [s_prev[h].astype(BF16) for h in hs]
        s = [(_dot(qr[h].astype(BF16), kr[h].astype(BF16), NT) * mask_ref[h]).astype(BF16) for h in hs]
        y_raw = [_dot(s[h], v[h]) + _dot((qr[h] * qd_ref[h]).astype(BF16), s_prev_b[h]) for h in hs]
        s_new = [s_prev[h] * cd[h] + _dot((kr[h] * kd_ref[h]).astype(BF16), v[h], TN) for h in hs]
        for h in hs:
            st_ref[0, h] = s_prev_b[h]
            state[h] = s_new[h]
        for h in hs:
            g = p_ref[:, col(3, h)].astype(F32)
            mu = jnp.mean(y_raw[h], axis=-1, keepdims=True)
            yc = y_raw[h] - mu
            var = jnp.mean(yc * yc, axis=-1, keepdims=True)
            out = yc * lax.rsqrt(var + EPS) * gn_ref[:, col(0, h)] * (g * _sigmoid(g))
            y_ref[:, col(0, h)] = out.astype(BF16)

    const3 = lambda l: (0, 0, 0)
    return pl.pallas_call(
        body, name="retention_fwd",
        grid=(NC,),
        in_specs=[pl.BlockSpec((CHUNK, 4 * RW), lambda l: (phys(l), 0)),
                  pl.BlockSpec((CHUNK, half), lambda l: (phys(l), 0)),
                  pl.BlockSpec((CHUNK, half), lambda l: (phys(l), 0)),
                  pl.BlockSpec((H, CHUNK, CHUNK), const3),
                  pl.BlockSpec((H, CHUNK, 1), const3),
                  pl.BlockSpec((H, CHUNK, 1), const3),
                  pl.BlockSpec((1, RW), lambda l: (0, 0))],
        out_specs=[pl.BlockSpec((CHUNK, RW), lambda l: (phys(l), 0)),
                   pl.BlockSpec((1, H, hd, hd), lambda l: (phys(l), 0, 0, 0))],
        out_shape=[jax.ShapeDtypeStruct((R, mix), BF16), jax.ShapeDtypeStruct((NC, H, hd, hd), BF16)],
        scratch_shapes=[pltpu.VMEM((H, hd, hd), F32)],
        compiler_params=_params(("arbitrary",), 32),
    )(proj, cos, sin, jnp.asarray(mask), jnp.asarray(qd), jnp.asarray(kd), gn_g)


CONV_ROWS = 32
CONV_LANES = 512
DW_LANES = 256


def _conv_order(n_tiles):
    return lambda l: (l + n_tiles - 1) % n_tiles


def _halo_block(n_tiles, tm):
    per = tm // HALO
    return lambda l: ((l + n_tiles - 2) % n_tiles) * per + per - 1


def _fill_shifted(src, dst):
    rows, width = dst.shape[1], dst.shape[2]
    step = _pick_tile(rows, 64, 8)
    for r in range(1, 8):
        for r0 in range(0, rows, step):
            for l0 in range(0, width, CONV_LANES):
                dst[r - 1, r0:r0 + step, l0:l0 + CONV_LANES] = src[r + r0:r + r0 + step, l0:l0 + CONV_LANES]


def _at_offset(src, shifted, off, r0, rows, lanes):
    r = off % 8
    a = off - r + r0
    if r == 0:
        return src[a:a + rows, lanes]
    return shifted[r - 1, a:a + rows, lanes]


def _fill_glu(first, a_ref, b_ref, ah_ref, bh_ref, u_ext, tm):
    uh = ah_ref[...].astype(F32) * _sigmoid(bh_ref[...].astype(F32))
    u_ext[0:HALO, :] = jnp.where(first, 0.0, uh)
    u_ext[HALO:HALO + tm, :] = a_ref[...].astype(F32) * _sigmoid(b_ref[...].astype(F32))


def _layer_norm(cv, lg_ref, lb_ref):
    mu = jnp.mean(cv, axis=-1, keepdims=True)
    cc = cv - mu
    rstd = lax.rsqrt(jnp.mean(cc * cc, axis=-1, keepdims=True) + EPS)
    xh = cc * rstd
    return xh, rstd, xh * lg_ref[...] + lb_ref[...]


def _conv_fwd(proj, y_in, dw_w, dw_b, ln_g, ln_b, pw_w, pw_b, ex, ex_parts):
    R, E = proj.shape
    CW = pw_w.shape[0]
    tm = META_TILE
    NTL = R // tm
    phys = _conv_order(NTL)
    halo = _halo_block(NTL, tm)
    cb = (E - 3 * CW) // CW
    base = HALO - (CONV_K - 1)

    def body(*refs):
        a_ref, b_ref, g_ref, ah_ref, bh_ref, w_ref, wb_ref, lg_ref, lb_ref, pw_ref, pb_ref, yin_ref = refs[:12]
        ex_ins = refs[12:12 + ex.n]
        y_ref, c_ref = refs[12 + ex.n:14 + ex.n]
        ex_outs = refs[14 + ex.n:14 + 2 * ex.n]
        u_ext, u_sh = refs[14 + 2 * ex.n:16 + 2 * ex.n]
        sems = refs[16 + 2 * ex.n:]

        @pl.when(pl.program_id(0) == 0)
        def _():
            ex.start(ex_ins, ex_outs, sems)

        _fill_glu(pl.program_id(0) == 0, a_ref, b_ref, ah_ref, bh_ref, u_ext, tm)
        _fill_shifted(u_ext, u_sh)
        for r0 in range(0, tm, CONV_ROWS):
            for l0 in range(0, CW, CONV_LANES):
                lanes = slice(l0, l0 + CONV_LANES)
                acc = None
                for k in range(CONV_K):
                    term = _at_offset(u_ext, u_sh, base + k, r0, CONV_ROWS, lanes) * w_ref[k:k + 1, lanes]
                    acc = term if acc is None else acc + term
                c_ref[r0:r0 + CONV_ROWS, lanes] = acc + wb_ref[:, lanes]
        _, _, ln = _layer_norm(c_ref[...], lg_ref, lb_ref)
        s = (ln * _sigmoid(ln)).astype(BF16)
        upw = _dot(s, pw_ref[...]) + pb_ref[...]
        g = g_ref[...].astype(F32)
        y_ref[...] = (upw * (g * _sigmoid(g))).astype(BF16)

        @pl.when(pl.program_id(0) == NTL - 1)
        def _():
            ex.wait(ex_ins, ex_outs, sems)

    row = lambda l: (0, 0)
    outs = pl.pallas_call(
        body, name="conv_fwd",
        grid=(NTL,),
        in_specs=[pl.BlockSpec((tm, CW), lambda l: (phys(l), cb)),
                  pl.BlockSpec((tm, CW), lambda l: (phys(l), cb + 1)),
                  pl.BlockSpec((tm, CW), lambda l: (phys(l), cb + 2)),
                  pl.BlockSpec((HALO, CW), lambda l: (halo(l), cb)),
                  pl.BlockSpec((HALO, CW), lambda l: (halo(l), cb + 1)),
                  pl.BlockSpec((HALO, CW), row),
                  pl.BlockSpec((1, CW), row), pl.BlockSpec((1, CW), row), pl.BlockSpec((1, CW), row),
                  pl.BlockSpec((CW, CW), row),
                  pl.BlockSpec((1, CW), row),
                  pl.BlockSpec(memory_space=pl.ANY)] + ex.specs,
        out_specs=[pl.BlockSpec((tm, CW), lambda l: (phys(l), 1)),
                   pl.BlockSpec((tm, CW), lambda l: (phys(l), 0))] + ex.specs,
        out_shape=[jax.ShapeDtypeStruct(y_in.shape, BF16), jax.ShapeDtypeStruct((R, CW), F32)] + ex.out_shape,
        input_output_aliases={11: 0},
        scratch_shapes=[pltpu.VMEM((HALO + tm, CW), F32), pltpu.VMEM((7, tm + HALO - 8, CW), F32)] + ex.scratch,
        compiler_params=_params(("arbitrary",), 48),
    )(proj, proj, proj, proj, proj, dw_w, dw_b, ln_g, ln_b, pw_w, pw_b, y_in, *ex_parts)
    return outs[0], outs[1], outs[2:]


def _out_proj_loss(xs, meta_tile, y, w_out, final_g, target):
    SEQ, D = xs.shape
    R, MIX = y.shape
    tm = META_TILE
    n_seq = SEQ // tm
    n_tiles = R // tm
    rows_out = _pick_tile(MIX, 256)

    def body(x_ref, mt_ref, y_ref, w_hbm, fg_ref, t_ref, dh2_ref, dy_ref, dwo_hbm, dfg_ref, loss_ref,
             w_scr, acc, stage, sem):
        i = pl.program_id(0)

        @pl.when(i == 0)
        def _():
            cp = pltpu.make_async_copy(w_hbm, w_scr, sem)
            cp.start()
            acc[...] = jnp.zeros_like(acc)
            dfg_ref[...] = jnp.zeros_like(dfg_ref)
            loss_ref[...] = jnp.zeros_like(loss_ref)
            cp.wait()

        yb = y_ref[...]
        h2 = jnp.where(i < n_seq, x_ref[...], mt_ref[...]) + _dot(yb, w_scr[...])
        r2 = lax.rsqrt(jnp.mean(h2 * h2, axis=-1, keepdims=True) + EPS)
        n = h2 * r2
        fg = fg_ref[...]
        err = jnp.where(i < n_seq, n * fg - t_ref[...], 0.0)
        loss_ref[...] += 0.5 * jnp.sum(jnp.mean(err * err, axis=-1, keepdims=True), axis=0, keepdims=True)
        dout = err * (1.0 / D)
        dfg_ref[...] += jnp.sum(dout * n, axis=0, keepdims=True)
        dn = dout * fg
        dh2 = r2 * (dn - n * jnp.mean(dn * n, axis=-1, keepdims=True))
        dh2_ref[...] = dh2
        dh2b = dh2.astype(BF16)
        dy_ref[...] = _dot(dh2b, w_scr[...], NT).astype(BF16)
        acc[...] += _dot(yb, dh2b, TN)

        @pl.when(i == n_tiles - 1)
        def _():
            for r in range(0, MIX, rows_out):
                stage[...] = acc[r:r + rows_out, :].astype(BF16)
                cp = pltpu.make_async_copy(stage, dwo_hbm.at[r:r + rows_out, :], sem)
                cp.start()
                cp.wait()

    row = lambda i: (0, 0)
    return pl.pallas_call(
        body, name="out_proj_loss",
        grid=(n_tiles,),
        in_specs=[pl.BlockSpec((tm, D), lambda i: (jnp.minimum(i, n_seq - 1), 0)),
                  pl.BlockSpec((tm, D), row),
                  pl.BlockSpec((tm, MIX), lambda i: (i, 0)),
                  pl.BlockSpec(memory_space=pl.ANY),
                  pl.BlockSpec((1, D), row),
                  pl.BlockSpec((tm, D), lambda i: (jnp.minimum(i, n_seq - 1), 0))],
        out_specs=[pl.BlockSpec((tm, D), lambda i: (i, 0)),
                   pl.BlockSpec((tm, MIX), lambda i: (i, 0)),
                   pl.BlockSpec(memory_space=pl.ANY),
                   pl.BlockSpec((1, D), row),
                   pl.BlockSpec((1, 1), row)],
        out_shape=[jax.ShapeDtypeStruct((R, D), F32), jax.ShapeDtypeStruct((R, MIX), BF16),
                   jax.ShapeDtypeStruct((MIX, D), BF16), jax.ShapeDtypeStruct((1, D), F32),
                   jax.ShapeDtypeStruct((1, 1), F32)],
        scratch_shapes=[pltpu.VMEM((MIX, D), BF16), pltpu.VMEM((MIX, D), F32), pltpu.VMEM((rows_out, D), BF16),
                        pltpu.SemaphoreType.DMA],
        compiler_params=_params(("arbitrary",), 60),
    )(xs, meta_tile, y, w_out, final_g, target)


def _conv_bwd(proj, conv_out, dy, dw_w, ln_g, ln_b, pw_w, pw_b):
    R, E = proj.shape
    CW = pw_w.shape[0]
    tm = META_TILE
    NTL = R // tm
    order = _conv_order(NTL)
    phys = lambda i: order(NTL - 1 - i)
    halo_l = _halo_block(NTL, tm)
    halo = lambda i: halo_l(NTL - 1 - i)
    cb = (E - 3 * CW) // CW
    base = HALO - (CONV_K - 1)

    def body(a_ref, b_ref, g_ref, ah_ref, bh_ref, c_ref, dy_ref, w_ref, lg_ref, lb_ref, pw_ref, pb_ref,
             dp_ref, dpw_ref, dww_ref, vec_ref, u_ext, u_sh, dc_ext, dc_sh, du_scr, dww_acc, dpw_acc):
        i = pl.program_id(0)

        @pl.when(i == 0)
        def _():
            dpw_acc[...] = jnp.zeros_like(dpw_acc)
            dww_ref[...] = jnp.zeros_like(dww_ref)
            vec_ref[...] = jnp.zeros_like(vec_ref)
            dww_acc[...] = jnp.zeros_like(dww_acc)
            dc_ext[tm:tm + HALO, :] = jnp.zeros((HALO, CW), F32)

        _fill_glu(i == NTL - 1, a_ref, b_ref, ah_ref, bh_ref, u_ext, tm)
        _fill_shifted(u_ext, u_sh)
        xh, rstd, ln = _layer_norm(c_ref[...], lg_ref, lb_ref)
        sg = _sigmoid(ln)
        sb = (ln * sg).astype(BF16)
        upw = _dot(sb, pw_ref[...]) + pb_ref[...]
        g = g_ref[...].astype(F32)
        sgg = _sigmoid(g)
        dyc = dy_ref[...].astype(F32)
        dp_ref[:, 2 * CW:3 * CW] = (dyc * upw * _dsilu(g, sgg)).astype(BF16)
        dupw = dyc * (g * sgg)
        dupw_b = dupw.astype(BF16)
        vec_ref[0:1, :] += jnp.sum(dupw, axis=0, keepdims=True)
        dpw_acc[...] += _dot(sb, dupw_b, TN)
        dln = _dot(dupw_b, pw_ref[...], NT) * _dsilu(ln, sg)
        vec_ref[1:2, :] += jnp.sum(dln * xh, axis=0, keepdims=True)
        vec_ref[2:3, :] += jnp.sum(dln, axis=0, keepdims=True)
        dxh = dln * lg_ref[...]
        dc = rstd * (dxh - jnp.mean(dxh, axis=-1, keepdims=True) - xh * jnp.mean(dxh * xh, axis=-1, keepdims=True))
        vec_ref[3:4, :] += jnp.sum(dc, axis=0, keepdims=True)
        dc_ext[0:tm, :] = dc
        _fill_shifted(dc_ext, dc_sh)

        for l0 in range(0, CW, CONV_LANES):
            lanes = slice(l0, l0 + CONV_LANES)
            for r0 in range(0, tm, CONV_ROWS):
                acc = None
                for k in range(CONV_K):
                    term = _at_offset(dc_ext, dc_sh, CONV_K - 1 - k, r0, CONV_ROWS, lanes) * w_ref[k:k + 1, lanes]
                    acc = term if acc is None else acc + term
                du_scr[r0:r0 + CONV_ROWS, lanes] = acc
        for l0 in range(0, CW, DW_LANES):
            lanes = slice(l0, l0 + DW_LANES)
            for r0 in range(0, tm, CONV_ROWS):
                dcb = dc_ext[r0:r0 + CONV_ROWS, lanes]
                for k in range(CONV_K):
                    prod = dcb * _at_offset(u_ext, u_sh, base + k, r0, CONV_ROWS, lanes)
                    part = prod[0:8]
                    for q in range(8, CONV_ROWS, 8):
                        part = part + prod[q:q + 8]
                    dww_acc[k, :, lanes] += part

        du = du_scr[...]
        a = a_ref[...].astype(F32)
        sgb = _sigmoid(b_ref[...].astype(F32))
        dp_ref[:, 0:CW] = (du * sgb).astype(BF16)
        dp_ref[:, CW:2 * CW] = (du * a * sgb * (1.0 - sgb)).astype(BF16)
        dc_ext[tm:tm + HALO, :] = dc_ext[0:HALO, :]

        @pl.when(i == NTL - 1)
        def _():
            for k in range(CONV_K):
                dww_ref[k:k + 1, :] = jnp.sum(dww_acc[k], axis=0, keepdims=True)
            dpw_ref[...] = dpw_acc[...].astype(BF16)

    row = lambda i: (0, 0)
    return pl.pallas_call(
        body, name="conv_bwd",
        grid=(NTL,),
        in_specs=[pl.BlockSpec((tm, CW), lambda i: (phys(i), cb)),
                  pl.BlockSpec((tm, CW), lambda i: (phys(i), cb + 1)),
                  pl.BlockSpec((tm, CW), lambda i: (phys(i), cb + 2)),
                  pl.BlockSpec((HALO, CW), lambda i: (halo(i), cb)),
                  pl.BlockSpec((HALO, CW), lambda i: (halo(i), cb + 1)),
                  pl.BlockSpec((tm, CW), lambda i: (phys(i), 0)),
                  pl.BlockSpec((tm, CW), lambda i: (phys(i), 1)),
                  pl.BlockSpec((HALO, CW), row),
                  pl.BlockSpec((1, CW), row), pl.BlockSpec((1, CW), row),
                  pl.BlockSpec((CW, CW), row),
                  pl.BlockSpec((1, CW), row)],
        out_specs=[pl.BlockSpec((tm, 3 * CW), lambda i: (phys(i), 0)),
                   pl.BlockSpec((CW, CW), row),
                   pl.BlockSpec((HALO, CW), row),
                   pl.BlockSpec((8, CW), row)],
        out_shape=[jax.ShapeDtypeStruct((R, 3 * CW), BF16), jax.ShapeDtypeStruct((CW, CW), BF16),
                   jax.ShapeDtypeStruct((HALO, CW), F32), jax.ShapeDtypeStruct((8, CW), F32)],
        scratch_shapes=[pltpu.VMEM((HALO + tm, CW), F32), pltpu.VMEM((7, tm + HALO - 8, CW), F32),
                        pltpu.VMEM((tm + HALO, CW), F32), pltpu.VMEM((7, tm + HALO - 8, CW), F32),
                        pltpu.VMEM((tm, CW), F32), pltpu.VMEM((CONV_K, 8, CW), F32), pltpu.VMEM((CW, CW), F32)],
        compiler_params=_params(("arbitrary",), 60),
    )(proj, proj, proj, proj, proj, conv_out, dy, dw_w, ln_g, ln_b, pw_w, pw_b)


def _retention_bwd(proj, cos, sin, gn_g, states, dy, ex, ex_parts):
    R, E = proj.shape
    RW = gn_g.shape[1]
    H = RET_HEADS
    hd = RW // H
    half = hd // 2
    NC = R // CHUNK
    mask, qd, kd, cd = _decay_tables(H)
    scale = float(hd) ** -0.5
    order = _chunk_order(NC)
    phys = lambda i: order(NC - 1 - i)

    def body(*refs):
        p_ref, cos_ref, sin_ref, mask_ref, qd_ref, kd_ref, gn_ref, st_ref, dy_ref = refs[:9]
        ex_ins = refs[9:9 + ex.n]
        dp_ref, dgn_ref = refs[9 + ex.n:11 + ex.n]
        ex_outs = refs[11 + ex.n:11 + 2 * ex.n]
        dstate = refs[11 + 2 * ex.n]
        sems = refs[12 + 2 * ex.n:]

        @pl.when(pl.program_id(0) == 0)
        def _():
            ex.start(ex_ins, ex_outs, sems)
            dstate[...] = jnp.zeros_like(dstate)
            dgn_ref[...] = jnp.zeros_like(dgn_ref)

        cs, sn = cos_ref[...], sin_ref[...]
        hs = range(H)
        col = lambda j, h: slice(j * RW + h * hd, j * RW + (h + 1) * hd)
        qr = [_rot(p_ref[:, col(0, h)].astype(F32), cs, sn, half) for h in hs]
        kr = [_rot(p_ref[:, col(1, h)].astype(F32), cs, sn, half) * scale for h in hs]
        v = [p_ref[:, col(2, h)] for h in hs]
        qb = [qr[h].astype(BF16) for h in hs]
        kb = [kr[h].astype(BF16) for h in hs]
        qdb = [(qr[h] * qd_ref[h]).astype(BF16) for h in hs]
        kdb = [(kr[h] * kd_ref[h]).astype(BF16) for h in hs]
        s_prev = [st_ref[0, h] for h in hs]
        dst = [dstate[h] for h in hs]
        dstb = [dst[h].astype(BF16) for h in hs]
        sb = [(_dot(qb[h], kb[h], NT) * mask_ref[h]).astype(BF16) for h in hs]
        y_raw = [_dot(sb[h], v[h]) + _dot(qdb[h], s_prev[h]) for h in hs]
        dyrb, dg = [], []
        for h in hs:
            g = p_ref[:, col(3, h)].astype(F32)
            mu = jnp.mean(y_raw[h], axis=-1, keepdims=True)
            yc = y_raw[h] - mu
            rstd = lax.rsqrt(jnp.mean(yc * yc, axis=-1, keepdims=True) + EPS)
            xh = yc * rstd
            gn = gn_ref[:, col(0, h)]
            sg = _sigmoid(g)
            dyh = dy_ref[:, col(0, h)].astype(F32)
            dg.append((dyh * (xh * gn) * _dsilu(g, sg)).astype(BF16))
            dyn = dyh * (g * sg)
            dgn_ref[:, col(0, h)] += jnp.sum(dyn * xh, axis=0, keepdims=True)
            dxh = dyn * gn
            dyr = rstd * (dxh - jnp.mean(dxh, axis=-1, keepdims=True)
                          - xh * jnp.mean(dxh * xh, axis=-1, keepdims=True))
            dyrb.append(dyr.astype(BF16))
        dsb = [(_dot(dyrb[h], v[h], NT) * mask_ref[h]).astype(BF16) for h in hs]
        dqr = [_dot(dsb[h], kb[h]) + _dot(dyrb[h], s_prev[h], NT) * qd_ref[h] for h in hs]
        dkr = [_dot(dsb[h], qb[h], TN) + _dot(v[h], dstb[h], NT) * kd_ref[h] for h in hs]
        dv = [_dot(sb[h], dyrb[h], TN) + _dot(kdb[h], dstb[h]) for h in hs]
        dst_new = [dst[h] * cd[h] + _dot(qdb[h], dyrb[h], TN) for h in hs]
        for h in hs:
            dstate[h] = dst_new[h]
            dp_ref[:, col(0, h)] = _rot_inv(dqr[h], cs, sn, half).astype(BF16)
            dp_ref[:, col(1, h)] = (_rot_inv(dkr[h], cs, sn, half) * scale).astype(BF16)
            dp_ref[:, col(2, h)] = dv[h].astype(BF16)
            dp_ref[:, col(3, h)] = dg[h]

        @pl.when(pl.program_id(0) == NC - 1)
        def _():
            ex.wait(ex_ins, ex_outs, sems)

    const3 = lambda i: (0, 0, 0)
    outs = pl.pallas_call(
        body, name="retention_bwd",
        grid=(NC,),
        in_specs=[pl.BlockSpec((CHUNK, 4 * RW), lambda i: (phys(i), 0)),
                  pl.BlockSpec((CHUNK, half), lambda i: (phys(i), 0)),
                  pl.BlockSpec((CHUNK, half), lambda i: (phys(i), 0)),
                  pl.BlockSpec((H, CHUNK, CHUNK), const3),
                  pl.BlockSpec((H, CHUNK, 1), const3),
                  pl.BlockSpec((H, CHUNK, 1), const3),
                  pl.BlockSpec((1, RW), lambda i: (0, 0)),
                  pl.BlockSpec((1, H, hd, hd), lambda i: (phys(i), 0, 0, 0)),
                  pl.BlockSpec((CHUNK, RW), lambda i: (phys(i), 0))] + ex.specs,
        out_specs=[pl.BlockSpec((CHUNK, 4 * RW), lambda i: (phys(i), 0)),
                   pl.BlockSpec((1, RW), lambda i: (0, 0))] + ex.specs,
        out_shape=[jax.ShapeDtypeStruct((R, 4 * RW), BF16), jax.ShapeDtypeStruct((1, RW), F32)] + ex.out_shape,
        scratch_shapes=[pltpu.VMEM((H, hd, hd), F32)] + ex.scratch,
        compiler_params=_params(("arbitrary",), 32),
    )(proj, cos, sin, jnp.asarray(mask), jnp.asarray(qd), jnp.asarray(kd), gn_g, states, dy, *ex_parts)
    return outs[0], outs[1], outs[2:]


def _dproj_specs(tk, tn, n_ret, tile_axis, col_axis):
    def ret_map(*ids):
        t, j = ids[tile_axis], ids[col_axis]
        return (jnp.where(j < n_ret, t, 0), jnp.minimum(j, n_ret - 1))

    def conv_map(*ids):
        t, j = ids[tile_axis], ids[col_axis]
        return (jnp.where(j >= n_ret, t, 0), jnp.maximum(j - n_ret, 0))

    return pl.BlockSpec((tk, tn), ret_map), pl.BlockSpec((tk, tn), conv_map)


def _w_in_grad(hn, dp_ret, dp_conv):
    R, D = hn.shape
    tn = _pick_tile(dp_conv.shape[1] // 3, 1024, 128)
    n_ret, n_conv = dp_ret.shape[1] // tn, dp_conv.shape[1] // tn
    E = dp_ret.shape[1] + dp_conv.shape[1]
    tk = _pick_tile(R, 1024, MXU_DIM)
    n_t = R // tk
    ret_spec, conv_spec = _dproj_specs(tk, tn, n_ret, 1, 0)

    def body(hn_ref, r_ref, c_ref, out_ref, acc):
        j, t = pl.program_id(0), pl.program_id(1)

        @pl.when(t == 0)
        def _():
            acc[...] = jnp.zeros_like(acc)

        @pl.when(j < n_ret)
        def _():
            acc[...] += _dot(hn_ref[...], r_ref[...], TN)

        @pl.when(j >= n_ret)
        def _():
            acc[...] += _dot(hn_ref[...], c_ref[...], TN)

        @pl.when(t == n_t - 1)
        def _():
            out_ref[...] = acc[...].astype(BF16)

    return pl.pallas_call(
        body, name="w_in_grad",
        grid=(n_ret + n_conv, n_t),
        in_specs=[pl.BlockSpec((tk, D), lambda j, t: (t, 0)), ret_spec, conv_spec],
        out_specs=pl.BlockSpec((D, tn), lambda j, t: (0, j)),
        out_shape=jax.ShapeDtypeStruct((D, E), BF16),
        scratch_shapes=[pltpu.VMEM((D, tn), F32)],
        compiler_params=_params(("arbitrary", "arbitrary"), 48),
    )(hn, dp_ret, dp_conv)


def _h_grad(name, dp_ret, dp_conv, w_in, h, dh2, ln_g, row0, dlg_init=None, ex=None, ex_parts=()):
    n, D = h.shape
    tn = _pick_tile(dp_conv.shape[1] // 3, 1024, 128)
    n_ret, n_conv = dp_ret.shape[1] // tn, dp_conv.shape[1] // tn
    n_k = n_ret + n_conv
    tm = _pick_tile(n, 1024)
    te = _pick_tile(tm, 256)
    n_e = tm // te
    assert row0 % tm == 0 and n_e <= n_k
    b0, e0 = row0 // tm, row0 // te
    n_ex = 0 if ex is None else ex.n
    n_t = n // tm

    def ret_map(t, k):
        return (jnp.where(k < n_ret, b0 + jnp.minimum(t, n_t - 1), b0), jnp.minimum(k, n_ret - 1))

    def conv_map(t, k):
        return (jnp.where(k >= n_ret, b0 + jnp.minimum(t, n_t - 1), b0), jnp.maximum(k - n_ret, 0))

    def rows_e(first):
        return lambda t, k: (first + jnp.maximum(t - 1, 0) * n_e + jnp.where(t > 0, jnp.minimum(k, n_e - 1), 0), 0)

    def body(*refs):
        r_ref, c_ref, w_ref, h_ref, dh2_ref, g_ref, init_ref = refs[:7]
        ex_ins = refs[7:7 + n_ex]
        o = 7 + n_ex
        dh_ref, dlg_ref = refs[o:o + 2]
        ex_outs = refs[o + 2:o + 2 + n_ex]
        acc = refs[o + 2 + n_ex]
        sems = refs[o + 3 + n_ex:]
        t, k = pl.program_id(0), pl.program_id(1)
        cur, old = t % 2, (t + 1) % 2

        @pl.when((k == 0) & (t == 0))
        def _():
            if ex is not None:
                ex.start(ex_ins, ex_outs, sems)
            dlg_ref[...] = init_ref[...]

        @pl.when((k < n_ret) & (t < n_t))
        def _():
            part = _dot(r_ref[...], w_ref[...], NT)

            @pl.when(k == 0)
            def _():
                acc[cur] = part

            @pl.when(k > 0)
            def _():
                acc[cur] += part

        @pl.when((k >= n_ret) & (t < n_t))
        def _():
            acc[cur] += _dot(c_ref[...], w_ref[...], NT)

        @pl.when((k < n_e) & (t > 0))
        def _():
            hv = h_ref[...]
            r = lax.rsqrt(jnp.mean(hv * hv, axis=-1, keepdims=True) + EPS)
            nrm = hv * r
            dhn = acc[old, pl.ds(pl.multiple_of(k * te, te), te), :]
            dlg_ref[...] += jnp.sum(dhn * nrm, axis=0, keepdims=True)
            dn = dhn * g_ref[...]
            dh_ref[...] = dh2_ref[...] + r * (dn - nrm * jnp.mean(dn * nrm, axis=-1, keepdims=True))

        if ex is not None:
            @pl.when((k == n_k - 1) & (t == n_t))
            def _():
                ex.wait(ex_ins, ex_outs, sems)

    row = lambda t, k: (0, 0)
    if dlg_init is None:
        dlg_init = jnp.zeros((1, D), F32)
    ex_specs, ex_shape, ex_scratch = ([], [], []) if ex is None else (ex.specs, ex.out_shape, ex.scratch)
    outs = pl.pallas_call(
        body, name=name,
        grid=(n_t + 1, n_k),
        in_specs=[pl.BlockSpec((tm, tn), ret_map), pl.BlockSpec((tm, tn), conv_map),
                  pl.BlockSpec((D, tn), lambda t, k: (0, k)),
                  pl.BlockSpec((te, D), rows_e(0)),
                  pl.BlockSpec((te, D), rows_e(e0)),
                  pl.BlockSpec((1, D), row), pl.BlockSpec((1, D), row)] + ex_specs,
        out_specs=[pl.BlockSpec((te, D), rows_e(0)),
                   pl.BlockSpec((1, D), row)] + ex_specs,
        out_shape=[jax.ShapeDtypeStruct((n, D), F32), jax.ShapeDtypeStruct((1, D), F32)] + ex_shape,
        scratch_shapes=[pltpu.VMEM((2, tm, D), F32)] + ex_scratch,
        compiler_params=_params(("arbitrary", "arbitrary"), 60),
    )(dp_ret, dp_conv, w_in, h, dh2, ln_g, dlg_init, *ex_parts)
    return outs[0], outs[1], outs[2:]


def _adamw(w, g, m, v):
    m = ADAM_B1 * m + (1.0 - ADAM_B1) * g
    v = ADAM_B2 * v + (1.0 - ADAM_B2) * (g * g)
    m_hat = m / (1.0 - ADAM_B1 ** ADAM_STEP)
    v_hat = v / (1.0 - ADAM_B2 ** ADAM_STEP)
    delta = -ADAM_LR * (m_hat / (jnp.sqrt(v_hat) + ADAM_EPS) + ADAM_WD * w)
    return delta, m, v


def _sum_slots(ref):
    g = ref[0].astype(F32)
    for s in range(1, N_DEV):
        g = g + ref[s].astype(F32)
    return g


def _sum_adamw(name, parts, w, m, v, rows_target, ex=None, ex_parts=()):
    rows, cols = w.shape
    tr = _pick_tile(rows, rows_target, 8)
    n_ex = 0 if ex is None else ex.n
    n_steps = rows // tr

    def body(*refs):
        p_ref, w_ref, m_ref, v_ref = refs[:4]
        ex_ins = refs[4:4 + n_ex]
        o = 4 + n_ex
        g_ref, d_ref, nm_ref, nv_ref = refs[o:o + 4]
        ex_outs, sems = refs[o + 4:o + 4 + n_ex], refs[o + 4 + n_ex:]
        if ex is not None:
            @pl.when(pl.program_id(0) == 0)
            def _():
                ex.start(ex_ins, ex_outs, sems)

        g = _sum_slots(p_ref)
        d, nm, nv = _adamw(w_ref[...], g, m_ref[...], v_ref[...])
        g_ref[...] = g
        d_ref[...] = d
        nm_ref[...] = nm
        nv_ref[...] = nv
        if ex is not None:
            @pl.when(pl.program_id(0) == n_steps - 1)
            def _():
                ex.wait(ex_ins, ex_outs, sems)

    tile = pl.BlockSpec((tr, cols), lambda i: (i, 0))
    ex_specs, ex_shape, ex_scratch = ([], [], []) if ex is None else (ex.specs, ex.out_shape, ex.scratch)
    outs = pl.pallas_call(
        body, name=name,
        grid=(n_steps,),
        in_specs=[pl.BlockSpec((N_DEV, tr, cols), lambda i: (0, i, 0)), tile, tile, tile] + ex_specs,
        out_specs=[tile] * 4 + ex_specs,
        out_shape=[jax.ShapeDtypeStruct((rows, cols), F32)] * 4 + ex_shape,
        scratch_shapes=ex_scratch,
        compiler_params=_params(("arbitrary",), 40),
    )(parts, w, m, v, *ex_parts)
    return outs[:4], outs[4:]


def _sum_adamw_small(parts_list, w_list, m_list, v_list, loss_parts):
    n = len(w_list)

    def body(*refs):
        p_refs, w_refs, m_refs, v_refs = refs[:n], refs[n:2 * n], refs[2 * n:3 * n], refs[3 * n:4 * n]
        lp_ref = refs[4 * n]
        outs = refs[4 * n + 1:]
        for a in range(n):
            g = _sum_slots(p_refs[a])
            d, nm, nv = _adamw(w_refs[a][...], g, m_refs[a][...], v_refs[a][...])
            outs[4 * a][...] = g
            outs[4 * a + 1][...] = d
            outs[4 * a + 2][...] = nm
            outs[4 * a + 3][...] = nv
        outs[4 * n][...] = _sum_slots(lp_ref)

    out_shape = []
    for w in w_list:
        out_shape += [jax.ShapeDtypeStruct(w.shape, F32)] * 4
    out_shape.append(jax.ShapeDtypeStruct(loss_parts.shape[1:], F32))
    return pl.pallas_call(body, name="sum_adamw_small", out_shape=out_shape)(
        *parts_list, *w_list, *m_list, *v_list, loss_parts)


def kernel(x, meta_tokens, ln_g, w_in, ret_gn_g, conv_dw_w, conv_dw_b, conv_ln_g, conv_ln_b, conv_pw_w, conv_pw_b, w_out, final_g, loss_target, m_meta_tokens, m_ln_g, m_w_in, m_ret_gn_g, m_conv_dw_w, m_conv_dw_b, m_conv_ln_g, m_conv_ln_b, m_conv_pw_w, m_conv_pw_b, m_w_out, m_final_g, v_meta_tokens, v_ln_g, v_w_in, v_ret_gn_g, v_conv_dw_w, v_conv_dw_b, v_conv_ln_g, v_conv_ln_b, v_conv_pw_w, v_conv_pw_b, v_w_out, v_final_g):
    _, SEQ, D = x.shape
    MIX = w_out.shape[2]
    RW = ret_gn_g.shape[1]
    CW = conv_pw_b.shape[1]
    assert RW == CW and MIX == RW + CW and SEQ % META_TILE == 0 and CONV_K - 1 <= HALO
    R = SEQ + META_TILE
    hd = RW // RET_HEADS
    half = hd // 2
    me = 4 * lax.axis_index("x") + 2 * lax.axis_index("y") + lax.axis_index("c")

    dw_pad = jnp.pad(conv_dw_w[0], ((0, HALO - CONV_K), (0, 0)))
    dw_g, meta_g = _gather_weights([dw_pad, meta_tokens], [1, 1])

    pos = jnp.concatenate([jnp.arange(SEQ, dtype=F32) + N_META, jnp.zeros((META_TILE - N_META,), F32),
                           jnp.arange(N_META, dtype=F32)])
    inv_freq = ROPE_BASE ** (-jnp.arange(half, dtype=F32) / half)
    ang = pos[:, None] * inv_freq[None, :]
    cos, sin = jnp.cos(ang), jnp.sin(ang)

    xs = x[0]
    meta_tile = jnp.concatenate([jnp.zeros((META_TILE - N_META, D), F32), meta_g], axis=0)
    target = loss_target[0]
    final_g2 = final_g[None, :]

    hn = _rms_norm(xs, meta_tile, ln_g)
    pw_shard, w_out_shard = [conv_pw_w[0].astype(BF16)], [w_out[0].astype(BF16)]
    proj, w_in_g, (pw_g,) = _in_proj_gather(hn, w_in[0].astype(BF16), _Exchange(pw_shard, [None]), pw_shard)
    pw_g = pw_g.reshape(CW, CW)
    y, states = _retention_fwd(proj, cos, sin, ret_gn_g, MIX)
    y, conv_out, (w_out_g,) = _conv_fwd(proj, y, dw_g, conv_dw_b, conv_ln_g, conv_ln_b, pw_g, conv_pw_b,
                                        _Exchange(w_out_shard, [None]), w_out_shard)
    w_out_g = w_out_g.reshape(MIX, D)
    dh2, dy, dwo_p, dfg_p, loss_p = _out_proj_loss(xs, meta_tile, y, w_out_g, final_g2, target)

    dp_conv, dpw_p, dww_p, cvec_p = _conv_bwd(proj, conv_out, dy, dw_g, conv_ln_g, conv_ln_b, pw_g, conv_pw_b)
    dp_ret, dgn_p, (r_wo, r_pw) = _retention_bwd(proj, cos, sin, ret_gn_g, states, dy,
                                                 _Exchange([dwo_p, dpw_p], [0, 0]), [dwo_p, dpw_p])
    dwi_p = _w_in_grad(hn, dp_ret, dp_conv)
    grad_xs, dlg_x, (r_wi,) = _h_grad("h_grad", dp_ret, dp_conv, w_in_g, xs, dh2, ln_g, 0,
                                      ex=_Exchange([dwi_p], [1]), ex_parts=[dwi_p])
    dh_meta, dlg_p, _ = _h_grad("h_grad_meta", dp_ret, dp_conv, w_in_g, meta_tile, dh2, ln_g, SEQ, dlg_init=dlg_x)
    grad_x = grad_xs[None]

    def at_row(r, a, b=None):
        v = a if b is None else jnp.concatenate([a, b], axis=1)
        return jnp.pad(v, ((r, 7 - r), (0, D - v.shape[1])))
    vec8 = (at_row(0, dlg_p) + at_row(1, dfg_p)
            + at_row(2, dgn_p, cvec_p[3:4])
            + at_row(3, cvec_p[1:2], cvec_p[2:3])
            + at_row(4, cvec_p[0:1], jnp.broadcast_to(loss_p, (1, CW))))
    small = jnp.concatenate([dh_meta[META_TILE - N_META:], vec8,
                             jnp.zeros((SMALL_ROWS - N_META - 8, D), F32)], axis=0)

    (g_wi, d_wi, nm_wi, nv_wi), (r_dww, r_small) = _sum_adamw(
        "sum_adamw_w_in", r_wi, w_in[0], m_w_in[0], v_w_in[0], 256,
        ex=_Exchange([dww_p, small], [1, None]), ex_parts=[dww_p, small])
    (g_wo, d_wo, nm_wo, nv_wo), _ = _sum_adamw("sum_adamw_w_out", r_wo, w_out[0], m_w_out[0], v_w_out[0], 128)
    (g_pw, d_pw, nm_pw, nv_pw), _ = _sum_adamw("sum_adamw_pw", r_pw, conv_pw_w[0], m_conv_pw_w[0], v_conv_pw_w[0], 128)

    dcol = D // N_DEV
    sm = lambda r0, nr, c0, nc: lax.slice(r_small, (0, r0, c0), (N_DEV, r0 + nr, c0 + nc))
    meta_parts = lax.dynamic_slice(r_small, (0, 0, me * dcol), (N_DEV, N_META, dcol))
    small_parts = [meta_parts, sm(16, 1, 0, D), sm(18, 1, 0, RW), r_dww, sm(18, 1, RW, CW),
                   sm(19, 1, 0, CW), sm(19, 1, CW, CW), sm(20, 1, 0, CW), sm(17, 1, 0, D)]
    pad31 = lambda a: jnp.pad(a, ((0, HALO - CONV_K), (0, 0)))
    ws = [meta_tokens, ln_g, ret_gn_g, pad31(conv_dw_w[0]), conv_dw_b, conv_ln_g, conv_ln_b, conv_pw_b, final_g2]
    ms = [m_meta_tokens, m_ln_g, m_ret_gn_g, pad31(m_conv_dw_w[0]), m_conv_dw_b, m_conv_ln_g, m_conv_ln_b,
          m_conv_pw_b, m_final_g[None, :]]
    vs = [v_meta_tokens, v_ln_g, v_ret_gn_g, pad31(v_conv_dw_w[0]), v_conv_dw_b, v_conv_ln_g, v_conv_ln_b,
          v_conv_pw_b, v_final_g[None, :]]
    loss_parts = sm(20, 1, CW, 1)
    outs = _sum_adamw_small(small_parts, ws, ms, vs, loss_parts)
    loss = outs[-1][0, 0]
    quad = [outs[4 * a:4 * a + 4] for a in range(len(ws))]
    (q_meta, q_lng, q_gn, q_dww, q_dwb, q_clg, q_clb, q_pwb, q_fg) = quad
    q_dww = [t[:CONV_K][None] for t in q_dww]
    q_fg = [t[0] for t in q_fg]
    q_wi = [t[None] for t in (g_wi, d_wi, nm_wi, nv_wi)]
    q_wo = [t[None] for t in (g_wo, d_wo, nm_wo, nv_wo)]
    q_pw = [t[None] for t in (g_pw, d_pw, nm_pw, nv_pw)]

    per_w = [q_meta, q_lng, q_wi, q_gn, q_dww, q_dwb, q_clg, q_clb, q_pw, q_pwb, q_wo, q_fg]
    result = [loss, grad_x]
    for which in range(4):
        result += [q[which] for q in per_w]
    return tuple(result)
```

```python
import functools

import numpy as np
import jax
import jax.numpy as jnp
from jax import lax
from jax.experimental import pallas as pl
from jax.experimental.pallas import tpu as pltpu

N_META = 16
RET_HEADS = 4
CONV_K = 31
CHUNK = 128
ROPE_BASE = 10000.0
EPS = 1e-6
ADAM_LR = 0.001
ADAM_B1 = 0.9
ADAM_B2 = 0.999
ADAM_EPS = 1e-08
ADAM_WD = 0.01
ADAM_STEP = 10

N_DEV = 8
META_TILE = 256
HALO = 32
SMALL_ROWS = 32
VMEM_BYTES_V7X = 64 * 1024 * 1024
MXU_DIM = 256

F32 = jnp.float32
BF16 = jnp.bfloat16
MESH = pl.DeviceIdType.MESH

NN = (((1,), (0,)), ((), ()))
NT = (((1,), (1,)), ((), ()))
TN = (((0,), (0,)), ((), ()))


def _dot(a, b, dims=NN):
    return lax.dot_general(a, b, dims, preferred_element_type=F32)


def _pick_tile(n, target, mult=16):
    best = None
    for t in range(mult, min(n, target) + 1, mult):
        if n % t == 0:
            best = t
    assert best is not None, (n, target)
    return best


def _params(sem=None, vmem_mb=None):
    kw = {}
    if sem is not None:
        kw["dimension_semantics"] = sem
    if vmem_mb is not None:
        kw["vmem_limit_bytes"] = min(vmem_mb * 1024 * 1024, VMEM_BYTES_V7X - 4 * 1024 * 1024)
    return pltpu.CompilerParams(**kw)


def _sigmoid(x):
    return jax.nn.sigmoid(x)


def _dsilu(x, sg):
    return sg * (1.0 + x * (1.0 - sg))


def _decay_tables(heads):
    h = np.arange(heads, dtype=np.float32)
    gamma = (1.0 - np.exp2(-5.0 - h)).astype(np.float32)
    log_g = np.log(gamma).astype(np.float32)
    idx = np.arange(CHUNK, dtype=np.float32)
    rel = idx[:, None] - idx[None, :]
    mask = np.where(rel[None] >= 0, np.exp(np.maximum(rel, 0.0)[None] * log_g[:, None, None]), 0.0)
    qd = np.exp((idx[None, :] + 1.0) * log_g[:, None])
    kd = np.exp((CHUNK - 1.0 - idx[None, :]) * log_g[:, None])
    cd = np.exp(CHUNK * log_g)
    return (mask.astype(np.float32), qd.astype(np.float32)[:, :, None], kd.astype(np.float32)[:, :, None],
            [float(c) for c in cd.astype(np.float32)])


def _gather_weights(shards, block_axes):
    n_arr = len(shards)
    out_shapes = []
    for s, ax in zip(shards, block_axes):
        shp = list(s.shape)
        shp[ax] *= N_DEV
        out_shapes.append(jax.ShapeDtypeStruct(tuple(shp), s.dtype))

    def body(*refs):
        ins, outs = refs[:n_arr], refs[n_arr:2 * n_arr]
        send_sems, recv_sems, local_sems = refs[2 * n_arr:]
        x, y, c = lax.axis_index("x"), lax.axis_index("y"), lax.axis_index("c")
        me, sibling = (x, y, c), (x, y, 1 - c)
        chips = [(1 - x, y), (x, 1 - y), (1 - x, 1 - y)]

        def block(a, dev):
            n = ins[a].shape[block_axes[a]]
            start = pl.multiple_of((4 * dev[0] + 2 * dev[1] + dev[2]) * n, n)
            idx = [slice(None)] * len(ins[a].shape)
            idx[block_axes[a]] = pl.ds(start, n)
            return outs[a].at[tuple(idx)]

        def copy(a, k, dev, to, src=None):
            return pltpu.make_async_remote_copy(
                src_ref=block(a, dev) if src is None else src, dst_ref=block(a, dev),
                send_sem=send_sems.at[a, k], recv_sem=recv_sems.at[a, k],
                device_id=to, device_id_type=MESH)

        mine = [pltpu.make_async_copy(ins[a], block(a, me), local_sems.at[a]) for a in range(n_arr)]
        for cp in mine:
            cp.start()
        first = []
        for a in range(n_arr):
            first.append(copy(a, 0, me, sibling, src=ins[a]))
            first += [copy(a, 1 + j, me, (*chip, c), src=ins[a]) for j, chip in enumerate(chips)]
        for cp in first:
            cp.start()
        passed = []
        for j, chip in enumerate(chips):
            for a in range(n_arr):
                copy(a, 1 + j, (*chip, c), me).wait_recv()
                fwd = copy(a, 4 + j, (*chip, c), sibling)
                fwd.start()
                passed.append(fwd)
        for a in range(n_arr):
            copy(a, 0, sibling, me).wait_recv()
            for j, chip in enumerate(chips):
                copy(a, 4 + j, (*chip, 1 - c), me).wait_recv()
        for cp in first + passed:
            cp.wait_send()
        for cp in mine:
            cp.wait()

    hbm = pl.BlockSpec(memory_space=pl.ANY)
    return pl.pallas_call(
        body, name="gather_weights",
        out_shape=out_shapes,
        in_specs=[hbm] * n_arr, out_specs=[hbm] * n_arr,
        scratch_shapes=[pltpu.SemaphoreType.DMA((n_arr, 7)), pltpu.SemaphoreType.DMA((n_arr, 7)),
                        pltpu.SemaphoreType.DMA((n_arr,))],
    )(*shards)


class _Exchange:
    def __init__(self, parts, block_axes):
        self.block_axes = list(block_axes)
        self.n = len(parts)
        self.out_shape = []
        for p, ax in zip(parts, block_axes):
            shp = list(p.shape)
            if ax is not None:
                assert shp[ax] % N_DEV == 0
                shp[ax] //= N_DEV
            self.out_shape.append(jax.ShapeDtypeStruct((N_DEV, *shp), p.dtype))
        self.scratch = [pltpu.SemaphoreType.DMA((self.n, N_DEV - 1)), pltpu.SemaphoreType.DMA((self.n, N_DEV - 1)),
                        pltpu.SemaphoreType.DMA((self.n,))]
        self.specs = [pl.BlockSpec(memory_space=pl.ANY)] * self.n

    def _copies(self, ins, outs, sems):
        send_sems, recv_sems, local_sems = sems
        x, y, c = lax.axis_index("x"), lax.axis_index("y"), lax.axis_index("c")
        me_idx = 4 * x + 2 * y + c

        def src_block(a, dev_idx):
            ax = self.block_axes[a]
            if ax is None:
                return ins[a]
            n = ins[a].shape[ax] // N_DEV
            idx = [slice(None)] * len(ins[a].shape)
            idx[ax] = pl.ds(pl.multiple_of(dev_idx * n, n), n)
            return ins[a].at[tuple(idx)]

        local = [pltpu.make_async_copy(src_block(a, me_idx), outs[a].at[me_idx], local_sems.at[a])
                 for a in range(self.n)]
        remote = []
        for m in range(1, N_DEV):
            px, py, pc = x ^ ((m >> 2) & 1), y ^ ((m >> 1) & 1), c ^ (m & 1)
            for a in range(self.n):
                remote.append(pltpu.make_async_remote_copy(
                    src_ref=src_block(a, 4 * px + 2 * py + pc), dst_ref=outs[a].at[me_idx],
                    send_sem=send_sems.at[a, m - 1], recv_sem=recv_sems.at[a, m - 1],
                    device_id=(px, py, pc), device_id_type=MESH))
        return local, remote

    def start(self, ins, outs, sems):
        local, remote = self._copies(ins, outs, sems)
        for cp in local + remote:
            cp.start()

    def wait(self, ins, outs, sems):
        local, remote = self._copies(ins, outs, sems)
        for cp in remote:
            cp.wait_recv()
        for cp in remote:
            cp.wait_send()
        for cp in local:
            cp.wait()


def _rms_norm(xs, meta_tile, ln_g):
    SEQ, D = xs.shape
    tm = meta_tile.shape[0]
    n_seq = SEQ // tm

    def body(x_ref, mt_ref, g_ref, hn_ref):
        hv = jnp.where(pl.program_id(0) < n_seq, x_ref[...], mt_ref[...])
        r = lax.rsqrt(jnp.mean(hv * hv, axis=-1, keepdims=True) + EPS)
        hn_ref[...] = (hv * r * g_ref[...]).astype(BF16)

    return pl.pallas_call(
        body, name="rms_norm",
        grid=(n_seq + 1,),
        in_specs=[pl.BlockSpec((tm, D), lambda i: (jnp.minimum(i, n_seq - 1), 0)),
                  pl.BlockSpec((tm, D), lambda i: (0, 0)),
                  pl.BlockSpec((1, D), lambda i: (0, 0))],
        out_specs=pl.BlockSpec((tm, D), lambda i: (i, 0)),
        out_shape=jax.ShapeDtypeStruct((SEQ + tm, D), BF16),
        compiler_params=_params(("arbitrary",), 32),
    )(xs, meta_tile, ln_g)


def _chip_visited(q):
    mine = 2 * lax.axis_index("x") + lax.axis_index("y")
    return mine ^ (((q & 1) << 1) | (q >> 1))


def _in_proj_gather(hn, w_shard, ex, ex_parts):
    R, D = hn.shape
    wb = w_shard.shape[1]
    E, tn = wb * N_DEV, 2 * wb
    n_q = N_DEV // 2
    tm = _pick_tile(R, min(768, R // 2), MXU_DIM)
    n_i = R // tm

    def body(*refs):
        hn_ref, wsh_hbm = refs[:2]
        ex_ins = refs[2:2 + ex.n]
        proj_ref, wg_hbm = refs[2 + ex.n:4 + ex.n]
        ex_outs = refs[4 + ex.n:4 + 2 * ex.n]
        w_vmem, send_sems, recv_sems, local_sem, vmem_sems = refs[4 + 2 * ex.n:9 + 2 * ex.n]
        ex_sems = refs[9 + 2 * ex.n:]
        q, i = pl.program_id(0), pl.program_id(1)
        x, y, c = lax.axis_index("x"), lax.axis_index("y"), lax.axis_index("c")
        me, sibling = (x, y, c), (x, y, 1 - c)
        chips = [(1 - x, y), (x, 1 - y), (1 - x, 1 - y)]

        def block(dev):
            return wg_hbm.at[:, pl.ds(pl.multiple_of((4 * dev[0] + 2 * dev[1] + dev[2]) * wb, wb), wb)]

        def copy(k, dev, to, src=None):
            return pltpu.make_async_remote_copy(
                src_ref=block(dev) if src is None else src, dst_ref=block(dev),
                send_sem=send_sems.at[k], recv_sem=recv_sems.at[k], device_id=to, device_id_type=MESH)

        def to_vmem(p):
            cols = pl.ds(pl.multiple_of(_chip_visited(p) * tn, tn), tn)
            return pltpu.make_async_copy(wg_hbm.at[:, cols], w_vmem.at[p % 2], vmem_sems.at[p % 2])

        mine = pltpu.make_async_copy(wsh_hbm, block(me), local_sem)
        first = [copy(0, me, sibling, src=wsh_hbm)] + [copy(1 + j, me, (*chip, c), src=wsh_hbm)
                                                       for j, chip in enumerate(chips)]
        passed = [copy(4 + j, (*chip, c), sibling) for j, chip in enumerate(chips)]

        @pl.when((q == 0) & (i == 0))
        def _():
            mine.start()
            for cp in first:
                cp.start()
            ex.start(ex_ins, ex_outs, ex_sems)
            mine.wait()
            copy(0, sibling, me).wait_recv()
            to_vmem(0).start()
            to_vmem(0).wait()

        for p in range(1, n_q):
            chip = chips[p - 1]

            @pl.when((q == p - 1) & (i == n_i - 2))
            def _():
                copy(p, (*chip, c), me).wait_recv()
                passed[p - 1].start()

            @pl.when((q == p - 1) & (i == n_i - 1))
            def _():
                copy(3 + p, (*chip, 1 - c), me).wait_recv()
                to_vmem(p).start()

            @pl.when((q == p) & (i == 0))
            def _():
                to_vmem(p).wait()

        proj_ref[...] = _dot(hn_ref[...], w_vmem[q % 2]).astype(BF16)

        @pl.when((q == n_q - 1) & (i == n_i - 1))
        def _():
            for cp in first + passed:
                cp.wait_send()
            ex.wait(ex_ins, ex_outs, ex_sems)

    hbm = pl.BlockSpec(memory_space=pl.ANY)
    outs = pl.pallas_call(
        body, name="in_proj",
        grid=(n_q, n_i),
        in_specs=[pl.BlockSpec((tm, D), lambda q, i: (i, 0)), hbm] + ex.specs,
        out_specs=[pl.BlockSpec((tm, tn), lambda q, i: (i, _chip_visited(q))), hbm] + ex.specs,
        out_shape=[jax.ShapeDtypeStruct((R, E), BF16), jax.ShapeDtypeStruct((D, E), BF16)] + ex.out_shape,
        scratch_shapes=[pltpu.VMEM((2, D, tn), BF16), pltpu.SemaphoreType.DMA((7,)), pltpu.SemaphoreType.DMA((7,)),
                        pltpu.SemaphoreType.DMA, pltpu.SemaphoreType.DMA((2,))] + ex.scratch,
        compiler_params=_params(("arbitrary", "arbitrary"), 48),
    )(hn, w_shard, *ex_parts)
    return outs[0], outs[1], outs[2:]


def _rope_chunk(cb_ref, sb_ref, ci_ref, si_ref):
    cb, sb, ci, si = cb_ref[0], sb_ref[0], ci_ref[...], si_ref[...]
    return cb * ci - sb * si, sb * ci + cb * si


def _rot(t, cos, sin, half):
    t1, t2 = t[:, :half], t[:, half:]
    return jnp.concatenate([t1 * cos - t2 * sin, t1 * sin + t2 * cos], axis=-1)


def _rot_inv(t, cos, sin, half):
    t1, t2 = t[:, :half], t[:, half:]
    return jnp.concatenate([t1 * cos + t2 * sin, t2 * cos - t1 * sin], axis=-1)


def _chunk_order(n_chunks):
    lead = META_TILE // CHUNK
    return lambda l: (l + n_chunks - lead) % n_chunks


def _retention_fwd(proj, rope, gn_g, mix):
    R, E = proj.shape
    RW = gn_g.shape[1]
    H = RET_HEADS
    hd = RW // H
    half = hd // 2
    NC = R // CHUNK
    mask, qd, kd, cd = _decay_tables(H)
    scale = float(hd) ** -0.5
    phys = _chunk_order(NC)

    def body(p_ref, cb_ref, sb_ref, ci_ref, si_ref, mask_ref, qd_ref, kd_ref, gn_ref, y_ref, st_ref, state):
        @pl.when(pl.program_id(0) == 0)
        def _():
            state[...] = jnp.zeros_like(state)

        cs, sn = _rope_chunk(cb_ref, sb_ref, ci_ref, si_ref)
        hs = range(H)
        col = lambda j, h: slice(j * RW + h * hd, j * RW + (h + 1) * hd)
        qr = [_rot(p_ref[:, col(0, h)].astype(F32), cs, sn, half) for h in hs]
        kr = [_rot(p_ref[:, col(1, h)].astype(F32), cs, sn, half) * scale for h in hs]
        v = [p_ref[:, col(2, h)] for h in hs]
        s_prev = [state[h] for h in hs]
        s_prev_b = [s_prev[h].astype(BF16) for h in hs]
        s = [(_dot(qr[h].astype(BF16), kr[h].astype(BF16), NT) * mask_ref[h]).astype(BF16) for h in hs]
        y_raw = [_dot(s[h], v[h]) + _dot((qr[h] * qd_ref[h]).astype(BF16), s_prev_b[h]) for h in hs]
        s_new = [s_prev[h] * cd[h] + _dot((kr[h] * kd_ref[h]).astype(BF16), v[h], TN) for h in hs]
        for h in hs:
            st_ref[0, h] = s_prev_b[h]
            state[h] = s_new[h]
        for h in hs:
            g = p_ref[:, col(3, h)].astype(F32)
            mu = jnp.mean(y_raw[h], axis=-1, keepdims=True)
            yc = y_raw[h] - mu
            var = jnp.mean(yc * yc, axis=-1, keepdims=True)
            out = yc * lax.rsqrt(var + EPS) * gn_ref[:, col(0, h)] * (g * _sigmoid(g))
            y_ref[:, col(0, h)] = out.astype(BF16)

    const3 = lambda l: (0, 0, 0)
    return pl.pallas_call(
        body, name="retention_fwd",
        grid=(NC,),
        in_specs=[pl.BlockSpec((CHUNK, 4 * RW), lambda l: (phys(l), 0)),
                  pl.BlockSpec((1, 1, half), lambda l: (phys(l), 0, 0)),
                  pl.BlockSpec((1, 1, half), lambda l: (phys(l), 0, 0)),
                  pl.BlockSpec((CHUNK, half), lambda l: (0, 0)),
                  pl.BlockSpec((CHUNK, half), lambda l: (0, 0)),
                  pl.BlockSpec((H, CHUNK, CHUNK), const3),
                  pl.BlockSpec((H, CHUNK, 1), const3),
                  pl.BlockSpec((H, CHUNK, 1), const3),
                  pl.BlockSpec((1, RW), lambda l: (0, 0))],
        out_specs=[pl.BlockSpec((CHUNK, RW), lambda l: (phys(l), 0)),
                   pl.BlockSpec((1, H, hd, hd), lambda l: (phys(l), 0, 0, 0))],
        out_shape=[jax.ShapeDtypeStruct((R, mix), BF16), jax.ShapeDtypeStruct((NC, H, hd, hd), BF16)],
        scratch_shapes=[pltpu.VMEM((H, hd, hd), F32)],
        compiler_params=_params(("arbitrary",), 32),
    )(proj, *rope, jnp.asarray(mask), jnp.asarray(qd), jnp.asarray(kd), gn_g)


CONV_ROWS = 64
CONV_LANES = 128
LANE = 128
ELEM_ROWS = 32


def _conv_order(n_tiles):
    return lambda l: (l + n_tiles - 1) % n_tiles


def _halo_block(n_tiles, tm):
    per = tm // HALO
    return lambda l: ((l + n_tiles - 2) % n_tiles) * per + per - 1


def _fill_shifted(src, dst):
    rows, width = dst.shape[1], dst.shape[2]
    step = _pick_tile(rows, 64, 8)
    for r in range(1, 8):
        for r0 in range(0, rows, step):
            for l0 in range(0, width, CONV_LANES):
                dst[r - 1, r0:r0 + step, l0:l0 + CONV_LANES] = src[r + r0:r + r0 + step, l0:l0 + CONV_LANES]


def _at_offset(src, shifted, off, r0, rows, lanes):
    r = off % 8
    a = off - r + r0
    if r == 0:
        return src[a:a + rows, lanes]
    return shifted[r - 1, a:a + rows, lanes]


def _fill_glu(first, a_ref, b_ref, ah_ref, bh_ref, u_ext, tm):
    uh = ah_ref[...].astype(F32) * _sigmoid(bh_ref[...].astype(F32))
    u_ext[0:HALO, :] = jnp.where(first, 0.0, uh)
    for r0 in range(0, tm, ELEM_ROWS):
        rows = slice(r0, r0 + ELEM_ROWS)
        u_ext[HALO + r0:HALO + r0 + ELEM_ROWS, :] = a_ref[rows, :].astype(F32) * _sigmoid(b_ref[rows, :].astype(F32))


def _layer_norm(cv, lg_ref, lb_ref):
    mu = jnp.mean(cv, axis=-1, keepdims=True)
    cc = cv - mu
    rstd = lax.rsqrt(jnp.mean(cc * cc, axis=-1, keepdims=True) + EPS)
    xh = cc * rstd
    return xh, rstd, xh * lg_ref[...] + lb_ref[...]


def _conv_fwd(proj, y_in, dw_w, dw_b, ln_g, ln_b, pw_w, pw_b, ex, ex_parts):
    R, E = proj.shape
    CW = pw_w.shape[0]
    tm = META_TILE
    NTL = R // tm
    phys = _conv_order(NTL)
    halo = _halo_block(NTL, tm)
    cb = (E - 3 * CW) // CW
    base = HALO - (CONV_K - 1)

    def body(*refs):
        a_ref, b_ref, g_ref, ah_ref, bh_ref, w_ref, wb_ref, lg_ref, lb_ref, pw_ref, pb_ref, yin_ref = refs[:12]
        ex_ins = refs[12:12 + ex.n]
        y_ref, c_ref = refs[12 + ex.n:14 + ex.n]
        ex_outs = refs[14 + ex.n:14 + 2 * ex.n]
        u_ext, u_sh, s_scr, upw_scr = refs[14 + 2 * ex.n:18 + 2 * ex.n]
        sems = refs[18 + 2 * ex.n:]

        @pl.when(pl.program_id(0) == 0)
        def _():
            ex.start(ex_ins, ex_outs, sems)

        _fill_glu(pl.program_id(0) == 0, a_ref, b_ref, ah_ref, bh_ref, u_ext, tm)
        _fill_shifted(u_ext, u_sh)
        for r0 in range(0, tm, CONV_ROWS):
            for l0 in range(0, CW, CONV_LANES):
                lanes = slice(l0, l0 + CONV_LANES)
                acc = None
                for k in range(CONV_K):
                    term = _at_offset(u_ext, u_sh, base + k, r0, CONV_ROWS, lanes) * w_ref[k:k + 1, lanes]
                    acc = term if acc is None else acc + term
                c_ref[r0:r0 + CONV_ROWS, lanes] = acc + wb_ref[:, lanes]
        blocks = [slice(r0, r0 + ELEM_ROWS) for r0 in range(0, tm, ELEM_ROWS)]
        for rows in blocks:
            _, _, ln = _layer_norm(c_ref[rows, :], lg_ref, lb_ref)
            s_scr[rows, :] = (ln * _sigmoid(ln)).astype(BF16)
        upw_scr[...] = _dot(s_scr[...], pw_ref[...]) + pb_ref[...]
        for rows in blocks:
            g = g_ref[rows, :].astype(F32)
            y_ref[rows, :] = (upw_scr[rows, :] * (g * _sigmoid(g))).astype(BF16)

        @pl.when(pl.program_id(0) == NTL - 1)
        def _():
            ex.wait(ex_ins, ex_outs, sems)

    row = lambda l: (0, 0)
    outs = pl.pallas_call(
        body, name="conv_fwd",
        grid=(NTL,),
        in_specs=[pl.BlockSpec((tm, CW), lambda l: (phys(l), cb)),
                  pl.BlockSpec((tm, CW), lambda l: (phys(l), cb + 1)),
                  pl.BlockSpec((tm, CW), lambda l: (phys(l), cb + 2)),
                  pl.BlockSpec((HALO, CW), lambda l: (halo(l), cb)),
                  pl.BlockSpec((HALO, CW), lambda l: (halo(l), cb + 1)),
                  pl.BlockSpec((HALO, CW), row),
                  pl.BlockSpec((1, CW), row), pl.BlockSpec((1, CW), row), pl.BlockSpec((1, CW), row),
                  pl.BlockSpec((CW, CW), row),
                  pl.BlockSpec((1, CW), row),
                  pl.BlockSpec(memory_space=pl.ANY)] + ex.specs,
        out_specs=[pl.BlockSpec((tm, CW), lambda l: (phys(l), 1)),
                   pl.BlockSpec((tm, CW), lambda l: (phys(l), 0))] + ex.specs,
        out_shape=[jax.ShapeDtypeStruct(y_in.shape, BF16), jax.ShapeDtypeStruct((R, CW), F32)] + ex.out_shape,
        input_output_aliases={11: 0},
        scratch_shapes=[pltpu.VMEM((HALO + tm, CW), F32), pltpu.VMEM((7, tm + HALO - 8, CW), F32),
                        pltpu.VMEM((tm, CW), BF16), pltpu.VMEM((tm, CW), F32)] + ex.scratch,
        compiler_params=_params(("arbitrary",), 48),
    )(proj, proj, proj, proj, proj, dw_w, dw_b, ln_g, ln_b, pw_w, pw_b, y_in, *ex_parts)
    return outs[0], outs[1], outs[2:]


def _out_proj_loss(xs, meta_tile, y, w_out, final_g, target):
    SEQ, D = xs.shape
    R, MIX = y.shape
    tm = META_TILE
    n_seq = SEQ // tm
    n_tiles = R // tm
    rows_out = _pick_tile(MIX, 256)

    def body(x_ref, mt_ref, y_ref, w_hbm, fg_ref, t_ref, dh2_ref, dy_ref, dwo_hbm, dfg_ref, loss_ref,
             w_scr, acc, stage, sem):
        i = pl.program_id(0)

        @pl.when(i == 0)
        def _():
            cp = pltpu.make_async_copy(w_hbm, w_scr, sem)
            cp.start()
            acc[...] = jnp.zeros_like(acc)
            dfg_ref[...] = jnp.zeros_like(dfg_ref)
            loss_ref[...] = jnp.zeros_like(loss_ref)
            cp.wait()

        yb = y_ref[...]
        h2 = jnp.where(i < n_seq, x_ref[...], mt_ref[...]) + _dot(yb, w_scr[...])
        r2 = lax.rsqrt(jnp.mean(h2 * h2, axis=-1, keepdims=True) + EPS)
        n = h2 * r2
        fg = fg_ref[...]
        err = jnp.where(i < n_seq, n * fg - t_ref[...], 0.0)
        loss_ref[...] += 0.5 * jnp.sum(jnp.mean(err * err, axis=-1, keepdims=True), axis=0, keepdims=True)
        dout = err * (1.0 / D)
        dfg_ref[...] += jnp.sum(dout * n, axis=0, keepdims=True)
        dn = dout * fg
        dh2 = r2 * (dn - n * jnp.mean(dn * n, axis=-1, keepdims=True))
        dh2_ref[...] = dh2
        dh2b = dh2.astype(BF16)
        dy_ref[...] = _dot(dh2b, w_scr[...], NT).astype(BF16)
        acc[...] += _dot(yb, dh2b, TN)

        @pl.when(i == n_tiles - 1)
        def _():
            for r in range(0, MIX, rows_out):
                stage[...] = acc[r:r + rows_out, :].astype(BF16)
                cp = pltpu.make_async_copy(stage, dwo_hbm.at[r:r + rows_out, :], sem)
                cp.start()
                cp.wait()

    row = lambda i: (0, 0)
    return pl.pallas_call(
        body, name="out_proj_loss",
        grid=(n_tiles,),
        in_specs=[pl.BlockSpec((tm, D), lambda i: (jnp.minimum(i, n_seq - 1), 0)),
                  pl.BlockSpec((tm, D), row),
                  pl.BlockSpec((tm, MIX), lambda i: (i, 0)),
                  pl.BlockSpec(memory_space=pl.ANY),
                  pl.BlockSpec((1, D), row),
                  pl.BlockSpec((tm, D), lambda i: (jnp.minimum(i, n_seq - 1), 0))],
        out_specs=[pl.BlockSpec((tm, D), lambda i: (i, 0)),
                   pl.BlockSpec((tm, MIX), lambda i: (i, 0)),
                   pl.BlockSpec(memory_space=pl.ANY),
                   pl.BlockSpec((1, D), row),
                   pl.BlockSpec((1, 1), row)],
        out_shape=[jax.ShapeDtypeStruct((R, D), F32), jax.ShapeDtypeStruct((R, MIX), BF16),
                   jax.ShapeDtypeStruct((MIX, D), BF16), jax.ShapeDtypeStruct((1, D), F32),
                   jax.ShapeDtypeStruct((1, 1), F32)],
        scratch_shapes=[pltpu.VMEM((MIX, D), BF16), pltpu.VMEM((MIX, D), F32), pltpu.VMEM((rows_out, D), BF16),
                        pltpu.SemaphoreType.DMA],
        compiler_params=_params(("arbitrary",), 60),
    )(xs, meta_tile, y, w_out, final_g, target)


def _conv_bwd(proj, conv_out, dy, dw_w, ln_g, ln_b, pw_w, pw_b):
    R, E = proj.shape
    CW = pw_w.shape[0]
    tm = META_TILE
    NTL = R // tm
    order = _conv_order(NTL)
    phys = lambda i: order(NTL - 1 - i)
    halo_l = _halo_block(NTL, tm)
    halo = lambda i: halo_l(NTL - 1 - i)
    cb = (E - 3 * CW) // CW
    base = HALO - (CONV_K - 1)

    def body(a_ref, b_ref, g_ref, ah_ref, bh_ref, c_ref, dy_ref, w_ref, lg_ref, lb_ref, pw_ref, pb_ref,
             dp_ref, dpw_ref, dww_ref, vec_ref, u_ext, u_sh, dc_ext, dc_sh, du_scr, dww_acc, dpw_acc,
             xh_scr, rstd_scr, ln_scr, sg_scr, upw_scr, s_scr, dupw_scr):
        i = pl.program_id(0)

        @pl.when(i == 0)
        def _():
            dpw_acc[...] = jnp.zeros_like(dpw_acc)
            dww_ref[...] = jnp.zeros_like(dww_ref)
            vec_ref[...] = jnp.zeros_like(vec_ref)
            dww_acc[...] = jnp.zeros_like(dww_acc)
            dc_ext[tm:tm + HALO, :] = jnp.zeros((HALO, CW), F32)

        _fill_glu(i == NTL - 1, a_ref, b_ref, ah_ref, bh_ref, u_ext, tm)
        _fill_shifted(u_ext, u_sh)
        blocks = [slice(r0, r0 + ELEM_ROWS) for r0 in range(0, tm, ELEM_ROWS)]
        for rows in blocks:
            xh, rstd, ln = _layer_norm(c_ref[rows, :], lg_ref, lb_ref)
            sg = _sigmoid(ln)
            xh_scr[rows, :], rstd_scr[rows, :], ln_scr[rows, :], sg_scr[rows, :] = xh, rstd, ln, sg
            s_scr[rows, :] = (ln * sg).astype(BF16)
        upw_scr[...] = _dot(s_scr[...], pw_ref[...]) + pb_ref[...]
        col_sum = jnp.zeros((1, CW), F32)
        for rows in blocks:
            g = g_ref[rows, :].astype(F32)
            sgg = _sigmoid(g)
            dyc = dy_ref[rows, :].astype(F32)
            dp_ref[rows, 2 * CW:3 * CW] = (dyc * upw_scr[rows, :] * _dsilu(g, sgg)).astype(BF16)
            dupw = dyc * (g * sgg)
            dupw_scr[rows, :] = dupw.astype(BF16)
            col_sum = col_sum + jnp.sum(dupw, axis=0, keepdims=True)
        vec_ref[0:1, :] += col_sum
        dpw_acc[...] += _dot(s_scr[...], dupw_scr[...], TN)
        upw_scr[...] = _dot(dupw_scr[...], pw_ref[...], NT)
        sum_g, sum_b, sum_c = col_sum * 0.0, col_sum * 0.0, col_sum * 0.0
        for rows in blocks:
            xh, rstd = xh_scr[rows, :], rstd_scr[rows, :]
            dln = upw_scr[rows, :] * _dsilu(ln_scr[rows, :], sg_scr[rows, :])
            sum_g = sum_g + jnp.sum(dln * xh, axis=0, keepdims=True)
            sum_b = sum_b + jnp.sum(dln, axis=0, keepdims=True)
            dxh = dln * lg_ref[...]
            dc = rstd * (dxh - jnp.mean(dxh, axis=-1, keepdims=True)
                         - xh * jnp.mean(dxh * xh, axis=-1, keepdims=True))
            sum_c = sum_c + jnp.sum(dc, axis=0, keepdims=True)
            dc_ext[rows, :] = dc
        vec_ref[1:2, :] += sum_g
        vec_ref[2:3, :] += sum_b
        vec_ref[3:4, :] += sum_c
        _fill_shifted(dc_ext, dc_sh)

        for l0 in range(0, CW, CONV_LANES):
            lanes = slice(l0, l0 + CONV_LANES)
            for r0 in range(0, tm, CONV_ROWS):
                acc = None
                for k in range(CONV_K):
                    term = _at_offset(dc_ext, dc_sh, CONV_K - 1 - k, r0, CONV_ROWS, lanes) * w_ref[k:k + 1, lanes]
                    acc = term if acc is None else acc + term
                du_scr[r0:r0 + CONV_ROWS, lanes] = acc

        n_grp = tm // 8
        by_shift = [[(k, (base + k) // 8) for k in range(CONV_K) if (base + k) % 8 == r] for r in range(8)]
        for l0 in range(0, CW, LANE):
            lane = slice(l0, l0 + LANE)
            for r in range(8):
                src = u_ext if r == 0 else u_sh.at[r - 1]
                a_lo, a_hi = by_shift[r][0][1], by_shift[r][-1][1]
                sums = {k: None for k, _ in by_shift[r]}
                dcg = {}
                for gi in range(a_lo, n_grp + a_hi):
                    if gi - a_lo < n_grp:
                        dcg[gi - a_lo] = dc_ext[8 * (gi - a_lo):8 * (gi - a_lo) + 8, lane]
                    dcg.pop(gi - a_hi - 1, None)
                    ug = src[8 * gi:8 * gi + 8, lane]
                    for k, a in by_shift[r]:
                        if 0 <= gi - a < n_grp:
                            prod = dcg[gi - a] * ug
                            sums[k] = prod if sums[k] is None else sums[k] + prod
                for k, _ in by_shift[r]:
                    dww_acc[k, :, lane] += sums[k]

        for rows in blocks:
            du = du_scr[rows, :]
            sgb = _sigmoid(b_ref[rows, :].astype(F32))
            dp_ref[rows, 0:CW] = (du * sgb).astype(BF16)
            dp_ref[rows, CW:2 * CW] = (du * a_ref[rows, :].astype(F32) * sgb * (1.0 - sgb)).astype(BF16)
        dc_ext[tm:tm + HALO, :] = dc_ext[0:HALO, :]

        @pl.when(i == NTL - 1)
        def _():
            for k in range(CONV_K):
                dww_ref[k:k + 1, :] = jnp.sum(dww_acc[k], axis=0, keepdims=True)
            dpw_ref[...] = dpw_acc[...].astype(BF16)

    row = lambda i: (0, 0)
    return pl.pallas_call(
        body, name="conv_bwd",
        grid=(NTL,),
        in_specs=[pl.BlockSpec((tm, CW), lambda i: (phys(i), cb)),
                  pl.BlockSpec((tm, CW), lambda i: (phys(i), cb + 1)),
                  pl.BlockSpec((tm, CW), lambda i: (phys(i), cb + 2)),
                  pl.BlockSpec((HALO, CW), lambda i: (halo(i), cb)),
                  pl.BlockSpec((HALO, CW), lambda i: (halo(i), cb + 1)),
                  pl.BlockSpec((tm, CW), lambda i: (phys(i), 0)),
                  pl.BlockSpec((tm, CW), lambda i: (phys(i), 1)),
                  pl.BlockSpec((HALO, CW), row),
                  pl.BlockSpec((1, CW), row), pl.BlockSpec((1, CW), row),
                  pl.BlockSpec((CW, CW), row),
                  pl.BlockSpec((1, CW), row)],
        out_specs=[pl.BlockSpec((tm, 3 * CW), lambda i: (phys(i), 0)),
                   pl.BlockSpec((CW, CW), row),
                   pl.BlockSpec((HALO, CW), row),
                   pl.BlockSpec((8, CW), row)],
        out_shape=[jax.ShapeDtypeStruct((R, 3 * CW), BF16), jax.ShapeDtypeStruct((CW, CW), BF16),
                   jax.ShapeDtypeStruct((HALO, CW), F32), jax.ShapeDtypeStruct((8, CW), F32)],
        scratch_shapes=[pltpu.VMEM((HALO + tm, CW), F32), pltpu.VMEM((7, tm + HALO - 8, CW), F32),
                        pltpu.VMEM((tm + HALO, CW), F32), pltpu.VMEM((7, tm + HALO - 8, CW), F32),
                        pltpu.VMEM((tm, CW), F32), pltpu.VMEM((CONV_K, 8, CW), F32), pltpu.VMEM((CW, CW), F32),
                        pltpu.VMEM((tm, CW), F32), pltpu.VMEM((tm, 1), F32), pltpu.VMEM((tm, CW), F32),
                        pltpu.VMEM((tm, CW), F32), pltpu.VMEM((tm, CW), F32), pltpu.VMEM((tm, CW), BF16),
                        pltpu.VMEM((tm, CW), BF16)],
        compiler_params=_params(("arbitrary",), 60),
    )(proj, proj, proj, proj, proj, conv_out, dy, dw_w, ln_g, ln_b, pw_w, pw_b)


def _retention_bwd(proj, rope, gn_g, states, dy, ex, ex_parts):
    R, E = proj.shape
    RW = gn_g.shape[1]
    H = RET_HEADS
    hd = RW // H
    half = hd // 2
    NC = R // CHUNK
    mask, qd, kd, cd = _decay_tables(H)
    scale = float(hd) ** -0.5
    order = _chunk_order(NC)
    phys = lambda i: order(NC - 1 - i)

    def body(*refs):
        p_ref, cb_ref, sb_ref, ci_ref, si_ref, mask_ref, qd_ref, kd_ref, gn_ref, st_ref, dy_ref = refs[:11]
        ex_ins = refs[11:11 + ex.n]
        dp_ref, dgn_ref = refs[11 + ex.n:13 + ex.n]
        ex_outs = refs[13 + ex.n:13 + 2 * ex.n]
        dstate = refs[13 + 2 * ex.n]
        sems = refs[14 + 2 * ex.n:]

        @pl.when(pl.program_id(0) == 0)
        def _():
            ex.start(ex_ins, ex_outs, sems)
            dstate[...] = jnp.zeros_like(dstate)
            dgn_ref[...] = jnp.zeros_like(dgn_ref)

        cs, sn = _rope_chunk(cb_ref, sb_ref, ci_ref, si_ref)
        hs = range(H)
        col = lambda j, h: slice(j * RW + h * hd, j * RW + (h + 1) * hd)
        qr = [_rot(p_ref[:, col(0, h)].astype(F32), cs, sn, half) for h in hs]
        kr = [_rot(p_ref[:, col(1, h)].astype(F32), cs, sn, half) * scale for h in hs]
        v = [p_ref[:, col(2, h)] for h in hs]
        qb = [qr[h].astype(BF16) for h in hs]
        kb = [kr[h].astype(BF16) for h in hs]
        qdb = [(qr[h] * qd_ref[h]).astype(BF16) for h in hs]
        kdb = [(kr[h] * kd_ref[h]).astype(BF16) for h in hs]
        s_prev = [st_ref[0, h] for h in hs]
        dst = [dstate[h] for h in hs]
        dstb = [dst[h].astype(BF16) for h in hs]
        sb = [(_dot(qb[h], kb[h], NT) * mask_ref[h]).astype(BF16) for h in hs]
        y_raw = [_dot(sb[h], v[h]) + _dot(qdb[h], s_prev[h]) for h in hs]
        dyrb, dg = [], []
        for h in hs:
            g = p_ref[:, col(3, h)].astype(F32)
            mu = jnp.mean(y_raw[h], axis=-1, keepdims=True)
            yc = y_raw[h] - mu
            rstd = lax.rsqrt(jnp.mean(yc * yc, axis=-1, keepdims=True) + EPS)
            xh = yc * rstd
            gn = gn_ref[:, col(0, h)]
            sg = _sigmoid(g)
            dyh = dy_ref[:, col(0, h)].astype(F32)
            dg.append((dyh * (xh * gn) * _dsilu(g, sg)).astype(BF16))
            dyn = dyh * (g * sg)
            dgn_ref[:, col(0, h)] += jnp.sum(dyn * xh, axis=0, keepdims=True)
            dxh = dyn * gn
            dyr = rstd * (dxh - jnp.mean(dxh, axis=-1, keepdims=True)
                          - xh * jnp.mean(dxh * xh, axis=-1, keepdims=True))
            dyrb.append(dyr.astype(BF16))
        dsb = [(_dot(dyrb[h], v[h], NT) * mask_ref[h]).astype(BF16) for h in hs]
        dqr = [_dot(dsb[h], kb[h]) + _dot(dyrb[h], s_prev[h], NT) * qd_ref[h] for h in hs]
        dkr = [_dot(dsb[h], qb[h], TN) + _dot(v[h], dstb[h], NT) * kd_ref[h] for h in hs]
        dv = [_dot(sb[h], dyrb[h], TN) + _dot(kdb[h], dstb[h]) for h in hs]
        dst_new = [dst[h] * cd[h] + _dot(qdb[h], dyrb[h], TN) for h in hs]
        for h in hs:
            dstate[h] = dst_new[h]
            dp_ref[:, col(0, h)] = _rot_inv(dqr[h], cs, sn, half).astype(BF16)
            dp_ref[:, col(1, h)] = (_rot_inv(dkr[h], cs, sn, half) * scale).astype(BF16)
            dp_ref[:, col(2, h)] = dv[h].astype(BF16)
            dp_ref[:, col(3, h)] = dg[h]

        @pl.when(pl.program_id(0) == NC - 1)
        def _():
            ex.wait(ex_ins, ex_outs, sems)

    const3 = lambda i: (0, 0, 0)
    outs = pl.pallas_call(
        body, name="retention_bwd",
        grid=(NC,),
        in_specs=[pl.BlockSpec((CHUNK, 4 * RW), lambda i: (phys(i), 0)),
                  pl.BlockSpec((1, 1, half), lambda i: (phys(i), 0, 0)),
                  pl.BlockSpec((1, 1, half), lambda i: (phys(i), 0, 0)),
                  pl.BlockSpec((CHUNK, half), lambda i: (0, 0)),
                  pl.BlockSpec((CHUNK, half), lambda i: (0, 0)),
                  pl.BlockSpec((H, CHUNK, CHUNK), const3),
                  pl.BlockSpec((H, CHUNK, 1), const3),
                  pl.BlockSpec((H, CHUNK, 1), const3),
                  pl.BlockSpec((1, RW), lambda i: (0, 0)),
                  pl.BlockSpec((1, H, hd, hd), lambda i: (phys(i), 0, 0, 0)),
                  pl.BlockSpec((CHUNK, RW), lambda i: (phys(i), 0))] + ex.specs,
        out_specs=[pl.BlockSpec((CHUNK, 4 * RW), lambda i: (phys(i), 0)),
                   pl.BlockSpec((1, RW), lambda i: (0, 0))] + ex.specs,
        out_shape=[jax.ShapeDtypeStruct((R, 4 * RW), BF16), jax.ShapeDtypeStruct((1, RW), F32)] + ex.out_shape,
        scratch_shapes=[pltpu.VMEM((H, hd, hd), F32)] + ex.scratch,
        compiler_params=_params(("arbitrary",), 32),
    )(proj, *rope, jnp.asarray(mask), jnp.asarray(qd), jnp.asarray(kd), gn_g, states, dy, *ex_parts)
    return outs[0], outs[1], outs[2:]


def _dproj_specs(tk, tn, n_ret, tile_axis, col_axis):
    def ret_map(*ids):
        t, j = ids[tile_axis], ids[col_axis]
        return (jnp.where(j < n_ret, t, 0), jnp.minimum(j, n_ret - 1))

    def conv_map(*ids):
        t, j = ids[tile_axis], ids[col_axis]
        return (jnp.where(j >= n_ret, t, 0), jnp.maximum(j - n_ret, 0))

    return pl.BlockSpec((tk, tn), ret_map), pl.BlockSpec((tk, tn), conv_map)


def _w_in_grad(hn, dp_ret, dp_conv):
    R, D = hn.shape
    tn = _pick_tile(dp_conv.shape[1] // 3, 1024, 128)
    n_ret, n_conv = dp_ret.shape[1] // tn, dp_conv.shape[1] // tn
    E = dp_ret.shape[1] + dp_conv.shape[1]
    tk = _pick_tile(R, 1024, MXU_DIM)
    n_t = R // tk
    ret_spec, conv_spec = _dproj_specs(tk, tn, n_ret, 1, 0)

    def body(hn_ref, r_ref, c_ref, out_ref, acc):
        j, t = pl.program_id(0), pl.program_id(1)

        @pl.when(t == 0)
        def _():
            acc[...] = jnp.zeros_like(acc)

        @pl.when(j < n_ret)
        def _():
            acc[...] += _dot(hn_ref[...], r_ref[...], TN)

        @pl.when(j >= n_ret)
        def _():
            acc[...] += _dot(hn_ref[...], c_ref[...], TN)

        @pl.when(t == n_t - 1)
        def _():
            out_ref[...] = acc[...].astype(BF16)

    return pl.pallas_call(
        body, name="w_in_grad",
        grid=(n_ret + n_conv, n_t),
        in_specs=[pl.BlockSpec((tk, D), lambda j, t: (t, 0)), ret_spec, conv_spec],
        out_specs=pl.BlockSpec((D, tn), lambda j, t: (0, j)),
        out_shape=jax.ShapeDtypeStruct((D, E), BF16),
        scratch_shapes=[pltpu.VMEM((D, tn), F32)],
        compiler_params=_params(("arbitrary", "arbitrary"), 48),
    )(hn, dp_ret, dp_conv)


def _h_grad(name, dp_ret, dp_conv, w_in, h, dh2, ln_g, row0, dlg_init=None, ex=None, ex_parts=()):
    n, D = h.shape
    tn = _pick_tile(dp_conv.shape[1] // 3, 1024, 128)
    n_ret, n_conv = dp_ret.shape[1] // tn, dp_conv.shape[1] // tn
    n_k = n_ret + n_conv
    tm = _pick_tile(n, 1024)
    te = _pick_tile(tm, 256)
    n_e = tm // te
    assert row0 % tm == 0 and n_e <= n_k
    b0, e0 = row0 // tm, row0 // te
    n_ex = 0 if ex is None else ex.n
    n_t = n // tm

    def ret_map(t, k):
        return (jnp.where(k < n_ret, b0 + jnp.minimum(t, n_t - 1), b0), jnp.minimum(k, n_ret - 1))

    def conv_map(t, k):
        return (jnp.where(k >= n_ret, b0 + jnp.minimum(t, n_t - 1), b0), jnp.maximum(k - n_ret, 0))

    def rows_e(first):
        return lambda t, k: (first + jnp.maximum(t - 1, 0) * n_e + jnp.where(t > 0, jnp.minimum(k, n_e - 1), 0), 0)

    def body(*refs):
        r_ref, c_ref, w_ref, h_ref, dh2_ref, g_ref, init_ref = refs[:7]
        ex_ins = refs[7:7 + n_ex]
        o = 7 + n_ex
        dh_ref, dlg_ref = refs[o:o + 2]
        ex_outs = refs[o + 2:o + 2 + n_ex]
        acc = refs[o + 2 + n_ex]
        sems = refs[o + 3 + n_ex:]
        t, k = pl.program_id(0), pl.program_id(1)
        cur, old = t % 2, (t + 1) % 2

        @pl.when((k == 0) & (t == 0))
        def _():
            if ex is not None:
                ex.start(ex_ins, ex_outs, sems)
            dlg_ref[...] = init_ref[...]

        @pl.when((k < n_ret) & (t < n_t))
        def _():
            part = _dot(r_ref[...], w_ref[...], NT)

            @pl.when(k == 0)
            def _():
                acc[cur] = part

            @pl.when(k > 0)
            def _():
                acc[cur] += part

        @pl.when((k >= n_ret) & (t < n_t))
        def _():
            acc[cur] += _dot(c_ref[...], w_ref[...], NT)

        @pl.when((k < n_e) & (t > 0))
        def _():
            hv = h_ref[...]
            r = lax.rsqrt(jnp.mean(hv * hv, axis=-1, keepdims=True) + EPS)
            nrm = hv * r
            dhn = acc[old, pl.ds(pl.multiple_of(k * te, te), te), :]
            dlg_ref[...] += jnp.sum(dhn * nrm, axis=0, keepdims=True)
            dn = dhn * g_ref[...]
            dh_ref[...] = dh2_ref[...] + r * (dn - nrm * jnp.mean(dn * nrm, axis=-1, keepdims=True))

        if ex is not None:
            @pl.when((k == n_k - 1) & (t == n_t))
            def _():
                ex.wait(ex_ins, ex_outs, sems)

    row = lambda t, k: (0, 0)
    if dlg_init is None:
        dlg_init = jnp.zeros((1, D), F32)
    ex_specs, ex_shape, ex_scratch = ([], [], []) if ex is None else (ex.specs, ex.out_shape, ex.scratch)
    outs = pl.pallas_call(
        body, name=name,
        grid=(n_t + 1, n_k),
        in_specs=[pl.BlockSpec((tm, tn), ret_map), pl.BlockSpec((tm, tn), conv_map),
                  pl.BlockSpec((D, tn), lambda t, k: (0, k)),
                  pl.BlockSpec((te, D), rows_e(0)),
                  pl.BlockSpec((te, D), rows_e(e0)),
                  pl.BlockSpec((1, D), row), pl.BlockSpec((1, D), row)] + ex_specs,
        out_specs=[pl.BlockSpec((te, D), rows_e(0)),
                   pl.BlockSpec((1, D), row)] + ex_specs,
        out_shape=[jax.ShapeDtypeStruct((n, D), F32), jax.ShapeDtypeStruct((1, D), F32)] + ex_shape,
        scratch_shapes=[pltpu.VMEM((2, tm, D), F32)] + ex_scratch,
        compiler_params=_params(("arbitrary", "arbitrary"), 60),
    )(dp_ret, dp_conv, w_in, h, dh2, ln_g, dlg_init, *ex_parts)
    return outs[0], outs[1], outs[2:]


def _adamw(w, g, m, v):
    m = ADAM_B1 * m + (1.0 - ADAM_B1) * g
    v = ADAM_B2 * v + (1.0 - ADAM_B2) * (g * g)
    m_hat = m / (1.0 - ADAM_B1 ** ADAM_STEP)
    v_hat = v / (1.0 - ADAM_B2 ** ADAM_STEP)
    delta = -ADAM_LR * (m_hat / (jnp.sqrt(v_hat) + ADAM_EPS) + ADAM_WD * w)
    return delta, m, v


def _sum_slots(ref):
    g = ref[0].astype(F32)
    for s in range(1, N_DEV):
        g = g + ref[s].astype(F32)
    return g


def _sum_adamw(name, parts, w, m, v, rows_target, ex=None, ex_parts=()):
    rows, cols = w.shape
    tr = _pick_tile(rows, rows_target, 8)
    n_ex = 0 if ex is None else ex.n
    n_steps = rows // tr

    def body(*refs):
        p_ref, w_ref, m_ref, v_ref = refs[:4]
        ex_ins = refs[4:4 + n_ex]
        o = 4 + n_ex
        g_ref, d_ref, nm_ref, nv_ref = refs[o:o + 4]
        ex_outs, sems = refs[o + 4:o + 4 + n_ex], refs[o + 4 + n_ex:]
        if ex is not None:
            @pl.when(pl.program_id(0) == 0)
            def _():
                ex.start(ex_ins, ex_outs, sems)

        g = _sum_slots(p_ref)
        d, nm, nv = _adamw(w_ref[...], g, m_ref[...], v_ref[...])
        g_ref[...] = g
        d_ref[...] = d
        nm_ref[...] = nm
        nv_ref[...] = nv
        if ex is not None:
            @pl.when(pl.program_id(0) == n_steps - 1)
            def _():
                ex.wait(ex_ins, ex_outs, sems)

    tile = pl.BlockSpec((tr, cols), lambda i: (i, 0))
    ex_specs, ex_shape, ex_scratch = ([], [], []) if ex is None else (ex.specs, ex.out_shape, ex.scratch)
    outs = pl.pallas_call(
        body, name=name,
        grid=(n_steps,),
        in_specs=[pl.BlockSpec((N_DEV, tr, cols), lambda i: (0, i, 0)), tile, tile, tile] + ex_specs,
        out_specs=[tile] * 4 + ex_specs,
        out_shape=[jax.ShapeDtypeStruct((rows, cols), F32)] * 4 + ex_shape,
        scratch_shapes=ex_scratch,
        compiler_params=_params(("arbitrary",), 40),
    )(parts, w, m, v, *ex_parts)
    return outs[:4], outs[4:]


def _sum_adamw_small(parts_list, w_list, m_list, v_list, loss_parts):
    n = len(w_list)

    def body(*refs):
        p_refs, w_refs, m_refs, v_refs = refs[:n], refs[n:2 * n], refs[2 * n:3 * n], refs[3 * n:4 * n]
        lp_ref = refs[4 * n]
        outs = refs[4 * n + 1:]
        for a in range(n):
            g = _sum_slots(p_refs[a])
            d, nm, nv = _adamw(w_refs[a][...], g, m_refs[a][...], v_refs[a][...])
            outs[4 * a][...] = g
            outs[4 * a + 1][...] = d
            outs[4 * a + 2][...] = nm
            outs[4 * a + 3][...] = nv
        outs[4 * n][...] = _sum_slots(lp_ref)

    out_shape = []
    for w in w_list:
        out_shape += [jax.ShapeDtypeStruct(w.shape, F32)] * 4
    out_shape.append(jax.ShapeDtypeStruct(loss_parts.shape[1:], F32))
    return pl.pallas_call(body, name="sum_adamw_small", out_shape=out_shape)(
        *parts_list, *w_list, *m_list, *v_list, loss_parts)


def kernel(x, meta_tokens, ln_g, w_in, ret_gn_g, conv_dw_w, conv_dw_b, conv_ln_g, conv_ln_b, conv_pw_w, conv_pw_b, w_out, final_g, loss_target, m_meta_tokens, m_ln_g, m_w_in, m_ret_gn_g, m_conv_dw_w, m_conv_dw_b, m_conv_ln_g, m_conv_ln_b, m_conv_pw_w, m_conv_pw_b, m_w_out, m_final_g, v_meta_tokens, v_ln_g, v_w_in, v_ret_gn_g, v_conv_dw_w, v_conv_dw_b, v_conv_ln_g, v_conv_ln_b, v_conv_pw_w, v_conv_pw_b, v_w_out, v_final_g):
    _, SEQ, D = x.shape
    MIX = w_out.shape[2]
    RW = ret_gn_g.shape[1]
    CW = conv_pw_b.shape[1]
    assert RW == CW and MIX == RW + CW and SEQ % META_TILE == 0 and CONV_K - 1 <= HALO
    R = SEQ + META_TILE
    hd = RW // RET_HEADS
    half = hd // 2
    me = 4 * lax.axis_index("x") + 2 * lax.axis_index("y") + lax.axis_index("c")

    dw_pad = jnp.pad(conv_dw_w[0], ((0, HALO - CONV_K), (0, 0)))
    dw_g, meta_g = _gather_weights([dw_pad, meta_tokens], [1, 1])

    n_seq_chunks = SEQ // CHUNK
    base = jnp.concatenate([jnp.arange(n_seq_chunks, dtype=F32) * CHUNK + N_META,
                            jnp.zeros((META_TILE // CHUNK - 1,), F32), jnp.full((1,), N_META - CHUNK, F32)])
    inv_freq = ROPE_BASE ** (-jnp.arange(half, dtype=F32) / half)
    ang_base = (base[:, None] * inv_freq[None, :])[:, None, :]
    ang_row = jnp.arange(CHUNK, dtype=F32)[:, None] * inv_freq[None, :]
    rope = (jnp.cos(ang_base), jnp.sin(ang_base), jnp.cos(ang_row), jnp.sin(ang_row))

    xs = x[0]
    meta_tile = jnp.concatenate([jnp.zeros((META_TILE - N_META, D), F32), meta_g], axis=0)
    target = loss_target[0]
    final_g2 = final_g[None, :]

    hn = _rms_norm(xs, meta_tile, ln_g)
    pw_shard, w_out_shard = [conv_pw_w[0].astype(BF16)], [w_out[0].astype(BF16)]
    proj, w_in_g, (pw_g,) = _in_proj_gather(hn, w_in[0].astype(BF16), _Exchange(pw_shard, [None]), pw_shard)
    pw_g = pw_g.reshape(CW, CW)
    y, states = _retention_fwd(proj, rope, ret_gn_g, MIX)
    y, conv_out, (w_out_g,) = _conv_fwd(proj, y, dw_g, conv_dw_b, conv_ln_g, conv_ln_b, pw_g, conv_pw_b,
                                        _Exchange(w_out_shard, [None]), w_out_shard)
    w_out_g = w_out_g.reshape(MIX, D)
    dh2, dy, dwo_p, dfg_p, loss_p = _out_proj_loss(xs, meta_tile, y, w_out_g, final_g2, target)

    dp_conv, dpw_p, dww_p, cvec_p = _conv_bwd(proj, conv_out, dy, dw_g, conv_ln_g, conv_ln_b, pw_g, conv_pw_b)
    dp_ret, dgn_p, (r_wo, r_pw) = _retention_bwd(proj, rope, ret_gn_g, states, dy,
                                                 _Exchange([dwo_p, dpw_p], [0, 0]), [dwo_p, dpw_p])
    dwi_p = _w_in_grad(hn, dp_ret, dp_conv)
    grad_xs, dlg_x, (r_wi,) = _h_grad("h_grad", dp_ret, dp_conv, w_in_g, xs, dh2, ln_g, 0,
                                      ex=_Exchange([dwi_p], [1]), ex_parts=[dwi_p])
    dh_meta, dlg_p, _ = _h_grad("h_grad_meta", dp_ret, dp_conv, w_in_g, meta_tile, dh2, ln_g, SEQ, dlg_init=dlg_x)
    grad_x = grad_xs[None]

    def at_row(r, a, b=None):
        v = a if b is None else jnp.concatenate([a, b], axis=1)
        return jnp.pad(v, ((r, 7 - r), (0, D - v.shape[1])))
    vec8 = (at_row(0, dlg_p) + at_row(1, dfg_p)
            + at_row(2, dgn_p, cvec_p[3:4])
            + at_row(3, cvec_p[1:2], cvec_p[2:3])
            + at_row(4, cvec_p[0:1], jnp.broadcast_to(loss_p, (1, CW))))
    small = jnp.concatenate([dh_meta[META_TILE - N_META:], vec8,
                             jnp.zeros((SMALL_ROWS - N_META - 8, D), F32)], axis=0)

    (g_wi, d_wi, nm_wi, nv_wi), (r_dww, r_small) = _sum_adamw(
        "sum_adamw_w_in", r_wi, w_in[0], m_w_in[0], v_w_in[0], 256,
        ex=_Exchange([dww_p, small], [1, None]), ex_parts=[dww_p, small])
    (g_wo, d_wo, nm_wo, nv_wo), _ = _sum_adamw("sum_adamw_w_out", r_wo, w_out[0], m_w_out[0], v_w_out[0], 128)
    (g_pw, d_pw, nm_pw, nv_pw), _ = _sum_adamw("sum_adamw_pw", r_pw, conv_pw_w[0], m_conv_pw_w[0], v_conv_pw_w[0], 128)

    dcol = D // N_DEV
    sm = lambda r0, nr, c0, nc: lax.slice(r_small, (0, r0, c0), (N_DEV, r0 + nr, c0 + nc))
    meta_parts = lax.dynamic_slice(r_small, (0, 0, me * dcol), (N_DEV, N_META, dcol))
    small_parts = [meta_parts, sm(16, 1, 0, D), sm(18, 1, 0, RW), r_dww, sm(18, 1, RW, CW),
                   sm(19, 1, 0, CW), sm(19, 1, CW, CW), sm(20, 1, 0, CW), sm(17, 1, 0, D)]
    pad31 = lambda a: jnp.pad(a, ((0, HALO - CONV_K), (0, 0)))
    ws = [meta_tokens, ln_g, ret_gn_g, pad31(conv_dw_w[0]), conv_dw_b, conv_ln_g, conv_ln_b, conv_pw_b, final_g2]
    ms = [m_meta_tokens, m_ln_g, m_ret_gn_g, pad31(m_conv_dw_w[0]), m_conv_dw_b, m_conv_ln_g, m_conv_ln_b,
          m_conv_pw_b, m_final_g[None, :]]
    vs = [v_meta_tokens, v_ln_g, v_ret_gn_g, pad31(v_conv_dw_w[0]), v_conv_dw_b, v_conv_ln_g, v_conv_ln_b,
          v_conv_pw_b, v_final_g[None, :]]
    loss_parts = sm(20, 1, CW, 1)
    outs = _sum_adamw_small(small_parts, ws, ms, vs, loss_parts)
    loss = outs[-1][0, 0]
    quad = [outs[4 * a:4 * a + 4] for a in range(len(ws))]
    (q_meta, q_lng, q_gn, q_dww, q_dwb, q_clg, q_clb, q_pwb, q_fg) = quad
    q_dww = [t[:CONV_K][None] for t in q_dww]
    q_fg = [t[0] for t in q_fg]
    q_wi = [t[None] for t in (g_wi, d_wi, nm_wi, nv_wi)]
    q_wo = [t[None] for t in (g_wo, d_wo, nm_wo, nv_wo)]
    q_pw = [t[None] for t in (g_pw, d_pw, nm_pw, nv_pw)]

    per_w = [q_meta, q_lng, q_wi, q_gn, q_dww, q_dwb, q_clg, q_clb, q_pw, q_pwb, q_wo, q_fg]
    result = [loss, grad_x]
    for which in range(4):
        result += [q[which] for q in per_w]
    return tuple(result)
```

```python
import functools

import numpy as np
import jax
import jax.numpy as jnp
from jax import lax
from jax.experimental import pallas as pl
from jax.experimental.pallas import tpu as pltpu

N_META = 16
RET_HEADS = 4
CONV_K = 31
CHUNK = 128
ROPE_BASE = 10000.0
EPS = 1e-6
ADAM_LR = 0.001
ADAM_B1 = 0.9
ADAM_B2 = 0.999
ADAM_EPS = 1e-08
ADAM_WD = 0.01
ADAM_STEP = 10

N_DEV = 8
META_TILE = 256
HALO = 32
SMALL_ROWS = 32
VMEM_BYTES_V7X = 64 * 1024 * 1024
MXU_DIM = 256

F32 = jnp.float32
BF16 = jnp.bfloat16
MESH = pl.DeviceIdType.MESH

NN = (((1,), (0,)), ((), ()))
NT = (((1,), (1,)), ((), ()))
TN = (((0,), (0,)), ((), ()))


def _dot(a, b, dims=NN):
    return lax.dot_general(a, b, dims, preferred_element_type=F32)


def _pick_tile(n, target, mult=16):
    best = None
    for t in range(mult, min(n, target) + 1, mult):
        if n % t == 0:
            best = t
    assert best is not None, (n, target)
    return best


def _params(sem=None, vmem_mb=None):
    kw = {}
    if sem is not None:
        kw["dimension_semantics"] = sem
    if vmem_mb is not None:
        kw["vmem_limit_bytes"] = min(vmem_mb * 1024 * 1024, VMEM_BYTES_V7X - 4 * 1024 * 1024)
    return pltpu.CompilerParams(**kw)


def _sigmoid(x):
    return jax.nn.sigmoid(x)


def _dsilu(x, sg):
    return sg * (1.0 + x * (1.0 - sg))


def _decay_tables(heads):
    h = np.arange(heads, dtype=np.float32)
    gamma = (1.0 - np.exp2(-5.0 - h)).astype(np.float32)
    log_g = np.log(gamma).astype(np.float32)
    idx = np.arange(CHUNK, dtype=np.float32)
    rel = idx[:, None] - idx[None, :]
    mask = np.where(rel[None] >= 0, np.exp(np.maximum(rel, 0.0)[None] * log_g[:, None, None]), 0.0)
    qd = np.exp((idx[None, :] + 1.0) * log_g[:, None])
    kd = np.exp((CHUNK - 1.0 - idx[None, :]) * log_g[:, None])
    cd = np.exp(CHUNK * log_g)
    return (mask.astype(np.float32), qd.astype(np.float32)[:, :, None], kd.astype(np.float32)[:, :, None],
            [float(c) for c in cd.astype(np.float32)])


def _gather_weights(shards, block_axes):
    n_arr = len(shards)
    out_shapes = []
    for s, ax in zip(shards, block_axes):
        shp = list(s.shape)
        shp[ax] *= N_DEV
        out_shapes.append(jax.ShapeDtypeStruct(tuple(shp), s.dtype))

    def body(*refs):
        ins, outs = refs[:n_arr], refs[n_arr:2 * n_arr]
        send_sems, recv_sems, local_sems = refs[2 * n_arr:]
        x, y, c = lax.axis_index("x"), lax.axis_index("y"), lax.axis_index("c")
        me, sibling = (x, y, c), (x, y, 1 - c)
        chips = [(1 - x, y), (x, 1 - y), (1 - x, 1 - y)]

        def block(a, dev):
            n = ins[a].shape[block_axes[a]]
            start = pl.multiple_of((4 * dev[0] + 2 * dev[1] + dev[2]) * n, n)
            idx = [slice(None)] * len(ins[a].shape)
            idx[block_axes[a]] = pl.ds(start, n)
            return outs[a].at[tuple(idx)]

        def copy(a, k, dev, to, src=None):
            return pltpu.make_async_remote_copy(
                src_ref=block(a, dev) if src is None else src, dst_ref=block(a, dev),
                send_sem=send_sems.at[a, k], recv_sem=recv_sems.at[a, k],
                device_id=to, device_id_type=MESH)

        mine = [pltpu.make_async_copy(ins[a], block(a, me), local_sems.at[a]) for a in range(n_arr)]
        for cp in mine:
            cp.start()
        first = []
        for a in range(n_arr):
            first.append(copy(a, 0, me, sibling, src=ins[a]))
            first += [copy(a, 1 + j, me, (*chip, c), src=ins[a]) for j, chip in enumerate(chips)]
        for cp in first:
            cp.start()
        passed = []
        for j, chip in enumerate(chips):
            for a in range(n_arr):
                copy(a, 1 + j, (*chip, c), me).wait_recv()
                fwd = copy(a, 4 + j, (*chip, c), sibling)
                fwd.start()
                passed.append(fwd)
        for a in range(n_arr):
            copy(a, 0, sibling, me).wait_recv()
            for j, chip in enumerate(chips):
                copy(a, 4 + j, (*chip, 1 - c), me).wait_recv()
        for cp in first + passed:
            cp.wait_send()
        for cp in mine:
            cp.wait()

    hbm = pl.BlockSpec(memory_space=pl.ANY)
    return pl.pallas_call(
        body, name="gather_weights",
        out_shape=out_shapes,
        in_specs=[hbm] * n_arr, out_specs=[hbm] * n_arr,
        scratch_shapes=[pltpu.SemaphoreType.DMA((n_arr, 7)), pltpu.SemaphoreType.DMA((n_arr, 7)),
                        pltpu.SemaphoreType.DMA((n_arr,))],
    )(*shards)


class _Exchange:
    def __init__(self, parts, block_axes):
        self.block_axes = list(block_axes)
        self.n = len(parts)
        self.out_shape = []
        for p, ax in zip(parts, block_axes):
            shp = list(p.shape)
            if ax is not None:
                assert shp[ax] % N_DEV == 0
                shp[ax] //= N_DEV
            self.out_shape.append(jax.ShapeDtypeStruct((N_DEV, *shp), p.dtype))
        self.scratch = [pltpu.SemaphoreType.DMA((self.n, N_DEV - 1)), pltpu.SemaphoreType.DMA((self.n, N_DEV - 1)),
                        pltpu.SemaphoreType.DMA((self.n,))]
        self.specs = [pl.BlockSpec(memory_space=pl.ANY)] * self.n

    def _copies(self, ins, outs, sems):
        send_sems, recv_sems, local_sems = sems
        x, y, c = lax.axis_index("x"), lax.axis_index("y"), lax.axis_index("c")
        me_idx = 4 * x + 2 * y + c

        def src_block(a, dev_idx):
            ax = self.block_axes[a]
            if ax is None:
                return ins[a]
            n = ins[a].shape[ax] // N_DEV
            idx = [slice(None)] * len(ins[a].shape)
            idx[ax] = pl.ds(pl.multiple_of(dev_idx * n, n), n)
            return ins[a].at[tuple(idx)]

        local = [pltpu.make_async_copy(src_block(a, me_idx), outs[a].at[me_idx], local_sems.at[a])
                 for a in range(self.n)]
        remote = []
        for m in range(1, N_DEV):
            px, py, pc = x ^ ((m >> 2) & 1), y ^ ((m >> 1) & 1), c ^ (m & 1)
            for a in range(self.n):
                remote.append(pltpu.make_async_remote_copy(
                    src_ref=src_block(a, 4 * px + 2 * py + pc), dst_ref=outs[a].at[me_idx],
                    send_sem=send_sems.at[a, m - 1], recv_sem=recv_sems.at[a, m - 1],
                    device_id=(px, py, pc), device_id_type=MESH))
        return local, remote

    def start(self, ins, outs, sems):
        local, remote = self._copies(ins, outs, sems)
        for cp in local + remote:
            cp.start()

    def wait(self, ins, outs, sems):
        local, remote = self._copies(ins, outs, sems)
        for cp in remote:
            cp.wait_recv()
        for cp in remote:
            cp.wait_send()
        for cp in local:
            cp.wait()


def _rms_norm(xs, meta_tile, ln_g):
    SEQ, D = xs.shape
    tm = meta_tile.shape[0]
    n_seq = SEQ // tm

    def body(x_ref, mt_ref, g_ref, hn_ref):
        hv = jnp.where(pl.program_id(0) < n_seq, x_ref[...], mt_ref[...])
        r = lax.rsqrt(jnp.mean(hv * hv, axis=-1, keepdims=True) + EPS)
        hn_ref[...] = (hv * r * g_ref[...]).astype(BF16)

    return pl.pallas_call(
        body, name="rms_norm",
        grid=(n_seq + 1,),
        in_specs=[pl.BlockSpec((tm, D), lambda i: (jnp.minimum(i, n_seq - 1), 0)),
                  pl.BlockSpec((tm, D), lambda i: (0, 0)),
                  pl.BlockSpec((1, D), lambda i: (0, 0))],
        out_specs=pl.BlockSpec((tm, D), lambda i: (i, 0)),
        out_shape=jax.ShapeDtypeStruct((SEQ + tm, D), BF16),
        compiler_params=_params(("arbitrary",), 32),
    )(xs, meta_tile, ln_g)


def _chip_visited(q):
    mine = 2 * lax.axis_index("x") + lax.axis_index("y")
    return mine ^ (((q & 1) << 1) | (q >> 1))


def _in_proj_gather(hn, w_shard, ex, ex_parts):
    R, D = hn.shape
    wb = w_shard.shape[1]
    E, tn = wb * N_DEV, 2 * wb
    n_q = N_DEV // 2
    tm = _pick_tile(R, min(768, R // 2), MXU_DIM)
    n_i = R // tm

    def body(*refs):
        hn_ref, wsh_hbm = refs[:2]
        ex_ins = refs[2:2 + ex.n]
        proj_ref, wg_hbm = refs[2 + ex.n:4 + ex.n]
        ex_outs = refs[4 + ex.n:4 + 2 * ex.n]
        w_vmem, send_sems, recv_sems, local_sem, vmem_sems = refs[4 + 2 * ex.n:9 + 2 * ex.n]
        ex_sems = refs[9 + 2 * ex.n:]
        q, i = pl.program_id(0), pl.program_id(1)
        x, y, c = lax.axis_index("x"), lax.axis_index("y"), lax.axis_index("c")
        me, sibling = (x, y, c), (x, y, 1 - c)
        chips = [(1 - x, y), (x, 1 - y), (1 - x, 1 - y)]

        def block(dev):
            return wg_hbm.at[:, pl.ds(pl.multiple_of((4 * dev[0] + 2 * dev[1] + dev[2]) * wb, wb), wb)]

        def copy(k, dev, to, src=None):
            return pltpu.make_async_remote_copy(
                src_ref=block(dev) if src is None else src, dst_ref=block(dev),
                send_sem=send_sems.at[k], recv_sem=recv_sems.at[k], device_id=to, device_id_type=MESH)

        def to_vmem(p):
            cols = pl.ds(pl.multiple_of(_chip_visited(p) * tn, tn), tn)
            return pltpu.make_async_copy(wg_hbm.at[:, cols], w_vmem.at[p % 2], vmem_sems.at[p % 2])

        mine = pltpu.make_async_copy(wsh_hbm, block(me), local_sem)
        first = [copy(0, me, sibling, src=wsh_hbm)] + [copy(1 + j, me, (*chip, c), src=wsh_hbm)
                                                       for j, chip in enumerate(chips)]
        passed = [copy(4 + j, (*chip, c), sibling) for j, chip in enumerate(chips)]

        @pl.when((q == 0) & (i == 0))
        def _():
            mine.start()
            for cp in first:
                cp.start()
            ex.start(ex_ins, ex_outs, ex_sems)
            mine.wait()
            copy(0, sibling, me).wait_recv()
            to_vmem(0).start()
            to_vmem(0).wait()

        for p in range(1, n_q):
            chip = chips[p - 1]

            @pl.when((q == p - 1) & (i == n_i - 2))
            def _():
                copy(p, (*chip, c), me).wait_recv()
                passed[p - 1].start()

            @pl.when((q == p - 1) & (i == n_i - 1))
            def _():
                copy(3 + p, (*chip, 1 - c), me).wait_recv()
                to_vmem(p).start()

            @pl.when((q == p) & (i == 0))
            def _():
                to_vmem(p).wait()

        proj_ref[...] = _dot(hn_ref[...], w_vmem[q % 2]).astype(BF16)

        @pl.when((q == n_q - 1) & (i == n_i - 1))
        def _():
            for cp in first + passed:
                cp.wait_send()
            ex.wait(ex_ins, ex_outs, ex_sems)

    hbm = pl.BlockSpec(memory_space=pl.ANY)
    outs = pl.pallas_call(
        body, name="in_proj",
        grid=(n_q, n_i),
        in_specs=[pl.BlockSpec((tm, D), lambda q, i: (i, 0)), hbm] + ex.specs,
        out_specs=[pl.BlockSpec((tm, tn), lambda q, i: (i, _chip_visited(q))), hbm] + ex.specs,
        out_shape=[jax.ShapeDtypeStruct((R, E), BF16), jax.ShapeDtypeStruct((D, E), BF16)] + ex.out_shape,
        scratch_shapes=[pltpu.VMEM((2, D, tn), BF16), pltpu.SemaphoreType.DMA((7,)), pltpu.SemaphoreType.DMA((7,)),
                        pltpu.SemaphoreType.DMA, pltpu.SemaphoreType.DMA((2,))] + ex.scratch,
        compiler_params=_params(("arbitrary", "arbitrary"), 48),
    )(hn, w_shard, *ex_parts)
    return outs[0], outs[1], outs[2:]


def _rope_chunk(cb_ref, sb_ref, ci_ref, si_ref):
    cb, sb, ci, si = cb_ref[0], sb_ref[0], ci_ref[...], si_ref[...]
    return cb * ci - sb * si, sb * ci + cb * si


def _rot(t, cos, sin, half):
    t1, t2 = t[:, :half], t[:, half:]
    return jnp.concatenate([t1 * cos - t2 * sin, t1 * sin + t2 * cos], axis=-1)


def _rot_inv(t, cos, sin, half):
    t1, t2 = t[:, :half], t[:, half:]
    return jnp.concatenate([t1 * cos + t2 * sin, t2 * cos - t1 * sin], axis=-1)


def _chunk_order(n_chunks):
    lead = META_TILE // CHUNK
    return lambda l: (l + n_chunks - lead) % n_chunks


def _retention_fwd(proj, rope, gn_g, mix):
    R, E = proj.shape
    RW = gn_g.shape[1]
    H = RET_HEADS
    hd = RW // H
    half = hd // 2
    NC = R // CHUNK
    mask, qd, kd, cd = _decay_tables(H)
    scale = float(hd) ** -0.5
    phys = _chunk_order(NC)

    def body(p_ref, cb_ref, sb_ref, ci_ref, si_ref, mask_ref, qd_ref, kd_ref, gn_ref, y_ref, st_ref, state):
        @pl.when(pl.program_id(0) == 0)
        def _():
            state[...] = jnp.zeros_like(state)

        cs, sn = _rope_chunk(cb_ref, sb_ref, ci_ref, si_ref)
        hs = range(H)
        col = lambda j, h: slice(j * RW + h * hd, j * RW + (h + 1) * hd)
        qr = [_rot(p_ref[:, col(0, h)].astype(F32), cs, sn, half) for h in hs]
        kr = [_rot(p_ref[:, col(1, h)].astype(F32), cs, sn, half) * scale for h in hs]
        v = [p_ref[:, col(2, h)] for h in hs]
        s_prev = [state[h] for h in hs]
        s_prev_b = [s_prev[h].astype(BF16) for h in hs]
        s = [(_dot(qr[h].astype(BF16), kr[h].astype(BF16), NT) * mask_ref[h]).astype(BF16) for h in hs]
        y_raw = [_dot(s[h], v[h]) + _dot((qr[h] * qd_ref[h]).astype(BF16), s_prev_b[h]) for h in hs]
        s_new = [s_prev[h] * cd[h] + _dot((kr[h] * kd_ref[h]).astype(BF16), v[h], TN) for h in hs]
        for h in hs:
            st_ref[0, h] = s_prev_b[h]
            state[h] = s_new[h]
        for h in hs:
            g = p_ref[:, col(3, h)].astype(F32)
            mu = jnp.mean(y_raw[h], axis=-1, keepdims=True)
            yc = y_raw[h] - mu
            var = jnp.mean(yc * yc, axis=-1, keepdims=True)
            out = yc * lax.rsqrt(var + EPS) * gn_ref[:, col(0, h)] * (g * _sigmoid(g))
            y_ref[:, col(0, h)] = out.astype(BF16)

    const3 = lambda l: (0, 0, 0)
    return pl.pallas_call(
        body, name="retention_fwd",
        grid=(NC,),
        in_specs=[pl.BlockSpec((CHUNK, 4 * RW), lambda l: (phys(l), 0)),
                  pl.BlockSpec((1, 1, half), lambda l: (phys(l), 0, 0)),
                  pl.BlockSpec((1, 1, half), lambda l: (phys(l), 0, 0)),
                  pl.BlockSpec((CHUNK, half), lambda l: (0, 0)),
                  pl.BlockSpec((CHUNK, half), lambda l: (0, 0)),
                  pl.BlockSpec((H, CHUNK, CHUNK), const3),
                  pl.BlockSpec((H, CHUNK, 1), const3),
                  pl.BlockSpec((H, CHUNK, 1), const3),
                  pl.BlockSpec((1, RW), lambda l: (0, 0))],
        out_specs=[pl.BlockSpec((CHUNK, RW), lambda l: (phys(l), 0)),
                   pl.BlockSpec((1, H, hd, hd), lambda l: (phys(l), 0, 0, 0))],
        out_shape=[jax.ShapeDtypeStruct((R, mix), BF16), jax.ShapeDtypeStruct((NC, H, hd, hd), BF16)],
        scratch_shapes=[pltpu.VMEM((H, hd, hd), F32)],
        compiler_params=_params(("arbitrary",), 32),
    )(proj, *rope, jnp.asarray(mask), jnp.asarray(qd), jnp.asarray(kd), gn_g)


CONV_ROWS = 64
CONV_LANES = 128
LANE = 128
ELEM_ROWS = 32


def _conv_order(n_tiles):
    return lambda l: (l + n_tiles - 1) % n_tiles


def _halo_block(n_tiles, tm):
    per = tm // HALO
    return lambda l: ((l + n_tiles - 2) % n_tiles) * per + per - 1


def _fill_shifted(src, dst):
    rows, width = dst.shape[1], dst.shape[2]
    step = _pick_tile(rows, 64, 8)
    for r in range(1, 8):
        for r0 in range(0, rows, step):
            for l0 in range(0, width, CONV_LANES):
                dst[r - 1, r0:r0 + step, l0:l0 + CONV_LANES] = src[r + r0:r + r0 + step, l0:l0 + CONV_LANES]


def _at_offset(src, shifted, off, r0, rows, lanes):
    r = off % 8
    a = off - r + r0
    if r == 0:
        return src[a:a + rows, lanes]
    return shifted[r - 1, a:a + rows, lanes]


def _fill_glu(first, a_ref, b_ref, ah_ref, bh_ref, u_ext, tm):
    uh = ah_ref[...].astype(F32) * _sigmoid(bh_ref[...].astype(F32))
    u_ext[0:HALO, :] = jnp.where(first, 0.0, uh)
    for r0 in range(0, tm, ELEM_ROWS):
        rows = slice(r0, r0 + ELEM_ROWS)
        u_ext[HALO + r0:HALO + r0 + ELEM_ROWS, :] = a_ref[rows, :].astype(F32) * _sigmoid(b_ref[rows, :].astype(F32))


def _layer_norm(cv, lg_ref, lb_ref):
    mu = jnp.mean(cv, axis=-1, keepdims=True)
    cc = cv - mu
    rstd = lax.rsqrt(jnp.mean(cc * cc, axis=-1, keepdims=True) + EPS)
    xh = cc * rstd
    return xh, rstd, xh * lg_ref[...] + lb_ref[...]


def _conv_fwd(proj, y_in, dw_w, dw_b, ln_g, ln_b, pw_w, pw_b, ex, ex_parts):
    R, E = proj.shape
    CW = pw_w.shape[0]
    tm = META_TILE
    NTL = R // tm
    phys = _conv_order(NTL)
    halo = _halo_block(NTL, tm)
    cb = (E - 3 * CW) // CW
    base = HALO - (CONV_K - 1)

    def body(*refs):
        a_ref, b_ref, g_ref, ah_ref, bh_ref, w_ref, wb_ref, lg_ref, lb_ref, pw_ref, pb_ref, yin_ref = refs[:12]
        ex_ins = refs[12:12 + ex.n]
        y_ref, c_ref = refs[12 + ex.n:14 + ex.n]
        ex_outs = refs[14 + ex.n:14 + 2 * ex.n]
        u_ext, u_sh, s_scr, upw_scr = refs[14 + 2 * ex.n:18 + 2 * ex.n]
        sems = refs[18 + 2 * ex.n:]

        @pl.when(pl.program_id(0) == 0)
        def _():
            ex.start(ex_ins, ex_outs, sems)

        _fill_glu(pl.program_id(0) == 0, a_ref, b_ref, ah_ref, bh_ref, u_ext, tm)
        _fill_shifted(u_ext, u_sh)
        for r0 in range(0, tm, CONV_ROWS):
            for l0 in range(0, CW, CONV_LANES):
                lanes = slice(l0, l0 + CONV_LANES)
                acc = None
                for k in range(CONV_K):
                    term = _at_offset(u_ext, u_sh, base + k, r0, CONV_ROWS, lanes) * w_ref[k:k + 1, lanes]
                    acc = term if acc is None else acc + term
                c_ref[r0:r0 + CONV_ROWS, lanes] = acc + wb_ref[:, lanes]
        blocks = [slice(r0, r0 + ELEM_ROWS) for r0 in range(0, tm, ELEM_ROWS)]
        for rows in blocks:
            _, _, ln = _layer_norm(c_ref[rows, :], lg_ref, lb_ref)
            s_scr[rows, :] = (ln * _sigmoid(ln)).astype(BF16)
        upw_scr[...] = _dot(s_scr[...], pw_ref[...]) + pb_ref[...]
        for rows in blocks:
            g = g_ref[rows, :].astype(F32)
            y_ref[rows, :] = (upw_scr[rows, :] * (g * _sigmoid(g))).astype(BF16)

        @pl.when(pl.program_id(0) == NTL - 1)
        def _():
            ex.wait(ex_ins, ex_outs, sems)

    row = lambda l: (0, 0)
    outs = pl.pallas_call(
        body, name="conv_fwd",
        grid=(NTL,),
        in_specs=[pl.BlockSpec((tm, CW), lambda l: (phys(l), cb)),
                  pl.BlockSpec((tm, CW), lambda l: (phys(l), cb + 1)),
                  pl.BlockSpec((tm, CW), lambda l: (phys(l), cb + 2)),
                  pl.BlockSpec((HALO, CW), lambda l: (halo(l), cb)),
                  pl.BlockSpec((HALO, CW), lambda l: (halo(l), cb + 1)),
                  pl.BlockSpec((HALO, CW), row),
                  pl.BlockSpec((1, CW), row), pl.BlockSpec((1, CW), row), pl.BlockSpec((1, CW), row),
                  pl.BlockSpec((CW, CW), row),
                  pl.BlockSpec((1, CW), row),
                  pl.BlockSpec(memory_space=pl.ANY)] + ex.specs,
        out_specs=[pl.BlockSpec((tm, CW), lambda l: (phys(l), 1)),
                   pl.BlockSpec((tm, CW), lambda l: (phys(l), 0))] + ex.specs,
        out_shape=[jax.ShapeDtypeStruct(y_in.shape, BF16), jax.ShapeDtypeStruct((R, CW), F32)] + ex.out_shape,
        input_output_aliases={11: 0},
        scratch_shapes=[pltpu.VMEM((HALO + tm, CW), F32), pltpu.VMEM((7, tm + HALO - 8, CW), F32),
                        pltpu.VMEM((tm, CW), BF16), pltpu.VMEM((tm, CW), F32)] + ex.scratch,
        compiler_params=_params(("arbitrary",), 48),
    )(proj, proj, proj, proj, proj, dw_w, dw_b, ln_g, ln_b, pw_w, pw_b, y_in, *ex_parts)
    return outs[0], outs[1], outs[2:]


def _out_proj_loss(xs, meta_tile, y, w_out, final_g, target):
    SEQ, D = xs.shape
    R, MIX = y.shape
    tm = META_TILE
    n_seq = SEQ // tm
    n_tiles = R // tm
    rows_out = _pick_tile(MIX, 256)

    def body(x_ref, mt_ref, y_ref, w_hbm, fg_ref, t_ref, dh2_ref, dy_ref, dwo_hbm, dfg_ref, loss_ref,
             w_scr, acc, stage, sem):
        i = pl.program_id(0)

        @pl.when(i == 0)
        def _():
            cp = pltpu.make_async_copy(w_hbm, w_scr, sem)
            cp.start()
            acc[...] = jnp.zeros_like(acc)
            dfg_ref[...] = jnp.zeros_like(dfg_ref)
            loss_ref[...] = jnp.zeros_like(loss_ref)
            cp.wait()

        yb = y_ref[...]
        h2 = jnp.where(i < n_seq, x_ref[...], mt_ref[...]) + _dot(yb, w_scr[...])
        r2 = lax.rsqrt(jnp.mean(h2 * h2, axis=-1, keepdims=True) + EPS)
        n = h2 * r2
        fg = fg_ref[...]
        err = jnp.where(i < n_seq, n * fg - t_ref[...], 0.0)
        loss_ref[...] += 0.5 * jnp.sum(jnp.mean(err * err, axis=-1, keepdims=True), axis=0, keepdims=True)
        dout = err * (1.0 / D)
        dfg_ref[...] += jnp.sum(dout * n, axis=0, keepdims=True)
        dn = dout * fg
        dh2 = r2 * (dn - n * jnp.mean(dn * n, axis=-1, keepdims=True))
        dh2_ref[...] = dh2
        dh2b = dh2.astype(BF16)
        dy_ref[...] = _dot(dh2b, w_scr[...], NT).astype(BF16)
        acc[...] += _dot(yb, dh2b, TN)

        @pl.when(i == n_tiles - 1)
        def _():
            for r in range(0, MIX, rows_out):
                stage[...] = acc[r:r + rows_out, :].astype(BF16)
                cp = pltpu.make_async_copy(stage, dwo_hbm.at[r:r + rows_out, :], sem)
                cp.start()
                cp.wait()

    row = lambda i: (0, 0)
    return pl.pallas_call(
        body, name="out_proj_loss",
        grid=(n_tiles,),
        in_specs=[pl.BlockSpec((tm, D), lambda i: (jnp.minimum(i, n_seq - 1), 0)),
                  pl.BlockSpec((tm, D), row),
                  pl.BlockSpec((tm, MIX), lambda i: (i, 0)),
                  pl.BlockSpec(memory_space=pl.ANY),
                  pl.BlockSpec((1, D), row),
                  pl.BlockSpec((tm, D), lambda i: (jnp.minimum(i, n_seq - 1), 0))],
        out_specs=[pl.BlockSpec((tm, D), lambda i: (i, 0)),
                   pl.BlockSpec((tm, MIX), lambda i: (i, 0)),
                   pl.BlockSpec(memory_space=pl.ANY),
                   pl.BlockSpec((1, D), row),
                   pl.BlockSpec((1, 1), row)],
        out_shape=[jax.ShapeDtypeStruct((R, D), F32), jax.ShapeDtypeStruct((R, MIX), BF16),
                   jax.ShapeDtypeStruct((MIX, D), BF16), jax.ShapeDtypeStruct((1, D), F32),
                   jax.ShapeDtypeStruct((1, 1), F32)],
        scratch_shapes=[pltpu.VMEM((MIX, D), BF16), pltpu.VMEM((MIX, D), F32), pltpu.VMEM((rows_out, D), BF16),
                        pltpu.SemaphoreType.DMA],
        compiler_params=_params(("arbitrary",), 60),
    )(xs, meta_tile, y, w_out, final_g, target)


def _conv_bwd(proj, conv_out, dy, dw_w, ln_g, ln_b, pw_w, pw_b):
    R, E = proj.shape
    CW = pw_w.shape[0]
    tm = META_TILE
    NTL = R // tm
    order = _conv_order(NTL)
    phys = lambda i: order(NTL - 1 - i)
    halo_l = _halo_block(NTL, tm)
    halo = lambda i: halo_l(NTL - 1 - i)
    cb = (E - 3 * CW) // CW
    base = HALO - (CONV_K - 1)

    def body(a_ref, b_ref, g_ref, ah_ref, bh_ref, c_ref, dy_ref, w_ref, lg_ref, lb_ref, pw_ref, pb_ref,
             dp_ref, dpw_ref, dww_ref, vec_ref, u_ext, u_sh, dc_ext, dc_sh, du_scr, dww_acc, dpw_acc,
             xh_scr, rstd_scr, ln_scr, sg_scr, upw_scr, s_scr, dupw_scr):
        i = pl.program_id(0)

        @pl.when(i == 0)
        def _():
            dpw_acc[...] = jnp.zeros_like(dpw_acc)
            dww_ref[...] = jnp.zeros_like(dww_ref)
            vec_ref[...] = jnp.zeros_like(vec_ref)
            dww_acc[...] = jnp.zeros_like(dww_acc)
            dc_ext[tm:tm + HALO, :] = jnp.zeros((HALO, CW), F32)

        _fill_glu(i == NTL - 1, a_ref, b_ref, ah_ref, bh_ref, u_ext, tm)
        _fill_shifted(u_ext, u_sh)
        blocks = [slice(r0, r0 + ELEM_ROWS) for r0 in range(0, tm, ELEM_ROWS)]
        for rows in blocks:
            xh, rstd, ln = _layer_norm(c_ref[rows, :], lg_ref, lb_ref)
            sg = _sigmoid(ln)
            xh_scr[rows, :], rstd_scr[rows, :], ln_scr[rows, :], sg_scr[rows, :] = xh, rstd, ln, sg
            s_scr[rows, :] = (ln * sg).astype(BF16)
        upw_scr[...] = _dot(s_scr[...], pw_ref[...]) + pb_ref[...]
        col_sum = jnp.zeros((1, CW), F32)
        for rows in blocks:
            g = g_ref[rows, :].astype(F32)
            sgg = _sigmoid(g)
            dyc = dy_ref[rows, :].astype(F32)
            dp_ref[rows, 2 * CW:3 * CW] = (dyc * upw_scr[rows, :] * _dsilu(g, sgg)).astype(BF16)
            dupw = dyc * (g * sgg)
            dupw_scr[rows, :] = dupw.astype(BF16)
            col_sum = col_sum + jnp.sum(dupw, axis=0, keepdims=True)
        vec_ref[0:1, :] += col_sum
        dpw_acc[...] += _dot(s_scr[...], dupw_scr[...], TN)
        upw_scr[...] = _dot(dupw_scr[...], pw_ref[...], NT)
        sum_g, sum_b, sum_c = col_sum * 0.0, col_sum * 0.0, col_sum * 0.0
        for rows in blocks:
            xh, rstd = xh_scr[rows, :], rstd_scr[rows, :]
            dln = upw_scr[rows, :] * _dsilu(ln_scr[rows, :], sg_scr[rows, :])
            sum_g = sum_g + jnp.sum(dln * xh, axis=0, keepdims=True)
            sum_b = sum_b + jnp.sum(dln, axis=0, keepdims=True)
            dxh = dln * lg_ref[...]
            dc = rstd * (dxh - jnp.mean(dxh, axis=-1, keepdims=True)
                         - xh * jnp.mean(dxh * xh, axis=-1, keepdims=True))
            sum_c = sum_c + jnp.sum(dc, axis=0, keepdims=True)
            dc_ext[rows, :] = dc
        vec_ref[1:2, :] += sum_g
        vec_ref[2:3, :] += sum_b
        vec_ref[3:4, :] += sum_c
        _fill_shifted(dc_ext, dc_sh)

        for l0 in range(0, CW, CONV_LANES):
            lanes = slice(l0, l0 + CONV_LANES)
            for r0 in range(0, tm, CONV_ROWS):
                acc = None
                for k in range(CONV_K):
                    term = _at_offset(dc_ext, dc_sh, CONV_K - 1 - k, r0, CONV_ROWS, lanes) * w_ref[k:k + 1, lanes]
                    acc = term if acc is None else acc + term
                du_scr[r0:r0 + CONV_ROWS, lanes] = acc

        n_grp = tm // 8
        by_shift = [[(k, (base + k) // 8) for k in range(CONV_K) if (base + k) % 8 == r] for r in range(8)]
        for l0 in range(0, CW, LANE):
            lane = slice(l0, l0 + LANE)
            for r in range(8):
                src = u_ext if r == 0 else u_sh.at[r - 1]
                a_lo, a_hi = by_shift[r][0][1], by_shift[r][-1][1]
                sums = {k: None for k, _ in by_shift[r]}
                dcg = {}
                for gi in range(a_lo, n_grp + a_hi):
                    if gi - a_lo < n_grp:
                        dcg[gi - a_lo] = dc_ext[8 * (gi - a_lo):8 * (gi - a_lo) + 8, lane]
                    dcg.pop(gi - a_hi - 1, None)
                    ug = src[8 * gi:8 * gi + 8, lane]
                    for k, a in by_shift[r]:
                        if 0 <= gi - a < n_grp:
                            prod = dcg[gi - a] * ug
                            sums[k] = prod if sums[k] is None else sums[k] + prod
                for k, _ in by_shift[r]:
                    dww_acc[k, :, lane] += sums[k]

        for rows in blocks:
            du = du_scr[rows, :]
            sgb = _sigmoid(b_ref[rows, :].astype(F32))
            dp_ref[rows, 0:CW] = (du * sgb).astype(BF16)
            dp_ref[rows, CW:2 * CW] = (du * a_ref[rows, :].astype(F32) * sgb * (1.0 - sgb)).astype(BF16)
        dc_ext[tm:tm + HALO, :] = dc_ext[0:HALO, :]

        @pl.when(i == NTL - 1)
        def _():
            for k in range(CONV_K):
                dww_ref[k:k + 1, :] = jnp.sum(dww_acc[k], axis=0, keepdims=True)
            dpw_ref[...] = dpw_acc[...].astype(BF16)

    row = lambda i: (0, 0)
    return pl.pallas_call(
        body, name="conv_bwd",
        grid=(NTL,),
        in_specs=[pl.BlockSpec((tm, CW), lambda i: (phys(i), cb)),
                  pl.BlockSpec((tm, CW), lambda i: (phys(i), cb + 1)),
                  pl.BlockSpec((tm, CW), lambda i: (phys(i), cb + 2)),
                  pl.BlockSpec((HALO, CW), lambda i: (halo(i), cb)),
                  pl.BlockSpec((HALO, CW), lambda i: (halo(i), cb + 1)),
                  pl.BlockSpec((tm, CW), lambda i: (phys(i), 0)),
                  pl.BlockSpec((tm, CW), lambda i: (phys(i), 1)),
                  pl.BlockSpec((HALO, CW), row),
                  pl.BlockSpec((1, CW), row), pl.BlockSpec((1, CW), row),
                  pl.BlockSpec((CW, CW), row),
                  pl.BlockSpec((1, CW), row)],
        out_specs=[pl.BlockSpec((tm, 3 * CW), lambda i: (phys(i), 0)),
                   pl.BlockSpec((CW, CW), row),
                   pl.BlockSpec((HALO, CW), row),
                   pl.BlockSpec((8, CW), row)],
        out_shape=[jax.ShapeDtypeStruct((R, 3 * CW), BF16), jax.ShapeDtypeStruct((CW, CW), BF16),
                   jax.ShapeDtypeStruct((HALO, CW), F32), jax.ShapeDtypeStruct((8, CW), F32)],
        scratch_shapes=[pltpu.VMEM((HALO + tm, CW), F32), pltpu.VMEM((7, tm + HALO - 8, CW), F32),
                        pltpu.VMEM((tm + HALO, CW), F32), pltpu.VMEM((7, tm + HALO - 8, CW), F32),
                        pltpu.VMEM((tm, CW), F32), pltpu.VMEM((CONV_K, 8, CW), F32), pltpu.VMEM((CW, CW), F32),
                        pltpu.VMEM((tm, CW), F32), pltpu.VMEM((tm, 1), F32), pltpu.VMEM((tm, CW), F32),
                        pltpu.VMEM((tm, CW), F32), pltpu.VMEM((tm, CW), F32), pltpu.VMEM((tm, CW), BF16),
                        pltpu.VMEM((tm, CW), BF16)],
        compiler_params=_params(("arbitrary",), 60),
    )(proj, proj, proj, proj, proj, conv_out, dy, dw_w, ln_g, ln_b, pw_w, pw_b)


def _retention_bwd(proj, rope, gn_g, states, dy, ex, ex_parts):
    R, E = proj.shape
    RW = gn_g.shape[1]
    H = RET_HEADS
    hd = RW // H
    half = hd // 2
    NC = R // CHUNK
    mask, qd, kd, cd = _decay_tables(H)
    scale = float(hd) ** -0.5
    order = _chunk_order(NC)
    phys = lambda i: order(NC - 1 - i)

    def body(*refs):
        p_ref, cb_ref, sb_ref, ci_ref, si_ref, mask_ref, qd_ref, kd_ref, gn_ref, st_ref, dy_ref = refs[:11]
        ex_ins = refs[11:11 + ex.n]
        dp_ref, dgn_ref = refs[11 + ex.n:13 + ex.n]
        ex_outs = refs[13 + ex.n:13 + 2 * ex.n]
        dstate = refs[13 + 2 * ex.n]
        sems = refs[14 + 2 * ex.n:]

        @pl.when(pl.program_id(0) == 0)
        def _():
            ex.start(ex_ins, ex_outs, sems)
            dstate[...] = jnp.zeros_like(dstate)
            dgn_ref[...] = jnp.zeros_like(dgn_ref)

        cs, sn = _rope_chunk(cb_ref, sb_ref, ci_ref, si_ref)
        hs = range(H)
        col = lambda j, h: slice(j * RW + h * hd, j * RW + (h + 1) * hd)
        qr = [_rot(p_ref[:, col(0, h)].astype(F32), cs, sn, half) for h in hs]
        kr = [_rot(p_ref[:, col(1, h)].astype(F32), cs, sn, half) * scale for h in hs]
        v = [p_ref[:, col(2, h)] for h in hs]
        qb = [qr[h].astype(BF16) for h in hs]
        kb = [kr[h].astype(BF16) for h in hs]
        qdb = [(qr[h] * qd_ref[h]).astype(BF16) for h in hs]
        kdb = [(kr[h] * kd_ref[h]).astype(BF16) for h in hs]
        s_prev = [st_ref[0, h] for h in hs]
        dst = [dstate[h] for h in hs]
        dstb = [dst[h].astype(BF16) for h in hs]
        sb = [(_dot(qb[h], kb[h], NT) * mask_ref[h]).astype(BF16) for h in hs]
        y_raw = [_dot(sb[h], v[h]) + _dot(qdb[h], s_prev[h]) for h in hs]
        dyrb, dg = [], []
        for h in hs:
            g = p_ref[:, col(3, h)].astype(F32)
            mu = jnp.mean(y_raw[h], axis=-1, keepdims=True)
            yc = y_raw[h] - mu
            rstd = lax.rsqrt(jnp.mean(yc * yc, axis=-1, keepdims=True) + EPS)
            xh = yc * rstd
            gn = gn_ref[:, col(0, h)]
            sg = _sigmoid(g)
            dyh = dy_ref[:, col(0, h)].astype(F32)
            dg.append((dyh * (xh * gn) * _dsilu(g, sg)).astype(BF16))
            dyn = dyh * (g * sg)
            dgn_ref[:, col(0, h)] += jnp.sum(dyn * xh, axis=0, keepdims=True)
            dxh = dyn * gn
            dyr = rstd * (dxh - jnp.mean(dxh, axis=-1, keepdims=True)
                          - xh * jnp.mean(dxh * xh, axis=-1, keepdims=True))
            dyrb.append(dyr.astype(BF16))
        dsb = [(_dot(dyrb[h], v[h], NT) * mask_ref[h]).astype(BF16) for h in hs]
        dqr = [_dot(dsb[h], kb[h]) + _dot(dyrb[h], s_prev[h], NT) * qd_ref[h] for h in hs]
        dkr = [_dot(dsb[h], qb[h], TN) + _dot(v[h], dstb[h], NT) * kd_ref[h] for h in hs]
        dv = [_dot(sb[h], dyrb[h], TN) + _dot(kdb[h], dstb[h]) for h in hs]
        dst_new = [dst[h] * cd[h] + _dot(qdb[h], dyrb[h], TN) for h in hs]
        for h in hs:
            dstate[h] = dst_new[h]
            dp_ref[:, col(0, h)] = _rot_inv(dqr[h], cs, sn, half).astype(BF16)
            dp_ref[:, col(1, h)] = (_rot_inv(dkr[h], cs, sn, half) * scale).astype(BF16)
            dp_ref[:, col(2, h)] = dv[h].astype(BF16)
            dp_ref[:, col(3, h)] = dg[h]

        @pl.when(pl.program_id(0) == NC - 1)
        def _():
            ex.wait(ex_ins, ex_outs, sems)

    const3 = lambda i: (0, 0, 0)
    outs = pl.pallas_call(
        body, name="retention_bwd",
        grid=(NC,),
        in_specs=[pl.BlockSpec((CHUNK, 4 * RW), lambda i: (phys(i), 0)),
                  pl.BlockSpec((1, 1, half), lambda i: (phys(i), 0, 0)),
                  pl.BlockSpec((1, 1, half), lambda i: (phys(i), 0, 0)),
                  pl.BlockSpec((CHUNK, half), lambda i: (0, 0)),
                  pl.BlockSpec((CHUNK, half), lambda i: (0, 0)),
                  pl.BlockSpec((H, CHUNK, CHUNK), const3),
                  pl.BlockSpec((H, CHUNK, 1), const3),
                  pl.BlockSpec((H, CHUNK, 1), const3),
                  pl.BlockSpec((1, RW), lambda i: (0, 0)),
                  pl.BlockSpec((1, H, hd, hd), lambda i: (phys(i), 0, 0, 0)),
                  pl.BlockSpec((CHUNK, RW), lambda i: (phys(i), 0))] + ex.specs,
        out_specs=[pl.BlockSpec((CHUNK, 4 * RW), lambda i: (phys(i), 0)),
                   pl.BlockSpec((1, RW), lambda i: (0, 0))] + ex.specs,
        out_shape=[jax.ShapeDtypeStruct((R, 4 * RW), BF16), jax.ShapeDtypeStruct((1, RW), F32)] + ex.out_shape,
        scratch_shapes=[pltpu.VMEM((H, hd, hd), F32)] + ex.scratch,
        compiler_params=_params(("arbitrary",), 32),
    )(proj, *rope, jnp.asarray(mask), jnp.asarray(qd), jnp.asarray(kd), gn_g, states, dy, *ex_parts)
    return outs[0], outs[1], outs[2:]


def _dproj_specs(tk, tn, n_ret, tile_axis, col_axis):
    def ret_map(*ids):
        t, j = ids[tile_axis], ids[col_axis]
        return (jnp.where(j < n_ret, t, 0), jnp.minimum(j, n_ret - 1))

    def conv_map(*ids):
        t, j = ids[tile_axis], ids[col_axis]
        return (jnp.where(j >= n_ret, t, 0), jnp.maximum(j - n_ret, 0))

    return pl.BlockSpec((tk, tn), ret_map), pl.BlockSpec((tk, tn), conv_map)


def _w_in_grad(hn, dp_ret, dp_conv):
    R, D = hn.shape
    tn = _pick_tile(dp_conv.shape[1] // 3, 1024, 128)
    n_ret, n_conv = dp_ret.shape[1] // tn, dp_conv.shape[1] // tn
    E = dp_ret.shape[1] + dp_conv.shape[1]
    tk = _pick_tile(R, 1024, MXU_DIM)
    n_t = R // tk
    ret_spec, conv_spec = _dproj_specs(tk, tn, n_ret, 1, 0)

    def body(hn_ref, r_ref, c_ref, out_ref, acc):
        j, t = pl.program_id(0), pl.program_id(1)

        @pl.when(t == 0)
        def _():
            acc[...] = jnp.zeros_like(acc)

        @pl.when(j < n_ret)
        def _():
            acc[...] += _dot(hn_ref[...], r_ref[...], TN)

        @pl.when(j >= n_ret)
        def _():
            acc[...] += _dot(hn_ref[...], c_ref[...], TN)

        @pl.when(t == n_t - 1)
        def _():
            out_ref[...] = acc[...].astype(BF16)

    return pl.pallas_call(
        body, name="w_in_grad",
        grid=(n_ret + n_conv, n_t),
        in_specs=[pl.BlockSpec((tk, D), lambda j, t: (t, 0)), ret_spec, conv_spec],
        out_specs=pl.BlockSpec((D, tn), lambda j, t: (0, j)),
        out_shape=jax.ShapeDtypeStruct((D, E), BF16),
        scratch_shapes=[pltpu.VMEM((D, tn), F32)],
        compiler_params=_params(("arbitrary", "arbitrary"), 48),
    )(hn, dp_ret, dp_conv)


def _h_grad(dp_ret, dp_conv, w_in, xs, meta_tile, dh2, ln_g, ex, ex_parts):
    R, D = dh2.shape
    te = meta_tile.shape[0]
    n_x = xs.shape[0] // te
    tn = _pick_tile(dp_conv.shape[1] // 3, 1024, 128)
    n_ret, n_conv = dp_ret.shape[1] // tn, dp_conv.shape[1] // tn
    n_k = n_ret + n_conv
    tm = _pick_tile(R, 1024, te)
    n_e = tm // te
    n_t = R // tm
    assert n_e <= n_k and n_x * te + te == R

    def ret_map(t, k):
        return (jnp.where(k < n_ret, jnp.minimum(t, n_t - 1), 0), jnp.minimum(k, n_ret - 1))

    def conv_map(t, k):
        return (jnp.where(k >= n_ret, jnp.minimum(t, n_t - 1), 0), jnp.maximum(k - n_ret, 0))

    def row_block(t, k):
        return jnp.maximum(t - 1, 0) * n_e + jnp.where(t > 0, jnp.minimum(k, n_e - 1), 0)

    def body(*refs):
        r_ref, c_ref, w_ref, x_ref, mt_ref, dh2_ref, g_ref = refs[:7]
        ex_ins = refs[7:7 + ex.n]
        o = 7 + ex.n
        dh_ref, dlg_ref = refs[o:o + 2]
        ex_outs = refs[o + 2:o + 2 + ex.n]
        acc = refs[o + 2 + ex.n]
        sems = refs[o + 3 + ex.n:]
        t, k = pl.program_id(0), pl.program_id(1)
        cur, old = t % 2, (t + 1) % 2

        @pl.when((k == 0) & (t == 0))
        def _():
            ex.start(ex_ins, ex_outs, sems)
            dlg_ref[...] = jnp.zeros_like(dlg_ref)

        @pl.when((k < n_ret) & (t < n_t))
        def _():
            part = _dot(r_ref[...], w_ref[...], NT)

            @pl.when(k == 0)
            def _():
                acc[cur] = part

            @pl.when(k > 0)
            def _():
                acc[cur] += part

        @pl.when((k >= n_ret) & (t < n_t))
        def _():
            acc[cur] += _dot(c_ref[...], w_ref[...], NT)

        @pl.when((k < n_e) & (t > 0))
        def _():
            hv = jnp.where(row_block(t, k) < n_x, x_ref[...], mt_ref[...])
            r = lax.rsqrt(jnp.mean(hv * hv, axis=-1, keepdims=True) + EPS)
            nrm = hv * r
            dhn = acc[old, pl.ds(pl.multiple_of(k * te, te), te), :]
            dlg_ref[...] += jnp.sum(dhn * nrm, axis=0, keepdims=True)
            dn = dhn * g_ref[...]
            dh_ref[...] = dh2_ref[...] + r * (dn - nrm * jnp.mean(dn * nrm, axis=-1, keepdims=True))

        @pl.when((k == n_k - 1) & (t == n_t))
        def _():
            ex.wait(ex_ins, ex_outs, sems)

    row = lambda t, k: (0, 0)
    outs = pl.pallas_call(
        body, name="h_grad",
        grid=(n_t + 1, n_k),
        in_specs=[pl.BlockSpec((tm, tn), ret_map), pl.BlockSpec((tm, tn), conv_map),
                  pl.BlockSpec((D, tn), lambda t, k: (0, k)),
                  pl.BlockSpec((te, D), lambda t, k: (jnp.minimum(row_block(t, k), n_x - 1), 0)),
                  pl.BlockSpec((te, D), row),
                  pl.BlockSpec((te, D), lambda t, k: (row_block(t, k), 0)),
                  pl.BlockSpec((1, D), row)] + ex.specs,
        out_specs=[pl.BlockSpec((te, D), lambda t, k: (row_block(t, k), 0)),
                   pl.BlockSpec((1, D), row)] + ex.specs,
        out_shape=[jax.ShapeDtypeStruct((R, D), F32), jax.ShapeDtypeStruct((1, D), F32)] + ex.out_shape,
        scratch_shapes=[pltpu.VMEM((2, tm, D), F32)] + ex.scratch,
        compiler_params=_params(("arbitrary", "arbitrary"), 60),
    )(dp_ret, dp_conv, w_in, xs, meta_tile, dh2, ln_g, *ex_parts)
    return outs[0], outs[1], outs[2:]


def _adamw(w, g, m, v):
    m = ADAM_B1 * m + (1.0 - ADAM_B1) * g
    v = ADAM_B2 * v + (1.0 - ADAM_B2) * (g * g)
    m_hat = m / (1.0 - ADAM_B1 ** ADAM_STEP)
    v_hat = v / (1.0 - ADAM_B2 ** ADAM_STEP)
    delta = -ADAM_LR * (m_hat / (jnp.sqrt(v_hat) + ADAM_EPS) + ADAM_WD * w)
    return delta, m, v


def _sum_slots(ref):
    g = ref[0].astype(F32)
    for s in range(1, N_DEV):
        g = g + ref[s].astype(F32)
    return g


def _sum_adamw(name, parts, w, m, v, rows_target, ex=None, ex_parts=()):
    rows, cols = w.shape
    tr = _pick_tile(rows, rows_target, 8)
    n_ex = 0 if ex is None else ex.n
    n_steps = rows // tr

    def body(*refs):
        p_ref, w_ref, m_ref, v_ref = refs[:4]
        ex_ins = refs[4:4 + n_ex]
        o = 4 + n_ex
        g_ref, d_ref, nm_ref, nv_ref = refs[o:o + 4]
        ex_outs, sems = refs[o + 4:o + 4 + n_ex], refs[o + 4 + n_ex:]
        if ex is not None:
            @pl.when(pl.program_id(0) == 0)
            def _():
                ex.start(ex_ins, ex_outs, sems)

        g = _sum_slots(p_ref)
        d, nm, nv = _adamw(w_ref[...], g, m_ref[...], v_ref[...])
        g_ref[...] = g
        d_ref[...] = d
        nm_ref[...] = nm
        nv_ref[...] = nv
        if ex is not None:
            @pl.when(pl.program_id(0) == n_steps - 1)
            def _():
                ex.wait(ex_ins, ex_outs, sems)

    tile = pl.BlockSpec((tr, cols), lambda i: (i, 0))
    ex_specs, ex_shape, ex_scratch = ([], [], []) if ex is None else (ex.specs, ex.out_shape, ex.scratch)
    outs = pl.pallas_call(
        body, name=name,
        grid=(n_steps,),
        in_specs=[pl.BlockSpec((N_DEV, tr, cols), lambda i: (0, i, 0)), tile, tile, tile] + ex_specs,
        out_specs=[tile] * 4 + ex_specs,
        out_shape=[jax.ShapeDtypeStruct((rows, cols), F32)] * 4 + ex_shape,
        scratch_shapes=ex_scratch,
        compiler_params=_params(("arbitrary",), 40),
    )(parts, w, m, v, *ex_parts)
    return outs[:4], outs[4:]


def _sum_adamw_small(parts_list, w_list, m_list, v_list, loss_parts):
    n = len(w_list)

    def body(*refs):
        p_refs, w_refs, m_refs, v_refs = refs[:n], refs[n:2 * n], refs[2 * n:3 * n], refs[3 * n:4 * n]
        lp_ref = refs[4 * n]
        outs = refs[4 * n + 1:]
        for a in range(n):
            g = _sum_slots(p_refs[a])
            d, nm, nv = _adamw(w_refs[a][...], g, m_refs[a][...], v_refs[a][...])
            outs[4 * a][...] = g
            outs[4 * a + 1][...] = d
            outs[4 * a + 2][...] = nm
            outs[4 * a + 3][...] = nv
        outs[4 * n][...] = _sum_slots(lp_ref)

    out_shape = []
    for w in w_list:
        out_shape += [jax.ShapeDtypeStruct(w.shape, F32)] * 4
    out_shape.append(jax.ShapeDtypeStruct(loss_parts.shape[1:], F32))
    return pl.pallas_call(body, name="sum_adamw_small", out_shape=out_shape)(
        *parts_list, *w_list, *m_list, *v_list, loss_parts)


def kernel(x, meta_tokens, ln_g, w_in, ret_gn_g, conv_dw_w, conv_dw_b, conv_ln_g, conv_ln_b, conv_pw_w, conv_pw_b, w_out, final_g, loss_target, m_meta_tokens, m_ln_g, m_w_in, m_ret_gn_g, m_conv_dw_w, m_conv_dw_b, m_conv_ln_g, m_conv_ln_b, m_conv_pw_w, m_conv_pw_b, m_w_out, m_final_g, v_meta_tokens, v_ln_g, v_w_in, v_ret_gn_g, v_conv_dw_w, v_conv_dw_b, v_conv_ln_g, v_conv_ln_b, v_conv_pw_w, v_conv_pw_b, v_w_out, v_final_g):
    _, SEQ, D = x.shape
    MIX = w_out.shape[2]
    RW = ret_gn_g.shape[1]
    CW = conv_pw_b.shape[1]
    assert RW == CW and MIX == RW + CW and SEQ % META_TILE == 0 and CONV_K - 1 <= HALO
    R = SEQ + META_TILE
    hd = RW // RET_HEADS
    half = hd // 2
    me = 4 * lax.axis_index("x") + 2 * lax.axis_index("y") + lax.axis_index("c")

    dw_pad = jnp.pad(conv_dw_w[0], ((0, HALO - CONV_K), (0, 0)))
    dw_g, meta_g = _gather_weights([dw_pad, meta_tokens], [1, 1])

    n_seq_chunks = SEQ // CHUNK
    base = jnp.concatenate([jnp.arange(n_seq_chunks, dtype=F32) * CHUNK + N_META,
                            jnp.zeros((META_TILE // CHUNK - 1,), F32), jnp.full((1,), N_META - CHUNK, F32)])
    inv_freq = ROPE_BASE ** (-jnp.arange(half, dtype=F32) / half)
    ang_base = (base[:, None] * inv_freq[None, :])[:, None, :]
    ang_row = jnp.arange(CHUNK, dtype=F32)[:, None] * inv_freq[None, :]
    rope = (jnp.cos(ang_base), jnp.sin(ang_base), jnp.cos(ang_row), jnp.sin(ang_row))

    xs = x[0]
    meta_tile = jnp.concatenate([jnp.zeros((META_TILE - N_META, D), F32), meta_g], axis=0)
    target = loss_target[0]
    final_g2 = final_g[None, :]

    hn = _rms_norm(xs, meta_tile, ln_g)
    pw_shard, w_out_shard = [conv_pw_w[0].astype(BF16)], [w_out[0].astype(BF16)]
    proj, w_in_g, (pw_g,) = _in_proj_gather(hn, w_in[0].astype(BF16), _Exchange(pw_shard, [None]), pw_shard)
    pw_g = pw_g.reshape(CW, CW)
    y, states = _retention_fwd(proj, rope, ret_gn_g, MIX)
    y, conv_out, (w_out_g,) = _conv_fwd(proj, y, dw_g, conv_dw_b, conv_ln_g, conv_ln_b, pw_g, conv_pw_b,
                                        _Exchange(w_out_shard, [None]), w_out_shard)
    w_out_g = w_out_g.reshape(MIX, D)
    dh2, dy, dwo_p, dfg_p, loss_p = _out_proj_loss(xs, meta_tile, y, w_out_g, final_g2, target)

    dp_conv, dpw_p, dww_p, cvec_p = _conv_bwd(proj, conv_out, dy, dw_g, conv_ln_g, conv_ln_b, pw_g, conv_pw_b)
    dp_ret, dgn_p, (r_wo, r_pw) = _retention_bwd(proj, rope, ret_gn_g, states, dy,
                                                 _Exchange([dwo_p, dpw_p], [0, 0]), [dwo_p, dpw_p])
    dwi_p = _w_in_grad(hn, dp_ret, dp_conv)
    dh, dlg_p, (r_wi,) = _h_grad(dp_ret, dp_conv, w_in_g, xs, meta_tile, dh2, ln_g, _Exchange([dwi_p], [1]), [dwi_p])
    grad_x = dh[:SEQ][None]

    def at_row(r, a, b=None):
        v = a if b is None else jnp.concatenate([a, b], axis=1)
        return jnp.pad(v, ((r, 7 - r), (0, D - v.shape[1])))
    vec8 = (at_row(0, dlg_p) + at_row(1, dfg_p)
            + at_row(2, dgn_p, cvec_p[3:4])
            + at_row(3, cvec_p[1:2], cvec_p[2:3])
            + at_row(4, cvec_p[0:1], jnp.broadcast_to(loss_p, (1, CW))))
    small = jnp.concatenate([dh[R - N_META:], vec8,
                             jnp.zeros((SMALL_ROWS - N_META - 8, D), F32)], axis=0)

    (g_wi, d_wi, nm_wi, nv_wi), (r_dww, r_small) = _sum_adamw(
        "sum_adamw_w_in", r_wi, w_in[0], m_w_in[0], v_w_in[0], 256,
        ex=_Exchange([dww_p, small], [1, None]), ex_parts=[dww_p, small])
    (g_wo, d_wo, nm_wo, nv_wo), _ = _sum_adamw("sum_adamw_w_out", r_wo, w_out[0], m_w_out[0], v_w_out[0], 128)
    (g_pw, d_pw, nm_pw, nv_pw), _ = _sum_adamw("sum_adamw_pw", r_pw, conv_pw_w[0], m_conv_pw_w[0], v_conv_pw_w[0], 128)

    dcol = D // N_DEV
    sm = lambda r0, nr, c0, nc: lax.slice(r_small, (0, r0, c0), (N_DEV, r0 + nr, c0 + nc))
    meta_parts = lax.dynamic_slice(r_small, (0, 0, me * dcol), (N_DEV, N_META, dcol))
    small_parts = [meta_parts, sm(16, 1, 0, D), sm(18, 1, 0, RW), r_dww, sm(18, 1, RW, CW),
                   sm(19, 1, 0, CW), sm(19, 1, CW, CW), sm(20, 1, 0, CW), sm(17, 1, 0, D)]
    pad31 = lambda a: jnp.pad(a, ((0, HALO - CONV_K), (0, 0)))
    ws = [meta_tokens, ln_g, ret_gn_g, pad31(conv_dw_w[0]), conv_dw_b, conv_ln_g, conv_ln_b, conv_pw_b, final_g2]
    ms = [m_meta_tokens, m_ln_g, m_ret_gn_g, pad31(m_conv_dw_w[0]), m_conv_dw_b, m_conv_ln_g, m_conv_ln_b,
          m_conv_pw_b, m_final_g[None, :]]
    vs = [v_meta_tokens, v_ln_g, v_ret_gn_g, pad31(v_conv_dw_w[0]), v_conv_dw_b, v_conv_ln_g, v_conv_ln_b,
          v_conv_pw_b, v_final_g[None, :]]
    loss_parts = sm(20, 1, CW, 1)
    outs = _sum_adamw_small(small_parts, ws, ms, vs, loss_parts)
    loss = outs[-1][0, 0]
    quad = [outs[4 * a:4 * a + 4] for a in range(len(ws))]
    (q_meta, q_lng, q_gn, q_dww, q_dwb, q_clg, q_clb, q_pwb, q_fg) = quad
    q_dww = [t[:CONV_K][None] for t in q_dww]
    q_fg = [t[0] for t in q_fg]
    q_wi = [t[None] for t in (g_wi, d_wi, nm_wi, nv_wi)]
    q_wo = [t[None] for t in (g_wo, d_wo, nm_wo, nv_wo)]
    q_pw = [t[None] for t in (g_pw, d_pw, nm_pw, nv_pw)]

    per_w = [q_meta, q_lng, q_wi, q_gn, q_dww, q_dwb, q_clg, q_clb, q_pw, q_pwb, q_wo, q_fg]
    result = [loss, grad_x]
    for which in range(4):
        result += [q[which] for q in per_w]
    return tuple(result)
```

```python
import functools

import numpy as np
import jax
import jax.numpy as jnp
from jax import lax
from jax.experimental import pallas as pl
from jax.experimental.pallas import tpu as pltpu

N_META = 16
RET_HEADS = 4
CONV_K = 31
CHUNK = 128
ROPE_BASE = 10000.0
EPS = 1e-6
ADAM_LR = 0.001
ADAM_B1 = 0.9
ADAM_B2 = 0.999
ADAM_EPS = 1e-08
ADAM_WD = 0.01
ADAM_STEP = 10

N_DEV = 8
META_TILE = 256
HALO = 32
SMALL_ROWS = 32
VMEM_BYTES_V7X = 64 * 1024 * 1024
MXU_DIM = 256

F32 = jnp.float32
BF16 = jnp.bfloat16
MESH = pl.DeviceIdType.MESH

NN = (((1,), (0,)), ((), ()))
NT = (((1,), (1,)), ((), ()))
TN = (((0,), (0,)), ((), ()))


def _dot(a, b, dims=NN):
    return lax.dot_general(a, b, dims, preferred_element_type=F32)


def _pick_tile(n, target, mult=16):
    best = None
    for t in range(mult, min(n, target) + 1, mult):
        if n % t == 0:
            best = t
    assert best is not None, (n, target)
    return best


def _params(sem=None, vmem_mb=None):
    kw = {}
    if sem is not None:
        kw["dimension_semantics"] = sem
    if vmem_mb is not None:
        kw["vmem_limit_bytes"] = min(vmem_mb * 1024 * 1024, VMEM_BYTES_V7X - 4 * 1024 * 1024)
    return pltpu.CompilerParams(**kw)


def _sigmoid(x):
    return jax.nn.sigmoid(x)


def _dsilu(x, sg):
    return sg * (1.0 + x * (1.0 - sg))


def _decay_tables(heads):
    h = np.arange(heads, dtype=np.float32)
    gamma = (1.0 - np.exp2(-5.0 - h)).astype(np.float32)
    log_g = np.log(gamma).astype(np.float32)
    idx = np.arange(CHUNK, dtype=np.float32)
    rel = idx[:, None] - idx[None, :]
    mask = np.where(rel[None] >= 0, np.exp(np.maximum(rel, 0.0)[None] * log_g[:, None, None]), 0.0)
    qd = np.exp((idx[None, :] + 1.0) * log_g[:, None])
    kd = np.exp((CHUNK - 1.0 - idx[None, :]) * log_g[:, None])
    cd = np.exp(CHUNK * log_g)
    return (mask.astype(np.float32), qd.astype(np.float32)[:, :, None], kd.astype(np.float32)[:, :, None],
            [float(c) for c in cd.astype(np.float32)])


def _gather_weights(shards, block_axes):
    n_arr = len(shards)
    out_shapes = []
    for s, ax in zip(shards, block_axes):
        shp = list(s.shape)
        shp[ax] *= N_DEV
        out_shapes.append(jax.ShapeDtypeStruct(tuple(shp), s.dtype))

    def body(*refs):
        ins, outs = refs[:n_arr], refs[n_arr:2 * n_arr]
        send_sems, recv_sems, local_sems = refs[2 * n_arr:]
        x, y, c = lax.axis_index("x"), lax.axis_index("y"), lax.axis_index("c")
        me, sibling = (x, y, c), (x, y, 1 - c)
        chips = [(1 - x, y), (x, 1 - y), (1 - x, 1 - y)]

        def block(a, dev):
            n = ins[a].shape[block_axes[a]]
            start = pl.multiple_of((4 * dev[0] + 2 * dev[1] + dev[2]) * n, n)
            idx = [slice(None)] * len(ins[a].shape)
            idx[block_axes[a]] = pl.ds(start, n)
            return outs[a].at[tuple(idx)]

        def copy(a, k, dev, to, src=None):
            return pltpu.make_async_remote_copy(
                src_ref=block(a, dev) if src is None else src, dst_ref=block(a, dev),
                send_sem=send_sems.at[a, k], recv_sem=recv_sems.at[a, k],
                device_id=to, device_id_type=MESH)

        mine = [pltpu.make_async_copy(ins[a], block(a, me), local_sems.at[a]) for a in range(n_arr)]
        for cp in mine:
            cp.start()
        first = []
        for a in range(n_arr):
            first.append(copy(a, 0, me, sibling, src=ins[a]))
            first += [copy(a, 1 + j, me, (*chip, c), src=ins[a]) for j, chip in enumerate(chips)]
        for cp in first:
            cp.start()
        passed = []
        for j, chip in enumerate(chips):
            for a in range(n_arr):
                copy(a, 1 + j, (*chip, c), me).wait_recv()
                fwd = copy(a, 4 + j, (*chip, c), sibling)
                fwd.start()
                passed.append(fwd)
        for a in range(n_arr):
            copy(a, 0, sibling, me).wait_recv()
            for j, chip in enumerate(chips):
                copy(a, 4 + j, (*chip, 1 - c), me).wait_recv()
        for cp in first + passed:
            cp.wait_send()
        for cp in mine:
            cp.wait()

    hbm = pl.BlockSpec(memory_space=pl.ANY)
    return pl.pallas_call(
        body, name="gather_weights",
        out_shape=out_shapes,
        in_specs=[hbm] * n_arr, out_specs=[hbm] * n_arr,
        scratch_shapes=[pltpu.SemaphoreType.DMA((n_arr, 7)), pltpu.SemaphoreType.DMA((n_arr, 7)),
                        pltpu.SemaphoreType.DMA((n_arr,))],
    )(*shards)


class _Exchange:
    def __init__(self, parts, block_axes):
        self.block_axes = list(block_axes)
        self.n = len(parts)
        self.out_shape = []
        for p, ax in zip(parts, block_axes):
            shp = list(p.shape)
            if ax is not None:
                assert shp[ax] % N_DEV == 0
                shp[ax] //= N_DEV
            self.out_shape.append(jax.ShapeDtypeStruct((N_DEV, *shp), p.dtype))
        self.scratch = [pltpu.SemaphoreType.DMA((self.n, N_DEV - 1)), pltpu.SemaphoreType.DMA((self.n, N_DEV - 1)),
                        pltpu.SemaphoreType.DMA((self.n,))]
        self.specs = [pl.BlockSpec(memory_space=pl.ANY)] * self.n

    def _copies(self, ins, outs, sems):
        send_sems, recv_sems, local_sems = sems
        x, y, c = lax.axis_index("x"), lax.axis_index("y"), lax.axis_index("c")
        me_idx = 4 * x + 2 * y + c

        def src_block(a, dev_idx):
            ax = self.block_axes[a]
            if ax is None:
                return ins[a]
            n = ins[a].shape[ax] // N_DEV
            idx = [slice(None)] * len(ins[a].shape)
            idx[ax] = pl.ds(pl.multiple_of(dev_idx * n, n), n)
            return ins[a].at[tuple(idx)]

        local = [pltpu.make_async_copy(src_block(a, me_idx), outs[a].at[me_idx], local_sems.at[a])
                 for a in range(self.n)]
        remote = []
        for m in range(1, N_DEV):
            px, py, pc = x ^ ((m >> 2) & 1), y ^ ((m >> 1) & 1), c ^ (m & 1)
            for a in range(self.n):
                remote.append(pltpu.make_async_remote_copy(
                    src_ref=src_block(a, 4 * px + 2 * py + pc), dst_ref=outs[a].at[me_idx],
                    send_sem=send_sems.at[a, m - 1], recv_sem=recv_sems.at[a, m - 1],
                    device_id=(px, py, pc), device_id_type=MESH))
        return local, remote

    def start(self, ins, outs, sems):
        local, remote = self._copies(ins, outs, sems)
        for cp in local + remote:
            cp.start()

    def wait(self, ins, outs, sems):
        local, remote = self._copies(ins, outs, sems)
        for cp in remote:
            cp.wait_recv()
        for cp in remote:
            cp.wait_send()
        for cp in local:
            cp.wait()


def _rms_norm(xs, meta_tile, ln_g):
    SEQ, D = xs.shape
    tm = meta_tile.shape[0]
    n_seq = SEQ // tm

    def body(x_ref, mt_ref, g_ref, hn_ref):
        hv = jnp.where(pl.program_id(0) < n_seq, x_ref[...], mt_ref[...])
        r = lax.rsqrt(jnp.mean(hv * hv, axis=-1, keepdims=True) + EPS)
        hn_ref[...] = (hv * r * g_ref[...]).astype(BF16)

    return pl.pallas_call(
        body, name="rms_norm",
        grid=(n_seq + 1,),
        in_specs=[pl.BlockSpec((tm, D), lambda i: (jnp.minimum(i, n_seq - 1), 0)),
                  pl.BlockSpec((tm, D), lambda i: (0, 0)),
                  pl.BlockSpec((1, D), lambda i: (0, 0))],
        out_specs=pl.BlockSpec((tm, D), lambda i: (i, 0)),
        out_shape=jax.ShapeDtypeStruct((SEQ + tm, D), BF16),
        compiler_params=_params(("arbitrary",), 32),
    )(xs, meta_tile, ln_g)


def _chip_visited(q):
    mine = 2 * lax.axis_index("x") + lax.axis_index("y")
    return mine ^ (((q & 1) << 1) | (q >> 1))


def _in_proj_gather(hn, w_shard, ex, ex_parts):
    R, D = hn.shape
    wb = w_shard.shape[1]
    E, tn = wb * N_DEV, 2 * wb
    n_q = N_DEV // 2
    tm = _pick_tile(R, min(768, R // 2), MXU_DIM)
    n_i = R // tm

    def body(*refs):
        hn_ref, wsh_hbm = refs[:2]
        ex_ins = refs[2:2 + ex.n]
        proj_ref, wg_hbm = refs[2 + ex.n:4 + ex.n]
        ex_outs = refs[4 + ex.n:4 + 2 * ex.n]
        w_vmem, send_sems, recv_sems, local_sem, vmem_sems = refs[4 + 2 * ex.n:9 + 2 * ex.n]
        ex_sems = refs[9 + 2 * ex.n:]
        q, i = pl.program_id(0), pl.program_id(1)
        x, y, c = lax.axis_index("x"), lax.axis_index("y"), lax.axis_index("c")
        me, sibling = (x, y, c), (x, y, 1 - c)
        chips = [(1 - x, y), (x, 1 - y), (1 - x, 1 - y)]

        def block(dev):
            return wg_hbm.at[:, pl.ds(pl.multiple_of((4 * dev[0] + 2 * dev[1] + dev[2]) * wb, wb), wb)]

        def copy(k, dev, to, src=None):
            return pltpu.make_async_remote_copy(
                src_ref=block(dev) if src is None else src, dst_ref=block(dev),
                send_sem=send_sems.at[k], recv_sem=recv_sems.at[k], device_id=to, device_id_type=MESH)

        def to_vmem(p):
            cols = pl.ds(pl.multiple_of(_chip_visited(p) * tn, tn), tn)
            return pltpu.make_async_copy(wg_hbm.at[:, cols], w_vmem.at[p % 2], vmem_sems.at[p % 2])

        mine = pltpu.make_async_copy(wsh_hbm, block(me), local_sem)
        first = [copy(0, me, sibling, src=wsh_hbm)] + [copy(1 + j, me, (*chip, c), src=wsh_hbm)
                                                       for j, chip in enumerate(chips)]
        passed = [copy(4 + j, (*chip, c), sibling) for j, chip in enumerate(chips)]

        @pl.when((q == 0) & (i == 0))
        def _():
            mine.start()
            for cp in first:
                cp.start()
            ex.start(ex_ins, ex_outs, ex_sems)
            mine.wait()
            copy(0, sibling, me).wait_recv()
            to_vmem(0).start()
            to_vmem(0).wait()

        for p in range(1, n_q):
            chip = chips[p - 1]

            @pl.when((q == p - 1) & (i == n_i - 2))
            def _():
                copy(p, (*chip, c), me).wait_recv()
                passed[p - 1].start()

            @pl.when((q == p - 1) & (i == n_i - 1))
            def _():
                copy(3 + p, (*chip, 1 - c), me).wait_recv()
                to_vmem(p).start()

            @pl.when((q == p) & (i == 0))
            def _():
                to_vmem(p).wait()

        proj_ref[...] = _dot(hn_ref[...], w_vmem[q % 2]).astype(BF16)

        @pl.when((q == n_q - 1) & (i == n_i - 1))
        def _():
            for cp in first + passed:
                cp.wait_send()
            ex.wait(ex_ins, ex_outs, ex_sems)

    hbm = pl.BlockSpec(memory_space=pl.ANY)
    outs = pl.pallas_call(
        body, name="in_proj",
        grid=(n_q, n_i),
        in_specs=[pl.BlockSpec((tm, D), lambda q, i: (i, 0)), hbm] + ex.specs,
        out_specs=[pl.BlockSpec((tm, tn), lambda q, i: (i, _chip_visited(q))), hbm] + ex.specs,
        out_shape=[jax.ShapeDtypeStruct((R, E), BF16), jax.ShapeDtypeStruct((D, E), BF16)] + ex.out_shape,
        scratch_shapes=[pltpu.VMEM((2, D, tn), BF16), pltpu.SemaphoreType.DMA((7,)), pltpu.SemaphoreType.DMA((7,)),
                        pltpu.SemaphoreType.DMA, pltpu.SemaphoreType.DMA((2,))] + ex.scratch,
        compiler_params=_params(("arbitrary", "arbitrary"), 48),
    )(hn, w_shard, *ex_parts)
    return outs[0], outs[1], outs[2:]


def _rope_chunk(cb_ref, sb_ref, ci_ref, si_ref):
    cb, sb, ci, si = cb_ref[0], sb_ref[0], ci_ref[...], si_ref[...]
    return cb * ci - sb * si, sb * ci + cb * si


def _rot(t, cos, sin, half):
    t1, t2 = t[:, :half], t[:, half:]
    return jnp.concatenate([t1 * cos - t2 * sin, t1 * sin + t2 * cos], axis=-1)


def _rot_inv(t, cos, sin, half):
    t1, t2 = t[:, :half], t[:, half:]
    return jnp.concatenate([t1 * cos + t2 * sin, t2 * cos - t1 * sin], axis=-1)


def _chunk_order(n_chunks):
    lead = META_TILE // CHUNK
    return lambda l: (l + n_chunks - lead) % n_chunks


def _retention_fwd(proj, rope, gn_g, mix):
    R, E = proj.shape
    RW = gn_g.shape[1]
    H = RET_HEADS
    hd = RW // H
    half = hd // 2
    NC = R // CHUNK
    mask, qd, kd, cd = _decay_tables(H)
    scale = float(hd) ** -0.5
    phys = _chunk_order(NC)

    def body(p_ref, cb_ref, sb_ref, ci_ref, si_ref, mask_ref, qd_ref, kd_ref, gn_ref, y_ref, st_ref, state):
        @pl.when(pl.program_id(0) == 0)
        def _():
            state[...] = jnp.zeros_like(state)

        cs, sn = _rope_chunk(cb_ref, sb_ref, ci_ref, si_ref)
        hs = range(H)
        col = lambda j, h: slice(j * RW + h * hd, j * RW + (h + 1) * hd)
        qr = [_rot(p_ref[:, col(0, h)].astype(F32), cs, sn, half) for h in hs]
        kr = [_rot(p_ref[:, col(1, h)].astype(F32), cs, sn, half) * scale for h in hs]
        v = [p_ref[:, col(2, h)] for h in hs]
        s_prev = [state[h] for h in hs]
        s_prev_b = [s_prev[h].astype(BF16) for h in hs]
        s = [(_dot(qr[h].astype(BF16), kr[h].astype(BF16), NT) * mask_ref[h]).astype(BF16) for h in hs]
        y_raw = [_dot(s[h], v[h]) + _dot((qr[h] * qd_ref[h]).astype(BF16), s_prev_b[h]) for h in hs]
        s_new = [s_prev[h] * cd[h] + _dot((kr[h] * kd_ref[h]).astype(BF16), v[h], TN) for h in hs]
        for h in hs:
            st_ref[0, h] = s_prev_b[h]
            state[h] = s_new[h]
        for h in hs:
            g = p_ref[:, col(3, h)].astype(F32)
            mu = jnp.mean(y_raw[h], axis=-1, keepdims=True)
            yc = y_raw[h] - mu
            var = jnp.mean(yc * yc, axis=-1, keepdims=True)
            out = yc * lax.rsqrt(var + EPS) * gn_ref[:, col(0, h)] * (g * _sigmoid(g))
            y_ref[:, col(0, h)] = out.astype(BF16)

    const3 = lambda l: (0, 0, 0)
    return pl.pallas_call(
        body, name="retention_fwd",
        grid=(NC,),
        in_specs=[pl.BlockSpec((CHUNK, 4 * RW), lambda l: (phys(l), 0)),
                  pl.BlockSpec((1, 1, half), lambda l: (phys(l), 0, 0)),
                  pl.BlockSpec((1, 1, half), lambda l: (phys(l), 0, 0)),
                  pl.BlockSpec((CHUNK, half), lambda l: (0, 0)),
                  pl.BlockSpec((CHUNK, half), lambda l: (0, 0)),
                  pl.BlockSpec((H, CHUNK, CHUNK), const3),
                  pl.BlockSpec((H, CHUNK, 1), const3),
                  pl.BlockSpec((H, CHUNK, 1), const3),
                  pl.BlockSpec((1, RW), lambda l: (0, 0))],
        out_specs=[pl.BlockSpec((CHUNK, RW), lambda l: (phys(l), 0)),
                   pl.BlockSpec((1, H, hd, hd), lambda l: (phys(l), 0, 0, 0))],
        out_shape=[jax.ShapeDtypeStruct((R, mix), BF16), jax.ShapeDtypeStruct((NC, H, hd, hd), BF16)],
        scratch_shapes=[pltpu.VMEM((H, hd, hd), F32)],
        compiler_params=_params(("arbitrary",), 32),
    )(proj, *rope, jnp.asarray(mask), jnp.asarray(qd), jnp.asarray(kd), gn_g)


CONV_ROWS = 64
CONV_LANES = 128
LANE = 128
ELEM_ROWS = 32


def _conv_order(n_tiles):
    return lambda l: (l + n_tiles - 1) % n_tiles


def _halo_block(n_tiles, tm):
    per = tm // HALO
    return lambda l: ((l + n_tiles - 2) % n_tiles) * per + per - 1


def _fill_shifted(src, dst):
    rows, width = dst.shape[1], dst.shape[2]
    step = _pick_tile(rows, 64, 8)
    for r in range(1, 8):
        for r0 in range(0, rows, step):
            for l0 in range(0, width, CONV_LANES):
                dst[r - 1, r0:r0 + step, l0:l0 + CONV_LANES] = src[r + r0:r + r0 + step, l0:l0 + CONV_LANES]


def _at_offset(src, shifted, off, r0, rows, lanes):
    r = off % 8
    a = off - r + r0
    if r == 0:
        return src[a:a + rows, lanes]
    return shifted[r - 1, a:a + rows, lanes]


def _fill_glu(first, a_ref, b_ref, ah_ref, bh_ref, u_ext, tm):
    uh = ah_ref[...].astype(F32) * _sigmoid(bh_ref[...].astype(F32))
    u_ext[0:HALO, :] = jnp.where(first, 0.0, uh)
    for r0 in range(0, tm, ELEM_ROWS):
        rows = slice(r0, r0 + ELEM_ROWS)
        u_ext[HALO + r0:HALO + r0 + ELEM_ROWS, :] = a_ref[rows, :].astype(F32) * _sigmoid(b_ref[rows, :].astype(F32))


def _layer_norm(cv, lg_ref, lb_ref):
    mu = jnp.mean(cv, axis=-1, keepdims=True)
    cc = cv - mu
    rstd = lax.rsqrt(jnp.mean(cc * cc, axis=-1, keepdims=True) + EPS)
    xh = cc * rstd
    return xh, rstd, xh * lg_ref[...] + lb_ref[...]


def _conv_fwd(proj, y_in, dw_w, dw_b, ln_g, ln_b, pw_w, pw_b, ex, ex_parts):
    R, E = proj.shape
    CW = pw_w.shape[0]
    tm = META_TILE
    NTL = R // tm
    phys = _conv_order(NTL)
    halo = _halo_block(NTL, tm)
    cb = (E - 3 * CW) // CW
    base = HALO - (CONV_K - 1)

    def body(*refs):
        a_ref, b_ref, g_ref, ah_ref, bh_ref, w_ref, wb_ref, lg_ref, lb_ref, pw_ref, pb_ref, yin_ref = refs[:12]
        ex_ins = refs[12:12 + ex.n]
        y_ref, c_ref = refs[12 + ex.n:14 + ex.n]
        ex_outs = refs[14 + ex.n:14 + 2 * ex.n]
        u_ext, u_sh, s_scr, upw_scr = refs[14 + 2 * ex.n:18 + 2 * ex.n]
        sems = refs[18 + 2 * ex.n:]

        @pl.when(pl.program_id(0) == 0)
        def _():
            ex.start(ex_ins, ex_outs, sems)

        _fill_glu(pl.program_id(0) == 0, a_ref, b_ref, ah_ref, bh_ref, u_ext, tm)
        _fill_shifted(u_ext, u_sh)
        for r0 in range(0, tm, CONV_ROWS):
            for l0 in range(0, CW, CONV_LANES):
                lanes = slice(l0, l0 + CONV_LANES)
                acc = None
                for k in range(CONV_K):
                    term = _at_offset(u_ext, u_sh, base + k, r0, CONV_ROWS, lanes) * w_ref[k:k + 1, lanes]
                    acc = term if acc is None else acc + term
                c_ref[r0:r0 + CONV_ROWS, lanes] = acc + wb_ref[:, lanes]
        blocks = [slice(r0, r0 + ELEM_ROWS) for r0 in range(0, tm, ELEM_ROWS)]
        for rows in blocks:
            _, _, ln = _layer_norm(c_ref[rows, :], lg_ref, lb_ref)
            s_scr[rows, :] = (ln * _sigmoid(ln)).astype(BF16)
        upw_scr[...] = _dot(s_scr[...], pw_ref[...]) + pb_ref[...]
        for rows in blocks:
            g = g_ref[rows, :].astype(F32)
            y_ref[rows, :] = (upw_scr[rows, :] * (g * _sigmoid(g))).astype(BF16)

        @pl.when(pl.program_id(0) == NTL - 1)
        def _():
            ex.wait(ex_ins, ex_outs, sems)

    row = lambda l: (0, 0)
    outs = pl.pallas_call(
        body, name="conv_fwd",
        grid=(NTL,),
        in_specs=[pl.BlockSpec((tm, CW), lambda l: (phys(l), cb)),
                  pl.BlockSpec((tm, CW), lambda l: (phys(l), cb + 1)),
                  pl.BlockSpec((tm, CW), lambda l: (phys(l), cb + 2)),
                  pl.BlockSpec((HALO, CW), lambda l: (halo(l), cb)),
                  pl.BlockSpec((HALO, CW), lambda l: (halo(l), cb + 1)),
                  pl.BlockSpec((HALO, CW), row),
                  pl.BlockSpec((1, CW), row), pl.BlockSpec((1, CW), row), pl.BlockSpec((1, CW), row),
                  pl.BlockSpec((CW, CW), row),
                  pl.BlockSpec((1, CW), row),
                  pl.BlockSpec(memory_space=pl.ANY)] + ex.specs,
        out_specs=[pl.BlockSpec((tm, CW), lambda l: (phys(l), 1)),
                   pl.BlockSpec((tm, CW), lambda l: (phys(l), 0))] + ex.specs,
        out_shape=[jax.ShapeDtypeStruct(y_in.shape, BF16), jax.ShapeDtypeStruct((R, CW), F32)] + ex.out_shape,
        input_output_aliases={11: 0},
        scratch_shapes=[pltpu.VMEM((HALO + tm, CW), F32), pltpu.VMEM((7, tm + HALO - 8, CW), F32),
                        pltpu.VMEM((tm, CW), BF16), pltpu.VMEM((tm, CW), F32)] + ex.scratch,
        compiler_params=_params(("arbitrary",), 48),
    )(proj, proj, proj, proj, proj, dw_w, dw_b, ln_g, ln_b, pw_w, pw_b, y_in, *ex_parts)
    return outs[0], outs[1], outs[2:]


def _out_proj_loss(xs, meta_tile, y, w_out, final_g, target):
    SEQ, D = xs.shape
    R, MIX = y.shape
    tm = META_TILE
    n_seq = SEQ // tm
    n_tiles = R // tm
    rows_out = _pick_tile(MIX, 256)

    def body(x_ref, mt_ref, y_ref, w_hbm, fg_ref, t_ref, dh2_ref, dy_ref, dwo_hbm, dfg_ref, loss_ref,
             w_scr, acc, stage, sem):
        i = pl.program_id(0)

        @pl.when(i == 0)
        def _():
            cp = pltpu.make_async_copy(w_hbm, w_scr, sem)
            cp.start()
            acc[...] = jnp.zeros_like(acc)
            dfg_ref[...] = jnp.zeros_like(dfg_ref)
            loss_ref[...] = jnp.zeros_like(loss_ref)
            cp.wait()

        yb = y_ref[...]
        h2 = jnp.where(i < n_seq, x_ref[...], mt_ref[...]) + _dot(yb, w_scr[...])
        r2 = lax.rsqrt(jnp.mean(h2 * h2, axis=-1, keepdims=True) + EPS)
        n = h2 * r2
        fg = fg_ref[...]
        err = jnp.where(i < n_seq, n * fg - t_ref[...], 0.0)
        loss_ref[...] += 0.5 * jnp.sum(jnp.mean(err * err, axis=-1, keepdims=True), axis=0, keepdims=True)
        dout = err * (1.0 / D)
        dfg_ref[...] += jnp.sum(dout * n, axis=0, keepdims=True)
        dn = dout * fg
        dh2 = r2 * (dn - n * jnp.mean(dn * n, axis=-1, keepdims=True))
        dh2_ref[...] = dh2
        dh2b = dh2.astype(BF16)
        dy_ref[...] = _dot(dh2b, w_scr[...], NT).astype(BF16)
        acc[...] += _dot(yb, dh2b, TN)

        @pl.when(i == n_tiles - 1)
        def _():
            for r in range(0, MIX, rows_out):
                stage[...] = acc[r:r + rows_out, :].astype(BF16)
                cp = pltpu.make_async_copy(stage, dwo_hbm.at[r:r + rows_out, :], sem)
                cp.start()
                cp.wait()

    row = lambda i: (0, 0)
    return pl.pallas_call(
        body, name="out_proj_loss",
        grid=(n_tiles,),
        in_specs=[pl.BlockSpec((tm, D), lambda i: (jnp.minimum(i, n_seq - 1), 0)),
                  pl.BlockSpec((tm, D), row),
                  pl.BlockSpec((tm, MIX), lambda i: (i, 0)),
                  pl.BlockSpec(memory_space=pl.ANY),
                  pl.BlockSpec((1, D), row),
                  pl.BlockSpec((tm, D), lambda i: (jnp.minimum(i, n_seq - 1), 0))],
        out_specs=[pl.BlockSpec((tm, D), lambda i: (i, 0)),
                   pl.BlockSpec((tm, MIX), lambda i: (i, 0)),
                   pl.BlockSpec(memory_space=pl.ANY),
                   pl.BlockSpec((1, D), row),
                   pl.BlockSpec((1, 1), row)],
        out_shape=[jax.ShapeDtypeStruct((R, D), F32), jax.ShapeDtypeStruct((R, MIX), BF16),
                   jax.ShapeDtypeStruct((MIX, D), BF16), jax.ShapeDtypeStruct((1, D), F32),
                   jax.ShapeDtypeStruct((1, 1), F32)],
        scratch_shapes=[pltpu.VMEM((MIX, D), BF16), pltpu.VMEM((MIX, D), F32), pltpu.VMEM((rows_out, D), BF16),
                        pltpu.SemaphoreType.DMA],
        compiler_params=_params(("arbitrary",), 60),
    )(xs, meta_tile, y, w_out, final_g, target)


def _conv_bwd(proj, conv_out, dy, dw_w, ln_g, ln_b, pw_w, pw_b):
    R, E = proj.shape
    CW = pw_w.shape[0]
    tm = META_TILE
    NTL = R // tm
    order = _conv_order(NTL)
    phys = lambda i: order(NTL - 1 - i)
    halo_l = _halo_block(NTL, tm)
    halo = lambda i: halo_l(NTL - 1 - i)
    cb = (E - 3 * CW) // CW
    base = HALO - (CONV_K - 1)

    def body(a_ref, b_ref, g_ref, ah_ref, bh_ref, c_ref, dy_ref, w_ref, lg_ref, lb_ref, pw_ref, pb_ref,
             dp_ref, dpw_ref, dww_ref, vec_ref, u_ext, u_sh, dc_ext, dc_sh, du_scr, dww_acc, dpw_acc,
             xh_scr, rstd_scr, ln_scr, sg_scr, upw_scr, s_scr, dupw_scr):
        i = pl.program_id(0)

        @pl.when(i == 0)
        def _():
            dpw_acc[...] = jnp.zeros_like(dpw_acc)
            dww_ref[...] = jnp.zeros_like(dww_ref)
            vec_ref[...] = jnp.zeros_like(vec_ref)
            dww_acc[...] = jnp.zeros_like(dww_acc)
            dc_ext[tm:tm + HALO, :] = jnp.zeros((HALO, CW), F32)

        _fill_glu(i == NTL - 1, a_ref, b_ref, ah_ref, bh_ref, u_ext, tm)
        _fill_shifted(u_ext, u_sh)
        blocks = [slice(r0, r0 + ELEM_ROWS) for r0 in range(0, tm, ELEM_ROWS)]
        for rows in blocks:
            xh, rstd, ln = _layer_norm(c_ref[rows, :], lg_ref, lb_ref)
            sg = _sigmoid(ln)
            xh_scr[rows, :], rstd_scr[rows, :], ln_scr[rows, :], sg_scr[rows, :] = xh, rstd, ln, sg
            s_scr[rows, :] = (ln * sg).astype(BF16)
        upw_scr[...] = _dot(s_scr[...], pw_ref[...]) + pb_ref[...]
        col_sum = jnp.zeros((1, CW), F32)
        for rows in blocks:
            g = g_ref[rows, :].astype(F32)
            sgg = _sigmoid(g)
            dyc = dy_ref[rows, :].astype(F32)
            dp_ref[rows, 2 * CW:3 * CW] = (dyc * upw_scr[rows, :] * _dsilu(g, sgg)).astype(BF16)
            dupw = dyc * (g * sgg)
            dupw_scr[rows, :] = dupw.astype(BF16)
            col_sum = col_sum + jnp.sum(dupw, axis=0, keepdims=True)
        vec_ref[0:1, :] += col_sum
        dpw_acc[...] += _dot(s_scr[...], dupw_scr[...], TN)
        upw_scr[...] = _dot(dupw_scr[...], pw_ref[...], NT)
        sum_g, sum_b, sum_c = col_sum * 0.0, col_sum * 0.0, col_sum * 0.0
        for rows in blocks:
            xh, rstd = xh_scr[rows, :], rstd_scr[rows, :]
            dln = upw_scr[rows, :] * _dsilu(ln_scr[rows, :], sg_scr[rows, :])
            sum_g = sum_g + jnp.sum(dln * xh, axis=0, keepdims=True)
            sum_b = sum_b + jnp.sum(dln, axis=0, keepdims=True)
            dxh = dln * lg_ref[...]
            dc = rstd * (dxh - jnp.mean(dxh, axis=-1, keepdims=True)
                         - xh * jnp.mean(dxh * xh, axis=-1, keepdims=True))
            sum_c = sum_c + jnp.sum(dc, axis=0, keepdims=True)
            dc_ext[rows, :] = dc
        vec_ref[1:2, :] += sum_g
        vec_ref[2:3, :] += sum_b
        vec_ref[3:4, :] += sum_c
        _fill_shifted(dc_ext, dc_sh)

        for l0 in range(0, CW, CONV_LANES):
            lanes = slice(l0, l0 + CONV_LANES)
            for r0 in range(0, tm, CONV_ROWS):
                acc = None
                for k in range(CONV_K):
                    term = _at_offset(dc_ext, dc_sh, CONV_K - 1 - k, r0, CONV_ROWS, lanes) * w_ref[k:k + 1, lanes]
                    acc = term if acc is None else acc + term
                du_scr[r0:r0 + CONV_ROWS, lanes] = acc

        n_grp = tm // 8
        by_shift = [[(k, (base + k) // 8) for k in range(CONV_K) if (base + k) % 8 == r] for r in range(8)]
        for l0 in range(0, CW, LANE):
            lane = slice(l0, l0 + LANE)
            for r in range(8):
                src = u_ext if r == 0 else u_sh.at[r - 1]
                a_lo, a_hi = by_shift[r][0][1], by_shift[r][-1][1]
                sums = {k: None for k, _ in by_shift[r]}
                dcg = {}
                for gi in range(a_lo, n_grp + a_hi):
                    if gi - a_lo < n_grp:
                        dcg[gi - a_lo] = dc_ext[8 * (gi - a_lo):8 * (gi - a_lo) + 8, lane]
                    dcg.pop(gi - a_hi - 1, None)
                    ug = src[8 * gi:8 * gi + 8, lane]
                    for k, a in by_shift[r]:
                        if 0 <= gi - a < n_grp:
                            prod = dcg[gi - a] * ug
                            sums[k] = prod if sums[k] is None else sums[k] + prod
                for k, _ in by_shift[r]:
                    dww_acc[k, :, lane] += sums[k]

        for rows in blocks:
            du = du_scr[rows, :]
            sgb = _sigmoid(b_ref[rows, :].astype(F32))
            dp_ref[rows, 0:CW] = (du * sgb).astype(BF16)
            dp_ref[rows, CW:2 * CW] = (du * a_ref[rows, :].astype(F32) * sgb * (1.0 - sgb)).astype(BF16)
        dc_ext[tm:tm + HALO, :] = dc_ext[0:HALO, :]

        @pl.when(i == NTL - 1)
        def _():
            for k in range(CONV_K):
                dww_ref[k:k + 1, :] = jnp.sum(dww_acc[k], axis=0, keepdims=True)
            dpw_ref[...] = dpw_acc[...].astype(BF16)

    row = lambda i: (0, 0)
    return pl.pallas_call(
        body, name="conv_bwd",
        grid=(NTL,),
        in_specs=[pl.BlockSpec((tm, CW), lambda i: (phys(i), cb)),
                  pl.BlockSpec((tm, CW), lambda i: (phys(i), cb + 1)),
                  pl.BlockSpec((tm, CW), lambda i: (phys(i), cb + 2)),
                  pl.BlockSpec((HALO, CW), lambda i: (halo(i), cb)),
                  pl.BlockSpec((HALO, CW), lambda i: (halo(i), cb + 1)),
                  pl.BlockSpec((tm, CW), lambda i: (phys(i), 0)),
                  pl.BlockSpec((tm, CW), lambda i: (phys(i), 1)),
                  pl.BlockSpec((HALO, CW), row),
                  pl.BlockSpec((1, CW), row), pl.BlockSpec((1, CW), row),
                  pl.BlockSpec((CW, CW), row),
                  pl.BlockSpec((1, CW), row)],
        out_specs=[pl.BlockSpec((tm, 3 * CW), lambda i: (phys(i), 0)),
                   pl.BlockSpec((CW, CW), row),
                   pl.BlockSpec((HALO, CW), row),
                   pl.BlockSpec((8, CW), row)],
        out_shape=[jax.ShapeDtypeStruct((R, 3 * CW), BF16), jax.ShapeDtypeStruct((CW, CW), BF16),
                   jax.ShapeDtypeStruct((HALO, CW), F32), jax.ShapeDtypeStruct((8, CW), F32)],
        scratch_shapes=[pltpu.VMEM((HALO + tm, CW), F32), pltpu.VMEM((7, tm + HALO - 8, CW), F32),
                        pltpu.VMEM((tm + HALO, CW), F32), pltpu.VMEM((7, tm + HALO - 8, CW), F32),
                        pltpu.VMEM((tm, CW), F32), pltpu.VMEM((CONV_K, 8, CW), F32), pltpu.VMEM((CW, CW), F32),
                        pltpu.VMEM((tm, CW), F32), pltpu.VMEM((tm, 1), F32), pltpu.VMEM((tm, CW), F32),
                        pltpu.VMEM((tm, CW), F32), pltpu.VMEM((tm, CW), F32), pltpu.VMEM((tm, CW), BF16),
                        pltpu.VMEM((tm, CW), BF16)],
        compiler_params=_params(("arbitrary",), 60),
    )(proj, proj, proj, proj, proj, conv_out, dy, dw_w, ln_g, ln_b, pw_w, pw_b)


def _retention_bwd(proj, rope, gn_g, states, dy, ex, ex_parts):
    R, E = proj.shape
    RW = gn_g.shape[1]
    H = RET_HEADS
    hd = RW // H
    half = hd // 2
    NC = R // CHUNK
    mask, qd, kd, cd = _decay_tables(H)
    scale = float(hd) ** -0.5
    order = _chunk_order(NC)
    phys = lambda i: order(NC - 1 - i)

    def body(*refs):
        p_ref, cb_ref, sb_ref, ci_ref, si_ref, mask_ref, qd_ref, kd_ref, gn_ref, st_ref, dy_ref = refs[:11]
        ex_ins = refs[11:11 + ex.n]
        dp_ref, dgn_ref = refs[11 + ex.n:13 + ex.n]
        ex_outs = refs[13 + ex.n:13 + 2 * ex.n]
        dstate = refs[13 + 2 * ex.n]
        sems = refs[14 + 2 * ex.n:]

        @pl.when(pl.program_id(0) == 0)
        def _():
            ex.start(ex_ins, ex_outs, sems)
            dstate[...] = jnp.zeros_like(dstate)
            dgn_ref[...] = jnp.zeros_like(dgn_ref)

        cs, sn = _rope_chunk(cb_ref, sb_ref, ci_ref, si_ref)
        hs = range(H)
        col = lambda j, h: slice(j * RW + h * hd, j * RW + (h + 1) * hd)
        qr = [_rot(p_ref[:, col(0, h)].astype(F32), cs, sn, half) for h in hs]
        kr = [_rot(p_ref[:, col(1, h)].astype(F32), cs, sn, half) * scale for h in hs]
        v = [p_ref[:, col(2, h)] for h in hs]
        qb = [qr[h].astype(BF16) for h in hs]
        kb = [kr[h].astype(BF16) for h in hs]
        qdb = [(qr[h] * qd_ref[h]).astype(BF16) for h in hs]
        kdb = [(kr[h] * kd_ref[h]).astype(BF16) for h in hs]
        s_prev = [st_ref[0, h] for h in hs]
        dst = [dstate[h] for h in hs]
        dstb = [dst[h].astype(BF16) for h in hs]
        sb = [(_dot(qb[h], kb[h], NT) * mask_ref[h]).astype(BF16) for h in hs]
        y_raw = [_dot(sb[h], v[h]) + _dot(qdb[h], s_prev[h]) for h in hs]
        dyrb, dg = [], []
        for h in hs:
            g = p_ref[:, col(3, h)].astype(F32)
            mu = jnp.mean(y_raw[h], axis=-1, keepdims=True)
            yc = y_raw[h] - mu
            rstd = lax.rsqrt(jnp.mean(yc * yc, axis=-1, keepdims=True) + EPS)
            xh = yc * rstd
            gn = gn_ref[:, col(0, h)]
            sg = _sigmoid(g)
            dyh = dy_ref[:, col(0, h)].astype(F32)
            dg.append((dyh * (xh * gn) * _dsilu(g, sg)).astype(BF16))
            dyn = dyh * (g * sg)
            dgn_ref[:, col(0, h)] += jnp.sum(dyn * xh, axis=0, keepdims=True)
            dxh = dyn * gn
            dyr = rstd * (dxh - jnp.mean(dxh, axis=-1, keepdims=True)
                          - xh * jnp.mean(dxh * xh, axis=-1, keepdims=True))
            dyrb.append(dyr.astype(BF16))
        dsb = [(_dot(dyrb[h], v[h], NT) * mask_ref[h]).astype(BF16) for h in hs]
        dqr = [_dot(dsb[h], kb[h]) + _dot(dyrb[h], s_prev[h], NT) * qd_ref[h] for h in hs]
        dkr = [_dot(dsb[h], qb[h], TN) + _dot(v[h], dstb[h], NT) * kd_ref[h] for h in hs]
        dv = [_dot(sb[h], dyrb[h], TN) + _dot(kdb[h], dstb[h]) for h in hs]
        dst_new = [dst[h] * cd[h] + _dot(qdb[h], dyrb[h], TN) for h in hs]
        for h in hs:
            dstate[h] = dst_new[h]
            dp_ref[:, col(0, h)] = _rot_inv(dqr[h], cs, sn, half).astype(BF16)
            dp_ref[:, col(1, h)] = (_rot_inv(dkr[h], cs, sn, half) * scale).astype(BF16)
            dp_ref[:, col(2, h)] = dv[h].astype(BF16)
            dp_ref[:, col(3, h)] = dg[h]

        @pl.when(pl.program_id(0) == NC - 1)
        def _():
            ex.wait(ex_ins, ex_outs, sems)

    const3 = lambda i: (0, 0, 0)
    outs = pl.pallas_call(
        body, name="retention_bwd",
        grid=(NC,),
        in_specs=[pl.BlockSpec((CHUNK, 4 * RW), lambda i: (phys(i), 0)),
                  pl.BlockSpec((1, 1, half), lambda i: (phys(i), 0, 0)),
                  pl.BlockSpec((1, 1, half), lambda i: (phys(i), 0, 0)),
                  pl.BlockSpec((CHUNK, half), lambda i: (0, 0)),
                  pl.BlockSpec((CHUNK, half), lambda i: (0, 0)),
                  pl.BlockSpec((H, CHUNK, CHUNK), const3),
                  pl.BlockSpec((H, CHUNK, 1), const3),
                  pl.BlockSpec((H, CHUNK, 1), const3),
                  pl.BlockSpec((1, RW), lambda i: (0, 0)),
                  pl.BlockSpec((1, H, hd, hd), lambda i: (phys(i), 0, 0, 0)),
                  pl.BlockSpec((CHUNK, RW), lambda i: (phys(i), 0))] + ex.specs,
        out_specs=[pl.BlockSpec((CHUNK, 4 * RW), lambda i: (phys(i), 0)),
                   pl.BlockSpec((1, RW), lambda i: (0, 0))] + ex.specs,
        out_shape=[jax.ShapeDtypeStruct((R, 4 * RW), BF16), jax.ShapeDtypeStruct((1, RW), F32)] + ex.out_shape,
        scratch_shapes=[pltpu.VMEM((H, hd, hd), F32)] + ex.scratch,
        compiler_params=_params(("arbitrary",), 32),
    )(proj, *rope, jnp.asarray(mask), jnp.asarray(qd), jnp.asarray(kd), gn_g, states, dy, *ex_parts)
    return outs[0], outs[1], outs[2:]


def _dproj_specs(tk, tn, n_ret, tile_axis, col_axis):
    def ret_map(*ids):
        t, j = ids[tile_axis], ids[col_axis]
        return (jnp.where(j < n_ret, t, 0), jnp.minimum(j, n_ret - 1))

    def conv_map(*ids):
        t, j = ids[tile_axis], ids[col_axis]
        return (jnp.where(j >= n_ret, t, 0), jnp.maximum(j - n_ret, 0))

    return pl.BlockSpec((tk, tn), ret_map), pl.BlockSpec((tk, tn), conv_map)


def _w_in_grad(hn, dp_ret, dp_conv):
    R, D = hn.shape
    tn = _pick_tile(dp_conv.shape[1] // 3, 1024, 128)
    n_ret, n_conv = dp_ret.shape[1] // tn, dp_conv.shape[1] // tn
    E = dp_ret.shape[1] + dp_conv.shape[1]
    tk = _pick_tile(R, 1024, MXU_DIM)
    n_t = R // tk
    ret_spec, conv_spec = _dproj_specs(tk, tn, n_ret, 1, 0)

    def body(hn_ref, r_ref, c_ref, out_ref, acc):
        j, t = pl.program_id(0), pl.program_id(1)

        @pl.when(t == 0)
        def _():
            acc[...] = jnp.zeros_like(acc)

        @pl.when(j < n_ret)
        def _():
            acc[...] += _dot(hn_ref[...], r_ref[...], TN)

        @pl.when(j >= n_ret)
        def _():
            acc[...] += _dot(hn_ref[...], c_ref[...], TN)

        @pl.when(t == n_t - 1)
        def _():
            out_ref[...] = acc[...].astype(BF16)

    return pl.pallas_call(
        body, name="w_in_grad",
        grid=(n_ret + n_conv, n_t),
        in_specs=[pl.BlockSpec((tk, D), lambda j, t: (t, 0)), ret_spec, conv_spec],
        out_specs=pl.BlockSpec((D, tn), lambda j, t: (0, j)),
        out_shape=jax.ShapeDtypeStruct((D, E), BF16),
        scratch_shapes=[pltpu.VMEM((D, tn), F32)],
        compiler_params=_params(("arbitrary", "arbitrary"), 48),
    )(hn, dp_ret, dp_conv)


def _h_grad(dp_ret, dp_conv, w_in, xs, meta_tile, dh2, ln_g, ex, ex_parts):
    R, D = dh2.shape
    te = CHUNK
    n_x = xs.shape[0] // te
    n_m = meta_tile.shape[0] // te
    tn = _pick_tile(dp_conv.shape[1] // 3, 1024, 128)
    n_ret, n_conv = dp_ret.shape[1] // tn, dp_conv.shape[1] // tn
    n_k = n_ret + n_conv
    tm = _pick_tile(R, 1024, meta_tile.shape[0])
    n_e = tm // te
    n_t = R // tm
    assert n_e <= n_k and (n_x + n_m) * te == R

    def ret_map(t, k):
        return (jnp.where(k < n_ret, jnp.minimum(t, n_t - 1), 0), jnp.minimum(k, n_ret - 1))

    def conv_map(t, k):
        return (jnp.where(k >= n_ret, jnp.minimum(t, n_t - 1), 0), jnp.maximum(k - n_ret, 0))

    def row_block(t, k):
        return jnp.maximum(t - 1, 0) * n_e + jnp.where(t > 0, jnp.minimum(k, n_e - 1), 0)

    def body(*refs):
        r_ref, c_ref, w_hbm, w_ref, x_ref, mt_ref, dh2_ref, g_ref = refs[:8]
        ex_ins = refs[8:8 + ex.n]
        o = 8 + ex.n
        dh_ref, dlg_ref = refs[o:o + 2]
        ex_outs = refs[o + 2:o + 2 + ex.n]
        acc, w_keep, keep_sems = refs[o + 2 + ex.n:o + 5 + ex.n]
        sems = refs[o + 5 + ex.n:]
        t, k = pl.program_id(0), pl.program_id(1)
        cur, old = t % 2, (t + 1) % 2

        def keep(j):
            return pltpu.make_async_copy(w_hbm.at[:, j * tn:(j + 1) * tn], w_keep.at[j], keep_sems.at[j])

        @pl.when((k == 0) & (t == 0))
        def _():
            for j in range(n_ret):
                keep(j).start()
            ex.start(ex_ins, ex_outs, sems)
            dlg_ref[...] = jnp.zeros_like(dlg_ref)

        for j in range(n_ret):
            @pl.when((k == j) & (t == 0))
            def _():
                keep(j).wait()

        @pl.when((k == 0) & (t < n_t))
        def _():
            acc[cur] = _dot(r_ref[...], w_keep[0], NT)

        @pl.when((k > 0) & (k < n_ret) & (t < n_t))
        def _():
            acc[cur] += _dot(r_ref[...], w_keep[k], NT)

        @pl.when((k >= n_ret) & (t < n_t))
        def _():
            acc[cur] += _dot(c_ref[...], w_ref[...], NT)

        @pl.when((k < n_e) & (t > 0))
        def _():
            hv = jnp.where(row_block(t, k) < n_x, x_ref[...], mt_ref[...])
            r = lax.rsqrt(jnp.mean(hv * hv, axis=-1, keepdims=True) + EPS)
            nrm = hv * r
            dhn = acc[old, pl.ds(pl.multiple_of(k * te, te), te), :]
            dlg_ref[...] += jnp.sum(dhn * nrm, axis=0, keepdims=True)
            dn = dhn * g_ref[...]
            dh_ref[...] = dh2_ref[...] + r * (dn - nrm * jnp.mean(dn * nrm, axis=-1, keepdims=True))

        @pl.when((k == n_k - 1) & (t == n_t))
        def _():
            ex.wait(ex_ins, ex_outs, sems)

    row = lambda t, k: (0, 0)
    outs = pl.pallas_call(
        body, name="h_grad",
        grid=(n_t + 1, n_k),
        in_specs=[pl.BlockSpec((tm, tn), ret_map), pl.BlockSpec((tm, tn), conv_map),
                  pl.BlockSpec(memory_space=pl.ANY),
                  pl.BlockSpec((D, tn), lambda t, k: (0, jnp.maximum(k, n_ret))),
                  pl.BlockSpec((te, D), lambda t, k: (jnp.minimum(row_block(t, k), n_x - 1), 0)),
                  pl.BlockSpec((te, D), lambda t, k: (jnp.clip(row_block(t, k) - n_x, 0, n_m - 1), 0)),
                  pl.BlockSpec((te, D), lambda t, k: (row_block(t, k), 0)),
                  pl.BlockSpec((1, D), row)] + ex.specs,
        out_specs=[pl.BlockSpec((te, D), lambda t, k: (row_block(t, k), 0)),
                   pl.BlockSpec((1, D), row)] + ex.specs,
        out_shape=[jax.ShapeDtypeStruct((R, D), F32), jax.ShapeDtypeStruct((1, D), F32)] + ex.out_shape,
        scratch_shapes=[pltpu.VMEM((2, tm, D), F32), pltpu.VMEM((n_ret, D, tn), BF16),
                        pltpu.SemaphoreType.DMA((n_ret,))] + ex.scratch,
        compiler_params=_params(("arbitrary", "arbitrary"), 60),
    )(dp_ret, dp_conv, w_in, w_in, xs, meta_tile, dh2, ln_g, *ex_parts)
    return outs[0], outs[1], outs[2:]


def _adamw(w, g, m, v):
    m = ADAM_B1 * m + (1.0 - ADAM_B1) * g
    v = ADAM_B2 * v + (1.0 - ADAM_B2) * (g * g)
    m_hat = m / (1.0 - ADAM_B1 ** ADAM_STEP)
    v_hat = v / (1.0 - ADAM_B2 ** ADAM_STEP)
    delta = -ADAM_LR * (m_hat / (jnp.sqrt(v_hat) + ADAM_EPS) + ADAM_WD * w)
    return delta, m, v


def _sum_slots(ref):
    g = ref[0].astype(F32)
    for s in range(1, N_DEV):
        g = g + ref[s].astype(F32)
    return g


def _sum_adamw(name, parts, w, m, v, rows_target, ex=None, ex_parts=()):
    rows, cols = w.shape
    tr = _pick_tile(rows, rows_target, 8)
    n_ex = 0 if ex is None else ex.n
    n_steps = rows // tr

    def body(*refs):
        p_ref, w_ref, m_ref, v_ref = refs[:4]
        ex_ins = refs[4:4 + n_ex]
        o = 4 + n_ex
        g_ref, d_ref, nm_ref, nv_ref = refs[o:o + 4]
        ex_outs, sems = refs[o + 4:o + 4 + n_ex], refs[o + 4 + n_ex:]
        if ex is not None:
            @pl.when(pl.program_id(0) == 0)
            def _():
                ex.start(ex_ins, ex_outs, sems)

        g = _sum_slots(p_ref)
        d, nm, nv = _adamw(w_ref[...], g, m_ref[...], v_ref[...])
        g_ref[...] = g
        d_ref[...] = d
        nm_ref[...] = nm
        nv_ref[...] = nv
        if ex is not None:
            @pl.when(pl.program_id(0) == n_steps - 1)
            def _():
                ex.wait(ex_ins, ex_outs, sems)

    tile = pl.BlockSpec((tr, cols), lambda i: (i, 0))
    ex_specs, ex_shape, ex_scratch = ([], [], []) if ex is None else (ex.specs, ex.out_shape, ex.scratch)
    outs = pl.pallas_call(
        body, name=name,
        grid=(n_steps,),
        in_specs=[pl.BlockSpec((N_DEV, tr, cols), lambda i: (0, i, 0)), tile, tile, tile] + ex_specs,
        out_specs=[tile] * 4 + ex_specs,
        out_shape=[jax.ShapeDtypeStruct((rows, cols), F32)] * 4 + ex_shape,
        scratch_shapes=ex_scratch,
        compiler_params=_params(("arbitrary",), 40),
    )(parts, w, m, v, *ex_parts)
    return outs[:4], outs[4:]


def _sum_adamw_small(parts_list, w_list, m_list, v_list, loss_parts):
    n = len(w_list)

    def body(*refs):
        p_refs, w_refs, m_refs, v_refs = refs[:n], refs[n:2 * n], refs[2 * n:3 * n], refs[3 * n:4 * n]
        lp_ref = refs[4 * n]
        outs = refs[4 * n + 1:]
        for a in range(n):
            g = _sum_slots(p_refs[a])
            d, nm, nv = _adamw(w_refs[a][...], g, m_refs[a][...], v_refs[a][...])
            outs[4 * a][...] = g
            outs[4 * a + 1][...] = d
            outs[4 * a + 2][...] = nm
            outs[4 * a + 3][...] = nv
        outs[4 * n][...] = _sum_slots(lp_ref)

    out_shape = []
    for w in w_list:
        out_shape += [jax.ShapeDtypeStruct(w.shape, F32)] * 4
    out_shape.append(jax.ShapeDtypeStruct(loss_parts.shape[1:], F32))
    return pl.pallas_call(body, name="sum_adamw_small", out_shape=out_shape)(
        *parts_list, *w_list, *m_list, *v_list, loss_parts)


def kernel(x, meta_tokens, ln_g, w_in, ret_gn_g, conv_dw_w, conv_dw_b, conv_ln_g, conv_ln_b, conv_pw_w, conv_pw_b, w_out, final_g, loss_target, m_meta_tokens, m_ln_g, m_w_in, m_ret_gn_g, m_conv_dw_w, m_conv_dw_b, m_conv_ln_g, m_conv_ln_b, m_conv_pw_w, m_conv_pw_b, m_w_out, m_final_g, v_meta_tokens, v_ln_g, v_w_in, v_ret_gn_g, v_conv_dw_w, v_conv_dw_b, v_conv_ln_g, v_conv_ln_b, v_conv_pw_w, v_conv_pw_b, v_w_out, v_final_g):
    _, SEQ, D = x.shape
    MIX = w_out.shape[2]
    RW = ret_gn_g.shape[1]
    CW = conv_pw_b.shape[1]
    assert RW == CW and MIX == RW + CW and SEQ % META_TILE == 0 and CONV_K - 1 <= HALO
    R = SEQ + META_TILE
    hd = RW // RET_HEADS
    half = hd // 2
    me = 4 * lax.axis_index("x") + 2 * lax.axis_index("y") + lax.axis_index("c")

    dw_pad = jnp.pad(conv_dw_w[0], ((0, HALO - CONV_K), (0, 0)))
    dw_g, meta_g = _gather_weights([dw_pad, meta_tokens], [1, 1])

    n_seq_chunks = SEQ // CHUNK
    base = jnp.concatenate([jnp.arange(n_seq_chunks, dtype=F32) * CHUNK + N_META,
                            jnp.zeros((META_TILE // CHUNK - 1,), F32), jnp.full((1,), N_META - CHUNK, F32)])
    inv_freq = ROPE_BASE ** (-jnp.arange(half, dtype=F32) / half)
    ang_base = (base[:, None] * inv_freq[None, :])[:, None, :]
    ang_row = jnp.arange(CHUNK, dtype=F32)[:, None] * inv_freq[None, :]
    rope = (jnp.cos(ang_base), jnp.sin(ang_base), jnp.cos(ang_row), jnp.sin(ang_row))

    xs = x[0]
    meta_tile = jnp.concatenate([jnp.zeros((META_TILE - N_META, D), F32), meta_g], axis=0)
    target = loss_target[0]
    final_g2 = final_g[None, :]

    hn = _rms_norm(xs, meta_tile, ln_g)
    pw_shard, w_out_shard = [conv_pw_w[0].astype(BF16)], [w_out[0].astype(BF16)]
    proj, w_in_g, (pw_g,) = _in_proj_gather(hn, w_in[0].astype(BF16), _Exchange(pw_shard, [None]), pw_shard)
    pw_g = pw_g.reshape(CW, CW)
    y, states = _retention_fwd(proj, rope, ret_gn_g, MIX)
    y, conv_out, (w_out_g,) = _conv_fwd(proj, y, dw_g, conv_dw_b, conv_ln_g, conv_ln_b, pw_g, conv_pw_b,
                                        _Exchange(w_out_shard, [None]), w_out_shard)
    w_out_g = w_out_g.reshape(MIX, D)
    dh2, dy, dwo_p, dfg_p, loss_p = _out_proj_loss(xs, meta_tile, y, w_out_g, final_g2, target)

    dp_conv, dpw_p, dww_p, cvec_p = _conv_bwd(proj, conv_out, dy, dw_g, conv_ln_g, conv_ln_b, pw_g, conv_pw_b)
    dp_ret, dgn_p, (r_wo, r_pw) = _retention_bwd(proj, rope, ret_gn_g, states, dy,
                                                 _Exchange([dwo_p, dpw_p], [0, 0]), [dwo_p, dpw_p])
    dwi_p = _w_in_grad(hn, dp_ret, dp_conv)
    dh, dlg_p, (r_wi,) = _h_grad(dp_ret, dp_conv, w_in_g, xs, meta_tile, dh2, ln_g, _Exchange([dwi_p], [1]), [dwi_p])
    grad_x = dh[:SEQ][None]

    def at_row(r, a, b=None):
        v = a if b is None else jnp.concatenate([a, b], axis=1)
        return jnp.pad(v, ((r, 7 - r), (0, D - v.shape[1])))
    vec8 = (at_row(0, dlg_p) + at_row(1, dfg_p)
            + at_row(2, dgn_p, cvec_p[3:4])
            + at_row(3, cvec_p[1:2], cvec_p[2:3])
            + at_row(4, cvec_p[0:1], jnp.broadcast_to(loss_p, (1, CW))))
    small = jnp.concatenate([dh[R - N_META:], vec8,
                             jnp.zeros((SMALL_ROWS - N_META - 8, D), F32)], axis=0)

    (g_wi, d_wi, nm_wi, nv_wi), (r_dww, r_small) = _sum_adamw(
        "sum_adamw_w_in", r_wi, w_in[0], m_w_in[0], v_w_in[0], 256,
        ex=_Exchange([dww_p, small], [1, None]), ex_parts=[dww_p, small])
    (g_wo, d_wo, nm_wo, nv_wo), _ = _sum_adamw("sum_adamw_w_out", r_wo, w_out[0], m_w_out[0], v_w_out[0], 128)
    (g_pw, d_pw, nm_pw, nv_pw), _ = _sum_adamw("sum_adamw_pw", r_pw, conv_pw_w[0], m_conv_pw_w[0], v_conv_pw_w[0], 128)

    dcol = D // N_DEV
    sm = lambda r0, nr, c0, nc: lax.slice(r_small, (0, r0, c0), (N_DEV, r0 + nr, c0 + nc))
    meta_parts = lax.dynamic_slice(r_small, (0, 0, me * dcol), (N_DEV, N_META, dcol))
    small_parts = [meta_parts, sm(16, 1, 0, D), sm(18, 1, 0, RW), r_dww, sm(18, 1, RW, CW),
                   sm(19, 1, 0, CW), sm(19, 1, CW, CW), sm(20, 1, 0, CW), sm(17, 1, 0, D)]
    pad31 = lambda a: jnp.pad(a, ((0, HALO - CONV_K), (0, 0)))
    ws = [meta_tokens, ln_g, ret_gn_g, pad31(conv_dw_w[0]), conv_dw_b, conv_ln_g, conv_ln_b, conv_pw_b, final_g2]
    ms = [m_meta_tokens, m_ln_g, m_ret_gn_g, pad31(m_conv_dw_w[0]), m_conv_dw_b, m_conv_ln_g, m_conv_ln_b,
          m_conv_pw_b, m_final_g[None, :]]
    vs = [v_meta_tokens, v_ln_g, v_ret_gn_g, pad31(v_conv_dw_w[0]), v_conv_dw_b, v_conv_ln_g, v_conv_ln_b,
          v_conv_pw_b, v_final_g[None, :]]
    loss_parts = sm(20, 1, CW, 1)
    outs = _sum_adamw_small(small_parts, ws, ms, vs, loss_parts)
    loss = outs[-1][0, 0]
    quad = [outs[4 * a:4 * a + 4] for a in range(len(ws))]
    (q_meta, q_lng, q_gn, q_dww, q_dwb, q_clg, q_clb, q_pwb, q_fg) = quad
    q_dww = [t[:CONV_K][None] for t in q_dww]
    q_fg = [t[0] for t in q_fg]
    q_wi = [t[None] for t in (g_wi, d_wi, nm_wi, nv_wi)]
    q_wo = [t[None] for t in (g_wo, d_wo, nm_wo, nv_wo)]
    q_pw = [t[None] for t in (g_pw, d_pw, nm_pw, nv_pw)]

    per_w = [q_meta, q_lng, q_wi, q_gn, q_dww, q_dwb, q_clg, q_clb, q_pw, q_pwb, q_wo, q_fg]
    result = [loss, grad_x]
    for which in range(4):
        result += [q[which] for q in per_w]
    return tuple(result)
```

```python
import numpy as np
import jax
import jax.numpy as jnp
from jax import lax
from jax.experimental import pallas as pl
from jax.experimental.pallas import tpu as pltpu

N_META = 16
RET_HEADS = 4
CONV_K = 31
CHUNK = 128
ROPE_BASE = 10000.0
EPS = 1e-6
ADAM_LR = 0.001
ADAM_B1 = 0.9
ADAM_B2 = 0.999
ADAM_EPS = 1e-08
ADAM_WD = 0.01
ADAM_STEP = 10

N_DEV = 8
META_TILE = 256
HALO = 32
SMALL_ROWS = 32
VMEM_BYTES_V7X = 64 * 1024 * 1024
MXU_DIM = 256

F32 = jnp.float32
BF16 = jnp.bfloat16
MESH = pl.DeviceIdType.MESH

NN = (((1,), (0,)), ((), ()))
NT = (((1,), (1,)), ((), ()))
TN = (((0,), (0,)), ((), ()))


def _dot(a, b, dims=NN):
    return lax.dot_general(a, b, dims, preferred_element_type=F32)


def _pick_tile(n, target, mult=16):
    best = None
    for t in range(mult, min(n, target) + 1, mult):
        if n % t == 0:
            best = t
    assert best is not None, (n, target)
    return best


def _params(sem=None, vmem_mb=None):
    kw = {}
    if sem is not None:
        kw["dimension_semantics"] = sem
    if vmem_mb is not None:
        kw["vmem_limit_bytes"] = min(vmem_mb * 1024 * 1024, VMEM_BYTES_V7X - 4 * 1024 * 1024)
    return pltpu.CompilerParams(**kw)


def _sigmoid(x):
    return jax.nn.sigmoid(x)


def _dsilu(x, sg):
    return sg * (1.0 + x * (1.0 - sg))


def _decay_tables(heads):
    h = np.arange(heads, dtype=np.float32)
    gamma = (1.0 - np.exp2(-5.0 - h)).astype(np.float32)
    log_g = np.log(gamma).astype(np.float32)
    idx = np.arange(CHUNK, dtype=np.float32)
    rel = idx[:, None] - idx[None, :]
    mask = np.where(rel[None] >= 0, np.exp(np.maximum(rel, 0.0)[None] * log_g[:, None, None]), 0.0)
    qd = np.exp((idx[None, :] + 1.0) * log_g[:, None])
    kd = np.exp((CHUNK - 1.0 - idx[None, :]) * log_g[:, None])
    cd = np.exp(CHUNK * log_g)
    return (mask.astype(np.float32), qd.astype(np.float32)[:, :, None], kd.astype(np.float32)[:, :, None],
            [float(c) for c in cd.astype(np.float32)])


class _Exchange:
    def __init__(self, parts, block_axes):
        self.block_axes = list(block_axes)
        self.n = len(parts)
        self.out_shape = []
        for p, ax in zip(parts, block_axes):
            shp = list(p.shape)
            if ax is not None:
                assert shp[ax] % N_DEV == 0
                shp[ax] //= N_DEV
            self.out_shape.append(jax.ShapeDtypeStruct((N_DEV, *shp), p.dtype))
        self.scratch = [pltpu.SemaphoreType.DMA((self.n, N_DEV - 1)), pltpu.SemaphoreType.DMA((self.n, N_DEV - 1)),
                        pltpu.SemaphoreType.DMA((self.n,))]
        self.specs = [pl.BlockSpec(memory_space=pl.ANY)] * self.n

    def _copies(self, ins, outs, sems):
        send_sems, recv_sems, local_sems = sems
        x, y, c = lax.axis_index("x"), lax.axis_index("y"), lax.axis_index("c")
        me_idx = 4 * x + 2 * y + c

        def src_block(a, dev_idx):
            ax = self.block_axes[a]
            if ax is None:
                return ins[a]
            n = ins[a].shape[ax] // N_DEV
            idx = [slice(None)] * len(ins[a].shape)
            idx[ax] = pl.ds(pl.multiple_of(dev_idx * n, n), n)
            return ins[a].at[tuple(idx)]

        local = [pltpu.make_async_copy(src_block(a, me_idx), outs[a].at[me_idx], local_sems.at[a])
                 for a in range(self.n)]
        remote = []
        for m in range(1, N_DEV):
            px, py, pc = x ^ ((m >> 2) & 1), y ^ ((m >> 1) & 1), c ^ (m & 1)
            for a in range(self.n):
                remote.append(pltpu.make_async_remote_copy(
                    src_ref=src_block(a, 4 * px + 2 * py + pc), dst_ref=outs[a].at[me_idx],
                    send_sem=send_sems.at[a, m - 1], recv_sem=recv_sems.at[a, m - 1],
                    device_id=(px, py, pc), device_id_type=MESH))
        return local, remote

    def start(self, ins, outs, sems):
        local, remote = self._copies(ins, outs, sems)
        for cp in local + remote:
            cp.start()

    def wait(self, ins, outs, sems):
        local, remote = self._copies(ins, outs, sems)
        for cp in remote:
            cp.wait_recv()
        for cp in remote:
            cp.wait_send()
        for cp in local:
            cp.wait()


def _rms_norm(xs, meta_shard, ln_g, ex, ex_parts):
    SEQ, D = xs.shape
    n_meta, dcol = meta_shard.shape
    tm = META_TILE
    n_seq = SEQ // tm

    def body(*refs):
        x_ref, g_ref = refs[:2]
        ex_ins = refs[2:2 + ex.n]
        hn_ref, mt_ref = refs[2 + ex.n:4 + ex.n]
        ex_outs = refs[4 + ex.n:4 + 2 * ex.n]
        slots, slot_sem = refs[4 + 2 * ex.n:6 + 2 * ex.n]
        sems = refs[6 + 2 * ex.n:]
        i = pl.program_id(0)

        @pl.when(i == 0)
        def _():
            ex.start(ex_ins, ex_outs, sems)

        def norm(hv):
            r = lax.rsqrt(jnp.mean(hv * hv, axis=-1, keepdims=True) + EPS)
            return (hv * r * g_ref[...]).astype(BF16)

        @pl.when(i < n_seq)
        def _():
            hn_ref[...] = norm(x_ref[...])

        @pl.when(i == n_seq)
        def _():
            ex.wait(ex_ins, ex_outs, sems)
            cp = pltpu.make_async_copy(ex_outs[0], slots, slot_sem)
            cp.start()
            mt_ref[...] = jnp.zeros_like(mt_ref)
            cp.wait()
            for s in range(N_DEV):
                mt_ref[tm - n_meta:tm, s * dcol:(s + 1) * dcol] = slots[s]
            hn_ref[...] = norm(mt_ref[...])

    outs = pl.pallas_call(
        body, name="rms_norm",
        grid=(n_seq + 1,),
        in_specs=[pl.BlockSpec((tm, D), lambda i: (jnp.minimum(i, n_seq - 1), 0)),
                  pl.BlockSpec((1, D), lambda i: (0, 0))] + ex.specs,
        out_specs=[pl.BlockSpec((tm, D), lambda i: (i, 0)),
                   pl.BlockSpec((tm, D), lambda i: (0, 0))] + ex.specs,
        out_shape=[jax.ShapeDtypeStruct((SEQ + tm, D), BF16), jax.ShapeDtypeStruct((tm, D), F32)] + ex.out_shape,
        scratch_shapes=[pltpu.VMEM((N_DEV, n_meta, dcol), F32), pltpu.SemaphoreType.DMA] + ex.scratch,
        compiler_params=_params(("arbitrary",), 32),
    )(xs, ln_g, *ex_parts)
    return outs[0], outs[1], outs[2:]


def _chip_visited(q):
    mine = 2 * lax.axis_index("x") + lax.axis_index("y")
    return mine ^ (((q & 1) << 1) | (q >> 1))


def _in_proj_gather(hn, w_shard, ex, ex_parts):
    R, D = hn.shape
    wb = w_shard.shape[1]
    E, tn = wb * N_DEV, 2 * wb
    n_q = N_DEV // 2
    tm = _pick_tile(R, min(768, R // 2), MXU_DIM)
    n_i = R // tm

    def body(*refs):
        hn_ref, wsh_hbm = refs[:2]
        ex_ins = refs[2:2 + ex.n]
        proj_ref, wg_hbm = refs[2 + ex.n:4 + ex.n]
        ex_outs = refs[4 + ex.n:4 + 2 * ex.n]
        w_vmem, send_sems, recv_sems, local_sem, vmem_sems = refs[4 + 2 * ex.n:9 + 2 * ex.n]
        ex_sems = refs[9 + 2 * ex.n:]
        q, i = pl.program_id(0), pl.program_id(1)
        x, y, c = lax.axis_index("x"), lax.axis_index("y"), lax.axis_index("c")
        me, sibling = (x, y, c), (x, y, 1 - c)
        chips = [(1 - x, y), (x, 1 - y), (1 - x, 1 - y)]

        def block(dev):
            return wg_hbm.at[:, pl.ds(pl.multiple_of((4 * dev[0] + 2 * dev[1] + dev[2]) * wb, wb), wb)]

        def copy(k, dev, to, src=None):
            return pltpu.make_async_remote_copy(
                src_ref=block(dev) if src is None else src, dst_ref=block(dev),
                send_sem=send_sems.at[k], recv_sem=recv_sems.at[k], device_id=to, device_id_type=MESH)

        def to_vmem(p):
            cols = pl.ds(pl.multiple_of(_chip_visited(p) * tn, tn), tn)
            return pltpu.make_async_copy(wg_hbm.at[:, cols], w_vmem.at[p % 2], vmem_sems.at[p % 2])

        mine = pltpu.make_async_copy(wsh_hbm, block(me), local_sem)
        first = [copy(0, me, sibling, src=wsh_hbm)] + [copy(1 + j, me, (*chip, c), src=wsh_hbm)
                                                       for j, chip in enumerate(chips)]
        passed = [copy(4 + j, (*chip, c), sibling) for j, chip in enumerate(chips)]

        @pl.when((q == 0) & (i == 0))
        def _():
            mine.start()
            for cp in first:
                cp.start()
            ex.start(ex_ins, ex_outs, ex_sems)
            mine.wait()
            copy(0, sibling, me).wait_recv()
            to_vmem(0).start()
            to_vmem(0).wait()

        for p in range(1, n_q):
            chip = chips[p - 1]

            @pl.when((q == p - 1) & (i == n_i - 2))
            def _():
                copy(p, (*chip, c), me).wait_recv()
                passed[p - 1].start()

            @pl.when((q == p - 1) & (i == n_i - 1))
            def _():
                copy(3 + p, (*chip, 1 - c), me).wait_recv()
                to_vmem(p).start()

            @pl.when((q == p) & (i == 0))
            def _():
                to_vmem(p).wait()

        proj_ref[...] = _dot(hn_ref[...], w_vmem[q % 2]).astype(BF16)

        @pl.when((q == n_q - 1) & (i == n_i - 1))
        def _():
            for cp in first + passed:
                cp.wait_send()
            ex.wait(ex_ins, ex_outs, ex_sems)

    hbm = pl.BlockSpec(memory_space=pl.ANY)
    outs = pl.pallas_call(
        body, name="in_proj",
        grid=(n_q, n_i),
        in_specs=[pl.BlockSpec((tm, D), lambda q, i: (i, 0)), hbm] + ex.specs,
        out_specs=[pl.BlockSpec((tm, tn), lambda q, i: (i, _chip_visited(q))), hbm] + ex.specs,
        out_shape=[jax.ShapeDtypeStruct((R, E), BF16), jax.ShapeDtypeStruct((D, E), BF16)] + ex.out_shape,
        scratch_shapes=[pltpu.VMEM((2, D, tn), BF16), pltpu.SemaphoreType.DMA((7,)), pltpu.SemaphoreType.DMA((7,)),
                        pltpu.SemaphoreType.DMA, pltpu.SemaphoreType.DMA((2,))] + ex.scratch,
        compiler_params=_params(("arbitrary", "arbitrary"), 48),
    )(hn, w_shard, *ex_parts)
    return outs[0], outs[1], outs[2:]


def _rope_chunk(cb_ref, sb_ref, ci_ref, si_ref):
    cb, sb, ci, si = cb_ref[0], sb_ref[0], ci_ref[...], si_ref[...]
    return cb * ci - sb * si, sb * ci + cb * si


def _rot(t, cos, sin, half):
    t1, t2 = t[:, :half], t[:, half:]
    return jnp.concatenate([t1 * cos - t2 * sin, t1 * sin + t2 * cos], axis=-1)


def _rot_inv(t, cos, sin, half):
    t1, t2 = t[:, :half], t[:, half:]
    return jnp.concatenate([t1 * cos + t2 * sin, t2 * cos - t1 * sin], axis=-1)


def _chunk_order(n_chunks):
    lead = META_TILE // CHUNK
    return lambda l: (l + n_chunks - lead) % n_chunks


def _retention_fwd(proj, rope, gn_g, mix):
    R, E = proj.shape
    RW = gn_g.shape[1]
    H = RET_HEADS
    hd = RW // H
    half = hd // 2
    NC = R // CHUNK
    mask, qd, kd, cd = _decay_tables(H)
    scale = float(hd) ** -0.5
    phys = _chunk_order(NC)

    def body(p_ref, cb_ref, sb_ref, ci_ref, si_ref, mask_ref, qd_ref, kd_ref, gn_ref, y_ref, st_ref, state):
        @pl.when(pl.program_id(0) == 0)
        def _():
            state[...] = jnp.zeros_like(state)

        cs, sn = _rope_chunk(cb_ref, sb_ref, ci_ref, si_ref)
        hs = range(H)
        col = lambda j, h: slice(j * RW + h * hd, j * RW + (h + 1) * hd)
        qr = [_rot(p_ref[:, col(0, h)].astype(F32), cs, sn, half) for h in hs]
        kr = [_rot(p_ref[:, col(1, h)].astype(F32), cs, sn, half) * scale for h in hs]
        v = [p_ref[:, col(2, h)] for h in hs]
        s_prev = [state[h] for h in hs]
        s_prev_b = [s_prev[h].astype(BF16) for h in hs]
        s = [(_dot(qr[h].astype(BF16), kr[h].astype(BF16), NT) * mask_ref[h]).astype(BF16) for h in hs]
        y_raw = [_dot(s[h], v[h]) + _dot((qr[h] * qd_ref[h]).astype(BF16), s_prev_b[h]) for h in hs]
        s_new = [s_prev[h] * cd[h] + _dot((kr[h] * kd_ref[h]).astype(BF16), v[h], TN) for h in hs]
        for h in hs:
            st_ref[0, h] = s_prev_b[h]
            state[h] = s_new[h]
        for h in hs:
            g = p_ref[:, col(3, h)].astype(F32)
            mu = jnp.mean(y_raw[h], axis=-1, keepdims=True)
            yc = y_raw[h] - mu
            var = jnp.mean(yc * yc, axis=-1, keepdims=True)
            out = yc * lax.rsqrt(var + EPS) * gn_ref[:, col(0, h)] * (g * _sigmoid(g))
            y_ref[:, col(0, h)] = out.astype(BF16)

    const3 = lambda l: (0, 0, 0)
    return pl.pallas_call(
        body, name="retention_fwd",
        grid=(NC,),
        in_specs=[pl.BlockSpec((CHUNK, 4 * RW), lambda l: (phys(l), 0)),
                  pl.BlockSpec((1, 1, half), lambda l: (phys(l), 0, 0)),
                  pl.BlockSpec((1, 1, half), lambda l: (phys(l), 0, 0)),
                  pl.BlockSpec((CHUNK, half), lambda l: (0, 0)),
                  pl.BlockSpec((CHUNK, half), lambda l: (0, 0)),
                  pl.BlockSpec((H, CHUNK, CHUNK), const3),
                  pl.BlockSpec((H, CHUNK, 1), const3),
                  pl.BlockSpec((H, CHUNK, 1), const3),
                  pl.BlockSpec((1, RW), lambda l: (0, 0))],
        out_specs=[pl.BlockSpec((CHUNK, RW), lambda l: (phys(l), 0)),
                   pl.BlockSpec((1, H, hd, hd), lambda l: (phys(l), 0, 0, 0))],
        out_shape=[jax.ShapeDtypeStruct((R, mix), BF16), jax.ShapeDtypeStruct((NC, H, hd, hd), BF16)],
        scratch_shapes=[pltpu.VMEM((H, hd, hd), F32)],
        compiler_params=_params(("arbitrary",), 32),
    )(proj, *rope, jnp.asarray(mask), jnp.asarray(qd), jnp.asarray(kd), gn_g)


CONV_ROWS = 64
CONV_LANES = 128
LANE = 128
ELEM_ROWS = 32


def _conv_order(n_tiles):
    return lambda l: (l + n_tiles - 1) % n_tiles


def _halo_block(n_tiles, tm):
    per = tm // HALO
    return lambda l: ((l + n_tiles - 2) % n_tiles) * per + per - 1


def _fill_shifted(src, dst):
    rows, width = dst.shape[1], dst.shape[2]
    step = _pick_tile(rows, 64, 8)
    for r in range(1, 8):
        for r0 in range(0, rows, step):
            for l0 in range(0, width, CONV_LANES):
                dst[r - 1, r0:r0 + step, l0:l0 + CONV_LANES] = src[r + r0:r + r0 + step, l0:l0 + CONV_LANES]


def _at_offset(src, shifted, off, r0, rows, lanes):
    r = off % 8
    a = off - r + r0
    if r == 0:
        return src[a:a + rows, lanes]
    return shifted[r - 1, a:a + rows, lanes]


def _fill_glu(first, a_ref, b_ref, ah_ref, bh_ref, u_ext, tm):
    uh = ah_ref[...].astype(F32) * _sigmoid(bh_ref[...].astype(F32))
    u_ext[0:HALO, :] = jnp.where(first, 0.0, uh)
    for r0 in range(0, tm, ELEM_ROWS):
        rows = slice(r0, r0 + ELEM_ROWS)
        u_ext[HALO + r0:HALO + r0 + ELEM_ROWS, :] = a_ref[rows, :].astype(F32) * _sigmoid(b_ref[rows, :].astype(F32))


def _layer_norm(cv, lg_ref, lb_ref):
    mu = jnp.mean(cv, axis=-1, keepdims=True)
    cc = cv - mu
    rstd = lax.rsqrt(jnp.mean(cc * cc, axis=-1, keepdims=True) + EPS)
    xh = cc * rstd
    return xh, rstd, xh * lg_ref[...] + lb_ref[...]


def _conv_fwd(proj, y_in, dw_w, dw_b, ln_g, ln_b, pw_w, pw_b, ex, ex_parts):
    R, E = proj.shape
    CW = pw_w.shape[0]
    tm = META_TILE
    NTL = R // tm
    phys = _conv_order(NTL)
    halo = _halo_block(NTL, tm)
    cb = (E - 3 * CW) // CW
    base = HALO - (CONV_K - 1)

    def body(*refs):
        a_ref, b_ref, g_ref, ah_ref, bh_ref, w_ref, wb_ref, lg_ref, lb_ref, pw_ref, pb_ref, yin_ref = refs[:12]
        ex_ins = refs[12:12 + ex.n]
        y_ref, c_ref = refs[12 + ex.n:14 + ex.n]
        ex_outs = refs[14 + ex.n:14 + 2 * ex.n]
        u_ext, u_sh, s_scr, upw_scr = refs[14 + 2 * ex.n:18 + 2 * ex.n]
        sems = refs[18 + 2 * ex.n:]

        @pl.when(pl.program_id(0) == 0)
        def _():
            ex.start(ex_ins, ex_outs, sems)

        _fill_glu(pl.program_id(0) == 0, a_ref, b_ref, ah_ref, bh_ref, u_ext, tm)
        _fill_shifted(u_ext, u_sh)
        for r0 in range(0, tm, CONV_ROWS):
            for l0 in range(0, CW, CONV_LANES):
                lanes = slice(l0, l0 + CONV_LANES)
                acc = None
                for k in range(CONV_K):
                    term = _at_offset(u_ext, u_sh, base + k, r0, CONV_ROWS, lanes) * w_ref[k:k + 1, lanes]
                    acc = term if acc is None else acc + term
                c_ref[r0:r0 + CONV_ROWS, lanes] = acc + wb_ref[:, lanes]
        blocks = [slice(r0, r0 + ELEM_ROWS) for r0 in range(0, tm, ELEM_ROWS)]
        for rows in blocks:
            _, _, ln = _layer_norm(c_ref[rows, :], lg_ref, lb_ref)
            s_scr[rows, :] = (ln * _sigmoid(ln)).astype(BF16)
        upw_scr[...] = _dot(s_scr[...], pw_ref[...]) + pb_ref[...]
        for rows in blocks:
            g = g_ref[rows, :].astype(F32)
            y_ref[rows, :] = (upw_scr[rows, :] * (g * _sigmoid(g))).astype(BF16)

        @pl.when(pl.program_id(0) == NTL - 1)
        def _():
            ex.wait(ex_ins, ex_outs, sems)

    row = lambda l: (0, 0)
    outs = pl.pallas_call(
        body, name="conv_fwd",
        grid=(NTL,),
        in_specs=[pl.BlockSpec((tm, CW), lambda l: (phys(l), cb)),
                  pl.BlockSpec((tm, CW), lambda l: (phys(l), cb + 1)),
                  pl.BlockSpec((tm, CW), lambda l: (phys(l), cb + 2)),
                  pl.BlockSpec((HALO, CW), lambda l: (halo(l), cb)),
                  pl.BlockSpec((HALO, CW), lambda l: (halo(l), cb + 1)),
                  pl.BlockSpec((HALO, CW), row),
                  pl.BlockSpec((1, CW), row), pl.BlockSpec((1, CW), row), pl.BlockSpec((1, CW), row),
                  pl.BlockSpec((CW, CW), row),
                  pl.BlockSpec((1, CW), row),
                  pl.BlockSpec(memory_space=pl.ANY)] + ex.specs,
        out_specs=[pl.BlockSpec((tm, CW), lambda l: (phys(l), 1)),
                   pl.BlockSpec((tm, CW), lambda l: (phys(l), 0))] + ex.specs,
        out_shape=[jax.ShapeDtypeStruct(y_in.shape, BF16), jax.ShapeDtypeStruct((R, CW), F32)] + ex.out_shape,
        input_output_aliases={11: 0},
        scratch_shapes=[pltpu.VMEM((HALO + tm, CW), F32), pltpu.VMEM((7, tm + HALO - 8, CW), F32),
                        pltpu.VMEM((tm, CW), BF16), pltpu.VMEM((tm, CW), F32)] + ex.scratch,
        compiler_params=_params(("arbitrary",), 48),
    )(proj, proj, proj, proj, proj, dw_w, dw_b, ln_g, ln_b, pw_w, pw_b, y_in, *ex_parts)
    return outs[0], outs[1], outs[2:]


def _out_proj_loss(xs, meta_tile, y, w_out, final_g, target):
    SEQ, D = xs.shape
    R, MIX = y.shape
    tm = META_TILE
    n_seq = SEQ // tm
    n_tiles = R // tm
    rows_out = _pick_tile(MIX, 256)

    def body(x_ref, mt_ref, y_ref, w_hbm, fg_ref, t_ref, dh2_ref, dy_ref, dwo_hbm, dfg_ref, loss_ref,
             w_scr, acc, stage, sem):
        i = pl.program_id(0)

        @pl.when(i == 0)
        def _():
            cp = pltpu.make_async_copy(w_hbm, w_scr, sem)
            cp.start()
            acc[...] = jnp.zeros_like(acc)
            dfg_ref[...] = jnp.zeros_like(dfg_ref)
            loss_ref[...] = jnp.zeros_like(loss_ref)
            cp.wait()

        yb = y_ref[...]
        h2 = jnp.where(i < n_seq, x_ref[...], mt_ref[...]) + _dot(yb, w_scr[...])
        r2 = lax.rsqrt(jnp.mean(h2 * h2, axis=-1, keepdims=True) + EPS)
        n = h2 * r2
        fg = fg_ref[...]
        err = jnp.where(i < n_seq, n * fg - t_ref[...], 0.0)
        loss_ref[...] += 0.5 * jnp.sum(jnp.mean(err * err, axis=-1, keepdims=True), axis=0, keepdims=True)
        dout = err * (1.0 / D)
        dfg_ref[...] += jnp.sum(dout * n, axis=0, keepdims=True)
        dn = dout * fg
        dh2 = r2 * (dn - n * jnp.mean(dn * n, axis=-1, keepdims=True))
        dh2_ref[...] = dh2
        dh2b = dh2.astype(BF16)
        dy_ref[...] = _dot(dh2b, w_scr[...], NT).astype(BF16)
        acc[...] += _dot(yb, dh2b, TN)

        @pl.when(i == n_tiles - 1)
        def _():
            for r in range(0, MIX, rows_out):
                stage[...] = acc[r:r + rows_out, :].astype(BF16)
                cp = pltpu.make_async_copy(stage, dwo_hbm.at[r:r + rows_out, :], sem)
                cp.start()
                cp.wait()

    row = lambda i: (0, 0)
    return pl.pallas_call(
        body, name="out_proj_loss",
        grid=(n_tiles,),
        in_specs=[pl.BlockSpec((tm, D), lambda i: (jnp.minimum(i, n_seq - 1), 0)),
                  pl.BlockSpec((tm, D), row),
                  pl.BlockSpec((tm, MIX), lambda i: (i, 0)),
                  pl.BlockSpec(memory_space=pl.ANY),
                  pl.BlockSpec((1, D), row),
                  pl.BlockSpec((tm, D), lambda i: (jnp.minimum(i, n_seq - 1), 0))],
        out_specs=[pl.BlockSpec((tm, D), lambda i: (i, 0)),
                   pl.BlockSpec((tm, MIX), lambda i: (i, 0)),
                   pl.BlockSpec(memory_space=pl.ANY),
                   pl.BlockSpec((1, D), row),
                   pl.BlockSpec((1, 1), row)],
        out_shape=[jax.ShapeDtypeStruct((R, D), F32), jax.ShapeDtypeStruct((R, MIX), BF16),
                   jax.ShapeDtypeStruct((MIX, D), BF16), jax.ShapeDtypeStruct((1, D), F32),
                   jax.ShapeDtypeStruct((1, 1), F32)],
        scratch_shapes=[pltpu.VMEM((MIX, D), BF16), pltpu.VMEM((MIX, D), F32), pltpu.VMEM((rows_out, D), BF16),
                        pltpu.SemaphoreType.DMA],
        compiler_params=_params(("arbitrary",), 60),
    )(xs, meta_tile, y, w_out, final_g, target)


def _conv_bwd(proj, conv_out, dy, dw_w, ln_g, ln_b, pw_w, pw_b):
    R, E = proj.shape
    CW = pw_w.shape[0]
    tm = META_TILE
    NTL = R // tm
    order = _conv_order(NTL)
    phys = lambda i: order(NTL - 1 - i)
    halo_l = _halo_block(NTL, tm)
    halo = lambda i: halo_l(NTL - 1 - i)
    cb = (E - 3 * CW) // CW
    base = HALO - (CONV_K - 1)

    def body(a_ref, b_ref, g_ref, ah_ref, bh_ref, c_ref, dy_ref, w_ref, lg_ref, lb_ref, pw_ref, pb_ref,
             dp_ref, dpw_ref, dww_ref, vec_ref, u_ext, u_sh, dc_ext, dc_sh, du_scr, dww_acc, dpw_acc,
             xh_scr, rstd_scr, ln_scr, sg_scr, upw_scr, s_scr, dupw_scr):
        i = pl.program_id(0)

        @pl.when(i == 0)
        def _():
            dpw_acc[...] = jnp.zeros_like(dpw_acc)
            dww_ref[...] = jnp.zeros_like(dww_ref)
            vec_ref[...] = jnp.zeros_like(vec_ref)
            dww_acc[...] = jnp.zeros_like(dww_acc)
            dc_ext[tm:tm + HALO, :] = jnp.zeros((HALO, CW), F32)

        _fill_glu(i == NTL - 1, a_ref, b_ref, ah_ref, bh_ref, u_ext, tm)
        _fill_shifted(u_ext, u_sh)
        blocks = [slice(r0, r0 + ELEM_ROWS) for r0 in range(0, tm, ELEM_ROWS)]
        for rows in blocks:
            xh, rstd, ln = _layer_norm(c_ref[rows, :], lg_ref, lb_ref)
            sg = _sigmoid(ln)
            xh_scr[rows, :], rstd_scr[rows, :], ln_scr[rows, :], sg_scr[rows, :] = xh, rstd, ln, sg
            s_scr[rows, :] = (ln * sg).astype(BF16)
        upw_scr[...] = _dot(s_scr[...], pw_ref[...]) + pb_ref[...]
        col_sum = jnp.zeros((1, CW), F32)
        for rows in blocks:
            g = g_ref[rows, :].astype(F32)
            sgg = _sigmoid(g)
            dyc = dy_ref[rows, :].astype(F32)
            dp_ref[rows, 2 * CW:3 * CW] = (dyc * upw_scr[rows, :] * _dsilu(g, sgg)).astype(BF16)
            dupw = dyc * (g * sgg)
            dupw_scr[rows, :] = dupw.astype(BF16)
            col_sum = col_sum + jnp.sum(dupw, axis=0, keepdims=True)
        vec_ref[0:1, :] += col_sum
        dpw_acc[...] += _dot(s_scr[...], dupw_scr[...], TN)
        upw_scr[...] = _dot(dupw_scr[...], pw_ref[...], NT)
        sum_g, sum_b, sum_c = col_sum * 0.0, col_sum * 0.0, col_sum * 0.0
        for rows in blocks:
            xh, rstd = xh_scr[rows, :], rstd_scr[rows, :]
            dln = upw_scr[rows, :] * _dsilu(ln_scr[rows, :], sg_scr[rows, :])
            sum_g = sum_g + jnp.sum(dln * xh, axis=0, keepdims=True)
            sum_b = sum_b + jnp.sum(dln, axis=0, keepdims=True)
            dxh = dln * lg_ref[...]
            dc = rstd * (dxh - jnp.mean(dxh, axis=-1, keepdims=True)
                         - xh * jnp.mean(dxh * xh, axis=-1, keepdims=True))
            sum_c = sum_c + jnp.sum(dc, axis=0, keepdims=True)
            dc_ext[rows, :] = dc
        vec_ref[1:2, :] += sum_g
        vec_ref[2:3, :] += sum_b
        vec_ref[3:4, :] += sum_c
        _fill_shifted(dc_ext, dc_sh)

        for l0 in range(0, CW, CONV_LANES):
            lanes = slice(l0, l0 + CONV_LANES)
            for r0 in range(0, tm, CONV_ROWS):
                acc = None
                for k in range(CONV_K):
                    term = _at_offset(dc_ext, dc_sh, CONV_K - 1 - k, r0, CONV_ROWS, lanes) * w_ref[k:k + 1, lanes]
                    acc = term if acc is None else acc + term
                du_scr[r0:r0 + CONV_ROWS, lanes] = acc

        n_grp = tm // 8
        by_shift = [[(k, (base + k) // 8) for k in range(CONV_K) if (base + k) % 8 == r] for r in range(8)]
        for l0 in range(0, CW, LANE):
            lane = slice(l0, l0 + LANE)
            for r in range(8):
                src = u_ext if r == 0 else u_sh.at[r - 1]
                a_lo, a_hi = by_shift[r][0][1], by_shift[r][-1][1]
                sums = {k: None for k, _ in by_shift[r]}
                dcg = {}
                for gi in range(a_lo, n_grp + a_hi):
                    if gi - a_lo < n_grp:
                        dcg[gi - a_lo] = dc_ext[8 * (gi - a_lo):8 * (gi - a_lo) + 8, lane]
                    dcg.pop(gi - a_hi - 1, None)
                    ug = src[8 * gi:8 * gi + 8, lane]
                    for k, a in by_shift[r]:
                        if 0 <= gi - a < n_grp:
                            prod = dcg[gi - a] * ug
                            sums[k] = prod if sums[k] is None else sums[k] + prod
                for k, _ in by_shift[r]:
                    dww_acc[k, :, lane] += sums[k]

        for rows in blocks:
            du = du_scr[rows, :]
            sgb = _sigmoid(b_ref[rows, :].astype(F32))
            dp_ref[rows, 0:CW] = (du * sgb).astype(BF16)
            dp_ref[rows, CW:2 * CW] = (du * a_ref[rows, :].astype(F32) * sgb * (1.0 - sgb)).astype(BF16)
        dc_ext[tm:tm + HALO, :] = dc_ext[0:HALO, :]

        @pl.when(i == NTL - 1)
        def _():
            for k in range(CONV_K):
                dww_ref[k:k + 1, :] = jnp.sum(dww_acc[k], axis=0, keepdims=True)
            dpw_ref[...] = dpw_acc[...].astype(BF16)

    row = lambda i: (0, 0)
    return pl.pallas_call(
        body, name="conv_bwd",
        grid=(NTL,),
        in_specs=[pl.BlockSpec((tm, CW), lambda i: (phys(i), cb)),
                  pl.BlockSpec((tm, CW), lambda i: (phys(i), cb + 1)),
                  pl.BlockSpec((tm, CW), lambda i: (phys(i), cb + 2)),
                  pl.BlockSpec((HALO, CW), lambda i: (halo(i), cb)),
                  pl.BlockSpec((HALO, CW), lambda i: (halo(i), cb + 1)),
                  pl.BlockSpec((tm, CW), lambda i: (phys(i), 0)),
                  pl.BlockSpec((tm, CW), lambda i: (phys(i), 1)),
                  pl.BlockSpec((HALO, CW), row),
                  pl.BlockSpec((1, CW), row), pl.BlockSpec((1, CW), row),
                  pl.BlockSpec((CW, CW), row),
                  pl.BlockSpec((1, CW), row)],
        out_specs=[pl.BlockSpec((tm, 3 * CW), lambda i: (phys(i), 0)),
                   pl.BlockSpec((CW, CW), row),
                   pl.BlockSpec((HALO, CW), row),
                   pl.BlockSpec((8, CW), row)],
        out_shape=[jax.ShapeDtypeStruct((R, 3 * CW), BF16), jax.ShapeDtypeStruct((CW, CW), BF16),
                   jax.ShapeDtypeStruct((HALO, CW), F32), jax.ShapeDtypeStruct((8, CW), F32)],
        scratch_shapes=[pltpu.VMEM((HALO + tm, CW), F32), pltpu.VMEM((7, tm + HALO - 8, CW), F32),
                        pltpu.VMEM((tm + HALO, CW), F32), pltpu.VMEM((7, tm + HALO - 8, CW), F32),
                        pltpu.VMEM((tm, CW), F32), pltpu.VMEM((CONV_K, 8, CW), F32), pltpu.VMEM((CW, CW), F32),
                        pltpu.VMEM((tm, CW), F32), pltpu.VMEM((tm, 1), F32), pltpu.VMEM((tm, CW), F32),
                        pltpu.VMEM((tm, CW), F32), pltpu.VMEM((tm, CW), F32), pltpu.VMEM((tm, CW), BF16),
                        pltpu.VMEM((tm, CW), BF16)],
        compiler_params=_params(("arbitrary",), 60),
    )(proj, proj, proj, proj, proj, conv_out, dy, dw_w, ln_g, ln_b, pw_w, pw_b)


def _retention_bwd(proj, rope, gn_g, states, dy, ex, ex_parts):
    R, E = proj.shape
    RW = gn_g.shape[1]
    H = RET_HEADS
    hd = RW // H
    half = hd // 2
    NC = R // CHUNK
    mask, qd, kd, cd = _decay_tables(H)
    scale = float(hd) ** -0.5
    order = _chunk_order(NC)
    phys = lambda i: order(NC - 1 - i)

    def body(*refs):
        p_ref, cb_ref, sb_ref, ci_ref, si_ref, mask_ref, qd_ref, kd_ref, gn_ref, st_ref, dy_ref = refs[:11]
        ex_ins = refs[11:11 + ex.n]
        dp_ref, dgn_ref = refs[11 + ex.n:13 + ex.n]
        ex_outs = refs[13 + ex.n:13 + 2 * ex.n]
        dstate = refs[13 + 2 * ex.n]
        sems = refs[14 + 2 * ex.n:]

        @pl.when(pl.program_id(0) == 0)
        def _():
            ex.start(ex_ins, ex_outs, sems)
            dstate[...] = jnp.zeros_like(dstate)
            dgn_ref[...] = jnp.zeros_like(dgn_ref)

        cs, sn = _rope_chunk(cb_ref, sb_ref, ci_ref, si_ref)
        hs = range(H)
        col = lambda j, h: slice(j * RW + h * hd, j * RW + (h + 1) * hd)
        qr = [_rot(p_ref[:, col(0, h)].astype(F32), cs, sn, half) for h in hs]
        kr = [_rot(p_ref[:, col(1, h)].astype(F32), cs, sn, half) * scale for h in hs]
        v = [p_ref[:, col(2, h)] for h in hs]
        qb = [qr[h].astype(BF16) for h in hs]
        kb = [kr[h].astype(BF16) for h in hs]
        qdb = [(qr[h] * qd_ref[h]).astype(BF16) for h in hs]
        kdb = [(kr[h] * kd_ref[h]).astype(BF16) for h in hs]
        s_prev = [st_ref[0, h] for h in hs]
        dst = [dstate[h] for h in hs]
        dstb = [dst[h].astype(BF16) for h in hs]
        sb = [(_dot(qb[h], kb[h], NT) * mask_ref[h]).astype(BF16) for h in hs]
        y_raw = [_dot(sb[h], v[h]) + _dot(qdb[h], s_prev[h]) for h in hs]
        dyrb, dg = [], []
        for h in hs:
            g = p_ref[:, col(3, h)].astype(F32)
            mu = jnp.mean(y_raw[h], axis=-1, keepdims=True)
            yc = y_raw[h] - mu
            rstd = lax.rsqrt(jnp.mean(yc * yc, axis=-1, keepdims=True) + EPS)
            xh = yc * rstd
            gn = gn_ref[:, col(0, h)]
            sg = _sigmoid(g)
            dyh = dy_ref[:, col(0, h)].astype(F32)
            dg.append((dyh * (xh * gn) * _dsilu(g, sg)).astype(BF16))
            dyn = dyh * (g * sg)
            dgn_ref[:, col(0, h)] += jnp.sum(dyn * xh, axis=0, keepdims=True)
            dxh = dyn * gn
            dyr = rstd * (dxh - jnp.mean(dxh, axis=-1, keepdims=True)
                          - xh * jnp.mean(dxh * xh, axis=-1, keepdims=True))
            dyrb.append(dyr.astype(BF16))
        dsb = [(_dot(dyrb[h], v[h], NT) * mask_ref[h]).astype(BF16) for h in hs]
        dqr = [_dot(dsb[h], kb[h]) + _dot(dyrb[h], s_prev[h], NT) * qd_ref[h] for h in hs]
        dkr = [_dot(dsb[h], qb[h], TN) + _dot(v[h], dstb[h], NT) * kd_ref[h] for h in hs]
        dv = [_dot(sb[h], dyrb[h], TN) + _dot(kdb[h], dstb[h]) for h in hs]
        dst_new = [dst[h] * cd[h] + _dot(qdb[h], dyrb[h], TN) for h in hs]
        for h in hs:
            dstate[h] = dst_new[h]
            dp_ref[:, col(0, h)] = _rot_inv(dqr[h], cs, sn, half).astype(BF16)
            dp_ref[:, col(1, h)] = (_rot_inv(dkr[h], cs, sn, half) * scale).astype(BF16)
            dp_ref[:, col(2, h)] = dv[h].astype(BF16)
            dp_ref[:, col(3, h)] = dg[h]

        @pl.when(pl.program_id(0) == NC - 1)
        def _():
            ex.wait(ex_ins, ex_outs, sems)

    const3 = lambda i: (0, 0, 0)
    outs = pl.pallas_call(
        body, name="retention_bwd",
        grid=(NC,),
        in_specs=[pl.BlockSpec((CHUNK, 4 * RW), lambda i: (phys(i), 0)),
                  pl.BlockSpec((1, 1, half), lambda i: (phys(i), 0, 0)),
                  pl.BlockSpec((1, 1, half), lambda i: (phys(i), 0, 0)),
                  pl.BlockSpec((CHUNK, half), lambda i: (0, 0)),
                  pl.BlockSpec((CHUNK, half), lambda i: (0, 0)),
                  pl.BlockSpec((H, CHUNK, CHUNK), const3),
                  pl.BlockSpec((H, CHUNK, 1), const3),
                  pl.BlockSpec((H, CHUNK, 1), const3),
                  pl.BlockSpec((1, RW), lambda i: (0, 0)),
                  pl.BlockSpec((1, H, hd, hd), lambda i: (phys(i), 0, 0, 0)),
                  pl.BlockSpec((CHUNK, RW), lambda i: (phys(i), 0))] + ex.specs,
        out_specs=[pl.BlockSpec((CHUNK, 4 * RW), lambda i: (phys(i), 0)),
                   pl.BlockSpec((1, RW), lambda i: (0, 0))] + ex.specs,
        out_shape=[jax.ShapeDtypeStruct((R, 4 * RW), BF16), jax.ShapeDtypeStruct((1, RW), F32)] + ex.out_shape,
        scratch_shapes=[pltpu.VMEM((H, hd, hd), F32)] + ex.scratch,
        compiler_params=_params(("arbitrary",), 32),
    )(proj, *rope, jnp.asarray(mask), jnp.asarray(qd), jnp.asarray(kd), gn_g, states, dy, *ex_parts)
    return outs[0], outs[1], outs[2:]


def _dproj_specs(tk, tn, n_ret, tile_axis, col_axis):
    def ret_map(*ids):
        t, j = ids[tile_axis], ids[col_axis]
        return (jnp.where(j < n_ret, t, 0), jnp.minimum(j, n_ret - 1))

    def conv_map(*ids):
        t, j = ids[tile_axis], ids[col_axis]
        return (jnp.where(j >= n_ret, t, 0), jnp.maximum(j - n_ret, 0))

    return pl.BlockSpec((tk, tn), ret_map), pl.BlockSpec((tk, tn), conv_map)


def _w_in_grad(hn, dp_ret, dp_conv):
    R, D = hn.shape
    tn = _pick_tile(dp_conv.shape[1] // 3, 1024, 128)
    n_ret, n_conv = dp_ret.shape[1] // tn, dp_conv.shape[1] // tn
    E = dp_ret.shape[1] + dp_conv.shape[1]
    tk = _pick_tile(R, 1024, MXU_DIM)
    n_t = R // tk
    ret_spec, conv_spec = _dproj_specs(tk, tn, n_ret, 1, 0)

    def body(hn_ref, r_ref, c_ref, out_ref, acc):
        j, t = pl.program_id(0), pl.program_id(1)

        @pl.when(t == 0)
        def _():
            acc[...] = jnp.zeros_like(acc)

        @pl.when(j < n_ret)
        def _():
            acc[...] += _dot(hn_ref[...], r_ref[...], TN)

        @pl.when(j >= n_ret)
        def _():
            acc[...] += _dot(hn_ref[...], c_ref[...], TN)

        @pl.when(t == n_t - 1)
        def _():
            out_ref[...] = acc[...].astype(BF16)

    return pl.pallas_call(
        body, name="w_in_grad",
        grid=(n_ret + n_conv, n_t),
        in_specs=[pl.BlockSpec((tk, D), lambda j, t: (t, 0)), ret_spec, conv_spec],
        out_specs=pl.BlockSpec((D, tn), lambda j, t: (0, j)),
        out_shape=jax.ShapeDtypeStruct((D, E), BF16),
        scratch_shapes=[pltpu.VMEM((D, tn), F32)],
        compiler_params=_params(("arbitrary", "arbitrary"), 48),
    )(hn, dp_ret, dp_conv)


def _h_grad(dp_ret, dp_conv, w_in, xs, meta_tile, dh2, ln_g, ex, ex_parts):
    R, D = dh2.shape
    te = CHUNK
    n_x = xs.shape[0] // te
    n_m = meta_tile.shape[0] // te
    tn = _pick_tile(dp_conv.shape[1] // 3, 1024, 128)
    n_ret, n_conv = dp_ret.shape[1] // tn, dp_conv.shape[1] // tn
    n_k = n_ret + n_conv
    tm = _pick_tile(R, 1024, meta_tile.shape[0])
    n_e = tm // te
    n_t = R // tm
    assert n_e <= n_k and (n_x + n_m) * te == R

    def ret_map(t, k):
        return (jnp.where(k < n_ret, jnp.minimum(t, n_t - 1), 0), jnp.minimum(k, n_ret - 1))

    def conv_map(t, k):
        return (jnp.where(k >= n_ret, jnp.minimum(t, n_t - 1), 0), jnp.maximum(k - n_ret, 0))

    def row_block(t, k):
        return jnp.maximum(t - 1, 0) * n_e + jnp.where(t > 0, jnp.minimum(k, n_e - 1), 0)

    def body(*refs):
        r_ref, c_ref, w_hbm, w_ref, x_ref, mt_ref, dh2_ref, g_ref = refs[:8]
        ex_ins = refs[8:8 + ex.n]
        o = 8 + ex.n
        dh_ref, dlg_ref = refs[o:o + 2]
        ex_outs = refs[o + 2:o + 2 + ex.n]
        acc, w_keep, keep_sems = refs[o + 2 + ex.n:o + 5 + ex.n]
        sems = refs[o + 5 + ex.n:]
        t, k = pl.program_id(0), pl.program_id(1)
        cur, old = t % 2, (t + 1) % 2

        def keep(j):
            return pltpu.make_async_copy(w_hbm.at[:, j * tn:(j + 1) * tn], w_keep.at[j], keep_sems.at[j])

        @pl.when((k == 0) & (t == 0))
        def _():
            for j in range(n_ret):
                keep(j).start()
            ex.start(ex_ins, ex_outs, sems)
            dlg_ref[...] = jnp.zeros_like(dlg_ref)

        for j in range(n_ret):
            @pl.when((k == j) & (t == 0))
            def _():
                keep(j).wait()

        @pl.when((k == 0) & (t < n_t))
        def _():
            acc[cur] = _dot(r_ref[...], w_keep[0], NT)

        @pl.when((k > 0) & (k < n_ret) & (t < n_t))
        def _():
            acc[cur] += _dot(r_ref[...], w_keep[k], NT)

        @pl.when((k >= n_ret) & (t < n_t))
        def _():
            acc[cur] += _dot(c_ref[...], w_ref[...], NT)

        @pl.when((k < n_e) & (t > 0))
        def _():
            hv = jnp.where(row_block(t, k) < n_x, x_ref[...], mt_ref[...])
            r = lax.rsqrt(jnp.mean(hv * hv, axis=-1, keepdims=True) + EPS)
            nrm = hv * r
            dhn = acc[old, pl.ds(pl.multiple_of(k * te, te), te), :]
            dlg_ref[...] += jnp.sum(dhn * nrm, axis=0, keepdims=True)
            dn = dhn * g_ref[...]
            dh_ref[...] = dh2_ref[...] + r * (dn - nrm * jnp.mean(dn * nrm, axis=-1, keepdims=True))

        @pl.when((k == n_k - 1) & (t == n_t))
        def _():
            ex.wait(ex_ins, ex_outs, sems)

    row = lambda t, k: (0, 0)
    outs = pl.pallas_call(
        body, name="h_grad",
        grid=(n_t + 1, n_k),
        in_specs=[pl.BlockSpec((tm, tn), ret_map), pl.BlockSpec((tm, tn), conv_map),
                  pl.BlockSpec(memory_space=pl.ANY),
                  pl.BlockSpec((D, tn), lambda t, k: (0, jnp.maximum(k, n_ret))),
                  pl.BlockSpec((te, D), lambda t, k: (jnp.minimum(row_block(t, k), n_x - 1), 0)),
                  pl.BlockSpec((te, D), lambda t, k: (jnp.clip(row_block(t, k) - n_x, 0, n_m - 1), 0)),
                  pl.BlockSpec((te, D), lambda t, k: (row_block(t, k), 0)),
                  pl.BlockSpec((1, D), row)] + ex.specs,
        out_specs=[pl.BlockSpec((te, D), lambda t, k: (row_block(t, k), 0)),
                   pl.BlockSpec((1, D), row)] + ex.specs,
        out_shape=[jax.ShapeDtypeStruct((R, D), F32), jax.ShapeDtypeStruct((1, D), F32)] + ex.out_shape,
        scratch_shapes=[pltpu.VMEM((2, tm, D), F32), pltpu.VMEM((n_ret, D, tn), BF16),
                        pltpu.SemaphoreType.DMA((n_ret,))] + ex.scratch,
        compiler_params=_params(("arbitrary", "arbitrary"), 60),
    )(dp_ret, dp_conv, w_in, w_in, xs, meta_tile, dh2, ln_g, *ex_parts)
    return outs[0], outs[1], outs[2:]


def _adamw(w, g, m, v):
    m = ADAM_B1 * m + (1.0 - ADAM_B1) * g
    v = ADAM_B2 * v + (1.0 - ADAM_B2) * (g * g)
    m_hat = m / (1.0 - ADAM_B1 ** ADAM_STEP)
    v_hat = v / (1.0 - ADAM_B2 ** ADAM_STEP)
    delta = -ADAM_LR * (m_hat / (jnp.sqrt(v_hat) + ADAM_EPS) + ADAM_WD * w)
    return delta, m, v


def _sum_slots(ref):
    g = ref[0].astype(F32)
    for s in range(1, N_DEV):
        g = g + ref[s].astype(F32)
    return g


def _sum_adamw(name, parts, w, m, v, rows_target, ex=None, ex_parts=()):
    rows, cols = w.shape
    tr = _pick_tile(rows, rows_target, 8)
    n_ex = 0 if ex is None else ex.n
    n_steps = rows // tr

    def body(*refs):
        p_ref, w_ref, m_ref, v_ref = refs[:4]
        ex_ins = refs[4:4 + n_ex]
        o = 4 + n_ex
        g_ref, d_ref, nm_ref, nv_ref = refs[o:o + 4]
        ex_outs, sems = refs[o + 4:o + 4 + n_ex], refs[o + 4 + n_ex:]
        if ex is not None:
            @pl.when(pl.program_id(0) == 0)
            def _():
                ex.start(ex_ins, ex_outs, sems)

        g = _sum_slots(p_ref)
        d, nm, nv = _adamw(w_ref[...], g, m_ref[...], v_ref[...])
        g_ref[...] = g
        d_ref[...] = d
        nm_ref[...] = nm
        nv_ref[...] = nv
        if ex is not None:
            @pl.when(pl.program_id(0) == n_steps - 1)
            def _():
                ex.wait(ex_ins, ex_outs, sems)

    tile = pl.BlockSpec((tr, cols), lambda i: (i, 0))
    ex_specs, ex_shape, ex_scratch = ([], [], []) if ex is None else (ex.specs, ex.out_shape, ex.scratch)
    outs = pl.pallas_call(
        body, name=name,
        grid=(n_steps,),
        in_specs=[pl.BlockSpec((N_DEV, tr, cols), lambda i: (0, i, 0)), tile, tile, tile] + ex_specs,
        out_specs=[tile] * 4 + ex_specs,
        out_shape=[jax.ShapeDtypeStruct((rows, cols), F32)] * 4 + ex_shape,
        scratch_shapes=ex_scratch,
        compiler_params=_params(("arbitrary",), 40),
    )(parts, w, m, v, *ex_parts)
    return outs[:4], outs[4:]


def _sum_adamw_small(parts_list, w_list, m_list, v_list, loss_parts):
    n = len(w_list)

    def body(*refs):
        p_refs, w_refs, m_refs, v_refs = refs[:n], refs[n:2 * n], refs[2 * n:3 * n], refs[3 * n:4 * n]
        lp_ref = refs[4 * n]
        outs = refs[4 * n + 1:]
        for a in range(n):
            g = _sum_slots(p_refs[a])
            d, nm, nv = _adamw(w_refs[a][...], g, m_refs[a][...], v_refs[a][...])
            outs[4 * a][...] = g
            outs[4 * a + 1][...] = d
            outs[4 * a + 2][...] = nm
            outs[4 * a + 3][...] = nv
        outs[4 * n][...] = _sum_slots(lp_ref)

    out_shape = []
    for w in w_list:
        out_shape += [jax.ShapeDtypeStruct(w.shape, F32)] * 4
    out_shape.append(jax.ShapeDtypeStruct(loss_parts.shape[1:], F32))
    return pl.pallas_call(body, name="sum_adamw_small", out_shape=out_shape)(
        *parts_list, *w_list, *m_list, *v_list, loss_parts)


def kernel(x, meta_tokens, ln_g, w_in, ret_gn_g, conv_dw_w, conv_dw_b, conv_ln_g, conv_ln_b, conv_pw_w, conv_pw_b, w_out, final_g, loss_target, m_meta_tokens, m_ln_g, m_w_in, m_ret_gn_g, m_conv_dw_w, m_conv_dw_b, m_conv_ln_g, m_conv_ln_b, m_conv_pw_w, m_conv_pw_b, m_w_out, m_final_g, v_meta_tokens, v_ln_g, v_w_in, v_ret_gn_g, v_conv_dw_w, v_conv_dw_b, v_conv_ln_g, v_conv_ln_b, v_conv_pw_w, v_conv_pw_b, v_w_out, v_final_g):
    _, SEQ, D = x.shape
    MIX = w_out.shape[2]
    RW = ret_gn_g.shape[1]
    CW = conv_pw_b.shape[1]
    assert RW == CW and MIX == RW + CW and SEQ % META_TILE == 0 and CONV_K - 1 <= HALO
    R = SEQ + META_TILE
    hd = RW // RET_HEADS
    half = hd // 2
    me = 4 * lax.axis_index("x") + 2 * lax.axis_index("y") + lax.axis_index("c")

    dw_pad = jnp.pad(conv_dw_w[0], ((0, HALO - CONV_K), (0, 0)))

    n_seq_chunks = SEQ // CHUNK
    base = jnp.concatenate([jnp.arange(n_seq_chunks, dtype=F32) * CHUNK + N_META,
                            jnp.zeros((META_TILE // CHUNK - 1,), F32), jnp.full((1,), N_META - CHUNK, F32)])
    inv_freq = ROPE_BASE ** (-jnp.arange(half, dtype=F32) / half)
    ang_base = (base[:, None] * inv_freq[None, :])[:, None, :]
    ang_row = jnp.arange(CHUNK, dtype=F32)[:, None] * inv_freq[None, :]
    rope = (jnp.cos(ang_base), jnp.sin(ang_base), jnp.cos(ang_row), jnp.sin(ang_row))

    xs = x[0]
    target = loss_target[0]
    final_g2 = final_g[None, :]

    small_shards = [meta_tokens, dw_pad]
    hn, meta_tile, (_, dw_g) = _rms_norm(xs, meta_tokens, ln_g, _Exchange(small_shards, [None, None]), small_shards)
    dw_g = jnp.swapaxes(dw_g, 0, 1).reshape(HALO, CW)
    pw_shard, w_out_shard = [conv_pw_w[0].astype(BF16)], [w_out[0].astype(BF16)]
    proj, w_in_g, (pw_g,) = _in_proj_gather(hn, w_in[0].astype(BF16), _Exchange(pw_shard, [None]), pw_shard)
    pw_g = pw_g.reshape(CW, CW)
    y, states = _retention_fwd(proj, rope, ret_gn_g, MIX)
    y, conv_out, (w_out_g,) = _conv_fwd(proj, y, dw_g, conv_dw_b, conv_ln_g, conv_ln_b, pw_g, conv_pw_b,
                                        _Exchange(w_out_shard, [None]), w_out_shard)
    w_out_g = w_out_g.reshape(MIX, D)
    dh2, dy, dwo_p, dfg_p, loss_p = _out_proj_loss(xs, meta_tile, y, w_out_g, final_g2, target)

    dp_conv, dpw_p, dww_p, cvec_p = _conv_bwd(proj, conv_out, dy, dw_g, conv_ln_g, conv_ln_b, pw_g, conv_pw_b)
    dp_ret, dgn_p, (r_wo, r_pw) = _retention_bwd(proj, rope, ret_gn_g, states, dy,
                                                 _Exchange([dwo_p, dpw_p], [0, 0]), [dwo_p, dpw_p])
    dwi_p = _w_in_grad(hn, dp_ret, dp_conv)
    dh, dlg_p, (r_wi,) = _h_grad(dp_ret, dp_conv, w_in_g, xs, meta_tile, dh2, ln_g, _Exchange([dwi_p], [1]), [dwi_p])
    grad_x = dh[:SEQ][None]

    def at_row(r, a, b=None):
        v = a if b is None else jnp.concatenate([a, b], axis=1)
        return jnp.pad(v, ((r, 7 - r), (0, D - v.shape[1])))
    vec8 = (at_row(0, dlg_p) + at_row(1, dfg_p)
            + at_row(2, dgn_p, cvec_p[3:4])
            + at_row(3, cvec_p[1:2], cvec_p[2:3])
            + at_row(4, cvec_p[0:1], jnp.broadcast_to(loss_p, (1, CW))))
    small = jnp.concatenate([dh[R - N_META:], vec8,
                             jnp.zeros((SMALL_ROWS - N_META - 8, D), F32)], axis=0)

    (g_wi, d_wi, nm_wi, nv_wi), (r_dww, r_small) = _sum_adamw(
        "sum_adamw_w_in", r_wi, w_in[0], m_w_in[0], v_w_in[0], 256,
        ex=_Exchange([dww_p, small], [1, None]), ex_parts=[dww_p, small])
    (g_wo, d_wo, nm_wo, nv_wo), _ = _sum_adamw("sum_adamw_w_out", r_wo, w_out[0], m_w_out[0], v_w_out[0], 128)
    (g_pw, d_pw, nm_pw, nv_pw), _ = _sum_adamw("sum_adamw_pw", r_pw, conv_pw_w[0], m_conv_pw_w[0], v_conv_pw_w[0], 128)

    dcol = D // N_DEV
    sm = lambda r0, nr, c0, nc: lax.slice(r_small, (0, r0, c0), (N_DEV, r0 + nr, c0 + nc))
    meta_parts = lax.dynamic_slice(r_small, (0, 0, me * dcol), (N_DEV, N_META, dcol))
    small_parts = [meta_parts, sm(16, 1, 0, D), sm(18, 1, 0, RW), r_dww, sm(18, 1, RW, CW),
                   sm(19, 1, 0, CW), sm(19, 1, CW, CW), sm(20, 1, 0, CW), sm(17, 1, 0, D)]
    pad31 = lambda a: jnp.pad(a, ((0, HALO - CONV_K), (0, 0)))
    ws = [meta_tokens, ln_g, ret_gn_g, pad31(conv_dw_w[0]), conv_dw_b, conv_ln_g, conv_ln_b, conv_pw_b, final_g2]
    ms = [m_meta_tokens, m_ln_g, m_ret_gn_g, pad31(m_conv_dw_w[0]), m_conv_dw_b, m_conv_ln_g, m_conv_ln_b,
          m_conv_pw_b, m_final_g[None, :]]
    vs = [v_meta_tokens, v_ln_g, v_ret_gn_g, pad31(v_conv_dw_w[0]), v_conv_dw_b, v_conv_ln_g, v_conv_ln_b,
          v_conv_pw_b, v_final_g[None, :]]
    loss_parts = sm(20, 1, CW, 1)
    outs = _sum_adamw_small(small_parts, ws, ms, vs, loss_parts)
    loss = outs[-1][0, 0]
    quad = [outs[4 * a:4 * a + 4] for a in range(len(ws))]
    (q_meta, q_lng, q_gn, q_dww, q_dwb, q_clg, q_clb, q_pwb, q_fg) = quad
    q_dww = [t[:CONV_K][None] for t in q_dww]
    q_fg = [t[0] for t in q_fg]
    q_wi = [t[None] for t in (g_wi, d_wi, nm_wi, nv_wi)]
    q_wo = [t[None] for t in (g_wo, d_wo, nm_wo, nv_wo)]
    q_pw = [t[None] for t in (g_pw, d_pw, nm_pw, nv_pw)]

    per_w = [q_meta, q_lng, q_wi, q_gn, q_dww, q_dwb, q_clg, q_clb, q_pw, q_pwb, q_wo, q_fg]
    result = [loss, grad_x]
    for which in range(4):
        result += [q[which] for q in per_w]
    return tuple(result)
```

```python
import numpy as np
import jax
import jax.numpy as jnp
from jax import lax
from jax.experimental import pallas as pl
from jax.experimental.pallas import tpu as pltpu

N_META = 16
RET_HEADS = 4
CONV_K = 31
CHUNK = 128
ROPE_BASE = 10000.0
EPS = 1e-6
ADAM_LR = 0.001
ADAM_B1 = 0.9
ADAM_B2 = 0.999
ADAM_EPS = 1e-08
ADAM_WD = 0.01
ADAM_STEP = 10

N_DEV = 8
META_TILE = 256
HALO = 32
SMALL_ROWS = 32
VMEM_BYTES_V7X = 64 * 1024 * 1024
MXU_DIM = 256

F32 = jnp.float32
BF16 = jnp.bfloat16
MESH = pl.DeviceIdType.MESH

NN = (((1,), (0,)), ((), ()))
NT = (((1,), (1,)), ((), ()))
TN = (((0,), (0,)), ((), ()))


def _dot(a, b, dims=NN):
    return lax.dot_general(a, b, dims, preferred_element_type=F32)


def _pick_tile(n, target, mult=16):
    best = None
    for t in range(mult, min(n, target) + 1, mult):
        if n % t == 0:
            best = t
    assert best is not None, (n, target)
    return best


def _params(sem=None, vmem_mb=None):
    kw = {}
    if sem is not None:
        kw["dimension_semantics"] = sem
    if vmem_mb is not None:
        kw["vmem_limit_bytes"] = min(vmem_mb * 1024 * 1024, VMEM_BYTES_V7X - 4 * 1024 * 1024)
    return pltpu.CompilerParams(**kw)


def _sigmoid(x):
    return jax.nn.sigmoid(x)


def _dsilu(x, sg):
    return sg * (1.0 + x * (1.0 - sg))


def _decay_tables(heads):
    h = np.arange(heads, dtype=np.float32)
    gamma = (1.0 - np.exp2(-5.0 - h)).astype(np.float32)
    log_g = np.log(gamma).astype(np.float32)
    idx = np.arange(CHUNK, dtype=np.float32)
    rel = idx[:, None] - idx[None, :]
    mask = np.where(rel[None] >= 0, np.exp(np.maximum(rel, 0.0)[None] * log_g[:, None, None]), 0.0)
    qd = np.exp((idx[None, :] + 1.0) * log_g[:, None])
    kd = np.exp((CHUNK - 1.0 - idx[None, :]) * log_g[:, None])
    cd = np.exp(CHUNK * log_g)
    return (mask.astype(np.float32), qd.astype(np.float32)[:, :, None], kd.astype(np.float32)[:, :, None],
            [float(c) for c in cd.astype(np.float32)])


class _Exchange:
    def __init__(self, parts, block_axes):
        self.block_axes = list(block_axes)
        self.n = len(parts)
        self.out_shape = []
        for p, ax in zip(parts, block_axes):
            shp = list(p.shape)
            if ax is not None:
                assert shp[ax] % N_DEV == 0
                shp[ax] //= N_DEV
            self.out_shape.append(jax.ShapeDtypeStruct((N_DEV, *shp), p.dtype))
        self.scratch = [pltpu.SemaphoreType.DMA((self.n, N_DEV - 1)), pltpu.SemaphoreType.DMA((self.n, N_DEV - 1)),
                        pltpu.SemaphoreType.DMA((self.n,))]
        self.specs = [pl.BlockSpec(memory_space=pl.ANY)] * self.n

    def _copies(self, ins, outs, sems):
        send_sems, recv_sems, local_sems = sems
        x, y, c = lax.axis_index("x"), lax.axis_index("y"), lax.axis_index("c")
        me_idx = 4 * x + 2 * y + c

        def src_block(a, dev_idx):
            ax = self.block_axes[a]
            if ax is None:
                return ins[a]
            n = ins[a].shape[ax] // N_DEV
            idx = [slice(None)] * len(ins[a].shape)
            idx[ax] = pl.ds(pl.multiple_of(dev_idx * n, n), n)
            return ins[a].at[tuple(idx)]

        local = [pltpu.make_async_copy(src_block(a, me_idx), outs[a].at[me_idx], local_sems.at[a])
                 for a in range(self.n)]
        remote = []
        for m in range(1, N_DEV):
            px, py, pc = x ^ ((m >> 2) & 1), y ^ ((m >> 1) & 1), c ^ (m & 1)
            for a in range(self.n):
                remote.append(pltpu.make_async_remote_copy(
                    src_ref=src_block(a, 4 * px + 2 * py + pc), dst_ref=outs[a].at[me_idx],
                    send_sem=send_sems.at[a, m - 1], recv_sem=recv_sems.at[a, m - 1],
                    device_id=(px, py, pc), device_id_type=MESH))
        return local, remote

    def start(self, ins, outs, sems):
        local, remote = self._copies(ins, outs, sems)
        for cp in local + remote:
            cp.start()

    def wait(self, ins, outs, sems):
        local, remote = self._copies(ins, outs, sems)
        for cp in remote:
            cp.wait_recv()
        for cp in remote:
            cp.wait_send()
        for cp in local:
            cp.wait()


def _rms_norm(xs, meta_shard, ln_g, ex, ex_parts):
    SEQ, D = xs.shape
    n_meta, dcol = meta_shard.shape
    tm = 2 * META_TILE
    n_seq = SEQ // tm

    def body(*refs):
        x_ref, g_ref = refs[:2]
        ex_ins = refs[2:2 + ex.n]
        hn_ref, mt_ref = refs[2 + ex.n:4 + ex.n]
        ex_outs = refs[4 + ex.n:4 + 2 * ex.n]
        slots, slot_sem = refs[4 + 2 * ex.n:6 + 2 * ex.n]
        sems = refs[6 + 2 * ex.n:]
        i = pl.program_id(0)

        @pl.when(i == 0)
        def _():
            ex.start(ex_ins, ex_outs, sems)

        def norm(hv):
            r = lax.rsqrt(jnp.mean(hv * hv, axis=-1, keepdims=True) + EPS)
            return (hv * r * g_ref[...]).astype(BF16)

        @pl.when(i < n_seq)
        def _():
            hn_ref[...] = norm(x_ref[...])

        @pl.when(i == n_seq)
        def _():
            ex.wait(ex_ins, ex_outs, sems)
            cp = pltpu.make_async_copy(ex_outs[0], slots, slot_sem)
            cp.start()
            mt_ref[...] = jnp.zeros_like(mt_ref)
            cp.wait()
            for s in range(N_DEV):
                mt_ref[META_TILE - n_meta:META_TILE, s * dcol:(s + 1) * dcol] = slots[s]
            hn_ref[0:META_TILE, :] = norm(mt_ref[...])

    outs = pl.pallas_call(
        body, name="rms_norm",
        grid=(n_seq + 1,),
        in_specs=[pl.BlockSpec((tm, D), lambda i: (jnp.minimum(i, n_seq - 1), 0)),
                  pl.BlockSpec((1, D), lambda i: (0, 0))] + ex.specs,
        out_specs=[pl.BlockSpec((tm, D), lambda i: (i, 0)),
                   pl.BlockSpec((META_TILE, D), lambda i: (0, 0))] + ex.specs,
        out_shape=[jax.ShapeDtypeStruct((SEQ + tm, D), BF16), jax.ShapeDtypeStruct((META_TILE, D), F32)]
                  + ex.out_shape,
        scratch_shapes=[pltpu.VMEM((N_DEV, n_meta, dcol), F32), pltpu.SemaphoreType.DMA] + ex.scratch,
        compiler_params=_params(("arbitrary",), 32),
    )(xs, ln_g, *ex_parts)
    return outs[0], outs[1], outs[2:]


def _chip_visited(q):
    mine = 2 * lax.axis_index("x") + lax.axis_index("y")
    return mine ^ (((q & 1) << 1) | (q >> 1))


def _in_proj_gather(hn, n_rows, w_shard, ex, ex_parts):
    R, D = n_rows, hn.shape[1]
    wb = w_shard.shape[1]
    E, tn = wb * N_DEV, 2 * wb
    n_q = N_DEV // 2
    tm = _pick_tile(R, min(768, R // 2), MXU_DIM)
    n_i = R // tm

    def body(*refs):
        hn_ref, wsh_hbm = refs[:2]
        ex_ins = refs[2:2 + ex.n]
        proj_ref, wg_hbm = refs[2 + ex.n:4 + ex.n]
        ex_outs = refs[4 + ex.n:4 + 2 * ex.n]
        w_vmem, send_sems, recv_sems, local_sem, vmem_sems = refs[4 + 2 * ex.n:9 + 2 * ex.n]
        ex_sems = refs[9 + 2 * ex.n:]
        q, i = pl.program_id(0), pl.program_id(1)
        x, y, c = lax.axis_index("x"), lax.axis_index("y"), lax.axis_index("c")
        me, sibling = (x, y, c), (x, y, 1 - c)
        chips = [(1 - x, y), (x, 1 - y), (1 - x, 1 - y)]

        def block(dev):
            return wg_hbm.at[:, pl.ds(pl.multiple_of((4 * dev[0] + 2 * dev[1] + dev[2]) * wb, wb), wb)]

        def copy(k, dev, to, src=None):
            return pltpu.make_async_remote_copy(
                src_ref=block(dev) if src is None else src, dst_ref=block(dev),
                send_sem=send_sems.at[k], recv_sem=recv_sems.at[k], device_id=to, device_id_type=MESH)

        def to_vmem(p):
            cols = pl.ds(pl.multiple_of(_chip_visited(p) * tn, tn), tn)
            return pltpu.make_async_copy(wg_hbm.at[:, cols], w_vmem.at[p % 2], vmem_sems.at[p % 2])

        mine = pltpu.make_async_copy(wsh_hbm, block(me), local_sem)
        first = [copy(0, me, sibling, src=wsh_hbm)] + [copy(1 + j, me, (*chip, c), src=wsh_hbm)
                                                       for j, chip in enumerate(chips)]
        passed = [copy(4 + j, (*chip, c), sibling) for j, chip in enumerate(chips)]

        @pl.when((q == 0) & (i == 0))
        def _():
            mine.start()
            for cp in first:
                cp.start()
            ex.start(ex_ins, ex_outs, ex_sems)
            mine.wait()
            copy(0, sibling, me).wait_recv()
            to_vmem(0).start()
            to_vmem(0).wait()

        for p in range(1, n_q):
            chip = chips[p - 1]

            @pl.when((q == p - 1) & (i == n_i - 2))
            def _():
                copy(p, (*chip, c), me).wait_recv()
                passed[p - 1].start()

            @pl.when((q == p - 1) & (i == n_i - 1))
            def _():
                copy(3 + p, (*chip, 1 - c), me).wait_recv()
                to_vmem(p).start()

            @pl.when((q == p) & (i == 0))
            def _():
                to_vmem(p).wait()

        proj_ref[...] = _dot(hn_ref[...], w_vmem[q % 2]).astype(BF16)

        @pl.when((q == n_q - 1) & (i == n_i - 1))
        def _():
            for cp in first + passed:
                cp.wait_send()
            ex.wait(ex_ins, ex_outs, ex_sems)

    hbm = pl.BlockSpec(memory_space=pl.ANY)
    outs = pl.pallas_call(
        body, name="in_proj",
        grid=(n_q, n_i),
        in_specs=[pl.BlockSpec((tm, D), lambda q, i: (i, 0)), hbm] + ex.specs,
        out_specs=[pl.BlockSpec((tm, tn), lambda q, i: (i, _chip_visited(q))), hbm] + ex.specs,
        out_shape=[jax.ShapeDtypeStruct((R, E), BF16), jax.ShapeDtypeStruct((D, E), BF16)] + ex.out_shape,
        scratch_shapes=[pltpu.VMEM((2, D, tn), BF16), pltpu.SemaphoreType.DMA((7,)), pltpu.SemaphoreType.DMA((7,)),
                        pltpu.SemaphoreType.DMA, pltpu.SemaphoreType.DMA((2,))] + ex.scratch,
        compiler_params=_params(("arbitrary", "arbitrary"), 48),
    )(hn, w_shard, *ex_parts)
    return outs[0], outs[1], outs[2:]


def _rope_chunk(cb_ref, sb_ref, ci_ref, si_ref):
    cb, sb, ci, si = cb_ref[0], sb_ref[0], ci_ref[...], si_ref[...]
    return cb * ci - sb * si, sb * ci + cb * si


def _rot(t, cos, sin, half):
    t1, t2 = t[:, :half], t[:, half:]
    return jnp.concatenate([t1 * cos - t2 * sin, t1 * sin + t2 * cos], axis=-1)


def _rot_inv(t, cos, sin, half):
    t1, t2 = t[:, :half], t[:, half:]
    return jnp.concatenate([t1 * cos + t2 * sin, t2 * cos - t1 * sin], axis=-1)


def _chunk_order(n_chunks):
    lead = META_TILE // CHUNK
    return lambda l: (l + n_chunks - lead) % n_chunks


def _retention_fwd(proj, rope, gn_g, mix):
    R, E = proj.shape
    RW = gn_g.shape[1]
    H = RET_HEADS
    hd = RW // H
    half = hd // 2
    NC = R // CHUNK
    mask, qd, kd, cd = _decay_tables(H)
    scale = float(hd) ** -0.5
    phys = _chunk_order(NC)

    def body(p_ref, cb_ref, sb_ref, ci_ref, si_ref, mask_ref, qd_ref, kd_ref, gn_ref, y_ref, st_ref, state):
        @pl.when(pl.program_id(0) == 0)
        def _():
            state[...] = jnp.zeros_like(state)

        cs, sn = _rope_chunk(cb_ref, sb_ref, ci_ref, si_ref)
        hs = range(H)
        col = lambda j, h: slice(j * RW + h * hd, j * RW + (h + 1) * hd)
        qr = [_rot(p_ref[:, col(0, h)].astype(F32), cs, sn, half) for h in hs]
        kr = [_rot(p_ref[:, col(1, h)].astype(F32), cs, sn, half) * scale for h in hs]
        v = [p_ref[:, col(2, h)] for h in hs]
        s_prev = [state[h] for h in hs]
        s_prev_b = [s_prev[h].astype(BF16) for h in hs]
        s = [(_dot(qr[h].astype(BF16), kr[h].astype(BF16), NT) * mask_ref[h]).astype(BF16) for h in hs]
        y_raw = [_dot(s[h], v[h]) + _dot((qr[h] * qd_ref[h]).astype(BF16), s_prev_b[h]) for h in hs]
        s_new = [s_prev[h] * cd[h] + _dot((kr[h] * kd_ref[h]).astype(BF16), v[h], TN) for h in hs]
        for h in hs:
            st_ref[0, h] = s_prev_b[h]
            state[h] = s_new[h]
        for h in hs:
            g = p_ref[:, col(3, h)].astype(F32)
            mu = jnp.mean(y_raw[h], axis=-1, keepdims=True)
            yc = y_raw[h] - mu
            var = jnp.mean(yc * yc, axis=-1, keepdims=True)
            out = yc * lax.rsqrt(var + EPS) * gn_ref[:, col(0, h)] * (g * _sigmoid(g))
            y_ref[:, col(0, h)] = out.astype(BF16)

    const3 = lambda l: (0, 0, 0)
    return pl.pallas_call(
        body, name="retention_fwd",
        grid=(NC,),
        in_specs=[pl.BlockSpec((CHUNK, 4 * RW), lambda l: (phys(l), 0)),
                  pl.BlockSpec((1, 1, half), lambda l: (phys(l), 0, 0)),
                  pl.BlockSpec((1, 1, half), lambda l: (phys(l), 0, 0)),
                  pl.BlockSpec((CHUNK, half), lambda l: (0, 0)),
                  pl.BlockSpec((CHUNK, half), lambda l: (0, 0)),
                  pl.BlockSpec((H, CHUNK, CHUNK), const3),
                  pl.BlockSpec((H, CHUNK, 1), const3),
                  pl.BlockSpec((H, CHUNK, 1), const3),
                  pl.BlockSpec((1, RW), lambda l: (0, 0))],
        out_specs=[pl.BlockSpec((CHUNK, RW), lambda l: (phys(l), 0)),
                   pl.BlockSpec((1, H, hd, hd), lambda l: (phys(l), 0, 0, 0))],
        out_shape=[jax.ShapeDtypeStruct((R, mix), BF16), jax.ShapeDtypeStruct((NC, H, hd, hd), BF16)],
        scratch_shapes=[pltpu.VMEM((H, hd, hd), F32)],
        compiler_params=_params(("arbitrary",), 32),
    )(proj, *rope, jnp.asarray(mask), jnp.asarray(qd), jnp.asarray(kd), gn_g)


CONV_ROWS = 64
CONV_LANES = 128
LANE = 128
ELEM_ROWS = 32


def _conv_order(n_tiles):
    return lambda l: (l + n_tiles - 1) % n_tiles


def _halo_block(n_tiles, tm):
    per = tm // HALO
    return lambda l: ((l + n_tiles - 2) % n_tiles) * per + per - 1


def _fill_shifted(src, dst):
    rows, width = dst.shape[1], dst.shape[2]
    step = _pick_tile(rows, 64, 8)
    for r in range(1, 8):
        for r0 in range(0, rows, step):
            for l0 in range(0, width, CONV_LANES):
                dst[r - 1, r0:r0 + step, l0:l0 + CONV_LANES] = src[r + r0:r + r0 + step, l0:l0 + CONV_LANES]


def _at_offset(src, shifted, off, r0, rows, lanes):
    r = off % 8
    a = off - r + r0
    if r == 0:
        return src[a:a + rows, lanes]
    return shifted[r - 1, a:a + rows, lanes]


def _fill_glu(first, a_ref, b_ref, ah_ref, bh_ref, u_ext, tm):
    uh = ah_ref[...].astype(F32) * _sigmoid(bh_ref[...].astype(F32))
    u_ext[0:HALO, :] = jnp.where(first, 0.0, uh)
    for r0 in range(0, tm, ELEM_ROWS):
        rows = slice(r0, r0 + ELEM_ROWS)
        u_ext[HALO + r0:HALO + r0 + ELEM_ROWS, :] = a_ref[rows, :].astype(F32) * _sigmoid(b_ref[rows, :].astype(F32))


def _layer_norm(cv, lg_ref, lb_ref):
    mu = jnp.mean(cv, axis=-1, keepdims=True)
    cc = cv - mu
    rstd = lax.rsqrt(jnp.mean(cc * cc, axis=-1, keepdims=True) + EPS)
    xh = cc * rstd
    return xh, rstd, xh * lg_ref[...] + lb_ref[...]


def _conv_fwd(proj, y_in, dw_w, dw_b, ln_g, ln_b, pw_w, pw_b, ex, ex_parts):
    R, E = proj.shape
    CW = pw_w.shape[0]
    tm = META_TILE
    NTL = R // tm
    phys = _conv_order(NTL)
    halo = _halo_block(NTL, tm)
    cb = (E - 3 * CW) // CW
    base = HALO - (CONV_K - 1)

    def body(*refs):
        a_ref, b_ref, g_ref, ah_ref, bh_ref, w_ref, wb_ref, lg_ref, lb_ref, pw_ref, pb_ref, yin_ref = refs[:12]
        ex_ins = refs[12:12 + ex.n]
        y_ref, c_ref = refs[12 + ex.n:14 + ex.n]
        ex_outs = refs[14 + ex.n:14 + 2 * ex.n]
        u_ext, u_sh, s_scr, upw_scr = refs[14 + 2 * ex.n:18 + 2 * ex.n]
        sems = refs[18 + 2 * ex.n:]

        @pl.when(pl.program_id(0) == 0)
        def _():
            ex.start(ex_ins, ex_outs, sems)

        _fill_glu(pl.program_id(0) == 0, a_ref, b_ref, ah_ref, bh_ref, u_ext, tm)
        _fill_shifted(u_ext, u_sh)
        for r0 in range(0, tm, CONV_ROWS):
            for l0 in range(0, CW, CONV_LANES):
                lanes = slice(l0, l0 + CONV_LANES)
                acc = None
                for k in range(CONV_K):
                    term = _at_offset(u_ext, u_sh, base + k, r0, CONV_ROWS, lanes) * w_ref[k:k + 1, lanes]
                    acc = term if acc is None else acc + term
                c_ref[r0:r0 + CONV_ROWS, lanes] = acc + wb_ref[:, lanes]
        blocks = [slice(r0, r0 + ELEM_ROWS) for r0 in range(0, tm, ELEM_ROWS)]
        for rows in blocks:
            _, _, ln = _layer_norm(c_ref[rows, :], lg_ref, lb_ref)
            s_scr[rows, :] = (ln * _sigmoid(ln)).astype(BF16)
        upw_scr[...] = _dot(s_scr[...], pw_ref[...]) + pb_ref[...]
        for rows in blocks:
            g = g_ref[rows, :].astype(F32)
            y_ref[rows, :] = (upw_scr[rows, :] * (g * _sigmoid(g))).astype(BF16)

        @pl.when(pl.program_id(0) == NTL - 1)
        def _():
            ex.wait(ex_ins, ex_outs, sems)

    row = lambda l: (0, 0)
    outs = pl.pallas_call(
        body, name="conv_fwd",
        grid=(NTL,),
        in_specs=[pl.BlockSpec((tm, CW), lambda l: (phys(l), cb)),
                  pl.BlockSpec((tm, CW), lambda l: (phys(l), cb + 1)),
                  pl.BlockSpec((tm, CW), lambda l: (phys(l), cb + 2)),
                  pl.BlockSpec((HALO, CW), lambda l: (halo(l), cb)),
                  pl.BlockSpec((HALO, CW), lambda l: (halo(l), cb + 1)),
                  pl.BlockSpec((HALO, CW), row),
                  pl.BlockSpec((1, CW), row), pl.BlockSpec((1, CW), row), pl.BlockSpec((1, CW), row),
                  pl.BlockSpec((CW, CW), row),
                  pl.BlockSpec((1, CW), row),
                  pl.BlockSpec(memory_space=pl.ANY)] + ex.specs,
        out_specs=[pl.BlockSpec((tm, CW), lambda l: (phys(l), 1)),
                   pl.BlockSpec((tm, CW), lambda l: (phys(l), 0))] + ex.specs,
        out_shape=[jax.ShapeDtypeStruct(y_in.shape, BF16), jax.ShapeDtypeStruct((R, CW), F32)] + ex.out_shape,
        input_output_aliases={11: 0},
        scratch_shapes=[pltpu.VMEM((HALO + tm, CW), F32), pltpu.VMEM((7, tm + HALO - 8, CW), F32),
                        pltpu.VMEM((tm, CW), BF16), pltpu.VMEM((tm, CW), F32)] + ex.scratch,
        compiler_params=_params(("arbitrary",), 48),
    )(proj, proj, proj, proj, proj, dw_w, dw_b, ln_g, ln_b, pw_w, pw_b, y_in, *ex_parts)
    return outs[0], outs[1], outs[2:]


def _out_proj_loss(xs, meta_tile, y, w_out, final_g, target):
    SEQ, D = xs.shape
    R, MIX = y.shape
    tm = META_TILE
    n_seq = SEQ // tm
    n_tiles = R // tm
    rows_out = _pick_tile(MIX, 256)

    def body(x_ref, mt_ref, y_ref, w_hbm, fg_ref, t_ref, dh2_ref, dy_ref, dwo_hbm, dfg_ref, loss_ref,
             w_scr, acc, stage, sem):
        i = pl.program_id(0)

        @pl.when(i == 0)
        def _():
            cp = pltpu.make_async_copy(w_hbm, w_scr, sem)
            cp.start()
            acc[...] = jnp.zeros_like(acc)
            dfg_ref[...] = jnp.zeros_like(dfg_ref)
            loss_ref[...] = jnp.zeros_like(loss_ref)
            cp.wait()

        yb = y_ref[...]
        h2 = jnp.where(i < n_seq, x_ref[...], mt_ref[...]) + _dot(yb, w_scr[...])
        r2 = lax.rsqrt(jnp.mean(h2 * h2, axis=-1, keepdims=True) + EPS)
        n = h2 * r2
        fg = fg_ref[...]
        err = jnp.where(i < n_seq, n * fg - t_ref[...], 0.0)
        loss_ref[...] += 0.5 * jnp.sum(jnp.mean(err * err, axis=-1, keepdims=True), axis=0, keepdims=True)
        dout = err * (1.0 / D)
        dfg_ref[...] += jnp.sum(dout * n, axis=0, keepdims=True)
        dn = dout * fg
        dh2 = r2 * (dn - n * jnp.mean(dn * n, axis=-1, keepdims=True))
        dh2_ref[...] = dh2
        dh2b = dh2.astype(BF16)
        dy_ref[...] = _dot(dh2b, w_scr[...], NT).astype(BF16)
        acc[...] += _dot(yb, dh2b, TN)

        @pl.when(i == n_tiles - 1)
        def _():
            for r in range(0, MIX, rows_out):
                stage[...] = acc[r:r + rows_out, :].astype(BF16)
                cp = pltpu.make_async_copy(stage, dwo_hbm.at[r:r + rows_out, :], sem)
                cp.start()
                cp.wait()

    row = lambda i: (0, 0)
    return pl.pallas_call(
        body, name="out_proj_loss",
        grid=(n_tiles,),
        in_specs=[pl.BlockSpec((tm, D), lambda i: (jnp.minimum(i, n_seq - 1), 0)),
                  pl.BlockSpec((tm, D), row),
                  pl.BlockSpec((tm, MIX), lambda i: (i, 0)),
                  pl.BlockSpec(memory_space=pl.ANY),
                  pl.BlockSpec((1, D), row),
                  pl.BlockSpec((tm, D), lambda i: (jnp.minimum(i, n_seq - 1), 0))],
        out_specs=[pl.BlockSpec((tm, D), lambda i: (i, 0)),
                   pl.BlockSpec((tm, MIX), lambda i: (i, 0)),
                   pl.BlockSpec(memory_space=pl.ANY),
                   pl.BlockSpec((1, D), row),
                   pl.BlockSpec((1, 1), row)],
        out_shape=[jax.ShapeDtypeStruct((R, D), F32), jax.ShapeDtypeStruct((R, MIX), BF16),
                   jax.ShapeDtypeStruct((MIX, D), BF16), jax.ShapeDtypeStruct((1, D), F32),
                   jax.ShapeDtypeStruct((1, 1), F32)],
        scratch_shapes=[pltpu.VMEM((MIX, D), BF16), pltpu.VMEM((MIX, D), F32), pltpu.VMEM((rows_out, D), BF16),
                        pltpu.SemaphoreType.DMA],
        compiler_params=_params(("arbitrary",), 60),
    )(xs, meta_tile, y, w_out, final_g, target)


def _conv_bwd(proj, conv_out, dy, dw_w, ln_g, ln_b, pw_w, pw_b):
    R, E = proj.shape
    CW = pw_w.shape[0]
    tm = META_TILE
    NTL = R // tm
    order = _conv_order(NTL)
    phys = lambda i: order(NTL - 1 - i)
    halo_l = _halo_block(NTL, tm)
    halo = lambda i: halo_l(NTL - 1 - i)
    cb = (E - 3 * CW) // CW
    base = HALO - (CONV_K - 1)

    def body(a_ref, b_ref, g_ref, ah_ref, bh_ref, c_ref, dy_ref, w_ref, lg_ref, lb_ref, pw_ref, pb_ref,
             dp_ref, dpw_ref, dww_ref, vec_ref, u_ext, u_sh, dc_ext, dc_sh, du_scr, dww_acc, dpw_acc,
             xh_scr, rstd_scr, ln_scr, sg_scr, upw_scr, s_scr, dupw_scr):
        i = pl.program_id(0)

        @pl.when(i == 0)
        def _():
            dpw_acc[...] = jnp.zeros_like(dpw_acc)
            dww_ref[...] = jnp.zeros_like(dww_ref)
            vec_ref[...] = jnp.zeros_like(vec_ref)
            dww_acc[...] = jnp.zeros_like(dww_acc)
            dc_ext[tm:tm + HALO, :] = jnp.zeros((HALO, CW), F32)

        _fill_glu(i == NTL - 1, a_ref, b_ref, ah_ref, bh_ref, u_ext, tm)
        _fill_shifted(u_ext, u_sh)
        blocks = [slice(r0, r0 + ELEM_ROWS) for r0 in range(0, tm, ELEM_ROWS)]
        for rows in blocks:
            xh, rstd, ln = _layer_norm(c_ref[rows, :], lg_ref, lb_ref)
            sg = _sigmoid(ln)
            xh_scr[rows, :], rstd_scr[rows, :], ln_scr[rows, :], sg_scr[rows, :] = xh, rstd, ln, sg
            s_scr[rows, :] = (ln * sg).astype(BF16)
        upw_scr[...] = _dot(s_scr[...], pw_ref[...]) + pb_ref[...]
        col_sum = jnp.zeros((1, CW), F32)
        for rows in blocks:
            g = g_ref[rows, :].astype(F32)
            sgg = _sigmoid(g)
            dyc = dy_ref[rows, :].astype(F32)
            dp_ref[rows, 2 * CW:3 * CW] = (dyc * upw_scr[rows, :] * _dsilu(g, sgg)).astype(BF16)
            dupw = dyc * (g * sgg)
            dupw_scr[rows, :] = dupw.astype(BF16)
            col_sum = col_sum + jnp.sum(dupw, axis=0, keepdims=True)
        vec_ref[0:1, :] += col_sum
        dpw_acc[...] += _dot(s_scr[...], dupw_scr[...], TN)
        upw_scr[...] = _dot(dupw_scr[...], pw_ref[...], NT)
        sum_g, sum_b, sum_c = col_sum * 0.0, col_sum * 0.0, col_sum * 0.0
        for rows in blocks:
            xh, rstd = xh_scr[rows, :], rstd_scr[rows, :]
            dln = upw_scr[rows, :] * _dsilu(ln_scr[rows, :], sg_scr[rows, :])
            sum_g = sum_g + jnp.sum(dln * xh, axis=0, keepdims=True)
            sum_b = sum_b + jnp.sum(dln, axis=0, keepdims=True)
            dxh = dln * lg_ref[...]
            dc = rstd * (dxh - jnp.mean(dxh, axis=-1, keepdims=True)
                         - xh * jnp.mean(dxh * xh, axis=-1, keepdims=True))
            sum_c = sum_c + jnp.sum(dc, axis=0, keepdims=True)
            dc_ext[rows, :] = dc
        vec_ref[1:2, :] += sum_g
        vec_ref[2:3, :] += sum_b
        vec_ref[3:4, :] += sum_c
        _fill_shifted(dc_ext, dc_sh)

        for l0 in range(0, CW, CONV_LANES):
            lanes = slice(l0, l0 + CONV_LANES)
            for r0 in range(0, tm, CONV_ROWS):
                acc = None
                for k in range(CONV_K):
                    term = _at_offset(dc_ext, dc_sh, CONV_K - 1 - k, r0, CONV_ROWS, lanes) * w_ref[k:k + 1, lanes]
                    acc = term if acc is None else acc + term
                du_scr[r0:r0 + CONV_ROWS, lanes] = acc

        n_grp = tm // 8
        by_shift = [[(k, (base + k) // 8) for k in range(CONV_K) if (base + k) % 8 == r] for r in range(8)]
        for l0 in range(0, CW, LANE):
            lane = slice(l0, l0 + LANE)
            for r in range(8):
                src = u_ext if r == 0 else u_sh.at[r - 1]
                a_lo, a_hi = by_shift[r][0][1], by_shift[r][-1][1]
                sums = {k: None for k, _ in by_shift[r]}
                dcg = {}
                for gi in range(a_lo, n_grp + a_hi):
                    if gi - a_lo < n_grp:
                        dcg[gi - a_lo] = dc_ext[8 * (gi - a_lo):8 * (gi - a_lo) + 8, lane]
                    dcg.pop(gi - a_hi - 1, None)
                    ug = src[8 * gi:8 * gi + 8, lane]
                    for k, a in by_shift[r]:
                        if 0 <= gi - a < n_grp:
                            prod = dcg[gi - a] * ug
                            sums[k] = prod if sums[k] is None else sums[k] + prod
                for k, _ in by_shift[r]:
                    dww_acc[k, :, lane] += sums[k]

        for rows in blocks:
            du = du_scr[rows, :]
            sgb = _sigmoid(b_ref[rows, :].astype(F32))
            dp_ref[rows, 0:CW] = (du * sgb).astype(BF16)
            dp_ref[rows, CW:2 * CW] = (du * a_ref[rows, :].astype(F32) * sgb * (1.0 - sgb)).astype(BF16)
        dc_ext[tm:tm + HALO, :] = dc_ext[0:HALO, :]

        @pl.when(i == NTL - 1)
        def _():
            for k in range(CONV_K):
                dww_ref[k:k + 1, :] = jnp.sum(dww_acc[k], axis=0, keepdims=True)
            dpw_ref[...] = dpw_acc[...].astype(BF16)

    row = lambda i: (0, 0)
    return pl.pallas_call(
        body, name="conv_bwd",
        grid=(NTL,),
        in_specs=[pl.BlockSpec((tm, CW), lambda i: (phys(i), cb)),
                  pl.BlockSpec((tm, CW), lambda i: (phys(i), cb + 1)),
                  pl.BlockSpec((tm, CW), lambda i: (phys(i), cb + 2)),
                  pl.BlockSpec((HALO, CW), lambda i: (halo(i), cb)),
                  pl.BlockSpec((HALO, CW), lambda i: (halo(i), cb + 1)),
                  pl.BlockSpec((tm, CW), lambda i: (phys(i), 0)),
                  pl.BlockSpec((tm, CW), lambda i: (phys(i), 1)),
                  pl.BlockSpec((HALO, CW), row),
                  pl.BlockSpec((1, CW), row), pl.BlockSpec((1, CW), row),
                  pl.BlockSpec((CW, CW), row),
                  pl.BlockSpec((1, CW), row)],
        out_specs=[pl.BlockSpec((tm, 3 * CW), lambda i: (phys(i), 0)),
                   pl.BlockSpec((CW, CW), row),
                   pl.BlockSpec((HALO, CW), row),
                   pl.BlockSpec((8, CW), row)],
        out_shape=[jax.ShapeDtypeStruct((R, 3 * CW), BF16), jax.ShapeDtypeStruct((CW, CW), BF16),
                   jax.ShapeDtypeStruct((HALO, CW), F32), jax.ShapeDtypeStruct((8, CW), F32)],
        scratch_shapes=[pltpu.VMEM((HALO + tm, CW), F32), pltpu.VMEM((7, tm + HALO - 8, CW), F32),
                        pltpu.VMEM((tm + HALO, CW), F32), pltpu.VMEM((7, tm + HALO - 8, CW), F32),
                        pltpu.VMEM((tm, CW), F32), pltpu.VMEM((CONV_K, 8, CW), F32), pltpu.VMEM((CW, CW), F32),
                        pltpu.VMEM((tm, CW), F32), pltpu.VMEM((tm, 1), F32), pltpu.VMEM((tm, CW), F32),
                        pltpu.VMEM((tm, CW), F32), pltpu.VMEM((tm, CW), F32), pltpu.VMEM((tm, CW), BF16),
                        pltpu.VMEM((tm, CW), BF16)],
        compiler_params=_params(("arbitrary",), 60),
    )(proj, proj, proj, proj, proj, conv_out, dy, dw_w, ln_g, ln_b, pw_w, pw_b)


def _retention_bwd(proj, rope, gn_g, states, dy, ex, ex_parts):
    R, E = proj.shape
    RW = gn_g.shape[1]
    H = RET_HEADS
    hd = RW // H
    half = hd // 2
    NC = R // CHUNK
    mask, qd, kd, cd = _decay_tables(H)
    scale = float(hd) ** -0.5
    order = _chunk_order(NC)
    phys = lambda i: order(NC - 1 - i)

    def body(*refs):
        p_ref, cb_ref, sb_ref, ci_ref, si_ref, mask_ref, qd_ref, kd_ref, gn_ref, st_ref, dy_ref = refs[:11]
        ex_ins = refs[11:11 + ex.n]
        dp_ref, dgn_ref = refs[11 + ex.n:13 + ex.n]
        ex_outs = refs[13 + ex.n:13 + 2 * ex.n]
        dstate = refs[13 + 2 * ex.n]
        sems = refs[14 + 2 * ex.n:]

        @pl.when(pl.program_id(0) == 0)
        def _():
            ex.start(ex_ins, ex_outs, sems)
            dstate[...] = jnp.zeros_like(dstate)
            dgn_ref[...] = jnp.zeros_like(dgn_ref)

        cs, sn = _rope_chunk(cb_ref, sb_ref, ci_ref, si_ref)
        hs = range(H)
        col = lambda j, h: slice(j * RW + h * hd, j * RW + (h + 1) * hd)
        qr = [_rot(p_ref[:, col(0, h)].astype(F32), cs, sn, half) for h in hs]
        kr = [_rot(p_ref[:, col(1, h)].astype(F32), cs, sn, half) * scale for h in hs]
        v = [p_ref[:, col(2, h)] for h in hs]
        qb = [qr[h].astype(BF16) for h in hs]
        kb = [kr[h].astype(BF16) for h in hs]
        qdb = [(qr[h] * qd_ref[h]).astype(BF16) for h in hs]
        kdb = [(kr[h] * kd_ref[h]).astype(BF16) for h in hs]
        s_prev = [st_ref[0, h] for h in hs]
        dst = [dstate[h] for h in hs]
        dstb = [dst[h].astype(BF16) for h in hs]
        sb = [(_dot(qb[h], kb[h], NT) * mask_ref[h]).astype(BF16) for h in hs]
        y_raw = [_dot(sb[h], v[h]) + _dot(qdb[h], s_prev[h]) for h in hs]
        dyrb, dg = [], []
        for h in hs:
            g = p_ref[:, col(3, h)].astype(F32)
            mu = jnp.mean(y_raw[h], axis=-1, keepdims=True)
            yc = y_raw[h] - mu
            rstd = lax.rsqrt(jnp.mean(yc * yc, axis=-1, keepdims=True) + EPS)
            xh = yc * rstd
            gn = gn_ref[:, col(0, h)]
            sg = _sigmoid(g)
            dyh = dy_ref[:, col(0, h)].astype(F32)
            dg.append((dyh * (xh * gn) * _dsilu(g, sg)).astype(BF16))
            dyn = dyh * (g * sg)
            dgn_ref[:, col(0, h)] += jnp.sum(dyn * xh, axis=0, keepdims=True)
            dxh = dyn * gn
            dyr = rstd * (dxh - jnp.mean(dxh, axis=-1, keepdims=True)
                          - xh * jnp.mean(dxh * xh, axis=-1, keepdims=True))
            dyrb.append(dyr.astype(BF16))
        dsb = [(_dot(dyrb[h], v[h], NT) * mask_ref[h]).astype(BF16) for h in hs]
        dqr = [_dot(dsb[h], kb[h]) + _dot(dyrb[h], s_prev[h], NT) * qd_ref[h] for h in hs]
        dkr = [_dot(dsb[h], qb[h], TN) + _dot(v[h], dstb[h], NT) * kd_ref[h] for h in hs]
        dv = [_dot(sb[h], dyrb[h], TN) + _dot(kdb[h], dstb[h]) for h in hs]
        dst_new = [dst[h] * cd[h] + _dot(qdb[h], dyrb[h], TN) for h in hs]
        for h in hs:
            dstate[h] = dst_new[h]
            dp_ref[:, col(0, h)] = _rot_inv(dqr[h], cs, sn, half).astype(BF16)
            dp_ref[:, col(1, h)] = (_rot_inv(dkr[h], cs, sn, half) * scale).astype(BF16)
            dp_ref[:, col(2, h)] = dv[h].astype(BF16)
            dp_ref[:, col(3, h)] = dg[h]

        @pl.when(pl.program_id(0) == NC - 1)
        def _():
            ex.wait(ex_ins, ex_outs, sems)

    const3 = lambda i: (0, 0, 0)
    outs = pl.pallas_call(
        body, name="retention_bwd",
        grid=(NC,),
        in_specs=[pl.BlockSpec((CHUNK, 4 * RW), lambda i: (phys(i), 0)),
                  pl.BlockSpec((1, 1, half), lambda i: (phys(i), 0, 0)),
                  pl.BlockSpec((1, 1, half), lambda i: (phys(i), 0, 0)),
                  pl.BlockSpec((CHUNK, half), lambda i: (0, 0)),
                  pl.BlockSpec((CHUNK, half), lambda i: (0, 0)),
                  pl.BlockSpec((H, CHUNK, CHUNK), const3),
                  pl.BlockSpec((H, CHUNK, 1), const3),
                  pl.BlockSpec((H, CHUNK, 1), const3),
                  pl.BlockSpec((1, RW), lambda i: (0, 0)),
                  pl.BlockSpec((1, H, hd, hd), lambda i: (phys(i), 0, 0, 0)),
                  pl.BlockSpec((CHUNK, RW), lambda i: (phys(i), 0))] + ex.specs,
        out_specs=[pl.BlockSpec((CHUNK, 4 * RW), lambda i: (phys(i), 0)),
                   pl.BlockSpec((1, RW), lambda i: (0, 0))] + ex.specs,
        out_shape=[jax.ShapeDtypeStruct((R, 4 * RW), BF16), jax.ShapeDtypeStruct((1, RW), F32)] + ex.out_shape,
        scratch_shapes=[pltpu.VMEM((H, hd, hd), F32)] + ex.scratch,
        compiler_params=_params(("arbitrary",), 32),
    )(proj, *rope, jnp.asarray(mask), jnp.asarray(qd), jnp.asarray(kd), gn_g, states, dy, *ex_parts)
    return outs[0], outs[1], outs[2:]


def _dproj_specs(tk, tn, n_ret, tile_axis, col_axis):
    def ret_map(*ids):
        t, j = ids[tile_axis], ids[col_axis]
        return (jnp.where(j < n_ret, t, 0), jnp.minimum(j, n_ret - 1))

    def conv_map(*ids):
        t, j = ids[tile_axis], ids[col_axis]
        return (jnp.where(j >= n_ret, t, 0), jnp.maximum(j - n_ret, 0))

    return pl.BlockSpec((tk, tn), ret_map), pl.BlockSpec((tk, tn), conv_map)


def _w_in_grad(hn, dp_ret, dp_conv):
    R, D = dp_ret.shape[0], hn.shape[1]
    tn = _pick_tile(dp_conv.shape[1] // 3, 1024, 128)
    n_ret, n_conv = dp_ret.shape[1] // tn, dp_conv.shape[1] // tn
    E = dp_ret.shape[1] + dp_conv.shape[1]
    tk = _pick_tile(R, 1024, MXU_DIM)
    n_t = R // tk
    ret_spec, conv_spec = _dproj_specs(tk, tn, n_ret, 1, 0)

    def body(hn_ref, r_ref, c_ref, out_ref, acc):
        j, t = pl.program_id(0), pl.program_id(1)

        @pl.when(t == 0)
        def _():
            acc[...] = jnp.zeros_like(acc)

        @pl.when(j < n_ret)
        def _():
            acc[...] += _dot(hn_ref[...], r_ref[...], TN)

        @pl.when(j >= n_ret)
        def _():
            acc[...] += _dot(hn_ref[...], c_ref[...], TN)

        @pl.when(t == n_t - 1)
        def _():
            out_ref[...] = acc[...].astype(BF16)

    return pl.pallas_call(
        body, name="w_in_grad",
        grid=(n_ret + n_conv, n_t),
        in_specs=[pl.BlockSpec((tk, D), lambda j, t: (t, 0)), ret_spec, conv_spec],
        out_specs=pl.BlockSpec((D, tn), lambda j, t: (0, j)),
        out_shape=jax.ShapeDtypeStruct((D, E), BF16),
        scratch_shapes=[pltpu.VMEM((D, tn), F32)],
        compiler_params=_params(("arbitrary", "arbitrary"), 48),
    )(hn, dp_ret, dp_conv)


def _h_grad(dp_ret, dp_conv, w_in, xs, meta_tile, dh2, ln_g, ex, ex_parts):
    R, D = dh2.shape
    te = CHUNK
    n_x = xs.shape[0] // te
    n_m = meta_tile.shape[0] // te
    tn = _pick_tile(dp_conv.shape[1] // 3, 1024, 128)
    n_ret, n_conv = dp_ret.shape[1] // tn, dp_conv.shape[1] // tn
    n_k = n_ret + n_conv
    tm = _pick_tile(R, 1024, meta_tile.shape[0])
    n_e = tm // te
    n_t = R // tm
    assert n_e <= n_k and (n_x + n_m) * te == R

    def ret_map(t, k):
        return (jnp.where(k < n_ret, jnp.minimum(t, n_t - 1), 0), jnp.minimum(k, n_ret - 1))

    def conv_map(t, k):
        return (jnp.where(k >= n_ret, jnp.minimum(t, n_t - 1), 0), jnp.maximum(k - n_ret, 0))

    def row_block(t, k):
        return jnp.maximum(t - 1, 0) * n_e + jnp.where(t > 0, jnp.minimum(k, n_e - 1), 0)

    def body(*refs):
        r_ref, c_ref, w_hbm, w_ref, x_ref, mt_ref, dh2_ref, g_ref = refs[:8]
        ex_ins = refs[8:8 + ex.n]
        o = 8 + ex.n
        dh_ref, dlg_ref = refs[o:o + 2]
        ex_outs = refs[o + 2:o + 2 + ex.n]
        acc, w_keep, keep_sems = refs[o + 2 + ex.n:o + 5 + ex.n]
        sems = refs[o + 5 + ex.n:]
        t, k = pl.program_id(0), pl.program_id(1)
        cur, old = t % 2, (t + 1) % 2

        def keep(j):
            return pltpu.make_async_copy(w_hbm.at[:, j * tn:(j + 1) * tn], w_keep.at[j], keep_sems.at[j])

        @pl.when((k == 0) & (t == 0))
        def _():
            for j in range(n_ret):
                keep(j).start()
            ex.start(ex_ins, ex_outs, sems)
            dlg_ref[...] = jnp.zeros_like(dlg_ref)

        for j in range(n_ret):
            @pl.when((k == j) & (t == 0))
            def _():
                keep(j).wait()

        @pl.when((k == 0) & (t < n_t))
        def _():
            acc[cur] = _dot(r_ref[...], w_keep[0], NT)

        @pl.when((k > 0) & (k < n_ret) & (t < n_t))
        def _():
            acc[cur] += _dot(r_ref[...], w_keep[k], NT)

        @pl.when((k >= n_ret) & (t < n_t))
        def _():
            acc[cur] += _dot(c_ref[...], w_ref[...], NT)

        @pl.when((k < n_e) & (t > 0))
        def _():
            hv = jnp.where(row_block(t, k) < n_x, x_ref[...], mt_ref[...])
            r = lax.rsqrt(jnp.mean(hv * hv, axis=-1, keepdims=True) + EPS)
            nrm = hv * r
            dhn = acc[old, pl.ds(pl.multiple_of(k * te, te), te), :]
            dlg_ref[...] += jnp.sum(dhn * nrm, axis=0, keepdims=True)
            dn = dhn * g_ref[...]
            dh_ref[...] = dh2_ref[...] + r * (dn - nrm * jnp.mean(dn * nrm, axis=-1, keepdims=True))

        @pl.when((k == n_k - 1) & (t == n_t))
        def _():
            ex.wait(ex_ins, ex_outs, sems)

    row = lambda t, k: (0, 0)
    outs = pl.pallas_call(
        body, name="h_grad",
        grid=(n_t + 1, n_k),
        in_specs=[pl.BlockSpec((tm, tn), ret_map), pl.BlockSpec((tm, tn), conv_map),
                  pl.BlockSpec(memory_space=pl.ANY),
                  pl.BlockSpec((D, tn), lambda t, k: (0, jnp.maximum(k, n_ret))),
                  pl.BlockSpec((te, D), lambda t, k: (jnp.minimum(row_block(t, k), n_x - 1), 0)),
                  pl.BlockSpec((te, D), lambda t, k: (jnp.clip(row_block(t, k) - n_x, 0, n_m - 1), 0)),
                  pl.BlockSpec((te, D), lambda t, k: (row_block(t, k), 0)),
                  pl.BlockSpec((1, D), row)] + ex.specs,
        out_specs=[pl.BlockSpec((te, D), lambda t, k: (row_block(t, k), 0)),
                   pl.BlockSpec((1, D), row)] + ex.specs,
        out_shape=[jax.ShapeDtypeStruct((R, D), F32), jax.ShapeDtypeStruct((1, D), F32)] + ex.out_shape,
        scratch_shapes=[pltpu.VMEM((2, tm, D), F32), pltpu.VMEM((n_ret, D, tn), BF16),
                        pltpu.SemaphoreType.DMA((n_ret,))] + ex.scratch,
        compiler_params=_params(("arbitrary", "arbitrary"), 60),
    )(dp_ret, dp_conv, w_in, w_in, xs, meta_tile, dh2, ln_g, *ex_parts)
    return outs[0], outs[1], outs[2:]


def _adamw(w, g, m, v):
    m = ADAM_B1 * m + (1.0 - ADAM_B1) * g
    v = ADAM_B2 * v + (1.0 - ADAM_B2) * (g * g)
    m_hat = m / (1.0 - ADAM_B1 ** ADAM_STEP)
    v_hat = v / (1.0 - ADAM_B2 ** ADAM_STEP)
    delta = -ADAM_LR * (m_hat / (jnp.sqrt(v_hat) + ADAM_EPS) + ADAM_WD * w)
    return delta, m, v


def _sum_slots(ref):
    g = ref[0].astype(F32)
    for s in range(1, N_DEV):
        g = g + ref[s].astype(F32)
    return g


def _sum_adamw(name, parts, w, m, v, rows_target, ex=None, ex_parts=()):
    rows, cols = w.shape
    tr = _pick_tile(rows, rows_target, 8)
    n_ex = 0 if ex is None else ex.n
    n_steps = rows // tr

    def body(*refs):
        p_ref, w_ref, m_ref, v_ref = refs[:4]
        ex_ins = refs[4:4 + n_ex]
        o = 4 + n_ex
        g_ref, d_ref, nm_ref, nv_ref = refs[o:o + 4]
        ex_outs, sems = refs[o + 4:o + 4 + n_ex], refs[o + 4 + n_ex:]
        if ex is not None:
            @pl.when(pl.program_id(0) == 0)
            def _():
                ex.start(ex_ins, ex_outs, sems)

        g = _sum_slots(p_ref)
        d, nm, nv = _adamw(w_ref[...], g, m_ref[...], v_ref[...])
        g_ref[...] = g
        d_ref[...] = d
        nm_ref[...] = nm
        nv_ref[...] = nv
        if ex is not None:
            @pl.when(pl.program_id(0) == n_steps - 1)
            def _():
                ex.wait(ex_ins, ex_outs, sems)

    tile = pl.BlockSpec((tr, cols), lambda i: (i, 0))
    ex_specs, ex_shape, ex_scratch = ([], [], []) if ex is None else (ex.specs, ex.out_shape, ex.scratch)
    outs = pl.pallas_call(
        body, name=name,
        grid=(n_steps,),
        in_specs=[pl.BlockSpec((N_DEV, tr, cols), lambda i: (0, i, 0)), tile, tile, tile] + ex_specs,
        out_specs=[tile] * 4 + ex_specs,
        out_shape=[jax.ShapeDtypeStruct((rows, cols), F32)] * 4 + ex_shape,
        scratch_shapes=ex_scratch,
        compiler_params=_params(("arbitrary",), 40),
    )(parts, w, m, v, *ex_parts)
    return outs[:4], outs[4:]


def _sum_adamw_small(parts_list, w_list, m_list, v_list, loss_parts):
    n = len(w_list)

    def body(*refs):
        p_refs, w_refs, m_refs, v_refs = refs[:n], refs[n:2 * n], refs[2 * n:3 * n], refs[3 * n:4 * n]
        lp_ref = refs[4 * n]
        outs = refs[4 * n + 1:]
        for a in range(n):
            g = _sum_slots(p_refs[a])
            d, nm, nv = _adamw(w_refs[a][...], g, m_refs[a][...], v_refs[a][...])
            outs[4 * a][...] = g
            outs[4 * a + 1][...] = d
            outs[4 * a + 2][...] = nm
            outs[4 * a + 3][...] = nv
        outs[4 * n][...] = _sum_slots(lp_ref)

    out_shape = []
    for w in w_list:
        out_shape += [jax.ShapeDtypeStruct(w.shape, F32)] * 4
    out_shape.append(jax.ShapeDtypeStruct(loss_parts.shape[1:], F32))
    return pl.pallas_call(body, name="sum_adamw_small", out_shape=out_shape)(
        *parts_list, *w_list, *m_list, *v_list, loss_parts)


def kernel(x, meta_tokens, ln_g, w_in, ret_gn_g, conv_dw_w, conv_dw_b, conv_ln_g, conv_ln_b, conv_pw_w, conv_pw_b, w_out, final_g, loss_target, m_meta_tokens, m_ln_g, m_w_in, m_ret_gn_g, m_conv_dw_w, m_conv_dw_b, m_conv_ln_g, m_conv_ln_b, m_conv_pw_w, m_conv_pw_b, m_w_out, m_final_g, v_meta_tokens, v_ln_g, v_w_in, v_ret_gn_g, v_conv_dw_w, v_conv_dw_b, v_conv_ln_g, v_conv_ln_b, v_conv_pw_w, v_conv_pw_b, v_w_out, v_final_g):
    _, SEQ, D = x.shape
    MIX = w_out.shape[2]
    RW = ret_gn_g.shape[1]
    CW = conv_pw_b.shape[1]
    assert RW == CW and MIX == RW + CW and SEQ % META_TILE == 0 and CONV_K - 1 <= HALO
    R = SEQ + META_TILE
    hd = RW // RET_HEADS
    half = hd // 2
    me = 4 * lax.axis_index("x") + 2 * lax.axis_index("y") + lax.axis_index("c")

    dw_pad = jnp.pad(conv_dw_w[0], ((0, HALO - CONV_K), (0, 0)))

    n_seq_chunks = SEQ // CHUNK
    base = jnp.concatenate([jnp.arange(n_seq_chunks, dtype=F32) * CHUNK + N_META,
                            jnp.zeros((META_TILE // CHUNK - 1,), F32), jnp.full((1,), N_META - CHUNK, F32)])
    inv_freq = ROPE_BASE ** (-jnp.arange(half, dtype=F32) / half)
    ang_base = (base[:, None] * inv_freq[None, :])[:, None, :]
    ang_row = jnp.arange(CHUNK, dtype=F32)[:, None] * inv_freq[None, :]
    rope = (jnp.cos(ang_base), jnp.sin(ang_base), jnp.cos(ang_row), jnp.sin(ang_row))

    xs = x[0]
    target = loss_target[0]
    final_g2 = final_g[None, :]

    small_shards = [meta_tokens, dw_pad]
    hn, meta_tile, (_, dw_g) = _rms_norm(xs, meta_tokens, ln_g, _Exchange(small_shards, [None, None]), small_shards)
    dw_g = jnp.swapaxes(dw_g, 0, 1).reshape(HALO, CW)
    pw_shard, w_out_shard = [conv_pw_w[0].astype(BF16)], [w_out[0].astype(BF16)]
    proj, w_in_g, (pw_g,) = _in_proj_gather(hn, R, w_in[0].astype(BF16), _Exchange(pw_shard, [None]), pw_shard)
    pw_g = pw_g.reshape(CW, CW)
    y, states = _retention_fwd(proj, rope, ret_gn_g, MIX)
    y, conv_out, (w_out_g,) = _conv_fwd(proj, y, dw_g, conv_dw_b, conv_ln_g, conv_ln_b, pw_g, conv_pw_b,
                                        _Exchange(w_out_shard, [None]), w_out_shard)
    w_out_g = w_out_g.reshape(MIX, D)
    dh2, dy, dwo_p, dfg_p, loss_p = _out_proj_loss(xs, meta_tile, y, w_out_g, final_g2, target)

    dp_conv, dpw_p, dww_p, cvec_p = _conv_bwd(proj, conv_out, dy, dw_g, conv_ln_g, conv_ln_b, pw_g, conv_pw_b)
    dp_ret, dgn_p, (r_wo, r_pw) = _retention_bwd(proj, rope, ret_gn_g, states, dy,
                                                 _Exchange([dwo_p, dpw_p], [0, 0]), [dwo_p, dpw_p])
    dwi_p = _w_in_grad(hn, dp_ret, dp_conv)
    dh, dlg_p, (r_wi,) = _h_grad(dp_ret, dp_conv, w_in_g, xs, meta_tile, dh2, ln_g, _Exchange([dwi_p], [1]), [dwi_p])
    grad_x = dh[:SEQ][None]

    def at_row(r, a, b=None):
        v = a if b is None else jnp.concatenate([a, b], axis=1)
        return jnp.pad(v, ((r, 7 - r), (0, D - v.shape[1])))
    vec8 = (at_row(0, dlg_p) + at_row(1, dfg_p)
            + at_row(2, dgn_p, cvec_p[3:4])
            + at_row(3, cvec_p[1:2], cvec_p[2:3])
            + at_row(4, cvec_p[0:1], jnp.broadcast_to(loss_p, (1, CW))))
    small = jnp.concatenate([dh[R - N_META:], vec8,
                             jnp.zeros((SMALL_ROWS - N_META - 8, D), F32)], axis=0)

    (g_wi, d_wi, nm_wi, nv_wi), (r_dww, r_small) = _sum_adamw(
        "sum_adamw_w_in", r_wi, w_in[0], m_w_in[0], v_w_in[0], 256,
        ex=_Exchange([dww_p, small], [1, None]), ex_parts=[dww_p, small])
    (g_wo, d_wo, nm_wo, nv_wo), _ = _sum_adamw("sum_adamw_w_out", r_wo, w_out[0], m_w_out[0], v_w_out[0], 128)
    (g_pw, d_pw, nm_pw, nv_pw), _ = _sum_adamw("sum_adamw_pw", r_pw, conv_pw_w[0], m_conv_pw_w[0], v_conv_pw_w[0], 128)

    dcol = D // N_DEV
    sm = lambda r0, nr, c0, nc: lax.slice(r_small, (0, r0, c0), (N_DEV, r0 + nr, c0 + nc))
    meta_parts = lax.dynamic_slice(r_small, (0, 0, me * dcol), (N_DEV, N_META, dcol))
    small_parts = [meta_parts, sm(16, 1, 0, D), sm(18, 1, 0, RW), r_dww, sm(18, 1, RW, CW),
                   sm(19, 1, 0, CW), sm(19, 1, CW, CW), sm(20, 1, 0, CW), sm(17, 1, 0, D)]
    pad31 = lambda a: jnp.pad(a, ((0, HALO - CONV_K), (0, 0)))
    ws = [meta_tokens, ln_g, ret_gn_g, pad31(conv_dw_w[0]), conv_dw_b, conv_ln_g, conv_ln_b, conv_pw_b, final_g2]
    ms = [m_meta_tokens, m_ln_g, m_ret_gn_g, pad31(m_conv_dw_w[0]), m_conv_dw_b, m_conv_ln_g, m_conv_ln_b,
          m_conv_pw_b, m_final_g[None, :]]
    vs = [v_meta_tokens, v_ln_g, v_ret_gn_g, pad31(v_conv_dw_w[0]), v_conv_dw_b, v_conv_ln_g, v_conv_ln_b,
          v_conv_pw_b, v_final_g[None, :]]
    loss_parts = sm(20, 1, CW, 1)
    outs = _sum_adamw_small(small_parts, ws, ms, vs, loss_parts)
    loss = outs[-1][0, 0]
    quad = [outs[4 * a:4 * a + 4] for a in range(len(ws))]
    (q_meta, q_lng, q_gn, q_dww, q_dwb, q_clg, q_clb, q_pwb, q_fg) = quad
    q_dww = [t[:CONV_K][None] for t in q_dww]
    q_fg = [t[0] for t in q_fg]
    q_wi = [t[None] for t in (g_wi, d_wi, nm_wi, nv_wi)]
    q_wo = [t[None] for t in (g_wo, d_wo, nm_wo, nv_wo)]
    q_pw = [t[None] for t in (g_pw, d_pw, nm_pw, nv_pw)]

    per_w = [q_meta, q_lng, q_wi, q_gn, q_dww, q_dwb, q_clg, q_clb, q_pw, q_pwb, q_wo, q_fg]
    result = [loss, grad_x]
    for which in range(4):
        result += [q[which] for q in per_w]
    return tuple(result)
```

```python
import numpy as np
import jax
import jax.numpy as jnp
from jax import lax
from jax.experimental import pallas as pl
from jax.experimental.pallas import tpu as pltpu

N_META = 16
RET_HEADS = 4
CONV_K = 31
CHUNK = 128
ROPE_BASE = 10000.0
EPS = 1e-6
ADAM_LR = 0.001
ADAM_B1 = 0.9
ADAM_B2 = 0.999
ADAM_EPS = 1e-08
ADAM_WD = 0.01
ADAM_STEP = 10

N_DEV = 8
META_TILE = 256
HALO = 32
SMALL_ROWS = 32
VMEM_BYTES_V7X = 64 * 1024 * 1024
MXU_DIM = 256

F32 = jnp.float32
BF16 = jnp.bfloat16
MESH = pl.DeviceIdType.MESH

NN = (((1,), (0,)), ((), ()))
NT = (((1,), (1,)), ((), ()))
TN = (((0,), (0,)), ((), ()))


def _dot(a, b, dims=NN):
    return lax.dot_general(a, b, dims, preferred_element_type=F32)


def _pick_tile(n, target, mult=16):
    best = None
    for t in range(mult, min(n, target) + 1, mult):
        if n % t == 0:
            best = t
    assert best is not None, (n, target)
    return best


def _params(sem=None, vmem_mb=None):
    kw = {}
    if sem is not None:
        kw["dimension_semantics"] = sem
    if vmem_mb is not None:
        kw["vmem_limit_bytes"] = min(vmem_mb * 1024 * 1024, VMEM_BYTES_V7X - 4 * 1024 * 1024)
    return pltpu.CompilerParams(**kw)


def _sigmoid(x):
    return jax.nn.sigmoid(x)


def _dsilu(x, sg):
    return sg * (1.0 + x * (1.0 - sg))


def _decay_tables(heads):
    h = np.arange(heads, dtype=np.float32)
    gamma = (1.0 - np.exp2(-5.0 - h)).astype(np.float32)
    log_g = np.log(gamma).astype(np.float32)
    idx = np.arange(CHUNK, dtype=np.float32)
    rel = idx[:, None] - idx[None, :]
    mask = np.where(rel[None] >= 0, np.exp(np.maximum(rel, 0.0)[None] * log_g[:, None, None]), 0.0)
    qd = np.exp((idx[None, :] + 1.0) * log_g[:, None])
    kd = np.exp((CHUNK - 1.0 - idx[None, :]) * log_g[:, None])
    cd = np.exp(CHUNK * log_g)
    return (mask.astype(np.float32), qd.astype(np.float32)[:, :, None], kd.astype(np.float32)[:, :, None],
            [float(c) for c in cd.astype(np.float32)])


class _Exchange:
    def __init__(self, parts, block_axes):
        self.block_axes = list(block_axes)
        self.n = len(parts)
        self.out_shape = []
        for p, ax in zip(parts, block_axes):
            shp = list(p.shape)
            if ax is not None:
                assert shp[ax] % N_DEV == 0
                shp[ax] //= N_DEV
            self.out_shape.append(jax.ShapeDtypeStruct((N_DEV, *shp), p.dtype))
        self.scratch = [pltpu.SemaphoreType.DMA((self.n, N_DEV - 1)), pltpu.SemaphoreType.DMA((self.n, N_DEV - 1)),
                        pltpu.SemaphoreType.DMA((self.n,))]
        self.specs = [pl.BlockSpec(memory_space=pl.ANY)] * self.n

    def _copies(self, ins, outs, sems):
        send_sems, recv_sems, local_sems = sems
        x, y, c = lax.axis_index("x"), lax.axis_index("y"), lax.axis_index("c")
        me_idx = 4 * x + 2 * y + c

        def src_block(a, dev_idx):
            ax = self.block_axes[a]
            if ax is None:
                return ins[a]
            n = ins[a].shape[ax] // N_DEV
            idx = [slice(None)] * len(ins[a].shape)
            idx[ax] = pl.ds(pl.multiple_of(dev_idx * n, n), n)
            return ins[a].at[tuple(idx)]

        local = [pltpu.make_async_copy(src_block(a, me_idx), outs[a].at[me_idx], local_sems.at[a])
                 for a in range(self.n)]
        remote = []
        for m in range(1, N_DEV):
            px, py, pc = x ^ ((m >> 2) & 1), y ^ ((m >> 1) & 1), c ^ (m & 1)
            for a in range(self.n):
                remote.append(pltpu.make_async_remote_copy(
                    src_ref=src_block(a, 4 * px + 2 * py + pc), dst_ref=outs[a].at[me_idx],
                    send_sem=send_sems.at[a, m - 1], recv_sem=recv_sems.at[a, m - 1],
                    device_id=(px, py, pc), device_id_type=MESH))
        return local, remote

    def start(self, ins, outs, sems):
        local, remote = self._copies(ins, outs, sems)
        for cp in local + remote:
            cp.start()

    def wait(self, ins, outs, sems):
        local, remote = self._copies(ins, outs, sems)
        for cp in remote:
            cp.wait_recv()
        for cp in remote:
            cp.wait_send()
        for cp in local:
            cp.wait()


def _rms_norm(xs, meta_shard, ln_g, ex, ex_parts):
    SEQ, D = xs.shape
    n_meta, dcol = meta_shard.shape
    tm = 2 * META_TILE
    n_seq = SEQ // tm

    def body(*refs):
        x_ref, g_ref = refs[:2]
        ex_ins = refs[2:2 + ex.n]
        hn_ref, mt_ref = refs[2 + ex.n:4 + ex.n]
        ex_outs = refs[4 + ex.n:4 + 2 * ex.n]
        slots, slot_sem = refs[4 + 2 * ex.n:6 + 2 * ex.n]
        sems = refs[6 + 2 * ex.n:]
        i = pl.program_id(0)

        @pl.when(i == 0)
        def _():
            ex.start(ex_ins, ex_outs, sems)

        def norm(hv):
            r = lax.rsqrt(jnp.mean(hv * hv, axis=-1, keepdims=True) + EPS)
            return (hv * r * g_ref[...]).astype(BF16)

        @pl.when(i < n_seq)
        def _():
            hn_ref[...] = norm(x_ref[...])

        @pl.when(i == n_seq)
        def _():
            ex.wait(ex_ins, ex_outs, sems)
            cp = pltpu.make_async_copy(ex_outs[0], slots, slot_sem)
            cp.start()
            mt_ref[...] = jnp.zeros_like(mt_ref)
            cp.wait()
            for s in range(N_DEV):
                mt_ref[META_TILE - n_meta:META_TILE, s * dcol:(s + 1) * dcol] = slots[s]
            hn_ref[0:META_TILE, :] = norm(mt_ref[...])

    outs = pl.pallas_call(
        body, name="rms_norm",
        grid=(n_seq + 1,),
        in_specs=[pl.BlockSpec((tm, D), lambda i: (jnp.minimum(i, n_seq - 1), 0)),
                  pl.BlockSpec((1, D), lambda i: (0, 0))] + ex.specs,
        out_specs=[pl.BlockSpec((tm, D), lambda i: (i, 0)),
                   pl.BlockSpec((META_TILE, D), lambda i: (0, 0))] + ex.specs,
        out_shape=[jax.ShapeDtypeStruct((SEQ + tm, D), BF16), jax.ShapeDtypeStruct((META_TILE, D), F32)]
                  + ex.out_shape,
        scratch_shapes=[pltpu.VMEM((N_DEV, n_meta, dcol), F32), pltpu.SemaphoreType.DMA] + ex.scratch,
        compiler_params=_params(("arbitrary",), 32),
    )(xs, ln_g, *ex_parts)
    return outs[0], outs[1], outs[2:]


def _chip_visited(q):
    mine = 2 * lax.axis_index("x") + lax.axis_index("y")
    return mine ^ (((q & 1) << 1) | (q >> 1))


def _in_proj_gather(hn, n_rows, w_shard, ex, ex_parts):
    R, D = n_rows, hn.shape[1]
    wb = w_shard.shape[1]
    E, tn = wb * N_DEV, 2 * wb
    n_q = N_DEV // 2
    tm = _pick_tile(R, min(1056, R // 2))
    n_i = R // tm

    def body(*refs):
        hn_ref, wsh_hbm = refs[:2]
        ex_ins = refs[2:2 + ex.n]
        proj_ref, wg_hbm = refs[2 + ex.n:4 + ex.n]
        ex_outs = refs[4 + ex.n:4 + 2 * ex.n]
        w_vmem, send_sems, recv_sems, local_sem, vmem_sems = refs[4 + 2 * ex.n:9 + 2 * ex.n]
        ex_sems = refs[9 + 2 * ex.n:]
        q, i = pl.program_id(0), pl.program_id(1)
        x, y, c = lax.axis_index("x"), lax.axis_index("y"), lax.axis_index("c")
        me, sibling = (x, y, c), (x, y, 1 - c)
        chips = [(1 - x, y), (x, 1 - y), (1 - x, 1 - y)]

        def block(dev):
            return wg_hbm.at[:, pl.ds(pl.multiple_of((4 * dev[0] + 2 * dev[1] + dev[2]) * wb, wb), wb)]

        def copy(k, dev, to, src=None):
            return pltpu.make_async_remote_copy(
                src_ref=block(dev) if src is None else src, dst_ref=block(dev),
                send_sem=send_sems.at[k], recv_sem=recv_sems.at[k], device_id=to, device_id_type=MESH)

        def to_vmem(p):
            cols = pl.ds(pl.multiple_of(_chip_visited(p) * tn, tn), tn)
            return pltpu.make_async_copy(wg_hbm.at[:, cols], w_vmem.at[p % 2], vmem_sems.at[p % 2])

        mine = pltpu.make_async_copy(wsh_hbm, block(me), local_sem)
        first = [copy(0, me, sibling, src=wsh_hbm)] + [copy(1 + j, me, (*chip, c), src=wsh_hbm)
                                                       for j, chip in enumerate(chips)]
        passed = [copy(4 + j, (*chip, c), sibling) for j, chip in enumerate(chips)]

        @pl.when((q == 0) & (i == 0))
        def _():
            mine.start()
            for cp in first:
                cp.start()
            ex.start(ex_ins, ex_outs, ex_sems)
            mine.wait()
            copy(0, sibling, me).wait_recv()
            to_vmem(0).start()
            to_vmem(0).wait()

        for p in range(1, n_q):
            chip = chips[p - 1]

            @pl.when((q == p - 1) & (i == n_i - 2))
            def _():
                copy(p, (*chip, c), me).wait_recv()
                passed[p - 1].start()

            @pl.when((q == p - 1) & (i == n_i - 1))
            def _():
                copy(3 + p, (*chip, 1 - c), me).wait_recv()
                to_vmem(p).start()

            @pl.when((q == p) & (i == 0))
            def _():
                to_vmem(p).wait()

        proj_ref[...] = _dot(hn_ref[...], w_vmem[q % 2]).astype(BF16)

        @pl.when((q == n_q - 1) & (i == n_i - 1))
        def _():
            for cp in first + passed:
                cp.wait_send()
            ex.wait(ex_ins, ex_outs, ex_sems)

    hbm = pl.BlockSpec(memory_space=pl.ANY)
    outs = pl.pallas_call(
        body, name="in_proj",
        grid=(n_q, n_i),
        in_specs=[pl.BlockSpec((tm, D), lambda q, i: (i, 0)), hbm] + ex.specs,
        out_specs=[pl.BlockSpec((tm, tn), lambda q, i: (i, _chip_visited(q))), hbm] + ex.specs,
        out_shape=[jax.ShapeDtypeStruct((R, E), BF16), jax.ShapeDtypeStruct((D, E), BF16)] + ex.out_shape,
        scratch_shapes=[pltpu.VMEM((2, D, tn), BF16), pltpu.SemaphoreType.DMA((7,)), pltpu.SemaphoreType.DMA((7,)),
                        pltpu.SemaphoreType.DMA, pltpu.SemaphoreType.DMA((2,))] + ex.scratch,
        compiler_params=_params(("arbitrary", "arbitrary"), 48),
    )(hn, w_shard, *ex_parts)
    return outs[0], outs[1], outs[2:]


def _rope_chunk(cb_ref, sb_ref, ci_ref, si_ref):
    cb, sb, ci, si = cb_ref[0], sb_ref[0], ci_ref[...], si_ref[...]
    return cb * ci - sb * si, sb * ci + cb * si


def _rot(t, cos, sin, half):
    t1, t2 = t[:, :half], t[:, half:]
    return jnp.concatenate([t1 * cos - t2 * sin, t1 * sin + t2 * cos], axis=-1)


def _rot_inv(t, cos, sin, half):
    t1, t2 = t[:, :half], t[:, half:]
    return jnp.concatenate([t1 * cos + t2 * sin, t2 * cos - t1 * sin], axis=-1)


def _chunk_order(n_chunks):
    lead = META_TILE // CHUNK
    return lambda l: (l + n_chunks - lead) % n_chunks


def _retention_fwd(proj, rope, gn_g, mix):
    R, E = proj.shape
    RW = gn_g.shape[1]
    H = RET_HEADS
    hd = RW // H
    half = hd // 2
    NC = R // CHUNK
    mask, qd, kd, cd = _decay_tables(H)
    scale = float(hd) ** -0.5
    phys = _chunk_order(NC)

    def body(p_ref, cb_ref, sb_ref, ci_ref, si_ref, mask_ref, qd_ref, kd_ref, gn_ref, y_ref, st_ref, state):
        @pl.when(pl.program_id(0) == 0)
        def _():
            state[...] = jnp.zeros_like(state)

        cs, sn = _rope_chunk(cb_ref, sb_ref, ci_ref, si_ref)
        hs = range(H)
        col = lambda j, h: slice(j * RW + h * hd, j * RW + (h + 1) * hd)
        qr = [_rot(p_ref[:, col(0, h)].astype(F32), cs, sn, half) for h in hs]
        kr = [_rot(p_ref[:, col(1, h)].astype(F32), cs, sn, half) * scale for h in hs]
        v = [p_ref[:, col(2, h)] for h in hs]
        s_prev = [state[h] for h in hs]
        s_prev_b = [s_prev[h].astype(BF16) for h in hs]
        s = [(_dot(qr[h].astype(BF16), kr[h].astype(BF16), NT) * mask_ref[h]).astype(BF16) for h in hs]
        y_raw = [_dot(s[h], v[h]) + _dot((qr[h] * qd_ref[h]).astype(BF16), s_prev_b[h]) for h in hs]
        s_new = [s_prev[h] * cd[h] + _dot((kr[h] * kd_ref[h]).astype(BF16), v[h], TN) for h in hs]
        for h in hs:
            st_ref[0, h] = s_prev_b[h]
            state[h] = s_new[h]
        for h in hs:
            g = p_ref[:, col(3, h)].astype(F32)
            mu = jnp.mean(y_raw[h], axis=-1, keepdims=True)
            yc = y_raw[h] - mu
            var = jnp.mean(yc * yc, axis=-1, keepdims=True)
            out = yc * lax.rsqrt(var + EPS) * gn_ref[:, col(0, h)] * (g * _sigmoid(g))
            y_ref[:, col(0, h)] = out.astype(BF16)

    const3 = lambda l: (0, 0, 0)
    return pl.pallas_call(
        body, name="retention_fwd",
        grid=(NC,),
        in_specs=[pl.BlockSpec((CHUNK, 4 * RW), lambda l: (phys(l), 0)),
                  pl.BlockSpec((1, 1, half), lambda l: (phys(l), 0, 0)),
                  pl.BlockSpec((1, 1, half), lambda l: (phys(l), 0, 0)),
                  pl.BlockSpec((CHUNK, half), lambda l: (0, 0)),
                  pl.BlockSpec((CHUNK, half), lambda l: (0, 0)),
                  pl.BlockSpec((H, CHUNK, CHUNK), const3),
                  pl.BlockSpec((H, CHUNK, 1), const3),
                  pl.BlockSpec((H, CHUNK, 1), const3),
                  pl.BlockSpec((1, RW), lambda l: (0, 0))],
        out_specs=[pl.BlockSpec((CHUNK, RW), lambda l: (phys(l), 0)),
                   pl.BlockSpec((1, H, hd, hd), lambda l: (phys(l), 0, 0, 0))],
        out_shape=[jax.ShapeDtypeStruct((R, mix), BF16), jax.ShapeDtypeStruct((NC, H, hd, hd), BF16)],
        scratch_shapes=[pltpu.VMEM((H, hd, hd), F32)],
        compiler_params=_params(("arbitrary",), 32),
    )(proj, *rope, jnp.asarray(mask), jnp.asarray(qd), jnp.asarray(kd), gn_g)


CONV_ROWS = 64
CONV_LANES = 128
LANE = 128
ELEM_ROWS = 32


def _conv_order(n_tiles):
    return lambda l: (l + n_tiles - 1) % n_tiles


def _halo_block(n_tiles, tm):
    per = tm // HALO
    return lambda l: ((l + n_tiles - 2) % n_tiles) * per + per - 1


def _fill_shifted(src, dst):
    rows, width = dst.shape[1], dst.shape[2]
    step = _pick_tile(rows, 64, 8)
    for r in range(1, 8):
        for r0 in range(0, rows, step):
            for l0 in range(0, width, CONV_LANES):
                dst[r - 1, r0:r0 + step, l0:l0 + CONV_LANES] = src[r + r0:r + r0 + step, l0:l0 + CONV_LANES]


def _at_offset(src, shifted, off, r0, rows, lanes):
    r = off % 8
    a = off - r + r0
    if r == 0:
        return src[a:a + rows, lanes]
    return shifted[r - 1, a:a + rows, lanes]


def _fill_glu(first, a_ref, b_ref, ah_ref, bh_ref, u_ext, tm):
    uh = ah_ref[...].astype(F32) * _sigmoid(bh_ref[...].astype(F32))
    u_ext[0:HALO, :] = jnp.where(first, 0.0, uh)
    for r0 in range(0, tm, ELEM_ROWS):
        rows = slice(r0, r0 + ELEM_ROWS)
        u_ext[HALO + r0:HALO + r0 + ELEM_ROWS, :] = a_ref[rows, :].astype(F32) * _sigmoid(b_ref[rows, :].astype(F32))


def _layer_norm(cv, lg_ref, lb_ref):
    mu = jnp.mean(cv, axis=-1, keepdims=True)
    cc = cv - mu
    rstd = lax.rsqrt(jnp.mean(cc * cc, axis=-1, keepdims=True) + EPS)
    xh = cc * rstd
    return xh, rstd, xh * lg_ref[...] + lb_ref[...]


def _conv_fwd(proj, y_in, dw_w, dw_b, ln_g, ln_b, pw_w, pw_b, ex, ex_parts):
    R, E = proj.shape
    CW = pw_w.shape[0]
    tm = META_TILE
    NTL = R // tm
    phys = _conv_order(NTL)
    halo = _halo_block(NTL, tm)
    cb = (E - 3 * CW) // CW
    base = HALO - (CONV_K - 1)

    def body(*refs):
        a_ref, b_ref, g_ref, ah_ref, bh_ref, w_ref, wb_ref, lg_ref, lb_ref, pw_ref, pb_ref, yin_ref = refs[:12]
        ex_ins = refs[12:12 + ex.n]
        y_ref, c_ref = refs[12 + ex.n:14 + ex.n]
        ex_outs = refs[14 + ex.n:14 + 2 * ex.n]
        u_ext, u_sh, s_scr, upw_scr = refs[14 + 2 * ex.n:18 + 2 * ex.n]
        sems = refs[18 + 2 * ex.n:]

        @pl.when(pl.program_id(0) == 0)
        def _():
            ex.start(ex_ins, ex_outs, sems)

        _fill_glu(pl.program_id(0) == 0, a_ref, b_ref, ah_ref, bh_ref, u_ext, tm)
        _fill_shifted(u_ext, u_sh)
        for r0 in range(0, tm, CONV_ROWS):
            for l0 in range(0, CW, CONV_LANES):
                lanes = slice(l0, l0 + CONV_LANES)
                acc = None
                for k in range(CONV_K):
                    term = _at_offset(u_ext, u_sh, base + k, r0, CONV_ROWS, lanes) * w_ref[k:k + 1, lanes]
                    acc = term if acc is None else acc + term
                c_ref[r0:r0 + CONV_ROWS, lanes] = acc + wb_ref[:, lanes]
        blocks = [slice(r0, r0 + ELEM_ROWS) for r0 in range(0, tm, ELEM_ROWS)]
        for rows in blocks:
            _, _, ln = _layer_norm(c_ref[rows, :], lg_ref, lb_ref)
            s_scr[rows, :] = (ln * _sigmoid(ln)).astype(BF16)
        upw_scr[...] = _dot(s_scr[...], pw_ref[...]) + pb_ref[...]
        for rows in blocks:
            g = g_ref[rows, :].astype(F32)
            y_ref[rows, :] = (upw_scr[rows, :] * (g * _sigmoid(g))).astype(BF16)

        @pl.when(pl.program_id(0) == NTL - 1)
        def _():
            ex.wait(ex_ins, ex_outs, sems)

    row = lambda l: (0, 0)
    outs = pl.pallas_call(
        body, name="conv_fwd",
        grid=(NTL,),
        in_specs=[pl.BlockSpec((tm, CW), lambda l: (phys(l), cb)),
                  pl.BlockSpec((tm, CW), lambda l: (phys(l), cb + 1)),
                  pl.BlockSpec((tm, CW), lambda l: (phys(l), cb + 2)),
                  pl.BlockSpec((HALO, CW), lambda l: (halo(l), cb)),
                  pl.BlockSpec((HALO, CW), lambda l: (halo(l), cb + 1)),
                  pl.BlockSpec((HALO, CW), row),
                  pl.BlockSpec((1, CW), row), pl.BlockSpec((1, CW), row), pl.BlockSpec((1, CW), row),
                  pl.BlockSpec((CW, CW), row),
                  pl.BlockSpec((1, CW), row),
                  pl.BlockSpec(memory_space=pl.ANY)] + ex.specs,
        out_specs=[pl.BlockSpec((tm, CW), lambda l: (phys(l), 1)),
                   pl.BlockSpec((tm, CW), lambda l: (phys(l), 0))] + ex.specs,
        out_shape=[jax.ShapeDtypeStruct(y_in.shape, BF16), jax.ShapeDtypeStruct((R, CW), F32)] + ex.out_shape,
        input_output_aliases={11: 0},
        scratch_shapes=[pltpu.VMEM((HALO + tm, CW), F32), pltpu.VMEM((7, tm + HALO - 8, CW), F32),
                        pltpu.VMEM((tm, CW), BF16), pltpu.VMEM((tm, CW), F32)] + ex.scratch,
        compiler_params=_params(("arbitrary",), 48),
    )(proj, proj, proj, proj, proj, dw_w, dw_b, ln_g, ln_b, pw_w, pw_b, y_in, *ex_parts)
    return outs[0], outs[1], outs[2:]


def _out_proj_loss(xs, meta_tile, y, w_out, final_g, target):
    SEQ, D = xs.shape
    R, MIX = y.shape
    tm = META_TILE
    n_seq = SEQ // tm
    n_tiles = R // tm
    rows_out = _pick_tile(MIX, 256)

    def body(x_ref, mt_ref, y_ref, w_hbm, fg_ref, t_ref, dh2_ref, dy_ref, dwo_hbm, dfg_ref, loss_ref,
             w_scr, acc, stage, sem):
        i = pl.program_id(0)

        @pl.when(i == 0)
        def _():
            cp = pltpu.make_async_copy(w_hbm, w_scr, sem)
            cp.start()
            acc[...] = jnp.zeros_like(acc)
            dfg_ref[...] = jnp.zeros_like(dfg_ref)
            loss_ref[...] = jnp.zeros_like(loss_ref)
            cp.wait()

        yb = y_ref[...]
        h2 = jnp.where(i < n_seq, x_ref[...], mt_ref[...]) + _dot(yb, w_scr[...])
        r2 = lax.rsqrt(jnp.mean(h2 * h2, axis=-1, keepdims=True) + EPS)
        n = h2 * r2
        fg = fg_ref[...]
        err = jnp.where(i < n_seq, n * fg - t_ref[...], 0.0)
        loss_ref[...] += 0.5 * jnp.sum(jnp.mean(err * err, axis=-1, keepdims=True), axis=0, keepdims=True)
        dout = err * (1.0 / D)
        dfg_ref[...] += jnp.sum(dout * n, axis=0, keepdims=True)
        dn = dout * fg
        dh2 = r2 * (dn - n * jnp.mean(dn * n, axis=-1, keepdims=True))
        dh2_ref[...] = dh2
        dh2b = dh2.astype(BF16)
        dy_ref[...] = _dot(dh2b, w_scr[...], NT).astype(BF16)
        acc[...] += _dot(yb, dh2b, TN)

        @pl.when(i == n_tiles - 1)
        def _():
            for r in range(0, MIX, rows_out):
                stage[...] = acc[r:r + rows_out, :].astype(BF16)
                cp = pltpu.make_async_copy(stage, dwo_hbm.at[r:r + rows_out, :], sem)
                cp.start()
                cp.wait()

    row = lambda i: (0, 0)
    return pl.pallas_call(
        body, name="out_proj_loss",
        grid=(n_tiles,),
        in_specs=[pl.BlockSpec((tm, D), lambda i: (jnp.minimum(i, n_seq - 1), 0)),
                  pl.BlockSpec((tm, D), row),
                  pl.BlockSpec((tm, MIX), lambda i: (i, 0)),
                  pl.BlockSpec(memory_space=pl.ANY),
                  pl.BlockSpec((1, D), row),
                  pl.BlockSpec((tm, D), lambda i: (jnp.minimum(i, n_seq - 1), 0))],
        out_specs=[pl.BlockSpec((tm, D), lambda i: (i, 0)),
                   pl.BlockSpec((tm, MIX), lambda i: (i, 0)),
                   pl.BlockSpec(memory_space=pl.ANY),
                   pl.BlockSpec((1, D), row),
                   pl.BlockSpec((1, 1), row)],
        out_shape=[jax.ShapeDtypeStruct((R, D), F32), jax.ShapeDtypeStruct((R, MIX), BF16),
                   jax.ShapeDtypeStruct((MIX, D), BF16), jax.ShapeDtypeStruct((1, D), F32),
                   jax.ShapeDtypeStruct((1, 1), F32)],
        scratch_shapes=[pltpu.VMEM((MIX, D), BF16), pltpu.VMEM((MIX, D), F32), pltpu.VMEM((rows_out, D), BF16),
                        pltpu.SemaphoreType.DMA],
        compiler_params=_params(("arbitrary",), 60),
    )(xs, meta_tile, y, w_out, final_g, target)


def _conv_bwd(proj, conv_out, dy, dw_w, ln_g, ln_b, pw_w, pw_b):
    R, E = proj.shape
    CW = pw_w.shape[0]
    tm = META_TILE
    NTL = R // tm
    order = _conv_order(NTL)
    phys = lambda i: order(NTL - 1 - i)
    halo_l = _halo_block(NTL, tm)
    halo = lambda i: halo_l(NTL - 1 - i)
    cb = (E - 3 * CW) // CW
    base = HALO - (CONV_K - 1)

    def body(a_ref, b_ref, g_ref, ah_ref, bh_ref, c_ref, dy_ref, w_ref, lg_ref, lb_ref, pw_ref, pb_ref,
             dp_ref, dpw_ref, dww_ref, vec_ref, u_ext, u_sh, dc_ext, dc_sh, du_scr, dww_acc, dpw_acc,
             xh_scr, rstd_scr, ln_scr, sg_scr, upw_scr, s_scr, dupw_scr):
        i = pl.program_id(0)

        @pl.when(i == 0)
        def _():
            dpw_acc[...] = jnp.zeros_like(dpw_acc)
            dww_ref[...] = jnp.zeros_like(dww_ref)
            vec_ref[...] = jnp.zeros_like(vec_ref)
            dww_acc[...] = jnp.zeros_like(dww_acc)
            dc_ext[tm:tm + HALO, :] = jnp.zeros((HALO, CW), F32)

        _fill_glu(i == NTL - 1, a_ref, b_ref, ah_ref, bh_ref, u_ext, tm)
        _fill_shifted(u_ext, u_sh)
        blocks = [slice(r0, r0 + ELEM_ROWS) for r0 in range(0, tm, ELEM_ROWS)]
        for rows in blocks:
            xh, rstd, ln = _layer_norm(c_ref[rows, :], lg_ref, lb_ref)
            sg = _sigmoid(ln)
            xh_scr[rows, :], rstd_scr[rows, :], ln_scr[rows, :], sg_scr[rows, :] = xh, rstd, ln, sg
            s_scr[rows, :] = (ln * sg).astype(BF16)
        upw_scr[...] = _dot(s_scr[...], pw_ref[...]) + pb_ref[...]
        col_sum = jnp.zeros((1, CW), F32)
        for rows in blocks:
            g = g_ref[rows, :].astype(F32)
            sgg = _sigmoid(g)
            dyc = dy_ref[rows, :].astype(F32)
            dp_ref[rows, 2 * CW:3 * CW] = (dyc * upw_scr[rows, :] * _dsilu(g, sgg)).astype(BF16)
            dupw = dyc * (g * sgg)
            dupw_scr[rows, :] = dupw.astype(BF16)
            col_sum = col_sum + jnp.sum(dupw, axis=0, keepdims=True)
        vec_ref[0:1, :] += col_sum
        dpw_acc[...] += _dot(s_scr[...], dupw_scr[...], TN)
        upw_scr[...] = _dot(dupw_scr[...], pw_ref[...], NT)
        sum_g, sum_b, sum_c = col_sum * 0.0, col_sum * 0.0, col_sum * 0.0
        for rows in blocks:
            xh, rstd = xh_scr[rows, :], rstd_scr[rows, :]
            dln = upw_scr[rows, :] * _dsilu(ln_scr[rows, :], sg_scr[rows, :])
            sum_g = sum_g + jnp.sum(dln * xh, axis=0, keepdims=True)
            sum_b = sum_b + jnp.sum(dln, axis=0, keepdims=True)
            dxh = dln * lg_ref[...]
            dc = rstd * (dxh - jnp.mean(dxh, axis=-1, keepdims=True)
                         - xh * jnp.mean(dxh * xh, axis=-1, keepdims=True))
            sum_c = sum_c + jnp.sum(dc, axis=0, keepdims=True)
            dc_ext[rows, :] = dc
        vec_ref[1:2, :] += sum_g
        vec_ref[2:3, :] += sum_b
        vec_ref[3:4, :] += sum_c
        _fill_shifted(dc_ext, dc_sh)

        for l0 in range(0, CW, CONV_LANES):
            lanes = slice(l0, l0 + CONV_LANES)
            for r0 in range(0, tm, CONV_ROWS):
                acc = None
                for k in range(CONV_K):
                    term = _at_offset(dc_ext, dc_sh, CONV_K - 1 - k, r0, CONV_ROWS, lanes) * w_ref[k:k + 1, lanes]
                    acc = term if acc is None else acc + term
                du_scr[r0:r0 + CONV_ROWS, lanes] = acc

        n_grp = tm // 8
        by_shift = [[(k, (base + k) // 8) for k in range(CONV_K) if (base + k) % 8 == r] for r in range(8)]
        for l0 in range(0, CW, LANE):
            lane = slice(l0, l0 + LANE)
            for r in range(8):
                src = u_ext if r == 0 else u_sh.at[r - 1]
                a_lo, a_hi = by_shift[r][0][1], by_shift[r][-1][1]
                sums = {k: None for k, _ in by_shift[r]}
                dcg = {}
                for gi in range(a_lo, n_grp + a_hi):
                    if gi - a_lo < n_grp:
                        dcg[gi - a_lo] = dc_ext[8 * (gi - a_lo):8 * (gi - a_lo) + 8, lane]
                    dcg.pop(gi - a_hi - 1, None)
                    ug = src[8 * gi:8 * gi + 8, lane]
                    for k, a in by_shift[r]:
                        if 0 <= gi - a < n_grp:
                            prod = dcg[gi - a] * ug
                            sums[k] = prod if sums[k] is None else sums[k] + prod
                for k, _ in by_shift[r]:
                    dww_acc[k, :, lane] += sums[k]

        for rows in blocks:
            du = du_scr[rows, :]
            sgb = _sigmoid(b_ref[rows, :].astype(F32))
            dp_ref[rows, 0:CW] = (du * sgb).astype(BF16)
            dp_ref[rows, CW:2 * CW] = (du * a_ref[rows, :].astype(F32) * sgb * (1.0 - sgb)).astype(BF16)
        dc_ext[tm:tm + HALO, :] = dc_ext[0:HALO, :]

        @pl.when(i == NTL - 1)
        def _():
            for k in range(CONV_K):
                dww_ref[k:k + 1, :] = jnp.sum(dww_acc[k], axis=0, keepdims=True)
            dpw_ref[...] = dpw_acc[...].astype(BF16)

    row = lambda i: (0, 0)
    return pl.pallas_call(
        body, name="conv_bwd",
        grid=(NTL,),
        in_specs=[pl.BlockSpec((tm, CW), lambda i: (phys(i), cb)),
                  pl.BlockSpec((tm, CW), lambda i: (phys(i), cb + 1)),
                  pl.BlockSpec((tm, CW), lambda i: (phys(i), cb + 2)),
                  pl.BlockSpec((HALO, CW), lambda i: (halo(i), cb)),
                  pl.BlockSpec((HALO, CW), lambda i: (halo(i), cb + 1)),
                  pl.BlockSpec((tm, CW), lambda i: (phys(i), 0)),
                  pl.BlockSpec((tm, CW), lambda i: (phys(i), 1)),
                  pl.BlockSpec((HALO, CW), row),
                  pl.BlockSpec((1, CW), row), pl.BlockSpec((1, CW), row),
                  pl.BlockSpec((CW, CW), row),
                  pl.BlockSpec((1, CW), row)],
        out_specs=[pl.BlockSpec((tm, 3 * CW), lambda i: (phys(i), 0)),
                   pl.BlockSpec((CW, CW), row),
                   pl.BlockSpec((HALO, CW), row),
                   pl.BlockSpec((8, CW), row)],
        out_shape=[jax.ShapeDtypeStruct((R, 3 * CW), BF16), jax.ShapeDtypeStruct((CW, CW), BF16),
                   jax.ShapeDtypeStruct((HALO, CW), F32), jax.ShapeDtypeStruct((8, CW), F32)],
        scratch_shapes=[pltpu.VMEM((HALO + tm, CW), F32), pltpu.VMEM((7, tm + HALO - 8, CW), F32),
                        pltpu.VMEM((tm + HALO, CW), F32), pltpu.VMEM((7, tm + HALO - 8, CW), F32),
                        pltpu.VMEM((tm, CW), F32), pltpu.VMEM((CONV_K, 8, CW), F32), pltpu.VMEM((CW, CW), F32),
                        pltpu.VMEM((tm, CW), F32), pltpu.VMEM((tm, 1), F32), pltpu.VMEM((tm, CW), F32),
                        pltpu.VMEM((tm, CW), F32), pltpu.VMEM((tm, CW), F32), pltpu.VMEM((tm, CW), BF16),
                        pltpu.VMEM((tm, CW), BF16)],
        compiler_params=_params(("arbitrary",), 60),
    )(proj, proj, proj, proj, proj, conv_out, dy, dw_w, ln_g, ln_b, pw_w, pw_b)


def _retention_bwd(proj, rope, gn_g, states, dy, ex, ex_parts):
    R, E = proj.shape
    RW = gn_g.shape[1]
    H = RET_HEADS
    hd = RW // H
    half = hd // 2
    NC = R // CHUNK
    mask, qd, kd, cd = _decay_tables(H)
    scale = float(hd) ** -0.5
    order = _chunk_order(NC)
    phys = lambda i: order(NC - 1 - i)

    def body(*refs):
        p_ref, cb_ref, sb_ref, ci_ref, si_ref, mask_ref, qd_ref, kd_ref, gn_ref, st_ref, dy_ref = refs[:11]
        ex_ins = refs[11:11 + ex.n]
        dp_ref, dgn_ref = refs[11 + ex.n:13 + ex.n]
        ex_outs = refs[13 + ex.n:13 + 2 * ex.n]
        dstate = refs[13 + 2 * ex.n]
        sems = refs[14 + 2 * ex.n:]

        @pl.when(pl.program_id(0) == 0)
        def _():
            ex.start(ex_ins, ex_outs, sems)
            dstate[...] = jnp.zeros_like(dstate)
            dgn_ref[...] = jnp.zeros_like(dgn_ref)

        cs, sn = _rope_chunk(cb_ref, sb_ref, ci_ref, si_ref)
        hs = range(H)
        col = lambda j, h: slice(j * RW + h * hd, j * RW + (h + 1) * hd)
        qr = [_rot(p_ref[:, col(0, h)].astype(F32), cs, sn, half) for h in hs]
        kr = [_rot(p_ref[:, col(1, h)].astype(F32), cs, sn, half) * scale for h in hs]
        v = [p_ref[:, col(2, h)] for h in hs]
        qb = [qr[h].astype(BF16) for h in hs]
        kb = [kr[h].astype(BF16) for h in hs]
        qdb = [(qr[h] * qd_ref[h]).astype(BF16) for h in hs]
        kdb = [(kr[h] * kd_ref[h]).astype(BF16) for h in hs]
        s_prev = [st_ref[0, h] for h in hs]
        dst = [dstate[h] for h in hs]
        dstb = [dst[h].astype(BF16) for h in hs]
        sb = [(_dot(qb[h], kb[h], NT) * mask_ref[h]).astype(BF16) for h in hs]
        y_raw = [_dot(sb[h], v[h]) + _dot(qdb[h], s_prev[h]) for h in hs]
        dyrb, dg = [], []
        for h in hs:
            g = p_ref[:, col(3, h)].astype(F32)
            mu = jnp.mean(y_raw[h], axis=-1, keepdims=True)
            yc = y_raw[h] - mu
            rstd = lax.rsqrt(jnp.mean(yc * yc, axis=-1, keepdims=True) + EPS)
            xh = yc * rstd
            gn = gn_ref[:, col(0, h)]
            sg = _sigmoid(g)
            dyh = dy_ref[:, col(0, h)].astype(F32)
            dg.append((dyh * (xh * gn) * _dsilu(g, sg)).astype(BF16))
            dyn = dyh * (g * sg)
            dgn_ref[:, col(0, h)] += jnp.sum(dyn * xh, axis=0, keepdims=True)
            dxh = dyn * gn
            dyr = rstd * (dxh - jnp.mean(dxh, axis=-1, keepdims=True)
                          - xh * jnp.mean(dxh * xh, axis=-1, keepdims=True))
            dyrb.append(dyr.astype(BF16))
        dsb = [(_dot(dyrb[h], v[h], NT) * mask_ref[h]).astype(BF16) for h in hs]
        dqr = [_dot(dsb[h], kb[h]) + _dot(dyrb[h], s_prev[h], NT) * qd_ref[h] for h in hs]
        dkr = [_dot(dsb[h], qb[h], TN) + _dot(v[h], dstb[h], NT) * kd_ref[h] for h in hs]
        dv = [_dot(sb[h], dyrb[h], TN) + _dot(kdb[h], dstb[h]) for h in hs]
        dst_new = [dst[h] * cd[h] + _dot(qdb[h], dyrb[h], TN) for h in hs]
        for h in hs:
            dstate[h] = dst_new[h]
            dp_ref[:, col(0, h)] = _rot_inv(dqr[h], cs, sn, half).astype(BF16)
            dp_ref[:, col(1, h)] = (_rot_inv(dkr[h], cs, sn, half) * scale).astype(BF16)
            dp_ref[:, col(2, h)] = dv[h].astype(BF16)
            dp_ref[:, col(3, h)] = dg[h]

        @pl.when(pl.program_id(0) == NC - 1)
        def _():
            ex.wait(ex_ins, ex_outs, sems)

    const3 = lambda i: (0, 0, 0)
    outs = pl.pallas_call(
        body, name="retention_bwd",
        grid=(NC,),
        in_specs=[pl.BlockSpec((CHUNK, 4 * RW), lambda i: (phys(i), 0)),
                  pl.BlockSpec((1, 1, half), lambda i: (phys(i), 0, 0)),
                  pl.BlockSpec((1, 1, half), lambda i: (phys(i), 0, 0)),
                  pl.BlockSpec((CHUNK, half), lambda i: (0, 0)),
                  pl.BlockSpec((CHUNK, half), lambda i: (0, 0)),
                  pl.BlockSpec((H, CHUNK, CHUNK), const3),
                  pl.BlockSpec((H, CHUNK, 1), const3),
                  pl.BlockSpec((H, CHUNK, 1), const3),
                  pl.BlockSpec((1, RW), lambda i: (0, 0)),
                  pl.BlockSpec((1, H, hd, hd), lambda i: (phys(i), 0, 0, 0)),
                  pl.BlockSpec((CHUNK, RW), lambda i: (phys(i), 0))] + ex.specs,
        out_specs=[pl.BlockSpec((CHUNK, 4 * RW), lambda i: (phys(i), 0)),
                   pl.BlockSpec((1, RW), lambda i: (0, 0))] + ex.specs,
        out_shape=[jax.ShapeDtypeStruct((R, 4 * RW), BF16), jax.ShapeDtypeStruct((1, RW), F32)] + ex.out_shape,
        scratch_shapes=[pltpu.VMEM((H, hd, hd), F32)] + ex.scratch,
        compiler_params=_params(("arbitrary",), 32),
    )(proj, *rope, jnp.asarray(mask), jnp.asarray(qd), jnp.asarray(kd), gn_g, states, dy, *ex_parts)
    return outs[0], outs[1], outs[2:]


def _dproj_specs(tk, tn, n_ret, tile_axis, col_axis):
    def ret_map(*ids):
        t, j = ids[tile_axis], ids[col_axis]
        return (jnp.where(j < n_ret, t, 0), jnp.minimum(j, n_ret - 1))

    def conv_map(*ids):
        t, j = ids[tile_axis], ids[col_axis]
        return (jnp.where(j >= n_ret, t, 0), jnp.maximum(j - n_ret, 0))

    return pl.BlockSpec((tk, tn), ret_map), pl.BlockSpec((tk, tn), conv_map)


def _w_in_grad(hn, dp_ret, dp_conv):
    R, D = dp_ret.shape[0], hn.shape[1]
    tn = _pick_tile(dp_conv.shape[1] // 3, 1024, 128)
    n_ret, n_conv = dp_ret.shape[1] // tn, dp_conv.shape[1] // tn
    E = dp_ret.shape[1] + dp_conv.shape[1]
    tk = _pick_tile(R, 1024, MXU_DIM)
    n_t = R // tk
    ret_spec, conv_spec = _dproj_specs(tk, tn, n_ret, 1, 0)

    def body(hn_ref, r_ref, c_ref, out_ref, acc):
        j, t = pl.program_id(0), pl.program_id(1)

        @pl.when(t == 0)
        def _():
            acc[...] = jnp.zeros_like(acc)

        @pl.when(j < n_ret)
        def _():
            acc[...] += _dot(hn_ref[...], r_ref[...], TN)

        @pl.when(j >= n_ret)
        def _():
            acc[...] += _dot(hn_ref[...], c_ref[...], TN)

        @pl.when(t == n_t - 1)
        def _():
            out_ref[...] = acc[...].astype(BF16)

    return pl.pallas_call(
        body, name="w_in_grad",
        grid=(n_ret + n_conv, n_t),
        in_specs=[pl.BlockSpec((tk, D), lambda j, t: (t, 0)), ret_spec, conv_spec],
        out_specs=pl.BlockSpec((D, tn), lambda j, t: (0, j)),
        out_shape=jax.ShapeDtypeStruct((D, E), BF16),
        scratch_shapes=[pltpu.VMEM((D, tn), F32)],
        compiler_params=_params(("arbitrary", "arbitrary"), 48),
    )(hn, dp_ret, dp_conv)


def _h_grad(dp_ret, dp_conv, w_in, xs, meta_tile, dh2, ln_g, ex, ex_parts):
    R, D = dh2.shape
    te = CHUNK
    n_x = xs.shape[0] // te
    n_m = meta_tile.shape[0] // te
    tn = _pick_tile(dp_conv.shape[1] // 3, 1024, 128)
    n_ret, n_conv = dp_ret.shape[1] // tn, dp_conv.shape[1] // tn
    n_k = n_ret + n_conv
    tm = _pick_tile(R, 1024, meta_tile.shape[0])
    n_e = tm // te
    n_t = R // tm
    assert n_e <= n_k and (n_x + n_m) * te == R

    def ret_map(t, k):
        return (jnp.where(k < n_ret, jnp.minimum(t, n_t - 1), 0), jnp.minimum(k, n_ret - 1))

    def conv_map(t, k):
        return (jnp.where(k >= n_ret, jnp.minimum(t, n_t - 1), 0), jnp.maximum(k - n_ret, 0))

    def row_block(t, k):
        return jnp.maximum(t - 1, 0) * n_e + jnp.where(t > 0, jnp.minimum(k, n_e - 1), 0)

    def body(*refs):
        r_ref, c_ref, w_hbm, w_ref, x_ref, mt_ref, dh2_ref, g_ref = refs[:8]
        ex_ins = refs[8:8 + ex.n]
        o = 8 + ex.n
        dh_ref, dlg_ref = refs[o:o + 2]
        ex_outs = refs[o + 2:o + 2 + ex.n]
        acc, w_keep, keep_sems = refs[o + 2 + ex.n:o + 5 + ex.n]
        sems = refs[o + 5 + ex.n:]
        t, k = pl.program_id(0), pl.program_id(1)
        cur, old = t % 2, (t + 1) % 2

        def keep(j):
            return pltpu.make_async_copy(w_hbm.at[:, j * tn:(j + 1) * tn], w_keep.at[j], keep_sems.at[j])

        @pl.when((k == 0) & (t == 0))
        def _():
            for j in range(n_ret):
                keep(j).start()
            ex.start(ex_ins, ex_outs, sems)
            dlg_ref[...] = jnp.zeros_like(dlg_ref)

        for j in range(n_ret):
            @pl.when((k == j) & (t == 0))
            def _():
                keep(j).wait()

        @pl.when((k == 0) & (t < n_t))
        def _():
            acc[cur] = _dot(r_ref[...], w_keep[0], NT)

        @pl.when((k > 0) & (k < n_ret) & (t < n_t))
        def _():
            acc[cur] += _dot(r_ref[...], w_keep[k], NT)

        @pl.when((k >= n_ret) & (t < n_t))
        def _():
            acc[cur] += _dot(c_ref[...], w_ref[...], NT)

        @pl.when((k < n_e) & (t > 0))
        def _():
            hv = jnp.where(row_block(t, k) < n_x, x_ref[...], mt_ref[...])
            r = lax.rsqrt(jnp.mean(hv * hv, axis=-1, keepdims=True) + EPS)
            nrm = hv * r
            dhn = acc[old, pl.ds(pl.multiple_of(k * te, te), te), :]
            dlg_ref[...] += jnp.sum(dhn * nrm, axis=0, keepdims=True)
            dn = dhn * g_ref[...]
            dh_ref[...] = dh2_ref[...] + r * (dn - nrm * jnp.mean(dn * nrm, axis=-1, keepdims=True))

        @pl.when((k == n_k - 1) & (t == n_t))
        def _():
            ex.wait(ex_ins, ex_outs, sems)

    row = lambda t, k: (0, 0)
    outs = pl.pallas_call(
        body, name="h_grad",
        grid=(n_t + 1, n_k),
        in_specs=[pl.BlockSpec((tm, tn), ret_map), pl.BlockSpec((tm, tn), conv_map),
                  pl.BlockSpec(memory_space=pl.ANY),
                  pl.BlockSpec((D, tn), lambda t, k: (0, jnp.maximum(k, n_ret))),
                  pl.BlockSpec((te, D), lambda t, k: (jnp.minimum(row_block(t, k), n_x - 1), 0)),
                  pl.BlockSpec((te, D), lambda t, k: (jnp.clip(row_block(t, k) - n_x, 0, n_m - 1), 0)),
                  pl.BlockSpec((te, D), lambda t, k: (row_block(t, k), 0)),
                  pl.BlockSpec((1, D), row)] + ex.specs,
        out_specs=[pl.BlockSpec((te, D), lambda t, k: (row_block(t, k), 0)),
                   pl.BlockSpec((1, D), row)] + ex.specs,
        out_shape=[jax.ShapeDtypeStruct((R, D), F32), jax.ShapeDtypeStruct((1, D), F32)] + ex.out_shape,
        scratch_shapes=[pltpu.VMEM((2, tm, D), F32), pltpu.VMEM((n_ret, D, tn), BF16),
                        pltpu.SemaphoreType.DMA((n_ret,))] + ex.scratch,
        compiler_params=_params(("arbitrary", "arbitrary"), 60),
    )(dp_ret, dp_conv, w_in, w_in, xs, meta_tile, dh2, ln_g, *ex_parts)
    return outs[0], outs[1], outs[2:]


def _adamw(w, g, m, v):
    m = ADAM_B1 * m + (1.0 - ADAM_B1) * g
    v = ADAM_B2 * v + (1.0 - ADAM_B2) * (g * g)
    m_hat = m / (1.0 - ADAM_B1 ** ADAM_STEP)
    v_hat = v / (1.0 - ADAM_B2 ** ADAM_STEP)
    delta = -ADAM_LR * (m_hat / (jnp.sqrt(v_hat) + ADAM_EPS) + ADAM_WD * w)
    return delta, m, v


def _sum_slots(ref):
    g = ref[0].astype(F32)
    for s in range(1, N_DEV):
        g = g + ref[s].astype(F32)
    return g


def _sum_adamw(name, parts, w, m, v, rows_target, ex=None, ex_parts=()):
    rows, cols = w.shape
    tr = _pick_tile(rows, rows_target, 8)
    n_ex = 0 if ex is None else ex.n
    n_steps = rows // tr

    def body(*refs):
        p_ref, w_ref, m_ref, v_ref = refs[:4]
        ex_ins = refs[4:4 + n_ex]
        o = 4 + n_ex
        g_ref, d_ref, nm_ref, nv_ref = refs[o:o + 4]
        ex_outs, sems = refs[o + 4:o + 4 + n_ex], refs[o + 4 + n_ex:]
        if ex is not None:
            @pl.when(pl.program_id(0) == 0)
            def _():
                ex.start(ex_ins, ex_outs, sems)

        g = _sum_slots(p_ref)
        d, nm, nv = _adamw(w_ref[...], g, m_ref[...], v_ref[...])
        g_ref[...] = g
        d_ref[...] = d
        nm_ref[...] = nm
        nv_ref[...] = nv
        if ex is not None:
            @pl.when(pl.program_id(0) == n_steps - 1)
            def _():
                ex.wait(ex_ins, ex_outs, sems)

    tile = pl.BlockSpec((tr, cols), lambda i: (i, 0))
    ex_specs, ex_shape, ex_scratch = ([], [], []) if ex is None else (ex.specs, ex.out_shape, ex.scratch)
    outs = pl.pallas_call(
        body, name=name,
        grid=(n_steps,),
        in_specs=[pl.BlockSpec((N_DEV, tr, cols), lambda i: (0, i, 0)), tile, tile, tile] + ex_specs,
        out_specs=[tile] * 4 + ex_specs,
        out_shape=[jax.ShapeDtypeStruct((rows, cols), F32)] * 4 + ex_shape,
        scratch_shapes=ex_scratch,
        compiler_params=_params(("arbitrary",), 40),
    )(parts, w, m, v, *ex_parts)
    return outs[:4], outs[4:]


def _sum_adamw_small(parts_list, w_list, m_list, v_list, loss_parts):
    n = len(w_list)

    def body(*refs):
        p_refs, w_refs, m_refs, v_refs = refs[:n], refs[n:2 * n], refs[2 * n:3 * n], refs[3 * n:4 * n]
        lp_ref = refs[4 * n]
        outs = refs[4 * n + 1:]
        for a in range(n):
            g = _sum_slots(p_refs[a])
            d, nm, nv = _adamw(w_refs[a][...], g, m_refs[a][...], v_refs[a][...])
            outs[4 * a][...] = g
            outs[4 * a + 1][...] = d
            outs[4 * a + 2][...] = nm
            outs[4 * a + 3][...] = nv
        outs[4 * n][...] = _sum_slots(lp_ref)

    out_shape = []
    for w in w_list:
        out_shape += [jax.ShapeDtypeStruct(w.shape, F32)] * 4
    out_shape.append(jax.ShapeDtypeStruct(loss_parts.shape[1:], F32))
    return pl.pallas_call(body, name="sum_adamw_small", out_shape=out_shape)(
        *parts_list, *w_list, *m_list, *v_list, loss_parts)


def kernel(x, meta_tokens, ln_g, w_in, ret_gn_g, conv_dw_w, conv_dw_b, conv_ln_g, conv_ln_b, conv_pw_w, conv_pw_b, w_out, final_g, loss_target, m_meta_tokens, m_ln_g, m_w_in, m_ret_gn_g, m_conv_dw_w, m_conv_dw_b, m_conv_ln_g, m_conv_ln_b, m_conv_pw_w, m_conv_pw_b, m_w_out, m_final_g, v_meta_tokens, v_ln_g, v_w_in, v_ret_gn_g, v_conv_dw_w, v_conv_dw_b, v_conv_ln_g, v_conv_ln_b, v_conv_pw_w, v_conv_pw_b, v_w_out, v_final_g):
    _, SEQ, D = x.shape
    MIX = w_out.shape[2]
    RW = ret_gn_g.shape[1]
    CW = conv_pw_b.shape[1]
    assert RW == CW and MIX == RW + CW and SEQ % META_TILE == 0 and CONV_K - 1 <= HALO
    R = SEQ + META_TILE
    hd = RW // RET_HEADS
    half = hd // 2
    me = 4 * lax.axis_index("x") + 2 * lax.axis_index("y") + lax.axis_index("c")

    dw_pad = jnp.pad(conv_dw_w[0], ((0, HALO - CONV_K), (0, 0)))

    n_seq_chunks = SEQ // CHUNK
    base = jnp.concatenate([jnp.arange(n_seq_chunks, dtype=F32) * CHUNK + N_META,
                            jnp.zeros((META_TILE // CHUNK - 1,), F32), jnp.full((1,), N_META - CHUNK, F32)])
    inv_freq = ROPE_BASE ** (-jnp.arange(half, dtype=F32) / half)
    ang_base = (base[:, None] * inv_freq[None, :])[:, None, :]
    ang_row = jnp.arange(CHUNK, dtype=F32)[:, None] * inv_freq[None, :]
    rope = (jnp.cos(ang_base), jnp.sin(ang_base), jnp.cos(ang_row), jnp.sin(ang_row))

    xs = x[0]
    target = loss_target[0]
    final_g2 = final_g[None, :]

    small_shards = [meta_tokens, dw_pad]
    hn, meta_tile, (_, dw_g) = _rms_norm(xs, meta_tokens, ln_g, _Exchange(small_shards, [None, None]), small_shards)
    dw_g = jnp.swapaxes(dw_g, 0, 1).reshape(HALO, CW)
    pw_shard, w_out_shard = [conv_pw_w[0].astype(BF16)], [w_out[0].astype(BF16)]
    proj, w_in_g, (pw_g,) = _in_proj_gather(hn, R, w_in[0].astype(BF16), _Exchange(pw_shard, [None]), pw_shard)
    pw_g = pw_g.reshape(CW, CW)
    y, states = _retention_fwd(proj, rope, ret_gn_g, MIX)
    y, conv_out, (w_out_g,) = _conv_fwd(proj, y, dw_g, conv_dw_b, conv_ln_g, conv_ln_b, pw_g, conv_pw_b,
                                        _Exchange(w_out_shard, [None]), w_out_shard)
    w_out_g = w_out_g.reshape(MIX, D)
    dh2, dy, dwo_p, dfg_p, loss_p = _out_proj_loss(xs, meta_tile, y, w_out_g, final_g2, target)

    dp_conv, dpw_p, dww_p, cvec_p = _conv_bwd(proj, conv_out, dy, dw_g, conv_ln_g, conv_ln_b, pw_g, conv_pw_b)
    dp_ret, dgn_p, (r_wo, r_pw) = _retention_bwd(proj, rope, ret_gn_g, states, dy,
                                                 _Exchange([dwo_p, dpw_p], [0, 0]), [dwo_p, dpw_p])
    dwi_p = _w_in_grad(hn, dp_ret, dp_conv)
    dh, dlg_p, (r_wi,) = _h_grad(dp_ret, dp_conv, w_in_g, xs, meta_tile, dh2, ln_g, _Exchange([dwi_p], [1]), [dwi_p])
    grad_x = dh[:SEQ][None]

    def at_row(r, a, b=None):
        v = a if b is None else jnp.concatenate([a, b], axis=1)
        return jnp.pad(v, ((r, 7 - r), (0, D - v.shape[1])))
    vec8 = (at_row(0, dlg_p) + at_row(1, dfg_p)
            + at_row(2, dgn_p, cvec_p[3:4])
            + at_row(3, cvec_p[1:2], cvec_p[2:3])
            + at_row(4, cvec_p[0:1], jnp.broadcast_to(loss_p, (1, CW))))
    small = jnp.concatenate([dh[R - N_META:], vec8,
                             jnp.zeros((SMALL_ROWS - N_META - 8, D), F32)], axis=0)

    (g_wi, d_wi, nm_wi, nv_wi), (r_dww, r_small) = _sum_adamw(
        "sum_adamw_w_in", r_wi, w_in[0], m_w_in[0], v_w_in[0], 256,
        ex=_Exchange([dww_p, small], [1, None]), ex_parts=[dww_p, small])
    (g_wo, d_wo, nm_wo, nv_wo), _ = _sum_adamw("sum_adamw_w_out", r_wo, w_out[0], m_w_out[0], v_w_out[0], 128)
    (g_pw, d_pw, nm_pw, nv_pw), _ = _sum_adamw("sum_adamw_pw", r_pw, conv_pw_w[0], m_conv_pw_w[0], v_conv_pw_w[0], 128)

    dcol = D // N_DEV
    sm = lambda r0, nr, c0, nc: lax.slice(r_small, (0, r0, c0), (N_DEV, r0 + nr, c0 + nc))
    meta_parts = lax.dynamic_slice(r_small, (0, 0, me * dcol), (N_DEV, N_META, dcol))
    small_parts = [meta_parts, sm(16, 1, 0, D), sm(18, 1, 0, RW), r_dww, sm(18, 1, RW, CW),
                   sm(19, 1, 0, CW), sm(19, 1, CW, CW), sm(20, 1, 0, CW), sm(17, 1, 0, D)]
    pad31 = lambda a: jnp.pad(a, ((0, HALO - CONV_K), (0, 0)))
    ws = [meta_tokens, ln_g, ret_gn_g, pad31(conv_dw_w[0]), conv_dw_b, conv_ln_g, conv_ln_b, conv_pw_b, final_g2]
    ms = [m_meta_tokens, m_ln_g, m_ret_gn_g, pad31(m_conv_dw_w[0]), m_conv_dw_b, m_conv_ln_g, m_conv_ln_b,
          m_conv_pw_b, m_final_g[None, :]]
    vs = [v_meta_tokens, v_ln_g, v_ret_gn_g, pad31(v_conv_dw_w[0]), v_conv_dw_b, v_conv_ln_g, v_conv_ln_b,
          v_conv_pw_b, v_final_g[None, :]]
    loss_parts = sm(20, 1, CW, 1)
    outs = _sum_adamw_small(small_parts, ws, ms, vs, loss_parts)
    loss = outs[-1][0, 0]
    quad = [outs[4 * a:4 * a + 4] for a in range(len(ws))]
    (q_meta, q_lng, q_gn, q_dww, q_dwb, q_clg, q_clb, q_pwb, q_fg) = quad
    q_dww = [t[:CONV_K][None] for t in q_dww]
    q_fg = [t[0] for t in q_fg]
    q_wi = [t[None] for t in (g_wi, d_wi, nm_wi, nv_wi)]
    q_wo = [t[None] for t in (g_wo, d_wo, nm_wo, nv_wo)]
    q_pw = [t[None] for t in (g_pw, d_pw, nm_pw, nv_pw)]

    per_w = [q_meta, q_lng, q_wi, q_gn, q_dww, q_dwb, q_clg, q_clb, q_pw, q_pwb, q_wo, q_fg]
    result = [loss, grad_x]
    for which in range(4):
        result += [q[which] for q in per_w]
    return tuple(result)
```

```python
import numpy as np
import jax
import jax.numpy as jnp
from jax import lax
from jax.experimental import pallas as pl
from jax.experimental.pallas import tpu as pltpu

N_META = 16
RET_HEADS = 4
CONV_K = 31
CHUNK = 128
ROPE_BASE = 10000.0
EPS = 1e-6
ADAM_LR = 0.001
ADAM_B1 = 0.9
ADAM_B2 = 0.999
ADAM_EPS = 1e-08
ADAM_WD = 0.01
ADAM_STEP = 10

N_DEV = 8
META_TILE = 256
HALO = 32
SMALL_ROWS = 32
VMEM_BYTES_V7X = 64 * 1024 * 1024
MXU_DIM = 256

F32 = jnp.float32
BF16 = jnp.bfloat16
MESH = pl.DeviceIdType.MESH

NN = (((1,), (0,)), ((), ()))
NT = (((1,), (1,)), ((), ()))
TN = (((0,), (0,)), ((), ()))


def _dot(a, b, dims=NN):
    return lax.dot_general(a, b, dims, preferred_element_type=F32)


def _pick_tile(n, target, mult=16):
    best = None
    for t in range(mult, min(n, target) + 1, mult):
        if n % t == 0:
            best = t
    assert best is not None, (n, target)
    return best


def _params(sem=None, vmem_mb=None):
    kw = {}
    if sem is not None:
        kw["dimension_semantics"] = sem
    if vmem_mb is not None:
        kw["vmem_limit_bytes"] = min(vmem_mb * 1024 * 1024, VMEM_BYTES_V7X - 4 * 1024 * 1024)
    return pltpu.CompilerParams(**kw)


def _sigmoid(x):
    return jax.nn.sigmoid(x)


def _dsilu(x, sg):
    return sg * (1.0 + x * (1.0 - sg))


def _decay_tables(heads):
    h = np.arange(heads, dtype=np.float32)
    gamma = (1.0 - np.exp2(-5.0 - h)).astype(np.float32)
    log_g = np.log(gamma).astype(np.float32)
    idx = np.arange(CHUNK, dtype=np.float32)
    rel = idx[:, None] - idx[None, :]
    mask = np.where(rel[None] >= 0, np.exp(np.maximum(rel, 0.0)[None] * log_g[:, None, None]), 0.0)
    qd = np.exp((idx[None, :] + 1.0) * log_g[:, None])
    kd = np.exp((CHUNK - 1.0 - idx[None, :]) * log_g[:, None])
    cd = np.exp(CHUNK * log_g)
    return (mask.astype(np.float32), qd.astype(np.float32)[:, :, None], kd.astype(np.float32)[:, :, None],
            [float(c) for c in cd.astype(np.float32)])


class _Exchange:
    def __init__(self, parts, block_axes):
        self.block_axes = list(block_axes)
        self.n = len(parts)
        self.out_shape = []
        for p, ax in zip(parts, block_axes):
            shp = list(p.shape)
            if ax is not None:
                assert shp[ax] % N_DEV == 0
                shp[ax] //= N_DEV
            self.out_shape.append(jax.ShapeDtypeStruct((N_DEV, *shp), p.dtype))
        self.scratch = [pltpu.SemaphoreType.DMA((self.n, N_DEV - 1)), pltpu.SemaphoreType.DMA((self.n, N_DEV - 1)),
                        pltpu.SemaphoreType.DMA((self.n,))]
        self.specs = [pl.BlockSpec(memory_space=pl.ANY)] * self.n

    def _copies(self, ins, outs, sems):
        send_sems, recv_sems, local_sems = sems
        x, y, c = lax.axis_index("x"), lax.axis_index("y"), lax.axis_index("c")
        me_idx = 4 * x + 2 * y + c

        def src_block(a, dev_idx):
            ax = self.block_axes[a]
            if ax is None:
                return ins[a]
            n = ins[a].shape[ax] // N_DEV
            idx = [slice(None)] * len(ins[a].shape)
            idx[ax] = pl.ds(pl.multiple_of(dev_idx * n, n), n)
            return ins[a].at[tuple(idx)]

        local = [pltpu.make_async_copy(src_block(a, me_idx), outs[a].at[me_idx], local_sems.at[a])
                 for a in range(self.n)]
        remote = []
        for m in range(1, N_DEV):
            px, py, pc = x ^ ((m >> 2) & 1), y ^ ((m >> 1) & 1), c ^ (m & 1)
            for a in range(self.n):
                remote.append(pltpu.make_async_remote_copy(
                    src_ref=src_block(a, 4 * px + 2 * py + pc), dst_ref=outs[a].at[me_idx],
                    send_sem=send_sems.at[a, m - 1], recv_sem=recv_sems.at[a, m - 1],
                    device_id=(px, py, pc), device_id_type=MESH))
        return local, remote

    def start(self, ins, outs, sems):
        local, remote = self._copies(ins, outs, sems)
        for cp in local + remote:
            cp.start()

    def wait(self, ins, outs, sems):
        local, remote = self._copies(ins, outs, sems)
        for cp in remote:
            cp.wait_recv()
        for cp in remote:
            cp.wait_send()
        for cp in local:
            cp.wait()


def _rms_norm(xs, meta_shard, ln_g, ex, ex_parts):
    SEQ, D = xs.shape
    n_meta, dcol = meta_shard.shape
    tm = 2 * META_TILE
    n_seq = SEQ // tm

    def body(*refs):
        x_ref, g_ref = refs[:2]
        ex_ins = refs[2:2 + ex.n]
        hn_ref, mt_ref = refs[2 + ex.n:4 + ex.n]
        ex_outs = refs[4 + ex.n:4 + 2 * ex.n]
        slots, slot_sem = refs[4 + 2 * ex.n:6 + 2 * ex.n]
        sems = refs[6 + 2 * ex.n:]
        i = pl.program_id(0)

        @pl.when(i == 0)
        def _():
            ex.start(ex_ins, ex_outs, sems)

        def norm(hv):
            r = lax.rsqrt(jnp.mean(hv * hv, axis=-1, keepdims=True) + EPS)
            return (hv * r * g_ref[...]).astype(BF16)

        @pl.when(i < n_seq)
        def _():
            hn_ref[...] = norm(x_ref[...])

        @pl.when(i == n_seq)
        def _():
            ex.wait(ex_ins, ex_outs, sems)
            cp = pltpu.make_async_copy(ex_outs[0], slots, slot_sem)
            cp.start()
            mt_ref[...] = jnp.zeros_like(mt_ref)
            cp.wait()
            for s in range(N_DEV):
                mt_ref[META_TILE - n_meta:META_TILE, s * dcol:(s + 1) * dcol] = slots[s]
            hn_ref[0:META_TILE, :] = norm(mt_ref[...])

    outs = pl.pallas_call(
        body, name="rms_norm",
        grid=(n_seq + 1,),
        in_specs=[pl.BlockSpec((tm, D), lambda i: (jnp.minimum(i, n_seq - 1), 0)),
                  pl.BlockSpec((1, D), lambda i: (0, 0))] + ex.specs,
        out_specs=[pl.BlockSpec((tm, D), lambda i: (i, 0)),
                   pl.BlockSpec((META_TILE, D), lambda i: (0, 0))] + ex.specs,
        out_shape=[jax.ShapeDtypeStruct((SEQ + tm, D), BF16), jax.ShapeDtypeStruct((META_TILE, D), F32)]
                  + ex.out_shape,
        scratch_shapes=[pltpu.VMEM((N_DEV, n_meta, dcol), F32), pltpu.SemaphoreType.DMA] + ex.scratch,
        compiler_params=_params(("arbitrary",), 32),
    )(xs, ln_g, *ex_parts)
    return outs[0], outs[1], outs[2:]


def _chip_visited(q):
    mine = 2 * lax.axis_index("x") + lax.axis_index("y")
    return mine ^ (((q & 1) << 1) | (q >> 1))


def _in_proj_gather(hn, n_rows, w_shard, ex, ex_parts):
    R, D = n_rows, hn.shape[1]
    wb = w_shard.shape[1]
    E, tn = wb * N_DEV, 2 * wb
    n_q = N_DEV // 2
    tm = _pick_tile(R, min(768, R // 2), MXU_DIM)
    n_i = R // tm

    def body(*refs):
        hn_ref, wsh_hbm = refs[:2]
        ex_ins = refs[2:2 + ex.n]
        proj_ref, wg_hbm = refs[2 + ex.n:4 + ex.n]
        ex_outs = refs[4 + ex.n:4 + 2 * ex.n]
        w_vmem, send_sems, recv_sems, local_sem, vmem_sems = refs[4 + 2 * ex.n:9 + 2 * ex.n]
        ex_sems = refs[9 + 2 * ex.n:]
        q, i = pl.program_id(0), pl.program_id(1)
        x, y, c = lax.axis_index("x"), lax.axis_index("y"), lax.axis_index("c")
        me, sibling = (x, y, c), (x, y, 1 - c)
        chips = [(1 - x, y), (x, 1 - y), (1 - x, 1 - y)]

        def block(dev):
            return wg_hbm.at[:, pl.ds(pl.multiple_of((4 * dev[0] + 2 * dev[1] + dev[2]) * wb, wb), wb)]

        def copy(k, dev, to, src=None):
            return pltpu.make_async_remote_copy(
                src_ref=block(dev) if src is None else src, dst_ref=block(dev),
                send_sem=send_sems.at[k], recv_sem=recv_sems.at[k], device_id=to, device_id_type=MESH)

        def to_vmem(p):
            cols = pl.ds(pl.multiple_of(_chip_visited(p) * tn, tn), tn)
            return pltpu.make_async_copy(wg_hbm.at[:, cols], w_vmem.at[p % 2], vmem_sems.at[p % 2])

        mine = pltpu.make_async_copy(wsh_hbm, block(me), local_sem)
        first = [copy(0, me, sibling, src=wsh_hbm)] + [copy(1 + j, me, (*chip, c), src=wsh_hbm)
                                                       for j, chip in enumerate(chips)]
        passed = [copy(4 + j, (*chip, c), sibling) for j, chip in enumerate(chips)]

        @pl.when((q == 0) & (i == 0))
        def _():
            mine.start()
            for cp in first:
                cp.start()
            ex.start(ex_ins, ex_outs, ex_sems)
            mine.wait()
            copy(0, sibling, me).wait_recv()
            to_vmem(0).start()
            to_vmem(0).wait()

        for p in range(1, n_q):
            chip = chips[p - 1]

            @pl.when((q == p - 1) & (i == n_i - 2))
            def _():
                copy(p, (*chip, c), me).wait_recv()
                passed[p - 1].start()

            @pl.when((q == p - 1) & (i == n_i - 1))
            def _():
                copy(3 + p, (*chip, 1 - c), me).wait_recv()
                to_vmem(p).start()

            @pl.when((q == p) & (i == 0))
            def _():
                to_vmem(p).wait()

        proj_ref[...] = _dot(hn_ref[...], w_vmem[q % 2]).astype(BF16)

        @pl.when((q == n_q - 1) & (i == n_i - 1))
        def _():
            for cp in first + passed:
                cp.wait_send()
            ex.wait(ex_ins, ex_outs, ex_sems)

    hbm = pl.BlockSpec(memory_space=pl.ANY)
    outs = pl.pallas_call(
        body, name="in_proj",
        grid=(n_q, n_i),
        in_specs=[pl.BlockSpec((tm, D), lambda q, i: (i, 0)), hbm] + ex.specs,
        out_specs=[pl.BlockSpec((tm, tn), lambda q, i: (i, _chip_visited(q))), hbm] + ex.specs,
        out_shape=[jax.ShapeDtypeStruct((R, E), BF16), jax.ShapeDtypeStruct((D, E), BF16)] + ex.out_shape,
        scratch_shapes=[pltpu.VMEM((2, D, tn), BF16), pltpu.SemaphoreType.DMA((7,)), pltpu.SemaphoreType.DMA((7,)),
                        pltpu.SemaphoreType.DMA, pltpu.SemaphoreType.DMA((2,))] + ex.scratch,
        compiler_params=_params(("arbitrary", "arbitrary"), 48),
    )(hn, w_shard, *ex_parts)
    return outs[0], outs[1], outs[2:]


def _rope_chunk(cb_ref, sb_ref, ci_ref, si_ref):
    cb, sb, ci, si = cb_ref[0], sb_ref[0], ci_ref[...], si_ref[...]
    return cb * ci - sb * si, sb * ci + cb * si


def _rot(t, cos, sin, half):
    t1, t2 = t[:, :half], t[:, half:]
    return jnp.concatenate([t1 * cos - t2 * sin, t1 * sin + t2 * cos], axis=-1)


def _rot_inv(t, cos, sin, half):
    t1, t2 = t[:, :half], t[:, half:]
    return jnp.concatenate([t1 * cos + t2 * sin, t2 * cos - t1 * sin], axis=-1)


def _chunk_order(n_chunks):
    lead = META_TILE // CHUNK
    return lambda l: (l + n_chunks - lead) % n_chunks


def _retention_fwd(proj, rope, gn_g, mix):
    R, E = proj.shape
    RW = gn_g.shape[1]
    H = RET_HEADS
    hd = RW // H
    half = hd // 2
    NC = R // CHUNK
    mask, qd, kd, cd = _decay_tables(H)
    scale = float(hd) ** -0.5
    phys = _chunk_order(NC)

    def body(p_ref, cb_ref, sb_ref, ci_ref, si_ref, mask_ref, qd_ref, kd_ref, gn_ref, y_ref, st_ref, state):
        @pl.when(pl.program_id(0) == 0)
        def _():
            state[...] = jnp.zeros_like(state)

        cs, sn = _rope_chunk(cb_ref, sb_ref, ci_ref, si_ref)
        hs = range(H)
        col = lambda j, h: slice(j * RW + h * hd, j * RW + (h + 1) * hd)
        qr = [_rot(p_ref[:, col(0, h)].astype(F32), cs, sn, half) for h in hs]
        kr = [_rot(p_ref[:, col(1, h)].astype(F32), cs, sn, half) * scale for h in hs]
        v = [p_ref[:, col(2, h)] for h in hs]
        s_prev = [state[h] for h in hs]
        s_prev_b = [s_prev[h].astype(BF16) for h in hs]
        s = [(_dot(qr[h].astype(BF16), kr[h].astype(BF16), NT) * mask_ref[h]).astype(BF16) for h in hs]
        y_raw = [_dot(s[h], v[h]) + _dot((qr[h] * qd_ref[h]).astype(BF16), s_prev_b[h]) for h in hs]
        s_new = [s_prev[h] * cd[h] + _dot((kr[h] * kd_ref[h]).astype(BF16), v[h], TN) for h in hs]
        for h in hs:
            st_ref[0, h] = s_prev_b[h]
            state[h] = s_new[h]
        for h in hs:
            g = p_ref[:, col(3, h)].astype(F32)
            mu = jnp.mean(y_raw[h], axis=-1, keepdims=True)
            yc = y_raw[h] - mu
            var = jnp.mean(yc * yc, axis=-1, keepdims=True)
            out = yc * lax.rsqrt(var + EPS) * gn_ref[:, col(0, h)] * (g * _sigmoid(g))
            y_ref[:, col(0, h)] = out.astype(BF16)

    const3 = lambda l: (0, 0, 0)
    return pl.pallas_call(
        body, name="retention_fwd",
        grid=(NC,),
        in_specs=[pl.BlockSpec((CHUNK, 4 * RW), lambda l: (phys(l), 0)),
                  pl.BlockSpec((1, 1, half), lambda l: (phys(l), 0, 0)),
                  pl.BlockSpec((1, 1, half), lambda l: (phys(l), 0, 0)),
                  pl.BlockSpec((CHUNK, half), lambda l: (0, 0)),
                  pl.BlockSpec((CHUNK, half), lambda l: (0, 0)),
                  pl.BlockSpec((H, CHUNK, CHUNK), const3),
                  pl.BlockSpec((H, CHUNK, 1), const3),
                  pl.BlockSpec((H, CHUNK, 1), const3),
                  pl.BlockSpec((1, RW), lambda l: (0, 0))],
        out_specs=[pl.BlockSpec((CHUNK, RW), lambda l: (phys(l), 0)),
                   pl.BlockSpec((1, H, hd, hd), lambda l: (phys(l), 0, 0, 0))],
        out_shape=[jax.ShapeDtypeStruct((R, mix), BF16), jax.ShapeDtypeStruct((NC, H, hd, hd), BF16)],
        scratch_shapes=[pltpu.VMEM((H, hd, hd), F32)],
        compiler_params=_params(("arbitrary",), 32),
    )(proj, *rope, jnp.asarray(mask), jnp.asarray(qd), jnp.asarray(kd), gn_g)


CONV_ROWS = 64
CONV_LANES = 128
LANE = 128
ELEM_ROWS = 32


def _conv_order(n_tiles):
    return lambda l: (l + n_tiles - 1) % n_tiles


def _halo_block(n_tiles, tm):
    per = tm // HALO
    return lambda l: ((l + n_tiles - 2) % n_tiles) * per + per - 1


def _fill_shifted(src, dst):
    rows, width = dst.shape[1], dst.shape[2]
    step = _pick_tile(rows, 64, 8)
    for r in range(1, 8):
        for r0 in range(0, rows, step):
            for l0 in range(0, width, CONV_LANES):
                dst[r - 1, r0:r0 + step, l0:l0 + CONV_LANES] = src[r + r0:r + r0 + step, l0:l0 + CONV_LANES]


def _at_offset(src, shifted, off, r0, rows, lanes):
    r = off % 8
    a = off - r + r0
    if r == 0:
        return src[a:a + rows, lanes]
    return shifted[r - 1, a:a + rows, lanes]


def _fill_glu(first, a_ref, b_ref, ah_ref, bh_ref, u_ext, tm):
    uh = ah_ref[...].astype(F32) * _sigmoid(bh_ref[...].astype(F32))
    u_ext[0:HALO, :] = jnp.where(first, 0.0, uh)
    for r0 in range(0, tm, ELEM_ROWS):
        rows = slice(r0, r0 + ELEM_ROWS)
        u_ext[HALO + r0:HALO + r0 + ELEM_ROWS, :] = a_ref[rows, :].astype(F32) * _sigmoid(b_ref[rows, :].astype(F32))


def _layer_norm(cv, lg_ref, lb_ref):
    mu = jnp.mean(cv, axis=-1, keepdims=True)
    cc = cv - mu
    rstd = lax.rsqrt(jnp.mean(cc * cc, axis=-1, keepdims=True) + EPS)
    xh = cc * rstd
    return xh, rstd, xh * lg_ref[...] + lb_ref[...]


def _conv_fwd(proj, y_in, dw_w, dw_b, ln_g, ln_b, pw_w, pw_b, ex, ex_parts):
    R, E = proj.shape
    CW = pw_w.shape[0]
    tm = META_TILE
    NTL = R // tm
    phys = _conv_order(NTL)
    halo = _halo_block(NTL, tm)
    cb = (E - 3 * CW) // CW
    base = HALO - (CONV_K - 1)

    def body(*refs):
        a_ref, b_ref, g_ref, ah_ref, bh_ref, w_ref, wb_ref, lg_ref, lb_ref, pw_ref, pb_ref, yin_ref = refs[:12]
        ex_ins = refs[12:12 + ex.n]
        y_ref, c_ref = refs[12 + ex.n:14 + ex.n]
        ex_outs = refs[14 + ex.n:14 + 2 * ex.n]
        u_ext, u_sh, s_scr, upw_scr = refs[14 + 2 * ex.n:18 + 2 * ex.n]
        sems = refs[18 + 2 * ex.n:]

        @pl.when(pl.program_id(0) == 0)
        def _():
            ex.start(ex_ins, ex_outs, sems)

        _fill_glu(pl.program_id(0) == 0, a_ref, b_ref, ah_ref, bh_ref, u_ext, tm)
        _fill_shifted(u_ext, u_sh)
        for r0 in range(0, tm, CONV_ROWS):
            for l0 in range(0, CW, CONV_LANES):
                lanes = slice(l0, l0 + CONV_LANES)
                acc = None
                for k in range(CONV_K):
                    term = _at_offset(u_ext, u_sh, base + k, r0, CONV_ROWS, lanes) * w_ref[k:k + 1, lanes]
                    acc = term if acc is None else acc + term
                c_ref[r0:r0 + CONV_ROWS, lanes] = acc + wb_ref[:, lanes]
        blocks = [slice(r0, r0 + ELEM_ROWS) for r0 in range(0, tm, ELEM_ROWS)]
        for rows in blocks:
            _, _, ln = _layer_norm(c_ref[rows, :], lg_ref, lb_ref)
            s_scr[rows, :] = (ln * _sigmoid(ln)).astype(BF16)
        upw_scr[...] = _dot(s_scr[...], pw_ref[...]) + pb_ref[...]
        for rows in blocks:
            g = g_ref[rows, :].astype(F32)
            y_ref[rows, :] = (upw_scr[rows, :] * (g * _sigmoid(g))).astype(BF16)

        @pl.when(pl.program_id(0) == NTL - 1)
        def _():
            ex.wait(ex_ins, ex_outs, sems)

    row = lambda l: (0, 0)
    outs = pl.pallas_call(
        body, name="conv_fwd",
        grid=(NTL,),
        in_specs=[pl.BlockSpec((tm, CW), lambda l: (phys(l), cb)),
                  pl.BlockSpec((tm, CW), lambda l: (phys(l), cb + 1)),
                  pl.BlockSpec((tm, CW), lambda l: (phys(l), cb + 2)),
                  pl.BlockSpec((HALO, CW), lambda l: (halo(l), cb)),
                  pl.BlockSpec((HALO, CW), lambda l: (halo(l), cb + 1)),
                  pl.BlockSpec((HALO, CW), row),
                  pl.BlockSpec((1, CW), row), pl.BlockSpec((1, CW), row), pl.BlockSpec((1, CW), row),
                  pl.BlockSpec((CW, CW), row),
                  pl.BlockSpec((1, CW), row),
                  pl.BlockSpec(memory_space=pl.ANY)] + ex.specs,
        out_specs=[pl.BlockSpec((tm, CW), lambda l: (phys(l), 1)),
                   pl.BlockSpec((tm, CW), lambda l: (phys(l), 0))] + ex.specs,
        out_shape=[jax.ShapeDtypeStruct(y_in.shape, BF16), jax.ShapeDtypeStruct((R, CW), F32)] + ex.out_shape,
        input_output_aliases={11: 0},
        scratch_shapes=[pltpu.VMEM((HALO + tm, CW), F32), pltpu.VMEM((7, tm + HALO - 8, CW), F32),
                        pltpu.VMEM((tm, CW), BF16), pltpu.VMEM((tm, CW), F32)] + ex.scratch,
        compiler_params=_params(("arbitrary",), 48),
    )(proj, proj, proj, proj, proj, dw_w, dw_b, ln_g, ln_b, pw_w, pw_b, y_in, *ex_parts)
    return outs[0], outs[1], outs[2:]


def _out_proj_loss(xs, meta_tile, y, w_out, final_g, target):
    SEQ, D = xs.shape
    R, MIX = y.shape
    tm = META_TILE
    n_seq = SEQ // tm
    n_tiles = R // tm
    rows_out = _pick_tile(MIX, 256)

    def body(x_ref, mt_ref, y_ref, w_hbm, fg_ref, t_ref, dh2_ref, dy_ref, dwo_hbm, dfg_ref, loss_ref,
             w_scr, acc, stage, sem, put_sems):
        i = pl.program_id(0)

        @pl.when(i == 0)
        def _():
            cp = pltpu.make_async_copy(w_hbm, w_scr, sem)
            cp.start()
            acc[...] = jnp.zeros_like(acc)
            dfg_ref[...] = jnp.zeros_like(dfg_ref)
            loss_ref[...] = jnp.zeros_like(loss_ref)
            cp.wait()

        yb = y_ref[...]
        h2 = jnp.where(i < n_seq, x_ref[...], mt_ref[...]) + _dot(yb, w_scr[...])
        r2 = lax.rsqrt(jnp.mean(h2 * h2, axis=-1, keepdims=True) + EPS)
        n = h2 * r2
        fg = fg_ref[...]
        err = jnp.where(i < n_seq, n * fg - t_ref[...], 0.0)
        loss_ref[...] += 0.5 * jnp.sum(jnp.mean(err * err, axis=-1, keepdims=True), axis=0, keepdims=True)
        dout = err * (1.0 / D)
        dfg_ref[...] += jnp.sum(dout * n, axis=0, keepdims=True)
        dn = dout * fg
        dh2 = r2 * (dn - n * jnp.mean(dn * n, axis=-1, keepdims=True))
        dh2_ref[...] = dh2
        dh2b = dh2.astype(BF16)
        dy_ref[...] = _dot(dh2b, w_scr[...], NT).astype(BF16)
        acc[...] += _dot(yb, dh2b, TN)

        @pl.when(i == n_tiles - 1)
        def _():
            def put(n):
                rows = slice(n * rows_out, (n + 1) * rows_out)
                return pltpu.make_async_copy(stage.at[n % 2], dwo_hbm.at[rows, :], put_sems.at[n % 2])

            n_put = MIX // rows_out
            for n in range(n_put):
                if n >= 2:
                    put(n - 2).wait()
                stage[n % 2] = acc[n * rows_out:(n + 1) * rows_out, :].astype(BF16)
                put(n).start()
            for n in range(max(n_put - 2, 0), n_put):
                put(n).wait()

    row = lambda i: (0, 0)
    return pl.pallas_call(
        body, name="out_proj_loss",
        grid=(n_tiles,),
        in_specs=[pl.BlockSpec((tm, D), lambda i: (jnp.minimum(i, n_seq - 1), 0)),
                  pl.BlockSpec((tm, D), row),
                  pl.BlockSpec((tm, MIX), lambda i: (i, 0)),
                  pl.BlockSpec(memory_space=pl.ANY),
                  pl.BlockSpec((1, D), row),
                  pl.BlockSpec((tm, D), lambda i: (jnp.minimum(i, n_seq - 1), 0))],
        out_specs=[pl.BlockSpec((tm, D), lambda i: (i, 0)),
                   pl.BlockSpec((tm, MIX), lambda i: (i, 0)),
                   pl.BlockSpec(memory_space=pl.ANY),
                   pl.BlockSpec((1, D), row),
                   pl.BlockSpec((1, 1), row)],
        out_shape=[jax.ShapeDtypeStruct((R, D), F32), jax.ShapeDtypeStruct((R, MIX), BF16),
                   jax.ShapeDtypeStruct((MIX, D), BF16), jax.ShapeDtypeStruct((1, D), F32),
                   jax.ShapeDtypeStruct((1, 1), F32)],
        scratch_shapes=[pltpu.VMEM((MIX, D), BF16), pltpu.VMEM((MIX, D), F32), pltpu.VMEM((2, rows_out, D), BF16),
                        pltpu.SemaphoreType.DMA, pltpu.SemaphoreType.DMA((2,))],
        compiler_params=_params(("arbitrary",), 60),
    )(xs, meta_tile, y, w_out, final_g, target)


def _conv_bwd(proj, conv_out, dy, dw_w, ln_g, ln_b, pw_w, pw_b):
    R, E = proj.shape
    CW = pw_w.shape[0]
    tm = META_TILE
    NTL = R // tm
    order = _conv_order(NTL)
    phys = lambda i: order(NTL - 1 - i)
    halo_l = _halo_block(NTL, tm)
    halo = lambda i: halo_l(NTL - 1 - i)
    cb = (E - 3 * CW) // CW
    base = HALO - (CONV_K - 1)

    def body(a_ref, b_ref, g_ref, ah_ref, bh_ref, c_ref, dy_ref, w_ref, lg_ref, lb_ref, pw_ref, pb_ref,
             dp_ref, dpw_ref, dww_ref, vec_ref, u_ext, u_sh, dc_ext, dc_sh, du_scr, dww_acc, dpw_acc,
             xh_scr, rstd_scr, ln_scr, sg_scr, upw_scr, s_scr, dupw_scr):
        i = pl.program_id(0)

        @pl.when(i == 0)
        def _():
            dpw_acc[...] = jnp.zeros_like(dpw_acc)
            dww_ref[...] = jnp.zeros_like(dww_ref)
            vec_ref[...] = jnp.zeros_like(vec_ref)
            dww_acc[...] = jnp.zeros_like(dww_acc)
            dc_ext[tm:tm + HALO, :] = jnp.zeros((HALO, CW), F32)

        _fill_glu(i == NTL - 1, a_ref, b_ref, ah_ref, bh_ref, u_ext, tm)
        _fill_shifted(u_ext, u_sh)
        blocks = [slice(r0, r0 + ELEM_ROWS) for r0 in range(0, tm, ELEM_ROWS)]
        for rows in blocks:
            xh, rstd, ln = _layer_norm(c_ref[rows, :], lg_ref, lb_ref)
            sg = _sigmoid(ln)
            xh_scr[rows, :], rstd_scr[rows, :], ln_scr[rows, :], sg_scr[rows, :] = xh, rstd, ln, sg
            s_scr[rows, :] = (ln * sg).astype(BF16)
        upw_scr[...] = _dot(s_scr[...], pw_ref[...]) + pb_ref[...]
        col_sum = jnp.zeros((1, CW), F32)
        for rows in blocks:
            g = g_ref[rows, :].astype(F32)
            sgg = _sigmoid(g)
            dyc = dy_ref[rows, :].astype(F32)
            dp_ref[rows, 2 * CW:3 * CW] = (dyc * upw_scr[rows, :] * _dsilu(g, sgg)).astype(BF16)
            dupw = dyc * (g * sgg)
            dupw_scr[rows, :] = dupw.astype(BF16)
            col_sum = col_sum + jnp.sum(dupw, axis=0, keepdims=True)
        vec_ref[0:1, :] += col_sum
        dpw_acc[...] += _dot(s_scr[...], dupw_scr[...], TN)
        upw_scr[...] = _dot(dupw_scr[...], pw_ref[...], NT)
        sum_g, sum_b, sum_c = col_sum * 0.0, col_sum * 0.0, col_sum * 0.0
        for rows in blocks:
            xh, rstd = xh_scr[rows, :], rstd_scr[rows, :]
            dln = upw_scr[rows, :] * _dsilu(ln_scr[rows, :], sg_scr[rows, :])
            sum_g = sum_g + jnp.sum(dln * xh, axis=0, keepdims=True)
            sum_b = sum_b + jnp.sum(dln, axis=0, keepdims=True)
            dxh = dln * lg_ref[...]
            dc = rstd * (dxh - jnp.mean(dxh, axis=-1, keepdims=True)
                         - xh * jnp.mean(dxh * xh, axis=-1, keepdims=True))
            sum_c = sum_c + jnp.sum(dc, axis=0, keepdims=True)
            dc_ext[rows, :] = dc
        vec_ref[1:2, :] += sum_g
        vec_ref[2:3, :] += sum_b
        vec_ref[3:4, :] += sum_c
        _fill_shifted(dc_ext, dc_sh)

        for l0 in range(0, CW, CONV_LANES):
            lanes = slice(l0, l0 + CONV_LANES)
            for r0 in range(0, tm, CONV_ROWS):
                acc = None
                for k in range(CONV_K):
                    term = _at_offset(dc_ext, dc_sh, CONV_K - 1 - k, r0, CONV_ROWS, lanes) * w_ref[k:k + 1, lanes]
                    acc = term if acc is None else acc + term
                du_scr[r0:r0 + CONV_ROWS, lanes] = acc

        n_grp = tm // 8
        by_shift = [[(k, (base + k) // 8) for k in range(CONV_K) if (base + k) % 8 == r] for r in range(8)]
        for l0 in range(0, CW, LANE):
            lane = slice(l0, l0 + LANE)
            for r in range(8):
                src = u_ext if r == 0 else u_sh.at[r - 1]
                a_lo, a_hi = by_shift[r][0][1], by_shift[r][-1][1]
                sums = {k: None for k, _ in by_shift[r]}
                dcg = {}
                for gi in range(a_lo, n_grp + a_hi):
                    if gi - a_lo < n_grp:
                        dcg[gi - a_lo] = dc_ext[8 * (gi - a_lo):8 * (gi - a_lo) + 8, lane]
                    dcg.pop(gi - a_hi - 1, None)
                    ug = src[8 * gi:8 * gi + 8, lane]
                    for k, a in by_shift[r]:
                        if 0 <= gi - a < n_grp:
                            prod = dcg[gi - a] * ug
                            sums[k] = prod if sums[k] is None else sums[k] + prod
                for k, _ in by_shift[r]:
                    dww_acc[k, :, lane] += sums[k]

        for rows in blocks:
            du = du_scr[rows, :]
            sgb = _sigmoid(b_ref[rows, :].astype(F32))
            dp_ref[rows, 0:CW] = (du * sgb).astype(BF16)
            dp_ref[rows, CW:2 * CW] = (du * a_ref[rows, :].astype(F32) * sgb * (1.0 - sgb)).astype(BF16)
        dc_ext[tm:tm + HALO, :] = dc_ext[0:HALO, :]

        @pl.when(i == NTL - 1)
        def _():
            for k in range(CONV_K):
                dww_ref[k:k + 1, :] = jnp.sum(dww_acc[k], axis=0, keepdims=True)
            dpw_ref[...] = dpw_acc[...].astype(BF16)

    row = lambda i: (0, 0)
    return pl.pallas_call(
        body, name="conv_bwd",
        grid=(NTL,),
        in_specs=[pl.BlockSpec((tm, CW), lambda i: (phys(i), cb)),
                  pl.BlockSpec((tm, CW), lambda i: (phys(i), cb + 1)),
                  pl.BlockSpec((tm, CW), lambda i: (phys(i), cb + 2)),
                  pl.BlockSpec((HALO, CW), lambda i: (halo(i), cb)),
                  pl.BlockSpec((HALO, CW), lambda i: (halo(i), cb + 1)),
                  pl.BlockSpec((tm, CW), lambda i: (phys(i), 0)),
                  pl.BlockSpec((tm, CW), lambda i: (phys(i), 1)),
                  pl.BlockSpec((HALO, CW), row),
                  pl.BlockSpec((1, CW), row), pl.BlockSpec((1, CW), row),
                  pl.BlockSpec((CW, CW), row),
                  pl.BlockSpec((1, CW), row)],
        out_specs=[pl.BlockSpec((tm, 3 * CW), lambda i: (phys(i), 0)),
                   pl.BlockSpec((CW, CW), row),
                   pl.BlockSpec((HALO, CW), row),
                   pl.BlockSpec((8, CW), row)],
        out_shape=[jax.ShapeDtypeStruct((R, 3 * CW), BF16), jax.ShapeDtypeStruct((CW, CW), BF16),
                   jax.ShapeDtypeStruct((HALO, CW), F32), jax.ShapeDtypeStruct((8, CW), F32)],
        scratch_shapes=[pltpu.VMEM((HALO + tm, CW), F32), pltpu.VMEM((7, tm + HALO - 8, CW), F32),
                        pltpu.VMEM((tm + HALO, CW), F32), pltpu.VMEM((7, tm + HALO - 8, CW), F32),
                        pltpu.VMEM((tm, CW), F32), pltpu.VMEM((CONV_K, 8, CW), F32), pltpu.VMEM((CW, CW), F32),
                        pltpu.VMEM((tm, CW), F32), pltpu.VMEM((tm, 1), F32), pltpu.VMEM((tm, CW), F32),
                        pltpu.VMEM((tm, CW), F32), pltpu.VMEM((tm, CW), F32), pltpu.VMEM((tm, CW), BF16),
                        pltpu.VMEM((tm, CW), BF16)],
        compiler_params=_params(("arbitrary",), 60),
    )(proj, proj, proj, proj, proj, conv_out, dy, dw_w, ln_g, ln_b, pw_w, pw_b)


def _retention_bwd(proj, rope, gn_g, states, dy, ex, ex_parts):
    R, E = proj.shape
    RW = gn_g.shape[1]
    H = RET_HEADS
    hd = RW // H
    half = hd // 2
    NC = R // CHUNK
    mask, qd, kd, cd = _decay_tables(H)
    scale = float(hd) ** -0.5
    order = _chunk_order(NC)
    phys = lambda i: order(NC - 1 - i)

    def body(*refs):
        p_ref, cb_ref, sb_ref, ci_ref, si_ref, mask_ref, qd_ref, kd_ref, gn_ref, st_ref, dy_ref = refs[:11]
        ex_ins = refs[11:11 + ex.n]
        dp_ref, dgn_ref = refs[11 + ex.n:13 + ex.n]
        ex_outs = refs[13 + ex.n:13 + 2 * ex.n]
        dstate = refs[13 + 2 * ex.n]
        sems = refs[14 + 2 * ex.n:]

        @pl.when(pl.program_id(0) == 0)
        def _():
            ex.start(ex_ins, ex_outs, sems)
            dstate[...] = jnp.zeros_like(dstate)
            dgn_ref[...] = jnp.zeros_like(dgn_ref)

        cs, sn = _rope_chunk(cb_ref, sb_ref, ci_ref, si_ref)
        hs = range(H)
        col = lambda j, h: slice(j * RW + h * hd, j * RW + (h + 1) * hd)
        qr = [_rot(p_ref[:, col(0, h)].astype(F32), cs, sn, half) for h in hs]
        kr = [_rot(p_ref[:, col(1, h)].astype(F32), cs, sn, half) * scale for h in hs]
        v = [p_ref[:, col(2, h)] for h in hs]
        qb = [qr[h].astype(BF16) for h in hs]
        kb = [kr[h].astype(BF16) for h in hs]
        qdb = [(qr[h] * qd_ref[h]).astype(BF16) for h in hs]
        kdb = [(kr[h] * kd_ref[h]).astype(BF16) for h in hs]
        s_prev = [st_ref[0, h] for h in hs]
        dst = [dstate[h] for h in hs]
        dstb = [dst[h].astype(BF16) for h in hs]
        sb = [(_dot(qb[h], kb[h], NT) * mask_ref[h]).astype(BF16) for h in hs]
        y_raw = [_dot(sb[h], v[h]) + _dot(qdb[h], s_prev[h]) for h in hs]
        dyrb, dg = [], []
        for h in hs:
            g = p_ref[:, col(3, h)].astype(F32)
            mu = jnp.mean(y_raw[h], axis=-1, keepdims=True)
            yc = y_raw[h] - mu
            rstd = lax.rsqrt(jnp.mean(yc * yc, axis=-1, keepdims=True) + EPS)
            xh = yc * rstd
            gn = gn_ref[:, col(0, h)]
            sg = _sigmoid(g)
            dyh = dy_ref[:, col(0, h)].astype(F32)
            dg.append((dyh * (xh * gn) * _dsilu(g, sg)).astype(BF16))
            dyn = dyh * (g * sg)
            dgn_ref[:, col(0, h)] += jnp.sum(dyn * xh, axis=0, keepdims=True)
            dxh = dyn * gn
            dyr = rstd * (dxh - jnp.mean(dxh, axis=-1, keepdims=True)
                          - xh * jnp.mean(dxh * xh, axis=-1, keepdims=True))
            dyrb.append(dyr.astype(BF16))
        dsb = [(_dot(dyrb[h], v[h], NT) * mask_ref[h]).astype(BF16) for h in hs]
        dqr = [_dot(dsb[h], kb[h]) + _dot(dyrb[h], s_prev[h], NT) * qd_ref[h] for h in hs]
        dkr = [_dot(dsb[h], qb[h], TN) + _dot(v[h], dstb[h], NT) * kd_ref[h] for h in hs]
        dv = [_dot(sb[h], dyrb[h], TN) + _dot(kdb[h], dstb[h]) for h in hs]
        dst_new = [dst[h] * cd[h] + _dot(qdb[h], dyrb[h], TN) for h in hs]
        for h in hs:
            dstate[h] = dst_new[h]
            dp_ref[:, col(0, h)] = _rot_inv(dqr[h], cs, sn, half).astype(BF16)
            dp_ref[:, col(1, h)] = (_rot_inv(dkr[h], cs, sn, half) * scale).astype(BF16)
            dp_ref[:, col(2, h)] = dv[h].astype(BF16)
            dp_ref[:, col(3, h)] = dg[h]

        @pl.when(pl.program_id(0) == NC - 1)
        def _():
            ex.wait(ex_ins, ex_outs, sems)

    const3 = lambda i: (0, 0, 0)
    outs = pl.pallas_call(
        body, name="retention_bwd",
        grid=(NC,),
        in_specs=[pl.BlockSpec((CHUNK, 4 * RW), lambda i: (phys(i), 0)),
                  pl.BlockSpec((1, 1, half), lambda i: (phys(i), 0, 0)),
                  pl.BlockSpec((1, 1, half), lambda i: (phys(i), 0, 0)),
                  pl.BlockSpec((CHUNK, half), lambda i: (0, 0)),
                  pl.BlockSpec((CHUNK, half), lambda i: (0, 0)),
                  pl.BlockSpec((H, CHUNK, CHUNK), const3),
                  pl.BlockSpec((H, CHUNK, 1), const3),
                  pl.BlockSpec((H, CHUNK, 1), const3),
                  pl.BlockSpec((1, RW), lambda i: (0, 0)),
                  pl.BlockSpec((1, H, hd, hd), lambda i: (phys(i), 0, 0, 0)),
                  pl.BlockSpec((CHUNK, RW), lambda i: (phys(i), 0))] + ex.specs,
        out_specs=[pl.BlockSpec((CHUNK, 4 * RW), lambda i: (phys(i), 0)),
                   pl.BlockSpec((1, RW), lambda i: (0, 0))] + ex.specs,
        out_shape=[jax.ShapeDtypeStruct((R, 4 * RW), BF16), jax.ShapeDtypeStruct((1, RW), F32)] + ex.out_shape,
        scratch_shapes=[pltpu.VMEM((H, hd, hd), F32)] + ex.scratch,
        compiler_params=_params(("arbitrary",), 32),
    )(proj, *rope, jnp.asarray(mask), jnp.asarray(qd), jnp.asarray(kd), gn_g, states, dy, *ex_parts)
    return outs[0], outs[1], outs[2:]


def _dproj_specs(tk, tn, n_ret, tile_axis, col_axis):
    def ret_map(*ids):
        t, j = ids[tile_axis], ids[col_axis]
        return (jnp.where(j < n_ret, t, 0), jnp.minimum(j, n_ret - 1))

    def conv_map(*ids):
        t, j = ids[tile_axis], ids[col_axis]
        return (jnp.where(j >= n_ret, t, 0), jnp.maximum(j - n_ret, 0))

    return pl.BlockSpec((tk, tn), ret_map), pl.BlockSpec((tk, tn), conv_map)


def _w_in_grad(hn, dp_ret, dp_conv):
    R, D = dp_ret.shape[0], hn.shape[1]
    tn = _pick_tile(dp_conv.shape[1] // 3, 1024, 128)
    n_ret, n_conv = dp_ret.shape[1] // tn, dp_conv.shape[1] // tn
    E = dp_ret.shape[1] + dp_conv.shape[1]
    tk = _pick_tile(R, 1024, MXU_DIM)
    n_t = R // tk
    ret_spec, conv_spec = _dproj_specs(tk, tn, n_ret, 1, 0)

    def body(hn_ref, r_ref, c_ref, out_ref, acc):
        j, t = pl.program_id(0), pl.program_id(1)

        for first, ret in ((True, True), (True, False), (False, True), (False, False)):
            @pl.when(((t == 0) == first) & ((j < n_ret) == ret))
            def _():
                part = _dot(hn_ref[...], (r_ref if ret else c_ref)[...], TN)
                if first:
                    acc[...] = part
                else:
                    acc[...] += part

        @pl.when(t == n_t - 1)
        def _():
            out_ref[...] = acc[...].astype(BF16)

    return pl.pallas_call(
        body, name="w_in_grad",
        grid=(n_ret + n_conv, n_t),
        in_specs=[pl.BlockSpec((tk, D), lambda j, t: (t, 0)), ret_spec, conv_spec],
        out_specs=pl.BlockSpec((D, tn), lambda j, t: (0, j)),
        out_shape=jax.ShapeDtypeStruct((D, E), BF16),
        scratch_shapes=[pltpu.VMEM((D, tn), F32)],
        compiler_params=_params(("arbitrary", "arbitrary"), 48),
    )(hn, dp_ret, dp_conv)


def _h_grad(dp_ret, dp_conv, w_in, xs, meta_tile, dh2, ln_g, ex, ex_parts):
    R, D = dh2.shape
    te = CHUNK
    n_x = xs.shape[0] // te
    n_m = meta_tile.shape[0] // te
    tn = _pick_tile(dp_conv.shape[1] // 3, 1024, 128)
    n_ret, n_conv = dp_ret.shape[1] // tn, dp_conv.shape[1] // tn
    n_k = n_ret + n_conv
    tm = _pick_tile(R, 1024, meta_tile.shape[0])
    n_e = tm // te
    n_t = R // tm
    assert n_e <= n_k and (n_x + n_m) * te == R

    def ret_map(t, k):
        return (jnp.where(k < n_ret, jnp.minimum(t, n_t - 1), 0), jnp.minimum(k, n_ret - 1))

    def conv_map(t, k):
        return (jnp.where(k >= n_ret, jnp.minimum(t, n_t - 1), 0), jnp.maximum(k - n_ret, 0))

    def row_block(t, k):
        return jnp.maximum(t - 1, 0) * n_e + jnp.where(t > 0, jnp.minimum(k, n_e - 1), 0)

    def body(*refs):
        r_ref, c_ref, w_hbm, w_ref, x_ref, mt_ref, dh2_ref, g_ref = refs[:8]
        ex_ins = refs[8:8 + ex.n]
        o = 8 + ex.n
        dh_ref, dlg_ref = refs[o:o + 2]
        ex_outs = refs[o + 2:o + 2 + ex.n]
        acc, w_keep, keep_sems = refs[o + 2 + ex.n:o + 5 + ex.n]
        sems = refs[o + 5 + ex.n:]
        t, k = pl.program_id(0), pl.program_id(1)
        cur, old = t % 2, (t + 1) % 2

        def keep(j):
            return pltpu.make_async_copy(w_hbm.at[:, j * tn:(j + 1) * tn], w_keep.at[j], keep_sems.at[j])

        @pl.when((k == 0) & (t == 0))
        def _():
            for j in range(n_ret):
                keep(j).start()
            ex.start(ex_ins, ex_outs, sems)
            dlg_ref[...] = jnp.zeros_like(dlg_ref)

        for j in range(n_ret):
            @pl.when((k == j) & (t == 0))
            def _():
                keep(j).wait()

        @pl.when((k == 0) & (t < n_t))
        def _():
            acc[cur] = _dot(r_ref[...], w_keep[0], NT)

        @pl.when((k > 0) & (k < n_ret) & (t < n_t))
        def _():
            acc[cur] += _dot(r_ref[...], w_keep[k], NT)

        @pl.when((k >= n_ret) & (t < n_t))
        def _():
            acc[cur] += _dot(c_ref[...], w_ref[...], NT)

        @pl.when((k < n_e) & (t > 0))
        def _():
            hv = jnp.where(row_block(t, k) < n_x, x_ref[...], mt_ref[...])
            r = lax.rsqrt(jnp.mean(hv * hv, axis=-1, keepdims=True) + EPS)
            nrm = hv * r
            dhn = acc[old, pl.ds(pl.multiple_of(k * te, te), te), :]
            dlg_ref[...] += jnp.sum(dhn * nrm, axis=0, keepdims=True)
            dn = dhn * g_ref[...]
            dh_ref[...] = dh2_ref[...] + r * (dn - nrm * jnp.mean(dn * nrm, axis=-1, keepdims=True))

        @pl.when((k == n_k - 1) & (t == n_t))
        def _():
            ex.wait(ex_ins, ex_outs, sems)

    row = lambda t, k: (0, 0)
    outs = pl.pallas_call(
        body, name="h_grad",
        grid=(n_t + 1, n_k),
        in_specs=[pl.BlockSpec((tm, tn), ret_map), pl.BlockSpec((tm, tn), conv_map),
                  pl.BlockSpec(memory_space=pl.ANY),
                  pl.BlockSpec((D, tn), lambda t, k: (0, jnp.maximum(k, n_ret))),
                  pl.BlockSpec((te, D), lambda t, k: (jnp.minimum(row_block(t, k), n_x - 1), 0)),
                  pl.BlockSpec((te, D), lambda t, k: (jnp.clip(row_block(t, k) - n_x, 0, n_m - 1), 0)),
                  pl.BlockSpec((te, D), lambda t, k: (row_block(t, k), 0)),
                  pl.BlockSpec((1, D), row)] + ex.specs,
        out_specs=[pl.BlockSpec((te, D), lambda t, k: (row_block(t, k), 0)),
                   pl.BlockSpec((1, D), row)] + ex.specs,
        out_shape=[jax.ShapeDtypeStruct((R, D), F32), jax.ShapeDtypeStruct((1, D), F32)] + ex.out_shape,
        scratch_shapes=[pltpu.VMEM((2, tm, D), F32), pltpu.VMEM((n_ret, D, tn), BF16),
                        pltpu.SemaphoreType.DMA((n_ret,))] + ex.scratch,
        compiler_params=_params(("arbitrary", "arbitrary"), 60),
    )(dp_ret, dp_conv, w_in, w_in, xs, meta_tile, dh2, ln_g, *ex_parts)
    return outs[0], outs[1], outs[2:]


def _adamw(w, g, m, v):
    m = ADAM_B1 * m + (1.0 - ADAM_B1) * g
    v = ADAM_B2 * v + (1.0 - ADAM_B2) * (g * g)
    m_hat = m / (1.0 - ADAM_B1 ** ADAM_STEP)
    v_hat = v / (1.0 - ADAM_B2 ** ADAM_STEP)
    delta = -ADAM_LR * (m_hat / (jnp.sqrt(v_hat) + ADAM_EPS) + ADAM_WD * w)
    return delta, m, v


def _sum_slots(ref):
    g = ref[0].astype(F32)
    for s in range(1, N_DEV):
        g = g + ref[s].astype(F32)
    return g


def _sum_adamw(name, parts, w, m, v, rows_target, ex=None, ex_parts=()):
    rows, cols = w.shape
    tr = _pick_tile(rows, rows_target, 8)
    n_ex = 0 if ex is None else ex.n
    n_steps = rows // tr

    def body(*refs):
        p_ref, w_ref, m_ref, v_ref = refs[:4]
        ex_ins = refs[4:4 + n_ex]
        o = 4 + n_ex
        g_ref, d_ref, nm_ref, nv_ref = refs[o:o + 4]
        ex_outs, sems = refs[o + 4:o + 4 + n_ex], refs[o + 4 + n_ex:]
        if ex is not None:
            @pl.when(pl.program_id(0) == 0)
            def _():
                ex.start(ex_ins, ex_outs, sems)

        g = _sum_slots(p_ref)
        d, nm, nv = _adamw(w_ref[...], g, m_ref[...], v_ref[...])
        g_ref[...] = g
        d_ref[...] = d
        nm_ref[...] = nm
        nv_ref[...] = nv
        if ex is not None:
            @pl.when(pl.program_id(0) == n_steps - 1)
            def _():
                ex.wait(ex_ins, ex_outs, sems)

    tile = pl.BlockSpec((tr, cols), lambda i: (i, 0))
    ex_specs, ex_shape, ex_scratch = ([], [], []) if ex is None else (ex.specs, ex.out_shape, ex.scratch)
    outs = pl.pallas_call(
        body, name=name,
        grid=(n_steps,),
        in_specs=[pl.BlockSpec((N_DEV, tr, cols), lambda i: (0, i, 0)), tile, tile, tile] + ex_specs,
        out_specs=[tile] * 4 + ex_specs,
        out_shape=[jax.ShapeDtypeStruct((rows, cols), F32)] * 4 + ex_shape,
        scratch_shapes=ex_scratch,
        compiler_params=_params(("arbitrary",), 40),
    )(parts, w, m, v, *ex_parts)
    return outs[:4], outs[4:]


def _sum_adamw_small(parts_list, w_list, m_list, v_list, loss_parts):
    n = len(w_list)

    def body(*refs):
        p_refs, w_refs, m_refs, v_refs = refs[:n], refs[n:2 * n], refs[2 * n:3 * n], refs[3 * n:4 * n]
        lp_ref = refs[4 * n]
        outs = refs[4 * n + 1:]
        for a in range(n):
            g = _sum_slots(p_refs[a])
            d, nm, nv = _adamw(w_refs[a][...], g, m_refs[a][...], v_refs[a][...])
            outs[4 * a][...] = g
            outs[4 * a + 1][...] = d
            outs[4 * a + 2][...] = nm
            outs[4 * a + 3][...] = nv
        outs[4 * n][...] = _sum_slots(lp_ref)

    out_shape = []
    for w in w_list:
        out_shape += [jax.ShapeDtypeStruct(w.shape, F32)] * 4
    out_shape.append(jax.ShapeDtypeStruct(loss_parts.shape[1:], F32))
    return pl.pallas_call(body, name="sum_adamw_small", out_shape=out_shape)(
        *parts_list, *w_list, *m_list, *v_list, loss_parts)


def kernel(x, meta_tokens, ln_g, w_in, ret_gn_g, conv_dw_w, conv_dw_b, conv_ln_g, conv_ln_b, conv_pw_w, conv_pw_b, w_out, final_g, loss_target, m_meta_tokens, m_ln_g, m_w_in, m_ret_gn_g, m_conv_dw_w, m_conv_dw_b, m_conv_ln_g, m_conv_ln_b, m_conv_pw_w, m_conv_pw_b, m_w_out, m_final_g, v_meta_tokens, v_ln_g, v_w_in, v_ret_gn_g, v_conv_dw_w, v_conv_dw_b, v_conv_ln_g, v_conv_ln_b, v_conv_pw_w, v_conv_pw_b, v_w_out, v_final_g):
    _, SEQ, D = x.shape
    MIX = w_out.shape[2]
    RW = ret_gn_g.shape[1]
    CW = conv_pw_b.shape[1]
    assert RW == CW and MIX == RW + CW and SEQ % META_TILE == 0 and CONV_K - 1 <= HALO
    R = SEQ + META_TILE
    hd = RW // RET_HEADS
    half = hd // 2
    me = 4 * lax.axis_index("x") + 2 * lax.axis_index("y") + lax.axis_index("c")

    dw_pad = jnp.pad(conv_dw_w[0], ((0, HALO - CONV_K), (0, 0)))

    n_seq_chunks = SEQ // CHUNK
    base = jnp.concatenate([jnp.arange(n_seq_chunks, dtype=F32) * CHUNK + N_META,
                            jnp.zeros((META_TILE // CHUNK - 1,), F32), jnp.full((1,), N_META - CHUNK, F32)])
    inv_freq = ROPE_BASE ** (-jnp.arange(half, dtype=F32) / half)
    ang_base = (base[:, None] * inv_freq[None, :])[:, None, :]
    ang_row = jnp.arange(CHUNK, dtype=F32)[:, None] * inv_freq[None, :]
    rope = (jnp.cos(ang_base), jnp.sin(ang_base), jnp.cos(ang_row), jnp.sin(ang_row))

    xs = x[0]
    target = loss_target[0]
    final_g2 = final_g[None, :]

    small_shards = [meta_tokens, dw_pad]
    hn, meta_tile, (_, dw_g) = _rms_norm(xs, meta_tokens, ln_g, _Exchange(small_shards, [None, None]), small_shards)
    dw_g = jnp.swapaxes(dw_g, 0, 1).reshape(HALO, CW)
    pw_shard, w_out_shard = [conv_pw_w[0].astype(BF16)], [w_out[0].astype(BF16)]
    proj, w_in_g, (pw_g,) = _in_proj_gather(hn, R, w_in[0].astype(BF16), _Exchange(pw_shard, [None]), pw_shard)
    pw_g = pw_g.reshape(CW, CW)
    y, states = _retention_fwd(proj, rope, ret_gn_g, MIX)
    y, conv_out, (w_out_g,) = _conv_fwd(proj, y, dw_g, conv_dw_b, conv_ln_g, conv_ln_b, pw_g, conv_pw_b,
                                        _Exchange(w_out_shard, [None]), w_out_shard)
    w_out_g = w_out_g.reshape(MIX, D)
    dh2, dy, dwo_p, dfg_p, loss_p = _out_proj_loss(xs, meta_tile, y, w_out_g, final_g2, target)

    dp_conv, dpw_p, dww_p, cvec_p = _conv_bwd(proj, conv_out, dy, dw_g, conv_ln_g, conv_ln_b, pw_g, conv_pw_b)
    dp_ret, dgn_p, (r_wo, r_pw) = _retention_bwd(proj, rope, ret_gn_g, states, dy,
                                                 _Exchange([dwo_p, dpw_p], [0, 0]), [dwo_p, dpw_p])
    dwi_p = _w_in_grad(hn, dp_ret, dp_conv)
    dh, dlg_p, (r_wi,) = _h_grad(dp_ret, dp_conv, w_in_g, xs, meta_tile, dh2, ln_g, _Exchange([dwi_p], [1]), [dwi_p])
    grad_x = dh[:SEQ][None]

    def at_row(r, a, b=None):
        v = a if b is None else jnp.concatenate([a, b], axis=1)
        return jnp.pad(v, ((r, 7 - r), (0, D - v.shape[1])))
    vec8 = (at_row(0, dlg_p) + at_row(1, dfg_p)
            + at_row(2, dgn_p, cvec_p[3:4])
            + at_row(3, cvec_p[1:2], cvec_p[2:3])
            + at_row(4, cvec_p[0:1], jnp.broadcast_to(loss_p, (1, CW))))
    small = jnp.concatenate([dh[R - N_META:], vec8,
                             jnp.zeros((SMALL_ROWS - N_META - 8, D), F32)], axis=0)

    (g_wi, d_wi, nm_wi, nv_wi), (r_dww, r_small) = _sum_adamw(
        "sum_adamw_w_in", r_wi, w_in[0], m_w_in[0], v_w_in[0], 256,
        ex=_Exchange([dww_p, small], [1, None]), ex_parts=[dww_p, small])
    (g_wo, d_wo, nm_wo, nv_wo), _ = _sum_adamw("sum_adamw_w_out", r_wo, w_out[0], m_w_out[0], v_w_out[0], 128)
    (g_pw, d_pw, nm_pw, nv_pw), _ = _sum_adamw("sum_adamw_pw", r_pw, conv_pw_w[0], m_conv_pw_w[0], v_conv_pw_w[0], 128)

    dcol = D // N_DEV
    sm = lambda r0, nr, c0, nc: lax.slice(r_small, (0, r0, c0), (N_DEV, r0 + nr, c0 + nc))
    meta_parts = lax.dynamic_slice(r_small, (0, 0, me * dcol), (N_DEV, N_META, dcol))
    small_parts = [meta_parts, sm(16, 1, 0, D), sm(18, 1, 0, RW), r_dww, sm(18, 1, RW, CW),
                   sm(19, 1, 0, CW), sm(19, 1, CW, CW), sm(20, 1, 0, CW), sm(17, 1, 0, D)]
    pad31 = lambda a: jnp.pad(a, ((0, HALO - CONV_K), (0, 0)))
    ws = [meta_tokens, ln_g, ret_gn_g, pad31(conv_dw_w[0]), conv_dw_b, conv_ln_g, conv_ln_b, conv_pw_b, final_g2]
    ms = [m_meta_tokens, m_ln_g, m_ret_gn_g, pad31(m_conv_dw_w[0]), m_conv_dw_b, m_conv_ln_g, m_conv_ln_b,
          m_conv_pw_b, m_final_g[None, :]]
    vs = [v_meta_tokens, v_ln_g, v_ret_gn_g, pad31(v_conv_dw_w[0]), v_conv_dw_b, v_conv_ln_g, v_conv_ln_b,
          v_conv_pw_b, v_final_g[None, :]]
    loss_parts = sm(20, 1, CW, 1)
    outs = _sum_adamw_small(small_parts, ws, ms, vs, loss_parts)
    loss = outs[-1][0, 0]
    quad = [outs[4 * a:4 * a + 4] for a in range(len(ws))]
    (q_meta, q_lng, q_gn, q_dww, q_dwb, q_clg, q_clb, q_pwb, q_fg) = quad
    q_dww = [t[:CONV_K][None] for t in q_dww]
    q_fg = [t[0] for t in q_fg]
    q_wi = [t[None] for t in (g_wi, d_wi, nm_wi, nv_wi)]
    q_wo = [t[None] for t in (g_wo, d_wo, nm_wo, nv_wo)]
    q_pw = [t[None] for t in (g_pw, d_pw, nm_pw, nv_pw)]

    per_w = [q_meta, q_lng, q_wi, q_gn, q_dww, q_dwb, q_clg, q_clb, q_pw, q_pwb, q_wo, q_fg]
    result = [loss, grad_x]
    for which in range(4):
        result += [q[which] for q in per_w]
    return tuple(result)
```

```python
import numpy as np
import jax
import jax.numpy as jnp
from jax import lax
from jax.experimental import pallas as pl
from jax.experimental.pallas import tpu as pltpu

N_META = 16
RET_HEADS = 4
CONV_K = 31
CHUNK = 128
ROPE_BASE = 10000.0
EPS = 1e-6
ADAM_LR = 0.001
ADAM_B1 = 0.9
ADAM_B2 = 0.999
ADAM_EPS = 1e-08
ADAM_WD = 0.01
ADAM_STEP = 10

N_DEV = 8
META_TILE = 256
HALO = 32
SMALL_ROWS = 32
VMEM_BYTES_V7X = 64 * 1024 * 1024
MXU_DIM = 256

F32 = jnp.float32
BF16 = jnp.bfloat16
MESH = pl.DeviceIdType.MESH

NN = (((1,), (0,)), ((), ()))
NT = (((1,), (1,)), ((), ()))
TN = (((0,), (0,)), ((), ()))


def _dot(a, b, dims=NN):
    return lax.dot_general(a, b, dims, preferred_element_type=F32)


def _pick_tile(n, target, mult=16):
    best = None
    for t in range(mult, min(n, target) + 1, mult):
        if n % t == 0:
            best = t
    assert best is not None, (n, target)
    return best


def _params(sem=None, vmem_mb=None):
    kw = {}
    if sem is not None:
        kw["dimension_semantics"] = sem
    if vmem_mb is not None:
        kw["vmem_limit_bytes"] = min(vmem_mb * 1024 * 1024, VMEM_BYTES_V7X - 4 * 1024 * 1024)
    return pltpu.CompilerParams(**kw)


def _sigmoid(x):
    return jax.nn.sigmoid(x)


def _dsilu(x, sg):
    return sg * (1.0 + x * (1.0 - sg))


def _decay_tables(heads):
    h = np.arange(heads, dtype=np.float32)
    gamma = (1.0 - np.exp2(-5.0 - h)).astype(np.float32)
    log_g = np.log(gamma).astype(np.float32)
    idx = np.arange(CHUNK, dtype=np.float32)
    rel = idx[:, None] - idx[None, :]
    mask = np.where(rel[None] >= 0, np.exp(np.maximum(rel, 0.0)[None] * log_g[:, None, None]), 0.0)
    qd = np.exp((idx[None, :] + 1.0) * log_g[:, None])
    kd = np.exp((CHUNK - 1.0 - idx[None, :]) * log_g[:, None])
    cd = np.exp(CHUNK * log_g)
    return (mask.astype(np.float32), qd.astype(np.float32)[:, :, None], kd.astype(np.float32)[:, :, None],
            [float(c) for c in cd.astype(np.float32)])


class _Exchange:
    def __init__(self, parts, block_axes):
        self.block_axes = list(block_axes)
        self.n = len(parts)
        self.out_shape = []
        for p, ax in zip(parts, block_axes):
            shp = list(p.shape)
            if ax is not None:
                assert shp[ax] % N_DEV == 0
                shp[ax] //= N_DEV
            self.out_shape.append(jax.ShapeDtypeStruct((N_DEV, *shp), p.dtype))
        self.scratch = [pltpu.SemaphoreType.DMA((self.n, N_DEV - 1)), pltpu.SemaphoreType.DMA((self.n, N_DEV - 1)),
                        pltpu.SemaphoreType.DMA((self.n,))]
        self.specs = [pl.BlockSpec(memory_space=pl.ANY)] * self.n

    def _copies(self, ins, outs, sems):
        send_sems, recv_sems, local_sems = sems
        x, y, c = lax.axis_index("x"), lax.axis_index("y"), lax.axis_index("c")
        me_idx = 4 * x + 2 * y + c

        def src_block(a, dev_idx):
            ax = self.block_axes[a]
            if ax is None:
                return ins[a]
            n = ins[a].shape[ax] // N_DEV
            idx = [slice(None)] * len(ins[a].shape)
            idx[ax] = pl.ds(pl.multiple_of(dev_idx * n, n), n)
            return ins[a].at[tuple(idx)]

        local = [pltpu.make_async_copy(src_block(a, me_idx), outs[a].at[me_idx], local_sems.at[a])
                 for a in range(self.n)]
        remote = []
        for m in range(1, N_DEV):
            px, py, pc = x ^ ((m >> 2) & 1), y ^ ((m >> 1) & 1), c ^ (m & 1)
            for a in range(self.n):
                remote.append(pltpu.make_async_remote_copy(
                    src_ref=src_block(a, 4 * px + 2 * py + pc), dst_ref=outs[a].at[me_idx],
                    send_sem=send_sems.at[a, m - 1], recv_sem=recv_sems.at[a, m - 1],
                    device_id=(px, py, pc), device_id_type=MESH))
        return local, remote

    def start(self, ins, outs, sems):
        local, remote = self._copies(ins, outs, sems)
        for cp in local + remote:
            cp.start()

    def wait(self, ins, outs, sems):
        local, remote = self._copies(ins, outs, sems)
        for cp in remote:
            cp.wait_recv()
        for cp in remote:
            cp.wait_send()
        for cp in local:
            cp.wait()


def _rms_norm(xs, meta_shard, ln_g, ex, ex_parts):
    SEQ, D = xs.shape
    n_meta, dcol = meta_shard.shape
    tm = 2 * META_TILE
    n_seq = SEQ // tm

    def body(*refs):
        x_ref, g_ref = refs[:2]
        ex_ins = refs[2:2 + ex.n]
        hn_ref, mt_ref = refs[2 + ex.n:4 + ex.n]
        ex_outs = refs[4 + ex.n:4 + 2 * ex.n]
        slots, slot_sem = refs[4 + 2 * ex.n:6 + 2 * ex.n]
        sems = refs[6 + 2 * ex.n:]
        i = pl.program_id(0)

        @pl.when(i == 0)
        def _():
            ex.start(ex_ins, ex_outs, sems)

        def norm(hv):
            r = lax.rsqrt(jnp.mean(hv * hv, axis=-1, keepdims=True) + EPS)
            return (hv * r * g_ref[...]).astype(BF16)

        @pl.when(i < n_seq)
        def _():
            hn_ref[...] = norm(x_ref[...])

        @pl.when(i == n_seq)
        def _():
            ex.wait(ex_ins, ex_outs, sems)
            cp = pltpu.make_async_copy(ex_outs[0], slots, slot_sem)
            cp.start()
            mt_ref[...] = jnp.zeros_like(mt_ref)
            cp.wait()
            for s in range(N_DEV):
                mt_ref[META_TILE - n_meta:META_TILE, s * dcol:(s + 1) * dcol] = slots[s]
            hn_ref[0:META_TILE, :] = norm(mt_ref[...])

    outs = pl.pallas_call(
        body, name="rms_norm",
        grid=(n_seq + 1,),
        in_specs=[pl.BlockSpec((tm, D), lambda i: (jnp.minimum(i, n_seq - 1), 0)),
                  pl.BlockSpec((1, D), lambda i: (0, 0))] + ex.specs,
        out_specs=[pl.BlockSpec((tm, D), lambda i: (i, 0)),
                   pl.BlockSpec((META_TILE, D), lambda i: (0, 0))] + ex.specs,
        out_shape=[jax.ShapeDtypeStruct((SEQ + tm, D), BF16), jax.ShapeDtypeStruct((META_TILE, D), F32)]
                  + ex.out_shape,
        scratch_shapes=[pltpu.VMEM((N_DEV, n_meta, dcol), F32), pltpu.SemaphoreType.DMA] + ex.scratch,
        compiler_params=_params(("arbitrary",), 32),
    )(xs, ln_g, *ex_parts)
    return outs[0], outs[1], outs[2:]


def _chip_visited(q):
    mine = 2 * lax.axis_index("x") + lax.axis_index("y")
    return mine ^ (((q & 1) << 1) | (q >> 1))


def _in_proj_gather(hn, n_rows, w_shard, ex, ex_parts):
    R, D = n_rows, hn.shape[1]
    wb = w_shard.shape[1]
    E, tn = wb * N_DEV, 2 * wb
    n_q = N_DEV // 2
    tm = _pick_tile(R, min(768, R // 2), MXU_DIM)
    n_i = R // tm

    def body(*refs):
        hn_ref, wsh_hbm = refs[:2]
        ex_ins = refs[2:2 + ex.n]
        proj_ref, wg_hbm = refs[2 + ex.n:4 + ex.n]
        ex_outs = refs[4 + ex.n:4 + 2 * ex.n]
        w_vmem, send_sems, recv_sems, local_sem, vmem_sems = refs[4 + 2 * ex.n:9 + 2 * ex.n]
        ex_sems = refs[9 + 2 * ex.n:]
        q, i = pl.program_id(0), pl.program_id(1)
        x, y, c = lax.axis_index("x"), lax.axis_index("y"), lax.axis_index("c")
        me, sibling = (x, y, c), (x, y, 1 - c)
        chips = [(1 - x, y), (x, 1 - y), (1 - x, 1 - y)]

        def block(dev):
            return wg_hbm.at[:, pl.ds(pl.multiple_of((4 * dev[0] + 2 * dev[1] + dev[2]) * wb, wb), wb)]

        def copy(k, dev, to, src=None):
            return pltpu.make_async_remote_copy(
                src_ref=block(dev) if src is None else src, dst_ref=block(dev),
                send_sem=send_sems.at[k], recv_sem=recv_sems.at[k], device_id=to, device_id_type=MESH)

        def to_vmem(p):
            cols = pl.ds(pl.multiple_of(_chip_visited(p) * tn, tn), tn)
            return pltpu.make_async_copy(wg_hbm.at[:, cols], w_vmem.at[p % 2], vmem_sems.at[p % 2])

        mine = pltpu.make_async_copy(wsh_hbm, block(me), local_sem)
        first = [copy(0, me, sibling, src=wsh_hbm)] + [copy(1 + j, me, (*chip, c), src=wsh_hbm)
                                                       for j, chip in enumerate(chips)]
        passed = [copy(4 + j, (*chip, c), sibling) for j, chip in enumerate(chips)]

        @pl.when((q == 0) & (i == 0))
        def _():
            mine.start()
            for cp in first:
                cp.start()
            ex.start(ex_ins, ex_outs, ex_sems)
            mine.wait()
            copy(0, sibling, me).wait_recv()
            to_vmem(0).start()
            to_vmem(0).wait()

        for p in range(1, n_q):
            chip = chips[p - 1]

            @pl.when((q == p - 1) & (i == n_i - 2))
            def _():
                copy(p, (*chip, c), me).wait_recv()
                passed[p - 1].start()

            @pl.when((q == p - 1) & (i == n_i - 1))
            def _():
                copy(3 + p, (*chip, 1 - c), me).wait_recv()
                to_vmem(p).start()

            @pl.when((q == p) & (i == 0))
            def _():
                to_vmem(p).wait()

        proj_ref[...] = _dot(hn_ref[...], w_vmem[q % 2]).astype(BF16)

        @pl.when((q == n_q - 1) & (i == n_i - 1))
        def _():
            for cp in first + passed:
                cp.wait_send()
            ex.wait(ex_ins, ex_outs, ex_sems)

    hbm = pl.BlockSpec(memory_space=pl.ANY)
    outs = pl.pallas_call(
        body, name="in_proj",
        grid=(n_q, n_i),
        in_specs=[pl.BlockSpec((tm, D), lambda q, i: (i, 0)), hbm] + ex.specs,
        out_specs=[pl.BlockSpec((tm, tn), lambda q, i: (i, _chip_visited(q))), hbm] + ex.specs,
        out_shape=[jax.ShapeDtypeStruct((R, E), BF16), jax.ShapeDtypeStruct((D, E), BF16)] + ex.out_shape,
        scratch_shapes=[pltpu.VMEM((2, D, tn), BF16), pltpu.SemaphoreType.DMA((7,)), pltpu.SemaphoreType.DMA((7,)),
                        pltpu.SemaphoreType.DMA, pltpu.SemaphoreType.DMA((2,))] + ex.scratch,
        compiler_params=_params(("arbitrary", "arbitrary"), 48),
    )(hn, w_shard, *ex_parts)
    return outs[0], outs[1], outs[2:]


def _rope_chunk(cb_ref, sb_ref, ci_ref, si_ref):
    cb, sb, ci, si = cb_ref[0], sb_ref[0], ci_ref[...], si_ref[...]
    return cb * ci - sb * si, sb * ci + cb * si


def _rot(t, cos, sin, half):
    t1, t2 = t[:, :half], t[:, half:]
    return jnp.concatenate([t1 * cos - t2 * sin, t1 * sin + t2 * cos], axis=-1)


def _rot_inv(t, cos, sin, half):
    t1, t2 = t[:, :half], t[:, half:]
    return jnp.concatenate([t1 * cos + t2 * sin, t2 * cos - t1 * sin], axis=-1)


def _chunk_order(n_chunks):
    lead = META_TILE // CHUNK
    return lambda l: (l + n_chunks - lead) % n_chunks


def _retention_fwd(proj, rope, gn_g, mix):
    R, E = proj.shape
    RW = gn_g.shape[1]
    H = RET_HEADS
    hd = RW // H
    half = hd // 2
    NC = R // CHUNK
    mask, qd, kd, cd = _decay_tables(H)
    scale = float(hd) ** -0.5
    phys = _chunk_order(NC)

    def body(p_ref, cb_ref, sb_ref, ci_ref, si_ref, mask_ref, qd_ref, kd_ref, gn_ref, y_ref, st_ref, state):
        @pl.when(pl.program_id(0) == 0)
        def _():
            state[...] = jnp.zeros_like(state)

        cs, sn = _rope_chunk(cb_ref, sb_ref, ci_ref, si_ref)
        hs = range(H)
        col = lambda j, h: slice(j * RW + h * hd, j * RW + (h + 1) * hd)
        qr = [_rot(p_ref[:, col(0, h)].astype(F32), cs, sn, half) for h in hs]
        kr = [_rot(p_ref[:, col(1, h)].astype(F32), cs, sn, half) * scale for h in hs]
        v = [p_ref[:, col(2, h)] for h in hs]
        s_prev = [state[h] for h in hs]
        s_prev_b = [s_prev[h].astype(BF16) for h in hs]
        s = [(_dot(qr[h].astype(BF16), kr[h].astype(BF16), NT) * mask_ref[h]).astype(BF16) for h in hs]
        y_raw = [_dot(s[h], v[h]) + _dot((qr[h] * qd_ref[h]).astype(BF16), s_prev_b[h]) for h in hs]
        s_new = [s_prev[h] * cd[h] + _dot((kr[h] * kd_ref[h]).astype(BF16), v[h], TN) for h in hs]
        for h in hs:
            st_ref[0, h] = s_prev_b[h]
            state[h] = s_new[h]
        for h in hs:
            g = p_ref[:, col(3, h)].astype(F32)
            mu = jnp.mean(y_raw[h], axis=-1, keepdims=True)
            yc = y_raw[h] - mu
            var = jnp.mean(yc * yc, axis=-1, keepdims=True)
            out = yc * lax.rsqrt(var + EPS) * gn_ref[:, col(0, h)] * (g * _sigmoid(g))
            y_ref[:, col(0, h)] = out.astype(BF16)

    const3 = lambda l: (0, 0, 0)
    return pl.pallas_call(
        body, name="retention_fwd",
        grid=(NC,),
        in_specs=[pl.BlockSpec((CHUNK, 4 * RW), lambda l: (phys(l), 0)),
                  pl.BlockSpec((1, 1, half), lambda l: (phys(l), 0, 0)),
                  pl.BlockSpec((1, 1, half), lambda l: (phys(l), 0, 0)),
                  pl.BlockSpec((CHUNK, half), lambda l: (0, 0)),
                  pl.BlockSpec((CHUNK, half), lambda l: (0, 0)),
                  pl.BlockSpec((H, CHUNK, CHUNK), const3),
                  pl.BlockSpec((H, CHUNK, 1), const3),
                  pl.BlockSpec((H, CHUNK, 1), const3),
                  pl.BlockSpec((1, RW), lambda l: (0, 0))],
        out_specs=[pl.BlockSpec((CHUNK, RW), lambda l: (phys(l), 0)),
                   pl.BlockSpec((1, H, hd, hd), lambda l: (phys(l), 0, 0, 0))],
        out_shape=[jax.ShapeDtypeStruct((R, mix), BF16), jax.ShapeDtypeStruct((NC, H, hd, hd), BF16)],
        scratch_shapes=[pltpu.VMEM((H, hd, hd), F32)],
        compiler_params=_params(("arbitrary",), 32),
    )(proj, *rope, jnp.asarray(mask), jnp.asarray(qd), jnp.asarray(kd), gn_g)


CONV_ROWS = 64
CONV_LANES = 128
LANE = 128
ELEM_ROWS = 32


def _conv_order(n_tiles):
    return lambda l: (l + n_tiles - 1) % n_tiles


def _halo_block(n_tiles, tm):
    per = tm // HALO
    return lambda l: ((l + n_tiles - 2) % n_tiles) * per + per - 1


def _fill_shifted(src, dst):
    rows, width = dst.shape[1], dst.shape[2]
    step = _pick_tile(rows, 64, 8)
    for r in range(1, 8):
        for r0 in range(0, rows, step):
            for l0 in range(0, width, CONV_LANES):
                dst[r - 1, r0:r0 + step, l0:l0 + CONV_LANES] = src[r + r0:r + r0 + step, l0:l0 + CONV_LANES]


def _at_offset(src, shifted, off, r0, rows, lanes):
    r = off % 8
    a = off - r + r0
    if r == 0:
        return src[a:a + rows, lanes]
    return shifted[r - 1, a:a + rows, lanes]


def _fill_glu(first, a_ref, b_ref, ah_ref, bh_ref, u_ext, tm, gate_keep=None):
    uh = ah_ref[...].astype(F32) * _sigmoid(bh_ref[...].astype(F32))
    u_ext[0:HALO, :] = jnp.where(first, 0.0, uh)
    for r0 in range(0, tm, ELEM_ROWS):
        rows = slice(r0, r0 + ELEM_ROWS)
        gate = _sigmoid(b_ref[rows, :].astype(F32))
        u_ext[HALO + r0:HALO + r0 + ELEM_ROWS, :] = a_ref[rows, :].astype(F32) * gate
        if gate_keep is not None:
            gate_keep[rows, :] = gate


def _layer_norm(cv, lg_ref, lb_ref):
    mu = jnp.mean(cv, axis=-1, keepdims=True)
    cc = cv - mu
    rstd = lax.rsqrt(jnp.mean(cc * cc, axis=-1, keepdims=True) + EPS)
    xh = cc * rstd
    return xh, rstd, xh * lg_ref[...] + lb_ref[...]


def _conv_fwd(proj, y_in, dw_w, dw_b, ln_g, ln_b, pw_w, pw_b, ex, ex_parts):
    R, E = proj.shape
    CW = pw_w.shape[0]
    tm = META_TILE
    NTL = R // tm
    phys = _conv_order(NTL)
    halo = _halo_block(NTL, tm)
    cb = (E - 3 * CW) // CW
    base = HALO - (CONV_K - 1)

    def body(*refs):
        a_ref, b_ref, g_ref, ah_ref, bh_ref, w_ref, wb_ref, lg_ref, lb_ref, pw_ref, pb_ref, yin_ref = refs[:12]
        ex_ins = refs[12:12 + ex.n]
        y_ref, c_ref = refs[12 + ex.n:14 + ex.n]
        ex_outs = refs[14 + ex.n:14 + 2 * ex.n]
        u_ext, u_sh, s_scr, upw_scr = refs[14 + 2 * ex.n:18 + 2 * ex.n]
        sems = refs[18 + 2 * ex.n:]

        @pl.when(pl.program_id(0) == 0)
        def _():
            ex.start(ex_ins, ex_outs, sems)

        _fill_glu(pl.program_id(0) == 0, a_ref, b_ref, ah_ref, bh_ref, u_ext, tm)
        _fill_shifted(u_ext, u_sh)
        for r0 in range(0, tm, CONV_ROWS):
            for l0 in range(0, CW, CONV_LANES):
                lanes = slice(l0, l0 + CONV_LANES)
                acc = None
                for k in range(CONV_K):
                    term = _at_offset(u_ext, u_sh, base + k, r0, CONV_ROWS, lanes) * w_ref[k:k + 1, lanes]
                    acc = term if acc is None else acc + term
                c_ref[r0:r0 + CONV_ROWS, lanes] = acc + wb_ref[:, lanes]
        blocks = [slice(r0, r0 + ELEM_ROWS) for r0 in range(0, tm, ELEM_ROWS)]
        for rows in blocks:
            _, _, ln = _layer_norm(c_ref[rows, :], lg_ref, lb_ref)
            s_scr[rows, :] = (ln * _sigmoid(ln)).astype(BF16)
        upw_scr[...] = _dot(s_scr[...], pw_ref[...]) + pb_ref[...]
        for rows in blocks:
            g = g_ref[rows, :].astype(F32)
            y_ref[rows, :] = (upw_scr[rows, :] * (g * _sigmoid(g))).astype(BF16)

        @pl.when(pl.program_id(0) == NTL - 1)
        def _():
            ex.wait(ex_ins, ex_outs, sems)

    row = lambda l: (0, 0)
    outs = pl.pallas_call(
        body, name="conv_fwd",
        grid=(NTL,),
        in_specs=[pl.BlockSpec((tm, CW), lambda l: (phys(l), cb)),
                  pl.BlockSpec((tm, CW), lambda l: (phys(l), cb + 1)),
                  pl.BlockSpec((tm, CW), lambda l: (phys(l), cb + 2)),
                  pl.BlockSpec((HALO, CW), lambda l: (halo(l), cb)),
                  pl.BlockSpec((HALO, CW), lambda l: (halo(l), cb + 1)),
                  pl.BlockSpec((HALO, CW), row),
                  pl.BlockSpec((1, CW), row), pl.BlockSpec((1, CW), row), pl.BlockSpec((1, CW), row),
                  pl.BlockSpec((CW, CW), row),
                  pl.BlockSpec((1, CW), row),
                  pl.BlockSpec(memory_space=pl.ANY)] + ex.specs,
        out_specs=[pl.BlockSpec((tm, CW), lambda l: (phys(l), 1)),
                   pl.BlockSpec((tm, CW), lambda l: (phys(l), 0))] + ex.specs,
        out_shape=[jax.ShapeDtypeStruct(y_in.shape, BF16), jax.ShapeDtypeStruct((R, CW), F32)] + ex.out_shape,
        input_output_aliases={11: 0},
        scratch_shapes=[pltpu.VMEM((HALO + tm, CW), F32), pltpu.VMEM((7, tm + HALO - 8, CW), F32),
                        pltpu.VMEM((tm, CW), BF16), pltpu.VMEM((tm, CW), F32)] + ex.scratch,
        compiler_params=_params(("arbitrary",), 48),
    )(proj, proj, proj, proj, proj, dw_w, dw_b, ln_g, ln_b, pw_w, pw_b, y_in, *ex_parts)
    return outs[0], outs[1], outs[2:]


def _out_proj_loss(xs, meta_tile, y, w_out, final_g, target):
    SEQ, D = xs.shape
    R, MIX = y.shape
    tm = META_TILE
    n_seq = SEQ // tm
    n_tiles = R // tm
    rows_out = _pick_tile(MIX, 256)

    def body(x_ref, mt_ref, y_ref, w_hbm, fg_ref, t_ref, dh2_ref, dy_ref, dwo_hbm, dfg_ref, loss_ref,
             w_scr, acc, stage, sem, put_sems):
        i = pl.program_id(0)

        @pl.when(i == 0)
        def _():
            cp = pltpu.make_async_copy(w_hbm, w_scr, sem)
            cp.start()
            acc[...] = jnp.zeros_like(acc)
            dfg_ref[...] = jnp.zeros_like(dfg_ref)
            loss_ref[...] = jnp.zeros_like(loss_ref)
            cp.wait()

        yb = y_ref[...]
        h2 = jnp.where(i < n_seq, x_ref[...], mt_ref[...]) + _dot(yb, w_scr[...])
        r2 = lax.rsqrt(jnp.mean(h2 * h2, axis=-1, keepdims=True) + EPS)
        n = h2 * r2
        fg = fg_ref[...]
        err = jnp.where(i < n_seq, n * fg - t_ref[...], 0.0)
        loss_ref[...] += 0.5 * jnp.sum(jnp.mean(err * err, axis=-1, keepdims=True), axis=0, keepdims=True)
        dout = err * (1.0 / D)
        dfg_ref[...] += jnp.sum(dout * n, axis=0, keepdims=True)
        dn = dout * fg
        dh2 = r2 * (dn - n * jnp.mean(dn * n, axis=-1, keepdims=True))
        dh2_ref[...] = dh2
        dh2b = dh2.astype(BF16)
        dy_ref[...] = _dot(dh2b, w_scr[...], NT).astype(BF16)
        acc[...] += _dot(yb, dh2b, TN)

        @pl.when(i == n_tiles - 1)
        def _():
            def put(n):
                rows = slice(n * rows_out, (n + 1) * rows_out)
                return pltpu.make_async_copy(stage.at[n % 2], dwo_hbm.at[rows, :], put_sems.at[n % 2])

            n_put = MIX // rows_out
            for n in range(n_put):
                if n >= 2:
                    put(n - 2).wait()
                stage[n % 2] = acc[n * rows_out:(n + 1) * rows_out, :].astype(BF16)
                put(n).start()
            for n in range(max(n_put - 2, 0), n_put):
                put(n).wait()

    row = lambda i: (0, 0)
    return pl.pallas_call(
        body, name="out_proj_loss",
        grid=(n_tiles,),
        in_specs=[pl.BlockSpec((tm, D), lambda i: (jnp.minimum(i, n_seq - 1), 0)),
                  pl.BlockSpec((tm, D), row),
                  pl.BlockSpec((tm, MIX), lambda i: (i, 0)),
                  pl.BlockSpec(memory_space=pl.ANY),
                  pl.BlockSpec((1, D), row),
                  pl.BlockSpec((tm, D), lambda i: (jnp.minimum(i, n_seq - 1), 0))],
        out_specs=[pl.BlockSpec((tm, D), lambda i: (i, 0)),
                   pl.BlockSpec((tm, MIX), lambda i: (i, 0)),
                   pl.BlockSpec(memory_space=pl.ANY),
                   pl.BlockSpec((1, D), row),
                   pl.BlockSpec((1, 1), row)],
        out_shape=[jax.ShapeDtypeStruct((R, D), F32), jax.ShapeDtypeStruct((R, MIX), BF16),
                   jax.ShapeDtypeStruct((MIX, D), BF16), jax.ShapeDtypeStruct((1, D), F32),
                   jax.ShapeDtypeStruct((1, 1), F32)],
        scratch_shapes=[pltpu.VMEM((MIX, D), BF16), pltpu.VMEM((MIX, D), F32), pltpu.VMEM((2, rows_out, D), BF16),
                        pltpu.SemaphoreType.DMA, pltpu.SemaphoreType.DMA((2,))],
        compiler_params=_params(("arbitrary",), 60),
    )(xs, meta_tile, y, w_out, final_g, target)


def _conv_bwd(proj, conv_out, dy, dw_w, ln_g, ln_b, pw_w, pw_b):
    R, E = proj.shape
    CW = pw_w.shape[0]
    tm = META_TILE
    NTL = R // tm
    order = _conv_order(NTL)
    phys = lambda i: order(NTL - 1 - i)
    halo_l = _halo_block(NTL, tm)
    halo = lambda i: halo_l(NTL - 1 - i)
    cb = (E - 3 * CW) // CW
    base = HALO - (CONV_K - 1)

    def body(a_ref, b_ref, g_ref, ah_ref, bh_ref, c_ref, dy_ref, w_ref, lg_ref, lb_ref, pw_ref, pb_ref,
             dp_ref, dpw_ref, dww_ref, vec_ref, u_ext, u_sh, dc_ext, dc_sh, du_scr, dww_acc, dpw_acc,
             xh_scr, rstd_scr, ln_scr, sg_scr, upw_scr, s_scr, dupw_scr, gate_scr):
        i = pl.program_id(0)

        @pl.when(i == 0)
        def _():
            dpw_acc[...] = jnp.zeros_like(dpw_acc)
            dww_ref[...] = jnp.zeros_like(dww_ref)
            vec_ref[...] = jnp.zeros_like(vec_ref)
            dww_acc[...] = jnp.zeros_like(dww_acc)
            dc_ext[tm:tm + HALO, :] = jnp.zeros((HALO, CW), F32)

        _fill_glu(i == NTL - 1, a_ref, b_ref, ah_ref, bh_ref, u_ext, tm, gate_scr)
        _fill_shifted(u_ext, u_sh)
        blocks = [slice(r0, r0 + ELEM_ROWS) for r0 in range(0, tm, ELEM_ROWS)]
        for rows in blocks:
            xh, rstd, ln = _layer_norm(c_ref[rows, :], lg_ref, lb_ref)
            sg = _sigmoid(ln)
            xh_scr[rows, :], rstd_scr[rows, :], ln_scr[rows, :], sg_scr[rows, :] = xh, rstd, ln, sg
            s_scr[rows, :] = (ln * sg).astype(BF16)
        upw_scr[...] = _dot(s_scr[...], pw_ref[...]) + pb_ref[...]
        col_sum = jnp.zeros((1, CW), F32)
        for rows in blocks:
            g = g_ref[rows, :].astype(F32)
            sgg = _sigmoid(g)
            dyc = dy_ref[rows, :].astype(F32)
            dp_ref[rows, 2 * CW:3 * CW] = (dyc * upw_scr[rows, :] * _dsilu(g, sgg)).astype(BF16)
            dupw = dyc * (g * sgg)
            dupw_scr[rows, :] = dupw.astype(BF16)
            col_sum = col_sum + jnp.sum(dupw, axis=0, keepdims=True)
        vec_ref[0:1, :] += col_sum
        dpw_acc[...] += _dot(s_scr[...], dupw_scr[...], TN)
        upw_scr[...] = _dot(dupw_scr[...], pw_ref[...], NT)
        sum_g, sum_b, sum_c = col_sum * 0.0, col_sum * 0.0, col_sum * 0.0
        for rows in blocks:
            xh, rstd = xh_scr[rows, :], rstd_scr[rows, :]
            dln = upw_scr[rows, :] * _dsilu(ln_scr[rows, :], sg_scr[rows, :])
            sum_g = sum_g + jnp.sum(dln * xh, axis=0, keepdims=True)
            sum_b = sum_b + jnp.sum(dln, axis=0, keepdims=True)
            dxh = dln * lg_ref[...]
            dc = rstd * (dxh - jnp.mean(dxh, axis=-1, keepdims=True)
                         - xh * jnp.mean(dxh * xh, axis=-1, keepdims=True))
            sum_c = sum_c + jnp.sum(dc, axis=0, keepdims=True)
            dc_ext[rows, :] = dc
        vec_ref[1:2, :] += sum_g
        vec_ref[2:3, :] += sum_b
        vec_ref[3:4, :] += sum_c
        _fill_shifted(dc_ext, dc_sh)

        for l0 in range(0, CW, CONV_LANES):
            lanes = slice(l0, l0 + CONV_LANES)
            for r0 in range(0, tm, CONV_ROWS):
                acc = None
                for k in range(CONV_K):
                    term = _at_offset(dc_ext, dc_sh, CONV_K - 1 - k, r0, CONV_ROWS, lanes) * w_ref[k:k + 1, lanes]
                    acc = term if acc is None else acc + term
                du_scr[r0:r0 + CONV_ROWS, lanes] = acc

        n_grp = tm // 8
        by_shift = [[(k, (base + k) // 8) for k in range(CONV_K) if (base + k) % 8 == r] for r in range(8)]
        for l0 in range(0, CW, LANE):
            lane = slice(l0, l0 + LANE)
            for r in range(8):
                src = u_ext if r == 0 else u_sh.at[r - 1]
                a_lo, a_hi = by_shift[r][0][1], by_shift[r][-1][1]
                sums = {k: None for k, _ in by_shift[r]}
                dcg = {}
                for gi in range(a_lo, n_grp + a_hi):
                    if gi - a_lo < n_grp:
                        dcg[gi - a_lo] = dc_ext[8 * (gi - a_lo):8 * (gi - a_lo) + 8, lane]
                    dcg.pop(gi - a_hi - 1, None)
                    ug = src[8 * gi:8 * gi + 8, lane]
                    for k, a in by_shift[r]:
                        if 0 <= gi - a < n_grp:
                            prod = dcg[gi - a] * ug
                            sums[k] = prod if sums[k] is None else sums[k] + prod
                for k, _ in by_shift[r]:
                    dww_acc[k, :, lane] += sums[k]

        for rows in blocks:
            du = du_scr[rows, :]
            sgb = gate_scr[rows, :]
            dp_ref[rows, 0:CW] = (du * sgb).astype(BF16)
            dp_ref[rows, CW:2 * CW] = (du * a_ref[rows, :].astype(F32) * sgb * (1.0 - sgb)).astype(BF16)
        dc_ext[tm:tm + HALO, :] = dc_ext[0:HALO, :]

        @pl.when(i == NTL - 1)
        def _():
            for k in range(CONV_K):
                dww_ref[k:k + 1, :] = jnp.sum(dww_acc[k], axis=0, keepdims=True)
            dpw_ref[...] = dpw_acc[...].astype(BF16)

    row = lambda i: (0, 0)
    return pl.pallas_call(
        body, name="conv_bwd",
        grid=(NTL,),
        in_specs=[pl.BlockSpec((tm, CW), lambda i: (phys(i), cb)),
                  pl.BlockSpec((tm, CW), lambda i: (phys(i), cb + 1)),
                  pl.BlockSpec((tm, CW), lambda i: (phys(i), cb + 2)),
                  pl.BlockSpec((HALO, CW), lambda i: (halo(i), cb)),
                  pl.BlockSpec((HALO, CW), lambda i: (halo(i), cb + 1)),
                  pl.BlockSpec((tm, CW), lambda i: (phys(i), 0)),
                  pl.BlockSpec((tm, CW), lambda i: (phys(i), 1)),
                  pl.BlockSpec((HALO, CW), row),
                  pl.BlockSpec((1, CW), row), pl.BlockSpec((1, CW), row),
                  pl.BlockSpec((CW, CW), row),
                  pl.BlockSpec((1, CW), row)],
        out_specs=[pl.BlockSpec((tm, 3 * CW), lambda i: (phys(i), 0)),
                   pl.BlockSpec((CW, CW), row),
                   pl.BlockSpec((HALO, CW), row),
                   pl.BlockSpec((8, CW), row)],
        out_shape=[jax.ShapeDtypeStruct((R, 3 * CW), BF16), jax.ShapeDtypeStruct((CW, CW), BF16),
                   jax.ShapeDtypeStruct((HALO, CW), F32), jax.ShapeDtypeStruct((8, CW), F32)],
        scratch_shapes=[pltpu.VMEM((HALO + tm, CW), F32), pltpu.VMEM((7, tm + HALO - 8, CW), F32),
                        pltpu.VMEM((tm + HALO, CW), F32), pltpu.VMEM((7, tm + HALO - 8, CW), F32),
                        pltpu.VMEM((tm, CW), F32), pltpu.VMEM((CONV_K, 8, CW), F32), pltpu.VMEM((CW, CW), F32),
                        pltpu.VMEM((tm, CW), F32), pltpu.VMEM((tm, 1), F32), pltpu.VMEM((tm, CW), F32),
                        pltpu.VMEM((tm, CW), F32), pltpu.VMEM((tm, CW), F32), pltpu.VMEM((tm, CW), BF16),
                        pltpu.VMEM((tm, CW), BF16), pltpu.VMEM((tm, CW), F32)],
        compiler_params=_params(("arbitrary",), 60),
    )(proj, proj, proj, proj, proj, conv_out, dy, dw_w, ln_g, ln_b, pw_w, pw_b)


def _retention_bwd(proj, rope, gn_g, states, dy, ex, ex_parts):
    R, E = proj.shape
    RW = gn_g.shape[1]
    H = RET_HEADS
    hd = RW // H
    half = hd // 2
    NC = R // CHUNK
    mask, qd, kd, cd = _decay_tables(H)
    scale = float(hd) ** -0.5
    order = _chunk_order(NC)
    phys = lambda i: order(NC - 1 - i)

    def body(*refs):
        p_ref, cb_ref, sb_ref, ci_ref, si_ref, mask_ref, qd_ref, kd_ref, gn_ref, st_ref, dy_ref = refs[:11]
        ex_ins = refs[11:11 + ex.n]
        dp_ref, dgn_ref = refs[11 + ex.n:13 + ex.n]
        ex_outs = refs[13 + ex.n:13 + 2 * ex.n]
        dstate = refs[13 + 2 * ex.n]
        sems = refs[14 + 2 * ex.n:]

        @pl.when(pl.program_id(0) == 0)
        def _():
            ex.start(ex_ins, ex_outs, sems)
            dstate[...] = jnp.zeros_like(dstate)
            dgn_ref[...] = jnp.zeros_like(dgn_ref)

        cs, sn = _rope_chunk(cb_ref, sb_ref, ci_ref, si_ref)
        hs = range(H)
        col = lambda j, h: slice(j * RW + h * hd, j * RW + (h + 1) * hd)
        qr = [_rot(p_ref[:, col(0, h)].astype(F32), cs, sn, half) for h in hs]
        kr = [_rot(p_ref[:, col(1, h)].astype(F32), cs, sn, half) * scale for h in hs]
        v = [p_ref[:, col(2, h)] for h in hs]
        qb = [qr[h].astype(BF16) for h in hs]
        kb = [kr[h].astype(BF16) for h in hs]
        qdb = [(qr[h] * qd_ref[h]).astype(BF16) for h in hs]
        kdb = [(kr[h] * kd_ref[h]).astype(BF16) for h in hs]
        s_prev = [st_ref[0, h] for h in hs]
        dst = [dstate[h] for h in hs]
        dstb = [dst[h].astype(BF16) for h in hs]
        sb = [(_dot(qb[h], kb[h], NT) * mask_ref[h]).astype(BF16) for h in hs]
        y_raw = [_dot(sb[h], v[h]) + _dot(qdb[h], s_prev[h]) for h in hs]
        dyrb, dg = [], []
        for h in hs:
            g = p_ref[:, col(3, h)].astype(F32)
            mu = jnp.mean(y_raw[h], axis=-1, keepdims=True)
            yc = y_raw[h] - mu
            rstd = lax.rsqrt(jnp.mean(yc * yc, axis=-1, keepdims=True) + EPS)
            xh = yc * rstd
            gn = gn_ref[:, col(0, h)]
            sg = _sigmoid(g)
            dyh = dy_ref[:, col(0, h)].astype(F32)
            dg.append((dyh * (xh * gn) * _dsilu(g, sg)).astype(BF16))
            dyn = dyh * (g * sg)
            dgn_ref[:, col(0, h)] += jnp.sum(dyn * xh, axis=0, keepdims=True)
            dxh = dyn * gn
            dyr = rstd * (dxh - jnp.mean(dxh, axis=-1, keepdims=True)
                          - xh * jnp.mean(dxh * xh, axis=-1, keepdims=True))
            dyrb.append(dyr.astype(BF16))
        dsb = [(_dot(dyrb[h], v[h], NT) * mask_ref[h]).astype(BF16) for h in hs]
        dqr = [_dot(dsb[h], kb[h]) + _dot(dyrb[h], s_prev[h], NT) * qd_ref[h] for h in hs]
        dkr = [_dot(dsb[h], qb[h], TN) + _dot(v[h], dstb[h], NT) * kd_ref[h] for h in hs]
        dv = [_dot(sb[h], dyrb[h], TN) + _dot(kdb[h], dstb[h]) for h in hs]
        dst_new = [dst[h] * cd[h] + _dot(qdb[h], dyrb[h], TN) for h in hs]
        for h in hs:
            dstate[h] = dst_new[h]
            dp_ref[:, col(0, h)] = _rot_inv(dqr[h], cs, sn, half).astype(BF16)
            dp_ref[:, col(1, h)] = (_rot_inv(dkr[h], cs, sn, half) * scale).astype(BF16)
            dp_ref[:, col(2, h)] = dv[h].astype(BF16)
            dp_ref[:, col(3, h)] = dg[h]

        @pl.when(pl.program_id(0) == NC - 1)
        def _():
            ex.wait(ex_ins, ex_outs, sems)

    const3 = lambda i: (0, 0, 0)
    outs = pl.pallas_call(
        body, name="retention_bwd",
        grid=(NC,),
        in_specs=[pl.BlockSpec((CHUNK, 4 * RW), lambda i: (phys(i), 0)),
                  pl.BlockSpec((1, 1, half), lambda i: (phys(i), 0, 0)),
                  pl.BlockSpec((1, 1, half), lambda i: (phys(i), 0, 0)),
                  pl.BlockSpec((CHUNK, half), lambda i: (0, 0)),
                  pl.BlockSpec((CHUNK, half), lambda i: (0, 0)),
                  pl.BlockSpec((H, CHUNK, CHUNK), const3),
                  pl.BlockSpec((H, CHUNK, 1), const3),
                  pl.BlockSpec((H, CHUNK, 1), const3),
                  pl.BlockSpec((1, RW), lambda i: (0, 0)),
                  pl.BlockSpec((1, H, hd, hd), lambda i: (phys(i), 0, 0, 0)),
                  pl.BlockSpec((CHUNK, RW), lambda i: (phys(i), 0))] + ex.specs,
        out_specs=[pl.BlockSpec((CHUNK, 4 * RW), lambda i: (phys(i), 0)),
                   pl.BlockSpec((1, RW), lambda i: (0, 0))] + ex.specs,
        out_shape=[jax.ShapeDtypeStruct((R, 4 * RW), BF16), jax.ShapeDtypeStruct((1, RW), F32)] + ex.out_shape,
        scratch_shapes=[pltpu.VMEM((H, hd, hd), F32)] + ex.scratch,
        compiler_params=_params(("arbitrary",), 32),
    )(proj, *rope, jnp.asarray(mask), jnp.asarray(qd), jnp.asarray(kd), gn_g, states, dy, *ex_parts)
    return outs[0], outs[1], outs[2:]


def _dproj_specs(tk, tn, n_ret, tile_axis, col_axis):
    def ret_map(*ids):
        t, j = ids[tile_axis], ids[col_axis]
        return (jnp.where(j < n_ret, t, 0), jnp.minimum(j, n_ret - 1))

    def conv_map(*ids):
        t, j = ids[tile_axis], ids[col_axis]
        return (jnp.where(j >= n_ret, t, 0), jnp.maximum(j - n_ret, 0))

    return pl.BlockSpec((tk, tn), ret_map), pl.BlockSpec((tk, tn), conv_map)


def _w_in_grad(hn, dp_ret, dp_conv):
    R, D = dp_ret.shape[0], hn.shape[1]
    tn = _pick_tile(dp_conv.shape[1] // 3, 1024, 128)
    n_ret, n_conv = dp_ret.shape[1] // tn, dp_conv.shape[1] // tn
    E = dp_ret.shape[1] + dp_conv.shape[1]
    tk = _pick_tile(R, 1024, MXU_DIM)
    n_t = R // tk
    ret_spec, conv_spec = _dproj_specs(tk, tn, n_ret, 1, 0)

    def body(hn_ref, r_ref, c_ref, out_ref, acc):
        j, t = pl.program_id(0), pl.program_id(1)

        for first, ret in ((True, True), (True, False), (False, True), (False, False)):
            @pl.when(((t == 0) == first) & ((j < n_ret) == ret))
            def _():
                part = _dot(hn_ref[...], (r_ref if ret else c_ref)[...], TN)
                if first:
                    acc[...] = part
                else:
                    acc[...] += part

        @pl.when(t == n_t - 1)
        def _():
            out_ref[...] = acc[...].astype(BF16)

    return pl.pallas_call(
        body, name="w_in_grad",
        grid=(n_ret + n_conv, n_t),
        in_specs=[pl.BlockSpec((tk, D), lambda j, t: (t, 0)), ret_spec, conv_spec],
        out_specs=pl.BlockSpec((D, tn), lambda j, t: (0, j)),
        out_shape=jax.ShapeDtypeStruct((D, E), BF16),
        scratch_shapes=[pltpu.VMEM((D, tn), F32)],
        compiler_params=_params(("arbitrary", "arbitrary"), 48),
    )(hn, dp_ret, dp_conv)


def _h_grad(dp_ret, dp_conv, w_in, xs, meta_tile, dh2, ln_g, ex, ex_parts):
    R, D = dh2.shape
    te = CHUNK
    n_x = xs.shape[0] // te
    n_m = meta_tile.shape[0] // te
    tn = _pick_tile(dp_conv.shape[1] // 3, 1024, 128)
    n_ret, n_conv = dp_ret.shape[1] // tn, dp_conv.shape[1] // tn
    n_k = n_ret + n_conv
    tm = _pick_tile(R, 1024, meta_tile.shape[0])
    n_e = tm // te
    n_t = R // tm
    assert n_e <= n_k and (n_x + n_m) * te == R

    def ret_map(t, k):
        return (jnp.where(k < n_ret, jnp.minimum(t, n_t - 1), 0), jnp.minimum(k, n_ret - 1))

    def conv_map(t, k):
        return (jnp.where(k >= n_ret, jnp.minimum(t, n_t - 1), 0), jnp.maximum(k - n_ret, 0))

    def row_block(t, k):
        return jnp.maximum(t - 1, 0) * n_e + jnp.where(t > 0, jnp.minimum(k, n_e - 1), 0)

    def body(*refs):
        r_ref, c_ref, w_hbm, w_ref, x_ref, mt_ref, dh2_ref, g_ref = refs[:8]
        ex_ins = refs[8:8 + ex.n]
        o = 8 + ex.n
        dh_ref, dlg_ref = refs[o:o + 2]
        ex_outs = refs[o + 2:o + 2 + ex.n]
        acc, w_keep, keep_sems = refs[o + 2 + ex.n:o + 5 + ex.n]
        sems = refs[o + 5 + ex.n:]
        t, k = pl.program_id(0), pl.program_id(1)
        cur, old = t % 2, (t + 1) % 2

        def keep(j):
            return pltpu.make_async_copy(w_hbm.at[:, j * tn:(j + 1) * tn], w_keep.at[j], keep_sems.at[j])

        @pl.when((k == 0) & (t == 0))
        def _():
            for j in range(n_ret):
                keep(j).start()
            ex.start(ex_ins, ex_outs, sems)
            dlg_ref[...] = jnp.zeros_like(dlg_ref)

        for j in range(n_ret):
            @pl.when((k == j) & (t == 0))
            def _():
                keep(j).wait()

        @pl.when((k == 0) & (t < n_t))
        def _():
            acc[cur] = _dot(r_ref[...], w_keep[0], NT)

        @pl.when((k > 0) & (k < n_ret) & (t < n_t))
        def _():
            acc[cur] += _dot(r_ref[...], w_keep[k], NT)

        @pl.when((k >= n_ret) & (t < n_t))
        def _():
            acc[cur] += _dot(c_ref[...], w_ref[...], NT)

        @pl.when((k < n_e) & (t > 0))
        def _():
            hv = jnp.where(row_block(t, k) < n_x, x_ref[...], mt_ref[...])
            r = lax.rsqrt(jnp.mean(hv * hv, axis=-1, keepdims=True) + EPS)
            nrm = hv * r
            dhn = acc[old, pl.ds(pl.multiple_of(k * te, te), te), :]
            dlg_ref[...] += jnp.sum(dhn * nrm, axis=0, keepdims=True)
            dn = dhn * g_ref[...]
            dh_ref[...] = dh2_ref[...] + r * (dn - nrm * jnp.mean(dn * nrm, axis=-1, keepdims=True))

        @pl.when((k == n_k - 1) & (t == n_t))
        def _():
            ex.wait(ex_ins, ex_outs, sems)

    row = lambda t, k: (0, 0)
    outs = pl.pallas_call(
        body, name="h_grad",
        grid=(n_t + 1, n_k),
        in_specs=[pl.BlockSpec((tm, tn), ret_map), pl.BlockSpec((tm, tn), conv_map),
                  pl.BlockSpec(memory_space=pl.ANY),
                  pl.BlockSpec((D, tn), lambda t, k: (0, jnp.maximum(k, n_ret))),
                  pl.BlockSpec((te, D), lambda t, k: (jnp.minimum(row_block(t, k), n_x - 1), 0)),
                  pl.BlockSpec((te, D), lambda t, k: (jnp.clip(row_block(t, k) - n_x, 0, n_m - 1), 0)),
                  pl.BlockSpec((te, D), lambda t, k: (row_block(t, k), 0)),
                  pl.BlockSpec((1, D), row)] + ex.specs,
        out_specs=[pl.BlockSpec((te, D), lambda t, k: (row_block(t, k), 0)),
                   pl.BlockSpec((1, D), row)] + ex.specs,
        out_shape=[jax.ShapeDtypeStruct((R, D), F32), jax.ShapeDtypeStruct((1, D), F32)] + ex.out_shape,
        scratch_shapes=[pltpu.VMEM((2, tm, D), F32), pltpu.VMEM((n_ret, D, tn), BF16),
                        pltpu.SemaphoreType.DMA((n_ret,))] + ex.scratch,
        compiler_params=_params(("arbitrary", "arbitrary"), 60),
    )(dp_ret, dp_conv, w_in, w_in, xs, meta_tile, dh2, ln_g, *ex_parts)
    return outs[0], outs[1], outs[2:]


def _adamw(w, g, m, v):
    m = ADAM_B1 * m + (1.0 - ADAM_B1) * g
    v = ADAM_B2 * v + (1.0 - ADAM_B2) * (g * g)
    m_hat = m / (1.0 - ADAM_B1 ** ADAM_STEP)
    v_hat = v / (1.0 - ADAM_B2 ** ADAM_STEP)
    delta = -ADAM_LR * (m_hat / (jnp.sqrt(v_hat) + ADAM_EPS) + ADAM_WD * w)
    return delta, m, v


def _sum_slots(ref):
    g = ref[0].astype(F32)
    for s in range(1, N_DEV):
        g = g + ref[s].astype(F32)
    return g


def _sum_adamw(name, parts, w, m, v, rows_target, ex=None, ex_parts=()):
    rows, cols = w.shape
    tr = _pick_tile(rows, rows_target, 8)
    n_ex = 0 if ex is None else ex.n
    n_steps = rows // tr

    def body(*refs):
        p_ref, w_ref, m_ref, v_ref = refs[:4]
        ex_ins = refs[4:4 + n_ex]
        o = 4 + n_ex
        g_ref, d_ref, nm_ref, nv_ref = refs[o:o + 4]
        ex_outs, sems = refs[o + 4:o + 4 + n_ex], refs[o + 4 + n_ex:]
        if ex is not None:
            @pl.when(pl.program_id(0) == 0)
            def _():
                ex.start(ex_ins, ex_outs, sems)

        g = _sum_slots(p_ref)
        d, nm, nv = _adamw(w_ref[...], g, m_ref[...], v_ref[...])
        g_ref[...] = g
        d_ref[...] = d
        nm_ref[...] = nm
        nv_ref[...] = nv
        if ex is not None:
            @pl.when(pl.program_id(0) == n_steps - 1)
            def _():
                ex.wait(ex_ins, ex_outs, sems)

    tile = pl.BlockSpec((tr, cols), lambda i: (i, 0))
    ex_specs, ex_shape, ex_scratch = ([], [], []) if ex is None else (ex.specs, ex.out_shape, ex.scratch)
    outs = pl.pallas_call(
        body, name=name,
        grid=(n_steps,),
        in_specs=[pl.BlockSpec((N_DEV, tr, cols), lambda i: (0, i, 0)), tile, tile, tile] + ex_specs,
        out_specs=[tile] * 4 + ex_specs,
        out_shape=[jax.ShapeDtypeStruct((rows, cols), F32)] * 4 + ex_shape,
        scratch_shapes=ex_scratch,
        compiler_params=_params(("arbitrary",), 40),
    )(parts, w, m, v, *ex_parts)
    return outs[:4], outs[4:]


def _sum_adamw_small(parts_list, w_list, m_list, v_list, loss_parts):
    n = len(w_list)

    def body(*refs):
        p_refs, w_refs, m_refs, v_refs = refs[:n], refs[n:2 * n], refs[2 * n:3 * n], refs[3 * n:4 * n]
        lp_ref = refs[4 * n]
        outs = refs[4 * n + 1:]
        for a in range(n):
            g = _sum_slots(p_refs[a])
            d, nm, nv = _adamw(w_refs[a][...], g, m_refs[a][...], v_refs[a][...])
            outs[4 * a][...] = g
            outs[4 * a + 1][...] = d
            outs[4 * a + 2][...] = nm
            outs[4 * a + 3][...] = nv
        outs[4 * n][...] = _sum_slots(lp_ref)

    out_shape = []
    for w in w_list:
        out_shape += [jax.ShapeDtypeStruct(w.shape, F32)] * 4
    out_shape.append(jax.ShapeDtypeStruct(loss_parts.shape[1:], F32))
    return pl.pallas_call(body, name="sum_adamw_small", out_shape=out_shape)(
        *parts_list, *w_list, *m_list, *v_list, loss_parts)


def kernel(x, meta_tokens, ln_g, w_in, ret_gn_g, conv_dw_w, conv_dw_b, conv_ln_g, conv_ln_b, conv_pw_w, conv_pw_b, w_out, final_g, loss_target, m_meta_tokens, m_ln_g, m_w_in, m_ret_gn_g, m_conv_dw_w, m_conv_dw_b, m_conv_ln_g, m_conv_ln_b, m_conv_pw_w, m_conv_pw_b, m_w_out, m_final_g, v_meta_tokens, v_ln_g, v_w_in, v_ret_gn_g, v_conv_dw_w, v_conv_dw_b, v_conv_ln_g, v_conv_ln_b, v_conv_pw_w, v_conv_pw_b, v_w_out, v_final_g):
    _, SEQ, D = x.shape
    MIX = w_out.shape[2]
    RW = ret_gn_g.shape[1]
    CW = conv_pw_b.shape[1]
    assert RW == CW and MIX == RW + CW and SEQ % META_TILE == 0 and CONV_K - 1 <= HALO
    R = SEQ + META_TILE
    hd = RW // RET_HEADS
    half = hd // 2
    me = 4 * lax.axis_index("x") + 2 * lax.axis_index("y") + lax.axis_index("c")

    dw_pad = jnp.pad(conv_dw_w[0], ((0, HALO - CONV_K), (0, 0)))

    n_seq_chunks = SEQ // CHUNK
    base = jnp.concatenate([jnp.arange(n_seq_chunks, dtype=F32) * CHUNK + N_META,
                            jnp.zeros((META_TILE // CHUNK - 1,), F32), jnp.full((1,), N_META - CHUNK, F32)])
    inv_freq = ROPE_BASE ** (-jnp.arange(half, dtype=F32) / half)
    ang_base = (base[:, None] * inv_freq[None, :])[:, None, :]
    ang_row = jnp.arange(CHUNK, dtype=F32)[:, None] * inv_freq[None, :]
    rope = (jnp.cos(ang_base), jnp.sin(ang_base), jnp.cos(ang_row), jnp.sin(ang_row))

    xs = x[0]
    target = loss_target[0]
    final_g2 = final_g[None, :]

    small_shards = [meta_tokens, dw_pad]
    hn, meta_tile, (_, dw_g) = _rms_norm(xs, meta_tokens, ln_g, _Exchange(small_shards, [None, None]), small_shards)
    dw_g = jnp.swapaxes(dw_g, 0, 1).reshape(HALO, CW)
    pw_shard, w_out_shard = [conv_pw_w[0].astype(BF16)], [w_out[0].astype(BF16)]
    proj, w_in_g, (pw_g,) = _in_proj_gather(hn, R, w_in[0].astype(BF16), _Exchange(pw_shard, [None]), pw_shard)
    pw_g = pw_g.reshape(CW, CW)
    y, states = _retention_fwd(proj, rope, ret_gn_g, MIX)
    y, conv_out, (w_out_g,) = _conv_fwd(proj, y, dw_g, conv_dw_b, conv_ln_g, conv_ln_b, pw_g, conv_pw_b,
                                        _Exchange(w_out_shard, [None]), w_out_shard)
    w_out_g = w_out_g.reshape(MIX, D)
    dh2, dy, dwo_p, dfg_p, loss_p = _out_proj_loss(xs, meta_tile, y, w_out_g, final_g2, target)

    dp_conv, dpw_p, dww_p, cvec_p = _conv_bwd(proj, conv_out, dy, dw_g, conv_ln_g, conv_ln_b, pw_g, conv_pw_b)
    dp_ret, dgn_p, (r_wo, r_pw) = _retention_bwd(proj, rope, ret_gn_g, states, dy,
                                                 _Exchange([dwo_p, dpw_p], [0, 0]), [dwo_p, dpw_p])
    dwi_p = _w_in_grad(hn, dp_ret, dp_conv)
    dh, dlg_p, (r_wi,) = _h_grad(dp_ret, dp_conv, w_in_g, xs, meta_tile, dh2, ln_g, _Exchange([dwi_p], [1]), [dwi_p])
    grad_x = dh[:SEQ][None]

    def at_row(r, a, b=None):
        v = a if b is None else jnp.concatenate([a, b], axis=1)
        return jnp.pad(v, ((r, 7 - r), (0, D - v.shape[1])))
    vec8 = (at_row(0, dlg_p) + at_row(1, dfg_p)
            + at_row(2, dgn_p, cvec_p[3:4])
            + at_row(3, cvec_p[1:2], cvec_p[2:3])
            + at_row(4, cvec_p[0:1], jnp.broadcast_to(loss_p, (1, CW))))
    small = jnp.concatenate([dh[R - N_META:], vec8,
                             jnp.zeros((SMALL_ROWS - N_META - 8, D), F32)], axis=0)

    (g_wi, d_wi, nm_wi, nv_wi), (r_dww, r_small) = _sum_adamw(
        "sum_adamw_w_in", r_wi, w_in[0], m_w_in[0], v_w_in[0], 256,
        ex=_Exchange([dww_p, small], [1, None]), ex_parts=[dww_p, small])
    (g_wo, d_wo, nm_wo, nv_wo), _ = _sum_adamw("sum_adamw_w_out", r_wo, w_out[0], m_w_out[0], v_w_out[0], 128)
    (g_pw, d_pw, nm_pw, nv_pw), _ = _sum_adamw("sum_adamw_pw", r_pw, conv_pw_w[0], m_conv_pw_w[0], v_conv_pw_w[0], 128)

    dcol = D // N_DEV
    sm = lambda r0, nr, c0, nc: lax.slice(r_small, (0, r0, c0), (N_DEV, r0 + nr, c0 + nc))
    meta_parts = lax.dynamic_slice(r_small, (0, 0, me * dcol), (N_DEV, N_META, dcol))
    small_parts = [meta_parts, sm(16, 1, 0, D), sm(18, 1, 0, RW), r_dww, sm(18, 1, RW, CW),
                   sm(19, 1, 0, CW), sm(19, 1, CW, CW), sm(20, 1, 0, CW), sm(17, 1, 0, D)]
    pad31 = lambda a: jnp.pad(a, ((0, HALO - CONV_K), (0, 0)))
    ws = [meta_tokens, ln_g, ret_gn_g, pad31(conv_dw_w[0]), conv_dw_b, conv_ln_g, conv_ln_b, conv_pw_b, final_g2]
    ms = [m_meta_tokens, m_ln_g, m_ret_gn_g, pad31(m_conv_dw_w[0]), m_conv_dw_b, m_conv_ln_g, m_conv_ln_b,
          m_conv_pw_b, m_final_g[None, :]]
    vs = [v_meta_tokens, v_ln_g, v_ret_gn_g, pad31(v_conv_dw_w[0]), v_conv_dw_b, v_conv_ln_g, v_conv_ln_b,
          v_conv_pw_b, v_final_g[None, :]]
    loss_parts = sm(20, 1, CW, 1)
    outs = _sum_adamw_small(small_parts, ws, ms, vs, loss_parts)
    loss = outs[-1][0, 0]
    quad = [outs[4 * a:4 * a + 4] for a in range(len(ws))]
    (q_meta, q_lng, q_gn, q_dww, q_dwb, q_clg, q_clb, q_pwb, q_fg) = quad
    q_dww = [t[:CONV_K][None] for t in q_dww]
    q_fg = [t[0] for t in q_fg]
    q_wi = [t[None] for t in (g_wi, d_wi, nm_wi, nv_wi)]
    q_wo = [t[None] for t in (g_wo, d_wo, nm_wo, nv_wo)]
    q_pw = [t[None] for t in (g_pw, d_pw, nm_pw, nv_pw)]

    per_w = [q_meta, q_lng, q_wi, q_gn, q_dww, q_dwb, q_clg, q_clb, q_pw, q_pwb, q_wo, q_fg]
    result = [loss, grad_x]
    for which in range(4):
        result += [q[which] for q in per_w]
    return tuple(result)
```

```python
import numpy as np
import jax
import jax.numpy as jnp
from jax import lax
from jax.experimental import pallas as pl
from jax.experimental.pallas import tpu as pltpu

N_META = 16
RET_HEADS = 4
CONV_K = 31
CHUNK = 128
ROPE_BASE = 10000.0
EPS = 1e-6
ADAM_LR = 0.001
ADAM_B1 = 0.9
ADAM_B2 = 0.999
ADAM_EPS = 1e-08
ADAM_WD = 0.01
ADAM_STEP = 10

N_DEV = 8
META_TILE = 256
HALO = 32
SMALL_ROWS = 32
VMEM_BYTES_V7X = 64 * 1024 * 1024
MXU_DIM = 256

F32 = jnp.float32
BF16 = jnp.bfloat16
MESH = pl.DeviceIdType.MESH

NN = (((1,), (0,)), ((), ()))
NT = (((1,), (1,)), ((), ()))
TN = (((0,), (0,)), ((), ()))


def _dot(a, b, dims=NN):
    return lax.dot_general(a, b, dims, preferred_element_type=F32)


def _pick_tile(n, target, mult=16):
    best = None
    for t in range(mult, min(n, target) + 1, mult):
        if n % t == 0:
            best = t
    assert best is not None, (n, target)
    return best


def _params(sem=None, vmem_mb=None):
    kw = {}
    if sem is not None:
        kw["dimension_semantics"] = sem
    if vmem_mb is not None:
        kw["vmem_limit_bytes"] = min(vmem_mb * 1024 * 1024, VMEM_BYTES_V7X - 4 * 1024 * 1024)
    return pltpu.CompilerParams(**kw)


def _sigmoid(x):
    return jax.nn.sigmoid(x)


def _dsilu(x, sg):
    return sg * (1.0 + x * (1.0 - sg))


def _decay_tables(heads):
    h = np.arange(heads, dtype=np.float32)
    gamma = (1.0 - np.exp2(-5.0 - h)).astype(np.float32)
    log_g = np.log(gamma).astype(np.float32)
    idx = np.arange(CHUNK, dtype=np.float32)
    rel = idx[:, None] - idx[None, :]
    mask = np.where(rel[None] >= 0, np.exp(np.maximum(rel, 0.0)[None] * log_g[:, None, None]), 0.0)
    qd = np.exp((idx[None, :] + 1.0) * log_g[:, None])
    kd = np.exp((CHUNK - 1.0 - idx[None, :]) * log_g[:, None])
    cd = np.exp(CHUNK * log_g)
    return (mask.astype(np.float32), qd.astype(np.float32)[:, :, None], kd.astype(np.float32)[:, :, None],
            [float(c) for c in cd.astype(np.float32)])


class _Exchange:
    def __init__(self, parts, block_axes):
        self.block_axes = list(block_axes)
        self.n = len(parts)
        self.out_shape = []
        for p, ax in zip(parts, block_axes):
            shp = list(p.shape)
            if ax is not None:
                assert shp[ax] % N_DEV == 0
                shp[ax] //= N_DEV
            self.out_shape.append(jax.ShapeDtypeStruct((N_DEV, *shp), p.dtype))
        self.scratch = [pltpu.SemaphoreType.DMA((self.n, N_DEV - 1)), pltpu.SemaphoreType.DMA((self.n, N_DEV - 1)),
                        pltpu.SemaphoreType.DMA((self.n,))]
        self.specs = [pl.BlockSpec(memory_space=pl.ANY)] * self.n

    def _copies(self, ins, outs, sems):
        send_sems, recv_sems, local_sems = sems
        x, y, c = lax.axis_index("x"), lax.axis_index("y"), lax.axis_index("c")
        me_idx = 4 * x + 2 * y + c

        def src_block(a, dev_idx):
            ax = self.block_axes[a]
            if ax is None:
                return ins[a]
            n = ins[a].shape[ax] // N_DEV
            idx = [slice(None)] * len(ins[a].shape)
            idx[ax] = pl.ds(pl.multiple_of(dev_idx * n, n), n)
            return ins[a].at[tuple(idx)]

        local = [pltpu.make_async_copy(src_block(a, me_idx), outs[a].at[me_idx], local_sems.at[a])
                 for a in range(self.n)]
        remote = []
        for m in range(1, N_DEV):
            px, py, pc = x ^ ((m >> 2) & 1), y ^ ((m >> 1) & 1), c ^ (m & 1)
            for a in range(self.n):
                remote.append(pltpu.make_async_remote_copy(
                    src_ref=src_block(a, 4 * px + 2 * py + pc), dst_ref=outs[a].at[me_idx],
                    send_sem=send_sems.at[a, m - 1], recv_sem=recv_sems.at[a, m - 1],
                    device_id=(px, py, pc), device_id_type=MESH))
        return local, remote

    def start(self, ins, outs, sems):
        local, remote = self._copies(ins, outs, sems)
        for cp in local + remote:
            cp.start()

    def wait(self, ins, outs, sems):
        local, remote = self._copies(ins, outs, sems)
        for cp in remote:
            cp.wait_recv()
        for cp in remote:
            cp.wait_send()
        for cp in local:
            cp.wait()


def _rms_norm(xs, meta_shard, ln_g, ex, ex_parts):
    SEQ, D = xs.shape
    n_meta, dcol = meta_shard.shape
    tm = 2 * META_TILE
    n_seq = SEQ // tm

    def body(*refs):
        x_ref, g_ref = refs[:2]
        ex_ins = refs[2:2 + ex.n]
        hn_ref, mt_ref = refs[2 + ex.n:4 + ex.n]
        ex_outs = refs[4 + ex.n:4 + 2 * ex.n]
        slots, slot_sem = refs[4 + 2 * ex.n:6 + 2 * ex.n]
        sems = refs[6 + 2 * ex.n:]
        i = pl.program_id(0)

        @pl.when(i == 0)
        def _():
            ex.start(ex_ins, ex_outs, sems)

        def norm(hv):
            r = lax.rsqrt(jnp.mean(hv * hv, axis=-1, keepdims=True) + EPS)
            return (hv * r * g_ref[...]).astype(BF16)

        @pl.when(i < n_seq)
        def _():
            hn_ref[...] = norm(x_ref[...])

        @pl.when(i == n_seq)
        def _():
            ex.wait(ex_ins, ex_outs, sems)
            cp = pltpu.make_async_copy(ex_outs[0], slots, slot_sem)
            cp.start()
            mt_ref[...] = jnp.zeros_like(mt_ref)
            cp.wait()
            for s in range(N_DEV):
                mt_ref[META_TILE - n_meta:META_TILE, s * dcol:(s + 1) * dcol] = slots[s]
            hn_ref[0:META_TILE, :] = norm(mt_ref[...])

    outs = pl.pallas_call(
        body, name="rms_norm",
        grid=(n_seq + 1,),
        in_specs=[pl.BlockSpec((tm, D), lambda i: (jnp.minimum(i, n_seq - 1), 0)),
                  pl.BlockSpec((1, D), lambda i: (0, 0))] + ex.specs,
        out_specs=[pl.BlockSpec((tm, D), lambda i: (i, 0)),
                   pl.BlockSpec((META_TILE, D), lambda i: (0, 0))] + ex.specs,
        out_shape=[jax.ShapeDtypeStruct((SEQ + tm, D), BF16), jax.ShapeDtypeStruct((META_TILE, D), F32)]
                  + ex.out_shape,
        scratch_shapes=[pltpu.VMEM((N_DEV, n_meta, dcol), F32), pltpu.SemaphoreType.DMA] + ex.scratch,
        compiler_params=_params(("arbitrary",), 32),
    )(xs, ln_g, *ex_parts)
    return outs[0], outs[1], outs[2:]


def _chip_visited(q):
    mine = 2 * lax.axis_index("x") + lax.axis_index("y")
    return mine ^ (((q & 1) << 1) | (q >> 1))


def _in_proj_gather(hn, n_rows, w_shard, ex, ex_parts):
    R, D = n_rows, hn.shape[1]
    wb = w_shard.shape[1]
    E, tn = wb * N_DEV, 2 * wb
    n_q = N_DEV // 2
    tm = _pick_tile(R, min(768, R // 2), MXU_DIM)
    n_i = R // tm

    def body(*refs):
        hn_ref, wsh_hbm = refs[:2]
        ex_ins = refs[2:2 + ex.n]
        proj_ref, wg_hbm = refs[2 + ex.n:4 + ex.n]
        ex_outs = refs[4 + ex.n:4 + 2 * ex.n]
        w_vmem, send_sems, recv_sems, local_sem, vmem_sems = refs[4 + 2 * ex.n:9 + 2 * ex.n]
        ex_sems = refs[9 + 2 * ex.n:]
        q, i = pl.program_id(0), pl.program_id(1)
        x, y, c = lax.axis_index("x"), lax.axis_index("y"), lax.axis_index("c")
        me, sibling = (x, y, c), (x, y, 1 - c)
        chips = [(1 - x, y), (x, 1 - y), (1 - x, 1 - y)]

        def block(dev):
            return wg_hbm.at[:, pl.ds(pl.multiple_of((4 * dev[0] + 2 * dev[1] + dev[2]) * wb, wb), wb)]

        def copy(k, dev, to, src=None):
            return pltpu.make_async_remote_copy(
                src_ref=block(dev) if src is None else src, dst_ref=block(dev),
                send_sem=send_sems.at[k], recv_sem=recv_sems.at[k], device_id=to, device_id_type=MESH)

        def to_vmem(p):
            cols = pl.ds(pl.multiple_of(_chip_visited(p) * tn, tn), tn)
            return pltpu.make_async_copy(wg_hbm.at[:, cols], w_vmem.at[p % 2], vmem_sems.at[p % 2])

        mine = pltpu.make_async_copy(wsh_hbm, block(me), local_sem)
        first = [copy(0, me, sibling, src=wsh_hbm)] + [copy(1 + j, me, (*chip, c), src=wsh_hbm)
                                                       for j, chip in enumerate(chips)]
        passed = [copy(4 + j, (*chip, c), sibling) for j, chip in enumerate(chips)]

        @pl.when((q == 0) & (i == 0))
        def _():
            mine.start()
            for cp in first:
                cp.start()
            ex.start(ex_ins, ex_outs, ex_sems)
            mine.wait()
            copy(0, sibling, me).wait_recv()
            to_vmem(0).start()
            to_vmem(0).wait()

        for p in range(1, n_q):
            chip = chips[p - 1]

            @pl.when((q == p - 1) & (i == n_i - 2))
            def _():
                copy(p, (*chip, c), me).wait_recv()
                passed[p - 1].start()

            @pl.when((q == p - 1) & (i == n_i - 1))
            def _():
                copy(3 + p, (*chip, 1 - c), me).wait_recv()
                to_vmem(p).start()

            @pl.when((q == p) & (i == 0))
            def _():
                to_vmem(p).wait()

        proj_ref[...] = _dot(hn_ref[...], w_vmem[q % 2]).astype(BF16)

        @pl.when((q == n_q - 1) & (i == n_i - 1))
        def _():
            for cp in first + passed:
                cp.wait_send()
            ex.wait(ex_ins, ex_outs, ex_sems)

    hbm = pl.BlockSpec(memory_space=pl.ANY)
    outs = pl.pallas_call(
        body, name="in_proj",
        grid=(n_q, n_i),
        in_specs=[pl.BlockSpec((tm, D), lambda q, i: (i, 0)), hbm] + ex.specs,
        out_specs=[pl.BlockSpec((tm, tn), lambda q, i: (i, _chip_visited(q))), hbm] + ex.specs,
        out_shape=[jax.ShapeDtypeStruct((R, E), BF16), jax.ShapeDtypeStruct((D, E), BF16)] + ex.out_shape,
        scratch_shapes=[pltpu.VMEM((2, D, tn), BF16), pltpu.SemaphoreType.DMA((7,)), pltpu.SemaphoreType.DMA((7,)),
                        pltpu.SemaphoreType.DMA, pltpu.SemaphoreType.DMA((2,))] + ex.scratch,
        compiler_params=_params(("arbitrary", "arbitrary"), 48),
    )(hn, w_shard, *ex_parts)
    return outs[0], outs[1], outs[2:]


def _rope_chunk(cb_ref, sb_ref, ci_ref, si_ref):
    cb, sb, ci, si = cb_ref[0], sb_ref[0], ci_ref[...], si_ref[...]
    return cb * ci - sb * si, sb * ci + cb * si


def _rot(t, cos, sin, half):
    t1, t2 = t[:, :half], t[:, half:]
    return jnp.concatenate([t1 * cos - t2 * sin, t1 * sin + t2 * cos], axis=-1)


def _rot_inv(t, cos, sin, half):
    t1, t2 = t[:, :half], t[:, half:]
    return jnp.concatenate([t1 * cos + t2 * sin, t2 * cos - t1 * sin], axis=-1)


def _chunk_order(n_chunks):
    lead = META_TILE // CHUNK
    return lambda l: (l + n_chunks - lead) % n_chunks


def _retention_fwd(proj, rope, gn_g, mix):
    R, E = proj.shape
    RW = gn_g.shape[1]
    H = RET_HEADS
    hd = RW // H
    half = hd // 2
    NC = R // CHUNK
    mask, qd, kd, cd = _decay_tables(H)
    scale = float(hd) ** -0.5
    PAIR = META_TILE // CHUNK
    n_grp = NC // PAIR
    phys = lambda l: (l + n_grp - 1) % n_grp

    def body(p_ref, cb_ref, sb_ref, ci_ref, si_ref, mask_ref, qd_ref, kd_ref, gn_ref, y_ref, st_ref, state):
        @pl.when(pl.program_id(0) == 0)
        def _():
            state[...] = jnp.zeros_like(state)

        hs = range(H)
        col = lambda j, h: slice(j * RW + h * hd, j * RW + (h + 1) * hd)
        for sub in range(PAIR):
            rows = slice(sub * CHUNK, (sub + 1) * CHUNK)
            cb, sb, ci, si = cb_ref[sub], sb_ref[sub], ci_ref[...], si_ref[...]
            cs, sn = cb * ci - sb * si, sb * ci + cb * si
            qr = [_rot(p_ref[rows, col(0, h)].astype(F32), cs, sn, half) for h in hs]
            kr = [_rot(p_ref[rows, col(1, h)].astype(F32), cs, sn, half) * scale for h in hs]
            v = [p_ref[rows, col(2, h)] for h in hs]
            s_prev = [state[h] for h in hs]
            s_prev_b = [s_prev[h].astype(BF16) for h in hs]
            s = [(_dot(qr[h].astype(BF16), kr[h].astype(BF16), NT) * mask_ref[h]).astype(BF16) for h in hs]
            y_raw = [_dot(s[h], v[h]) + _dot((qr[h] * qd_ref[h]).astype(BF16), s_prev_b[h]) for h in hs]
            s_new = [s_prev[h] * cd[h] + _dot((kr[h] * kd_ref[h]).astype(BF16), v[h], TN) for h in hs]
            for h in hs:
                st_ref[sub, h] = s_prev_b[h]
                state[h] = s_new[h]
            for h in hs:
                g = p_ref[rows, col(3, h)].astype(F32)
                mu = jnp.mean(y_raw[h], axis=-1, keepdims=True)
                yc = y_raw[h] - mu
                var = jnp.mean(yc * yc, axis=-1, keepdims=True)
                out = yc * lax.rsqrt(var + EPS) * gn_ref[:, col(0, h)] * (g * _sigmoid(g))
                y_ref[rows, col(0, h)] = out.astype(BF16)

    const3 = lambda l: (0, 0, 0)
    return pl.pallas_call(
        body, name="retention_fwd",
        grid=(NC // PAIR,),
        in_specs=[pl.BlockSpec((PAIR * CHUNK, 4 * RW), lambda l: (phys(l), 0)),
                  pl.BlockSpec((PAIR, 1, half), lambda l: (phys(l), 0, 0)),
                  pl.BlockSpec((PAIR, 1, half), lambda l: (phys(l), 0, 0)),
                  pl.BlockSpec((CHUNK, half), lambda l: (0, 0)),
                  pl.BlockSpec((CHUNK, half), lambda l: (0, 0)),
                  pl.BlockSpec((H, CHUNK, CHUNK), const3),
                  pl.BlockSpec((H, CHUNK, 1), const3),
                  pl.BlockSpec((H, CHUNK, 1), const3),
                  pl.BlockSpec((1, RW), lambda l: (0, 0))],
        out_specs=[pl.BlockSpec((PAIR * CHUNK, RW), lambda l: (phys(l), 0)),
                   pl.BlockSpec((PAIR, H, hd, hd), lambda l: (phys(l), 0, 0, 0))],
        out_shape=[jax.ShapeDtypeStruct((R, mix), BF16), jax.ShapeDtypeStruct((NC, H, hd, hd), BF16)],
        scratch_shapes=[pltpu.VMEM((H, hd, hd), F32)],
        compiler_params=_params(("arbitrary",), 40),
    )(proj, *rope, jnp.asarray(mask), jnp.asarray(qd), jnp.asarray(kd), gn_g)


CONV_ROWS = 64
CONV_LANES = 128
LANE = 128
ELEM_ROWS = 32


def _conv_order(n_tiles):
    return lambda l: (l + n_tiles - 1) % n_tiles


def _halo_block(n_tiles, tm):
    per = tm // HALO
    return lambda l: ((l + n_tiles - 2) % n_tiles) * per + per - 1


def _fill_shifted(src, dst):
    rows, width = dst.shape[1], dst.shape[2]
    step = _pick_tile(rows, 64, 8)
    for r in range(1, 8):
        for r0 in range(0, rows, step):
            for l0 in range(0, width, CONV_LANES):
                dst[r - 1, r0:r0 + step, l0:l0 + CONV_LANES] = src[r + r0:r + r0 + step, l0:l0 + CONV_LANES]


def _at_offset(src, shifted, off, r0, rows, lanes):
    r = off % 8
    a = off - r + r0
    if r == 0:
        return src[a:a + rows, lanes]
    return shifted[r - 1, a:a + rows, lanes]


def _fill_glu(first, a_ref, b_ref, ah_ref, bh_ref, u_ext, tm):
    uh = ah_ref[...].astype(F32) * _sigmoid(bh_ref[...].astype(F32))
    u_ext[0:HALO, :] = jnp.where(first, 0.0, uh)
    for r0 in range(0, tm, ELEM_ROWS):
        rows = slice(r0, r0 + ELEM_ROWS)
        u_ext[HALO + r0:HALO + r0 + ELEM_ROWS, :] = a_ref[rows, :].astype(F32) * _sigmoid(b_ref[rows, :].astype(F32))


def _layer_norm(cv, lg_ref, lb_ref):
    mu = jnp.mean(cv, axis=-1, keepdims=True)
    cc = cv - mu
    rstd = lax.rsqrt(jnp.mean(cc * cc, axis=-1, keepdims=True) + EPS)
    xh = cc * rstd
    return xh, rstd, xh * lg_ref[...] + lb_ref[...]


def _conv_fwd(proj, y_in, dw_w, dw_b, ln_g, ln_b, pw_w, pw_b, ex, ex_parts):
    R, E = proj.shape
    CW = pw_w.shape[0]
    tm = META_TILE
    NTL = R // tm
    phys = _conv_order(NTL)
    halo = _halo_block(NTL, tm)
    cb = (E - 3 * CW) // CW
    base = HALO - (CONV_K - 1)

    def body(*refs):
        a_ref, b_ref, g_ref, ah_ref, bh_ref, w_ref, wb_ref, lg_ref, lb_ref, pw_ref, pb_ref, yin_ref = refs[:12]
        ex_ins = refs[12:12 + ex.n]
        y_ref, c_ref = refs[12 + ex.n:14 + ex.n]
        ex_outs = refs[14 + ex.n:14 + 2 * ex.n]
        u_ext, u_sh, s_scr, upw_scr = refs[14 + 2 * ex.n:18 + 2 * ex.n]
        sems = refs[18 + 2 * ex.n:]

        @pl.when(pl.program_id(0) == 0)
        def _():
            ex.start(ex_ins, ex_outs, sems)

        _fill_glu(pl.program_id(0) == 0, a_ref, b_ref, ah_ref, bh_ref, u_ext, tm)
        _fill_shifted(u_ext, u_sh)
        for r0 in range(0, tm, CONV_ROWS):
            for l0 in range(0, CW, CONV_LANES):
                lanes = slice(l0, l0 + CONV_LANES)
                acc = None
                for k in range(CONV_K):
                    term = _at_offset(u_ext, u_sh, base + k, r0, CONV_ROWS, lanes) * w_ref[k:k + 1, lanes]
                    acc = term if acc is None else acc + term
                c_ref[r0:r0 + CONV_ROWS, lanes] = acc + wb_ref[:, lanes]
        blocks = [slice(r0, r0 + ELEM_ROWS) for r0 in range(0, tm, ELEM_ROWS)]
        for rows in blocks:
            _, _, ln = _layer_norm(c_ref[rows, :], lg_ref, lb_ref)
            s_scr[rows, :] = (ln * _sigmoid(ln)).astype(BF16)
        upw_scr[...] = _dot(s_scr[...], pw_ref[...]) + pb_ref[...]
        for rows in blocks:
            g = g_ref[rows, :].astype(F32)
            y_ref[rows, :] = (upw_scr[rows, :] * (g * _sigmoid(g))).astype(BF16)

        @pl.when(pl.program_id(0) == NTL - 1)
        def _():
            ex.wait(ex_ins, ex_outs, sems)

    row = lambda l: (0, 0)
    outs = pl.pallas_call(
        body, name="conv_fwd",
        grid=(NTL,),
        in_specs=[pl.BlockSpec((tm, CW), lambda l: (phys(l), cb)),
                  pl.BlockSpec((tm, CW), lambda l: (phys(l), cb + 1)),
                  pl.BlockSpec((tm, CW), lambda l: (phys(l), cb + 2)),
                  pl.BlockSpec((HALO, CW), lambda l: (halo(l), cb)),
                  pl.BlockSpec((HALO, CW), lambda l: (halo(l), cb + 1)),
                  pl.BlockSpec((HALO, CW), row),
                  pl.BlockSpec((1, CW), row), pl.BlockSpec((1, CW), row), pl.BlockSpec((1, CW), row),
                  pl.BlockSpec((CW, CW), row),
                  pl.BlockSpec((1, CW), row),
                  pl.BlockSpec(memory_space=pl.ANY)] + ex.specs,
        out_specs=[pl.BlockSpec((tm, CW), lambda l: (phys(l), 1)),
                   pl.BlockSpec((tm, CW), lambda l: (phys(l), 0))] + ex.specs,
        out_shape=[jax.ShapeDtypeStruct(y_in.shape, BF16), jax.ShapeDtypeStruct((R, CW), F32)] + ex.out_shape,
        input_output_aliases={11: 0},
        scratch_shapes=[pltpu.VMEM((HALO + tm, CW), F32), pltpu.VMEM((7, tm + HALO - 8, CW), F32),
                        pltpu.VMEM((tm, CW), BF16), pltpu.VMEM((tm, CW), F32)] + ex.scratch,
        compiler_params=_params(("arbitrary",), 48),
    )(proj, proj, proj, proj, proj, dw_w, dw_b, ln_g, ln_b, pw_w, pw_b, y_in, *ex_parts)
    return outs[0], outs[1], outs[2:]


def _out_proj_loss(xs, meta_tile, y, w_out, final_g, target):
    SEQ, D = xs.shape
    R, MIX = y.shape
    tm = META_TILE
    n_seq = SEQ // tm
    n_tiles = R // tm
    rows_out = _pick_tile(MIX, 256)

    def body(x_ref, mt_ref, y_ref, w_hbm, fg_ref, t_ref, dh2_ref, dy_ref, dwo_hbm, dfg_ref, loss_ref,
             w_scr, acc, stage, sem, put_sems):
        i = pl.program_id(0)

        @pl.when(i == 0)
        def _():
            cp = pltpu.make_async_copy(w_hbm, w_scr, sem)
            cp.start()
            acc[...] = jnp.zeros_like(acc)
            dfg_ref[...] = jnp.zeros_like(dfg_ref)
            loss_ref[...] = jnp.zeros_like(loss_ref)
            cp.wait()

        yb = y_ref[...]
        h2 = jnp.where(i < n_seq, x_ref[...], mt_ref[...]) + _dot(yb, w_scr[...])
        r2 = lax.rsqrt(jnp.mean(h2 * h2, axis=-1, keepdims=True) + EPS)
        n = h2 * r2
        fg = fg_ref[...]
        err = jnp.where(i < n_seq, n * fg - t_ref[...], 0.0)
        loss_ref[...] += 0.5 * jnp.sum(jnp.mean(err * err, axis=-1, keepdims=True), axis=0, keepdims=True)
        dout = err * (1.0 / D)
        dfg_ref[...] += jnp.sum(dout * n, axis=0, keepdims=True)
        dn = dout * fg
        dh2 = r2 * (dn - n * jnp.mean(dn * n, axis=-1, keepdims=True))
        dh2_ref[...] = dh2
        dh2b = dh2.astype(BF16)
        dy_ref[...] = _dot(dh2b, w_scr[...], NT).astype(BF16)
        acc[...] += _dot(yb, dh2b, TN)

        @pl.when(i == n_tiles - 1)
        def _():
            def put(n):
                rows = slice(n * rows_out, (n + 1) * rows_out)
                return pltpu.make_async_copy(stage.at[n % 2], dwo_hbm.at[rows, :], put_sems.at[n % 2])

            n_put = MIX // rows_out
            for n in range(n_put):
                if n >= 2:
                    put(n - 2).wait()
                stage[n % 2] = acc[n * rows_out:(n + 1) * rows_out, :].astype(BF16)
                put(n).start()
            for n in range(max(n_put - 2, 0), n_put):
                put(n).wait()

    row = lambda i: (0, 0)
    return pl.pallas_call(
        body, name="out_proj_loss",
        grid=(n_tiles,),
        in_specs=[pl.BlockSpec((tm, D), lambda i: (jnp.minimum(i, n_seq - 1), 0)),
                  pl.BlockSpec((tm, D), row),
                  pl.BlockSpec((tm, MIX), lambda i: (i, 0)),
                  pl.BlockSpec(memory_space=pl.ANY),
                  pl.BlockSpec((1, D), row),
                  pl.BlockSpec((tm, D), lambda i: (jnp.minimum(i, n_seq - 1), 0))],
        out_specs=[pl.BlockSpec((tm, D), lambda i: (i, 0)),
                   pl.BlockSpec((tm, MIX), lambda i: (i, 0)),
                   pl.BlockSpec(memory_space=pl.ANY),
                   pl.BlockSpec((1, D), row),
                   pl.BlockSpec((1, 1), row)],
        out_shape=[jax.ShapeDtypeStruct((R, D), F32), jax.ShapeDtypeStruct((R, MIX), BF16),
                   jax.ShapeDtypeStruct((MIX, D), BF16), jax.ShapeDtypeStruct((1, D), F32),
                   jax.ShapeDtypeStruct((1, 1), F32)],
        scratch_shapes=[pltpu.VMEM((MIX, D), BF16), pltpu.VMEM((MIX, D), F32), pltpu.VMEM((2, rows_out, D), BF16),
                        pltpu.SemaphoreType.DMA, pltpu.SemaphoreType.DMA((2,))],
        compiler_params=_params(("arbitrary",), 60),
    )(xs, meta_tile, y, w_out, final_g, target)


def _conv_bwd(proj, conv_out, dy, dw_w, ln_g, ln_b, pw_w, pw_b):
    R, E = proj.shape
    CW = pw_w.shape[0]
    tm = META_TILE
    NTL = R // tm
    order = _conv_order(NTL)
    phys = lambda i: order(NTL - 1 - i)
    halo_l = _halo_block(NTL, tm)
    halo = lambda i: halo_l(NTL - 1 - i)
    cb = (E - 3 * CW) // CW
    base = HALO - (CONV_K - 1)

    def body(a_ref, b_ref, g_ref, ah_ref, bh_ref, c_ref, dy_ref, w_ref, lg_ref, lb_ref, pw_ref, pb_ref,
             dp_ref, dpw_ref, dww_ref, vec_ref, u_ext, u_sh, dc_ext, dc_sh, du_scr, dww_acc, dpw_acc,
             xh_scr, rstd_scr, ln_scr, sg_scr, upw_scr, s_scr, dupw_scr):
        i = pl.program_id(0)

        @pl.when(i == 0)
        def _():
            dpw_acc[...] = jnp.zeros_like(dpw_acc)
            dww_ref[...] = jnp.zeros_like(dww_ref)
            vec_ref[...] = jnp.zeros_like(vec_ref)
            dww_acc[...] = jnp.zeros_like(dww_acc)
            dc_ext[tm:tm + HALO, :] = jnp.zeros((HALO, CW), F32)

        _fill_glu(i == NTL - 1, a_ref, b_ref, ah_ref, bh_ref, u_ext, tm)
        _fill_shifted(u_ext, u_sh)
        blocks = [slice(r0, r0 + ELEM_ROWS) for r0 in range(0, tm, ELEM_ROWS)]
        for rows in blocks:
            xh, rstd, ln = _layer_norm(c_ref[rows, :], lg_ref, lb_ref)
            sg = _sigmoid(ln)
            xh_scr[rows, :], rstd_scr[rows, :], ln_scr[rows, :], sg_scr[rows, :] = xh, rstd, ln, sg
            s_scr[rows, :] = (ln * sg).astype(BF16)
        upw_scr[...] = _dot(s_scr[...], pw_ref[...]) + pb_ref[...]
        col_sum = jnp.zeros((1, CW), F32)
        for rows in blocks:
            g = g_ref[rows, :].astype(F32)
            sgg = _sigmoid(g)
            dyc = dy_ref[rows, :].astype(F32)
            dp_ref[rows, 2 * CW:3 * CW] = (dyc * upw_scr[rows, :] * _dsilu(g, sgg)).astype(BF16)
            dupw = dyc * (g * sgg)
            dupw_scr[rows, :] = dupw.astype(BF16)
            col_sum = col_sum + jnp.sum(dupw, axis=0, keepdims=True)
        vec_ref[0:1, :] += col_sum
        dpw_acc[...] += _dot(s_scr[...], dupw_scr[...], TN)
        upw_scr[...] = _dot(dupw_scr[...], pw_ref[...], NT)
        sum_g, sum_b, sum_c = col_sum * 0.0, col_sum * 0.0, col_sum * 0.0
        for rows in blocks:
            xh, rstd = xh_scr[rows, :], rstd_scr[rows, :]
            dln = upw_scr[rows, :] * _dsilu(ln_scr[rows, :], sg_scr[rows, :])
            sum_g = sum_g + jnp.sum(dln * xh, axis=0, keepdims=True)
            sum_b = sum_b + jnp.sum(dln, axis=0, keepdims=True)
            dxh = dln * lg_ref[...]
            dc = rstd * (dxh - jnp.mean(dxh, axis=-1, keepdims=True)
                         - xh * jnp.mean(dxh * xh, axis=-1, keepdims=True))
            sum_c = sum_c + jnp.sum(dc, axis=0, keepdims=True)
            dc_ext[rows, :] = dc
        vec_ref[1:2, :] += sum_g
        vec_ref[2:3, :] += sum_b
        vec_ref[3:4, :] += sum_c
        _fill_shifted(dc_ext, dc_sh)

        for l0 in range(0, CW, CONV_LANES):
            lanes = slice(l0, l0 + CONV_LANES)
            for r0 in range(0, tm, CONV_ROWS):
                acc = None
                for k in range(CONV_K):
                    term = _at_offset(dc_ext, dc_sh, CONV_K - 1 - k, r0, CONV_ROWS, lanes) * w_ref[k:k + 1, lanes]
                    acc = term if acc is None else acc + term
                du_scr[r0:r0 + CONV_ROWS, lanes] = acc

        n_grp = tm // 8
        by_shift = [[(k, (base + k) // 8) for k in range(CONV_K) if (base + k) % 8 == r] for r in range(8)]
        for l0 in range(0, CW, LANE):
            lane = slice(l0, l0 + LANE)
            for r in range(8):
                src = u_ext if r == 0 else u_sh.at[r - 1]
                a_lo, a_hi = by_shift[r][0][1], by_shift[r][-1][1]
                sums = {k: None for k, _ in by_shift[r]}
                dcg = {}
                for gi in range(a_lo, n_grp + a_hi):
                    if gi - a_lo < n_grp:
                        dcg[gi - a_lo] = dc_ext[8 * (gi - a_lo):8 * (gi - a_lo) + 8, lane]
                    dcg.pop(gi - a_hi - 1, None)
                    ug = src[8 * gi:8 * gi + 8, lane]
                    for k, a in by_shift[r]:
                        if 0 <= gi - a < n_grp:
                            prod = dcg[gi - a] * ug
                            sums[k] = prod if sums[k] is None else sums[k] + prod
                for k, _ in by_shift[r]:
                    dww_acc[k, :, lane] += sums[k]

        for rows in blocks:
            du = du_scr[rows, :]
            sgb = _sigmoid(b_ref[rows, :].astype(F32))
            dp_ref[rows, 0:CW] = (du * sgb).astype(BF16)
            dp_ref[rows, CW:2 * CW] = (du * a_ref[rows, :].astype(F32) * sgb * (1.0 - sgb)).astype(BF16)
        dc_ext[tm:tm + HALO, :] = dc_ext[0:HALO, :]

        @pl.when(i == NTL - 1)
        def _():
            for k in range(CONV_K):
                dww_ref[k:k + 1, :] = jnp.sum(dww_acc[k], axis=0, keepdims=True)
            dpw_ref[...] = dpw_acc[...].astype(BF16)

    row = lambda i: (0, 0)
    return pl.pallas_call(
        body, name="conv_bwd",
        grid=(NTL,),
        in_specs=[pl.BlockSpec((tm, CW), lambda i: (phys(i), cb)),
                  pl.BlockSpec((tm, CW), lambda i: (phys(i), cb + 1)),
                  pl.BlockSpec((tm, CW), lambda i: (phys(i), cb + 2)),
                  pl.BlockSpec((HALO, CW), lambda i: (halo(i), cb)),
                  pl.BlockSpec((HALO, CW), lambda i: (halo(i), cb + 1)),
                  pl.BlockSpec((tm, CW), lambda i: (phys(i), 0)),
                  pl.BlockSpec((tm, CW), lambda i: (phys(i), 1)),
                  pl.BlockSpec((HALO, CW), row),
                  pl.BlockSpec((1, CW), row), pl.BlockSpec((1, CW), row),
                  pl.BlockSpec((CW, CW), row),
                  pl.BlockSpec((1, CW), row)],
        out_specs=[pl.BlockSpec((tm, 3 * CW), lambda i: (phys(i), 0)),
                   pl.BlockSpec((CW, CW), row),
                   pl.BlockSpec((HALO, CW), row),
                   pl.BlockSpec((8, CW), row)],
        out_shape=[jax.ShapeDtypeStruct((R, 3 * CW), BF16), jax.ShapeDtypeStruct((CW, CW), BF16),
                   jax.ShapeDtypeStruct((HALO, CW), F32), jax.ShapeDtypeStruct((8, CW), F32)],
        scratch_shapes=[pltpu.VMEM((HALO + tm, CW), F32), pltpu.VMEM((7, tm + HALO - 8, CW), F32),
                        pltpu.VMEM((tm + HALO, CW), F32), pltpu.VMEM((7, tm + HALO - 8, CW), F32),
                        pltpu.VMEM((tm, CW), F32), pltpu.VMEM((CONV_K, 8, CW), F32), pltpu.VMEM((CW, CW), F32),
                        pltpu.VMEM((tm, CW), F32), pltpu.VMEM((tm, 1), F32), pltpu.VMEM((tm, CW), F32),
                        pltpu.VMEM((tm, CW), F32), pltpu.VMEM((tm, CW), F32), pltpu.VMEM((tm, CW), BF16),
                        pltpu.VMEM((tm, CW), BF16)],
        compiler_params=_params(("arbitrary",), 60),
    )(proj, proj, proj, proj, proj, conv_out, dy, dw_w, ln_g, ln_b, pw_w, pw_b)


def _retention_bwd(proj, rope, gn_g, states, dy, ex, ex_parts):
    R, E = proj.shape
    RW = gn_g.shape[1]
    H = RET_HEADS
    hd = RW // H
    half = hd // 2
    NC = R // CHUNK
    mask, qd, kd, cd = _decay_tables(H)
    scale = float(hd) ** -0.5
    order = _chunk_order(NC)
    phys = lambda i: order(NC - 1 - i)

    def body(*refs):
        p_ref, cb_ref, sb_ref, ci_ref, si_ref, mask_ref, qd_ref, kd_ref, gn_ref, st_ref, dy_ref = refs[:11]
        ex_ins = refs[11:11 + ex.n]
        dp_ref, dgn_ref = refs[11 + ex.n:13 + ex.n]
        ex_outs = refs[13 + ex.n:13 + 2 * ex.n]
        dstate = refs[13 + 2 * ex.n]
        sems = refs[14 + 2 * ex.n:]

        @pl.when(pl.program_id(0) == 0)
        def _():
            ex.start(ex_ins, ex_outs, sems)
            dstate[...] = jnp.zeros_like(dstate)
            dgn_ref[...] = jnp.zeros_like(dgn_ref)

        cs, sn = _rope_chunk(cb_ref, sb_ref, ci_ref, si_ref)
        hs = range(H)
        col = lambda j, h: slice(j * RW + h * hd, j * RW + (h + 1) * hd)
        qr = [_rot(p_ref[:, col(0, h)].astype(F32), cs, sn, half) for h in hs]
        kr = [_rot(p_ref[:, col(1, h)].astype(F32), cs, sn, half) * scale for h in hs]
        v = [p_ref[:, col(2, h)] for h in hs]
        qb = [qr[h].astype(BF16) for h in hs]
        kb = [kr[h].astype(BF16) for h in hs]
        qdb = [(qr[h] * qd_ref[h]).astype(BF16) for h in hs]
        kdb = [(kr[h] * kd_ref[h]).astype(BF16) for h in hs]
        s_prev = [st_ref[0, h] for h in hs]
        dst = [dstate[h] for h in hs]
        dstb = [dst[h].astype(BF16) for h in hs]
        sb = [(_dot(qb[h], kb[h], NT) * mask_ref[h]).astype(BF16) for h in hs]
        y_raw = [_dot(sb[h], v[h]) + _dot(qdb[h], s_prev[h]) for h in hs]
        dyrb, dg = [], []
        for h in hs:
            g = p_ref[:, col(3, h)].astype(F32)
            mu = jnp.mean(y_raw[h], axis=-1, keepdims=True)
            yc = y_raw[h] - mu
            rstd = lax.rsqrt(jnp.mean(yc * yc, axis=-1, keepdims=True) + EPS)
            xh = yc * rstd
            gn = gn_ref[:, col(0, h)]
            sg = _sigmoid(g)
            dyh = dy_ref[:, col(0, h)].astype(F32)
            dg.append((dyh * (xh * gn) * _dsilu(g, sg)).astype(BF16))
            dyn = dyh * (g * sg)
            dgn_ref[:, col(0, h)] += jnp.sum(dyn * xh, axis=0, keepdims=True)
            dxh = dyn * gn
            dyr = rstd * (dxh - jnp.mean(dxh, axis=-1, keepdims=True)
                          - xh * jnp.mean(dxh * xh, axis=-1, keepdims=True))
            dyrb.append(dyr.astype(BF16))
        dsb = [(_dot(dyrb[h], v[h], NT) * mask_ref[h]).astype(BF16) for h in hs]
        dqr = [_dot(dsb[h], kb[h]) + _dot(dyrb[h], s_prev[h], NT) * qd_ref[h] for h in hs]
        dkr = [_dot(dsb[h], qb[h], TN) + _dot(v[h], dstb[h], NT) * kd_ref[h] for h in hs]
        dv = [_dot(sb[h], dyrb[h], TN) + _dot(kdb[h], dstb[h]) for h in hs]
        dst_new = [dst[h] * cd[h] + _dot(qdb[h], dyrb[h], TN) for h in hs]
        for h in hs:
            dstate[h] = dst_new[h]
            dp_ref[:, col(0, h)] = _rot_inv(dqr[h], cs, sn, half).astype(BF16)
            dp_ref[:, col(1, h)] = (_rot_inv(dkr[h], cs, sn, half) * scale).astype(BF16)
            dp_ref[:, col(2, h)] = dv[h].astype(BF16)
            dp_ref[:, col(3, h)] = dg[h]

        @pl.when(pl.program_id(0) == NC - 1)
        def _():
            ex.wait(ex_ins, ex_outs, sems)

    const3 = lambda i: (0, 0, 0)
    outs = pl.pallas_call(
        body, name="retention_bwd",
        grid=(NC,),
        in_specs=[pl.BlockSpec((CHUNK, 4 * RW), lambda i: (phys(i), 0)),
                  pl.BlockSpec((1, 1, half), lambda i: (phys(i), 0, 0)),
                  pl.BlockSpec((1, 1, half), lambda i: (phys(i), 0, 0)),
                  pl.BlockSpec((CHUNK, half), lambda i: (0, 0)),
                  pl.BlockSpec((CHUNK, half), lambda i: (0, 0)),
                  pl.BlockSpec((H, CHUNK, CHUNK), const3),
                  pl.BlockSpec((H, CHUNK, 1), const3),
                  pl.BlockSpec((H, CHUNK, 1), const3),
                  pl.BlockSpec((1, RW), lambda i: (0, 0)),
                  pl.BlockSpec((1, H, hd, hd), lambda i: (phys(i), 0, 0, 0)),
                  pl.BlockSpec((CHUNK, RW), lambda i: (phys(i), 0))] + ex.specs,
        out_specs=[pl.BlockSpec((CHUNK, 4 * RW), lambda i: (phys(i), 0)),
                   pl.BlockSpec((1, RW), lambda i: (0, 0))] + ex.specs,
        out_shape=[jax.ShapeDtypeStruct((R, 4 * RW), BF16), jax.ShapeDtypeStruct((1, RW), F32)] + ex.out_shape,
        scratch_shapes=[pltpu.VMEM((H, hd, hd), F32)] + ex.scratch,
        compiler_params=_params(("arbitrary",), 32),
    )(proj, *rope, jnp.asarray(mask), jnp.asarray(qd), jnp.asarray(kd), gn_g, states, dy, *ex_parts)
    return outs[0], outs[1], outs[2:]


def _dproj_specs(tk, tn, n_ret, tile_axis, col_axis):
    def ret_map(*ids):
        t, j = ids[tile_axis], ids[col_axis]
        return (jnp.where(j < n_ret, t, 0), jnp.minimum(j, n_ret - 1))

    def conv_map(*ids):
        t, j = ids[tile_axis], ids[col_axis]
        return (jnp.where(j >= n_ret, t, 0), jnp.maximum(j - n_ret, 0))

    return pl.BlockSpec((tk, tn), ret_map), pl.BlockSpec((tk, tn), conv_map)


def _w_in_grad(hn, dp_ret, dp_conv):
    R, D = dp_ret.shape[0], hn.shape[1]
    tn = _pick_tile(dp_conv.shape[1] // 3, 1024, 128)
    n_ret, n_conv = dp_ret.shape[1] // tn, dp_conv.shape[1] // tn
    E = dp_ret.shape[1] + dp_conv.shape[1]
    tk = _pick_tile(R, 1024, MXU_DIM)
    n_t = R // tk
    ret_spec, conv_spec = _dproj_specs(tk, tn, n_ret, 1, 0)

    def body(hn_ref, r_ref, c_ref, out_ref, acc):
        j, t = pl.program_id(0), pl.program_id(1)

        for first, ret in ((True, True), (True, False), (False, True), (False, False)):
            @pl.when(((t == 0) == first) & ((j < n_ret) == ret))
            def _():
                part = _dot(hn_ref[...], (r_ref if ret else c_ref)[...], TN)
                if first:
                    acc[...] = part
                else:
                    acc[...] += part

        @pl.when(t == n_t - 1)
        def _():
            out_ref[...] = acc[...].astype(BF16)

    return pl.pallas_call(
        body, name="w_in_grad",
        grid=(n_ret + n_conv, n_t),
        in_specs=[pl.BlockSpec((tk, D), lambda j, t: (t, 0)), ret_spec, conv_spec],
        out_specs=pl.BlockSpec((D, tn), lambda j, t: (0, j)),
        out_shape=jax.ShapeDtypeStruct((D, E), BF16),
        scratch_shapes=[pltpu.VMEM((D, tn), F32)],
        compiler_params=_params(("arbitrary", "arbitrary"), 48),
    )(hn, dp_ret, dp_conv)


def _h_grad(dp_ret, dp_conv, w_in, xs, meta_tile, dh2, ln_g, ex, ex_parts):
    R, D = dh2.shape
    te = CHUNK
    n_x = xs.shape[0] // te
    n_m = meta_tile.shape[0] // te
    tn = _pick_tile(dp_conv.shape[1] // 3, 1024, 128)
    n_ret, n_conv = dp_ret.shape[1] // tn, dp_conv.shape[1] // tn
    n_k = n_ret + n_conv
    tm = _pick_tile(R, 1024, meta_tile.shape[0])
    n_e = tm // te
    n_t = R // tm
    assert n_e <= n_k and (n_x + n_m) * te == R

    def ret_map(t, k):
        return (jnp.where(k < n_ret, jnp.minimum(t, n_t - 1), 0), jnp.minimum(k, n_ret - 1))

    def conv_map(t, k):
        return (jnp.where(k >= n_ret, jnp.minimum(t, n_t - 1), 0), jnp.maximum(k - n_ret, 0))

    def row_block(t, k):
        return jnp.maximum(t - 1, 0) * n_e + jnp.where(t > 0, jnp.minimum(k, n_e - 1), 0)

    def body(*refs):
        r_ref, c_ref, w_hbm, w_ref, x_ref, mt_ref, dh2_ref, g_ref = refs[:8]
        ex_ins = refs[8:8 + ex.n]
        o = 8 + ex.n
        dh_ref, dlg_ref = refs[o:o + 2]
        ex_outs = refs[o + 2:o + 2 + ex.n]
        acc, w_keep, keep_sems = refs[o + 2 + ex.n:o + 5 + ex.n]
        sems = refs[o + 5 + ex.n:]
        t, k = pl.program_id(0), pl.program_id(1)
        cur, old = t % 2, (t + 1) % 2

        def keep(j):
            return pltpu.make_async_copy(w_hbm.at[:, j * tn:(j + 1) * tn], w_keep.at[j], keep_sems.at[j])

        @pl.when((k == 0) & (t == 0))
        def _():
            for j in range(n_ret):
                keep(j).start()
            ex.start(ex_ins, ex_outs, sems)
            dlg_ref[...] = jnp.zeros_like(dlg_ref)

        for j in range(n_ret):
            @pl.when((k == j) & (t == 0))
            def _():
                keep(j).wait()

        @pl.when((k == 0) & (t < n_t))
        def _():
            acc[cur] = _dot(r_ref[...], w_keep[0], NT)

        @pl.when((k > 0) & (k < n_ret) & (t < n_t))
        def _():
            acc[cur] += _dot(r_ref[...], w_keep[k], NT)

        @pl.when((k >= n_ret) & (t < n_t))
        def _():
            acc[cur] += _dot(c_ref[...], w_ref[...], NT)

        @pl.when((k < n_e) & (t > 0))
        def _():
            hv = jnp.where(row_block(t, k) < n_x, x_ref[...], mt_ref[...])
            r = lax.rsqrt(jnp.mean(hv * hv, axis=-1, keepdims=True) + EPS)
            nrm = hv * r
            dhn = acc[old, pl.ds(pl.multiple_of(k * te, te), te), :]
            dlg_ref[...] += jnp.sum(dhn * nrm, axis=0, keepdims=True)
            dn = dhn * g_ref[...]
            dh_ref[...] = dh2_ref[...] + r * (dn - nrm * jnp.mean(dn * nrm, axis=-1, keepdims=True))

        @pl.when((k == n_k - 1) & (t == n_t))
        def _():
            ex.wait(ex_ins, ex_outs, sems)

    row = lambda t, k: (0, 0)
    outs = pl.pallas_call(
        body, name="h_grad",
        grid=(n_t + 1, n_k),
        in_specs=[pl.BlockSpec((tm, tn), ret_map), pl.BlockSpec((tm, tn), conv_map),
                  pl.BlockSpec(memory_space=pl.ANY),
                  pl.BlockSpec((D, tn), lambda t, k: (0, jnp.maximum(k, n_ret))),
                  pl.BlockSpec((te, D), lambda t, k: (jnp.minimum(row_block(t, k), n_x - 1), 0)),
                  pl.BlockSpec((te, D), lambda t, k: (jnp.clip(row_block(t, k) - n_x, 0, n_m - 1), 0)),
                  pl.BlockSpec((te, D), lambda t, k: (row_block(t, k), 0)),
                  pl.BlockSpec((1, D), row)] + ex.specs,
        out_specs=[pl.BlockSpec((te, D), lambda t, k: (row_block(t, k), 0)),
                   pl.BlockSpec((1, D), row)] + ex.specs,
        out_shape=[jax.ShapeDtypeStruct((R, D), F32), jax.ShapeDtypeStruct((1, D), F32)] + ex.out_shape,
        scratch_shapes=[pltpu.VMEM((2, tm, D), F32), pltpu.VMEM((n_ret, D, tn), BF16),
                        pltpu.SemaphoreType.DMA((n_ret,))] + ex.scratch,
        compiler_params=_params(("arbitrary", "arbitrary"), 60),
    )(dp_ret, dp_conv, w_in, w_in, xs, meta_tile, dh2, ln_g, *ex_parts)
    return outs[0], outs[1], outs[2:]


def _adamw(w, g, m, v):
    m = ADAM_B1 * m + (1.0 - ADAM_B1) * g
    v = ADAM_B2 * v + (1.0 - ADAM_B2) * (g * g)
    m_hat = m / (1.0 - ADAM_B1 ** ADAM_STEP)
    v_hat = v / (1.0 - ADAM_B2 ** ADAM_STEP)
    delta = -ADAM_LR * (m_hat / (jnp.sqrt(v_hat) + ADAM_EPS) + ADAM_WD * w)
    return delta, m, v


def _sum_slots(ref):
    g = ref[0].astype(F32)
    for s in range(1, N_DEV):
        g = g + ref[s].astype(F32)
    return g


def _sum_adamw(name, parts, w, m, v, rows_target, ex=None, ex_parts=()):
    rows, cols = w.shape
    tr = _pick_tile(rows, rows_target, 8)
    n_ex = 0 if ex is None else ex.n
    n_steps = rows // tr

    def body(*refs):
        p_ref, w_ref, m_ref, v_ref = refs[:4]
        ex_ins = refs[4:4 + n_ex]
        o = 4 + n_ex
        g_ref, d_ref, nm_ref, nv_ref = refs[o:o + 4]
        ex_outs, sems = refs[o + 4:o + 4 + n_ex], refs[o + 4 + n_ex:]
        if ex is not None:
            @pl.when(pl.program_id(0) == 0)
            def _():
                ex.start(ex_ins, ex_outs, sems)

        g = _sum_slots(p_ref)
        d, nm, nv = _adamw(w_ref[...], g, m_ref[...], v_ref[...])
        g_ref[...] = g
        d_ref[...] = d
        nm_ref[...] = nm
        nv_ref[...] = nv
        if ex is not None:
            @pl.when(pl.program_id(0) == n_steps - 1)
            def _():
                ex.wait(ex_ins, ex_outs, sems)

    tile = pl.BlockSpec((tr, cols), lambda i: (i, 0))
    ex_specs, ex_shape, ex_scratch = ([], [], []) if ex is None else (ex.specs, ex.out_shape, ex.scratch)
    outs = pl.pallas_call(
        body, name=name,
        grid=(n_steps,),
        in_specs=[pl.BlockSpec((N_DEV, tr, cols), lambda i: (0, i, 0)), tile, tile, tile] + ex_specs,
        out_specs=[tile] * 4 + ex_specs,
        out_shape=[jax.ShapeDtypeStruct((rows, cols), F32)] * 4 + ex_shape,
        scratch_shapes=ex_scratch,
        compiler_params=_params(("arbitrary",), 40),
    )(parts, w, m, v, *ex_parts)
    return outs[:4], outs[4:]


def _sum_adamw_small(parts_list, w_list, m_list, v_list, loss_parts):
    n = len(w_list)

    def body(*refs):
        p_refs, w_refs, m_refs, v_refs = refs[:n], refs[n:2 * n], refs[2 * n:3 * n], refs[3 * n:4 * n]
        lp_ref = refs[4 * n]
        outs = refs[4 * n + 1:]
        for a in range(n):
            g = _sum_slots(p_refs[a])
            d, nm, nv = _adamw(w_refs[a][...], g, m_refs[a][...], v_refs[a][...])
            outs[4 * a][...] = g
            outs[4 * a + 1][...] = d
            outs[4 * a + 2][...] = nm
            outs[4 * a + 3][...] = nv
        outs[4 * n][...] = _sum_slots(lp_ref)

    out_shape = []
    for w in w_list:
        out_shape += [jax.ShapeDtypeStruct(w.shape, F32)] * 4
    out_shape.append(jax.ShapeDtypeStruct(loss_parts.shape[1:], F32))
    return pl.pallas_call(body, name="sum_adamw_small", out_shape=out_shape)(
        *parts_list, *w_list, *m_list, *v_list, loss_parts)


def kernel(x, meta_tokens, ln_g, w_in, ret_gn_g, conv_dw_w, conv_dw_b, conv_ln_g, conv_ln_b, conv_pw_w, conv_pw_b, w_out, final_g, loss_target, m_meta_tokens, m_ln_g, m_w_in, m_ret_gn_g, m_conv_dw_w, m_conv_dw_b, m_conv_ln_g, m_conv_ln_b, m_conv_pw_w, m_conv_pw_b, m_w_out, m_final_g, v_meta_tokens, v_ln_g, v_w_in, v_ret_gn_g, v_conv_dw_w, v_conv_dw_b, v_conv_ln_g, v_conv_ln_b, v_conv_pw_w, v_conv_pw_b, v_w_out, v_final_g):
    _, SEQ, D = x.shape
    MIX = w_out.shape[2]
    RW = ret_gn_g.shape[1]
    CW = conv_pw_b.shape[1]
    assert RW == CW and MIX == RW + CW and SEQ % META_TILE == 0 and CONV_K - 1 <= HALO
    R = SEQ + META_TILE
    hd = RW // RET_HEADS
    half = hd // 2
    me = 4 * lax.axis_index("x") + 2 * lax.axis_index("y") + lax.axis_index("c")

    dw_pad = jnp.pad(conv_dw_w[0], ((0, HALO - CONV_K), (0, 0)))

    n_seq_chunks = SEQ // CHUNK
    base = jnp.concatenate([jnp.arange(n_seq_chunks, dtype=F32) * CHUNK + N_META,
                            jnp.zeros((META_TILE // CHUNK - 1,), F32), jnp.full((1,), N_META - CHUNK, F32)])
    inv_freq = ROPE_BASE ** (-jnp.arange(half, dtype=F32) / half)
    ang_base = (base[:, None] * inv_freq[None, :])[:, None, :]
    ang_row = jnp.arange(CHUNK, dtype=F32)[:, None] * inv_freq[None, :]
    rope = (jnp.cos(ang_base), jnp.sin(ang_base), jnp.cos(ang_row), jnp.sin(ang_row))

    xs = x[0]
    target = loss_target[0]
    final_g2 = final_g[None, :]

    small_shards = [meta_tokens, dw_pad]
    hn, meta_tile, (_, dw_g) = _rms_norm(xs, meta_tokens, ln_g, _Exchange(small_shards, [None, None]), small_shards)
    dw_g = jnp.swapaxes(dw_g, 0, 1).reshape(HALO, CW)
    pw_shard, w_out_shard = [conv_pw_w[0].astype(BF16)], [w_out[0].astype(BF16)]
    proj, w_in_g, (pw_g,) = _in_proj_gather(hn, R, w_in[0].astype(BF16), _Exchange(pw_shard, [None]), pw_shard)
    pw_g = pw_g.reshape(CW, CW)
    y, states = _retention_fwd(proj, rope, ret_gn_g, MIX)
    y, conv_out, (w_out_g,) = _conv_fwd(proj, y, dw_g, conv_dw_b, conv_ln_g, conv_ln_b, pw_g, conv_pw_b,
                                        _Exchange(w_out_shard, [None]), w_out_shard)
    w_out_g = w_out_g.reshape(MIX, D)
    dh2, dy, dwo_p, dfg_p, loss_p = _out_proj_loss(xs, meta_tile, y, w_out_g, final_g2, target)

    dp_conv, dpw_p, dww_p, cvec_p = _conv_bwd(proj, conv_out, dy, dw_g, conv_ln_g, conv_ln_b, pw_g, conv_pw_b)
    dp_ret, dgn_p, (r_wo, r_pw) = _retention_bwd(proj, rope, ret_gn_g, states, dy,
                                                 _Exchange([dwo_p, dpw_p], [0, 0]), [dwo_p, dpw_p])
    dwi_p = _w_in_grad(hn, dp_ret, dp_conv)
    dh, dlg_p, (r_wi,) = _h_grad(dp_ret, dp_conv, w_in_g, xs, meta_tile, dh2, ln_g, _Exchange([dwi_p], [1]), [dwi_p])
    grad_x = dh[:SEQ][None]

    def at_row(r, a, b=None):
        v = a if b is None else jnp.concatenate([a, b], axis=1)
        return jnp.pad(v, ((r, 7 - r), (0, D - v.shape[1])))
    vec8 = (at_row(0, dlg_p) + at_row(1, dfg_p)
            + at_row(2, dgn_p, cvec_p[3:4])
            + at_row(3, cvec_p[1:2], cvec_p[2:3])
            + at_row(4, cvec_p[0:1], jnp.broadcast_to(loss_p, (1, CW))))
    small = jnp.concatenate([dh[R - N_META:], vec8,
                             jnp.zeros((SMALL_ROWS - N_META - 8, D), F32)], axis=0)

    (g_wi, d_wi, nm_wi, nv_wi), (r_dww, r_small) = _sum_adamw(
        "sum_adamw_w_in", r_wi, w_in[0], m_w_in[0], v_w_in[0], 256,
        ex=_Exchange([dww_p, small], [1, None]), ex_parts=[dww_p, small])
    (g_wo, d_wo, nm_wo, nv_wo), _ = _sum_adamw("sum_adamw_w_out", r_wo, w_out[0], m_w_out[0], v_w_out[0], 128)
    (g_pw, d_pw, nm_pw, nv_pw), _ = _sum_adamw("sum_adamw_pw", r_pw, conv_pw_w[0], m_conv_pw_w[0], v_conv_pw_w[0], 128)

    dcol = D // N_DEV
    sm = lambda r0, nr, c0, nc: lax.slice(r_small, (0, r0, c0), (N_DEV, r0 + nr, c0 + nc))
    meta_parts = lax.dynamic_slice(r_small, (0, 0, me * dcol), (N_DEV, N_META, dcol))
    small_parts = [meta_parts, sm(16, 1, 0, D), sm(18, 1, 0, RW), r_dww, sm(18, 1, RW, CW),
                   sm(19, 1, 0, CW), sm(19, 1, CW, CW), sm(20, 1, 0, CW), sm(17, 1, 0, D)]
    pad31 = lambda a: jnp.pad(a, ((0, HALO - CONV_K), (0, 0)))
    ws = [meta_tokens, ln_g, ret_gn_g, pad31(conv_dw_w[0]), conv_dw_b, conv_ln_g, conv_ln_b, conv_pw_b, final_g2]
    ms = [m_meta_tokens, m_ln_g, m_ret_gn_g, pad31(m_conv_dw_w[0]), m_conv_dw_b, m_conv_ln_g, m_conv_ln_b,
          m_conv_pw_b, m_final_g[None, :]]
    vs = [v_meta_tokens, v_ln_g, v_ret_gn_g, pad31(v_conv_dw_w[0]), v_conv_dw_b, v_conv_ln_g, v_conv_ln_b,
          v_conv_pw_b, v_final_g[None, :]]
    loss_parts = sm(20, 1, CW, 1)
    outs = _sum_adamw_small(small_parts, ws, ms, vs, loss_parts)
    loss = outs[-1][0, 0]
    quad = [outs[4 * a:4 * a + 4] for a in range(len(ws))]
    (q_meta, q_lng, q_gn, q_dww, q_dwb, q_clg, q_clb, q_pwb, q_fg) = quad
    q_dww = [t[:CONV_K][None] for t in q_dww]
    q_fg = [t[0] for t in q_fg]
    q_wi = [t[None] for t in (g_wi, d_wi, nm_wi, nv_wi)]
    q_wo = [t[None] for t in (g_wo, d_wo, nm_wo, nv_wo)]
    q_pw = [t[None] for t in (g_pw, d_pw, nm_pw, nv_pw)]

    per_w = [q_meta, q_lng, q_wi, q_gn, q_dww, q_dwb, q_clg, q_clb, q_pw, q_pwb, q_wo, q_fg]
    result = [loss, grad_x]
    for which in range(4):
        result += [q[which] for q in per_w]
    return tuple(result)
```
